```python
import math
import jax, jax.numpy as jnp
from jax import lax
import numpy as np

D_MODEL = 2048
BATCH = 16
SEQ = 256
DEPTH = 4
DEC_BATCH = 8
DEC_SEQ = 4096
PAST_LEN = 256

GRID_W = 64
N_MIXERS = 4
N_A = (DEPTH + 3) // 4
N_B = (DEPTH + 2) // 4
N_C = (DEPTH + 1) // 4
N_D = DEPTH // 4

NA_HEADS = 16
NA_HEAD_DIM = 128
NA_WIN_ROWS = 8
NA_WIN_COLS = 16
NA_KEY_COLS = 32

LRU_WIDTH = 2688
LRU_BLOCKS = 16
LRU_BLOCK = LRU_WIDTH // LRU_BLOCKS
LRU_CONV = 4
LRU_C = 8.0

MLA_HEADS = 16
MLA_Q_RANK = 768
MLA_KV_RANK = 512
MLA_NOPE = 128
MLA_ROPE = 64
MLA_QK = MLA_NOPE + MLA_ROPE
MLA_V = 128

SWA_HEADS = 32
SWA_KV_HEADS = 4
SWA_HEAD_DIM = 64
SWA_WINDOW = 128
SWA_BLOCK = 128

D_FF = 5632
FFN_CONV = 3

Q_BLOCK = 128
ROPE_BASE = 10000.0
EPS = 1e-6
NEG = -1e30

kernel_name = 'hybrid_diffusion_trunk_step'


def rms_norm(x, g):
    xf = x.astype(jnp.float32)
    y = xf * lax.rsqrt(jnp.mean(xf * xf, axis=-1, keepdims=True) + EPS)
    return (y * g.astype(jnp.float32)).astype(x.dtype)


def modulation(cond, w_mod, b_mod):
    m = jax.nn.silu(cond) @ w_mod + b_mod
    return jnp.split(m[:, None, :], 6, axis=-1)


def modulate(x, shift, scale):
    return x * (1 + scale) + shift


def dwconv_centred(x, w, b):
    K = w.shape[0]
    S = x.shape[1]
    left = K // 2
    xp = jnp.pad(x, ((0, 0), (left, K - 1 - left), (0, 0)))
    out = b
    for k in range(K):
        out = out + xp[:, k:k + S] * w[k]
    return out


def axial_rope_tables(n_tokens, rot_dim):
    t = jnp.arange(n_tokens)
    row = (t // GRID_W).astype(jnp.float32)
    col = (t % GRID_W).astype(jnp.float32)
    half = rot_dim // 2
    inv = ROPE_BASE ** (-jnp.arange(0, half, 2, dtype=jnp.float32) / half)
    ar = row[:, None] * inv
    ac = col[:, None] * inv
    ang = jnp.concatenate([ar, ar, ac, ac], axis=-1)
    return jnp.cos(ang), jnp.sin(ang)


def rotate_half(z):
    z1, z2 = jnp.split(z, 2, axis=-1)
    return jnp.concatenate([-z2, z1], axis=-1)


def apply_axial_rope(x, cos, sin):
    xf = x.astype(jnp.float32)
    xr, xc = jnp.split(xf, 2, axis=-1)
    rot = jnp.concatenate([rotate_half(xr), rotate_half(xc)], axis=-1)
    return (xf * cos + rot * sin).astype(x.dtype)


def rope_tail(x, cos, sin):
    n = cos.shape[-1]
    return jnp.concatenate([x[..., :-n], apply_axial_rope(x[..., -n:], cos, sin)], axis=-1)


def attn_probs(s, sink=None):
    s = s.astype(jnp.float32)
    m = jnp.max(s, axis=-1, keepdims=True)
    if sink is not None:
        m = jnp.maximum(m, sink)
    e = jnp.exp(s - m)
    den = jnp.sum(e, axis=-1, keepdims=True)
    if sink is not None:
        den = den + jnp.exp(sink - m)
    return e / den


def joint_probs(s_a, s_b, sink=None):
    p = attn_probs(jnp.concatenate([s_a.astype(jnp.float32), s_b.astype(jnp.float32)], axis=-1), sink)
    n = s_a.shape[-1]
    return p[..., :n], p[..., n:]


def dense_attention(q, k, v):
    s = jnp.einsum('bhqd,bhkd->bhqk', q, k).astype(jnp.float32) * (q.shape[-1] ** -0.5)
    p = attn_probs(s)
    return jnp.einsum('bhqk,bhkd->bhqd', p.astype(v.dtype), v)


def merge_heads(o, w_o):
    B, H, S, Dh = o.shape
    return o.transpose(0, 2, 1, 3).reshape(B, S, H * Dh) @ w_o


def natten_qkv(h, w_qkv, g_q, g_k):
    B, S, _ = h.shape
    qkv = (h @ w_qkv).reshape(B, S, 3, NA_HEADS, NA_HEAD_DIM)
    q = rms_norm(qkv[:, :, 0], g_q).transpose(0, 2, 1, 3)
    k = rms_norm(qkv[:, :, 1], g_k).transpose(0, 2, 1, 3)
    v = qkv[:, :, 2].transpose(0, 2, 1, 3)
    return q, k, v


def natten_context(h, w_qkv, g_q, g_k, w_o):
    q, k, v = natten_qkv(h, w_qkv, g_q, g_k)
    return merge_heads(dense_attention(q, k, v), w_o), k, v


def natten_latent(h, k_ctx, v_ctx, w_qkv, g_q, g_k, rpb, w_o):
    B, N, _ = h.shape
    H, Dh = NA_HEADS, NA_HEAD_DIM
    rows = N // GRID_W
    wr = min(NA_WIN_ROWS, rows)
    ncb = GRID_W // NA_WIN_COLS
    scale = Dh ** -0.5
    q, k, v = natten_qkv(h, w_qkv, g_q, g_k)
    qg = q.reshape(B, H, rows, GRID_W, Dh)
    kg = k.reshape(B, H, rows, GRID_W, Dh)
    vg = v.reshape(B, H, rows, GRID_W, Dh)
    qcol = np.arange(GRID_W).reshape(ncb, NA_WIN_COLS)
    kcol = np.clip(qcol[:, :1] - NA_WIN_COLS // 2, 0, GRID_W - NA_KEY_COLS) + np.arange(NA_KEY_COLS)
    cstart = np.clip(qcol - NA_WIN_COLS // 2, 0, GRID_W - NA_WIN_COLS)
    kc = kcol[:, None, :]
    col_mask = (kc >= cstart[:, :, None]) & (kc < cstart[:, :, None] + NA_WIN_COLS)
    dc_idx = np.clip(kc - qcol[:, :, None], 1 - NA_WIN_COLS, NA_WIN_COLS - 1) + NA_WIN_COLS - 1
    mask = jnp.asarray(np.broadcast_to(col_mask[:, :, None, :], (ncb, NA_WIN_COLS, wr, NA_KEY_COLS)).reshape(ncb, NA_WIN_COLS, wr * NA_KEY_COLS))

    def row_block(r):
        rs = jnp.clip(r - wr // 2, 0, rows - wr)

        def gather(z):
            zr = lax.dynamic_slice_in_dim(z, rs, wr, axis=2)[:, :, :, kcol]
            return zr.transpose(0, 1, 3, 2, 4, 5).reshape(B, H, ncb, wr * NA_KEY_COLS, Dh)

        kb, vb = gather(kg), gather(vg)
        qr = lax.dynamic_index_in_dim(qg, r, axis=2, keepdims=False).reshape(B, H, ncb, NA_WIN_COLS, Dh)
        dr_idx = rs + jnp.arange(wr) - r + NA_WIN_ROWS - 1
        bias = rpb[:, dr_idx][:, :, dc_idx].transpose(0, 2, 3, 1, 4).reshape(H, ncb, NA_WIN_COLS, wr * NA_KEY_COLS)
        s_loc = jnp.einsum('bhnqd,bhnkd->bhnqk', qr, kb).astype(jnp.float32) * scale + bias.astype(jnp.float32)
        s_loc = jnp.where(mask, s_loc, NEG)
        s_ctx = jnp.einsum('bhnqd,bhkd->bhnqk', qr, k_ctx).astype(jnp.float32) * scale
        p_loc, p_ctx = joint_probs(s_loc, s_ctx)
        o = (jnp.einsum('bhnqk,bhnkd->bhnqd', p_loc.astype(v.dtype), vb)
             + jnp.einsum('bhnqk,bhkd->bhnqd', p_ctx.astype(v.dtype), v_ctx))
        return o.reshape(B, H, GRID_W, Dh)

    o = lax.map(row_block, jnp.arange(rows))
    o = o.transpose(1, 2, 0, 3, 4).reshape(B, H, N, Dh)
    return merge_heads(o, w_o)


def lru_gates(xc, w_a, b_a, w_i, b_i, lam):
    B, S, C = xc.shape
    xb = xc.reshape(B, S, LRU_BLOCKS, LRU_BLOCK)
    r = jax.nn.sigmoid((jnp.einsum('bsnk,nkj->bsnj', xb, w_a).reshape(B, S, C) + b_a).astype(jnp.float32))
    i = jax.nn.sigmoid((jnp.einsum('bsnk,nkj->bsnj', xb, w_i).reshape(B, S, C) + b_i).astype(jnp.float32))
    log_a = -LRU_C * r * jax.nn.softplus(-lam.astype(jnp.float32))
    a = jnp.exp(log_a)
    bx = jnp.sqrt(-jnp.expm1(2.0 * log_a)) * (i * xc.astype(jnp.float32))
    return a, bx


def lru_scan(a, bx, h0, reverse):
    def step(hc, ab):
        hc = ab[0] * hc + ab[1]
        return hc, hc
    h_last, hs = lax.scan(step, h0, (jnp.swapaxes(a, 0, 1), jnp.swapaxes(bx, 0, 1)), reverse=reverse)
    return jnp.swapaxes(hs, 0, 1), h_last


def rglru_mixer(h, h0, w_in, conv_w, conv_b, w_a, b_a, w_i, b_i, lam, w_out):
    xb, gate = jnp.split(h @ w_in, 2, axis=-1)
    xc = dwconv_centred(xb, conv_w, conv_b)
    h0 = h0.astype(jnp.float32)
    a_f, b_f = lru_gates(xc, w_a[0], b_a[0], w_i[0], b_i[0], lam[0])
    hs_f, hT_f = lru_scan(a_f, b_f, h0[:, 0], False)
    a_b, b_b = lru_gates(xc, w_a[1], b_a[1], w_i[1], b_i[1], lam[1])
    hs_b, hT_b = lru_scan(a_b, b_b, h0[:, 1], True)
    y = (jax.nn.gelu(gate) * (hs_f + hs_b).astype(h.dtype)) @ w_out
    return y, jnp.stack([hT_f, hT_b], axis=1).astype(h.dtype)


def mla_down(h, w_down, g_qa, g_kva):
    d = h @ w_down
    cq = rms_norm(d[..., :MLA_Q_RANK], g_qa)
    ckv = rms_norm(d[..., MLA_Q_RANK:MLA_Q_RANK + MLA_KV_RANK], g_kva)
    k_rope = d[..., MLA_Q_RANK + MLA_KV_RANK:]
    return cq, ckv, k_rope


def mla_queries(cq, w_uq, g_q):
    B, S, _ = cq.shape
    q = (cq @ w_uq).reshape(B, S, MLA_HEADS, MLA_QK)
    return rms_norm(q, g_q).transpose(0, 2, 1, 3)


def mla_keys_values(ckv, k_rope, w_ukv, g_k):
    B, S, _ = ckv.shape
    kv = (ckv @ w_ukv).reshape(B, S, MLA_HEADS, MLA_NOPE + MLA_V)
    kr = jnp.broadcast_to(k_rope[:, :, None, :], (B, S, MLA_HEADS, MLA_ROPE))
    k = rms_norm(jnp.concatenate([kv[..., :MLA_NOPE], kr], axis=-1), g_k).transpose(0, 2, 1, 3)
    v = kv[..., MLA_NOPE:].transpose(0, 2, 1, 3)
    return k, v


def mla_context(h, w_down, g_qa, g_kva, w_uq, w_ukv, g_q, g_k, w_o):
    cq, ckv, kr = mla_down(h, w_down, g_qa, g_kva)
    q = mla_queries(cq, w_uq, g_q)
    k, v = mla_keys_values(ckv, kr, w_ukv, g_k)
    return merge_heads(dense_attention(q, k, v), w_o), ckv, kr


def blocked_joint_attention(q, k, v, k_ctx, v_ctx):
    B, H, N, Dq = q.shape
    nb = N // Q_BLOCK
    scale = Dq ** -0.5
    qb = q.reshape(B, H, nb, Q_BLOCK, Dq).transpose(2, 0, 1, 3, 4)

    def block(qi):
        s_lat = jnp.einsum('bhqd,bhkd->bhqk', qi, k).astype(jnp.float32) * scale
        s_ctx = jnp.einsum('bhqd,bhkd->bhqk', qi, k_ctx).astype(jnp.float32) * scale
        p_lat, p_ctx = joint_probs(s_lat, s_ctx)
        return (jnp.einsum('bhqk,bhkd->bhqd', p_lat.astype(v.dtype), v)
                + jnp.einsum('bhqk,bhkd->bhqd', p_ctx.astype(v.dtype), v_ctx))

    o = lax.map(block, qb)
    return o.transpose(1, 2, 0, 3, 4).reshape(B, H, N, v.shape[-1])


def mla_latent(h, ckv_ctx, kr_ctx, w_down, g_qa, g_kva, w_uq, w_ukv, g_q, g_k, w_o):
    N = h.shape[1]
    cos, sin = axial_rope_tables(N, MLA_ROPE)
    cq, ckv, kr = mla_down(h, w_down, g_qa, g_kva)
    q = rope_tail(mla_queries(cq, w_uq, g_q), cos, sin)
    k, v = mla_keys_values(ckv, kr, w_ukv, g_k)
    k = rope_tail(k, cos, sin)
    kc, vc = mla_keys_values(ckv_ctx, kr_ctx, w_ukv, g_k)
    return merge_heads(blocked_joint_attention(q, k, v, kc, vc), w_o)


def swa_qkv(h, w_qkv, g_q, g_k, cos=None, sin=None):
    B, S, _ = h.shape
    G = SWA_HEADS // SWA_KV_HEADS
    qkv = (h @ w_qkv).reshape(B, S, SWA_HEADS + 2 * SWA_KV_HEADS, SWA_HEAD_DIM)
    q = rms_norm(qkv[:, :, :SWA_HEADS], g_q).reshape(B, S, SWA_KV_HEADS, G, SWA_HEAD_DIM).transpose(0, 2, 3, 1, 4)
    k = rms_norm(qkv[:, :, SWA_HEADS:SWA_HEADS + SWA_KV_HEADS], g_k).transpose(0, 2, 1, 3)
    v = qkv[:, :, SWA_HEADS + SWA_KV_HEADS:].transpose(0, 2, 1, 3)
    if cos is not None:
        q = apply_axial_rope(q, cos, sin)
        k = apply_axial_rope(k, cos, sin)
    return q, k, v


def swa_merge(o, w_o):
    B, KVH, G, S, Dh = o.shape
    return o.transpose(0, 3, 1, 2, 4).reshape(B, S, KVH * G * Dh) @ w_o


def swa_context(h, w_qkv, g_q, g_k, sinks, w_o):
    q, k, v = swa_qkv(h, w_qkv, g_q, g_k)
    sink = sinks.astype(jnp.float32).reshape(SWA_KV_HEADS, -1, 1, 1)
    s = jnp.einsum('bkgqd,bksd->bkgqs', q, k).astype(jnp.float32) * (SWA_HEAD_DIM ** -0.5)
    p = attn_probs(s, sink)
    o = jnp.einsum('bkgqs,bksd->bkgqd', p.astype(v.dtype), v)
    return swa_merge(o, w_o), k, v


def swa_latent(h, k_ctx, v_ctx, w_qkv, g_q, g_k, sinks, w_o):
    B, N, _ = h.shape
    G = SWA_HEADS // SWA_KV_HEADS
    nb = N // SWA_BLOCK
    scale = SWA_HEAD_DIM ** -0.5
    cos, sin = axial_rope_tables(N, SWA_HEAD_DIM)
    q, k, v = swa_qkv(h, w_qkv, g_q, g_k, cos, sin)
    sink = sinks.astype(jnp.float32).reshape(SWA_KV_HEADS, G, 1, 1)
    pad = ((0, 0), (0, 0), (SWA_BLOCK, SWA_BLOCK), (0, 0))
    kp, vp = jnp.pad(k, pad), jnp.pad(v, pad)
    qb = q.reshape(B, SWA_KV_HEADS, G, nb, SWA_BLOCK, SWA_HEAD_DIM).transpose(3, 0, 1, 2, 4, 5)

    def block(args):
        qi, b = args
        kb = lax.dynamic_slice_in_dim(kp, b * SWA_BLOCK, 3 * SWA_BLOCK, axis=2)
        vb = lax.dynamic_slice_in_dim(vp, b * SWA_BLOCK, 3 * SWA_BLOCK, axis=2)
        qpos = b * SWA_BLOCK + jnp.arange(SWA_BLOCK)
        kpos = (b - 1) * SWA_BLOCK + jnp.arange(3 * SWA_BLOCK)
        mask = (jnp.abs(qpos[:, None] - kpos[None, :]) <= SWA_WINDOW) & (kpos[None, :] >= 0) & (kpos[None, :] < N)
        s_loc = jnp.where(mask, jnp.einsum('bkgqd,bksd->bkgqs', qi, kb).astype(jnp.float32) * scale, NEG)
        s_ctx = jnp.einsum('bkgqd,bksd->bkgqs', qi, k_ctx).astype(jnp.float32) * scale
        p_loc, p_ctx = joint_probs(s_loc, s_ctx, sink)
        return (jnp.einsum('bkgqs,bksd->bkgqd', p_loc.astype(v.dtype), vb)
                + jnp.einsum('bkgqs,bksd->bkgqd', p_ctx.astype(v.dtype), v_ctx))

    o = lax.map(block, (qb, jnp.arange(nb)))
    o = o.transpose(1, 2, 3, 0, 4, 5).reshape(B, SWA_KV_HEADS, G, N, SWA_HEAD_DIM)
    return swa_merge(o, w_o)


def conv_ffn(h, w_in, conv_w, conv_b, w_out):
    a, b = jnp.split(h @ w_in, 2, axis=-1)
    a = dwconv_centred(a, conv_w, conv_b)
    return (jax.nn.silu(a) * b) @ w_out


def setup_inputs(seed: int = 0) -> dict:
    key = jax.random.key(seed)
    ks = iter(jax.random.split(key, 64))

    def nrm(shape, scale=1.0):
        return jax.random.normal(next(ks), shape, jnp.float32) * scale

    def gain(shape):
        return 1.0 + nrm(shape, 0.02)

    D = D_MODEL
    u = jax.random.uniform(next(ks), (N_B, 2, LRU_WIDTH), jnp.float32, minval=0.9, maxval=0.999)
    a0 = u ** (1.0 / LRU_C)
    lam = jnp.log(a0) - jnp.log1p(-a0)
    return {
        'x_prompt': nrm((BATCH, SEQ, D)),
        'x_sample': nrm((DEC_BATCH, DEC_SEQ, D)),
        'cache_nat_k': nrm((DEC_BATCH, N_A, NA_HEADS, PAST_LEN, NA_HEAD_DIM)),
        'cache_nat_v': nrm((DEC_BATCH, N_A, NA_HEADS, PAST_LEN, NA_HEAD_DIM)),
        'state_lru': nrm((DEC_BATCH, N_B, 2, LRU_WIDTH), 0.5),
        'cache_mla_ckv': nrm((DEC_BATCH, N_C, PAST_LEN, MLA_KV_RANK)),
        'cache_mla_krope': nrm((DEC_BATCH, N_C, PAST_LEN, MLA_ROPE)),
        'cache_swa_k': nrm((DEC_BATCH, N_D, SWA_KV_HEADS, PAST_LEN, SWA_HEAD_DIM)),
        'cache_swa_v': nrm((DEC_BATCH, N_D, SWA_KV_HEADS, PAST_LEN, SWA_HEAD_DIM)),
        'c': nrm((DEC_BATCH, D)),
        'c_ctx': nrm((D,)),
        'norm_mix': gain((DEPTH, D)),
        'norm_ffn': gain((DEPTH, D)),
        'w_mod': nrm((DEPTH, D, 6 * D), 0.5 * D ** -0.5),
        'b_mod': nrm((DEPTH, 6 * D), 0.02),
        'ffn_w_in': nrm((DEPTH, D, 2 * D_FF), D ** -0.5),
        'ffn_conv_w': nrm((DEPTH, FFN_CONV, D_FF), FFN_CONV ** -0.5),
        'ffn_conv_b': nrm((DEPTH, D_FF), 0.02),
        'ffn_w_out': nrm((DEPTH, D_FF, D), D_FF ** -0.5),
        'nat_w_qkv': nrm((N_A, D, 3 * NA_HEADS * NA_HEAD_DIM), D ** -0.5),
        'nat_q_norm': gain((N_A, NA_HEAD_DIM)),
        'nat_k_norm': gain((N_A, NA_HEAD_DIM)),
        'nat_rpb': nrm((N_A, NA_HEADS, 2 * NA_WIN_ROWS - 1, 2 * NA_WIN_COLS - 1), 0.1),
        'nat_w_o': nrm((N_A, NA_HEADS * NA_HEAD_DIM, D), (NA_HEADS * NA_HEAD_DIM) ** -0.5),
        'lru_w_in': nrm((N_B, D, 2 * LRU_WIDTH), D ** -0.5),
        'lru_conv_w': nrm((N_B, LRU_CONV, LRU_WIDTH), LRU_CONV ** -0.5),
        'lru_conv_b': nrm((N_B, LRU_WIDTH), 0.02),
        'lru_w_a': nrm((N_B, 2, LRU_BLOCKS, LRU_BLOCK, LRU_BLOCK), LRU_BLOCK ** -0.5),
        'lru_b_a': nrm((N_B, 2, LRU_WIDTH), 0.02),
        'lru_w_i': nrm((N_B, 2, LRU_BLOCKS, LRU_BLOCK, LRU_BLOCK), LRU_BLOCK ** -0.5),
        'lru_b_i': nrm((N_B, 2, LRU_WIDTH), 0.02),
        'lru_lambda': lam,
        'lru_w_out': nrm((N_B, LRU_WIDTH, D), LRU_WIDTH ** -0.5),
        'mla_w_down': nrm((N_C, D, MLA_Q_RANK + MLA_KV_RANK + MLA_ROPE), D ** -0.5),
        'mla_q_a_norm': gain((N_C, MLA_Q_RANK)),
        'mla_kv_a_norm': gain((N_C, MLA_KV_RANK)),
        'mla_w_uq': nrm((N_C, MLA_Q_RANK, MLA_HEADS * MLA_QK), MLA_Q_RANK ** -0.5),
        'mla_w_ukv': nrm((N_C, MLA_KV_RANK, MLA_HEADS * (MLA_NOPE + MLA_V)), MLA_KV_RANK ** -0.5),
        'mla_q_norm': gain((N_C, MLA_QK)),
        'mla_k_norm': gain((N_C, MLA_QK)),
        'mla_w_o': nrm((N_C, MLA_HEADS * MLA_V, D), (MLA_HEADS * MLA_V) ** -0.5),
        'swa_w_qkv': nrm((N_D, D, (SWA_HEADS + 2 * SWA_KV_HEADS) * SWA_HEAD_DIM), D ** -0.5),
        'swa_q_norm': gain((N_D, SWA_HEAD_DIM)),
        'swa_k_norm': gain((N_D, SWA_HEAD_DIM)),
        'swa_sinks': nrm((N_D, SWA_HEADS)),
        'swa_w_o': nrm((N_D, SWA_HEADS * SWA_HEAD_DIM, D), (SWA_HEADS * SWA_HEAD_DIM) ** -0.5),
    }


def reference(x_prompt, x_sample, cache_nat_k, cache_nat_v, state_lru, cache_mla_ckv, cache_mla_krope,
              cache_swa_k, cache_swa_v, c, c_ctx, norm_mix, norm_ffn, w_mod, b_mod, ffn_w_in, ffn_conv_w,
              ffn_conv_b, ffn_w_out, nat_w_qkv, nat_q_norm, nat_k_norm, nat_rpb, nat_w_o, lru_w_in, lru_conv_w,
              lru_conv_b, lru_w_a, lru_b_a, lru_w_i, lru_b_i, lru_lambda, lru_w_out, mla_w_down, mla_q_a_norm,
              mla_kv_a_norm, mla_w_uq, mla_w_ukv, mla_q_norm, mla_k_norm, mla_w_o, swa_w_qkv, swa_q_norm,
              swa_k_norm, swa_sinks, swa_w_o):
    xp, xs = x_prompt, x_sample
    nat_k_l, nat_v_l, lru_l, ckv_l, krope_l, swa_k_l, swa_v_l = [], [], [], [], [], [], []
    for l in range(DEPTH):
        kind, j = l % N_MIXERS, l // N_MIXERS
        mp = modulation(c_ctx[None, :], w_mod[l], b_mod[l])
        ms = modulation(c, w_mod[l], b_mod[l])
        hp = modulate(rms_norm(xp, norm_mix[l]), mp[0], mp[1])
        hs = modulate(rms_norm(xs, norm_mix[l]), ms[0], ms[1])
        if kind == 0:
            yp, kc, vc = natten_context(hp, nat_w_qkv[j], nat_q_norm[j], nat_k_norm[j], nat_w_o[j])
            ys = natten_latent(hs, cache_nat_k[:, j], cache_nat_v[:, j], nat_w_qkv[j], nat_q_norm[j],
                               nat_k_norm[j], nat_rpb[j], nat_w_o[j])
            nat_k_l.append(kc)
            nat_v_l.append(vc)
        elif kind == 1:
            lru_args = (lru_w_in[j], lru_conv_w[j], lru_conv_b[j], lru_w_a[j], lru_b_a[j], lru_w_i[j],
                        lru_b_i[j], lru_lambda[j], lru_w_out[j])
            h0 = jnp.zeros((hp.shape[0], 2, LRU_WIDTH), jnp.float32)
            yp, st = rglru_mixer(hp, h0, *lru_args)
            ys, _ = rglru_mixer(hs, state_lru[:, j], *lru_args)
            lru_l.append(st)
        elif kind == 2:
            yp, ckv, kr = mla_context(hp, mla_w_down[j], mla_q_a_norm[j], mla_kv_a_norm[j], mla_w_uq[j],
                                      mla_w_ukv[j], mla_q_norm[j], mla_k_norm[j], mla_w_o[j])
            ys = mla_latent(hs, cache_mla_ckv[:, j], cache_mla_krope[:, j], mla_w_down[j], mla_q_a_norm[j],
                            mla_kv_a_norm[j], mla_w_uq[j], mla_w_ukv[j], mla_q_norm[j], mla_k_norm[j], mla_w_o[j])
            ckv_l.append(ckv)
            krope_l.append(kr)
        else:
            yp, kc, vc = swa_context(hp, swa_w_qkv[j], swa_q_norm[j], swa_k_norm[j], swa_sinks[j], swa_w_o[j])
            ys = swa_latent(hs, cache_swa_k[:, j], cache_swa_v[:, j], swa_w_qkv[j], swa_q_norm[j],
                            swa_k_norm[j], swa_sinks[j], swa_w_o[j])
            swa_k_l.append(kc)
            swa_v_l.append(vc)
        xp = xp + mp[2] * yp
        xs = xs + ms[2] * ys
        hp = modulate(rms_norm(xp, norm_ffn[l]), mp[3], mp[4])
        hs = modulate(rms_norm(xs, norm_ffn[l]), ms[3], ms[4])
        xp = xp + mp[5] * conv_ffn(hp, ffn_w_in[l], ffn_conv_w[l], ffn_conv_b[l], ffn_w_out[l])
        xs = xs + ms[5] * conv_ffn(hs, ffn_w_in[l], ffn_conv_w[l], ffn_conv_b[l], ffn_w_out[l])
    new_nat_k = jnp.stack(nat_k_l, axis=1)
    new_nat_v = jnp.stack(nat_v_l, axis=1)
    new_lru = jnp.stack(lru_l, axis=1)
    new_ckv = jnp.stack(ckv_l, axis=1)
    new_krope = jnp.stack(krope_l, axis=1)
    new_swa_k = jnp.stack(swa_k_l, axis=1)
    new_swa_v = jnp.stack(swa_v_l, axis=1)
    return (xp, xs, new_nat_k, new_nat_v, new_lru, new_ckv, new_krope, new_swa_k, new_swa_v)
```

```python
import functools
import math

import numpy as np
import jax
import jax.numpy as jnp
from jax import lax
from jax.experimental import pallas as pl
from jax.experimental.pallas import tpu as pltpu

F32 = jnp.float32
BF16 = jnp.bfloat16

GRID_W = 64
NA_WIN_ROWS = 8
NA_WIN_COLS = 16
NA_Q_ROWS = 8
NA_K_ROWS = 16
LRU_BLOCKS = 16
LRU_C = 8.0
SWA_WINDOW = 128
SWA_BLOCK = 128
ROPE_BASE = 10000.0
EPS = 1e-6
NEG = -1e30
LANE = 128
HALO = 16
MIB = 1024 * 1024


def _cparams(n_axes, vmem_mib):
    return pltpu.CompilerParams(dimension_semantics=("arbitrary",) * n_axes,
                                vmem_limit_bytes=int(vmem_mib * MIB))


def _largest_divisor(n, candidates):
    for c in candidates:
        if n % c == 0:
            return c
    return n


class _Layout:
    def __init__(self, ctx_rows, ctx_seq, lat_rows, lat_seq):
        self.ctx_rows, self.ctx_seq, self.lat_rows, self.lat_seq = ctx_rows, ctx_seq, lat_rows, lat_seq
        self.rows = ctx_rows + lat_rows
        self.bm = _largest_divisor(math.gcd(ctx_rows, lat_seq), (1024, 512, 256, 128, 64, 32, 16))

    def mod_index(self, row0):
        return jnp.where(row0 < self.ctx_rows, 0, 1 + jnp.maximum(row0 - self.ctx_rows, 0) // self.lat_seq)


def _norm_mod(x, g, shift, scale):
    ms = jnp.mean(x * x, axis=-1, keepdims=True)
    y = (x * lax.rsqrt(ms + EPS)) * g
    return y * (1.0 + scale) + shift


def _modulation_kernel(c_ref, w_ref, b_ref, o_ref):
    c = c_ref[...]
    sc = (c * jax.nn.sigmoid(c)).astype(BF16)
    o_ref[...] = jnp.dot(sc, w_ref[...].astype(BF16), preferred_element_type=F32) + b_ref[...]


def _modulation(cond, w_mod, b_mod):
    depth, d, n = w_mod.shape
    rows = cond.shape[0]
    bn = _largest_divisor(n, (512, 256, 128))
    return pl.pallas_call(
        _modulation_kernel,
        grid=(depth, n // bn),
        in_specs=[pl.BlockSpec((rows, d), lambda l, j: (0, 0)),
                  pl.BlockSpec((None, d, bn), lambda l, j: (l, 0, j)),
                  pl.BlockSpec((None, 1, bn), lambda l, j: (l, 0, j))],
        out_specs=pl.BlockSpec((None, rows, bn), lambda l, j: (l, 0, j)),
        out_shape=jax.ShapeDtypeStruct((depth, rows, n), F32),
        compiler_params=_cparams(2, 32),
        name="modulation",
    )(cond, w_mod, b_mod.reshape(depth, 1, n))


def _linear_kernel(*refs, has_mod, has_res, row_chunk):
    it = iter(refs)
    x_ref = next(it)
    if has_mod:
        g_ref, sh_ref, sc_ref = next(it), next(it), next(it)
    w_ref = next(it)
    if has_res:
        res_ref, gate_ref = next(it), next(it)
    o_ref = next(it)
    xs_ref = next(it)
    bm = x_ref.shape[0]

    @pl.when(pl.program_id(1) == 0)
    def _():
        def chunk(r, carry):
            rows = pl.ds(pl.multiple_of(r * row_chunk, row_chunk), row_chunk)
            x = x_ref[rows, :]
            if has_mod:
                x = _norm_mod(x, g_ref[...], sh_ref[...], sc_ref[...])
            xs_ref[rows, :] = x.astype(BF16)
            return carry
        lax.fori_loop(0, bm // row_chunk, chunk, 0)

    acc = jnp.dot(xs_ref[...], w_ref[...], preferred_element_type=F32)
    if has_res:
        acc = res_ref[...] + gate_ref[...] * acc
    o_ref[...] = acc


def _linear(x, w, *, lay=None, mod=None, res=None, gate=None, bm=None, name="linear"):
    m, k = x.shape
    n = w.shape[1]
    if bm is None:
        bm = lay.bm if lay is not None else _largest_divisor(m, (1024, 512, 256, 128, 64, 32, 16, 8))
    bn = _largest_divisor(n, (512, 256, 128))
    row_chunk = min(bm, 128)
    has_mod, has_res = mod is not None, res is not None
    mod_idx = (lambda i: lay.mod_index(i * bm)) if lay is not None else None

    in_specs = [pl.BlockSpec((bm, k), lambda i, j: (i, 0))]
    args = [x]
    if has_mod:
        in_specs += [pl.BlockSpec((1, k), lambda i, j: (0, 0)),
                     pl.BlockSpec((None, 1, k), lambda i, j: (mod_idx(i), 0, 0)),
                     pl.BlockSpec((None, 1, k), lambda i, j: (mod_idx(i), 0, 0))]
        args += list(mod)
    in_specs.append(pl.BlockSpec((k, bn), lambda i, j: (0, j)))
    args.append(w)
    if has_res:
        in_specs += [pl.BlockSpec((bm, bn), lambda i, j: (i, j)),
                     pl.BlockSpec((None, 1, bn), lambda i, j: (mod_idx(i), 0, j))]
        args += [res, gate]
    vmem = (2 * bm * k * 4 + bm * k * 2 + 2 * k * bn * 2 + (4 if has_res else 2) * bm * bn * 4) / MIB + 8
    return pl.pallas_call(
        functools.partial(_linear_kernel, has_mod=has_mod, has_res=has_res, row_chunk=row_chunk),
        grid=(m // bm, n // bn),
        in_specs=in_specs,
        out_specs=pl.BlockSpec((bm, bn), lambda i, j: (i, j)),
        out_shape=jax.ShapeDtypeStruct((m, n), F32),
        scratch_shapes=[pltpu.VMEM((bm, k), BF16)],
        compiler_params=_cparams(2, vmem),
        name=name,
    )(*args)


def _ffn_kernel(xp_ref, x_ref, xn_ref, g_ref, sh_ref, sc_ref, gate_ref, wa_ref, wb_ref, cw_ref, cb_ref,
                wo_ref, o_ref, h_ref, *, bm, ctx_rows, ctx_seq, lat_seq, row_chunk):
    i = pl.program_id(0)
    c = pl.program_id(1)
    n_chunks = pl.num_programs(1)

    @pl.when(c == 0)
    def _():
        g, sh, sc = g_ref[...], sh_ref[...], sc_ref[...]
        h_ref[0:HALO, :] = _norm_mod(xp_ref[...], g, sh, sc).astype(BF16)
        h_ref[HALO + bm:, :] = _norm_mod(xn_ref[...], g, sh, sc).astype(BF16)

        def chunk(r, carry):
            src = pl.ds(pl.multiple_of(r * row_chunk, row_chunk), row_chunk)
            dst = pl.ds(pl.multiple_of(HALO + r * row_chunk, HALO), row_chunk)
            h_ref[dst, :] = _norm_mod(x_ref[src, :], g, sh, sc).astype(BF16)
            return carry
        lax.fori_loop(0, bm // row_chunk, chunk, 0)

    ua = jnp.dot(h_ref[...], wa_ref[...], preferred_element_type=F32)
    ub = jnp.dot(h_ref[HALO:HALO + bm, :], wb_ref[...], preferred_element_type=F32)
    n_all = bm + 2 * HALO
    u_prev = pltpu.roll(ua, 1, 0)[HALO:HALO + bm]
    u_next = pltpu.roll(ua, n_all - 1, 0)[HALO:HALO + bm]
    u_mid = ua[HALO:HALO + bm]
    row = i * bm + lax.broadcasted_iota(jnp.int32, (bm, 1), 0)
    seq = jnp.where(i * bm < ctx_rows, ctx_seq, lat_seq)
    pos = jnp.bitwise_and(row - jnp.where(i * bm < ctx_rows, 0, ctx_rows), seq - 1)
    u_prev = jnp.where(pos == 0, 0.0, u_prev)
    u_next = jnp.where(pos == seq - 1, 0.0, u_next)
    cw = cw_ref[...]
    a = cb_ref[...] + u_prev * cw[0:1] + u_mid * cw[1:2] + u_next * cw[2:3]
    gated = ((a * jax.nn.sigmoid(a)) * ub).astype(BF16)
    contrib = jnp.dot(gated, wo_ref[...], preferred_element_type=F32)

    @pl.when(c == 0)
    def _():
        o_ref[...] = contrib

    @pl.when(c > 0)
    def _():
        o_ref[...] += contrib

    @pl.when(c == n_chunks - 1)
    def _():
        o_ref[...] = x_ref[...] + gate_ref[...] * o_ref[...]


def _conv_ffn(x, lay, g, shift, scale, gate, w_in, conv_w, conv_b, w_out):
    m, d = x.shape
    d_ff = w_out.shape[0]
    bm = min(lay.bm, 512)
    ck = _largest_divisor(d_ff, (512, 256, 128))
    n_chunks = d_ff // ck
    n_halo_blocks = m // HALO
    assert lay.ctx_seq & (lay.ctx_seq - 1) == 0 and lay.lat_seq & (lay.lat_seq - 1) == 0
    mod_idx = lambda i: lay.mod_index(i * bm)
    kern = functools.partial(_ffn_kernel, bm=bm, ctx_rows=lay.ctx_rows, ctx_seq=lay.ctx_seq,
                             lat_seq=lay.lat_seq, row_chunk=min(bm, 128))
    vmem = (4 * bm * d * 4 + (bm + 2 * HALO) * d * 2 + 6 * d * ck * 2 + 6 * (bm + 2 * HALO) * ck * 4) / MIB + 10
    return pl.pallas_call(
        kern,
        grid=(m // bm, n_chunks),
        in_specs=[
            pl.BlockSpec((HALO, d), lambda i, c: (jnp.maximum(i * (bm // HALO) - 1, 0), 0)),
            pl.BlockSpec((bm, d), lambda i, c: (i, 0)),
            pl.BlockSpec((HALO, d), lambda i, c: (jnp.minimum((i + 1) * (bm // HALO), n_halo_blocks - 1), 0)),
            pl.BlockSpec((1, d), lambda i, c: (0, 0)),
            pl.BlockSpec((None, 1, d), lambda i, c: (mod_idx(i), 0, 0)),
            pl.BlockSpec((None, 1, d), lambda i, c: (mod_idx(i), 0, 0)),
            pl.BlockSpec((None, 1, d), lambda i, c: (mod_idx(i), 0, 0)),
            pl.BlockSpec((d, ck), lambda i, c: (0, c)),
            pl.BlockSpec((d, ck), lambda i, c: (0, n_chunks + c)),
            pl.BlockSpec((conv_w.shape[0], ck), lambda i, c: (0, c)),
            pl.BlockSpec((1, ck), lambda i, c: (0, c)),
            pl.BlockSpec((ck, d), lambda i, c: (c, 0)),
        ],
        out_specs=pl.BlockSpec((bm, d), lambda i, c: (i, 0)),
        out_shape=jax.ShapeDtypeStruct((m, d), F32),
        scratch_shapes=[pltpu.VMEM((bm + 2 * HALO, d), BF16)],
        compiler_params=_cparams(2, vmem),
        name="conv_ffn",
    )(x, x, x, g, shift, scale, gate, w_in, w_in, conv_w, conv_b.reshape(1, d_ff), w_out)


def _qk(q, k):
    return lax.dot_general(q, k, (((1,), (1,)), ((), ())), preferred_element_type=F32)


def _attend(scores, values, sink=None):
    m = None
    for s in scores:
        mi = jnp.max(s, axis=-1, keepdims=True)
        m = mi if m is None else jnp.maximum(m, mi)
    if sink is not None:
        m = jnp.maximum(m, sink)
    es = [jnp.exp(s - m) for s in scores]
    den = None
    for e in es:
        di = jnp.sum(e, axis=-1, keepdims=True)
        den = di if den is None else den + di
    if sink is not None:
        den = den + jnp.exp(sink - m)
    inv = 1.0 / den
    out = None
    for e, v in zip(es, values):
        oi = jnp.dot((e * inv).astype(BF16), v, preferred_element_type=F32)
        out = oi if out is None else out + oi
    return out


def _ctx_attn_kernel(q_ref, k_ref, v_ref, o_ref, *, scale, heads):
    for h in range(heads):
        q = q_ref[h].astype(BF16)
        k = k_ref[h].astype(BF16)
        v = v_ref[h].astype(BF16)
        o_ref[h] = _attend([_qk(q, k) * scale], [v])


def _ctx_attention(q, k, v):
    g, s, dq = q.shape
    dv = v.shape[-1]
    hb = _largest_divisor(g, (8, 4, 2, 1))
    return pl.pallas_call(
        functools.partial(_ctx_attn_kernel, scale=dq ** -0.5, heads=hb),
        grid=(g // hb,),
        in_specs=[pl.BlockSpec((hb, s, dq), lambda i: (i, 0, 0)),
                  pl.BlockSpec((hb, s, dq), lambda i: (i, 0, 0)),
                  pl.BlockSpec((hb, s, dv), lambda i: (i, 0, 0))],
        out_specs=pl.BlockSpec((hb, s, dv), lambda i: (i, 0, 0)),
        out_shape=jax.ShapeDtypeStruct((g, s, dv), F32),
        compiler_params=_cparams(1, 32),
        name="ctx_attention",
    )(q, k, v)


def _nat_kernel(q_ref, k_ref, v_ref, kc_ref, vc_ref, bias_ref, o_ref, *, scale, n_blocks, key_rows, rows):
    i = pl.program_id(2)
    n_keys = key_rows * GRID_W
    first_row = jnp.clip(i * NA_Q_ROWS - NA_WIN_ROWS // 2, 0, rows - key_rows)
    start = pl.multiple_of(first_row * GRID_W, GRID_W * 4)
    q = q_ref[...].astype(BF16)
    k = k_ref[pl.ds(start, n_keys), :].astype(BF16)
    v = v_ref[pl.ds(start, n_keys), :].astype(BF16)
    s_loc = _qk(q, k) * scale + bias_ref[...]
    s_ctx = _qk(q, kc_ref[...].astype(BF16)) * scale
    o_ref[...] = _attend([s_loc, s_ctx], [v, vc_ref[...].astype(BF16)])


def _nat_bias(rpb, rows):
    n_blocks = rows // NA_Q_ROWS
    key_rows = min(NA_K_ROWS, rows)
    wr = min(NA_WIN_ROWS, rows)
    reps = [0, min(1, n_blocks - 1), n_blocks - 1]
    dr_l, dc_l, mask_l = [], [], []
    for i in reps:
        ks = int(np.clip(i * NA_Q_ROWS - NA_WIN_ROWS // 2, 0, rows - key_rows))
        r = i * NA_Q_ROWS + np.arange(NA_Q_ROWS)
        rs = np.clip(r - wr // 2, 0, rows - wr)
        kr = ks + np.arange(key_rows)
        row_ok = (kr[None, :] >= rs[:, None]) & (kr[None, :] < rs[:, None] + wr)
        dr = np.clip(kr[None, :] - r[:, None] + NA_WIN_ROWS - 1, 0, 2 * NA_WIN_ROWS - 2)
        qc = np.arange(GRID_W)
        kc = np.arange(GRID_W)
        cstart = np.clip(qc - NA_WIN_COLS // 2, 0, GRID_W - NA_WIN_COLS)
        col_ok = (kc[None, :] >= cstart[:, None]) & (kc[None, :] < cstart[:, None] + NA_WIN_COLS)
        dc = np.clip(kc[None, :] - qc[:, None], 1 - NA_WIN_COLS, NA_WIN_COLS - 1) + NA_WIN_COLS - 1
        nq, nk = NA_Q_ROWS * GRID_W, key_rows * GRID_W
        shape = (NA_Q_ROWS, GRID_W, key_rows, GRID_W)
        dr_l.append(np.broadcast_to(dr[:, None, :, None], shape).reshape(nq, nk))
        dc_l.append(np.broadcast_to(dc[None, :, None, :], shape).reshape(nq, nk))
        mask_l.append(np.broadcast_to(row_ok[:, None, :, None] & col_ok[None, :, None, :], shape).reshape(nq, nk))
    dr_i, dc_i, mask = np.stack(dr_l), np.stack(dc_l), np.stack(mask_l)
    return jnp.where(jnp.asarray(mask)[None], rpb[:, dr_i, dc_i].astype(F32), NEG)


def _nat_attention(q, k, v, cache_k, cache_v, j, rpb):
    b, h, n, dh = q.shape
    p = cache_k.shape[3]
    rows = n // GRID_W
    assert rows % NA_Q_ROWS == 0 and rows >= NA_K_ROWS
    n_blocks = rows // NA_Q_ROWS
    key_rows = min(NA_K_ROWS, rows)
    nq, nk = NA_Q_ROWS * GRID_W, key_rows * GRID_W
    bias = _nat_bias(rpb, rows)
    btype = lambda i: jnp.where(i == 0, 0, jnp.where(i == n_blocks - 1, 2, 1))
    kern = functools.partial(_nat_kernel, scale=dh ** -0.5, n_blocks=n_blocks, key_rows=key_rows, rows=rows)
    return pl.pallas_call(
        kern,
        grid=(b, h, n_blocks),
        in_specs=[pl.BlockSpec((None, None, nq, dh), lambda b_, h_, i: (b_, h_, i, 0)),
                  pl.BlockSpec((None, None, n, dh), lambda b_, h_, i: (b_, h_, 0, 0)),
                  pl.BlockSpec((None, None, n, dh), lambda b_, h_, i: (b_, h_, 0, 0)),
                  pl.BlockSpec((None, None, None, p, dh), lambda b_, h_, i: (b_, j, h_, 0, 0)),
                  pl.BlockSpec((None, None, None, p, dh), lambda b_, h_, i: (b_, j, h_, 0, 0)),
                  pl.BlockSpec((None, None, nq, nk), lambda b_, h_, i: (h_, btype(i), 0, 0))],
        out_specs=pl.BlockSpec((None, None, nq, dh), lambda b_, h_, i: (b_, h_, i, 0)),
        out_shape=jax.ShapeDtypeStruct((b, h, n, dh), F32),
        compiler_params=_cparams(3, 40),
        name="nat_attention",
    )(q, k, v, cache_k, cache_v, bias)


def _joint_dense_kernel(q_ref, k_ref, v_ref, kc_ref, vc_ref, o_ref, *, scale):
    q = q_ref[...].astype(BF16)
    s_lat = _qk(q, k_ref[...].astype(BF16)) * scale
    s_ctx = _qk(q, kc_ref[...].astype(BF16)) * scale
    o_ref[...] = _attend([s_lat, s_ctx], [v_ref[...].astype(BF16), vc_ref[...].astype(BF16)])


def _joint_dense_attention(q, k, v, kc, vc):
    b, h, n, dq = q.shape
    dv = v.shape[-1]
    p = kc.shape[2]
    bq = _largest_divisor(n, (256, 128, 64, 32, 16, 8))
    return pl.pallas_call(
        functools.partial(_joint_dense_kernel, scale=dq ** -0.5),
        grid=(b, h, n // bq),
        in_specs=[pl.BlockSpec((None, None, bq, dq), lambda b_, h_, i: (b_, h_, i, 0)),
                  pl.BlockSpec((None, None, n, dq), lambda b_, h_, i: (b_, h_, 0, 0)),
                  pl.BlockSpec((None, None, n, dv), lambda b_, h_, i: (b_, h_, 0, 0)),
                  pl.BlockSpec((None, None, p, dq), lambda b_, h_, i: (b_, h_, 0, 0)),
                  pl.BlockSpec((None, None, p, dv), lambda b_, h_, i: (b_, h_, 0, 0))],
        out_specs=pl.BlockSpec((None, None, bq, dv), lambda b_, h_, i: (b_, h_, i, 0)),
        out_shape=jax.ShapeDtypeStruct((b, h, n, dv), F32),
        compiler_params=_cparams(3, 48),
        name="mla_attention",
    )(q, k, v, kc, vc)


def _sink_column(sinks_ref, kvh, groups, rows_per_group):
    row_group = lax.broadcasted_iota(jnp.int32, (groups * rows_per_group, 1), 0) // rows_per_group
    col = jnp.zeros((groups * rows_per_group, 1), F32)
    for g in range(groups):
        col = jnp.where(row_group == g, sinks_ref[kvh * groups + g], col)
    return col


def _swa_ctx_kernel(sinks_ref, q_ref, k_ref, v_ref, o_ref, *, scale, kv_heads):
    groups, s, dh = q_ref.shape
    kvh = pl.program_id(0) % kv_heads
    q = q_ref[...].reshape(groups * s, dh).astype(BF16)
    sc = _qk(q, k_ref[...].astype(BF16)) * scale
    sink = _sink_column(sinks_ref, kvh, groups, s)
    o_ref[...] = _attend([sc], [v_ref[...].astype(BF16)], sink).reshape(groups, s, dh)


def _swa_ctx_attention(q, k, v, sinks):
    b, kvh, g, s, dh = q.shape
    q = q.reshape(b * kvh, g, s, dh)
    k = k.reshape(b * kvh, s, dh)
    v = v.reshape(b * kvh, s, dh)
    out = pl.pallas_call(
        functools.partial(_swa_ctx_kernel, scale=dh ** -0.5, kv_heads=kvh),
        grid=(b * kvh,),
        in_specs=[pl.BlockSpec(memory_space=pltpu.SMEM),
                  pl.BlockSpec((None, g, s, dh), lambda i: (i, 0, 0, 0)),
                  pl.BlockSpec((None, s, dh), lambda i: (i, 0, 0)),
                  pl.BlockSpec((None, s, dh), lambda i: (i, 0, 0))],
        out_specs=pl.BlockSpec((None, g, s, dh), lambda i: (i, 0, 0, 0)),
        out_shape=jax.ShapeDtypeStruct((b * kvh, g, s, dh), F32),
        compiler_params=_cparams(1, 32),
        name="swa_ctx_attention",
    )(sinks, q, k, v)
    return out.reshape(b, kvh, g, s, dh)


def _swa_lat_kernel(sinks_ref, q_ref, k_ref, v_ref, kc_ref, vc_ref, o_ref, *, scale, n):
    groups, bq, dh = q_ref.shape
    kvh = pl.program_id(1)
    blk = pl.program_id(2)
    n_keys = min(3 * SWA_BLOCK, n)
    start = pl.multiple_of(jnp.clip((blk - 1) * SWA_BLOCK, 0, n - n_keys), SWA_BLOCK)
    q = q_ref[...].reshape(groups * bq, dh).astype(BF16)
    k = k_ref[pl.ds(start, n_keys), :].astype(BF16)
    v = v_ref[pl.ds(start, n_keys), :].astype(BF16)
    qpos = blk * SWA_BLOCK + lax.broadcasted_iota(jnp.int32, (groups * bq, 1), 0) % bq
    kpos = start + lax.broadcasted_iota(jnp.int32, (1, n_keys), 1)
    s_loc = jnp.where(jnp.abs(qpos - kpos) <= SWA_WINDOW, _qk(q, k) * scale, NEG)
    s_ctx = _qk(q, kc_ref[...].astype(BF16)) * scale
    sink = _sink_column(sinks_ref, kvh, groups, bq)
    out = _attend([s_loc, s_ctx], [v, vc_ref[...].astype(BF16)], sink)
    o_ref[...] = out.reshape(groups, bq, dh)


def _swa_lat_attention(q, k, v, cache_k, cache_v, j, sinks):
    b, kvh, g, n, dh = q.shape
    p = cache_k.shape[3]
    assert n % SWA_BLOCK == 0
    return pl.pallas_call(
        functools.partial(_swa_lat_kernel, scale=dh ** -0.5, n=n),
        grid=(b, kvh, n // SWA_BLOCK),
        in_specs=[pl.BlockSpec(memory_space=pltpu.SMEM),
                  pl.BlockSpec((None, None, g, SWA_BLOCK, dh), lambda b_, k_, i: (b_, k_, 0, i, 0)),
                  pl.BlockSpec((None, None, n, dh), lambda b_, k_, i: (b_, k_, 0, 0)),
                  pl.BlockSpec((None, None, n, dh), lambda b_, k_, i: (b_, k_, 0, 0)),
                  pl.BlockSpec((None, None, None, p, dh), lambda b_, k_, i: (b_, j, k_, 0, 0)),
                  pl.BlockSpec((None, None, None, p, dh), lambda b_, k_, i: (b_, j, k_, 0, 0))],
        out_specs=pl.BlockSpec((None, None, g, SWA_BLOCK, dh), lambda b_, k_, i: (b_, k_, 0, i, 0)),
        out_shape=jax.ShapeDtypeStruct((b, kvh, g, n, dh), F32),
        compiler_params=_cparams(3, 32),
        name="swa_attention",
    )(sinks, q, k, v, cache_k, cache_v)


def _band_plan(width, block):
    n_tiles = width // LANE
    lo = [((t * LANE) // block) * block for t in range(n_tiles)]
    hi = [(((t + 1) * LANE - 1) // block + 1) * block for t in range(n_tiles)]
    start = [(l // LANE) * LANE for l in lo]
    kb = max(-(-(h - s) // LANE) * LANE for h, s in zip(hi, start))
    kb = min(kb, width)
    start = [min(s, width - kb) for s in start]
    return start, kb


def _band_weights(w, width, block, start, kb):
    n_tiles = width // LANE
    col = np.arange(n_tiles)[:, None, None] * LANE + np.arange(LANE)[None, None, :]
    row = np.asarray(start)[:, None, None] + np.arange(kb)[None, :, None]
    blk = col // block
    kk = row - blk * block
    jj = col - blk * block
    valid = (kk >= 0) & (kk < block)
    kk = np.clip(kk, 0, block - 1)
    dense = w[np.broadcast_to(blk, kk.shape), kk, np.broadcast_to(jj, kk.shape)]
    return jnp.where(jnp.asarray(valid), dense, 0.0).astype(BF16)


def _expm1_neg(x):
    return -jnp.tanh(0.5 * x) * (jnp.exp(x) + 1.0)


def _lru_gate_kernel(xc_ref, wa_ref, wi_ref, ba_ref, bi_ref, lam_ref, a_ref, bx_ref, *, starts, kb):
    n_tiles = len(starts)
    x = xc_ref[...]
    xb = x.astype(BF16)
    neg_lam = -lam_ref[...]
    softplus = jnp.maximum(neg_lam, 0.0) + jnp.log1p(jnp.exp(-jnp.abs(neg_lam)))
    for d in range(2):
        for t in range(n_tiles):
            lanes = slice(t * LANE, (t + 1) * LANE)
            xw = xb[:, starts[t]:starts[t] + kb]
            r = jax.nn.sigmoid(jnp.dot(xw, wa_ref[d, t], preferred_element_type=F32) + ba_ref[d:d + 1, lanes])
            ig = jax.nn.sigmoid(jnp.dot(xw, wi_ref[d, t], preferred_element_type=F32) + bi_ref[d:d + 1, lanes])
            log_a = -LRU_C * r * softplus[d:d + 1, lanes]
            a_ref[d, :, lanes] = jnp.exp(log_a)
            bx_ref[d, :, lanes] = jnp.sqrt(_expm1_neg(2.0 * log_a)) * (ig * x[:, lanes])


def _lru_gates(xc, w_a, b_a, w_i, b_i, lam):
    m, c = xc.shape
    block = w_a.shape[-1]
    assert c % LANE == 0
    starts, kb = _band_plan(c, block)
    wa = jnp.stack([_band_weights(w_a[d], c, block, starts, kb) for d in range(2)])
    wi = jnp.stack([_band_weights(w_i[d], c, block, starts, kb) for d in range(2)])
    bm = _largest_divisor(m, (256, 128, 64, 32, 16, 8))
    n_tiles = c // LANE
    full = lambda *shape: pl.BlockSpec(shape, lambda i: (0,) * len(shape))
    out = pl.pallas_call(
        functools.partial(_lru_gate_kernel, starts=tuple(starts), kb=kb),
        grid=(m // bm,),
        in_specs=[pl.BlockSpec((bm, c), lambda i: (i, 0)),
                  full(2, n_tiles, kb, LANE), full(2, n_tiles, kb, LANE),
                  full(2, c), full(2, c), full(2, c)],
        out_specs=[pl.BlockSpec((2, bm, c), lambda i: (0, i, 0)),
                   pl.BlockSpec((2, bm, c), lambda i: (0, i, 0))],
        out_shape=[jax.ShapeDtypeStruct((2, m, c), F32), jax.ShapeDtypeStruct((2, m, c), F32)],
        compiler_params=_cparams(1, 48),
        name="lru_gates",
    )(xc, wa, wi, b_a, b_i, lam)
    return out


def _scan_kernel(a_ref, b_ref, h0_ref, hs_ref, ht_ref, carry_ref, *, reverse, unroll):
    sc = a_ref.shape[1]

    @pl.when(pl.program_id(1) == 0)
    def _():
        carry_ref[...] = h0_ref[...]

    def body(t, h):
        tt = (sc - 1 - t) if reverse else t
        h = a_ref[:, pl.ds(tt, 1), :] * h + b_ref[:, pl.ds(tt, 1), :]
        hs_ref[:, pl.ds(tt, 1), :] = h
        return h

    h = lax.fori_loop(0, sc, body, carry_ref[...], unroll=unroll)
    carry_ref[...] = h
    ht_ref[...] = h


def _lru_scan(a, bx, h0, reverse):
    b, s, c = a.shape
    ct = LANE
    sc = _largest_divisor(s, (256, 128, 64, 32, 16, 8))
    ns = s // sc
    smap = (lambda j: ns - 1 - j) if reverse else (lambda j: j)
    hs, ht = pl.pallas_call(
        functools.partial(_scan_kernel, reverse=reverse, unroll=8),
        grid=(c // ct, ns),
        in_specs=[pl.BlockSpec((b, sc, ct), lambda i, j: (0, smap(j), i)),
                  pl.BlockSpec((b, sc, ct), lambda i, j: (0, smap(j), i)),
                  pl.BlockSpec((b, 1, ct), lambda i, j: (0, 0, i))],
        out_specs=[pl.BlockSpec((b, sc, ct), lambda i, j: (0, smap(j), i)),
                   pl.BlockSpec((b, 1, ct), lambda i, j: (0, 0, i))],
        out_shape=[jax.ShapeDtypeStruct((b, s, c), F32), jax.ShapeDtypeStruct((b, 1, c), F32)],
        scratch_shapes=[pltpu.VMEM((b, 1, ct), F32)],
        compiler_params=_cparams(2, 32),
        name="lru_scan",
    )(a, bx, h0.reshape(b, 1, c))
    return hs, ht.reshape(b, c)


def _rms(x, g):
    return x * lax.rsqrt(jnp.mean(x * x, axis=-1, keepdims=True) + EPS) * g


def _rope_tables(n_tokens, rot_dim):
    t = jnp.arange(n_tokens)
    row = (t // GRID_W).astype(F32)
    col = (t % GRID_W).astype(F32)
    half = rot_dim // 2
    inv = ROPE_BASE ** (-jnp.arange(0, half, 2, dtype=F32) / half)
    ar = row[:, None] * inv
    ac = col[:, None] * inv
    ang = jnp.concatenate([ar, ar, ac, ac], axis=-1)
    return jnp.cos(ang), jnp.sin(ang)


def _rotate_half(z):
    z1, z2 = jnp.split(z, 2, axis=-1)
    return jnp.concatenate([-z2, z1], axis=-1)


def _axial_rope(x, cos, sin):
    xr, xc = jnp.split(x, 2, axis=-1)
    rot = jnp.concatenate([_rotate_half(xr), _rotate_half(xc)], axis=-1)
    return x * cos + rot * sin


def _rope_tail(x, cos, sin):
    n = cos.shape[-1]
    return jnp.concatenate([x[..., :-n], _axial_rope(x[..., -n:], cos, sin)], axis=-1)


def _dwconv(x, w, b):
    k = w.shape[0]
    s = x.shape[1]
    left = k // 2
    xp = jnp.pad(x, ((0, 0), (left, k - 1 - left), (0, 0)))
    out = b
    for t in range(k):
        out = out + xp[:, t:t + s] * w[t]
    return out


def _heads_first(x):
    return x.transpose(0, 2, 1, 3)


def _mixer_nat(x, lay, mod, gate, dims, cache_k, cache_v, j, w_qkv, g_q, g_k, rpb, w_o):
    bc, sc, bl, n = dims
    heads, dh = rpb.shape[0], g_q.shape[0]
    qkv = _linear(x, w_qkv.astype(BF16), lay=lay, mod=mod, name="nat_qkv")

    def split(rows, b, s):
        t = rows.reshape(b, s, 3, heads, dh)
        return (_heads_first(_rms(t[:, :, 0], g_q)), _heads_first(_rms(t[:, :, 1], g_k)), _heads_first(t[:, :, 2]))

    qc, kc, vc = split(qkv[:lay.ctx_rows], bc, sc)
    ql, kl, vl = split(qkv[lay.ctx_rows:], bl, n)
    oc = _ctx_attention(qc.reshape(bc * heads, sc, dh), kc.reshape(bc * heads, sc, dh),
                        vc.reshape(bc * heads, sc, dh)).reshape(bc, heads, sc, dh)
    ol = _nat_attention(ql, kl, vl, cache_k, cache_v, j, rpb)
    o = jnp.concatenate([_heads_first(oc).reshape(bc * sc, heads * dh),
                         _heads_first(ol).reshape(bl * n, heads * dh)], axis=0)
    x = _linear(o, w_o.astype(BF16), lay=lay, res=x, gate=gate, name="nat_out")
    return x, kc, vc


def _mixer_lru(x, lay, mod, gate, dims, state, w_in, conv_w, conv_b, w_a, b_a, w_i, b_i, lam, w_out):
    bc, sc, bl, n = dims
    c = conv_w.shape[1]
    u = _linear(x, w_in.astype(BF16), lay=lay, mod=mod, name="lru_in")
    xb, gt = u[:, :c], u[:, c:]
    xc = jnp.concatenate([_dwconv(xb[:lay.ctx_rows].reshape(bc, sc, c), conv_w, conv_b).reshape(bc * sc, c),
                          _dwconv(xb[lay.ctx_rows:].reshape(bl, n, c), conv_w, conv_b).reshape(bl * n, c)], axis=0)
    a, bx = _lru_gates(xc, w_a, b_a, w_i, b_i, lam)

    def run(rows, b, s, h0):
        hf, tf = _lru_scan(a[0, rows].reshape(b, s, c), bx[0, rows].reshape(b, s, c), h0[:, 0], False)
        hb, tb = _lru_scan(a[1, rows].reshape(b, s, c), bx[1, rows].reshape(b, s, c), h0[:, 1], True)
        return (hf + hb).reshape(b * s, c), jnp.stack([tf, tb], axis=1)

    hc, st = run(slice(0, lay.ctx_rows), bc, sc, jnp.zeros((bc, 2, c), F32))
    hl, _ = run(slice(lay.ctx_rows, lay.rows), bl, n, state.astype(F32))
    y = jax.nn.gelu(gt) * jnp.concatenate([hc, hl], axis=0)
    x = _linear(y, w_out.astype(BF16), lay=lay, res=x, gate=gate, name="lru_out")
    return x, st


def _mixer_mla(x, lay, mod, gate, dims, cache_ckv, cache_kr, w_down, g_qa, g_kva, w_uq, w_ukv, g_q, g_k, w_o):
    bc, sc, bl, n = dims
    q_rank, kv_rank = g_qa.shape[0], g_kva.shape[0]
    qk_dim = g_q.shape[0]
    heads = w_uq.shape[1] // qk_dim
    rope = w_down.shape[1] - q_rank - kv_rank
    nope = qk_dim - rope
    dv = w_ukv.shape[1] // heads - nope
    w_ukv_b = w_ukv.astype(BF16)
    d = _linear(x, w_down.astype(BF16), lay=lay, mod=mod, name="mla_down")
    cq = _rms(d[:, :q_rank], g_qa)
    ckv = _rms(d[:, q_rank:q_rank + kv_rank], g_kva)
    kr = d[:, q_rank + kv_rank:]
    qa = _rms(_linear(cq, w_uq.astype(BF16), lay=lay, name="mla_uq").reshape(lay.rows, heads, qk_dim), g_q)
    kv = _linear(ckv, w_ukv_b, lay=lay, name="mla_ukv").reshape(lay.rows, heads, nope + dv)

    def keys(kv_, kr_):
        rows = kv_.shape[0]
        krb = jnp.broadcast_to(kr_[:, None, :], (rows, heads, rope))
        return _rms(jnp.concatenate([kv_[..., :nope], krb], axis=-1), g_k), kv_[..., nope:]

    ka, va = keys(kv, kr)
    cos, sin = _rope_tables(n, rope)
    r4 = lambda t, b, s: _heads_first(t.reshape(b, s, heads, t.shape[-1]))
    qc, kc, vc = r4(qa[:lay.ctx_rows], bc, sc), r4(ka[:lay.ctx_rows], bc, sc), r4(va[:lay.ctx_rows], bc, sc)
    ql = _rope_tail(r4(qa[lay.ctx_rows:], bl, n), cos, sin)
    kl = _rope_tail(r4(ka[lay.ctx_rows:], bl, n), cos, sin)
    vl = r4(va[lay.ctx_rows:], bl, n)
    p = cache_ckv.shape[1]
    kvp = _linear(cache_ckv.reshape(bl * p, kv_rank), w_ukv_b, name="mla_ukv_cache").reshape(bl * p, heads, nope + dv)
    kp, vp = keys(kvp, cache_kr.reshape(bl * p, rope))
    oc = _ctx_attention(qc.reshape(bc * heads, sc, qk_dim), kc.reshape(bc * heads, sc, qk_dim),
                        vc.reshape(bc * heads, sc, dv)).reshape(bc, heads, sc, dv)
    ol = _joint_dense_attention(ql, kl, vl, r4(kp, bl, p), r4(vp, bl, p))
    o = jnp.concatenate([_heads_first(oc).reshape(bc * sc, heads * dv),
                         _heads_first(ol).reshape(bl * n, heads * dv)], axis=0)
    x = _linear(o, w_o.astype(BF16), lay=lay, res=x, gate=gate, name="mla_out")
    return x, ckv[:lay.ctx_rows].reshape(bc, sc, kv_rank), kr[:lay.ctx_rows].reshape(bc, sc, rope)


def _mixer_swa(x, lay, mod, gate, dims, cache_k, cache_v, j, w_qkv, g_q, g_k, sinks, w_o):
    bc, sc, bl, n = dims
    dh = g_q.shape[0]
    heads = sinks.shape[0]
    kvh = (w_qkv.shape[1] // dh - heads) // 2
    grp = heads // kvh
    qkv = _linear(x, w_qkv.astype(BF16), lay=lay, mod=mod, name="swa_qkv")

    def split(rows, b, s, cos=None, sin=None):
        t = rows.reshape(b, s, heads + 2 * kvh, dh)
        q = _rms(t[:, :, :heads], g_q).reshape(b, s, kvh, grp, dh).transpose(0, 2, 3, 1, 4)
        k = _heads_first(_rms(t[:, :, heads:heads + kvh], g_k))
        v = _heads_first(t[:, :, heads + kvh:])
        if cos is not None:
            q, k = _axial_rope(q, cos, sin), _axial_rope(k, cos, sin)
        return q, k, v

    cos, sin = _rope_tables(n, dh)
    qc, kc, vc = split(qkv[:lay.ctx_rows], bc, sc)
    ql, kl, vl = split(qkv[lay.ctx_rows:], bl, n, cos, sin)
    sinks = sinks.astype(F32)
    oc = _swa_ctx_attention(qc, kc, vc, sinks)
    ol = _swa_lat_attention(ql, kl, vl, cache_k, cache_v, j, sinks)
    merge = lambda o, b, s: o.transpose(0, 3, 1, 2, 4).reshape(b * s, heads * dh)
    o = jnp.concatenate([merge(oc, bc, sc), merge(ol, bl, n)], axis=0)
    x = _linear(o, w_o.astype(BF16), lay=lay, res=x, gate=gate, name="swa_out")
    return x, kc, vc


def kernel(x_prompt, x_sample, cache_nat_k, cache_nat_v, state_lru, cache_mla_ckv, cache_mla_krope, cache_swa_k, cache_swa_v, c, c_ctx, norm_mix, norm_ffn, w_mod, b_mod, ffn_w_in, ffn_conv_w, ffn_conv_b, ffn_w_out, nat_w_qkv, nat_q_norm, nat_k_norm, nat_rpb, nat_w_o, lru_w_in, lru_conv_w, lru_conv_b, lru_w_a, lru_b_a, lru_w_i, lru_b_i, lru_lambda, lru_w_out, mla_w_down, mla_q_a_norm, mla_kv_a_norm, mla_w_uq, mla_w_ukv, mla_q_norm, mla_k_norm, mla_w_o, swa_w_qkv, swa_q_norm, swa_k_norm, swa_sinks, swa_w_o):
    bc, sc, d = x_prompt.shape
    bl, n, _ = x_sample.shape
    depth = w_mod.shape[0]
    dims = (bc, sc, bl, n)
    lay = _Layout(bc * sc, sc, bl * n, n)
    x = jnp.concatenate([x_prompt.reshape(bc * sc, d), x_sample.reshape(bl * n, d)], axis=0)

    n_cond = 1 + bl
    cond_rows = -(-n_cond // 8) * 8
    cond = jnp.zeros((cond_rows, d), F32).at[0].set(c_ctx).at[1:n_cond].set(c)
    mods = _modulation(cond, w_mod, b_mod)[:, :n_cond]

    nat_k_l, nat_v_l, lru_l, ckv_l, krope_l, swa_k_l, swa_v_l = [], [], [], [], [], [], []
    for l in range(depth):
        kind, j = l % 4, l // 4
        m6 = [mods[l, :, None, t * d:(t + 1) * d] for t in range(6)]
        mix_mod = (norm_mix[l].reshape(1, d), m6[0], m6[1])
        if kind == 0:
            x, kc, vc = _mixer_nat(x, lay, mix_mod, m6[2], dims, cache_nat_k, cache_nat_v, j, nat_w_qkv[j],
                                   nat_q_norm[j], nat_k_norm[j], nat_rpb[j], nat_w_o[j])
            nat_k_l.append(kc)
            nat_v_l.append(vc)
        elif kind == 1:
            x, st = _mixer_lru(x, lay, mix_mod, m6[2], dims, state_lru[:, j], lru_w_in[j], lru_conv_w[j],
                               lru_conv_b[j], lru_w_a[j], lru_b_a[j], lru_w_i[j], lru_b_i[j], lru_lambda[j],
                               lru_w_out[j])
            lru_l.append(st)
        elif kind == 2:
            x, ckv, kr = _mixer_mla(x, lay, mix_mod, m6[2], dims, cache_mla_ckv[:, j], cache_mla_krope[:, j],
                                    mla_w_down[j], mla_q_a_norm[j], mla_kv_a_norm[j], mla_w_uq[j], mla_w_ukv[j],
                                    mla_q_norm[j], mla_k_norm[j], mla_w_o[j])
            ckv_l.append(ckv)
            krope_l.append(kr)
        else:
            x, kc, vc = _mixer_swa(x, lay, mix_mod, m6[2], dims, cache_swa_k, cache_swa_v, j, swa_w_qkv[j],
                                   swa_q_norm[j], swa_k_norm[j], swa_sinks[j], swa_w_o[j])
            swa_k_l.append(kc)
            swa_v_l.append(vc)
        x = _conv_ffn(x, lay, norm_ffn[l].reshape(1, d), m6[3], m6[4], m6[5], ffn_w_in[l].astype(BF16),
                      ffn_conv_w[l], ffn_conv_b[l], ffn_w_out[l].astype(BF16))

    xp = x[:lay.ctx_rows].reshape(bc, sc, d)
    xs = x[lay.ctx_rows:].reshape(bl, n, d)
    return (xp, xs, jnp.stack(nat_k_l, axis=1), jnp.stack(nat_v_l, axis=1), jnp.stack(lru_l, axis=1),
            jnp.stack(ckv_l, axis=1), jnp.stack(krope_l, axis=1), jnp.stack(swa_k_l, axis=1),
            jnp.stack(swa_v_l, axis=1))
```

```python
import functools
import math

import numpy as np
import jax
import jax.numpy as jnp
from jax import lax
from jax.experimental import pallas as pl
from jax.experimental.pallas import tpu as pltpu

F32 = jnp.float32
BF16 = jnp.bfloat16

GRID_W = 64
NA_WIN_ROWS = 8
NA_WIN_COLS = 16
NA_Q_ROWS = 8
NA_K_ROWS = 16
LRU_BLOCKS = 16
LRU_C = 8.0
SWA_WINDOW = 128
SWA_BLOCK = 128
ROPE_BASE = 10000.0
EPS = 1e-6
NEG = -1e30
LANE = 128
HALO = 16
MIB = 1024 * 1024


def _cparams(n_axes, vmem_mib):
    return pltpu.CompilerParams(dimension_semantics=("arbitrary",) * n_axes,
                                vmem_limit_bytes=int(vmem_mib * MIB))


def _largest_divisor(n, candidates):
    for c in candidates:
        if n % c == 0:
            return c
    return n


class _Layout:
    def __init__(self, ctx_rows, ctx_seq, lat_rows, lat_seq):
        self.ctx_rows, self.ctx_seq, self.lat_rows, self.lat_seq = ctx_rows, ctx_seq, lat_rows, lat_seq
        self.rows = ctx_rows + lat_rows
        self.bm = _largest_divisor(math.gcd(ctx_rows, lat_seq), (1024, 512, 256, 128, 64, 32, 16))

    def mod_index(self, row0):
        return jnp.where(row0 < self.ctx_rows, 0, 1 + jnp.maximum(row0 - self.ctx_rows, 0) // self.lat_seq)


def _norm_mod(x, g, shift, scale):
    ms = jnp.mean(x * x, axis=-1, keepdims=True)
    y = (x * lax.rsqrt(ms + EPS)) * g
    return y * (1.0 + scale) + shift


def _modulation_kernel(c_ref, w_ref, b_ref, o_ref):
    c = c_ref[...]
    sc = (c * jax.nn.sigmoid(c)).astype(BF16)
    o_ref[...] = jnp.dot(sc, w_ref[...].astype(BF16), preferred_element_type=F32) + b_ref[...]


def _modulation(cond, w_mod, b_mod):
    depth, d, n = w_mod.shape
    rows = cond.shape[0]
    bn = _largest_divisor(n, (512, 256, 128))
    return pl.pallas_call(
        _modulation_kernel,
        grid=(depth, n // bn),
        in_specs=[pl.BlockSpec((rows, d), lambda l, j: (0, 0)),
                  pl.BlockSpec((None, d, bn), lambda l, j: (l, 0, j)),
                  pl.BlockSpec((None, 1, bn), lambda l, j: (l, 0, j))],
        out_specs=pl.BlockSpec((None, rows, bn), lambda l, j: (l, 0, j)),
        out_shape=jax.ShapeDtypeStruct((depth, rows, n), F32),
        compiler_params=_cparams(2, 32),
        name="modulation",
    )(cond, w_mod, b_mod.reshape(depth, 1, n))


def _linear_kernel(*refs, has_mod, has_res, row_chunk):
    it = iter(refs)
    x_ref = next(it)
    if has_mod:
        g_ref, sh_ref, sc_ref = next(it), next(it), next(it)
    w_ref = next(it)
    if has_res:
        res_ref, gate_ref = next(it), next(it)
    o_ref = next(it)
    xs_ref = next(it)
    bm = x_ref.shape[0]

    @pl.when(pl.program_id(1) == 0)
    def _():
        def chunk(r, carry):
            rows = pl.ds(pl.multiple_of(r * row_chunk, row_chunk), row_chunk)
            x = x_ref[rows, :]
            if has_mod:
                x = _norm_mod(x, g_ref[...], sh_ref[...], sc_ref[...])
            xs_ref[rows, :] = x.astype(BF16)
            return carry
        lax.fori_loop(0, bm // row_chunk, chunk, 0)

    acc = jnp.dot(xs_ref[...], w_ref[...], preferred_element_type=F32)
    if has_res:
        acc = res_ref[...] + gate_ref[...] * acc
    o_ref[...] = acc


def _linear(x, w, *, lay=None, mod=None, res=None, gate=None, bm=None, name="linear"):
    m, k = x.shape
    n = w.shape[1]
    if bm is None:
        bm = lay.bm if lay is not None else _largest_divisor(m, (1024, 512, 256, 128, 64, 32, 16, 8))
    bn = _largest_divisor(n, (512, 256, 128))
    row_chunk = min(bm, 128)
    has_mod, has_res = mod is not None, res is not None
    mod_idx = (lambda i: lay.mod_index(i * bm)) if lay is not None else None

    in_specs = [pl.BlockSpec((bm, k), lambda i, j: (i, 0))]
    args = [x]
    if has_mod:
        in_specs += [pl.BlockSpec((1, k), lambda i, j: (0, 0)),
                     pl.BlockSpec((None, 1, k), lambda i, j: (mod_idx(i), 0, 0)),
                     pl.BlockSpec((None, 1, k), lambda i, j: (mod_idx(i), 0, 0))]
        args += list(mod)
    in_specs.append(pl.BlockSpec((k, bn), lambda i, j: (0, j)))
    args.append(w)
    if has_res:
        in_specs += [pl.BlockSpec((bm, bn), lambda i, j: (i, j)),
                     pl.BlockSpec((None, 1, bn), lambda i, j: (mod_idx(i), 0, j))]
        args += [res, gate]
    vmem = (2 * bm * k * 4 + bm * k * 2 + 2 * k * bn * 2 + (4 if has_res else 2) * bm * bn * 4) / MIB + 8
    return pl.pallas_call(
        functools.partial(_linear_kernel, has_mod=has_mod, has_res=has_res, row_chunk=row_chunk),
        grid=(m // bm, n // bn),
        in_specs=in_specs,
        out_specs=pl.BlockSpec((bm, bn), lambda i, j: (i, j)),
        out_shape=jax.ShapeDtypeStruct((m, n), F32),
        scratch_shapes=[pltpu.VMEM((bm, k), BF16)],
        compiler_params=_cparams(2, vmem),
        name=name,
    )(*args)


def _ffn_kernel(xp_ref, x_ref, xn_ref, g_ref, sh_ref, sc_ref, gate_ref, wa_ref, wb_ref, cw_ref, cb_ref,
                wo_ref, o_ref, h_ref, *, bm, ctx_rows, ctx_seq, lat_seq, row_chunk):
    i = pl.program_id(0)
    c = pl.program_id(1)
    n_chunks = pl.num_programs(1)

    @pl.when(c == 0)
    def _():
        g, sh, sc = g_ref[...], sh_ref[...], sc_ref[...]
        h_ref[0:HALO, :] = _norm_mod(xp_ref[...], g, sh, sc).astype(BF16)
        h_ref[HALO + bm:, :] = _norm_mod(xn_ref[...], g, sh, sc).astype(BF16)

        def chunk(r, carry):
            src = pl.ds(pl.multiple_of(r * row_chunk, row_chunk), row_chunk)
            dst = pl.ds(pl.multiple_of(HALO + r * row_chunk, HALO), row_chunk)
            h_ref[dst, :] = _norm_mod(x_ref[src, :], g, sh, sc).astype(BF16)
            return carry
        lax.fori_loop(0, bm // row_chunk, chunk, 0)

    ua = jnp.dot(h_ref[...], wa_ref[...], preferred_element_type=F32)
    ub = jnp.dot(h_ref[HALO:HALO + bm, :], wb_ref[...], preferred_element_type=F32)
    n_all = bm + 2 * HALO
    u_prev = pltpu.roll(ua, 1, 0)[HALO:HALO + bm]
    u_next = pltpu.roll(ua, n_all - 1, 0)[HALO:HALO + bm]
    u_mid = ua[HALO:HALO + bm]
    row = i * bm + lax.broadcasted_iota(jnp.int32, (bm, 1), 0)
    seq = jnp.where(i * bm < ctx_rows, ctx_seq, lat_seq)
    pos = jnp.bitwise_and(row - jnp.where(i * bm < ctx_rows, 0, ctx_rows), seq - 1)
    u_prev = jnp.where(pos == 0, 0.0, u_prev)
    u_next = jnp.where(pos == seq - 1, 0.0, u_next)
    cw = cw_ref[...]
    a = cb_ref[...] + u_prev * cw[0:1] + u_mid * cw[1:2] + u_next * cw[2:3]
    gated = ((a * jax.nn.sigmoid(a)) * ub).astype(BF16)
    contrib = jnp.dot(gated, wo_ref[...], preferred_element_type=F32)

    @pl.when(c == 0)
    def _():
        o_ref[...] = contrib

    @pl.when(c > 0)
    def _():
        o_ref[...] += contrib

    @pl.when(c == n_chunks - 1)
    def _():
        o_ref[...] = x_ref[...] + gate_ref[...] * o_ref[...]


def _conv_ffn(x, lay, g, shift, scale, gate, w_in, conv_w, conv_b, w_out):
    m, d = x.shape
    d_ff = w_out.shape[0]
    bm = min(lay.bm, 512)
    ck = _largest_divisor(d_ff, (512, 256, 128))
    n_chunks = d_ff // ck
    n_halo_blocks = m // HALO
    assert lay.ctx_seq & (lay.ctx_seq - 1) == 0 and lay.lat_seq & (lay.lat_seq - 1) == 0
    mod_idx = lambda i: lay.mod_index(i * bm)
    kern = functools.partial(_ffn_kernel, bm=bm, ctx_rows=lay.ctx_rows, ctx_seq=lay.ctx_seq,
                             lat_seq=lay.lat_seq, row_chunk=min(bm, 128))
    vmem = (4 * bm * d * 4 + (bm + 2 * HALO) * d * 2 + 6 * d * ck * 2 + 6 * (bm + 2 * HALO) * ck * 4) / MIB + 10
    return pl.pallas_call(
        kern,
        grid=(m // bm, n_chunks),
        in_specs=[
            pl.BlockSpec((HALO, d), lambda i, c: (jnp.maximum(i * (bm // HALO) - 1, 0), 0)),
            pl.BlockSpec((bm, d), lambda i, c: (i, 0)),
            pl.BlockSpec((HALO, d), lambda i, c: (jnp.minimum((i + 1) * (bm // HALO), n_halo_blocks - 1), 0)),
            pl.BlockSpec((1, d), lambda i, c: (0, 0)),
            pl.BlockSpec((None, 1, d), lambda i, c: (mod_idx(i), 0, 0)),
            pl.BlockSpec((None, 1, d), lambda i, c: (mod_idx(i), 0, 0)),
            pl.BlockSpec((None, 1, d), lambda i, c: (mod_idx(i), 0, 0)),
            pl.BlockSpec((d, ck), lambda i, c: (0, c)),
            pl.BlockSpec((d, ck), lambda i, c: (0, n_chunks + c)),
            pl.BlockSpec((conv_w.shape[0], ck), lambda i, c: (0, c)),
            pl.BlockSpec((1, ck), lambda i, c: (0, c)),
            pl.BlockSpec((ck, d), lambda i, c: (c, 0)),
        ],
        out_specs=pl.BlockSpec((bm, d), lambda i, c: (i, 0)),
        out_shape=jax.ShapeDtypeStruct((m, d), F32),
        scratch_shapes=[pltpu.VMEM((bm + 2 * HALO, d), BF16)],
        compiler_params=_cparams(2, vmem),
        name="conv_ffn",
    )(x, x, x, g, shift, scale, gate, w_in, w_in, conv_w, conv_b.reshape(1, d_ff), w_out)


def _qk(q, k):
    return lax.dot_general(q, k, (((1,), (1,)), ((), ())), preferred_element_type=F32)


def _attend(scores, values, sink=None):
    m = None
    for s in scores:
        mi = jnp.max(s, axis=-1, keepdims=True)
        m = mi if m is None else jnp.maximum(m, mi)
    if sink is not None:
        m = jnp.maximum(m, sink)
    es = [jnp.exp(s - m) for s in scores]
    den = None
    for e in es:
        di = jnp.sum(e, axis=-1, keepdims=True)
        den = di if den is None else den + di
    if sink is not None:
        den = den + jnp.exp(sink - m)
    inv = 1.0 / den
    out = None
    for e, v in zip(es, values):
        oi = jnp.dot((e * inv).astype(BF16), v, preferred_element_type=F32)
        out = oi if out is None else out + oi
    return out


def _ctx_attn_kernel(q_ref, k_ref, v_ref, o_ref, *, scale, heads):
    for h in range(heads):
        q = q_ref[h].astype(BF16)
        k = k_ref[h].astype(BF16)
        v = v_ref[h].astype(BF16)
        o_ref[h] = _attend([_qk(q, k) * scale], [v])


def _ctx_attention(q, k, v):
    g, s, dq = q.shape
    dv = v.shape[-1]
    hb = _largest_divisor(g, (8, 4, 2, 1))
    return pl.pallas_call(
        functools.partial(_ctx_attn_kernel, scale=dq ** -0.5, heads=hb),
        grid=(g // hb,),
        in_specs=[pl.BlockSpec((hb, s, dq), lambda i: (i, 0, 0)),
                  pl.BlockSpec((hb, s, dq), lambda i: (i, 0, 0)),
                  pl.BlockSpec((hb, s, dv), lambda i: (i, 0, 0))],
        out_specs=pl.BlockSpec((hb, s, dv), lambda i: (i, 0, 0)),
        out_shape=jax.ShapeDtypeStruct((g, s, dv), F32),
        compiler_params=_cparams(1, 32),
        name="ctx_attention",
    )(q, k, v)


def _nat_kernel(q_ref, k_ref, v_ref, kc_ref, vc_ref, bias_ref, o_ref, *, scale, n_blocks, key_rows, rows):
    i = pl.program_id(2)
    n_keys = key_rows * GRID_W
    first_row = jnp.clip(i * NA_Q_ROWS - NA_WIN_ROWS // 2, 0, rows - key_rows)
    start = pl.multiple_of(first_row * GRID_W, GRID_W * 4)
    q = q_ref[...].astype(BF16)
    k = k_ref[pl.ds(start, n_keys), :].astype(BF16)
    v = v_ref[pl.ds(start, n_keys), :].astype(BF16)
    s_loc = _qk(q, k) * scale + bias_ref[...]
    s_ctx = _qk(q, kc_ref[...].astype(BF16)) * scale
    o_ref[...] = _attend([s_loc, s_ctx], [v, vc_ref[...].astype(BF16)])


def _nat_bias(rpb, rows):
    n_blocks = rows // NA_Q_ROWS
    key_rows = min(NA_K_ROWS, rows)
    wr = min(NA_WIN_ROWS, rows)
    reps = [0, min(1, n_blocks - 1), n_blocks - 1]
    heads = rpb.shape[0]
    nq, nk = NA_Q_ROWS * GRID_W, key_rows * GRID_W
    shape = (NA_Q_ROWS, GRID_W, key_rows, GRID_W)
    qc = np.arange(GRID_W)
    cstart = np.clip(qc - NA_WIN_COLS // 2, 0, GRID_W - NA_WIN_COLS)
    col_ok = (qc[None, :] >= cstart[:, None]) & (qc[None, :] < cstart[:, None] + NA_WIN_COLS)
    rp = jnp.pad(rpb.astype(F32), ((0, 0), (key_rows, key_rows), (GRID_W - NA_WIN_COLS, GRID_W - NA_WIN_COLS)))
    row_slabs, mask_l = [], []
    for i in reps:
        ks = int(np.clip(i * NA_Q_ROWS - NA_WIN_ROWS // 2, 0, rows - key_rows))
        r = i * NA_Q_ROWS + np.arange(NA_Q_ROWS)
        rs = np.clip(r - wr // 2, 0, rows - wr)
        kr = ks + np.arange(key_rows)
        row_ok = (kr[None, :] >= rs[:, None]) & (kr[None, :] < rs[:, None] + wr)
        for rq in range(NA_Q_ROWS):
            first = ks - int(r[rq]) + NA_WIN_ROWS - 1 + key_rows
            assert 0 <= first and first + key_rows <= rp.shape[1]
            row_slabs.append(rp[:, first:first + key_rows, :])
        mask_l.append(np.broadcast_to(row_ok[:, None, :, None] & col_ok[None, :, None, :], shape).reshape(nq, nk))
    slab = jnp.stack(row_slabs, axis=1).reshape(heads, len(reps), NA_Q_ROWS, key_rows, 2 * GRID_W - 1)
    toep = jnp.stack([slab[..., GRID_W - 1 - c:2 * GRID_W - 1 - c] for c in range(GRID_W)], axis=3)
    bias = toep.reshape(heads, len(reps), nq, nk)
    return jnp.where(jnp.asarray(np.stack(mask_l))[None], bias, NEG)


def _nat_attention(q, k, v, cache_k, cache_v, j, rpb):
    b, h, n, dh = q.shape
    p = cache_k.shape[3]
    rows = n // GRID_W
    assert rows % NA_Q_ROWS == 0 and rows >= NA_K_ROWS
    n_blocks = rows // NA_Q_ROWS
    key_rows = min(NA_K_ROWS, rows)
    nq, nk = NA_Q_ROWS * GRID_W, key_rows * GRID_W
    bias = _nat_bias(rpb, rows)
    btype = lambda i: jnp.where(i == 0, 0, jnp.where(i == n_blocks - 1, 2, 1))
    kern = functools.partial(_nat_kernel, scale=dh ** -0.5, n_blocks=n_blocks, key_rows=key_rows, rows=rows)
    return pl.pallas_call(
        kern,
        grid=(b, h, n_blocks),
        in_specs=[pl.BlockSpec((None, None, nq, dh), lambda b_, h_, i: (b_, h_, i, 0)),
                  pl.BlockSpec((None, None, n, dh), lambda b_, h_, i: (b_, h_, 0, 0)),
                  pl.BlockSpec((None, None, n, dh), lambda b_, h_, i: (b_, h_, 0, 0)),
                  pl.BlockSpec((None, None, None, p, dh), lambda b_, h_, i: (b_, j, h_, 0, 0)),
                  pl.BlockSpec((None, None, None, p, dh), lambda b_, h_, i: (b_, j, h_, 0, 0)),
                  pl.BlockSpec((None, None, nq, nk), lambda b_, h_, i: (h_, btype(i), 0, 0))],
        out_specs=pl.BlockSpec((None, None, nq, dh), lambda b_, h_, i: (b_, h_, i, 0)),
        out_shape=jax.ShapeDtypeStruct((b, h, n, dh), F32),
        compiler_params=_cparams(3, 40),
        name="nat_attention",
    )(q, k, v, cache_k, cache_v, bias)


def _joint_dense_kernel(q_ref, k_ref, v_ref, kc_ref, vc_ref, o_ref, *, scale):
    q = q_ref[...].astype(BF16)
    s_lat = _qk(q, k_ref[...].astype(BF16)) * scale
    s_ctx = _qk(q, kc_ref[...].astype(BF16)) * scale
    o_ref[...] = _attend([s_lat, s_ctx], [v_ref[...].astype(BF16), vc_ref[...].astype(BF16)])


def _joint_dense_attention(q, k, v, kc, vc):
    b, h, n, dq = q.shape
    dv = v.shape[-1]
    p = kc.shape[2]
    bq = _largest_divisor(n, (256, 128, 64, 32, 16, 8))
    return pl.pallas_call(
        functools.partial(_joint_dense_kernel, scale=dq ** -0.5),
        grid=(b, h, n // bq),
        in_specs=[pl.BlockSpec((None, None, bq, dq), lambda b_, h_, i: (b_, h_, i, 0)),
                  pl.BlockSpec((None, None, n, dq), lambda b_, h_, i: (b_, h_, 0, 0)),
                  pl.BlockSpec((None, None, n, dv), lambda b_, h_, i: (b_, h_, 0, 0)),
                  pl.BlockSpec((None, None, p, dq), lambda b_, h_, i: (b_, h_, 0, 0)),
                  pl.BlockSpec((None, None, p, dv), lambda b_, h_, i: (b_, h_, 0, 0))],
        out_specs=pl.BlockSpec((None, None, bq, dv), lambda b_, h_, i: (b_, h_, i, 0)),
        out_shape=jax.ShapeDtypeStruct((b, h, n, dv), F32),
        compiler_params=_cparams(3, 48),
        name="mla_attention",
    )(q, k, v, kc, vc)


def _sink_column(sinks_ref, kvh, groups, rows_per_group):
    row_group = lax.broadcasted_iota(jnp.int32, (groups * rows_per_group, 1), 0) // rows_per_group
    col = jnp.zeros((groups * rows_per_group, 1), F32)
    for g in range(groups):
        col = jnp.where(row_group == g, sinks_ref[kvh * groups + g], col)
    return col


def _swa_ctx_kernel(sinks_ref, q_ref, k_ref, v_ref, o_ref, *, scale, kv_heads):
    groups, s, dh = q_ref.shape
    kvh = pl.program_id(0) % kv_heads
    q = q_ref[...].reshape(groups * s, dh).astype(BF16)
    sc = _qk(q, k_ref[...].astype(BF16)) * scale
    sink = _sink_column(sinks_ref, kvh, groups, s)
    o_ref[...] = _attend([sc], [v_ref[...].astype(BF16)], sink).reshape(groups, s, dh)


def _swa_ctx_attention(q, k, v, sinks):
    b, kvh, g, s, dh = q.shape
    q = q.reshape(b * kvh, g, s, dh)
    k = k.reshape(b * kvh, s, dh)
    v = v.reshape(b * kvh, s, dh)
    out = pl.pallas_call(
        functools.partial(_swa_ctx_kernel, scale=dh ** -0.5, kv_heads=kvh),
        grid=(b * kvh,),
        in_specs=[pl.BlockSpec(memory_space=pltpu.SMEM),
                  pl.BlockSpec((None, g, s, dh), lambda i: (i, 0, 0, 0)),
                  pl.BlockSpec((None, s, dh), lambda i: (i, 0, 0)),
                  pl.BlockSpec((None, s, dh), lambda i: (i, 0, 0))],
        out_specs=pl.BlockSpec((None, g, s, dh), lambda i: (i, 0, 0, 0)),
        out_shape=jax.ShapeDtypeStruct((b * kvh, g, s, dh), F32),
        compiler_params=_cparams(1, 32),
        name="swa_ctx_attention",
    )(sinks, q, k, v)
    return out.reshape(b, kvh, g, s, dh)


def _swa_lat_kernel(sinks_ref, q_ref, k_ref, v_ref, kc_ref, vc_ref, o_ref, *, scale, n):
    groups, bq, dh = q_ref.shape
    kvh = pl.program_id(1)
    blk = pl.program_id(2)
    n_keys = min(3 * SWA_BLOCK, n)
    start = pl.multiple_of(jnp.clip((blk - 1) * SWA_BLOCK, 0, n - n_keys), SWA_BLOCK)
    q = q_ref[...].reshape(groups * bq, dh).astype(BF16)
    k = k_ref[pl.ds(start, n_keys), :].astype(BF16)
    v = v_ref[pl.ds(start, n_keys), :].astype(BF16)
    qpos = blk * SWA_BLOCK + lax.broadcasted_iota(jnp.int32, (groups * bq, 1), 0) % bq
    kpos = start + lax.broadcasted_iota(jnp.int32, (1, n_keys), 1)
    s_loc = jnp.where(jnp.abs(qpos - kpos) <= SWA_WINDOW, _qk(q, k) * scale, NEG)
    s_ctx = _qk(q, kc_ref[...].astype(BF16)) * scale
    sink = _sink_column(sinks_ref, kvh, groups, bq)
    out = _attend([s_loc, s_ctx], [v, vc_ref[...].astype(BF16)], sink)
    o_ref[...] = out.reshape(groups, bq, dh)


def _swa_lat_attention(q, k, v, cache_k, cache_v, j, sinks):
    b, kvh, g, n, dh = q.shape
    p = cache_k.shape[3]
    assert n % SWA_BLOCK == 0
    return pl.pallas_call(
        functools.partial(_swa_lat_kernel, scale=dh ** -0.5, n=n),
        grid=(b, kvh, n // SWA_BLOCK),
        in_specs=[pl.BlockSpec(memory_space=pltpu.SMEM),
                  pl.BlockSpec((None, None, g, SWA_BLOCK, dh), lambda b_, k_, i: (b_, k_, 0, i, 0)),
                  pl.BlockSpec((None, None, n, dh), lambda b_, k_, i: (b_, k_, 0, 0)),
                  pl.BlockSpec((None, None, n, dh), lambda b_, k_, i: (b_, k_, 0, 0)),
                  pl.BlockSpec((None, None, None, p, dh), lambda b_, k_, i: (b_, j, k_, 0, 0)),
                  pl.BlockSpec((None, None, None, p, dh), lambda b_, k_, i: (b_, j, k_, 0, 0))],
        out_specs=pl.BlockSpec((None, None, g, SWA_BLOCK, dh), lambda b_, k_, i: (b_, k_, 0, i, 0)),
        out_shape=jax.ShapeDtypeStruct((b, kvh, g, n, dh), F32),
        compiler_params=_cparams(3, 32),
        name="swa_attention",
    )(sinks, q, k, v, cache_k, cache_v)


def _band_plan(width, block):
    n_tiles = width // LANE
    lo = [((t * LANE) // block) * block for t in range(n_tiles)]
    hi = [(((t + 1) * LANE - 1) // block + 1) * block for t in range(n_tiles)]
    start = [(l // LANE) * LANE for l in lo]
    kb = max(-(-(h - s) // LANE) * LANE for h, s in zip(hi, start))
    kb = min(kb, width)
    start = [min(s, width - kb) for s in start]
    return start, kb


def _band_weights(w, width, block, start, kb):
    n_tiles = width // LANE
    wb = w.astype(BF16)
    tiles = []
    for t in range(n_tiles):
        pieces = []
        col = t * LANE
        while col < (t + 1) * LANE:
            blk = col // block
            col_end = min((blk + 1) * block, (t + 1) * LANE)
            sub = wb[blk, :, col - blk * block:col_end - blk * block]
            top = blk * block - start[t]
            pieces.append(jnp.pad(sub, ((top, kb - top - block), (0, 0))))
            col = col_end
        tiles.append(jnp.concatenate(pieces, axis=1))
    return jnp.stack(tiles)


def _expm1_neg(x):
    return -jnp.tanh(0.5 * x) * (jnp.exp(x) + 1.0)


def _lru_gate_kernel(xc_ref, wa_ref, wi_ref, ba_ref, bi_ref, lam_ref, a_ref, bx_ref, *, starts, kb):
    n_tiles = len(starts)
    x = xc_ref[...]
    xb = x.astype(BF16)
    neg_lam = -lam_ref[...]
    softplus = jnp.maximum(neg_lam, 0.0) + jnp.log1p(jnp.exp(-jnp.abs(neg_lam)))
    for d in range(2):
        for t in range(n_tiles):
            lanes = slice(t * LANE, (t + 1) * LANE)
            xw = xb[:, starts[t]:starts[t] + kb]
            r = jax.nn.sigmoid(jnp.dot(xw, wa_ref[d, t], preferred_element_type=F32) + ba_ref[d:d + 1, lanes])
            ig = jax.nn.sigmoid(jnp.dot(xw, wi_ref[d, t], preferred_element_type=F32) + bi_ref[d:d + 1, lanes])
            log_a = -LRU_C * r * softplus[d:d + 1, lanes]
            a_ref[d, :, lanes] = jnp.exp(log_a)
            bx_ref[d, :, lanes] = jnp.sqrt(_expm1_neg(2.0 * log_a)) * (ig * x[:, lanes])


def _lru_gates(xc, w_a, b_a, w_i, b_i, lam):
    m, c = xc.shape
    block = w_a.shape[-1]
    assert c % LANE == 0
    starts, kb = _band_plan(c, block)
    wa = jnp.stack([_band_weights(w_a[d], c, block, starts, kb) for d in range(2)])
    wi = jnp.stack([_band_weights(w_i[d], c, block, starts, kb) for d in range(2)])
    bm = _largest_divisor(m, (256, 128, 64, 32, 16, 8))
    n_tiles = c // LANE
    full = lambda *shape: pl.BlockSpec(shape, lambda i: (0,) * len(shape))
    out = pl.pallas_call(
        functools.partial(_lru_gate_kernel, starts=tuple(starts), kb=kb),
        grid=(m // bm,),
        in_specs=[pl.BlockSpec((bm, c), lambda i: (i, 0)),
                  full(2, n_tiles, kb, LANE), full(2, n_tiles, kb, LANE),
                  full(2, c), full(2, c), full(2, c)],
        out_specs=[pl.BlockSpec((2, bm, c), lambda i: (0, i, 0)),
                   pl.BlockSpec((2, bm, c), lambda i: (0, i, 0))],
        out_shape=[jax.ShapeDtypeStruct((2, m, c), F32), jax.ShapeDtypeStruct((2, m, c), F32)],
        compiler_params=_cparams(1, 48),
        name="lru_gates",
    )(xc, wa, wi, b_a, b_i, lam)
    return out


def _scan_kernel(a_ref, b_ref, h0_ref, hs_ref, ht_ref, carry_ref, *, reverse, unroll):
    sc = a_ref.shape[1]

    @pl.when(pl.program_id(1) == 0)
    def _():
        carry_ref[...] = h0_ref[...]

    def body(t, h):
        tt = (sc - 1 - t) if reverse else t
        h = a_ref[:, pl.ds(tt, 1), :] * h + b_ref[:, pl.ds(tt, 1), :]
        hs_ref[:, pl.ds(tt, 1), :] = h
        return h

    h = lax.fori_loop(0, sc, body, carry_ref[...], unroll=unroll)
    carry_ref[...] = h
    ht_ref[...] = h


def _lru_scan(a, bx, h0, reverse):
    b, s, c = a.shape
    ct = LANE
    sc = _largest_divisor(s, (256, 128, 64, 32, 16, 8))
    ns = s // sc
    smap = (lambda j: ns - 1 - j) if reverse else (lambda j: j)
    hs, ht = pl.pallas_call(
        functools.partial(_scan_kernel, reverse=reverse, unroll=8),
        grid=(c // ct, ns),
        in_specs=[pl.BlockSpec((b, sc, ct), lambda i, j: (0, smap(j), i)),
                  pl.BlockSpec((b, sc, ct), lambda i, j: (0, smap(j), i)),
                  pl.BlockSpec((b, 1, ct), lambda i, j: (0, 0, i))],
        out_specs=[pl.BlockSpec((b, sc, ct), lambda i, j: (0, smap(j), i)),
                   pl.BlockSpec((b, 1, ct), lambda i, j: (0, 0, i))],
        out_shape=[jax.ShapeDtypeStruct((b, s, c), F32), jax.ShapeDtypeStruct((b, 1, c), F32)],
        scratch_shapes=[pltpu.VMEM((b, 1, ct), F32)],
        compiler_params=_cparams(2, 32),
        name="lru_scan",
    )(a, bx, h0.reshape(b, 1, c))
    return hs, ht.reshape(b, c)


def _rms(x, g):
    return x * lax.rsqrt(jnp.mean(x * x, axis=-1, keepdims=True) + EPS) * g


def _rope_tables(n_tokens, rot_dim):
    t = jnp.arange(n_tokens)
    row = (t // GRID_W).astype(F32)
    col = (t % GRID_W).astype(F32)
    half = rot_dim // 2
    inv = ROPE_BASE ** (-jnp.arange(0, half, 2, dtype=F32) / half)
    ar = row[:, None] * inv
    ac = col[:, None] * inv
    ang = jnp.concatenate([ar, ar, ac, ac], axis=-1)
    return jnp.cos(ang), jnp.sin(ang)


def _rotate_half(z):
    z1, z2 = jnp.split(z, 2, axis=-1)
    return jnp.concatenate([-z2, z1], axis=-1)


def _axial_rope(x, cos, sin):
    xr, xc = jnp.split(x, 2, axis=-1)
    rot = jnp.concatenate([_rotate_half(xr), _rotate_half(xc)], axis=-1)
    return x * cos + rot * sin


def _rope_tail(x, cos, sin):
    n = cos.shape[-1]
    return jnp.concatenate([x[..., :-n], _axial_rope(x[..., -n:], cos, sin)], axis=-1)


def _dwconv(x, w, b):
    k = w.shape[0]
    s = x.shape[1]
    left = k // 2
    xp = jnp.pad(x, ((0, 0), (left, k - 1 - left), (0, 0)))
    out = b
    for t in range(k):
        out = out + xp[:, t:t + s] * w[t]
    return out


def _heads_first(x):
    return x.transpose(0, 2, 1, 3)


def _mixer_nat(x, lay, mod, gate, dims, cache_k, cache_v, j, w_qkv, g_q, g_k, rpb, w_o):
    bc, sc, bl, n = dims
    heads, dh = rpb.shape[0], g_q.shape[0]
    qkv = _linear(x, w_qkv.astype(BF16), lay=lay, mod=mod, name="nat_qkv")

    def split(rows, b, s):
        t = rows.reshape(b, s, 3, heads, dh)
        return (_heads_first(_rms(t[:, :, 0], g_q)), _heads_first(_rms(t[:, :, 1], g_k)), _heads_first(t[:, :, 2]))

    qc, kc, vc = split(qkv[:lay.ctx_rows], bc, sc)
    ql, kl, vl = split(qkv[lay.ctx_rows:], bl, n)
    oc = _ctx_attention(qc.reshape(bc * heads, sc, dh), kc.reshape(bc * heads, sc, dh),
                        vc.reshape(bc * heads, sc, dh)).reshape(bc, heads, sc, dh)
    ol = _nat_attention(ql, kl, vl, cache_k, cache_v, j, rpb)
    o = jnp.concatenate([_heads_first(oc).reshape(bc * sc, heads * dh),
                         _heads_first(ol).reshape(bl * n, heads * dh)], axis=0)
    x = _linear(o, w_o.astype(BF16), lay=lay, res=x, gate=gate, name="nat_out")
    return x, kc, vc


def _mixer_lru(x, lay, mod, gate, dims, state, w_in, conv_w, conv_b, w_a, b_a, w_i, b_i, lam, w_out):
    bc, sc, bl, n = dims
    c = conv_w.shape[1]
    u = _linear(x, w_in.astype(BF16), lay=lay, mod=mod, name="lru_in")
    xb, gt = u[:, :c], u[:, c:]
    xc = jnp.concatenate([_dwconv(xb[:lay.ctx_rows].reshape(bc, sc, c), conv_w, conv_b).reshape(bc * sc, c),
                          _dwconv(xb[lay.ctx_rows:].reshape(bl, n, c), conv_w, conv_b).reshape(bl * n, c)], axis=0)
    a, bx = _lru_gates(xc, w_a, b_a, w_i, b_i, lam)

    def run(rows, b, s, h0):
        hf, tf = _lru_scan(a[0, rows].reshape(b, s, c), bx[0, rows].reshape(b, s, c), h0[:, 0], False)
        hb, tb = _lru_scan(a[1, rows].reshape(b, s, c), bx[1, rows].reshape(b, s, c), h0[:, 1], True)
        return (hf + hb).reshape(b * s, c), jnp.stack([tf, tb], axis=1)

    hc, st = run(slice(0, lay.ctx_rows), bc, sc, jnp.zeros((bc, 2, c), F32))
    hl, _ = run(slice(lay.ctx_rows, lay.rows), bl, n, state.astype(F32))
    y = jax.nn.gelu(gt) * jnp.concatenate([hc, hl], axis=0)
    x = _linear(y, w_out.astype(BF16), lay=lay, res=x, gate=gate, name="lru_out")
    return x, st


def _mixer_mla(x, lay, mod, gate, dims, cache_ckv, cache_kr, w_down, g_qa, g_kva, w_uq, w_ukv, g_q, g_k, w_o):
    bc, sc, bl, n = dims
    q_rank, kv_rank = g_qa.shape[0], g_kva.shape[0]
    qk_dim = g_q.shape[0]
    heads = w_uq.shape[1] // qk_dim
    rope = w_down.shape[1] - q_rank - kv_rank
    nope = qk_dim - rope
    dv = w_ukv.shape[1] // heads - nope
    w_ukv_b = w_ukv.astype(BF16)
    d = _linear(x, w_down.astype(BF16), lay=lay, mod=mod, name="mla_down")
    cq = _rms(d[:, :q_rank], g_qa)
    ckv = _rms(d[:, q_rank:q_rank + kv_rank], g_kva)
    kr = d[:, q_rank + kv_rank:]
    qa = _rms(_linear(cq, w_uq.astype(BF16), lay=lay, name="mla_uq").reshape(lay.rows, heads, qk_dim), g_q)
    kv = _linear(ckv, w_ukv_b, lay=lay, name="mla_ukv").reshape(lay.rows, heads, nope + dv)

    def keys(kv_, kr_):
        rows = kv_.shape[0]
        krb = jnp.broadcast_to(kr_[:, None, :], (rows, heads, rope))
        return _rms(jnp.concatenate([kv_[..., :nope], krb], axis=-1), g_k), kv_[..., nope:]

    ka, va = keys(kv, kr)
    cos, sin = _rope_tables(n, rope)
    r4 = lambda t, b, s: _heads_first(t.reshape(b, s, heads, t.shape[-1]))
    qc, kc, vc = r4(qa[:lay.ctx_rows], bc, sc), r4(ka[:lay.ctx_rows], bc, sc), r4(va[:lay.ctx_rows], bc, sc)
    ql = _rope_tail(r4(qa[lay.ctx_rows:], bl, n), cos, sin)
    kl = _rope_tail(r4(ka[lay.ctx_rows:], bl, n), cos, sin)
    vl = r4(va[lay.ctx_rows:], bl, n)
    p = cache_ckv.shape[1]
    kvp = _linear(cache_ckv.reshape(bl * p, kv_rank), w_ukv_b, name="mla_ukv_cache").reshape(bl * p, heads, nope + dv)
    kp, vp = keys(kvp, cache_kr.reshape(bl * p, rope))
    oc = _ctx_attention(qc.reshape(bc * heads, sc, qk_dim), kc.reshape(bc * heads, sc, qk_dim),
                        vc.reshape(bc * heads, sc, dv)).reshape(bc, heads, sc, dv)
    ol = _joint_dense_attention(ql, kl, vl, r4(kp, bl, p), r4(vp, bl, p))
    o = jnp.concatenate([_heads_first(oc).reshape(bc * sc, heads * dv),
                         _heads_first(ol).reshape(bl * n, heads * dv)], axis=0)
    x = _linear(o, w_o.astype(BF16), lay=lay, res=x, gate=gate, name="mla_out")
    return x, ckv[:lay.ctx_rows].reshape(bc, sc, kv_rank), kr[:lay.ctx_rows].reshape(bc, sc, rope)


def _mixer_swa(x, lay, mod, gate, dims, cache_k, cache_v, j, w_qkv, g_q, g_k, sinks, w_o):
    bc, sc, bl, n = dims
    dh = g_q.shape[0]
    heads = sinks.shape[0]
    kvh = (w_qkv.shape[1] // dh - heads) // 2
    grp = heads // kvh
    qkv = _linear(x, w_qkv.astype(BF16), lay=lay, mod=mod, name="swa_qkv")

    def split(rows, b, s, cos=None, sin=None):
        t = rows.reshape(b, s, heads + 2 * kvh, dh)
        q = _rms(t[:, :, :heads], g_q).reshape(b, s, kvh, grp, dh).transpose(0, 2, 3, 1, 4)
        k = _heads_first(_rms(t[:, :, heads:heads + kvh], g_k))
        v = _heads_first(t[:, :, heads + kvh:])
        if cos is not None:
            q, k = _axial_rope(q, cos, sin), _axial_rope(k, cos, sin)
        return q, k, v

    cos, sin = _rope_tables(n, dh)
    qc, kc, vc = split(qkv[:lay.ctx_rows], bc, sc)
    ql, kl, vl = split(qkv[lay.ctx_rows:], bl, n, cos, sin)
    sinks = sinks.astype(F32)
    oc = _swa_ctx_attention(qc, kc, vc, sinks)
    ol = _swa_lat_attention(ql, kl, vl, cache_k, cache_v, j, sinks)
    merge = lambda o, b, s: o.transpose(0, 3, 1, 2, 4).reshape(b * s, heads * dh)
    o = jnp.concatenate([merge(oc, bc, sc), merge(ol, bl, n)], axis=0)
    x = _linear(o, w_o.astype(BF16), lay=lay, res=x, gate=gate, name="swa_out")
    return x, kc, vc


def kernel(x_prompt, x_sample, cache_nat_k, cache_nat_v, state_lru, cache_mla_ckv, cache_mla_krope, cache_swa_k, cache_swa_v, c, c_ctx, norm_mix, norm_ffn, w_mod, b_mod, ffn_w_in, ffn_conv_w, ffn_conv_b, ffn_w_out, nat_w_qkv, nat_q_norm, nat_k_norm, nat_rpb, nat_w_o, lru_w_in, lru_conv_w, lru_conv_b, lru_w_a, lru_b_a, lru_w_i, lru_b_i, lru_lambda, lru_w_out, mla_w_down, mla_q_a_norm, mla_kv_a_norm, mla_w_uq, mla_w_ukv, mla_q_norm, mla_k_norm, mla_w_o, swa_w_qkv, swa_q_norm, swa_k_norm, swa_sinks, swa_w_o):
    bc, sc, d = x_prompt.shape
    bl, n, _ = x_sample.shape
    depth = w_mod.shape[0]
    dims = (bc, sc, bl, n)
    lay = _Layout(bc * sc, sc, bl * n, n)
    x = jnp.concatenate([x_prompt.reshape(bc * sc, d), x_sample.reshape(bl * n, d)], axis=0)

    n_cond = 1 + bl
    cond_rows = -(-n_cond // 8) * 8
    cond = jnp.zeros((cond_rows, d), F32).at[0].set(c_ctx).at[1:n_cond].set(c)
    mods = _modulation(cond, w_mod, b_mod)[:, :n_cond]

    nat_k_l, nat_v_l, lru_l, ckv_l, krope_l, swa_k_l, swa_v_l = [], [], [], [], [], [], []
    for l in range(depth):
        kind, j = l % 4, l // 4
        m6 = [mods[l, :, None, t * d:(t + 1) * d] for t in range(6)]
        mix_mod = (norm_mix[l].reshape(1, d), m6[0], m6[1])
        if kind == 0:
            x, kc, vc = _mixer_nat(x, lay, mix_mod, m6[2], dims, cache_nat_k, cache_nat_v, j, nat_w_qkv[j],
                                   nat_q_norm[j], nat_k_norm[j], nat_rpb[j], nat_w_o[j])
            nat_k_l.append(kc)
            nat_v_l.append(vc)
        elif kind == 1:
            x, st = _mixer_lru(x, lay, mix_mod, m6[2], dims, state_lru[:, j], lru_w_in[j], lru_conv_w[j],
                               lru_conv_b[j], lru_w_a[j], lru_b_a[j], lru_w_i[j], lru_b_i[j], lru_lambda[j],
                               lru_w_out[j])
            lru_l.append(st)
        elif kind == 2:
            x, ckv, kr = _mixer_mla(x, lay, mix_mod, m6[2], dims, cache_mla_ckv[:, j], cache_mla_krope[:, j],
                                    mla_w_down[j], mla_q_a_norm[j], mla_kv_a_norm[j], mla_w_uq[j], mla_w_ukv[j],
                                    mla_q_norm[j], mla_k_norm[j], mla_w_o[j])
            ckv_l.append(ckv)
            krope_l.append(kr)
        else:
            x, kc, vc = _mixer_swa(x, lay, mix_mod, m6[2], dims, cache_swa_k, cache_swa_v, j, swa_w_qkv[j],
                                   swa_q_norm[j], swa_k_norm[j], swa_sinks[j], swa_w_o[j])
            swa_k_l.append(kc)
            swa_v_l.append(vc)
        x = _conv_ffn(x, lay, norm_ffn[l].reshape(1, d), m6[3], m6[4], m6[5], ffn_w_in[l].astype(BF16),
                      ffn_conv_w[l], ffn_conv_b[l], ffn_w_out[l].astype(BF16))

    xp = x[:lay.ctx_rows].reshape(bc, sc, d)
    xs = x[lay.ctx_rows:].reshape(bl, n, d)
    return (xp, xs, jnp.stack(nat_k_l, axis=1), jnp.stack(nat_v_l, axis=1), jnp.stack(lru_l, axis=1),
            jnp.stack(ckv_l, axis=1), jnp.stack(krope_l, axis=1), jnp.stack(swa_k_l, axis=1),
            jnp.stack(swa_v_l, axis=1))
```

```python
import functools

import numpy as np
import jax
import jax.numpy as jnp
from jax import lax
from jax.experimental import pallas as pl
from jax.experimental.pallas import tpu as pltpu

F32 = jnp.float32
BF16 = jnp.bfloat16

GRID_W = 64
NA_WIN_ROWS = 8
NA_WIN_COLS = 16
NA_Q_ROWS = 8
NA_K_ROWS = 16
LRU_C = 8.0
SWA_WINDOW = 128
SWA_BLOCK = 128
ROPE_BASE = 10000.0
ROPE_GROUP = 32
EPS = 1e-6
NEG = -1e30
LANE = 128
SUBLANE = 8
HALO = 16
MIB = 1024 * 1024
ROW_TILES = (1024, 512, 256, 128, 64, 32, 16)


def _cparams(n_axes, vmem_mib):
    return pltpu.CompilerParams(dimension_semantics=("arbitrary",) * n_axes,
                                vmem_limit_bytes=int(min(vmem_mib, 60) * MIB))


def _largest_divisor(n, candidates):
    for c in candidates:
        if n % c == 0:
            return c
    return n


class _Stream:
    def __init__(self, nb, seq, mod0, shared_mod):
        self.nb, self.seq, self.rows, self.mod0, self.shared = nb, seq, nb * seq, mod0, shared_mod
        self.bm = _largest_divisor(self.rows if shared_mod else seq, ROW_TILES)

    def mod_index(self, row0):
        return self.mod0 if self.shared else self.mod0 + row0 // self.seq


def _norm_mod(x, g, shift, scale):
    ms = jnp.mean(x * x, axis=-1, keepdims=True)
    y = (x * lax.rsqrt(ms + EPS)) * g
    return y * (1.0 + scale) + shift


def _rms(x, g):
    return (x * lax.rsqrt(jnp.mean(x * x, axis=-1, keepdims=True) + EPS)) * g


def _modulation_kernel(c_ref, w_ref, b_ref, o_ref):
    c = c_ref[...]
    sc = (c * jax.nn.sigmoid(c)).astype(BF16)
    o_ref[...] = jnp.dot(sc, w_ref[...].astype(BF16), preferred_element_type=F32) + b_ref[...]


def _modulation(cond, w_mod, b_mod):
    depth, d, n = w_mod.shape
    rows = cond.shape[0]
    bn = _largest_divisor(n, (512, 256, 128))
    return pl.pallas_call(
        _modulation_kernel,
        grid=(depth, n // bn),
        in_specs=[pl.BlockSpec((rows, d), lambda l, j: (0, 0)),
                  pl.BlockSpec((None, d, bn), lambda l, j: (l, 0, j)),
                  pl.BlockSpec((None, 1, bn), lambda l, j: (l, 0, j))],
        out_specs=pl.BlockSpec((None, rows, bn), lambda l, j: (l, 0, j)),
        out_shape=jax.ShapeDtypeStruct((depth, rows, n), F32),
        compiler_params=_cparams(2, 32),
        name="modulation",
    )(cond, w_mod, b_mod.reshape(depth, 1, n))


def _rope_tables(n_tokens, rot_dim, lead, width):
    t = jnp.arange(n_tokens)
    row = (t // GRID_W).astype(F32)
    col = (t % GRID_W).astype(F32)
    half = rot_dim // 2
    inv = ROPE_BASE ** (-jnp.arange(0, half, 2, dtype=F32) / half)
    ar = row[:, None] * inv
    ac = col[:, None] * inv
    ang = jnp.concatenate([ar, ar, ac, ac], axis=-1)
    cos, sin = jnp.cos(ang), jnp.sin(ang)
    first = (np.arange(rot_dim) % ROPE_GROUP) < ROPE_GROUP // 2
    sin_a = jnp.where(first, -sin, 0.0)
    sin_b = jnp.where(first, 0.0, sin)
    pad = ((0, 0), (lead, width - lead - rot_dim))
    return (jnp.pad(cos, pad, constant_values=1.0), jnp.pad(sin_a, pad), jnp.pad(sin_b, pad))


def _rope_apply(y, cos, sin_a, sin_b):
    shift = ROPE_GROUP // 2
    return y * cos + pltpu.roll(y, LANE - shift, 1) * sin_a + pltpu.roll(y, shift, 1) * sin_b


def _fill_lhs(x_ref, xs_ref, xn_ref, prologue, g_ref, sh_ref, sc_ref, row_chunk):
    bm = x_ref.shape[0]

    def chunk(r, carry):
        rows = pl.ds(pl.multiple_of(r * row_chunk, row_chunk), row_chunk)
        x = x_ref[rows, :].astype(F32)
        if prologue == "norm_mod":
            x = _norm_mod(x, g_ref[...], sh_ref[...], sc_ref[...])
        elif prologue == "norm":
            x = _rms(x, g_ref[...])
        if xn_ref is not None:
            xn_ref[rows, :] = x
        xs_ref[rows, :] = x.astype(BF16)
        return carry
    lax.fori_loop(0, bm // row_chunk, chunk, 0)


def _head_norm_store(acc, o_ref, hg_ref, tabs, head_w, norm_div):
    bn = acc.shape[1]
    period = tabs[0].shape[1] if tabs is not None else LANE
    lane = lax.broadcasted_iota(jnp.int32, (1, LANE), 1)
    for s0 in range(0, bn, max(head_w, LANE)):
        tiles = [acc[:, s0 + k * LANE:s0 + (k + 1) * LANE] for k in range(max(head_w, LANE) // LANE)]
        if head_w >= LANE:
            ssq = None
            for y in tiles:
                part = jnp.sum(y * y, axis=-1, keepdims=True)
                ssq = part if ssq is None else ssq + part
            inv = lax.rsqrt(ssq / norm_div + EPS)
        else:
            y2 = tiles[0] * tiles[0]
            low = lane < head_w
            s_lo = jnp.sum(jnp.where(low, y2, 0.0), axis=-1, keepdims=True)
            s_hi = jnp.sum(jnp.where(low, 0.0, y2), axis=-1, keepdims=True)
            inv = jnp.where(low, lax.rsqrt(s_lo / norm_div + EPS), lax.rsqrt(s_hi / norm_div + EPS))
        for k, y in enumerate(tiles):
            c0 = s0 + k * LANE
            y = (y * inv) * hg_ref[:, c0:c0 + LANE]
            if tabs is not None:
                t0 = c0 % period
                y = _rope_apply(y, *(t[:, t0:t0 + LANE] for t in tabs))
            o_ref[:, c0:c0 + LANE] = y.astype(o_ref.dtype)


def _proj_kernel(*refs, prologue, emit_xn, epilogue, head_w, norm_div, norm_cols, rope, row_chunk):
    it = iter(refs)
    x_ref = next(it)
    g_ref = next(it) if prologue is not None else None
    sh_ref, sc_ref = (next(it), next(it)) if prologue == "norm_mod" else (None, None)
    w_ref = next(it)
    if epilogue == "res":
        res_ref, gate_ref = next(it), next(it)
    if epilogue == "heads":
        hg_ref = next(it)
        tabs = (next(it), next(it), next(it)) if rope else None
    o_ref = next(it)
    xn_ref = next(it) if emit_xn else None
    xs_ref = next(it)
    j = pl.program_id(1)
    bn = o_ref.shape[1]

    @pl.when(j == 0)
    def _():
        _fill_lhs(x_ref, xs_ref, xn_ref, prologue, g_ref, sh_ref, sc_ref, row_chunk)

    acc = jnp.dot(xs_ref[...], w_ref[...], preferred_element_type=F32)
    if epilogue == "res":
        o_ref[...] = res_ref[...] + gate_ref[...] * acc
    elif epilogue == "heads":
        @pl.when(j * bn < norm_cols)
        def _():
            _head_norm_store(acc, o_ref, hg_ref, tabs, head_w, norm_div)

        @pl.when(j * bn >= norm_cols)
        def _():
            o_ref[...] = acc.astype(o_ref.dtype)
    else:
        o_ref[...] = acc.astype(o_ref.dtype)


def _proj(x, w, st, *, x_block=None, norm_g=None, mod=None, res=None, gate=None, heads=None, emit_xn=False,
          out_dtype=F32, bn=None, name="proj"):
    rows = x.shape[0]
    k, n = w.shape
    kx, kidx = x_block if x_block is not None else (x.shape[1], 0)
    assert kx == k and rows == st.rows
    bm = st.bm
    if bn is None:
        bn = _largest_divisor(n, (512, 256, 128))
    prologue = None if norm_g is None else ("norm_mod" if mod is not None else "norm")
    epilogue = "res" if res is not None else ("heads" if heads is not None else None)
    rope = heads is not None and heads.get("tabs") is not None
    mod_idx = lambda i: st.mod_index(i * bm)

    in_specs = [pl.BlockSpec((bm, k), lambda i, j: (i, kidx))]
    args = [x]
    if prologue is not None:
        in_specs.append(pl.BlockSpec((1, k), lambda i, j: (0, 0)))
        args.append(norm_g)
    if prologue == "norm_mod":
        in_specs += [pl.BlockSpec((None, 1, k), lambda i, j: (mod_idx(i), 0, 0))] * 2
        args += list(mod)
    in_specs.append(pl.BlockSpec((k, bn), lambda i, j: (0, j)))
    args.append(w)
    if epilogue == "res":
        in_specs += [pl.BlockSpec((bm, bn), lambda i, j: (i, j)),
                     pl.BlockSpec((None, 1, bn), lambda i, j: (mod_idx(i), 0, j))]
        args += [res, gate]
    head_w = norm_div = norm_cols = 0
    if epilogue == "heads":
        head_w, norm_div, norm_cols = heads["head_w"], heads["norm_div"], heads["norm_cols"]
        assert bn % max(head_w, LANE) == 0 and norm_cols % bn == 0
        in_specs.append(pl.BlockSpec((1, bn), lambda i, j: (0, j)))
        args.append(heads["gains"])
        if rope:
            period = heads["tabs"][0].shape[1]
            assert bn % period == 0 and st.seq % bm == 0
            tiles_per_seq = st.seq // bm
            in_specs += [pl.BlockSpec((bm, period), lambda i, j: (i % tiles_per_seq, 0))] * 3
            args += list(heads["tabs"])
    out_shape = [jax.ShapeDtypeStruct((rows, n), out_dtype)]
    out_specs = [pl.BlockSpec((bm, bn), lambda i, j: (i, j))]
    if emit_xn:
        out_shape.append(jax.ShapeDtypeStruct((rows, k), F32))
        out_specs.append(pl.BlockSpec((bm, k), lambda i, j: (i, 0)))
    xbytes = x.dtype.itemsize
    vmem = (2 * bm * k * xbytes + bm * k * 2 + 2 * k * bn * 2 + 6 * bm * bn * 4
            + (2 * bm * k * 4 if emit_xn else 0)) / MIB + 8
    kern = functools.partial(_proj_kernel, prologue=prologue, emit_xn=emit_xn, epilogue=epilogue, head_w=head_w,
                             norm_div=norm_div, norm_cols=norm_cols, rope=rope, row_chunk=min(bm, 128))
    out = pl.pallas_call(
        kern,
        grid=(rows // bm, n // bn),
        in_specs=in_specs,
        out_specs=out_specs,
        out_shape=out_shape,
        scratch_shapes=[pltpu.VMEM((bm, k), BF16)],
        compiler_params=_cparams(2, vmem),
        name=name,
    )(*args)
    return out if emit_xn else out[0]


def _ffn_kernel(xp_ref, x_ref, xn_ref, g_ref, sh_ref, sc_ref, gate_ref, wa_ref, wb_ref, cw_ref, cb_ref,
                wo_ref, o_ref, h_ref, *, bm, seq, row_chunk):
    i = pl.program_id(0)
    c = pl.program_id(1)
    n_chunks = pl.num_programs(1)

    @pl.when(c == 0)
    def _():
        g, sh, sc = g_ref[...], sh_ref[...], sc_ref[...]
        h_ref[0:HALO, :] = _norm_mod(xp_ref[...], g, sh, sc).astype(BF16)
        h_ref[HALO + bm:, :] = _norm_mod(xn_ref[...], g, sh, sc).astype(BF16)

        def chunk(r, carry):
            src = pl.ds(pl.multiple_of(r * row_chunk, row_chunk), row_chunk)
            dst = pl.ds(pl.multiple_of(HALO + r * row_chunk, HALO), row_chunk)
            h_ref[dst, :] = _norm_mod(x_ref[src, :], g, sh, sc).astype(BF16)
            return carry
        lax.fori_loop(0, bm // row_chunk, chunk, 0)

    ua = jnp.dot(h_ref[...], wa_ref[...], preferred_element_type=F32)
    ub = jnp.dot(h_ref[HALO:HALO + bm, :], wb_ref[...], preferred_element_type=F32)
    n_all = bm + 2 * HALO
    u_prev = pltpu.roll(ua, 1, 0)[HALO:HALO + bm]
    u_next = pltpu.roll(ua, n_all - 1, 0)[HALO:HALO + bm]
    u_mid = ua[HALO:HALO + bm]
    pos = jnp.bitwise_and(i * bm + lax.broadcasted_iota(jnp.int32, (bm, 1), 0), seq - 1)
    u_prev = jnp.where(pos == 0, 0.0, u_prev)
    u_next = jnp.where(pos == seq - 1, 0.0, u_next)
    cw = cw_ref[...]
    a = cb_ref[...] + u_prev * cw[0:1] + u_mid * cw[1:2] + u_next * cw[2:3]
    gated = ((a * jax.nn.sigmoid(a)) * ub).astype(BF16)
    contrib = jnp.dot(gated, wo_ref[...], preferred_element_type=F32)

    @pl.when(c == 0)
    def _():
        o_ref[...] = contrib

    @pl.when(c > 0)
    def _():
        o_ref[...] += contrib

    @pl.when(c == n_chunks - 1)
    def _():
        o_ref[...] = x_ref[...] + gate_ref[...] * o_ref[...]


def _conv_ffn(x, st, g, shift, scale, gate, w_in, conv_w, conv_b, w_out):
    m, d = x.shape
    d_ff = w_out.shape[0]
    bm = min(st.bm, 512)
    ck = _largest_divisor(d_ff, (512, 256, 128))
    n_chunks = d_ff // ck
    n_halo_blocks = m // HALO
    assert st.seq & (st.seq - 1) == 0 and conv_w.shape[0] == 3
    mod_idx = lambda i: st.mod_index(i * bm)
    kern = functools.partial(_ffn_kernel, bm=bm, seq=st.seq, row_chunk=min(bm, 128))
    vmem = (4 * bm * d * 4 + (bm + 2 * HALO) * d * 2 + 6 * d * ck * 2 + 6 * (bm + 2 * HALO) * ck * 4) / MIB + 10
    return pl.pallas_call(
        kern,
        grid=(m // bm, n_chunks),
        in_specs=[
            pl.BlockSpec((HALO, d), lambda i, c: (jnp.maximum(i * (bm // HALO) - 1, 0), 0)),
            pl.BlockSpec((bm, d), lambda i, c: (i, 0)),
            pl.BlockSpec((HALO, d), lambda i, c: (jnp.minimum((i + 1) * (bm // HALO), n_halo_blocks - 1), 0)),
            pl.BlockSpec((1, d), lambda i, c: (0, 0)),
            pl.BlockSpec((None, 1, d), lambda i, c: (mod_idx(i), 0, 0)),
            pl.BlockSpec((None, 1, d), lambda i, c: (mod_idx(i), 0, 0)),
            pl.BlockSpec((None, 1, d), lambda i, c: (mod_idx(i), 0, 0)),
            pl.BlockSpec((d, ck), lambda i, c: (0, c)),
            pl.BlockSpec((d, ck), lambda i, c: (0, n_chunks + c)),
            pl.BlockSpec((conv_w.shape[0], ck), lambda i, c: (0, c)),
            pl.BlockSpec((1, ck), lambda i, c: (0, c)),
            pl.BlockSpec((ck, d), lambda i, c: (c, 0)),
        ],
        out_specs=pl.BlockSpec((bm, d), lambda i, c: (i, 0)),
        out_shape=jax.ShapeDtypeStruct((m, d), F32),
        scratch_shapes=[pltpu.VMEM((bm + 2 * HALO, d), BF16)],
        compiler_params=_cparams(2, vmem),
        name="conv_ffn",
    )(x, x, x, g, shift, scale, gate, w_in, w_in, conv_w, conv_b.reshape(1, d_ff), w_out)


def _qk(q, k):
    return lax.dot_general(q, k, (((1,), (1,)), ((), ())), preferred_element_type=F32)


def _attend(scores, values, sink=None):
    m = None
    for s in scores:
        mi = jnp.max(s, axis=-1, keepdims=True)
        m = mi if m is None else jnp.maximum(m, mi)
    if sink is not None:
        m = jnp.maximum(m, sink)
    es = [jnp.exp(s - m) for s in scores]
    den = None
    for e in es:
        di = jnp.sum(e, axis=-1, keepdims=True)
        den = di if den is None else den + di
    if sink is not None:
        den = den + jnp.exp(sink - m)
    inv = 1.0 / den
    out = None
    for e, v in zip(es, values):
        oi = jnp.dot((e * inv).astype(BF16), v, preferred_element_type=F32)
        out = oi if out is None else out + oi
    return out


def _ctx_attn_kernel(q_ref, k_ref, v_ref, *outs, scale, heads, dq, dv, emit_kv):
    o_ref = outs[0]
    for h in range(heads):
        q = q_ref[:, h * dq:(h + 1) * dq].astype(BF16)
        k = k_ref[:, h * dq:(h + 1) * dq]
        v = v_ref[:, h * dv:(h + 1) * dv]
        if emit_kv:
            outs[1][h] = k.astype(F32)
            outs[2][h] = v.astype(F32)
        o = _attend([_qk(q, k.astype(BF16)) * scale], [v.astype(BF16)])
        o_ref[:, h * dv:(h + 1) * dv] = o.astype(o_ref.dtype)


def _ctx_attention(qm, km, vm, st, *, n_heads, dq, dv, q_col, k_col, v_col, scale, emit_kv=False):
    hb = _largest_divisor(n_heads, (4, 2, 1))
    s = st.seq
    assert q_col % (hb * dq) == 0 and k_col % (hb * dq) == 0 and v_col % (hb * dv) == 0
    qo, ko, vo = q_col // (hb * dq), k_col // (hb * dq), v_col // (hb * dv)
    out_shape = [jax.ShapeDtypeStruct((st.rows, n_heads * dv), BF16)]
    out_specs = [pl.BlockSpec((s, hb * dv), lambda b, g: (b, g))]
    if emit_kv:
        out_shape += [jax.ShapeDtypeStruct((st.nb, n_heads, s, dq), F32),
                      jax.ShapeDtypeStruct((st.nb, n_heads, s, dv), F32)]
        out_specs += [pl.BlockSpec((None, hb, s, dq), lambda b, g: (b, g, 0, 0)),
                      pl.BlockSpec((None, hb, s, dv), lambda b, g: (b, g, 0, 0))]
    out = pl.pallas_call(
        functools.partial(_ctx_attn_kernel, scale=scale, heads=hb, dq=dq, dv=dv, emit_kv=emit_kv),
        grid=(st.nb, n_heads // hb),
        in_specs=[pl.BlockSpec((s, hb * dq), lambda b, g: (b, qo + g)),
                  pl.BlockSpec((s, hb * dq), lambda b, g: (b, ko + g)),
                  pl.BlockSpec((s, hb * dv), lambda b, g: (b, vo + g))],
        out_specs=out_specs,
        out_shape=out_shape,
        compiler_params=_cparams(2, 32),
        name="ctx_attention",
    )(qm, km, vm)
    return out if emit_kv else out[0]


def _nat_kernel(q_ref, k_ref, v_ref, kc_ref, vc_ref, bias_ref, o_ref, *, scale, key_rows, rows):
    i = pl.program_id(2)
    n_keys = key_rows * GRID_W
    first_row = jnp.clip(i * NA_Q_ROWS - NA_WIN_ROWS // 2, 0, rows - key_rows)
    start = pl.multiple_of(first_row * GRID_W, GRID_W * 4)
    q = q_ref[...]
    k = k_ref[pl.ds(start, n_keys), :]
    v = v_ref[pl.ds(start, n_keys), :]
    s_loc = _qk(q, k) * scale + bias_ref[...]
    s_ctx = _qk(q, kc_ref[...].astype(BF16)) * scale
    o_ref[...] = _attend([s_loc, s_ctx], [v, vc_ref[...].astype(BF16)]).astype(o_ref.dtype)


def _nat_bias(rpb, rows):
    n_blocks = rows // NA_Q_ROWS
    key_rows = min(NA_K_ROWS, rows)
    wr = min(NA_WIN_ROWS, rows)
    reps = [0, min(1, n_blocks - 1), n_blocks - 1]
    heads = rpb.shape[0]
    nq, nk = NA_Q_ROWS * GRID_W, key_rows * GRID_W
    shape = (NA_Q_ROWS, GRID_W, key_rows, GRID_W)
    qc = np.arange(GRID_W)
    cstart = np.clip(qc - NA_WIN_COLS // 2, 0, GRID_W - NA_WIN_COLS)
    col_ok = (qc[None, :] >= cstart[:, None]) & (qc[None, :] < cstart[:, None] + NA_WIN_COLS)
    rp = jnp.pad(rpb.astype(F32), ((0, 0), (key_rows, key_rows), (GRID_W - NA_WIN_COLS, GRID_W - NA_WIN_COLS)))
    row_slabs, mask_l = [], []
    for i in reps:
        ks = int(np.clip(i * NA_Q_ROWS - NA_WIN_ROWS // 2, 0, rows - key_rows))
        r = i * NA_Q_ROWS + np.arange(NA_Q_ROWS)
        rs = np.clip(r - wr // 2, 0, rows - wr)
        kr = ks + np.arange(key_rows)
        row_ok = (kr[None, :] >= rs[:, None]) & (kr[None, :] < rs[:, None] + wr)
        for rq in range(NA_Q_ROWS):
            first = ks - int(r[rq]) + NA_WIN_ROWS - 1 + key_rows
            assert 0 <= first and first + key_rows <= rp.shape[1]
            row_slabs.append(rp[:, first:first + key_rows, :])
        mask_l.append(np.broadcast_to(row_ok[:, None, :, None] & col_ok[None, :, None, :], shape).reshape(nq, nk))
    slab = jnp.stack(row_slabs, axis=1).reshape(heads, len(reps), NA_Q_ROWS, key_rows, 2 * GRID_W - 1)
    toep = jnp.stack([slab[..., GRID_W - 1 - c:2 * GRID_W - 1 - c] for c in range(GRID_W)], axis=3)
    bias = toep.reshape(heads, len(reps), nq, nk)
    return jnp.where(jnp.asarray(np.stack(mask_l))[None], bias, NEG)


def _nat_attention(qkv, st, cache_k, cache_v, j, rpb, dh):
    heads = rpb.shape[0]
    n = st.seq
    p = cache_k.shape[3]
    rows = n // GRID_W
    assert rows % NA_Q_ROWS == 0 and rows >= NA_K_ROWS and dh % LANE == 0
    n_blocks = rows // NA_Q_ROWS
    key_rows = min(NA_K_ROWS, rows)
    nq, nk = NA_Q_ROWS * GRID_W, key_rows * GRID_W
    bias = _nat_bias(rpb, rows)
    btype = lambda i: jnp.where(i == 0, 0, jnp.where(i == n_blocks - 1, 2, 1))
    kern = functools.partial(_nat_kernel, scale=dh ** -0.5, key_rows=key_rows, rows=rows)
    return pl.pallas_call(
        kern,
        grid=(st.nb, heads, n_blocks),
        in_specs=[pl.BlockSpec((nq, dh), lambda b, h, i: (b * n_blocks + i, h)),
                  pl.BlockSpec((n, dh), lambda b, h, i: (b, heads + h)),
                  pl.BlockSpec((n, dh), lambda b, h, i: (b, 2 * heads + h)),
                  pl.BlockSpec((None, None, None, p, dh), lambda b, h, i: (b, j, h, 0, 0)),
                  pl.BlockSpec((None, None, None, p, dh), lambda b, h, i: (b, j, h, 0, 0)),
                  pl.BlockSpec((None, None, nq, nk), lambda b, h, i: (h, btype(i), 0, 0))],
        out_specs=pl.BlockSpec((nq, dh), lambda b, h, i: (b * n_blocks + i, h)),
        out_shape=jax.ShapeDtypeStruct((st.rows, heads * dh), BF16),
        compiler_params=_cparams(3, 40),
        name="nat_attention",
    )(qkv, qkv, qkv, cache_k, cache_v, bias)


def _joint_dense_kernel(q_ref, k_ref, v_ref, kc_ref, vc_ref, o_ref, *, scale, sub):
    k, v, kc, vc = k_ref[...], v_ref[...], kc_ref[...], vc_ref[...]
    for r in range(0, q_ref.shape[0], sub):
        q = q_ref[r:r + sub, :]
        o = _attend([_qk(q, k) * scale, _qk(q, kc) * scale], [v, vc])
        o_ref[r:r + sub, :] = o.astype(o_ref.dtype)


def _joint_dense_attention(qm, km, vm, kcm, vcm, st, p, *, n_heads, dq, dv, scale):
    n = st.seq
    bq = _largest_divisor(n, (256, 128, 64, 32, 16))
    nqb = n // bq
    return pl.pallas_call(
        functools.partial(_joint_dense_kernel, scale=scale, sub=min(bq, 128)),
        grid=(st.nb, n_heads, nqb),
        in_specs=[pl.BlockSpec((bq, dq), lambda b, h, i: (b * nqb + i, h)),
                  pl.BlockSpec((n, dq), lambda b, h, i: (b, h)),
                  pl.BlockSpec((n, dv), lambda b, h, i: (b, h)),
                  pl.BlockSpec((p, dq), lambda b, h, i: (b, h)),
                  pl.BlockSpec((p, dv), lambda b, h, i: (b, h))],
        out_specs=pl.BlockSpec((bq, dv), lambda b, h, i: (b * nqb + i, h)),
        out_shape=jax.ShapeDtypeStruct((st.rows, n_heads * dv), BF16),
        compiler_params=_cparams(3, 48),
        name="mla_attention",
    )(qm, km, vm, kcm, vcm)


def _swa_step(sinks_ref, pair, q_ref, k, v, kc, vc, o_ref, *, dh, groups, scale, local_mask):
    kv_per_step = LANE // dh
    for s in range(kv_per_step):
        lanes = slice(s * dh, (s + 1) * dh)
        ks, vs = k[:, lanes].astype(BF16), v[:, lanes].astype(BF16)
        if kc is not None:
            kcs, vcs = kc[s].astype(BF16), vc[s].astype(BF16)
        outs = []
        for g in range(groups):
            c0 = (s * groups + g) * dh
            q = q_ref[:, c0:c0 + dh].astype(BF16)
            sink = sinks_ref[(pair * kv_per_step + s) * groups + g]
            s_loc = _qk(q, ks) * scale
            if local_mask is not None:
                s_loc = jnp.where(local_mask, s_loc, NEG)
            if kc is not None:
                outs.append(_attend([s_loc, _qk(q, kcs) * scale], [vs, vcs], sink))
            else:
                outs.append(_attend([s_loc], [vs], sink))
        for g in range(0, groups, LANE // dh):
            c0 = (s * groups + g) * dh
            o_ref[:, c0:c0 + LANE] = jnp.concatenate(outs[g:g + LANE // dh], axis=-1).astype(o_ref.dtype)


def _swa_ctx_kernel(sinks_ref, q_ref, k_ref, v_ref, o_ref, ko_ref, vo_ref, *, dh, groups, scale):
    pair = pl.program_id(1)
    k, v = k_ref[...], v_ref[...]
    for s in range(LANE // dh):
        ko_ref[s] = k[:, s * dh:(s + 1) * dh].astype(F32)
        vo_ref[s] = v[:, s * dh:(s + 1) * dh].astype(F32)
    _swa_step(sinks_ref, pair, q_ref, k, v, None, None, o_ref, dh=dh, groups=groups, scale=scale, local_mask=None)


def _swa_lat_kernel(sinks_ref, q_ref, k_ref, v_ref, kc_ref, vc_ref, o_ref, *, dh, groups, scale, n):
    pair = pl.program_id(1)
    blk = pl.program_id(2)
    n_keys = min(3 * SWA_BLOCK, n)
    start = pl.multiple_of(jnp.clip((blk - 1) * SWA_BLOCK, 0, n - n_keys), SWA_BLOCK)
    k = k_ref[pl.ds(start, n_keys), :]
    v = v_ref[pl.ds(start, n_keys), :]
    qpos = blk * SWA_BLOCK + lax.broadcasted_iota(jnp.int32, (SWA_BLOCK, 1), 0)
    kpos = start + lax.broadcasted_iota(jnp.int32, (1, n_keys), 1)
    mask = jnp.abs(qpos - kpos) <= SWA_WINDOW
    _swa_step(sinks_ref, pair, q_ref, k, v, kc_ref, vc_ref, o_ref, dh=dh, groups=groups, scale=scale,
              local_mask=mask)


def _swa_attention(qkv, st, sinks, *, heads, kvh, dh, cache=None):
    groups = heads // kvh
    kv_per_step = LANE // dh
    assert LANE % dh == 0 and kvh % kv_per_step == 0 and groups % kv_per_step == 0
    pairs = kvh // kv_per_step
    qw = kv_per_step * groups * dh
    k_blk = heads * dh // LANE
    v_blk = (heads + kvh) * dh // LANE
    n = st.seq
    common = dict(dh=dh, groups=groups, scale=dh ** -0.5)
    smem = pl.BlockSpec(memory_space=pltpu.SMEM)
    if cache is None:
        out = pl.pallas_call(
            functools.partial(_swa_ctx_kernel, **common),
            grid=(st.nb, pairs),
            in_specs=[smem,
                      pl.BlockSpec((n, qw), lambda b, c: (b, c)),
                      pl.BlockSpec((n, LANE), lambda b, c: (b, k_blk + c)),
                      pl.BlockSpec((n, LANE), lambda b, c: (b, v_blk + c))],
            out_specs=[pl.BlockSpec((n, qw), lambda b, c: (b, c)),
                       pl.BlockSpec((None, kv_per_step, n, dh), lambda b, c: (b, c, 0, 0)),
                       pl.BlockSpec((None, kv_per_step, n, dh), lambda b, c: (b, c, 0, 0))],
            out_shape=[jax.ShapeDtypeStruct((st.rows, heads * dh), BF16),
                       jax.ShapeDtypeStruct((st.nb, kvh, n, dh), F32),
                       jax.ShapeDtypeStruct((st.nb, kvh, n, dh), F32)],
            compiler_params=_cparams(2, 32),
            name="swa_ctx_attention",
        )(sinks, qkv, qkv, qkv)
        return out
    cache_k, cache_v, j = cache
    p = cache_k.shape[3]
    nblk = n // SWA_BLOCK
    assert n % SWA_BLOCK == 0
    return pl.pallas_call(
        functools.partial(_swa_lat_kernel, n=n, **common),
        grid=(st.nb, pairs, nblk),
        in_specs=[smem,
                  pl.BlockSpec((SWA_BLOCK, qw), lambda b, c, i: (b * nblk + i, c)),
                  pl.BlockSpec((n, LANE), lambda b, c, i: (b, k_blk + c)),
                  pl.BlockSpec((n, LANE), lambda b, c, i: (b, v_blk + c)),
                  pl.BlockSpec((None, None, kv_per_step, p, dh), lambda b, c, i: (b, j, c, 0, 0)),
                  pl.BlockSpec((None, None, kv_per_step, p, dh), lambda b, c, i: (b, j, c, 0, 0))],
        out_specs=pl.BlockSpec((SWA_BLOCK, qw), lambda b, c, i: (b * nblk + i, c)),
        out_shape=jax.ShapeDtypeStruct((st.rows, heads * dh), BF16),
        compiler_params=_cparams(3, 32),
        name="swa_attention",
    )(sinks, qkv, qkv, qkv, cache_k, cache_v)


def _mla_kv_kernel(*refs, norm, emit_xn, rope, row_chunk, norm_div):
    it = iter(refs)
    x_ref = next(it)
    g_ref = next(it) if norm else None
    w_ref, kr_ref, g1_ref, g2_ref = next(it), next(it), next(it), next(it)
    tabs = (next(it), next(it), next(it)) if rope else None
    k_ref, v_ref = next(it), next(it)
    xn_ref = next(it) if emit_xn else None
    xs_ref = next(it)

    @pl.when(pl.program_id(1) == 0)
    def _():
        _fill_lhs(x_ref, xs_ref, xn_ref, "norm" if norm else None, g_ref, None, None, row_chunk)

    acc = jnp.dot(xs_ref[...], w_ref[...], preferred_element_type=F32)
    nope = acc[:, :LANE]
    kr = kr_ref[...]
    ssq = jnp.sum(nope * nope, axis=-1, keepdims=True) + jnp.sum(kr * kr, axis=-1, keepdims=True)
    inv = lax.rsqrt(ssq / norm_div + EPS)
    k_rot = (kr * inv) * g2_ref[...]
    if rope:
        k_rot = _rope_apply(k_rot, *(t[...] for t in tabs))
    k_ref[:, :LANE] = ((nope * inv) * g1_ref[...]).astype(k_ref.dtype)
    k_ref[:, LANE:] = k_rot.astype(k_ref.dtype)
    v_ref[...] = acc[:, LANE:].astype(v_ref.dtype)


def _mla_kv(x, x_block, w_ukv, kr, kr_block, g_kva, g1, g2, tabs, st, *, n_heads, norm_div, emit_xn, name):
    rows = x.shape[0]
    k, n = w_ukv.shape
    head_n = n // n_heads
    assert head_n == 2 * LANE, "nope and value widths must both be one lane tile"
    kx, kidx = x_block
    krw, kridx = kr_block
    assert kx == k and krw == LANE
    bm = st.bm
    norm = g_kva is not None
    rope = tabs is not None
    in_specs = [pl.BlockSpec((bm, k), lambda i, h: (i, kidx))]
    args = [x]
    if norm:
        in_specs.append(pl.BlockSpec((1, k), lambda i, h: (0, 0)))
        args.append(g_kva)
    in_specs += [pl.BlockSpec((k, head_n), lambda i, h: (0, h)),
                 pl.BlockSpec((bm, LANE), lambda i, h: (i, kridx)),
                 pl.BlockSpec((1, LANE), lambda i, h: (0, 0)),
                 pl.BlockSpec((1, LANE), lambda i, h: (0, 0))]
    args += [w_ukv, kr, g1, g2]
    if rope:
        tiles_per_seq = st.seq // bm
        in_specs += [pl.BlockSpec((bm, LANE), lambda i, h: (i % tiles_per_seq, 0))] * 3
        args += list(tabs)
    out_shape = [jax.ShapeDtypeStruct((rows, n_heads * 2 * LANE), BF16),
                 jax.ShapeDtypeStruct((rows, n_heads * LANE), BF16)]
    out_specs = [pl.BlockSpec((bm, 2 * LANE), lambda i, h: (i, h)),
                 pl.BlockSpec((bm, LANE), lambda i, h: (i, h))]
    if emit_xn:
        out_shape.append(jax.ShapeDtypeStruct((rows, k), F32))
        out_specs.append(pl.BlockSpec((bm, k), lambda i, h: (i, 0)))
    kern = functools.partial(_mla_kv_kernel, norm=norm, emit_xn=emit_xn, rope=rope, row_chunk=min(bm, 128),
                             norm_div=norm_div)
    return pl.pallas_call(
        kern,
        grid=(rows // bm, n_heads),
        in_specs=in_specs,
        out_specs=out_specs,
        out_shape=out_shape,
        scratch_shapes=[pltpu.VMEM((bm, k), BF16)],
        compiler_params=_cparams(2, 40),
        name=name,
    )(*args)


def _band_plan(width, block):
    n_tiles = width // LANE
    lo = [((t * LANE) // block) * block for t in range(n_tiles)]
    hi = [(((t + 1) * LANE - 1) // block + 1) * block for t in range(n_tiles)]
    start = [(l // LANE) * LANE for l in lo]
    kb = max(-(-(h - s) // LANE) * LANE for h, s in zip(hi, start))
    kb = min(kb, width)
    start = [min(s, width - kb) for s in start]
    return start, kb


def _band_weights(w, width, block, start, kb):
    n_tiles = width // LANE
    wb = w.astype(BF16)
    tiles = []
    for t in range(n_tiles):
        pieces = []
        col = t * LANE
        while col < (t + 1) * LANE:
            blk = col // block
            col_end = min((blk + 1) * block, (t + 1) * LANE)
            sub = wb[blk, :, col - blk * block:col_end - blk * block]
            top = blk * block - start[t]
            pieces.append(jnp.pad(sub, ((top, kb - top - block), (0, 0))))
            col = col_end
        tiles.append(jnp.concatenate(pieces, axis=1))
    return jnp.stack(tiles)


def _expm1_neg(x):
    return -jnp.tanh(0.5 * x) * (jnp.exp(x) + 1.0)


def _gelu_tanh(x):
    cdf = 0.5 * (1.0 + jnp.tanh(np.float32(np.sqrt(2.0 / np.pi)) * (x + 0.044715 * (x * x * x))))
    return x * cdf


def _lru_pass_kernel(*refs, reverse, starts, kb, bt, seq, nb, taps):
    it = iter(refs)
    xp_ref, x_ref, xn_ref, cw_ref, cb_ref = next(it), next(it), next(it), next(it), next(it)
    wa_ref, wi_ref, ba_ref, bi_ref, lam_ref, h0_ref = (next(it) for _ in range(6))
    hsf_ref, gate_ref = (next(it), next(it)) if reverse else (None, None)
    out_ref, ht_ref = next(it), next(it)
    xc_s, a_s, bx_s, carry = next(it), next(it), next(it), next(it)

    step = pl.program_id(0)
    n_steps = pl.num_programs(0)
    tile = (n_steps - 1 - step) if reverse else step
    n_tiles = len(starts)
    rows = nb * bt
    ext = bt + 2 * SUBLANE
    left = taps // 2

    @pl.when(step == 0)
    def _():
        carry[...] = h0_ref[...]

    pos = tile * bt + lax.broadcasted_iota(jnp.int32, (1, bt, 1), 1)
    for t in range(n_tiles):
        lanes = slice(t * LANE, (t + 1) * LANE)
        full = jnp.concatenate([xp_ref[:, :, lanes], x_ref[:, :, lanes], xn_ref[:, :, lanes]], axis=1)
        flat = full.reshape(nb * ext, LANE)
        acc = jnp.broadcast_to(cb_ref[:, lanes], (nb, bt, LANE))
        for k in range(taps):
            off = k - left
            shifted = flat if off == 0 else pltpu.roll(flat, (-off) % (nb * ext), 0)
            shifted = shifted.reshape(nb, ext, LANE)[:, SUBLANE:SUBLANE + bt, :]
            if off != 0:
                shifted = jnp.where((pos + off >= 0) & (pos + off < seq), shifted, 0.0)
            acc = acc + shifted * cw_ref[k:k + 1, lanes]
        xc_s[:, lanes] = acc.reshape(rows, LANE)

    neg_lam = -lam_ref[...]
    softplus = jnp.maximum(neg_lam, 0.0) + jnp.log1p(jnp.exp(-jnp.abs(neg_lam)))
    for t in range(n_tiles):
        lanes = slice(t * LANE, (t + 1) * LANE)
        xw = xc_s[:, starts[t]:starts[t] + kb].astype(BF16)
        r = jax.nn.sigmoid(jnp.dot(xw, wa_ref[t], preferred_element_type=F32) + ba_ref[:, lanes])
        ig = jax.nn.sigmoid(jnp.dot(xw, wi_ref[t], preferred_element_type=F32) + bi_ref[:, lanes])
        log_a = -LRU_C * r * softplus[:, lanes]
        a_s[t] = jnp.exp(log_a)
        bx_s[t] = jnp.sqrt(_expm1_neg(2.0 * log_a)) * (ig * xc_s[:, lanes])

    h = [carry[:, t * LANE:(t + 1) * LANE] for t in range(n_tiles)]
    for s in range(bt):
        ts = (bt - 1 - s) if reverse else s
        slab = pl.ds(ts, nb, stride=bt)
        for t in range(n_tiles):
            h[t] = a_s[t, slab, :] * h[t] + bx_s[t, slab, :]
            a_s[t, slab, :] = h[t]
    for t in range(n_tiles):
        lanes = slice(t * LANE, (t + 1) * LANE)
        carry[:, lanes] = h[t]
        ht_ref[:, lanes] = h[t]
        hs = a_s[t].reshape(nb, bt, LANE)
        if reverse:
            out_ref[:, :, lanes] = (_gelu_tanh(gate_ref[:, :, lanes]) * (hsf_ref[:, :, lanes] + hs)).astype(out_ref.dtype)
        else:
            out_ref[:, :, lanes] = hs


def _lru_pass(u, st, conv_w, conv_b, wa, wi, b_a, b_i, lam, h0, starts, kb, *, reverse, hs_fwd=None):
    c = conv_w.shape[1]
    nb, seq = st.nb, st.seq
    bt = max(SUBLANE, min(256 // nb, seq))
    assert seq % bt == 0 and bt % SUBLANE == 0 and c % LANE == 0
    nt = seq // bt
    n_tiles = c // LANE
    u3 = u.reshape(nb, seq, 2 * c)
    tmap = (lambda s: nt - 1 - s) if reverse else (lambda s: s)
    per8 = bt // SUBLANE
    full = lambda *shape: pl.BlockSpec(shape, lambda s: (0,) * len(shape))
    in_specs = [pl.BlockSpec((nb, SUBLANE, c), lambda s: (0, jnp.maximum(tmap(s) * per8 - 1, 0), 0)),
                pl.BlockSpec((nb, bt, c), lambda s: (0, tmap(s), 0)),
                pl.BlockSpec((nb, SUBLANE, c), lambda s: (0, jnp.minimum((tmap(s) + 1) * per8, seq // SUBLANE - 1), 0)),
                full(conv_w.shape[0], c), full(1, c),
                full(n_tiles, kb, LANE), full(n_tiles, kb, LANE), full(1, c), full(1, c), full(1, c), full(nb, c)]
    args = [u3, u3, u3, conv_w, conv_b.reshape(1, c), wa, wi, b_a.reshape(1, c), b_i.reshape(1, c),
            lam.reshape(1, c), h0]
    if reverse:
        in_specs += [pl.BlockSpec((nb, bt, c), lambda s: (0, tmap(s), 0)),
                     pl.BlockSpec((nb, bt, c), lambda s: (0, tmap(s), 1))]
        args += [hs_fwd, u3]
    kern = functools.partial(_lru_pass_kernel, reverse=reverse, starts=tuple(starts), kb=kb, bt=bt, seq=seq, nb=nb,
                             taps=conv_w.shape[0])
    blk = nb * bt * c * 4 / MIB
    vmem = (2 + 2 + 3 + (4 if reverse else 0)) * blk + 4 * n_tiles * kb * LANE * 2 / MIB + 12
    return pl.pallas_call(
        kern,
        grid=(nt,),
        in_specs=in_specs,
        out_specs=[pl.BlockSpec((nb, bt, c), lambda s: (0, tmap(s), 0)),
                   pl.BlockSpec((nb, c), lambda s: (0, 0))],
        out_shape=[jax.ShapeDtypeStruct((nb, seq, c), BF16 if reverse else F32),
                   jax.ShapeDtypeStruct((nb, c), F32)],
        scratch_shapes=[pltpu.VMEM((nb * bt, c), F32), pltpu.VMEM((n_tiles, nb * bt, LANE), F32),
                        pltpu.VMEM((n_tiles, nb * bt, LANE), F32), pltpu.VMEM((nb, c), F32)],
        compiler_params=_cparams(1, vmem),
        name="lru_bwd" if reverse else "lru_fwd",
    )(*args)


def _mixer_nat(xs, streams, mods, cache_k, cache_v, j, w_qkv, g_mix, g_q, g_k, rpb, w_o):
    heads, dh = rpb.shape[0], g_q.shape[0]
    w_qkv, w_o = w_qkv.astype(BF16), w_o.astype(BF16)
    gains = jnp.concatenate([jnp.tile(g_q, heads), jnp.tile(g_k, heads), jnp.ones((heads * dh,), F32)])[None]
    spec = dict(head_w=dh, norm_div=dh, norm_cols=2 * heads * dh, gains=gains)
    new_x, extra = [], None
    for x, st in zip(xs, streams):
        latent = not st.shared
        qkv = _proj(x, w_qkv, st, norm_g=g_mix, mod=(mods[0], mods[1]), heads=spec,
                    out_dtype=BF16 if latent else F32, name="nat_qkv")
        if latent:
            o = _nat_attention(qkv, st, cache_k, cache_v, j, rpb, dh)
        else:
            o, kc, vc = _ctx_attention(qkv, qkv, qkv, st, n_heads=heads, dq=dh, dv=dh, q_col=0, k_col=heads * dh,
                                       v_col=2 * heads * dh, scale=dh ** -0.5, emit_kv=True)
            extra = (kc, vc)
        new_x.append(_proj(o, w_o, st, res=x, gate=mods[2], name="nat_out"))
    return new_x, extra


def _mixer_lru(xs, streams, mods, state, w_in, g_mix, conv_w, conv_b, w_a, b_a, w_i, b_i, lam, w_out):
    c = conv_w.shape[1]
    block = w_a.shape[-1]
    w_in, w_out = w_in.astype(BF16), w_out.astype(BF16)
    starts, kb = _band_plan(c, block)
    wa = [_band_weights(w_a[d], c, block, starts, kb) for d in range(2)]
    wi = [_band_weights(w_i[d], c, block, starts, kb) for d in range(2)]
    new_x, st_out = [], None
    for x, st in zip(xs, streams):
        latent = not st.shared
        h0 = state.astype(F32) if latent else jnp.zeros((st.nb, 2, c), F32)
        u = _proj(x, w_in, st, norm_g=g_mix, mod=(mods[0], mods[1]), name="lru_in")
        hs_f, t_f = _lru_pass(u, st, conv_w, conv_b, wa[0], wi[0], b_a[0], b_i[0], lam[0], h0[:, 0], starts, kb,
                              reverse=False)
        y, t_b = _lru_pass(u, st, conv_w, conv_b, wa[1], wi[1], b_a[1], b_i[1], lam[1], h0[:, 1], starts, kb,
                           reverse=True, hs_fwd=hs_f)
        if not latent:
            st_out = jnp.stack([t_f, t_b], axis=1)
        new_x.append(_proj(y.reshape(st.rows, c), w_out, st, res=x, gate=mods[2], name="lru_out"))
    return new_x, st_out


def _mixer_mla(xs, streams, mods, cache_ckv, cache_kr, w_down, g_mix, g_qa, g_kva, w_uq, w_ukv, g_q, g_k, w_o):
    d_model = w_down.shape[0]
    q_rank, kv_rank = g_qa.shape[0], g_kva.shape[0]
    qk_dim = g_q.shape[0]
    heads = w_uq.shape[1] // qk_dim
    rope = w_down.shape[1] - q_rank - kv_rank
    nope = qk_dim - rope
    assert nope == LANE and rope <= LANE and kv_rank % LANE == 0 and q_rank % LANE == 0
    head_w = 2 * LANE
    q_pad = -q_rank % kv_rank
    kv_col = q_rank + q_pad
    tail_pad = -(kv_col + kv_rank + rope) % 512
    w_dn = jnp.concatenate([w_down[:, :q_rank], jnp.zeros((d_model, q_pad), F32),
                            w_down[:, q_rank:q_rank + kv_rank], w_down[:, q_rank + kv_rank:],
                            jnp.zeros((d_model, tail_pad), F32)], axis=1).astype(BF16)
    kr_blk = (kv_col + kv_rank) // LANE
    w_q = jnp.pad(w_uq.reshape(q_rank, heads, qk_dim), ((0, 0), (0, 0), (0, head_w - qk_dim)))
    w_q = w_q.reshape(q_rank, heads * head_w).astype(BF16)
    w_ukv, w_o = w_ukv.astype(BF16), w_o.astype(BF16)
    gq = jnp.tile(jnp.pad(g_q, (0, head_w - qk_dim)), heads)[None]
    g1, g2 = g_k[None, :nope], jnp.pad(g_k[nope:], (0, LANE - rope))[None]
    p = cache_ckv.shape[1]
    new_x, extra = [], None
    for x, st in zip(xs, streams):
        latent = not st.shared
        d = _proj(x, w_dn, st, norm_g=g_mix, mod=(mods[0], mods[1]), name="mla_down")
        q_tabs = _rope_tables(st.seq, rope, nope, head_w) if latent else None
        k_tabs = _rope_tables(st.seq, rope, 0, LANE) if latent else None
        q = _proj(d, w_q, st, x_block=(q_rank, 0), norm_g=g_qa[None],
                  heads=dict(head_w=head_w, norm_div=qk_dim, norm_cols=heads * head_w, gains=gq, tabs=q_tabs),
                  out_dtype=BF16, name="mla_uq")
        kv = _mla_kv(d, (kv_rank, kv_col // kv_rank), w_ukv, d, (LANE, kr_blk), g_kva[None], g1, g2, k_tabs, st,
                     n_heads=heads, norm_div=qk_dim, emit_xn=not latent, name="mla_ukv")
        if latent:
            k, v = kv
            cst = _Stream(st.nb, p, 0, True)
            krc = jnp.pad(cache_kr.reshape(st.nb * p, rope), ((0, 0), (0, LANE - rope)))
            kc, vc = _mla_kv(cache_ckv.reshape(st.nb * p, kv_rank), (kv_rank, 0), w_ukv, krc, (LANE, 0), None,
                             g1, g2, None, cst, n_heads=heads, norm_div=qk_dim, emit_xn=False,
                             name="mla_ukv_cache")
            o = _joint_dense_attention(q, k, v, kc, vc, st, p, n_heads=heads, dq=head_w, dv=LANE,
                                       scale=qk_dim ** -0.5)
        else:
            k, v, ckv = kv
            o = _ctx_attention(q, k, v, st, n_heads=heads, dq=head_w, dv=LANE, q_col=0, k_col=0, v_col=0,
                               scale=qk_dim ** -0.5)
            kr_out = d[:, kv_col + kv_rank:kv_col + kv_rank + rope]
            extra = (ckv.reshape(st.nb, st.seq, kv_rank), kr_out.reshape(st.nb, st.seq, rope))
        new_x.append(_proj(o, w_o, st, res=x, gate=mods[2], name="mla_out"))
    return new_x, extra


def _mixer_swa(xs, streams, mods, cache_k, cache_v, j, w_qkv, g_mix, g_q, g_k, sinks, w_o):
    dh = g_q.shape[0]
    heads = sinks.shape[0]
    kvh = (w_qkv.shape[1] // dh - heads) // 2
    w_qkv, w_o = w_qkv.astype(BF16), w_o.astype(BF16)
    gains = jnp.concatenate([jnp.tile(g_q, heads), jnp.tile(g_k, kvh), jnp.ones((kvh * dh,), F32)])[None]
    sinks = sinks.astype(F32)
    new_x, extra = [], None
    for x, st in zip(xs, streams):
        latent = not st.shared
        tabs = _rope_tables(st.seq, dh, 0, dh) if latent else None
        if tabs is not None:
            tabs = tuple(jnp.tile(t, (1, LANE // dh)) for t in tabs)
        spec = dict(head_w=dh, norm_div=dh, norm_cols=(heads + kvh) * dh, gains=gains, tabs=tabs)
        qkv = _proj(x, w_qkv, st, norm_g=g_mix, mod=(mods[0], mods[1]), heads=spec,
                    out_dtype=BF16 if latent else F32, bn=kvh * dh, name="swa_qkv")
        if latent:
            o = _swa_attention(qkv, st, sinks, heads=heads, kvh=kvh, dh=dh, cache=(cache_k, cache_v, j))
        else:
            o, kc, vc = _swa_attention(qkv, st, sinks, heads=heads, kvh=kvh, dh=dh)
            extra = (kc, vc)
        new_x.append(_proj(o, w_o, st, res=x, gate=mods[2], name="swa_out"))
    return new_x, extra


def kernel(x_prompt, x_sample, cache_nat_k, cache_nat_v, state_lru, cache_mla_ckv, cache_mla_krope, cache_swa_k, cache_swa_v, c, c_ctx, norm_mix, norm_ffn, w_mod, b_mod, ffn_w_in, ffn_conv_w, ffn_conv_b, ffn_w_out, nat_w_qkv, nat_q_norm, nat_k_norm, nat_rpb, nat_w_o, lru_w_in, lru_conv_w, lru_conv_b, lru_w_a, lru_b_a, lru_w_i, lru_b_i, lru_lambda, lru_w_out, mla_w_down, mla_q_a_norm, mla_kv_a_norm, mla_w_uq, mla_w_ukv, mla_q_norm, mla_k_norm, mla_w_o, swa_w_qkv, swa_q_norm, swa_k_norm, swa_sinks, swa_w_o):
    bc, sc, d = x_prompt.shape
    bl, n, _ = x_sample.shape
    depth = w_mod.shape[0]
    streams = (_Stream(bc, sc, 0, True), _Stream(bl, n, 1, False))
    xs = [x_prompt.reshape(bc * sc, d), x_sample.reshape(bl * n, d)]

    n_cond = 1 + bl
    cond_rows = -(-n_cond // SUBLANE) * SUBLANE
    cond = jnp.zeros((cond_rows, d), F32).at[0].set(c_ctx).at[1:n_cond].set(c)
    mods = _modulation(cond, w_mod, b_mod)[:, :n_cond]

    nat_k_l, nat_v_l, lru_l, ckv_l, krope_l, swa_k_l, swa_v_l = [], [], [], [], [], [], []
    for l in range(depth):
        kind, j = l % 4, l // 4
        m6 = [mods[l, :, None, t * d:(t + 1) * d] for t in range(6)]
        g_mix = norm_mix[l].reshape(1, d)
        if kind == 0:
            xs, (kc, vc) = _mixer_nat(xs, streams, m6, cache_nat_k, cache_nat_v, j, nat_w_qkv[j], g_mix,
                                      nat_q_norm[j], nat_k_norm[j], nat_rpb[j], nat_w_o[j])
            nat_k_l.append(kc)
            nat_v_l.append(vc)
        elif kind == 1:
            xs, st = _mixer_lru(xs, streams, m6, state_lru[:, j], lru_w_in[j], g_mix, lru_conv_w[j], lru_conv_b[j],
                                lru_w_a[j], lru_b_a[j], lru_w_i[j], lru_b_i[j], lru_lambda[j], lru_w_out[j])
            lru_l.append(st)
        elif kind == 2:
            xs, (ckv, kr) = _mixer_mla(xs, streams, m6, cache_mla_ckv[:, j], cache_mla_krope[:, j], mla_w_down[j],
                                       g_mix, mla_q_a_norm[j], mla_kv_a_norm[j], mla_w_uq[j], mla_w_ukv[j],
                                       mla_q_norm[j], mla_k_norm[j], mla_w_o[j])
            ckv_l.append(ckv)
            krope_l.append(kr)
        else:
            xs, (kc, vc) = _mixer_swa(xs, streams, m6, cache_swa_k, cache_swa_v, j, swa_w_qkv[j], g_mix,
                                      swa_q_norm[j], swa_k_norm[j], swa_sinks[j], swa_w_o[j])
            swa_k_l.append(kc)
            swa_v_l.append(vc)
        w_in, w_out = ffn_w_in[l].astype(BF16), ffn_w_out[l].astype(BF16)
        xs = [_conv_ffn(x, st, norm_ffn[l].reshape(1, d), m6[3], m6[4], m6[5], w_in, ffn_conv_w[l],
                        ffn_conv_b[l], w_out) for x, st in zip(xs, streams)]

    return (xs[0].reshape(bc, sc, d), xs[1].reshape(bl, n, d), jnp.stack(nat_k_l, axis=1),
            jnp.stack(nat_v_l, axis=1), jnp.stack(lru_l, axis=1), jnp.stack(ckv_l, axis=1),
            jnp.stack(krope_l, axis=1), jnp.stack(swa_k_l, axis=1), jnp.stack(swa_v_l, axis=1))
```

```python
import functools

import numpy as np
import jax
import jax.numpy as jnp
from jax import lax
from jax.experimental import pallas as pl
from jax.experimental.pallas import tpu as pltpu

F32 = jnp.float32
BF16 = jnp.bfloat16

GRID_W = 64
NA_WIN_ROWS = 8
NA_WIN_COLS = 16
NA_Q_ROWS = 8
NA_K_ROWS = 16
LRU_C = 8.0
SWA_WINDOW = 128
SWA_BLOCK = 128
ROPE_BASE = 10000.0
ROPE_GROUP = 32
EPS = 1e-6
NEG = -1e30
LOG2E = float(np.log2(np.e))
LANE = 128
SUBLANE = 8
HALO = 16
MIB = 1024 * 1024
ROW_TILES = (1024, 512, 256, 128, 64, 32, 16)


def _cparams(n_axes, vmem_mib):
    return pltpu.CompilerParams(dimension_semantics=("arbitrary",) * n_axes,
                                vmem_limit_bytes=int(min(vmem_mib, 60) * MIB))


def _largest_divisor(n, candidates):
    for c in candidates:
        if n % c == 0:
            return c
    return n


class _Stream:
    def __init__(self, nb, seq, mod0, shared_mod):
        self.nb, self.seq, self.rows, self.mod0, self.shared = nb, seq, nb * seq, mod0, shared_mod
        self.bm = _largest_divisor(self.rows if shared_mod else seq, ROW_TILES)

    def mod_index(self, row0):
        return self.mod0 if self.shared else self.mod0 + row0 // self.seq


def _norm_mod(x, g, shift, scale):
    ms = jnp.mean(x * x, axis=-1, keepdims=True)
    y = (x * lax.rsqrt(ms + EPS)) * g
    return y * (1.0 + scale) + shift


def _rms(x, g):
    return (x * lax.rsqrt(jnp.mean(x * x, axis=-1, keepdims=True) + EPS)) * g


def _modulation_kernel(c_ref, w_ref, b_ref, o_ref):
    c = c_ref[...]
    sc = (c * jax.nn.sigmoid(c)).astype(BF16)
    o_ref[...] = jnp.dot(sc, w_ref[...].astype(BF16), preferred_element_type=F32) + b_ref[...]


def _modulation(cond, w_mod, b_mod):
    depth, d, n = w_mod.shape
    rows = cond.shape[0]
    bn = _largest_divisor(n, (512, 256, 128))
    return pl.pallas_call(
        _modulation_kernel,
        grid=(depth, n // bn),
        in_specs=[pl.BlockSpec((rows, d), lambda l, j: (0, 0)),
                  pl.BlockSpec((None, d, bn), lambda l, j: (l, 0, j)),
                  pl.BlockSpec((None, 1, bn), lambda l, j: (l, 0, j))],
        out_specs=pl.BlockSpec((None, rows, bn), lambda l, j: (l, 0, j)),
        out_shape=jax.ShapeDtypeStruct((depth, rows, n), F32),
        compiler_params=_cparams(2, 32),
        name="modulation",
    )(cond, w_mod, b_mod.reshape(depth, 1, n))


def _rope_tables(n_tokens, rot_dim, lead, width):
    t = jnp.arange(n_tokens)
    row = (t // GRID_W).astype(F32)
    col = (t % GRID_W).astype(F32)
    half = rot_dim // 2
    inv = ROPE_BASE ** (-jnp.arange(0, half, 2, dtype=F32) / half)
    ar = row[:, None] * inv
    ac = col[:, None] * inv
    ang = jnp.concatenate([ar, ar, ac, ac], axis=-1)
    cos, sin = jnp.cos(ang), jnp.sin(ang)
    first = (np.arange(rot_dim) % ROPE_GROUP) < ROPE_GROUP // 2
    sin_a = jnp.where(first, -sin, 0.0)
    sin_b = jnp.where(first, 0.0, sin)
    pad = ((0, 0), (lead, width - lead - rot_dim))
    return (jnp.pad(cos, pad, constant_values=1.0), jnp.pad(sin_a, pad), jnp.pad(sin_b, pad))


def _rope_apply(y, cos, sin_a, sin_b):
    shift = ROPE_GROUP // 2
    return y * cos + pltpu.roll(y, LANE - shift, 1) * sin_a + pltpu.roll(y, shift, 1) * sin_b


def _fill_lhs(x_ref, xs_ref, xn_ref, prologue, g_ref, sh_ref, sc_ref, row_chunk):
    bm = x_ref.shape[0]

    def chunk(r, carry):
        rows = pl.ds(pl.multiple_of(r * row_chunk, row_chunk), row_chunk)
        x = x_ref[rows, :].astype(F32)
        if prologue == "norm_mod":
            x = _norm_mod(x, g_ref[...], sh_ref[...], sc_ref[...])
        elif prologue == "norm":
            x = _rms(x, g_ref[...])
        if xn_ref is not None:
            xn_ref[rows, :] = x
        xs_ref[rows, :] = x.astype(BF16)
        return carry
    lax.fori_loop(0, bm // row_chunk, chunk, 0)


def _head_norm_store(acc, o_ref, hg_ref, tabs, head_w, norm_div):
    bn = acc.shape[1]
    period = tabs[0].shape[1] if tabs is not None else LANE
    lane = lax.broadcasted_iota(jnp.int32, (1, LANE), 1)
    for s0 in range(0, bn, max(head_w, LANE)):
        tiles = [acc[:, s0 + k * LANE:s0 + (k + 1) * LANE] for k in range(max(head_w, LANE) // LANE)]
        if head_w >= LANE:
            ssq = None
            for y in tiles:
                part = jnp.sum(y * y, axis=-1, keepdims=True)
                ssq = part if ssq is None else ssq + part
            inv = lax.rsqrt(ssq / norm_div + EPS)
        else:
            y2 = tiles[0] * tiles[0]
            low = lane < head_w
            s_lo = jnp.sum(jnp.where(low, y2, 0.0), axis=-1, keepdims=True)
            s_hi = jnp.sum(jnp.where(low, 0.0, y2), axis=-1, keepdims=True)
            inv = jnp.where(low, lax.rsqrt(s_lo / norm_div + EPS), lax.rsqrt(s_hi / norm_div + EPS))
        for k, y in enumerate(tiles):
            c0 = s0 + k * LANE
            y = (y * inv) * hg_ref[:, c0:c0 + LANE]
            if tabs is not None:
                t0 = c0 % period
                y = _rope_apply(y, *(t[:, t0:t0 + LANE] for t in tabs))
            o_ref[:, c0:c0 + LANE] = y.astype(o_ref.dtype)


def _proj_kernel(*refs, prologue, emit_xn, epilogue, head_w, norm_div, norm_cols, rope, row_chunk):
    it = iter(refs)
    x_ref = next(it)
    g_ref = next(it) if prologue is not None else None
    sh_ref, sc_ref = (next(it), next(it)) if prologue == "norm_mod" else (None, None)
    w_ref = next(it)
    if epilogue == "res":
        res_ref, gate_ref = next(it), next(it)
    if epilogue == "heads":
        hg_ref = next(it)
        tabs = (next(it), next(it), next(it)) if rope else None
    o_ref = next(it)
    xn_ref = next(it) if emit_xn else None
    xs_ref = next(it)
    j = pl.program_id(1)
    bn = o_ref.shape[1]

    @pl.when(j == 0)
    def _():
        _fill_lhs(x_ref, xs_ref, xn_ref, prologue, g_ref, sh_ref, sc_ref, row_chunk)

    acc = jnp.dot(xs_ref[...], w_ref[...], preferred_element_type=F32)
    if epilogue == "res":
        o_ref[...] = res_ref[...] + gate_ref[...] * acc
    elif epilogue == "heads":
        @pl.when(j * bn < norm_cols)
        def _():
            _head_norm_store(acc, o_ref, hg_ref, tabs, head_w, norm_div)

        @pl.when(j * bn >= norm_cols)
        def _():
            o_ref[...] = acc.astype(o_ref.dtype)
    else:
        o_ref[...] = acc.astype(o_ref.dtype)


def _proj(x, w, st, *, x_block=None, norm_g=None, mod=None, res=None, gate=None, heads=None, emit_xn=False,
          out_dtype=F32, bn=None, x_time_major=False, out_time_major=False, name="proj"):
    k, n = w.shape
    time_major = x_time_major or out_time_major
    bm = min(st.bm, st.seq) if time_major else st.bm
    tiles_per_seq = st.seq // bm if st.seq % bm == 0 else None
    if time_major:
        assert tiles_per_seq is not None and x_block is None
    if x_time_major:
        assert x.shape == (st.seq, st.nb * k)
        kidx = 0
    else:
        kx, kidx = x_block if x_block is not None else (x.shape[1], 0)
        assert kx == k and x.shape[0] == st.rows
    rows = st.rows
    if bn is None:
        bn = _largest_divisor(n, (512, 256, 128))
    prologue = None if norm_g is None else ("norm_mod" if mod is not None else "norm")
    epilogue = "res" if res is not None else ("heads" if heads is not None else None)
    rope = heads is not None and heads.get("tabs") is not None
    mod_idx = lambda i: st.mod_index(i * bm)

    if x_time_major:
        in_specs = [pl.BlockSpec((bm, k), lambda i, j: (i % tiles_per_seq, i // tiles_per_seq))]
    else:
        in_specs = [pl.BlockSpec((bm, k), lambda i, j: (i, kidx))]
    args = [x]
    if prologue is not None:
        in_specs.append(pl.BlockSpec((1, k), lambda i, j: (0, 0)))
        args.append(norm_g)
    if prologue == "norm_mod":
        in_specs += [pl.BlockSpec((None, 1, k), lambda i, j: (mod_idx(i), 0, 0))] * 2
        args += list(mod)
    in_specs.append(pl.BlockSpec((k, bn), lambda i, j: (0, j)))
    args.append(w)
    if epilogue == "res":
        in_specs += [pl.BlockSpec((bm, bn), lambda i, j: (i, j)),
                     pl.BlockSpec((None, 1, bn), lambda i, j: (mod_idx(i), 0, j))]
        args += [res, gate]
    head_w = norm_div = norm_cols = 0
    if epilogue == "heads":
        head_w, norm_div, norm_cols = heads["head_w"], heads["norm_div"], heads["norm_cols"]
        assert bn % max(head_w, LANE) == 0 and norm_cols % bn == 0
        in_specs.append(pl.BlockSpec((1, bn), lambda i, j: (0, j)))
        args.append(heads["gains"])
        if rope:
            period = heads["tabs"][0].shape[1]
            assert bn % period == 0 and tiles_per_seq is not None
            in_specs += [pl.BlockSpec((bm, period), lambda i, j: (i % tiles_per_seq, 0))] * 3
            args += list(heads["tabs"])
    if out_time_major:
        n_col_tiles = n // bn
        out_shape = [jax.ShapeDtypeStruct((st.seq, st.nb * n), out_dtype)]
        out_specs = [pl.BlockSpec((bm, bn), lambda i, j: (i % tiles_per_seq, (i // tiles_per_seq) * n_col_tiles + j))]
    else:
        out_shape = [jax.ShapeDtypeStruct((rows, n), out_dtype)]
        out_specs = [pl.BlockSpec((bm, bn), lambda i, j: (i, j))]
    if emit_xn:
        out_shape.append(jax.ShapeDtypeStruct((rows, k), F32))
        out_specs.append(pl.BlockSpec((bm, k), lambda i, j: (i, 0)))
    xbytes = x.dtype.itemsize
    vmem = (2 * bm * k * xbytes + bm * k * 2 + 2 * k * bn * 2 + 6 * bm * bn * 4
            + (2 * bm * k * 4 if emit_xn else 0)) / MIB + 8
    kern = functools.partial(_proj_kernel, prologue=prologue, emit_xn=emit_xn, epilogue=epilogue, head_w=head_w,
                             norm_div=norm_div, norm_cols=norm_cols, rope=rope, row_chunk=min(bm, 128))
    out = pl.pallas_call(
        kern,
        grid=(rows // bm, n // bn),
        in_specs=in_specs,
        out_specs=out_specs,
        out_shape=out_shape,
        scratch_shapes=[pltpu.VMEM((bm, k), BF16)],
        compiler_params=_cparams(2, vmem),
        name=name,
    )(*args)
    return out if emit_xn else out[0]


def _ffn_kernel(xp_ref, x_ref, xn_ref, g_ref, sh_ref, sc_ref, gate_ref, wa_ref, wb_ref, cw_ref, cb_ref,
                wo_ref, o_ref, h_ref, *, bm, seq, row_chunk):
    i = pl.program_id(0)
    c = pl.program_id(1)
    n_chunks = pl.num_programs(1)

    @pl.when(c == 0)
    def _():
        g, sh, sc = g_ref[...], sh_ref[...], sc_ref[...]
        h_ref[0:HALO, :] = _norm_mod(xp_ref[...], g, sh, sc).astype(BF16)
        h_ref[HALO + bm:, :] = _norm_mod(xn_ref[...], g, sh, sc).astype(BF16)

        def chunk(r, carry):
            src = pl.ds(pl.multiple_of(r * row_chunk, row_chunk), row_chunk)
            dst = pl.ds(pl.multiple_of(HALO + r * row_chunk, HALO), row_chunk)
            h_ref[dst, :] = _norm_mod(x_ref[src, :], g, sh, sc).astype(BF16)
            return carry
        lax.fori_loop(0, bm // row_chunk, chunk, 0)
        o_ref[...] = jnp.zeros_like(o_ref)

    ua = jnp.dot(h_ref[...], wa_ref[...], preferred_element_type=F32)
    ub = jnp.dot(h_ref[HALO:HALO + bm, :], wb_ref[...], preferred_element_type=F32)
    n_all = bm + 2 * HALO
    u_prev = pltpu.roll(ua, 1, 0)[HALO:HALO + bm]
    u_next = pltpu.roll(ua, n_all - 1, 0)[HALO:HALO + bm]
    u_mid = ua[HALO:HALO + bm]
    pos = jnp.bitwise_and(i * bm + lax.broadcasted_iota(jnp.int32, (bm, 1), 0), seq - 1)
    u_prev = jnp.where(pos == 0, 0.0, u_prev)
    u_next = jnp.where(pos == seq - 1, 0.0, u_next)
    cw = cw_ref[...]
    a = cb_ref[...] + u_prev * cw[0:1] + u_mid * cw[1:2] + u_next * cw[2:3]
    gated = ((a * jax.nn.sigmoid(a)) * ub).astype(BF16)
    o_ref[...] += jnp.dot(gated, wo_ref[...], preferred_element_type=F32)

    @pl.when(c == n_chunks - 1)
    def _():
        o_ref[...] = x_ref[...] + gate_ref[...] * o_ref[...]


def _conv_ffn(x, st, g, shift, scale, gate, w_in, conv_w, conv_b, w_out):
    m, d = x.shape
    d_ff = w_out.shape[0]
    bm = min(st.bm, 512)
    ck = _largest_divisor(d_ff, (512, 256, 128))
    n_chunks = d_ff // ck
    n_halo_blocks = m // HALO
    assert st.seq & (st.seq - 1) == 0 and conv_w.shape[0] == 3
    mod_idx = lambda i: st.mod_index(i * bm)
    kern = functools.partial(_ffn_kernel, bm=bm, seq=st.seq, row_chunk=min(bm, 128))
    vmem = (4 * bm * d * 4 + (bm + 2 * HALO) * d * 2 + 6 * d * ck * 2 + 6 * (bm + 2 * HALO) * ck * 4) / MIB + 10
    return pl.pallas_call(
        kern,
        grid=(m // bm, n_chunks),
        in_specs=[
            pl.BlockSpec((HALO, d), lambda i, c: (jnp.maximum(i * (bm // HALO) - 1, 0), 0)),
            pl.BlockSpec((bm, d), lambda i, c: (i, 0)),
            pl.BlockSpec((HALO, d), lambda i, c: (jnp.minimum((i + 1) * (bm // HALO), n_halo_blocks - 1), 0)),
            pl.BlockSpec((1, d), lambda i, c: (0, 0)),
            pl.BlockSpec((None, 1, d), lambda i, c: (mod_idx(i), 0, 0)),
            pl.BlockSpec((None, 1, d), lambda i, c: (mod_idx(i), 0, 0)),
            pl.BlockSpec((None, 1, d), lambda i, c: (mod_idx(i), 0, 0)),
            pl.BlockSpec((d, ck), lambda i, c: (0, c)),
            pl.BlockSpec((d, ck), lambda i, c: (0, n_chunks + c)),
            pl.BlockSpec((conv_w.shape[0], ck), lambda i, c: (0, c)),
            pl.BlockSpec((1, ck), lambda i, c: (0, c)),
            pl.BlockSpec((ck, d), lambda i, c: (c, 0)),
        ],
        out_specs=pl.BlockSpec((bm, d), lambda i, c: (i, 0)),
        out_shape=jax.ShapeDtypeStruct((m, d), F32),
        scratch_shapes=[pltpu.VMEM((bm + 2 * HALO, d), BF16)],
        compiler_params=_cparams(2, vmem),
        name="conv_ffn",
    )(x, x, x, g, shift, scale, gate, w_in, w_in, conv_w, conv_b.reshape(1, d_ff), w_out)


def _qk(q, k):
    return lax.dot_general(q, k, (((1,), (1,)), ((), ())), preferred_element_type=F32)


def _attend(scores, values, sink=None):
    m = None
    for s in scores:
        mi = jnp.max(s, axis=-1, keepdims=True)
        m = mi if m is None else jnp.maximum(m, mi)
    if sink is not None:
        m = jnp.maximum(m, sink)
    es = [jnp.exp2(s - m) for s in scores]
    den = None
    for e in es:
        di = jnp.sum(e, axis=-1, keepdims=True)
        den = di if den is None else den + di
    if sink is not None:
        den = den + jnp.exp2(sink - m)
    out = None
    for e, v in zip(es, values):
        oi = jnp.dot(e.astype(BF16), v, preferred_element_type=F32)
        out = oi if out is None else out + oi
    return out * (1.0 / den)


def _ctx_attn_kernel(q_ref, k_ref, v_ref, *outs, heads, dq, dv, emit_kv):
    o_ref = outs[0]
    for h in range(heads):
        q = q_ref[:, h * dq:(h + 1) * dq].astype(BF16)
        k = k_ref[:, h * dq:(h + 1) * dq]
        v = v_ref[:, h * dv:(h + 1) * dv]
        if emit_kv:
            outs[1][h] = k.astype(F32)
            outs[2][h] = v.astype(F32)
        o = _attend([_qk(q, k.astype(BF16))], [v.astype(BF16)])
        o_ref[:, h * dv:(h + 1) * dv] = o.astype(o_ref.dtype)


def _ctx_attention(qm, km, vm, st, *, n_heads, dq, dv, q_col, k_col, v_col, emit_kv=False):
    hb = _largest_divisor(n_heads, (4, 2, 1))
    s = st.seq
    assert q_col % (hb * dq) == 0 and k_col % (hb * dq) == 0 and v_col % (hb * dv) == 0
    qo, ko, vo = q_col // (hb * dq), k_col // (hb * dq), v_col // (hb * dv)
    out_shape = [jax.ShapeDtypeStruct((st.rows, n_heads * dv), BF16)]
    out_specs = [pl.BlockSpec((s, hb * dv), lambda b, g: (b, g))]
    if emit_kv:
        out_shape += [jax.ShapeDtypeStruct((st.nb, n_heads, s, dq), F32),
                      jax.ShapeDtypeStruct((st.nb, n_heads, s, dv), F32)]
        out_specs += [pl.BlockSpec((None, hb, s, dq), lambda b, g: (b, g, 0, 0)),
                      pl.BlockSpec((None, hb, s, dv), lambda b, g: (b, g, 0, 0))]
    out = pl.pallas_call(
        functools.partial(_ctx_attn_kernel, heads=hb, dq=dq, dv=dv, emit_kv=emit_kv),
        grid=(st.nb, n_heads // hb),
        in_specs=[pl.BlockSpec((s, hb * dq), lambda b, g: (b, qo + g)),
                  pl.BlockSpec((s, hb * dq), lambda b, g: (b, ko + g)),
                  pl.BlockSpec((s, hb * dv), lambda b, g: (b, vo + g))],
        out_specs=out_specs,
        out_shape=out_shape,
        compiler_params=_cparams(2, 32),
        name="ctx_attention",
    )(qm, km, vm)
    return out if emit_kv else out[0]


def _nat_kernel(q_ref, k_ref, v_ref, kc_ref, vc_ref, bias_ref, o_ref, *, key_rows, rows):
    i = pl.program_id(2)
    n_keys = key_rows * GRID_W
    first_row = jnp.clip(i * NA_Q_ROWS - NA_WIN_ROWS // 2, 0, rows - key_rows)
    start = pl.multiple_of(first_row * GRID_W, GRID_W * 4)
    q = q_ref[...]
    k = k_ref[pl.ds(start, n_keys), :]
    v = v_ref[pl.ds(start, n_keys), :]
    s_loc = _qk(q, k) + bias_ref[...]
    s_ctx = _qk(q, kc_ref[...].astype(BF16))
    o_ref[...] = _attend([s_loc, s_ctx], [v, vc_ref[...].astype(BF16)]).astype(o_ref.dtype)


def _nat_bias(rpb, rows):
    n_blocks = rows // NA_Q_ROWS
    key_rows = min(NA_K_ROWS, rows)
    wr = min(NA_WIN_ROWS, rows)
    reps = [0, min(1, n_blocks - 1), n_blocks - 1]
    heads = rpb.shape[0]
    nq, nk = NA_Q_ROWS * GRID_W, key_rows * GRID_W
    shape = (NA_Q_ROWS, GRID_W, key_rows, GRID_W)
    qc = np.arange(GRID_W)
    cstart = np.clip(qc - NA_WIN_COLS // 2, 0, GRID_W - NA_WIN_COLS)
    col_ok = (qc[None, :] >= cstart[:, None]) & (qc[None, :] < cstart[:, None] + NA_WIN_COLS)
    rp = jnp.pad(rpb.astype(F32), ((0, 0), (key_rows, key_rows), (GRID_W - NA_WIN_COLS, GRID_W - NA_WIN_COLS)))
    row_slabs, mask_l = [], []
    for i in reps:
        ks = int(np.clip(i * NA_Q_ROWS - NA_WIN_ROWS // 2, 0, rows - key_rows))
        r = i * NA_Q_ROWS + np.arange(NA_Q_ROWS)
        rs = np.clip(r - wr // 2, 0, rows - wr)
        kr = ks + np.arange(key_rows)
        row_ok = (kr[None, :] >= rs[:, None]) & (kr[None, :] < rs[:, None] + wr)
        for rq in range(NA_Q_ROWS):
            first = ks - int(r[rq]) + NA_WIN_ROWS - 1 + key_rows
            assert 0 <= first and first + key_rows <= rp.shape[1]
            row_slabs.append(rp[:, first:first + key_rows, :])
        mask_l.append(np.broadcast_to(row_ok[:, None, :, None] & col_ok[None, :, None, :], shape).reshape(nq, nk))
    slab = jnp.stack(row_slabs, axis=1).reshape(heads, len(reps), NA_Q_ROWS, key_rows, 2 * GRID_W - 1)
    toep = jnp.stack([slab[..., GRID_W - 1 - c:2 * GRID_W - 1 - c] for c in range(GRID_W)], axis=3)
    bias = toep.reshape(heads, len(reps), nq, nk)
    return jnp.where(jnp.asarray(np.stack(mask_l))[None], bias * LOG2E, NEG)


def _nat_attention(qkv, st, cache_k, cache_v, j, rpb, dh):
    heads = rpb.shape[0]
    n = st.seq
    p = cache_k.shape[3]
    rows = n // GRID_W
    assert rows % NA_Q_ROWS == 0 and rows >= NA_K_ROWS and dh % LANE == 0
    n_blocks = rows // NA_Q_ROWS
    key_rows = min(NA_K_ROWS, rows)
    nq, nk = NA_Q_ROWS * GRID_W, key_rows * GRID_W
    bias = _nat_bias(rpb, rows)
    btype = lambda i: jnp.where(i == 0, 0, jnp.where(i == n_blocks - 1, 2, 1))
    kern = functools.partial(_nat_kernel, key_rows=key_rows, rows=rows)
    return pl.pallas_call(
        kern,
        grid=(st.nb, heads, n_blocks),
        in_specs=[pl.BlockSpec((nq, dh), lambda b, h, i: (b * n_blocks + i, h)),
                  pl.BlockSpec((n, dh), lambda b, h, i: (b, heads + h)),
                  pl.BlockSpec((n, dh), lambda b, h, i: (b, 2 * heads + h)),
                  pl.BlockSpec((None, None, None, p, dh), lambda b, h, i: (b, j, h, 0, 0)),
                  pl.BlockSpec((None, None, None, p, dh), lambda b, h, i: (b, j, h, 0, 0)),
                  pl.BlockSpec((None, None, nq, nk), lambda b, h, i: (h, btype(i), 0, 0))],
        out_specs=pl.BlockSpec((nq, dh), lambda b, h, i: (b * n_blocks + i, h)),
        out_shape=jax.ShapeDtypeStruct((st.rows, heads * dh), BF16),
        compiler_params=_cparams(3, 40),
        name="nat_attention",
    )(qkv, qkv, qkv, cache_k, cache_v, bias)


def _joint_dense_kernel(q_ref, k_ref, v_ref, kc_ref, vc_ref, o_ref, *, chunk):
    q = q_ref[...]
    n = k_ref.shape[0]
    pieces = [(k_ref, v_ref, c0, min(chunk, n - c0)) for c0 in range(0, n, chunk)]
    pieces.append((kc_ref, vc_ref, 0, kc_ref.shape[0]))
    m = den = acc = None
    for kr, vr, c0, size in pieces:
        s = _qk(q, kr[c0:c0 + size, :])
        mc = jnp.max(s, axis=-1, keepdims=True)
        m_new = mc if m is None else jnp.maximum(m, mc)
        e = jnp.exp2(s - m_new)
        dc = jnp.sum(e, axis=-1, keepdims=True)
        pv = jnp.dot(e.astype(BF16), vr[c0:c0 + size, :], preferred_element_type=F32)
        if m is None:
            den, acc = dc, pv
        else:
            alpha = jnp.exp2(m - m_new)
            den, acc = alpha * den + dc, alpha * acc + pv
        m = m_new
    o_ref[...] = (acc * (1.0 / den)).astype(o_ref.dtype)


def _joint_dense_attention(qm, km, vm, kcm, vcm, st, p, *, n_heads, dq, dv):
    n = st.seq
    bq = _largest_divisor(n, (512, 256, 128, 64, 32, 16))
    nqb = n // bq
    return pl.pallas_call(
        functools.partial(_joint_dense_kernel, chunk=1024),
        grid=(st.nb, n_heads, nqb),
        in_specs=[pl.BlockSpec((bq, dq), lambda b, h, i: (b * nqb + i, h)),
                  pl.BlockSpec((n, dq), lambda b, h, i: (b, h)),
                  pl.BlockSpec((n, dv), lambda b, h, i: (b, h)),
                  pl.BlockSpec((p, dq), lambda b, h, i: (b, h)),
                  pl.BlockSpec((p, dv), lambda b, h, i: (b, h))],
        out_specs=pl.BlockSpec((bq, dv), lambda b, h, i: (b * nqb + i, h)),
        out_shape=jax.ShapeDtypeStruct((st.rows, n_heads * dv), BF16),
        compiler_params=_cparams(3, 48),
        name="mla_attention",
    )(qm, km, vm, kcm, vcm)


def _both_halves(x, s):
    x = x.astype(F32)
    low = lax.broadcasted_iota(jnp.int32, (1, LANE), 1) < LANE // 2
    keep = low if s == 0 else jnp.logical_not(low)
    return jnp.where(keep, x, pltpu.roll(x, LANE // 2, 1)).astype(BF16)


def _swa_step(sinks_ref, pair, q_ref, k, v, kc, vc, o_ref, *, dh, groups, local_mask):
    kv_per_step = LANE // dh
    assert kv_per_step == 2 and groups % 2 == 0
    rows = q_ref.shape[0]
    low = lax.broadcasted_iota(jnp.int32, (1, LANE), 1) < dh
    row_group = lax.broadcasted_iota(jnp.int32, (groups * rows, 1), 0) // rows
    for s in range(kv_per_step):
        kd, vd = _both_halves(k, s), _both_halves(v, s)
        q_parts = []
        for g in range(groups):
            c0 = ((s * groups + g) * dh // LANE) * LANE
            tile = q_ref[:, c0:c0 + LANE].astype(BF16)
            q_parts.append(jnp.where(low if g % 2 == 0 else jnp.logical_not(low), tile, jnp.zeros_like(tile)))
        q = jnp.concatenate(q_parts, axis=0)
        sink = jnp.zeros((groups * rows, 1), F32)
        for g in range(groups):
            sink = jnp.where(row_group == g, sinks_ref[(pair * kv_per_step + s) * groups + g], sink)
        s_loc = _qk(q, kd)
        if local_mask is not None:
            s_loc = jnp.where(local_mask, s_loc, NEG)
        if kc is not None:
            kcd = jnp.concatenate([kc[s], kc[s]], axis=-1).astype(BF16)
            vcd = jnp.concatenate([vc[s], vc[s]], axis=-1).astype(BF16)
            out = _attend([s_loc, _qk(q, kcd)], [vd, vcd], sink)
        else:
            out = _attend([s_loc], [vd], sink)
        for g in range(0, groups, 2):
            c0 = (s * groups + g) * dh
            o_ref[:, c0:c0 + LANE] = jnp.where(low, out[g * rows:(g + 1) * rows],
                                               out[(g + 1) * rows:(g + 2) * rows]).astype(o_ref.dtype)


def _swa_ctx_kernel(sinks_ref, q_ref, k_ref, v_ref, o_ref, ko_ref, vo_ref, *, dh, groups):
    pair = pl.program_id(1)
    k, v = k_ref[...], v_ref[...]
    for s in range(LANE // dh):
        ko_ref[s] = k[:, s * dh:(s + 1) * dh].astype(F32)
        vo_ref[s] = v[:, s * dh:(s + 1) * dh].astype(F32)
    _swa_step(sinks_ref, pair, q_ref, k, v, None, None, o_ref, dh=dh, groups=groups, local_mask=None)


def _swa_lat_kernel(sinks_ref, q_ref, k_ref, v_ref, kc_ref, vc_ref, o_ref, *, dh, groups, n):
    pair = pl.program_id(1)
    blk = pl.program_id(2)
    n_keys = min(3 * SWA_BLOCK, n)
    start = pl.multiple_of(jnp.clip((blk - 1) * SWA_BLOCK, 0, n - n_keys), SWA_BLOCK)
    k = k_ref[pl.ds(start, n_keys), :]
    v = v_ref[pl.ds(start, n_keys), :]
    qpos = blk * SWA_BLOCK + lax.broadcasted_iota(jnp.int32, (groups * SWA_BLOCK, 1), 0) % SWA_BLOCK
    kpos = start + lax.broadcasted_iota(jnp.int32, (1, n_keys), 1)
    mask = jnp.abs(qpos - kpos) <= SWA_WINDOW
    _swa_step(sinks_ref, pair, q_ref, k, v, kc_ref, vc_ref, o_ref, dh=dh, groups=groups, local_mask=mask)


def _swa_attention(qkv, st, sinks, *, heads, kvh, dh, cache=None):
    groups = heads // kvh
    kv_per_step = LANE // dh
    assert LANE % dh == 0 and kvh % kv_per_step == 0 and groups % kv_per_step == 0
    pairs = kvh // kv_per_step
    qw = kv_per_step * groups * dh
    k_blk = heads * dh // LANE
    v_blk = (heads + kvh) * dh // LANE
    n = st.seq
    common = dict(dh=dh, groups=groups)
    smem = pl.BlockSpec(memory_space=pltpu.SMEM)
    if cache is None:
        out = pl.pallas_call(
            functools.partial(_swa_ctx_kernel, **common),
            grid=(st.nb, pairs),
            in_specs=[smem,
                      pl.BlockSpec((n, qw), lambda b, c: (b, c)),
                      pl.BlockSpec((n, LANE), lambda b, c: (b, k_blk + c)),
                      pl.BlockSpec((n, LANE), lambda b, c: (b, v_blk + c))],
            out_specs=[pl.BlockSpec((n, qw), lambda b, c: (b, c)),
                       pl.BlockSpec((None, kv_per_step, n, dh), lambda b, c: (b, c, 0, 0)),
                       pl.BlockSpec((None, kv_per_step, n, dh), lambda b, c: (b, c, 0, 0))],
            out_shape=[jax.ShapeDtypeStruct((st.rows, heads * dh), BF16),
                       jax.ShapeDtypeStruct((st.nb, kvh, n, dh), F32),
                       jax.ShapeDtypeStruct((st.nb, kvh, n, dh), F32)],
            compiler_params=_cparams(2, 32),
            name="swa_ctx_attention",
        )(sinks, qkv, qkv, qkv)
        return out
    cache_k, cache_v, j = cache
    p = cache_k.shape[3]
    nblk = n // SWA_BLOCK
    assert n % SWA_BLOCK == 0
    return pl.pallas_call(
        functools.partial(_swa_lat_kernel, n=n, **common),
        grid=(st.nb, pairs, nblk),
        in_specs=[smem,
                  pl.BlockSpec((SWA_BLOCK, qw), lambda b, c, i: (b * nblk + i, c)),
                  pl.BlockSpec((n, LANE), lambda b, c, i: (b, k_blk + c)),
                  pl.BlockSpec((n, LANE), lambda b, c, i: (b, v_blk + c)),
                  pl.BlockSpec((None, None, kv_per_step, p, dh), lambda b, c, i: (b, j, c, 0, 0)),
                  pl.BlockSpec((None, None, kv_per_step, p, dh), lambda b, c, i: (b, j, c, 0, 0))],
        out_specs=pl.BlockSpec((SWA_BLOCK, qw), lambda b, c, i: (b * nblk + i, c)),
        out_shape=jax.ShapeDtypeStruct((st.rows, heads * dh), BF16),
        compiler_params=_cparams(3, 32),
        name="swa_attention",
    )(sinks, qkv, qkv, qkv, cache_k, cache_v)


def _mla_kv_kernel(*refs, norm, emit_xn, rope, row_chunk, norm_div):
    it = iter(refs)
    x_ref = next(it)
    g_ref = next(it) if norm else None
    w_ref, kr_ref, g1_ref, g2_ref = next(it), next(it), next(it), next(it)
    tabs = (next(it), next(it), next(it)) if rope else None
    k_ref, v_ref = next(it), next(it)
    xn_ref = next(it) if emit_xn else None
    xs_ref = next(it)

    @pl.when(pl.program_id(1) == 0)
    def _():
        _fill_lhs(x_ref, xs_ref, xn_ref, "norm" if norm else None, g_ref, None, None, row_chunk)

    acc = jnp.dot(xs_ref[...], w_ref[...], preferred_element_type=F32)
    nope = acc[:, :LANE]
    kr = kr_ref[...]
    ssq = jnp.sum(nope * nope, axis=-1, keepdims=True) + jnp.sum(kr * kr, axis=-1, keepdims=True)
    inv = lax.rsqrt(ssq / norm_div + EPS)
    k_rot = (kr * inv) * g2_ref[...]
    if rope:
        k_rot = _rope_apply(k_rot, *(t[...] for t in tabs))
    k_ref[:, :LANE] = ((nope * inv) * g1_ref[...]).astype(k_ref.dtype)
    k_ref[:, LANE:] = k_rot.astype(k_ref.dtype)
    v_ref[...] = acc[:, LANE:].astype(v_ref.dtype)


def _mla_kv(x, x_block, w_ukv, kr, kr_block, g_kva, g1, g2, tabs, st, *, n_heads, norm_div, emit_xn, name):
    rows = x.shape[0]
    k, n = w_ukv.shape
    head_n = n // n_heads
    assert head_n == 2 * LANE, "nope and value widths must both be one lane tile"
    kx, kidx = x_block
    krw, kridx = kr_block
    assert kx == k and krw == LANE
    bm = st.bm
    norm = g_kva is not None
    rope = tabs is not None
    in_specs = [pl.BlockSpec((bm, k), lambda i, h: (i, kidx))]
    args = [x]
    if norm:
        in_specs.append(pl.BlockSpec((1, k), lambda i, h: (0, 0)))
        args.append(g_kva)
    in_specs += [pl.BlockSpec((k, head_n), lambda i, h: (0, h)),
                 pl.BlockSpec((bm, LANE), lambda i, h: (i, kridx)),
                 pl.BlockSpec((1, LANE), lambda i, h: (0, 0)),
                 pl.BlockSpec((1, LANE), lambda i, h: (0, 0))]
    args += [w_ukv, kr, g1, g2]
    if rope:
        tiles_per_seq = st.seq // bm
        in_specs += [pl.BlockSpec((bm, LANE), lambda i, h: (i % tiles_per_seq, 0))] * 3
        args += list(tabs)
    out_shape = [jax.ShapeDtypeStruct((rows, n_heads * 2 * LANE), BF16),
                 jax.ShapeDtypeStruct((rows, n_heads * LANE), BF16)]
    out_specs = [pl.BlockSpec((bm, 2 * LANE), lambda i, h: (i, h)),
                 pl.BlockSpec((bm, LANE), lambda i, h: (i, h))]
    if emit_xn:
        out_shape.append(jax.ShapeDtypeStruct((rows, k), F32))
        out_specs.append(pl.BlockSpec((bm, k), lambda i, h: (i, 0)))
    kern = functools.partial(_mla_kv_kernel, norm=norm, emit_xn=emit_xn, rope=rope, row_chunk=min(bm, 128),
                             norm_div=norm_div)
    return pl.pallas_call(
        kern,
        grid=(rows // bm, n_heads),
        in_specs=in_specs,
        out_specs=out_specs,
        out_shape=out_shape,
        scratch_shapes=[pltpu.VMEM((bm, k), BF16)],
        compiler_params=_cparams(2, 40),
        name=name,
    )(*args)


def _band_plan(width, block):
    n_tiles = width // LANE
    lo = [((t * LANE) // block) * block for t in range(n_tiles)]
    hi = [(((t + 1) * LANE - 1) // block + 1) * block for t in range(n_tiles)]
    start = [(l // LANE) * LANE for l in lo]
    kb = max(-(-(h - s) // LANE) * LANE for h, s in zip(hi, start))
    kb = min(kb, width)
    start = [min(s, width - kb) for s in start]
    return start, kb


def _band_weights(w, width, block, start, kb):
    n_tiles = width // LANE
    wb = w.astype(BF16)
    tiles = []
    for t in range(n_tiles):
        pieces = []
        col = t * LANE
        while col < (t + 1) * LANE:
            blk = col // block
            col_end = min((blk + 1) * block, (t + 1) * LANE)
            sub = wb[blk, :, col - blk * block:col_end - blk * block]
            top = blk * block - start[t]
            pieces.append(jnp.pad(sub, ((top, kb - top - block), (0, 0))))
            col = col_end
        tiles.append(jnp.concatenate(pieces, axis=1))
    return jnp.stack(tiles)


def _expm1_neg(x):
    return -jnp.tanh(0.5 * x) * (jnp.exp(x) + 1.0)


def _gelu_tanh(x):
    cdf = 0.5 * (1.0 + jnp.tanh(np.float32(np.sqrt(2.0 / np.pi)) * (x + 0.044715 * (x * x * x))))
    return x * cdf


def _sigmoid(x):
    return 0.5 * (1.0 + jnp.tanh(0.5 * x))


def _lru_pass_kernel(*refs, reverse, starts, kb, bt, seq, nb, taps):
    left = taps // 2
    right = taps - 1 - left
    it = iter(refs)
    xp_ref, x_ref = next(it), next(it)
    xn_ref = next(it) if right > 0 else None
    cw_ref, cb_ref = next(it), next(it)
    wa_ref, wi_ref, ba_ref, bi_ref, lam_ref, h0_ref = (next(it) for _ in range(6))
    hsf_ref, gate_ref = (next(it), next(it)) if reverse else (None, None)
    out_ref, ht_ref = next(it), next(it)
    xc_s, a_s, bx_s, carry = next(it), next(it), next(it), next(it)

    step = pl.program_id(0)
    n_steps = pl.num_programs(0)
    tile = (n_steps - 1 - step) if reverse else step
    n_tiles = len(starts)
    rows = nb * bt

    @pl.when(step == 0)
    def _():
        carry[...] = h0_ref[...]

    pos = tile * bt + lax.broadcasted_iota(jnp.int32, (bt, 1, 1), 0)
    for t in range(n_tiles):
        lanes = slice(t * LANE, (t + 1) * LANE)
        parts = [xp_ref[:, :, lanes], x_ref[:, :, lanes]] + ([xn_ref[:, :, lanes]] if right > 0 else [])
        full = jnp.concatenate(parts, axis=0)
        acc = jnp.broadcast_to(cb_ref[:, lanes], (bt, nb, LANE))
        for k in range(taps):
            off = k - left
            shifted = full[k:k + bt]
            if off != 0:
                shifted = jnp.where((pos + off >= 0) & (pos + off < seq), shifted, 0.0)
            acc = acc + shifted * cw_ref[k:k + 1, lanes]
        xc_s[:, lanes] = acc.reshape(rows, LANE)

    neg_lam = -lam_ref[...]
    softplus = jnp.maximum(neg_lam, 0.0) + jnp.log1p(jnp.exp(-jnp.abs(neg_lam)))
    for t in range(n_tiles):
        lanes = slice(t * LANE, (t + 1) * LANE)
        xw = xc_s[:, starts[t]:starts[t] + kb].astype(BF16)
        r = _sigmoid(jnp.dot(xw, wa_ref[t], preferred_element_type=F32) + ba_ref[:, lanes])
        ig = _sigmoid(jnp.dot(xw, wi_ref[t], preferred_element_type=F32) + bi_ref[:, lanes])
        log_a = -LRU_C * r * softplus[:, lanes]
        a = jnp.exp(log_a)
        a_s[:, lanes] = a
        bx_s[:, lanes] = jnp.sqrt(-jnp.tanh(log_a) * (a * a + 1.0)) * (ig * xc_s[:, lanes])

    h = carry[...]
    for s in range(bt):
        ts = (bt - 1 - s) if reverse else s
        slab = slice(ts * nb, (ts + 1) * nb)
        h = a_s[slab, :] * h + bx_s[slab, :]
        a_s[slab, :] = h
    carry[...] = h
    ht_ref[...] = h
    hs = a_s[...].reshape(bt, nb, a_s.shape[1])
    if reverse:
        out_ref[...] = (_gelu_tanh(gate_ref[...]) * (hsf_ref[...] + hs)).astype(out_ref.dtype)
    else:
        out_ref[...] = hs


def _lru_pass(u, st, conv_w, conv_b, wa, wi, b_a, b_i, lam, h0, starts, kb, *, reverse, hs_fwd=None):
    c = conv_w.shape[1]
    taps = conv_w.shape[0]
    left, right = taps // 2, taps - 1 - taps // 2
    nb, seq = st.nb, st.seq
    bt = min(max(256 // nb, SUBLANE), seq)
    assert seq % bt == 0 and c % LANE == 0 and left > 0 and bt % left == 0 and (right == 0 or bt % right == 0)
    nt = seq // bt
    n_tiles = c // LANE
    u3 = u.reshape(seq, nb, 2 * c)
    tmap = (lambda s: nt - 1 - s) if reverse else (lambda s: s)
    full = lambda *shape: pl.BlockSpec(shape, lambda s: (0,) * len(shape))
    in_specs = [pl.BlockSpec((left, nb, c), lambda s: (jnp.maximum(tmap(s) * (bt // left) - 1, 0), 0, 0)),
                pl.BlockSpec((bt, nb, c), lambda s: (tmap(s), 0, 0))]
    args = [u3, u3]
    if right > 0:
        in_specs.append(pl.BlockSpec((right, nb, c),
                                     lambda s: (jnp.minimum((tmap(s) + 1) * (bt // right), seq // right - 1), 0, 0)))
        args.append(u3)
    in_specs += [full(taps, c), full(1, c), full(n_tiles, kb, LANE), full(n_tiles, kb, LANE), full(1, c), full(1, c),
                 full(1, c), full(nb, c)]
    args += [conv_w, conv_b.reshape(1, c), wa, wi, b_a.reshape(1, c), b_i.reshape(1, c), lam.reshape(1, c), h0]
    if reverse:
        in_specs += [pl.BlockSpec((bt, nb, c), lambda s: (tmap(s), 0, 0)),
                     pl.BlockSpec((bt, nb, c), lambda s: (tmap(s), 0, 1))]
        args += [hs_fwd, u3]
    kern = functools.partial(_lru_pass_kernel, reverse=reverse, starts=tuple(starts), kb=kb, bt=bt, seq=seq, nb=nb,
                             taps=taps)
    blk = nb * bt * c * 4 / MIB
    vmem = (2 + 2 + 3 + (4 if reverse else 0) + 4) * blk + 4 * n_tiles * kb * LANE * 2 / MIB + 8
    return pl.pallas_call(
        kern,
        grid=(nt,),
        in_specs=in_specs,
        out_specs=[pl.BlockSpec((bt, nb, c), lambda s: (tmap(s), 0, 0)),
                   pl.BlockSpec((nb, c), lambda s: (0, 0))],
        out_shape=[jax.ShapeDtypeStruct((seq, nb, c), F32),
                   jax.ShapeDtypeStruct((nb, c), F32)],
        scratch_shapes=[pltpu.VMEM((nb * bt, c), F32), pltpu.VMEM((nb * bt, c), F32),
                        pltpu.VMEM((nb * bt, c), F32), pltpu.VMEM((nb, c), F32)],
        compiler_params=_cparams(1, vmem),
        name="lru_bwd" if reverse else "lru_fwd",
    )(*args)


def _mixer_nat(xs, streams, mods, cache_k, cache_v, j, w_qkv, g_mix, g_q, g_k, rpb, w_o):
    heads, dh = rpb.shape[0], g_q.shape[0]
    w_qkv, w_o = w_qkv.astype(BF16), w_o.astype(BF16)
    gains = jnp.concatenate([jnp.tile(g_q * (dh ** -0.5 * LOG2E), heads), jnp.tile(g_k, heads),
                             jnp.ones((heads * dh,), F32)])[None]
    spec = dict(head_w=dh, norm_div=dh, norm_cols=2 * heads * dh, gains=gains)
    new_x, extra = [], None
    for x, st in zip(xs, streams):
        latent = not st.shared
        qkv = _proj(x, w_qkv, st, norm_g=g_mix, mod=(mods[0], mods[1]), heads=spec,
                    out_dtype=BF16 if latent else F32, name="nat_qkv")
        if latent:
            o = _nat_attention(qkv, st, cache_k, cache_v, j, rpb, dh)
        else:
            o, kc, vc = _ctx_attention(qkv, qkv, qkv, st, n_heads=heads, dq=dh, dv=dh, q_col=0, k_col=heads * dh,
                                       v_col=2 * heads * dh, emit_kv=True)
            extra = (kc, vc)
        new_x.append(_proj(o, w_o, st, res=x, gate=mods[2], name="nat_out"))
    return new_x, extra


def _mixer_lru(xs, streams, mods, state, w_in, g_mix, conv_w, conv_b, w_a, b_a, w_i, b_i, lam, w_out):
    c = conv_w.shape[1]
    block = w_a.shape[-1]
    w_in, w_out = w_in.astype(BF16), w_out.astype(BF16)
    starts, kb = _band_plan(c, block)
    wa = [_band_weights(w_a[d], c, block, starts, kb) for d in range(2)]
    wi = [_band_weights(w_i[d], c, block, starts, kb) for d in range(2)]
    new_x, st_out = [], None
    for x, st in zip(xs, streams):
        latent = not st.shared
        h0 = state.astype(F32) if latent else jnp.zeros((st.nb, 2, c), F32)
        u = _proj(x, w_in, st, norm_g=g_mix, mod=(mods[0], mods[1]), out_time_major=True, name="lru_in")
        hs_f, t_f = _lru_pass(u, st, conv_w, conv_b, wa[0], wi[0], b_a[0], b_i[0], lam[0], h0[:, 0], starts, kb,
                              reverse=False)
        y, t_b = _lru_pass(u, st, conv_w, conv_b, wa[1], wi[1], b_a[1], b_i[1], lam[1], h0[:, 1], starts, kb,
                           reverse=True, hs_fwd=hs_f)
        if not latent:
            st_out = jnp.stack([t_f, t_b], axis=1)
        new_x.append(_proj(y.reshape(st.seq, st.nb * c), w_out, st, res=x, gate=mods[2], x_time_major=True,
                           name="lru_out"))
    return new_x, st_out


def _mixer_mla(xs, streams, mods, cache_ckv, cache_kr, w_down, g_mix, g_qa, g_kva, w_uq, w_ukv, g_q, g_k, w_o):
    d_model = w_down.shape[0]
    q_rank, kv_rank = g_qa.shape[0], g_kva.shape[0]
    qk_dim = g_q.shape[0]
    heads = w_uq.shape[1] // qk_dim
    rope = w_down.shape[1] - q_rank - kv_rank
    nope = qk_dim - rope
    assert nope == LANE and rope <= LANE and kv_rank % LANE == 0 and q_rank % LANE == 0
    head_w = 2 * LANE
    q_pad = -q_rank % kv_rank
    kv_col = q_rank + q_pad
    tail_pad = -(kv_col + kv_rank + rope) % 512
    w_dn = jnp.concatenate([w_down[:, :q_rank], jnp.zeros((d_model, q_pad), F32),
                            w_down[:, q_rank:q_rank + kv_rank], w_down[:, q_rank + kv_rank:],
                            jnp.zeros((d_model, tail_pad), F32)], axis=1).astype(BF16)
    kr_blk = (kv_col + kv_rank) // LANE
    w_q = jnp.pad(w_uq.reshape(q_rank, heads, qk_dim), ((0, 0), (0, 0), (0, head_w - qk_dim)))
    w_q = w_q.reshape(q_rank, heads * head_w).astype(BF16)
    w_ukv, w_o = w_ukv.astype(BF16), w_o.astype(BF16)
    gq = jnp.tile(jnp.pad(g_q * (qk_dim ** -0.5 * LOG2E), (0, head_w - qk_dim)), heads)[None]
    g1, g2 = g_k[None, :nope], jnp.pad(g_k[nope:], (0, LANE - rope))[None]
    p = cache_ckv.shape[1]
    new_x, extra = [], None
    for x, st in zip(xs, streams):
        latent = not st.shared
        d = _proj(x, w_dn, st, norm_g=g_mix, mod=(mods[0], mods[1]), name="mla_down")
        q_tabs = _rope_tables(st.seq, rope, nope, head_w) if latent else None
        k_tabs = _rope_tables(st.seq, rope, 0, LANE) if latent else None
        q = _proj(d, w_q, st, x_block=(q_rank, 0), norm_g=g_qa[None],
                  heads=dict(head_w=head_w, norm_div=qk_dim, norm_cols=heads * head_w, gains=gq, tabs=q_tabs),
                  out_dtype=BF16, name="mla_uq")
        kv = _mla_kv(d, (kv_rank, kv_col // kv_rank), w_ukv, d, (LANE, kr_blk), g_kva[None], g1, g2, k_tabs, st,
                     n_heads=heads, norm_div=qk_dim, emit_xn=not latent, name="mla_ukv")
        if latent:
            k, v = kv
            cst = _Stream(st.nb, p, 0, True)
            krc = jnp.pad(cache_kr.reshape(st.nb * p, rope), ((0, 0), (0, LANE - rope)))
            kc, vc = _mla_kv(cache_ckv.reshape(st.nb * p, kv_rank), (kv_rank, 0), w_ukv, krc, (LANE, 0), None,
                             g1, g2, None, cst, n_heads=heads, norm_div=qk_dim, emit_xn=False,
                             name="mla_ukv_cache")
            o = _joint_dense_attention(q, k, v, kc, vc, st, p, n_heads=heads, dq=head_w, dv=LANE)
        else:
            k, v, ckv = kv
            o = _ctx_attention(q, k, v, st, n_heads=heads, dq=head_w, dv=LANE, q_col=0, k_col=0, v_col=0)
            kr_out = d[:, kv_col + kv_rank:kv_col + kv_rank + rope]
            extra = (ckv.reshape(st.nb, st.seq, kv_rank), kr_out.reshape(st.nb, st.seq, rope))
        new_x.append(_proj(o, w_o, st, res=x, gate=mods[2], name="mla_out"))
    return new_x, extra


def _mixer_swa(xs, streams, mods, cache_k, cache_v, j, w_qkv, g_mix, g_q, g_k, sinks, w_o):
    dh = g_q.shape[0]
    heads = sinks.shape[0]
    kvh = (w_qkv.shape[1] // dh - heads) // 2
    w_qkv, w_o = w_qkv.astype(BF16), w_o.astype(BF16)
    gains = jnp.concatenate([jnp.tile(g_q * (dh ** -0.5 * LOG2E), heads), jnp.tile(g_k, kvh),
                             jnp.ones((kvh * dh,), F32)])[None]
    sinks = sinks.astype(F32) * LOG2E
    new_x, extra = [], None
    for x, st in zip(xs, streams):
        latent = not st.shared
        tabs = _rope_tables(st.seq, dh, 0, dh) if latent else None
        if tabs is not None:
            tabs = tuple(jnp.tile(t, (1, LANE // dh)) for t in tabs)
        spec = dict(head_w=dh, norm_div=dh, norm_cols=(heads + kvh) * dh, gains=gains, tabs=tabs)
        qkv = _proj(x, w_qkv, st, norm_g=g_mix, mod=(mods[0], mods[1]), heads=spec,
                    out_dtype=BF16 if latent else F32, bn=kvh * dh, name="swa_qkv")
        if latent:
            o = _swa_attention(qkv, st, sinks, heads=heads, kvh=kvh, dh=dh, cache=(cache_k, cache_v, j))
        else:
            o, kc, vc = _swa_attention(qkv, st, sinks, heads=heads, kvh=kvh, dh=dh)
            extra = (kc, vc)
        new_x.append(_proj(o, w_o, st, res=x, gate=mods[2], name="swa_out"))
    return new_x, extra


def kernel(x_prompt, x_sample, cache_nat_k, cache_nat_v, state_lru, cache_mla_ckv, cache_mla_krope, cache_swa_k, cache_swa_v, c, c_ctx, norm_mix, norm_ffn, w_mod, b_mod, ffn_w_in, ffn_conv_w, ffn_conv_b, ffn_w_out, nat_w_qkv, nat_q_norm, nat_k_norm, nat_rpb, nat_w_o, lru_w_in, lru_conv_w, lru_conv_b, lru_w_a, lru_b_a, lru_w_i, lru_b_i, lru_lambda, lru_w_out, mla_w_down, mla_q_a_norm, mla_kv_a_norm, mla_w_uq, mla_w_ukv, mla_q_norm, mla_k_norm, mla_w_o, swa_w_qkv, swa_q_norm, swa_k_norm, swa_sinks, swa_w_o):
    bc, sc, d = x_prompt.shape
    bl, n, _ = x_sample.shape
    depth = w_mod.shape[0]
    streams = (_Stream(bc, sc, 0, True), _Stream(bl, n, 1, False))
    xs = [x_prompt.reshape(bc * sc, d), x_sample.reshape(bl * n, d)]

    n_cond = 1 + bl
    cond_rows = -(-n_cond // SUBLANE) * SUBLANE
    cond = jnp.zeros((cond_rows, d), F32).at[0].set(c_ctx).at[1:n_cond].set(c)
    mods = _modulation(cond, w_mod, b_mod)[:, :n_cond]

    nat_k_l, nat_v_l, lru_l, ckv_l, krope_l, swa_k_l, swa_v_l = [], [], [], [], [], [], []
    for l in range(depth):
        kind, j = l % 4, l // 4
        m6 = [mods[l, :, None, t * d:(t + 1) * d] for t in range(6)]
        g_mix = norm_mix[l].reshape(1, d)
        if kind == 0:
            xs, (kc, vc) = _mixer_nat(xs, streams, m6, cache_nat_k, cache_nat_v, j, nat_w_qkv[j], g_mix,
                                      nat_q_norm[j], nat_k_norm[j], nat_rpb[j], nat_w_o[j])
            nat_k_l.append(kc)
            nat_v_l.append(vc)
        elif kind == 1:
            xs, st = _mixer_lru(xs, streams, m6, state_lru[:, j], lru_w_in[j], g_mix, lru_conv_w[j], lru_conv_b[j],
                                lru_w_a[j], lru_b_a[j], lru_w_i[j], lru_b_i[j], lru_lambda[j], lru_w_out[j])
            lru_l.append(st)
        elif kind == 2:
            xs, (ckv, kr) = _mixer_mla(xs, streams, m6, cache_mla_ckv[:, j], cache_mla_krope[:, j], mla_w_down[j],
                                       g_mix, mla_q_a_norm[j], mla_kv_a_norm[j], mla_w_uq[j], mla_w_ukv[j],
                                       mla_q_norm[j], mla_k_norm[j], mla_w_o[j])
            ckv_l.append(ckv)
            krope_l.append(kr)
        else:
            xs, (kc, vc) = _mixer_swa(xs, streams, m6, cache_swa_k, cache_swa_v, j, swa_w_qkv[j], g_mix,
                                      swa_q_norm[j], swa_k_norm[j], swa_sinks[j], swa_w_o[j])
            swa_k_l.append(kc)
            swa_v_l.append(vc)
        w_in, w_out = ffn_w_in[l].astype(BF16), ffn_w_out[l].astype(BF16)
        xs = [_conv_ffn(x, st, norm_ffn[l].reshape(1, d), m6[3], m6[4], m6[5], w_in, ffn_conv_w[l],
                        ffn_conv_b[l], w_out) for x, st in zip(xs, streams)]

    return (xs[0].reshape(bc, sc, d), xs[1].reshape(bl, n, d), jnp.stack(nat_k_l, axis=1),
            jnp.stack(nat_v_l, axis=1), jnp.stack(lru_l, axis=1), jnp.stack(ckv_l, axis=1),
            jnp.stack(krope_l, axis=1), jnp.stack(swa_k_l, axis=1), jnp.stack(swa_v_l, axis=1))
```

```python
import functools

import numpy as np
import jax
import jax.numpy as jnp
from jax import lax
from jax.experimental import pallas as pl
from jax.experimental.pallas import tpu as pltpu

F32 = jnp.float32
BF16 = jnp.bfloat16

GRID_W = 64
NA_WIN_ROWS = 8
NA_WIN_COLS = 16
NA_Q_ROWS = 8
NA_K_ROWS = 16
LRU_C = 8.0
SWA_WINDOW = 128
SWA_BLOCK = 128
ROPE_BASE = 10000.0
ROPE_GROUP = 32
EPS = 1e-6
NEG = -1e30
LOG2E = float(np.log2(np.e))
LANE = 128
SUBLANE = 8
HALO = 16
MIB = 1024 * 1024
ROW_TILES = (1024, 512, 256, 128, 64, 32, 16)


def _cparams(n_axes, vmem_mib):
    return pltpu.CompilerParams(dimension_semantics=("arbitrary",) * n_axes,
                                vmem_limit_bytes=int(min(vmem_mib, 60) * MIB))


def _largest_divisor(n, candidates):
    for c in candidates:
        if n % c == 0:
            return c
    return n


class _Stream:
    def __init__(self, nb, seq, mod0, shared_mod):
        self.nb, self.seq, self.rows, self.mod0, self.shared = nb, seq, nb * seq, mod0, shared_mod
        self.bm = _largest_divisor(self.rows if shared_mod else seq, ROW_TILES)

    def mod_index(self, row0):
        return self.mod0 if self.shared else self.mod0 + row0 // self.seq


def _column_tiles(w, bn):
    k, n = w.shape
    return w.reshape(k, n // bn, bn).transpose(1, 0, 2).astype(BF16)


def _norm_mod(x, g, shift, scale):
    ms = jnp.mean(x * x, axis=-1, keepdims=True)
    y = (x * lax.rsqrt(ms + EPS)) * g
    return y * (1.0 + scale) + shift


def _rms(x, g):
    return (x * lax.rsqrt(jnp.mean(x * x, axis=-1, keepdims=True) + EPS)) * g


def _modulation_kernel(c_ref, w_ref, b_ref, o_ref):
    c = c_ref[...]
    sc = (c * jax.nn.sigmoid(c)).astype(BF16)
    o_ref[...] = jnp.dot(sc, w_ref[...].astype(BF16), preferred_element_type=F32) + b_ref[...]


def _modulation(cond, w_mod, b_mod):
    depth, d, n = w_mod.shape
    rows = cond.shape[0]
    bn = _largest_divisor(n, (512, 256, 128))
    return pl.pallas_call(
        _modulation_kernel,
        grid=(depth, n // bn),
        in_specs=[pl.BlockSpec((rows, d), lambda l, j: (0, 0)),
                  pl.BlockSpec((None, d, bn), lambda l, j: (l, 0, j)),
                  pl.BlockSpec((None, 1, bn), lambda l, j: (l, 0, j))],
        out_specs=pl.BlockSpec((None, rows, bn), lambda l, j: (l, 0, j)),
        out_shape=jax.ShapeDtypeStruct((depth, rows, n), F32),
        compiler_params=_cparams(2, 32),
        name="modulation",
    )(cond, w_mod, b_mod.reshape(depth, 1, n))


def _rope_tables(n_tokens, rot_dim, lead, width):
    t = jnp.arange(n_tokens)
    row = (t // GRID_W).astype(F32)
    col = (t % GRID_W).astype(F32)
    half = rot_dim // 2
    inv = ROPE_BASE ** (-jnp.arange(0, half, 2, dtype=F32) / half)
    ar = row[:, None] * inv
    ac = col[:, None] * inv
    ang = jnp.concatenate([ar, ar, ac, ac], axis=-1)
    cos, sin = jnp.cos(ang), jnp.sin(ang)
    first = (np.arange(rot_dim) % ROPE_GROUP) < ROPE_GROUP // 2
    sin_a = jnp.where(first, -sin, 0.0)
    sin_b = jnp.where(first, 0.0, sin)
    pad = ((0, 0), (lead, width - lead - rot_dim))
    return (jnp.pad(cos, pad, constant_values=1.0), jnp.pad(sin_a, pad), jnp.pad(sin_b, pad))


def _rope_apply(y, cos, sin_a, sin_b):
    shift = ROPE_GROUP // 2
    return y * cos + pltpu.roll(y, LANE - shift, 1) * sin_a + pltpu.roll(y, shift, 1) * sin_b


def _fill_lhs(x_ref, xs_ref, xn_ref, prologue, g_ref, sh_ref, sc_ref, row_chunk):
    bm = x_ref.shape[0]

    def chunk(r, carry):
        rows = pl.ds(pl.multiple_of(r * row_chunk, row_chunk), row_chunk)
        x = x_ref[rows, :].astype(F32)
        if prologue == "norm_mod":
            x = _norm_mod(x, g_ref[...], sh_ref[...], sc_ref[...])
        elif prologue == "norm":
            x = _rms(x, g_ref[...])
        if xn_ref is not None:
            xn_ref[rows, :] = x
        xs_ref[rows, :] = x.astype(BF16)
        return carry
    lax.fori_loop(0, bm // row_chunk, chunk, 0)


def _head_norm_store(acc, o_ref, hg_ref, tabs, head_w, norm_div):
    bn = acc.shape[1]
    period = tabs[0].shape[1] if tabs is not None else LANE
    lane = lax.broadcasted_iota(jnp.int32, (1, LANE), 1)
    for s0 in range(0, bn, max(head_w, LANE)):
        tiles = [acc[:, s0 + k * LANE:s0 + (k + 1) * LANE] for k in range(max(head_w, LANE) // LANE)]
        if head_w >= LANE:
            ssq = None
            for y in tiles:
                part = jnp.sum(y * y, axis=-1, keepdims=True)
                ssq = part if ssq is None else ssq + part
            inv = lax.rsqrt(ssq / norm_div + EPS)
        else:
            y2 = tiles[0] * tiles[0]
            low = lane < head_w
            s_lo = jnp.sum(jnp.where(low, y2, 0.0), axis=-1, keepdims=True)
            s_hi = jnp.sum(jnp.where(low, 0.0, y2), axis=-1, keepdims=True)
            inv = jnp.where(low, lax.rsqrt(s_lo / norm_div + EPS), lax.rsqrt(s_hi / norm_div + EPS))
        for k, y in enumerate(tiles):
            c0 = s0 + k * LANE
            y = (y * inv) * hg_ref[:, c0:c0 + LANE]
            if tabs is not None:
                t0 = c0 % period
                y = _rope_apply(y, *(t[:, t0:t0 + LANE] for t in tabs))
            o_ref[:, c0:c0 + LANE] = y.astype(o_ref.dtype)


def _proj_kernel(*refs, prologue, emit_xn, epilogue, head_w, norm_div, norm_cols, rope, row_chunk):
    it = iter(refs)
    x_ref = next(it)
    g_ref = next(it) if prologue is not None else None
    sh_ref, sc_ref = (next(it), next(it)) if prologue == "norm_mod" else (None, None)
    w_ref = next(it)
    if epilogue == "res":
        res_ref, gate_ref = next(it), next(it)
    if epilogue == "heads":
        hg_ref = next(it)
        tabs = (next(it), next(it), next(it)) if rope else None
    o_ref = next(it)
    xn_ref = next(it) if emit_xn else None
    xs_ref = next(it)
    j = pl.program_id(1)
    bn = o_ref.shape[1]

    @pl.when(j == 0)
    def _():
        _fill_lhs(x_ref, xs_ref, xn_ref, prologue, g_ref, sh_ref, sc_ref, row_chunk)

    acc = jnp.dot(xs_ref[...], w_ref[...], preferred_element_type=F32)
    if epilogue == "res":
        o_ref[...] = res_ref[...] + gate_ref[...] * acc
    elif epilogue == "heads":
        @pl.when(j * bn < norm_cols)
        def _():
            _head_norm_store(acc, o_ref, hg_ref, tabs, head_w, norm_div)

        @pl.when(j * bn >= norm_cols)
        def _():
            o_ref[...] = acc.astype(o_ref.dtype)
    else:
        o_ref[...] = acc.astype(o_ref.dtype)


def _proj(x, w, st, *, x_block=None, norm_g=None, mod=None, res=None, gate=None, heads=None, emit_xn=False,
          out_dtype=F32, bn=None, x_time_major=False, out_time_major=False, name="proj"):
    k, n = w.shape
    time_major = x_time_major or out_time_major
    bm = min(st.bm, st.seq) if time_major else st.bm
    tiles_per_seq = st.seq // bm if st.seq % bm == 0 else None
    if time_major:
        assert tiles_per_seq is not None and x_block is None
    if x_time_major:
        assert x.shape == (st.seq, st.nb * k)
        kidx = 0
    else:
        kx, kidx = x_block if x_block is not None else (x.shape[1], 0)
        assert kx == k and x.shape[0] == st.rows
    rows = st.rows
    if bn is None:
        bn = _largest_divisor(n, (512, 256, 128))
    prologue = None if norm_g is None else ("norm_mod" if mod is not None else "norm")
    epilogue = "res" if res is not None else ("heads" if heads is not None else None)
    rope = heads is not None and heads.get("tabs") is not None
    mod_idx = lambda i: st.mod_index(i * bm)

    if x_time_major:
        in_specs = [pl.BlockSpec((bm, k), lambda i, j: (i % tiles_per_seq, i // tiles_per_seq))]
    else:
        in_specs = [pl.BlockSpec((bm, k), lambda i, j: (i, kidx))]
    args = [x]
    if prologue is not None:
        in_specs.append(pl.BlockSpec((1, k), lambda i, j: (0, 0)))
        args.append(norm_g)
    if prologue == "norm_mod":
        in_specs += [pl.BlockSpec((None, 1, k), lambda i, j: (mod_idx(i), 0, 0))] * 2
        args += list(mod)
    in_specs.append(pl.BlockSpec((None, k, bn), lambda i, j: (j, 0, 0)))
    args.append(_column_tiles(w, bn))
    if epilogue == "res":
        in_specs += [pl.BlockSpec((bm, bn), lambda i, j: (i, j)),
                     pl.BlockSpec((None, 1, bn), lambda i, j: (mod_idx(i), 0, j))]
        args += [res, gate]
    head_w = norm_div = norm_cols = 0
    if epilogue == "heads":
        head_w, norm_div, norm_cols = heads["head_w"], heads["norm_div"], heads["norm_cols"]
        assert bn % max(head_w, LANE) == 0 and norm_cols % bn == 0
        in_specs.append(pl.BlockSpec((1, bn), lambda i, j: (0, j)))
        args.append(heads["gains"])
        if rope:
            period = heads["tabs"][0].shape[1]
            assert bn % period == 0 and tiles_per_seq is not None
            in_specs += [pl.BlockSpec((bm, period), lambda i, j: (i % tiles_per_seq, 0))] * 3
            args += list(heads["tabs"])
    if out_time_major:
        n_col_tiles = n // bn
        out_shape = [jax.ShapeDtypeStruct((st.seq, st.nb * n), out_dtype)]
        out_specs = [pl.BlockSpec((bm, bn), lambda i, j: (i % tiles_per_seq, (i // tiles_per_seq) * n_col_tiles + j))]
    else:
        out_shape = [jax.ShapeDtypeStruct((rows, n), out_dtype)]
        out_specs = [pl.BlockSpec((bm, bn), lambda i, j: (i, j))]
    if emit_xn:
        out_shape.append(jax.ShapeDtypeStruct((rows, k), F32))
        out_specs.append(pl.BlockSpec((bm, k), lambda i, j: (i, 0)))
    xbytes = x.dtype.itemsize
    vmem = (2 * bm * k * xbytes + bm * k * 2 + 2 * k * bn * 2 + 6 * bm * bn * 4
            + (2 * bm * k * 4 if emit_xn else 0)) / MIB + 8
    kern = functools.partial(_proj_kernel, prologue=prologue, emit_xn=emit_xn, epilogue=epilogue, head_w=head_w,
                             norm_div=norm_div, norm_cols=norm_cols, rope=rope, row_chunk=min(bm, 128))
    out = pl.pallas_call(
        kern,
        grid=(rows // bm, n // bn),
        in_specs=in_specs,
        out_specs=out_specs,
        out_shape=out_shape,
        scratch_shapes=[pltpu.VMEM((bm, k), BF16)],
        compiler_params=_cparams(2, vmem),
        name=name,
    )(*args)
    return out if emit_xn else out[0]


def _ffn_kernel(xp_ref, x_ref, xn_ref, g_ref, sh_ref, sc_ref, gate_ref, wa_ref, wb_ref, cw_ref, cb_ref,
                wo_ref, o_ref, h_ref, *, bm, seq, row_chunk):
    i = pl.program_id(0)
    c = pl.program_id(1)
    n_chunks = pl.num_programs(1)

    @pl.when(c == 0)
    def _():
        g, sh, sc = g_ref[...], sh_ref[...], sc_ref[...]
        h_ref[0:HALO, :] = _norm_mod(xp_ref[...], g, sh, sc).astype(BF16)
        h_ref[HALO + bm:, :] = _norm_mod(xn_ref[...], g, sh, sc).astype(BF16)

        def chunk(r, carry):
            src = pl.ds(pl.multiple_of(r * row_chunk, row_chunk), row_chunk)
            dst = pl.ds(pl.multiple_of(HALO + r * row_chunk, HALO), row_chunk)
            h_ref[dst, :] = _norm_mod(x_ref[src, :], g, sh, sc).astype(BF16)
            return carry
        lax.fori_loop(0, bm // row_chunk, chunk, 0)
        o_ref[...] = jnp.zeros_like(o_ref)

    ua = jnp.dot(h_ref[...], wa_ref[...], preferred_element_type=F32)
    ub = jnp.dot(h_ref[HALO:HALO + bm, :], wb_ref[...], preferred_element_type=F32)
    n_all = bm + 2 * HALO
    u_prev = pltpu.roll(ua, 1, 0)[HALO:HALO + bm]
    u_next = pltpu.roll(ua, n_all - 1, 0)[HALO:HALO + bm]
    u_mid = ua[HALO:HALO + bm]
    pos = jnp.bitwise_and(i * bm + lax.broadcasted_iota(jnp.int32, (bm, 1), 0), seq - 1)
    u_prev = jnp.where(pos == 0, 0.0, u_prev)
    u_next = jnp.where(pos == seq - 1, 0.0, u_next)
    cw = cw_ref[...]
    a = cb_ref[...] + u_prev * cw[0:1] + u_mid * cw[1:2] + u_next * cw[2:3]
    gated = ((a * jax.nn.sigmoid(a)) * ub).astype(BF16)
    o_ref[...] += jnp.dot(gated, wo_ref[...], preferred_element_type=F32)

    @pl.when(c == n_chunks - 1)
    def _():
        o_ref[...] = x_ref[...] + gate_ref[...] * o_ref[...]


def _conv_ffn(x, st, g, shift, scale, gate, w_in, conv_w, conv_b, w_out, bm=512, ck=512):
    m, d = x.shape
    d_ff = w_out.shape[0]
    bm = min(st.bm, bm)
    ck = _largest_divisor(d_ff, tuple(c for c in (512, 256, 128) if c <= ck))
    n_chunks = d_ff // ck
    n_halo_blocks = m // HALO
    assert st.seq & (st.seq - 1) == 0 and conv_w.shape[0] == 3
    mod_idx = lambda i: st.mod_index(i * bm)
    kern = functools.partial(_ffn_kernel, bm=bm, seq=st.seq, row_chunk=min(bm, 128))
    vmem = (4 * bm * d * 4 + (bm + 2 * HALO) * d * 2 + 6 * d * ck * 2 + 5 * (bm + 2 * HALO) * ck * 4) / MIB + 4
    w_in = _column_tiles(w_in, ck)
    return pl.pallas_call(
        kern,
        grid=(m // bm, n_chunks),
        in_specs=[
            pl.BlockSpec((HALO, d), lambda i, c: (jnp.maximum(i * (bm // HALO) - 1, 0), 0)),
            pl.BlockSpec((bm, d), lambda i, c: (i, 0)),
            pl.BlockSpec((HALO, d), lambda i, c: (jnp.minimum((i + 1) * (bm // HALO), n_halo_blocks - 1), 0)),
            pl.BlockSpec((1, d), lambda i, c: (0, 0)),
            pl.BlockSpec((None, 1, d), lambda i, c: (mod_idx(i), 0, 0)),
            pl.BlockSpec((None, 1, d), lambda i, c: (mod_idx(i), 0, 0)),
            pl.BlockSpec((None, 1, d), lambda i, c: (mod_idx(i), 0, 0)),
            pl.BlockSpec((None, d, ck), lambda i, c: (c, 0, 0)),
            pl.BlockSpec((None, d, ck), lambda i, c: (n_chunks + c, 0, 0)),
            pl.BlockSpec((conv_w.shape[0], ck), lambda i, c: (0, c)),
            pl.BlockSpec((1, ck), lambda i, c: (0, c)),
            pl.BlockSpec((ck, d), lambda i, c: (c, 0)),
        ],
        out_specs=pl.BlockSpec((bm, d), lambda i, c: (i, 0)),
        out_shape=jax.ShapeDtypeStruct((m, d), F32),
        scratch_shapes=[pltpu.VMEM((bm + 2 * HALO, d), BF16)],
        compiler_params=_cparams(2, vmem),
        name="conv_ffn",
    )(x, x, x, g, shift, scale, gate, w_in, w_in, conv_w, conv_b.reshape(1, d_ff), w_out)


def _qk(q, k):
    return lax.dot_general(q, k, (((1,), (1,)), ((), ())), preferred_element_type=F32)


def _attend(scores, values, sink=None):
    m = None
    for s in scores:
        mi = jnp.max(s, axis=-1, keepdims=True)
        m = mi if m is None else jnp.maximum(m, mi)
    if sink is not None:
        m = jnp.maximum(m, sink)
    es = [jnp.exp2(s - m) for s in scores]
    den = None
    for e in es:
        di = jnp.sum(e, axis=-1, keepdims=True)
        den = di if den is None else den + di
    if sink is not None:
        den = den + jnp.exp2(sink - m)
    out = None
    for e, v in zip(es, values):
        oi = jnp.dot(e.astype(BF16), v, preferred_element_type=F32)
        out = oi if out is None else out + oi
    return out * (1.0 / den)


def _ctx_attn_kernel(q_ref, k_ref, v_ref, *outs, heads, dq, dv, emit_kv):
    o_ref = outs[0]
    for h in range(heads):
        q = q_ref[:, h * dq:(h + 1) * dq].astype(BF16)
        k = k_ref[:, h * dq:(h + 1) * dq]
        v = v_ref[:, h * dv:(h + 1) * dv]
        if emit_kv:
            outs[1][h] = k.astype(F32)
            outs[2][h] = v.astype(F32)
        o = _attend([_qk(q, k.astype(BF16))], [v.astype(BF16)])
        o_ref[:, h * dv:(h + 1) * dv] = o.astype(o_ref.dtype)


def _ctx_attention(qm, km, vm, st, *, n_heads, dq, dv, q_col, k_col, v_col, emit_kv=False):
    hb = _largest_divisor(n_heads, (4, 2, 1))
    s = st.seq
    assert q_col % (hb * dq) == 0 and k_col % (hb * dq) == 0 and v_col % (hb * dv) == 0
    qo, ko, vo = q_col // (hb * dq), k_col // (hb * dq), v_col // (hb * dv)
    out_shape = [jax.ShapeDtypeStruct((st.rows, n_heads * dv), BF16)]
    out_specs = [pl.BlockSpec((s, hb * dv), lambda b, g: (b, g))]
    if emit_kv:
        out_shape += [jax.ShapeDtypeStruct((st.nb, n_heads, s, dq), F32),
                      jax.ShapeDtypeStruct((st.nb, n_heads, s, dv), F32)]
        out_specs += [pl.BlockSpec((None, hb, s, dq), lambda b, g: (b, g, 0, 0)),
                      pl.BlockSpec((None, hb, s, dv), lambda b, g: (b, g, 0, 0))]
    out = pl.pallas_call(
        functools.partial(_ctx_attn_kernel, heads=hb, dq=dq, dv=dv, emit_kv=emit_kv),
        grid=(st.nb, n_heads // hb),
        in_specs=[pl.BlockSpec((s, hb * dq), lambda b, g: (b, qo + g)),
                  pl.BlockSpec((s, hb * dq), lambda b, g: (b, ko + g)),
                  pl.BlockSpec((s, hb * dv), lambda b, g: (b, vo + g))],
        out_specs=out_specs,
        out_shape=out_shape,
        compiler_params=_cparams(2, 32),
        name="ctx_attention",
    )(qm, km, vm)
    return out if emit_kv else out[0]


def _nat_kernel(q_ref, k_ref, v_ref, kc_ref, vc_ref, bias_ref, o_ref, *, key_rows, rows, heads, dh):
    i = pl.program_id(2)
    n_keys = key_rows * GRID_W
    first_row = jnp.clip(i * NA_Q_ROWS - NA_WIN_ROWS // 2, 0, rows - key_rows)
    start = pl.multiple_of(first_row * GRID_W, GRID_W * 4)
    for h in range(heads):
        lanes = slice(h * dh, (h + 1) * dh)
        q = q_ref[:, lanes]
        k = k_ref[pl.ds(start, n_keys), lanes]
        v = v_ref[pl.ds(start, n_keys), lanes]
        s_loc = _qk(q, k) + bias_ref[h]
        s_ctx = _qk(q, kc_ref[h].astype(BF16))
        o_ref[:, lanes] = _attend([s_loc, s_ctx], [v, vc_ref[h].astype(BF16)]).astype(o_ref.dtype)


def _nat_bias(rpb, rows):
    n_blocks = rows // NA_Q_ROWS
    key_rows = min(NA_K_ROWS, rows)
    wr = min(NA_WIN_ROWS, rows)
    reps = [0, min(1, n_blocks - 1), n_blocks - 1]
    heads = rpb.shape[0]
    nq, nk = NA_Q_ROWS * GRID_W, key_rows * GRID_W
    shape = (NA_Q_ROWS, GRID_W, key_rows, GRID_W)
    qc = np.arange(GRID_W)
    cstart = np.clip(qc - NA_WIN_COLS // 2, 0, GRID_W - NA_WIN_COLS)
    col_ok = (qc[None, :] >= cstart[:, None]) & (qc[None, :] < cstart[:, None] + NA_WIN_COLS)
    rp = jnp.pad(rpb.astype(F32), ((0, 0), (key_rows, key_rows), (GRID_W - NA_WIN_COLS, GRID_W - NA_WIN_COLS)))
    row_slabs, mask_l = [], []
    for i in reps:
        ks = int(np.clip(i * NA_Q_ROWS - NA_WIN_ROWS // 2, 0, rows - key_rows))
        r = i * NA_Q_ROWS + np.arange(NA_Q_ROWS)
        rs = np.clip(r - wr // 2, 0, rows - wr)
        kr = ks + np.arange(key_rows)
        row_ok = (kr[None, :] >= rs[:, None]) & (kr[None, :] < rs[:, None] + wr)
        for rq in range(NA_Q_ROWS):
            first = ks - int(r[rq]) + NA_WIN_ROWS - 1 + key_rows
            assert 0 <= first and first + key_rows <= rp.shape[1]
            row_slabs.append(rp[:, first:first + key_rows, :])
        mask_l.append(np.broadcast_to(row_ok[:, None, :, None] & col_ok[None, :, None, :], shape).reshape(nq, nk))
    slab = jnp.stack(row_slabs, axis=1).reshape(heads, len(reps), NA_Q_ROWS, key_rows, 2 * GRID_W - 1)
    toep = jnp.stack([slab[..., GRID_W - 1 - c:2 * GRID_W - 1 - c] for c in range(GRID_W)], axis=3)
    bias = toep.reshape(heads, len(reps), nq, nk)
    return jnp.where(jnp.asarray(np.stack(mask_l))[None], bias * LOG2E, NEG)


def _nat_attention(qkv, st, cache_k, cache_v, j, rpb, dh):
    heads = rpb.shape[0]
    n = st.seq
    p = cache_k.shape[3]
    rows = n // GRID_W
    assert rows % NA_Q_ROWS == 0 and rows >= NA_K_ROWS and dh % LANE == 0
    n_blocks = rows // NA_Q_ROWS
    key_rows = min(NA_K_ROWS, rows)
    nq, nk = NA_Q_ROWS * GRID_W, key_rows * GRID_W
    bias = _nat_bias(rpb, rows)
    btype = lambda i: jnp.where(i == 0, 0, jnp.where(i == n_blocks - 1, 2, 1))
    hb = _largest_divisor(heads, (2, 1))
    hg = heads // hb
    kern = functools.partial(_nat_kernel, key_rows=key_rows, rows=rows, heads=hb, dh=dh)
    return pl.pallas_call(
        kern,
        grid=(st.nb, hg, n_blocks),
        in_specs=[pl.BlockSpec((nq, hb * dh), lambda b, h, i: (b * n_blocks + i, h)),
                  pl.BlockSpec((n, hb * dh), lambda b, h, i: (b, hg + h)),
                  pl.BlockSpec((n, hb * dh), lambda b, h, i: (b, 2 * hg + h)),
                  pl.BlockSpec((None, None, hb, p, dh), lambda b, h, i: (b, j, h, 0, 0)),
                  pl.BlockSpec((None, None, hb, p, dh), lambda b, h, i: (b, j, h, 0, 0)),
                  pl.BlockSpec((hb, None, nq, nk), lambda b, h, i: (h, btype(i), 0, 0))],
        out_specs=pl.BlockSpec((nq, hb * dh), lambda b, h, i: (b * n_blocks + i, h)),
        out_shape=jax.ShapeDtypeStruct((st.rows, heads * dh), BF16),
        compiler_params=_cparams(3, 40),
        name="nat_attention",
    )(qkv, qkv, qkv, cache_k, cache_v, bias)


def _joint_dense_kernel(q_ref, k_ref, v_ref, kc_ref, vc_ref, o_ref, *, chunk):
    q = q_ref[...]
    n = k_ref.shape[0]
    pieces = [(k_ref, v_ref, c0, min(chunk, n - c0)) for c0 in range(0, n, chunk)]
    pieces.append((kc_ref, vc_ref, 0, kc_ref.shape[0]))
    m = den = acc = None
    for kr, vr, c0, size in pieces:
        s = _qk(q, kr[c0:c0 + size, :])
        mc = jnp.max(s, axis=-1, keepdims=True)
        m_new = mc if m is None else jnp.maximum(m, mc)
        e = jnp.exp2(s - m_new)
        dc = jnp.sum(e, axis=-1, keepdims=True)
        pv = jnp.dot(e.astype(BF16), vr[c0:c0 + size, :], preferred_element_type=F32)
        if m is None:
            den, acc = dc, pv
        else:
            alpha = jnp.exp2(m - m_new)
            den, acc = alpha * den + dc, alpha * acc + pv
        m = m_new
    o_ref[...] = (acc * (1.0 / den)).astype(o_ref.dtype)


def _joint_dense_attention(qm, km, vm, kcm, vcm, st, p, *, n_heads, dq, dv):
    n = st.seq
    bq = _largest_divisor(n, (512, 256, 128, 64, 32, 16))
    nqb = n // bq
    return pl.pallas_call(
        functools.partial(_joint_dense_kernel, chunk=1024),
        grid=(st.nb, n_heads, nqb),
        in_specs=[pl.BlockSpec((bq, dq), lambda b, h, i: (b * nqb + i, h)),
                  pl.BlockSpec((n, dq), lambda b, h, i: (b, h)),
                  pl.BlockSpec((n, dv), lambda b, h, i: (b, h)),
                  pl.BlockSpec((p, dq), lambda b, h, i: (b, h)),
                  pl.BlockSpec((p, dv), lambda b, h, i: (b, h))],
        out_specs=pl.BlockSpec((bq, dv), lambda b, h, i: (b * nqb + i, h)),
        out_shape=jax.ShapeDtypeStruct((st.rows, n_heads * dv), BF16),
        compiler_params=_cparams(3, 48),
        name="mla_attention",
    )(qm, km, vm, kcm, vcm)


def _both_halves(x, s):
    x = x.astype(F32)
    low = lax.broadcasted_iota(jnp.int32, (1, LANE), 1) < LANE // 2
    keep = low if s == 0 else jnp.logical_not(low)
    return jnp.where(keep, x, pltpu.roll(x, LANE // 2, 1)).astype(BF16)


def _swa_step(sinks_ref, pair, q_ref, k, v, kc, vc, o_ref, *, dh, groups, local_mask):
    kv_per_step = LANE // dh
    assert kv_per_step == 2 and groups % 2 == 0
    rows = q_ref.shape[0]
    low = lax.broadcasted_iota(jnp.int32, (1, LANE), 1) < dh
    row_group = lax.broadcasted_iota(jnp.int32, (groups * rows, 1), 0) // rows
    for s in range(kv_per_step):
        kd, vd = _both_halves(k, s), _both_halves(v, s)
        q_parts = []
        for g in range(groups):
            c0 = ((s * groups + g) * dh // LANE) * LANE
            tile = q_ref[:, c0:c0 + LANE].astype(BF16)
            q_parts.append(jnp.where(low if g % 2 == 0 else jnp.logical_not(low), tile, jnp.zeros_like(tile)))
        q = jnp.concatenate(q_parts, axis=0)
        sink = jnp.zeros((groups * rows, 1), F32)
        for g in range(groups):
            sink = jnp.where(row_group == g, sinks_ref[(pair * kv_per_step + s) * groups + g], sink)
        s_loc = _qk(q, kd)
        if local_mask is not None:
            s_loc = jnp.where(local_mask, s_loc, NEG)
        if kc is not None:
            kcd = jnp.concatenate([kc[s], kc[s]], axis=-1).astype(BF16)
            vcd = jnp.concatenate([vc[s], vc[s]], axis=-1).astype(BF16)
            out = _attend([s_loc, _qk(q, kcd)], [vd, vcd], sink)
        else:
            out = _attend([s_loc], [vd], sink)
        for g in range(0, groups, 2):
            c0 = (s * groups + g) * dh
            o_ref[:, c0:c0 + LANE] = jnp.where(low, out[g * rows:(g + 1) * rows],
                                               out[(g + 1) * rows:(g + 2) * rows]).astype(o_ref.dtype)


def _swa_ctx_kernel(sinks_ref, q_ref, k_ref, v_ref, o_ref, ko_ref, vo_ref, *, dh, groups):
    pair = pl.program_id(1)
    k, v = k_ref[...], v_ref[...]
    for s in range(LANE // dh):
        ko_ref[s] = k[:, s * dh:(s + 1) * dh].astype(F32)
        vo_ref[s] = v[:, s * dh:(s + 1) * dh].astype(F32)
    _swa_step(sinks_ref, pair, q_ref, k, v, None, None, o_ref, dh=dh, groups=groups, local_mask=None)


def _swa_lat_kernel(sinks_ref, q_ref, k_ref, v_ref, kc_ref, vc_ref, o_ref, *, dh, groups, n):
    pair = pl.program_id(1)
    blk = pl.program_id(2)
    n_keys = min(3 * SWA_BLOCK, n)
    start = pl.multiple_of(jnp.clip((blk - 1) * SWA_BLOCK, 0, n - n_keys), SWA_BLOCK)
    k = k_ref[pl.ds(start, n_keys), :]
    v = v_ref[pl.ds(start, n_keys), :]
    qpos = blk * SWA_BLOCK + lax.broadcasted_iota(jnp.int32, (groups * SWA_BLOCK, 1), 0) % SWA_BLOCK
    kpos = start + lax.broadcasted_iota(jnp.int32, (1, n_keys), 1)
    mask = jnp.abs(qpos - kpos) <= SWA_WINDOW
    _swa_step(sinks_ref, pair, q_ref, k, v, kc_ref, vc_ref, o_ref, dh=dh, groups=groups, local_mask=mask)


def _swa_attention(qkv, st, sinks, *, heads, kvh, dh, cache=None):
    groups = heads // kvh
    kv_per_step = LANE // dh
    assert LANE % dh == 0 and kvh % kv_per_step == 0 and groups % kv_per_step == 0
    pairs = kvh // kv_per_step
    qw = kv_per_step * groups * dh
    k_blk = heads * dh // LANE
    v_blk = (heads + kvh) * dh // LANE
    n = st.seq
    common = dict(dh=dh, groups=groups)
    smem = pl.BlockSpec(memory_space=pltpu.SMEM)
    if cache is None:
        out = pl.pallas_call(
            functools.partial(_swa_ctx_kernel, **common),
            grid=(st.nb, pairs),
            in_specs=[smem,
                      pl.BlockSpec((n, qw), lambda b, c: (b, c)),
                      pl.BlockSpec((n, LANE), lambda b, c: (b, k_blk + c)),
                      pl.BlockSpec((n, LANE), lambda b, c: (b, v_blk + c))],
            out_specs=[pl.BlockSpec((n, qw), lambda b, c: (b, c)),
                       pl.BlockSpec((None, kv_per_step, n, dh), lambda b, c: (b, c, 0, 0)),
                       pl.BlockSpec((None, kv_per_step, n, dh), lambda b, c: (b, c, 0, 0))],
            out_shape=[jax.ShapeDtypeStruct((st.rows, heads * dh), BF16),
                       jax.ShapeDtypeStruct((st.nb, kvh, n, dh), F32),
                       jax.ShapeDtypeStruct((st.nb, kvh, n, dh), F32)],
            compiler_params=_cparams(2, 32),
            name="swa_ctx_attention",
        )(sinks, qkv, qkv, qkv)
        return out
    cache_k, cache_v, j = cache
    p = cache_k.shape[3]
    nblk = n // SWA_BLOCK
    assert n % SWA_BLOCK == 0
    return pl.pallas_call(
        functools.partial(_swa_lat_kernel, n=n, **common),
        grid=(st.nb, pairs, nblk),
        in_specs=[smem,
                  pl.BlockSpec((SWA_BLOCK, qw), lambda b, c, i: (b * nblk + i, c)),
                  pl.BlockSpec((n, LANE), lambda b, c, i: (b, k_blk + c)),
                  pl.BlockSpec((n, LANE), lambda b, c, i: (b, v_blk + c)),
                  pl.BlockSpec((None, None, kv_per_step, p, dh), lambda b, c, i: (b, j, c, 0, 0)),
                  pl.BlockSpec((None, None, kv_per_step, p, dh), lambda b, c, i: (b, j, c, 0, 0))],
        out_specs=pl.BlockSpec((SWA_BLOCK, qw), lambda b, c, i: (b * nblk + i, c)),
        out_shape=jax.ShapeDtypeStruct((st.rows, heads * dh), BF16),
        compiler_params=_cparams(3, 32),
        name="swa_attention",
    )(sinks, qkv, qkv, qkv, cache_k, cache_v)


def _mla_kv_kernel(*refs, norm, emit_xn, rope, row_chunk, norm_div):
    it = iter(refs)
    x_ref = next(it)
    g_ref = next(it) if norm else None
    w_ref, kr_ref, g1_ref, g2_ref = next(it), next(it), next(it), next(it)
    tabs = (next(it), next(it), next(it)) if rope else None
    k_ref, v_ref = next(it), next(it)
    xn_ref = next(it) if emit_xn else None
    xs_ref = next(it)

    @pl.when(pl.program_id(1) == 0)
    def _():
        _fill_lhs(x_ref, xs_ref, xn_ref, "norm" if norm else None, g_ref, None, None, row_chunk)

    acc = jnp.dot(xs_ref[...], w_ref[...], preferred_element_type=F32)
    kr = kr_ref[...]
    kr_ssq = jnp.sum(kr * kr, axis=-1, keepdims=True)
    for h in range(acc.shape[1] // (2 * LANE)):
        nope = acc[:, 2 * h * LANE:(2 * h + 1) * LANE]
        inv = lax.rsqrt((jnp.sum(nope * nope, axis=-1, keepdims=True) + kr_ssq) / norm_div + EPS)
        k_rot = (kr * inv) * g2_ref[...]
        if rope:
            k_rot = _rope_apply(k_rot, *(t[...] for t in tabs))
        k_ref[:, 2 * h * LANE:(2 * h + 1) * LANE] = ((nope * inv) * g1_ref[...]).astype(k_ref.dtype)
        k_ref[:, (2 * h + 1) * LANE:(2 * h + 2) * LANE] = k_rot.astype(k_ref.dtype)
        v_ref[:, h * LANE:(h + 1) * LANE] = acc[:, (2 * h + 1) * LANE:(2 * h + 2) * LANE].astype(v_ref.dtype)


def _mla_kv(x, x_block, w_ukv, kr, kr_block, g_kva, g1, g2, tabs, st, *, n_heads, norm_div, emit_xn, name):
    rows = x.shape[0]
    k, n = w_ukv.shape
    head_n = n // n_heads
    assert head_n == 2 * LANE, "nope and value widths must both be one lane tile"
    kx, kidx = x_block
    krw, kridx = kr_block
    assert kx == k and krw == LANE
    bm = st.bm
    norm = g_kva is not None
    rope = tabs is not None
    in_specs = [pl.BlockSpec((bm, k), lambda i, h: (i, kidx))]
    args = [x]
    if norm:
        in_specs.append(pl.BlockSpec((1, k), lambda i, h: (0, 0)))
        args.append(g_kva)
    hb = _largest_divisor(n_heads, (4, 2, 1))
    in_specs += [pl.BlockSpec((None, k, hb * head_n), lambda i, h: (h, 0, 0)),
                 pl.BlockSpec((bm, LANE), lambda i, h: (i, kridx)),
                 pl.BlockSpec((1, LANE), lambda i, h: (0, 0)),
                 pl.BlockSpec((1, LANE), lambda i, h: (0, 0))]
    args += [_column_tiles(w_ukv, hb * head_n), kr, g1, g2]
    if rope:
        tiles_per_seq = st.seq // bm
        in_specs += [pl.BlockSpec((bm, LANE), lambda i, h: (i % tiles_per_seq, 0))] * 3
        args += list(tabs)
    out_shape = [jax.ShapeDtypeStruct((rows, n_heads * 2 * LANE), BF16),
                 jax.ShapeDtypeStruct((rows, n_heads * LANE), BF16)]
    out_specs = [pl.BlockSpec((bm, hb * 2 * LANE), lambda i, h: (i, h)),
                 pl.BlockSpec((bm, hb * LANE), lambda i, h: (i, h))]
    if emit_xn:
        out_shape.append(jax.ShapeDtypeStruct((rows, k), F32))
        out_specs.append(pl.BlockSpec((bm, k), lambda i, h: (i, 0)))
    kern = functools.partial(_mla_kv_kernel, norm=norm, emit_xn=emit_xn, rope=rope, row_chunk=min(bm, 128),
                             norm_div=norm_div)
    return pl.pallas_call(
        kern,
        grid=(rows // bm, n_heads // hb),
        in_specs=in_specs,
        out_specs=out_specs,
        out_shape=out_shape,
        scratch_shapes=[pltpu.VMEM((bm, k), BF16)],
        compiler_params=_cparams(2, 40),
        name=name,
    )(*args)


def _band_plan(width, block):
    n_tiles = width // LANE
    lo = [((t * LANE) // block) * block for t in range(n_tiles)]
    hi = [(((t + 1) * LANE - 1) // block + 1) * block for t in range(n_tiles)]
    start = [(l // LANE) * LANE for l in lo]
    kb = max(-(-(h - s) // LANE) * LANE for h, s in zip(hi, start))
    kb = min(kb, width)
    start = [min(s, width - kb) for s in start]
    return start, kb


def _band_weights(w, width, block, start, kb):
    n_tiles = width // LANE
    wb = w.astype(BF16)
    tiles = []
    for t in range(n_tiles):
        pieces = []
        col = t * LANE
        while col < (t + 1) * LANE:
            blk = col // block
            col_end = min((blk + 1) * block, (t + 1) * LANE)
            sub = wb[blk, :, col - blk * block:col_end - blk * block]
            top = blk * block - start[t]
            pieces.append(jnp.pad(sub, ((top, kb - top - block), (0, 0))))
            col = col_end
        tiles.append(jnp.concatenate(pieces, axis=1))
    return jnp.stack(tiles)


def _gelu_tanh(x):
    cdf = 0.5 * (1.0 + jnp.tanh(np.float32(np.sqrt(2.0 / np.pi)) * (x + 0.044715 * (x * x * x))))
    return x * cdf


def _sigmoid(x):
    return 0.5 * (1.0 + jnp.tanh(0.5 * x))


def _lru_pass_kernel(*refs, reverse, starts, kb, bt, seq, nb, taps):
    left = taps // 2
    right = taps - 1 - left
    it = iter(refs)
    xp_ref, x_ref = next(it), next(it)
    xn_ref = next(it) if right > 0 else None
    cw_ref, cb_ref = next(it), next(it)
    wa_ref, wi_ref, ba_ref, bi_ref, lam_ref, h0_ref = (next(it) for _ in range(6))
    hsf_ref, gate_ref = (next(it), next(it)) if reverse else (None, None)
    out_ref, ht_ref = next(it), next(it)
    xc_s, a_s, bx_s, carry = next(it), next(it), next(it), next(it)

    step = pl.program_id(0)
    n_steps = pl.num_programs(0)
    tile = (n_steps - 1 - step) if reverse else step
    n_tiles = len(starts)
    rows = nb * bt

    @pl.when(step == 0)
    def _():
        carry[...] = h0_ref[...]

    pos = tile * bt + lax.broadcasted_iota(jnp.int32, (bt, 1, 1), 0)
    for t in range(n_tiles):
        lanes = slice(t * LANE, (t + 1) * LANE)
        parts = [xp_ref[:, :, lanes], x_ref[:, :, lanes]] + ([xn_ref[:, :, lanes]] if right > 0 else [])
        full = jnp.concatenate(parts, axis=0)
        acc = jnp.broadcast_to(cb_ref[:, lanes], (bt, nb, LANE))
        for k in range(taps):
            off = k - left
            shifted = full[k:k + bt]
            if off != 0:
                shifted = jnp.where((pos + off >= 0) & (pos + off < seq), shifted, 0.0)
            acc = acc + shifted * cw_ref[k:k + 1, lanes]
        xc_s[:, lanes] = acc.reshape(rows, LANE)

    neg_lam = -lam_ref[...]
    softplus = jnp.maximum(neg_lam, 0.0) + jnp.log1p(jnp.exp(-jnp.abs(neg_lam)))
    for t in range(n_tiles):
        lanes = slice(t * LANE, (t + 1) * LANE)
        xw = xc_s[:, starts[t]:starts[t] + kb].astype(BF16)
        r = _sigmoid(jnp.dot(xw, wa_ref[t], preferred_element_type=F32) + ba_ref[:, lanes])
        ig = _sigmoid(jnp.dot(xw, wi_ref[t], preferred_element_type=F32) + bi_ref[:, lanes])
        log_a = -LRU_C * r * softplus[:, lanes]
        a = jnp.exp(log_a)
        a_s[:, lanes] = a
        bx_s[:, lanes] = jnp.sqrt(-jnp.tanh(log_a) * (a * a + 1.0)) * (ig * xc_s[:, lanes])

    h = carry[...]
    for s in range(bt):
        ts = (bt - 1 - s) if reverse else s
        slab = slice(ts * nb, (ts + 1) * nb)
        h = a_s[slab, :] * h + bx_s[slab, :]
        a_s[slab, :] = h
    carry[...] = h
    ht_ref[...] = h
    hs = a_s[...].reshape(bt, nb, a_s.shape[1])
    if reverse:
        out_ref[...] = (_gelu_tanh(gate_ref[...]) * (hsf_ref[...] + hs)).astype(out_ref.dtype)
    else:
        out_ref[...] = hs


def _lru_pass(u, st, conv_w, conv_b, wa, wi, b_a, b_i, lam, h0, starts, kb, *, reverse, hs_fwd=None):
    c = conv_w.shape[1]
    taps = conv_w.shape[0]
    left, right = taps // 2, taps - 1 - taps // 2
    nb, seq = st.nb, st.seq
    bt = min(max(256 // nb, SUBLANE), seq)
    assert seq % bt == 0 and c % LANE == 0 and left > 0 and bt % left == 0 and (right == 0 or bt % right == 0)
    nt = seq // bt
    n_tiles = c // LANE
    u3 = u.reshape(seq, nb, 2 * c)
    tmap = (lambda s: nt - 1 - s) if reverse else (lambda s: s)
    full = lambda *shape: pl.BlockSpec(shape, lambda s: (0,) * len(shape))
    in_specs = [pl.BlockSpec((left, nb, c), lambda s: (jnp.maximum(tmap(s) * (bt // left) - 1, 0), 0, 0)),
                pl.BlockSpec((bt, nb, c), lambda s: (tmap(s), 0, 0))]
    args = [u3, u3]
    if right > 0:
        in_specs.append(pl.BlockSpec((right, nb, c),
                                     lambda s: (jnp.minimum((tmap(s) + 1) * (bt // right), seq // right - 1), 0, 0)))
        args.append(u3)
    in_specs += [full(taps, c), full(1, c), full(n_tiles, kb, LANE), full(n_tiles, kb, LANE), full(1, c), full(1, c),
                 full(1, c), full(nb, c)]
    args += [conv_w, conv_b.reshape(1, c), wa, wi, b_a.reshape(1, c), b_i.reshape(1, c), lam.reshape(1, c), h0]
    if reverse:
        in_specs += [pl.BlockSpec((bt, nb, c), lambda s: (tmap(s), 0, 0)),
                     pl.BlockSpec((bt, nb, c), lambda s: (tmap(s), 0, 1))]
        args += [hs_fwd, u3]
    kern = functools.partial(_lru_pass_kernel, reverse=reverse, starts=tuple(starts), kb=kb, bt=bt, seq=seq, nb=nb,
                             taps=taps)
    blk = nb * bt * c * 4 / MIB
    vmem = (2 + 2 + 3 + (4 if reverse else 0) + 4) * blk + 4 * n_tiles * kb * LANE * 2 / MIB + 8
    return pl.pallas_call(
        kern,
        grid=(nt,),
        in_specs=in_specs,
        out_specs=[pl.BlockSpec((bt, nb, c), lambda s: (tmap(s), 0, 0)),
                   pl.BlockSpec((nb, c), lambda s: (0, 0))],
        out_shape=[jax.ShapeDtypeStruct((seq, nb, c), F32),
                   jax.ShapeDtypeStruct((nb, c), F32)],
        scratch_shapes=[pltpu.VMEM((nb * bt, c), F32), pltpu.VMEM((nb * bt, c), F32),
                        pltpu.VMEM((nb * bt, c), F32), pltpu.VMEM((nb, c), F32)],
        compiler_params=_cparams(1, vmem),
        name="lru_bwd" if reverse else "lru_fwd",
    )(*args)


def _mixer_nat(xs, streams, mods, cache_k, cache_v, j, w_qkv, g_mix, g_q, g_k, rpb, w_o):
    heads, dh = rpb.shape[0], g_q.shape[0]
    w_qkv, w_o = w_qkv.astype(BF16), w_o.astype(BF16)
    gains = jnp.concatenate([jnp.tile(g_q * (dh ** -0.5 * LOG2E), heads), jnp.tile(g_k, heads),
                             jnp.ones((heads * dh,), F32)])[None]
    spec = dict(head_w=dh, norm_div=dh, norm_cols=2 * heads * dh, gains=gains)
    new_x, extra = [], None
    for x, st in zip(xs, streams):
        latent = not st.shared
        qkv = _proj(x, w_qkv, st, norm_g=g_mix, mod=(mods[0], mods[1]), heads=spec,
                    out_dtype=BF16 if latent else F32, name="nat_qkv")
        if latent:
            o = _nat_attention(qkv, st, cache_k, cache_v, j, rpb, dh)
        else:
            o, kc, vc = _ctx_attention(qkv, qkv, qkv, st, n_heads=heads, dq=dh, dv=dh, q_col=0, k_col=heads * dh,
                                       v_col=2 * heads * dh, emit_kv=True)
            extra = (kc, vc)
        new_x.append(_proj(o, w_o, st, res=x, gate=mods[2], name="nat_out"))
    return new_x, extra


def _mixer_lru(xs, streams, mods, state, w_in, g_mix, conv_w, conv_b, w_a, b_a, w_i, b_i, lam, w_out):
    c = conv_w.shape[1]
    block = w_a.shape[-1]
    w_in, w_out = w_in.astype(BF16), w_out.astype(BF16)
    starts, kb = _band_plan(c, block)
    wa = [_band_weights(w_a[d], c, block, starts, kb) for d in range(2)]
    wi = [_band_weights(w_i[d], c, block, starts, kb) for d in range(2)]
    new_x, st_out = [], None
    for x, st in zip(xs, streams):
        latent = not st.shared
        h0 = state.astype(F32) if latent else jnp.zeros((st.nb, 2, c), F32)
        u = _proj(x, w_in, st, norm_g=g_mix, mod=(mods[0], mods[1]), out_time_major=True, name="lru_in")
        hs_f, t_f = _lru_pass(u, st, conv_w, conv_b, wa[0], wi[0], b_a[0], b_i[0], lam[0], h0[:, 0], starts, kb,
                              reverse=False)
        y, t_b = _lru_pass(u, st, conv_w, conv_b, wa[1], wi[1], b_a[1], b_i[1], lam[1], h0[:, 1], starts, kb,
                           reverse=True, hs_fwd=hs_f)
        if not latent:
            st_out = jnp.stack([t_f, t_b], axis=1)
        new_x.append(_proj(y.reshape(st.seq, st.nb * c), w_out, st, res=x, gate=mods[2], x_time_major=True,
                           name="lru_out"))
    return new_x, st_out


def _mixer_mla(xs, streams, mods, cache_ckv, cache_kr, w_down, g_mix, g_qa, g_kva, w_uq, w_ukv, g_q, g_k, w_o):
    d_model = w_down.shape[0]
    q_rank, kv_rank = g_qa.shape[0], g_kva.shape[0]
    qk_dim = g_q.shape[0]
    heads = w_uq.shape[1] // qk_dim
    rope = w_down.shape[1] - q_rank - kv_rank
    nope = qk_dim - rope
    assert nope == LANE and rope <= LANE and kv_rank % LANE == 0 and q_rank % LANE == 0
    head_w = 2 * LANE
    q_pad = -q_rank % kv_rank
    kv_col = q_rank + q_pad
    tail_pad = -(kv_col + kv_rank + rope) % 512
    w_dn = jnp.concatenate([w_down[:, :q_rank], jnp.zeros((d_model, q_pad), F32),
                            w_down[:, q_rank:q_rank + kv_rank], w_down[:, q_rank + kv_rank:],
                            jnp.zeros((d_model, tail_pad), F32)], axis=1).astype(BF16)
    kr_blk = (kv_col + kv_rank) // LANE
    w_q = jnp.pad(w_uq.reshape(q_rank, heads, qk_dim), ((0, 0), (0, 0), (0, head_w - qk_dim)))
    w_q = w_q.reshape(q_rank, heads * head_w).astype(BF16)
    w_ukv, w_o = w_ukv.astype(BF16), w_o.astype(BF16)
    gq = jnp.tile(jnp.pad(g_q * (qk_dim ** -0.5 * LOG2E), (0, head_w - qk_dim)), heads)[None]
    g1, g2 = g_k[None, :nope], jnp.pad(g_k[nope:], (0, LANE - rope))[None]
    p = cache_ckv.shape[1]
    new_x, extra = [], None
    for x, st in zip(xs, streams):
        latent = not st.shared
        d = _proj(x, w_dn, st, norm_g=g_mix, mod=(mods[0], mods[1]), name="mla_down")
        q_tabs = _rope_tables(st.seq, rope, nope, head_w) if latent else None
        k_tabs = _rope_tables(st.seq, rope, 0, LANE) if latent else None
        q = _proj(d, w_q, st, x_block=(q_rank, 0), norm_g=g_qa[None],
                  heads=dict(head_w=head_w, norm_div=qk_dim, norm_cols=heads * head_w, gains=gq, tabs=q_tabs),
                  out_dtype=BF16, bn=_largest_divisor(heads * head_w, (1024, 512, 256)), name="mla_uq")
        kv = _mla_kv(d, (kv_rank, kv_col // kv_rank), w_ukv, d, (LANE, kr_blk), g_kva[None], g1, g2, k_tabs, st,
                     n_heads=heads, norm_div=qk_dim, emit_xn=not latent, name="mla_ukv")
        if latent:
            k, v = kv
            cst = _Stream(st.nb, p, 0, True)
            krc = jnp.pad(cache_kr.reshape(st.nb * p, rope), ((0, 0), (0, LANE - rope)))
            kc, vc = _mla_kv(cache_ckv.reshape(st.nb * p, kv_rank), (kv_rank, 0), w_ukv, krc, (LANE, 0), None,
                             g1, g2, None, cst, n_heads=heads, norm_div=qk_dim, emit_xn=False,
                             name="mla_ukv_cache")
            o = _joint_dense_attention(q, k, v, kc, vc, st, p, n_heads=heads, dq=head_w, dv=LANE)
        else:
            k, v, ckv = kv
            o = _ctx_attention(q, k, v, st, n_heads=heads, dq=head_w, dv=LANE, q_col=0, k_col=0, v_col=0)
            kr_out = d[:, kv_col + kv_rank:kv_col + kv_rank + rope]
            extra = (ckv.reshape(st.nb, st.seq, kv_rank), kr_out.reshape(st.nb, st.seq, rope))
        new_x.append(_proj(o, w_o, st, res=x, gate=mods[2], name="mla_out"))
    return new_x, extra


def _mixer_swa(xs, streams, mods, cache_k, cache_v, j, w_qkv, g_mix, g_q, g_k, sinks, w_o):
    dh = g_q.shape[0]
    heads = sinks.shape[0]
    kvh = (w_qkv.shape[1] // dh - heads) // 2
    w_qkv, w_o = w_qkv.astype(BF16), w_o.astype(BF16)
    gains = jnp.concatenate([jnp.tile(g_q * (dh ** -0.5 * LOG2E), heads), jnp.tile(g_k, kvh),
                             jnp.ones((kvh * dh,), F32)])[None]
    sinks = sinks.astype(F32) * LOG2E
    new_x, extra = [], None
    for x, st in zip(xs, streams):
        latent = not st.shared
        tabs = _rope_tables(st.seq, dh, 0, dh) if latent else None
        if tabs is not None:
            tabs = tuple(jnp.tile(t, (1, LANE // dh)) for t in tabs)
        spec = dict(head_w=dh, norm_div=dh, norm_cols=(heads + kvh) * dh, gains=gains, tabs=tabs)
        qkv = _proj(x, w_qkv, st, norm_g=g_mix, mod=(mods[0], mods[1]), heads=spec,
                    out_dtype=BF16 if latent else F32, bn=kvh * dh, name="swa_qkv")
        if latent:
            o = _swa_attention(qkv, st, sinks, heads=heads, kvh=kvh, dh=dh, cache=(cache_k, cache_v, j))
        else:
            o, kc, vc = _swa_attention(qkv, st, sinks, heads=heads, kvh=kvh, dh=dh)
            extra = (kc, vc)
        new_x.append(_proj(o, w_o, st, res=x, gate=mods[2], name="swa_out"))
    return new_x, extra


def kernel(x_prompt, x_sample, cache_nat_k, cache_nat_v, state_lru, cache_mla_ckv, cache_mla_krope, cache_swa_k, cache_swa_v, c, c_ctx, norm_mix, norm_ffn, w_mod, b_mod, ffn_w_in, ffn_conv_w, ffn_conv_b, ffn_w_out, nat_w_qkv, nat_q_norm, nat_k_norm, nat_rpb, nat_w_o, lru_w_in, lru_conv_w, lru_conv_b, lru_w_a, lru_b_a, lru_w_i, lru_b_i, lru_lambda, lru_w_out, mla_w_down, mla_q_a_norm, mla_kv_a_norm, mla_w_uq, mla_w_ukv, mla_q_norm, mla_k_norm, mla_w_o, swa_w_qkv, swa_q_norm, swa_k_norm, swa_sinks, swa_w_o):
    bc, sc, d = x_prompt.shape
    bl, n, _ = x_sample.shape
    depth = w_mod.shape[0]
    streams = (_Stream(bc, sc, 0, True), _Stream(bl, n, 1, False))
    xs = [x_prompt.reshape(bc * sc, d), x_sample.reshape(bl * n, d)]

    n_cond = 1 + bl
    cond_rows = -(-n_cond // SUBLANE) * SUBLANE
    cond = jnp.zeros((cond_rows, d), F32).at[0].set(c_ctx).at[1:n_cond].set(c)
    mods = _modulation(cond, w_mod, b_mod)[:, :n_cond]

    nat_k_l, nat_v_l, lru_l, ckv_l, krope_l, swa_k_l, swa_v_l = [], [], [], [], [], [], []
    for l in range(depth):
        kind, j = l % 4, l // 4
        m6 = [mods[l, :, None, t * d:(t + 1) * d] for t in range(6)]
        g_mix = norm_mix[l].reshape(1, d)
        if kind == 0:
            xs, (kc, vc) = _mixer_nat(xs, streams, m6, cache_nat_k, cache_nat_v, j, nat_w_qkv[j], g_mix,
                                      nat_q_norm[j], nat_k_norm[j], nat_rpb[j], nat_w_o[j])
            nat_k_l.append(kc)
            nat_v_l.append(vc)
        elif kind == 1:
            xs, st = _mixer_lru(xs, streams, m6, state_lru[:, j], lru_w_in[j], g_mix, lru_conv_w[j], lru_conv_b[j],
                                lru_w_a[j], lru_b_a[j], lru_w_i[j], lru_b_i[j], lru_lambda[j], lru_w_out[j])
            lru_l.append(st)
        elif kind == 2:
            xs, (ckv, kr) = _mixer_mla(xs, streams, m6, cache_mla_ckv[:, j], cache_mla_krope[:, j], mla_w_down[j],
                                       g_mix, mla_q_a_norm[j], mla_kv_a_norm[j], mla_w_uq[j], mla_w_ukv[j],
                                       mla_q_norm[j], mla_k_norm[j], mla_w_o[j])
            ckv_l.append(ckv)
            krope_l.append(kr)
        else:
            xs, (kc, vc) = _mixer_swa(xs, streams, m6, cache_swa_k, cache_swa_v, j, swa_w_qkv[j], g_mix,
                                      swa_q_norm[j], swa_k_norm[j], swa_sinks[j], swa_w_o[j])
            swa_k_l.append(kc)
            swa_v_l.append(vc)
        w_in, w_out = ffn_w_in[l].astype(BF16), ffn_w_out[l].astype(BF16)
        ffn_bm, ffn_ck = ((512, 512), (1024, 256), (512, 256), (1024, 512))[l % 4]
        xs = [_conv_ffn(x, st, norm_ffn[l].reshape(1, d), m6[3], m6[4], m6[5], w_in, ffn_conv_w[l],
                        ffn_conv_b[l], w_out, bm=512 if st.shared else ffn_bm, ck=512 if st.shared else ffn_ck)
              for x, st in zip(xs, streams)]

    return (xs[0].reshape(bc, sc, d), xs[1].reshape(bl, n, d), jnp.stack(nat_k_l, axis=1),
            jnp.stack(nat_v_l, axis=1), jnp.stack(lru_l, axis=1), jnp.stack(ckv_l, axis=1),
            jnp.stack(krope_l, axis=1), jnp.stack(swa_k_l, axis=1), jnp.stack(swa_v_l, axis=1))
```

```python
import functools

import numpy as np
import jax
import jax.numpy as jnp
from jax import lax
from jax.experimental import pallas as pl
from jax.experimental.pallas import tpu as pltpu

F32 = jnp.float32
BF16 = jnp.bfloat16

GRID_W = 64
NA_WIN_ROWS = 8
NA_WIN_COLS = 16
NA_Q_ROWS = 8
NA_K_ROWS = 16
LRU_C = 8.0
SWA_WINDOW = 128
SWA_BLOCK = 128
ROPE_BASE = 10000.0
ROPE_GROUP = 32
EPS = 1e-6
NEG = -1e30
LOG2E = float(np.log2(np.e))
LANE = 128
SUBLANE = 8
HALO = 16
MIB = 1024 * 1024
ROW_TILES = (1024, 512, 256, 128, 64, 32, 16)


def _cparams(n_axes, vmem_mib):
    return pltpu.CompilerParams(dimension_semantics=("arbitrary",) * n_axes,
                                vmem_limit_bytes=int(min(vmem_mib, 60) * MIB))


def _largest_divisor(n, candidates):
    for c in candidates:
        if n % c == 0:
            return c
    return n


class _Stream:
    def __init__(self, nb, seq, mod0, shared_mod):
        self.nb, self.seq, self.rows, self.mod0, self.shared = nb, seq, nb * seq, mod0, shared_mod
        self.bm = _largest_divisor(self.rows if shared_mod else seq, ROW_TILES)

    def mod_index(self, row0):
        return self.mod0 if self.shared else self.mod0 + row0 // self.seq


def _norm_mod(x, g, shift, scale):
    ms = jnp.mean(x * x, axis=-1, keepdims=True)
    y = (x * lax.rsqrt(ms + EPS)) * g
    return y * (1.0 + scale) + shift


def _rms(x, g):
    return (x * lax.rsqrt(jnp.mean(x * x, axis=-1, keepdims=True) + EPS)) * g


def _modulation_kernel(c_ref, w_ref, b_ref, o_ref):
    c = c_ref[...]
    sc = (c * jax.nn.sigmoid(c)).astype(BF16)
    o_ref[...] = jnp.dot(sc, w_ref[...].astype(BF16), preferred_element_type=F32) + b_ref[...]


def _modulation(cond, w_mod, b_mod):
    depth, d, n = w_mod.shape
    rows = cond.shape[0]
    bn = _largest_divisor(n, (512, 256, 128))
    return pl.pallas_call(
        _modulation_kernel,
        grid=(depth, n // bn),
        in_specs=[pl.BlockSpec((rows, d), lambda l, j: (0, 0)),
                  pl.BlockSpec((None, d, bn), lambda l, j: (l, 0, j)),
                  pl.BlockSpec((None, 1, bn), lambda l, j: (l, 0, j))],
        out_specs=pl.BlockSpec((None, rows, bn), lambda l, j: (l, 0, j)),
        out_shape=jax.ShapeDtypeStruct((depth, rows, n), F32),
        compiler_params=_cparams(2, 32),
        name="modulation",
    )(cond, w_mod, b_mod.reshape(depth, 1, n))


def _rope_tables(n_tokens, rot_dim, lead, width):
    t = jnp.arange(n_tokens)
    row = (t // GRID_W).astype(F32)
    col = (t % GRID_W).astype(F32)
    half = rot_dim // 2
    inv = ROPE_BASE ** (-jnp.arange(0, half, 2, dtype=F32) / half)
    ar = row[:, None] * inv
    ac = col[:, None] * inv
    ang = jnp.concatenate([ar, ar, ac, ac], axis=-1)
    cos, sin = jnp.cos(ang), jnp.sin(ang)
    first = (np.arange(rot_dim) % ROPE_GROUP) < ROPE_GROUP // 2
    sin_a = jnp.where(first, -sin, 0.0)
    sin_b = jnp.where(first, 0.0, sin)
    pad = ((0, 0), (lead, width - lead - rot_dim))
    return (jnp.pad(cos, pad, constant_values=1.0), jnp.pad(sin_a, pad), jnp.pad(sin_b, pad))


def _rope_apply(y, cos, sin_a, sin_b):
    shift = ROPE_GROUP // 2
    return y * cos + pltpu.roll(y, LANE - shift, 1) * sin_a + pltpu.roll(y, shift, 1) * sin_b


def _fill_lhs(x_ref, xs_ref, xn_ref, prologue, g_ref, sh_ref, sc_ref, row_chunk):
    bm = x_ref.shape[0]

    def chunk(r, carry):
        rows = pl.ds(pl.multiple_of(r * row_chunk, row_chunk), row_chunk)
        x = x_ref[rows, :].astype(F32)
        if prologue == "norm_mod":
            x = _norm_mod(x, g_ref[...], sh_ref[...], sc_ref[...])
        elif prologue == "norm":
            x = _rms(x, g_ref[...])
        if xn_ref is not None:
            xn_ref[rows, :] = x
        xs_ref[rows, :] = x.astype(BF16)
        return carry
    lax.fori_loop(0, bm // row_chunk, chunk, 0)


def _head_norm_store(acc, o_ref, hg_ref, tabs, head_w, norm_div, col0, norm_cols, rope_tiles):
    bn = acc.shape[1]
    period = tabs[0].shape[1] if tabs is not None else LANE
    lane = lax.broadcasted_iota(jnp.int32, (1, LANE), 1)
    for s0 in range(0, bn, max(head_w, LANE)):
        normed = None if norm_cols is None else (col0 + s0 < norm_cols)
        tiles = [acc[:, s0 + k * LANE:s0 + (k + 1) * LANE] for k in range(max(head_w, LANE) // LANE)]
        if head_w >= LANE:
            ssq = None
            for y in tiles:
                part = jnp.sum(y * y, axis=-1, keepdims=True)
                ssq = part if ssq is None else ssq + part
            inv = lax.rsqrt(ssq / norm_div + EPS)
        else:
            y2 = tiles[0] * tiles[0]
            low = lane < head_w
            s_lo = jnp.sum(jnp.where(low, y2, 0.0), axis=-1, keepdims=True)
            s_hi = jnp.sum(jnp.where(low, 0.0, y2), axis=-1, keepdims=True)
            inv = jnp.where(low, lax.rsqrt(s_lo / norm_div + EPS), lax.rsqrt(s_hi / norm_div + EPS))
        if normed is not None:
            inv = jnp.where(normed, inv, 1.0)
        for k, y in enumerate(tiles):
            c0 = s0 + k * LANE
            y = (y * inv) * hg_ref[:, c0:c0 + LANE]
            t0 = c0 % period
            if tabs is not None and rope_tiles[t0 // LANE]:
                rotated = _rope_apply(y, *(t[:, t0:t0 + LANE] for t in tabs))
                y = rotated if normed is None else jnp.where(normed, rotated, y)
            o_ref[:, c0:c0 + LANE] = y.astype(o_ref.dtype)


def _proj_kernel(*refs, prologue, emit_xn, epilogue, head_w, norm_div, norm_cols, rope, rope_tiles, row_chunk):
    it = iter(refs)
    x_ref = next(it)
    g_ref = next(it) if prologue is not None else None
    sh_ref, sc_ref = (next(it), next(it)) if prologue == "norm_mod" else (None, None)
    w_ref = next(it)
    if epilogue == "res":
        res_ref, gate_ref = next(it), next(it)
    if epilogue == "heads":
        hg_ref = next(it)
        tabs = (next(it), next(it), next(it)) if rope else None
    o_ref = next(it)
    xn_ref = next(it) if emit_xn else None
    xs_ref = next(it)
    j = pl.program_id(1)
    bn = o_ref.shape[1]

    @pl.when(j == 0)
    def _():
        _fill_lhs(x_ref, xs_ref, xn_ref, prologue, g_ref, sh_ref, sc_ref, row_chunk)

    acc = jnp.dot(xs_ref[...], w_ref[...], preferred_element_type=F32)
    if epilogue == "res":
        o_ref[...] = res_ref[...] + gate_ref[...] * acc
    elif epilogue == "heads":
        _head_norm_store(acc, o_ref, hg_ref, tabs, head_w, norm_div, j * bn, norm_cols, rope_tiles)
    else:
        o_ref[...] = acc.astype(o_ref.dtype)


def _proj(x, w, st, *, x_block=None, norm_g=None, mod=None, res=None, gate=None, heads=None, emit_xn=False,
          out_dtype=F32, bn=None, x_time_major=False, out_time_major=False, name="proj"):
    k, n = w.shape
    time_major = x_time_major or out_time_major
    bm = min(st.bm, st.seq) if time_major else st.bm
    tiles_per_seq = st.seq // bm if st.seq % bm == 0 else None
    if time_major:
        assert tiles_per_seq is not None and x_block is None
    if x_time_major:
        assert x.shape == (st.seq, st.nb * k)
        kidx = 0
    else:
        kx, kidx = x_block if x_block is not None else (x.shape[1], 0)
        assert kx == k and x.shape[0] == st.rows
    rows = st.rows
    prologue = None if norm_g is None else ("norm_mod" if mod is not None else "norm")
    epilogue = "res" if res is not None else ("heads" if heads is not None else None)
    rope = heads is not None and heads.get("tabs") is not None
    if bn is None:
        unit = LANE
        if epilogue == "heads":
            unit = max(heads["head_w"], LANE, heads["tabs"][0].shape[1] if rope else LANE)
        cap = 512 if epilogue == "res" else 1024
        bn = next((c for c in range(cap, unit - 1, -unit) if n % c == 0), n)
    mod_idx = lambda i: st.mod_index(i * bm)

    if x_time_major:
        in_specs = [pl.BlockSpec((bm, k), lambda i, j: (i % tiles_per_seq, i // tiles_per_seq))]
    else:
        in_specs = [pl.BlockSpec((bm, k), lambda i, j: (i, kidx))]
    args = [x]
    if prologue is not None:
        in_specs.append(pl.BlockSpec((1, k), lambda i, j: (0, 0)))
        args.append(norm_g)
    if prologue == "norm_mod":
        in_specs += [pl.BlockSpec((None, 1, k), lambda i, j: (mod_idx(i), 0, 0))] * 2
        args += list(mod)
    in_specs.append(pl.BlockSpec((k, bn), lambda i, j: (0, j)))
    args.append(w.astype(BF16))
    if epilogue == "res":
        in_specs += [pl.BlockSpec((bm, bn), lambda i, j: (i, j)),
                     pl.BlockSpec((None, 1, bn), lambda i, j: (mod_idx(i), 0, j))]
        args += [res, gate]
    head_w = norm_div = 0
    norm_cols = rope_tiles = None
    if epilogue == "heads":
        head_w, norm_div = heads["head_w"], heads["norm_div"]
        norm_cols = heads["norm_cols"] if heads["norm_cols"] < n else None
        assert bn % max(head_w, LANE) == 0 and heads["norm_cols"] % max(head_w, LANE) == 0
        in_specs.append(pl.BlockSpec((1, bn), lambda i, j: (0, j)))
        args.append(heads["gains"])
        if rope:
            period = heads["tabs"][0].shape[1]
            rope_tiles = heads["rope_tiles"]
            assert bn % period == 0 and tiles_per_seq is not None and len(rope_tiles) == period // LANE
            in_specs += [pl.BlockSpec((bm, period), lambda i, j: (i % tiles_per_seq, 0))] * 3
            args += list(heads["tabs"])
    if out_time_major:
        n_col_tiles = n // bn
        out_shape = [jax.ShapeDtypeStruct((st.seq, st.nb * n), out_dtype)]
        out_specs = [pl.BlockSpec((bm, bn), lambda i, j: (i % tiles_per_seq, (i // tiles_per_seq) * n_col_tiles + j))]
    else:
        out_shape = [jax.ShapeDtypeStruct((rows, n), out_dtype)]
        out_specs = [pl.BlockSpec((bm, bn), lambda i, j: (i, j))]
    if emit_xn:
        out_shape.append(jax.ShapeDtypeStruct((rows, k), F32))
        out_specs.append(pl.BlockSpec((bm, k), lambda i, j: (i, 0)))
    xbytes = x.dtype.itemsize
    vmem = (2 * bm * k * xbytes + bm * k * 2 + 2 * k * bn * 2 + (6 if epilogue == "res" else 4) * bm * bn * 4
            + (2 * bm * k * 4 if emit_xn else 0)) / MIB + 8
    kern = functools.partial(_proj_kernel, prologue=prologue, emit_xn=emit_xn, epilogue=epilogue, head_w=head_w,
                             norm_div=norm_div, norm_cols=norm_cols, rope=rope, rope_tiles=rope_tiles,
                             row_chunk=min(bm, 128))
    out = pl.pallas_call(
        kern,
        grid=(rows // bm, n // bn),
        in_specs=in_specs,
        out_specs=out_specs,
        out_shape=out_shape,
        scratch_shapes=[pltpu.VMEM((bm, k), BF16)],
        compiler_params=_cparams(2, vmem),
        name=name,
    )(*args)
    return out if emit_xn else out[0]


def _ffn_kernel(xp_ref, x_ref, xn_ref, g_ref, sh_ref, sc_ref, gate_ref, wa_ref, wb_ref, cw_ref, cb_ref,
                wo_ref, o_ref, h_ref, *, bm, seq, row_chunk):
    i = pl.program_id(0)
    c = pl.program_id(1)
    n_chunks = pl.num_programs(1)

    @pl.when(c == 0)
    def _():
        g, sh, sc = g_ref[...], sh_ref[...], sc_ref[...]
        h_ref[0:HALO, :] = _norm_mod(xp_ref[...], g, sh, sc).astype(BF16)
        h_ref[HALO + bm:, :] = _norm_mod(xn_ref[...], g, sh, sc).astype(BF16)

        def chunk(r, carry):
            src = pl.ds(pl.multiple_of(r * row_chunk, row_chunk), row_chunk)
            dst = pl.ds(pl.multiple_of(HALO + r * row_chunk, HALO), row_chunk)
            h_ref[dst, :] = _norm_mod(x_ref[src, :], g, sh, sc).astype(BF16)
            return carry
        lax.fori_loop(0, bm // row_chunk, chunk, 0)
        o_ref[...] = jnp.zeros_like(o_ref)

    ua = jnp.dot(h_ref[...], wa_ref[...], preferred_element_type=F32)
    ub = jnp.dot(h_ref[HALO:HALO + bm, :], wb_ref[...], preferred_element_type=F32)
    n_all = bm + 2 * HALO
    u_prev = pltpu.roll(ua, 1, 0)[HALO:HALO + bm]
    u_next = pltpu.roll(ua, n_all - 1, 0)[HALO:HALO + bm]
    u_mid = ua[HALO:HALO + bm]
    pos = jnp.bitwise_and(i * bm + lax.broadcasted_iota(jnp.int32, (bm, 1), 0), seq - 1)
    u_prev = jnp.where(pos == 0, 0.0, u_prev)
    u_next = jnp.where(pos == seq - 1, 0.0, u_next)
    cw = cw_ref[...]
    a = cb_ref[...] + u_prev * cw[0:1] + u_mid * cw[1:2] + u_next * cw[2:3]
    gated = ((a * jax.nn.sigmoid(a)) * ub).astype(BF16)
    o_ref[...] += jnp.dot(gated, wo_ref[...], preferred_element_type=F32)

    @pl.when(c == n_chunks - 1)
    def _():
        o_ref[...] = x_ref[...] + gate_ref[...] * o_ref[...]


def _conv_ffn(x, st, g, shift, scale, gate, w_in, conv_w, conv_b, w_out, bm=1024, ck=512):
    m, d = x.shape
    d_ff = w_out.shape[0]
    bm = min(st.bm, bm)
    ck = _largest_divisor(d_ff, tuple(c for c in (512, 256, 128) if c <= ck))
    n_chunks = d_ff // ck
    n_halo_blocks = m // HALO
    assert st.seq & (st.seq - 1) == 0 and conv_w.shape[0] == 3
    mod_idx = lambda i: st.mod_index(i * bm)
    kern = functools.partial(_ffn_kernel, bm=bm, seq=st.seq, row_chunk=min(bm, 128))
    vmem = (4 * bm * d * 4 + (bm + 2 * HALO) * d * 2 + 6 * d * ck * 2 + 5 * (bm + 2 * HALO) * ck * 4) / MIB + 4
    return pl.pallas_call(
        kern,
        grid=(m // bm, n_chunks),
        in_specs=[
            pl.BlockSpec((HALO, d), lambda i, c: (jnp.maximum(i * (bm // HALO) - 1, 0), 0)),
            pl.BlockSpec((bm, d), lambda i, c: (i, 0)),
            pl.BlockSpec((HALO, d), lambda i, c: (jnp.minimum((i + 1) * (bm // HALO), n_halo_blocks - 1), 0)),
            pl.BlockSpec((1, d), lambda i, c: (0, 0)),
            pl.BlockSpec((None, 1, d), lambda i, c: (mod_idx(i), 0, 0)),
            pl.BlockSpec((None, 1, d), lambda i, c: (mod_idx(i), 0, 0)),
            pl.BlockSpec((None, 1, d), lambda i, c: (mod_idx(i), 0, 0)),
            pl.BlockSpec((d, ck), lambda i, c: (0, c)),
            pl.BlockSpec((d, ck), lambda i, c: (0, n_chunks + c)),
            pl.BlockSpec((conv_w.shape[0], ck), lambda i, c: (0, c)),
            pl.BlockSpec((1, ck), lambda i, c: (0, c)),
            pl.BlockSpec((ck, d), lambda i, c: (c, 0)),
        ],
        out_specs=pl.BlockSpec((bm, d), lambda i, c: (i, 0)),
        out_shape=jax.ShapeDtypeStruct((m, d), F32),
        scratch_shapes=[pltpu.VMEM((bm + 2 * HALO, d), BF16)],
        compiler_params=_cparams(2, vmem),
        name="conv_ffn",
    )(x, x, x, g, shift, scale, gate, w_in, w_in, conv_w, conv_b.reshape(1, d_ff), w_out)


def _qk(q, k):
    return lax.dot_general(q, k, (((1,), (1,)), ((), ())), preferred_element_type=F32)


def _attend(scores, values, sink=None):
    m = None
    for s in scores:
        mi = jnp.max(s, axis=-1, keepdims=True)
        m = mi if m is None else jnp.maximum(m, mi)
    if sink is not None:
        m = jnp.maximum(m, sink)
    es = [jnp.exp2(s - m) for s in scores]
    den = None
    for e in es:
        di = jnp.sum(e, axis=-1, keepdims=True)
        den = di if den is None else den + di
    if sink is not None:
        den = den + jnp.exp2(sink - m)
    out = None
    for e, v in zip(es, values):
        oi = jnp.dot(e.astype(BF16), v, preferred_element_type=F32)
        out = oi if out is None else out + oi
    return out * (1.0 / den)


def _ctx_attn_kernel(q_ref, k_ref, v_ref, *outs, heads, dq, dv, emit_kv):
    o_ref = outs[0]
    for h in range(heads):
        q = q_ref[:, h * dq:(h + 1) * dq].astype(BF16)
        k = k_ref[:, h * dq:(h + 1) * dq]
        v = v_ref[:, h * dv:(h + 1) * dv]
        if emit_kv:
            outs[1][h] = k.astype(F32)
            outs[2][h] = v.astype(F32)
        o = _attend([_qk(q, k.astype(BF16))], [v.astype(BF16)])
        o_ref[:, h * dv:(h + 1) * dv] = o.astype(o_ref.dtype)


def _ctx_attention(qm, km, vm, st, *, n_heads, dq, dv, q_col, k_col, v_col, emit_kv=False):
    hb = _largest_divisor(n_heads, (4, 2, 1))
    s = st.seq
    assert q_col % (hb * dq) == 0 and k_col % (hb * dq) == 0 and v_col % (hb * dv) == 0
    qo, ko, vo = q_col // (hb * dq), k_col // (hb * dq), v_col // (hb * dv)
    out_shape = [jax.ShapeDtypeStruct((st.rows, n_heads * dv), BF16)]
    out_specs = [pl.BlockSpec((s, hb * dv), lambda b, g: (b, g))]
    if emit_kv:
        out_shape += [jax.ShapeDtypeStruct((st.nb, n_heads, s, dq), F32),
                      jax.ShapeDtypeStruct((st.nb, n_heads, s, dv), F32)]
        out_specs += [pl.BlockSpec((None, hb, s, dq), lambda b, g: (b, g, 0, 0)),
                      pl.BlockSpec((None, hb, s, dv), lambda b, g: (b, g, 0, 0))]
    out = pl.pallas_call(
        functools.partial(_ctx_attn_kernel, heads=hb, dq=dq, dv=dv, emit_kv=emit_kv),
        grid=(st.nb, n_heads // hb),
        in_specs=[pl.BlockSpec((s, hb * dq), lambda b, g: (b, qo + g)),
                  pl.BlockSpec((s, hb * dq), lambda b, g: (b, ko + g)),
                  pl.BlockSpec((s, hb * dv), lambda b, g: (b, vo + g))],
        out_specs=out_specs,
        out_shape=out_shape,
        compiler_params=_cparams(2, 32),
        name="ctx_attention",
    )(qm, km, vm)
    return out if emit_kv else out[0]


def _nat_kernel(q_ref, k_ref, v_ref, kc_ref, vc_ref, bias_ref, o_ref, *, key_rows, rows, heads, dh):
    i = pl.program_id(2)
    n_keys = key_rows * GRID_W
    first_row = jnp.clip(i * NA_Q_ROWS - NA_WIN_ROWS // 2, 0, rows - key_rows)
    start = pl.multiple_of(first_row * GRID_W, GRID_W * 4)
    for h in range(heads):
        lanes = slice(h * dh, (h + 1) * dh)
        q = q_ref[:, lanes]
        k = k_ref[pl.ds(start, n_keys), lanes]
        v = v_ref[pl.ds(start, n_keys), lanes]
        s_loc = _qk(q, k) + bias_ref[h]
        s_ctx = _qk(q, kc_ref[h].astype(BF16))
        o_ref[:, lanes] = _attend([s_loc, s_ctx], [v, vc_ref[h].astype(BF16)]).astype(o_ref.dtype)


def _nat_bias(rpb, rows):
    n_blocks = rows // NA_Q_ROWS
    key_rows = min(NA_K_ROWS, rows)
    wr = min(NA_WIN_ROWS, rows)
    reps = [0, min(1, n_blocks - 1), n_blocks - 1]
    heads = rpb.shape[0]
    nq, nk = NA_Q_ROWS * GRID_W, key_rows * GRID_W
    shape = (NA_Q_ROWS, GRID_W, key_rows, GRID_W)
    qc = np.arange(GRID_W)
    cstart = np.clip(qc - NA_WIN_COLS // 2, 0, GRID_W - NA_WIN_COLS)
    col_ok = (qc[None, :] >= cstart[:, None]) & (qc[None, :] < cstart[:, None] + NA_WIN_COLS)
    rp = jnp.pad(rpb.astype(F32), ((0, 0), (key_rows, key_rows), (GRID_W - NA_WIN_COLS, GRID_W - NA_WIN_COLS)))
    row_slabs, mask_l = [], []
    for i in reps:
        ks = int(np.clip(i * NA_Q_ROWS - NA_WIN_ROWS // 2, 0, rows - key_rows))
        r = i * NA_Q_ROWS + np.arange(NA_Q_ROWS)
        rs = np.clip(r - wr // 2, 0, rows - wr)
        kr = ks + np.arange(key_rows)
        row_ok = (kr[None, :] >= rs[:, None]) & (kr[None, :] < rs[:, None] + wr)
        for rq in range(NA_Q_ROWS):
            first = ks - int(r[rq]) + NA_WIN_ROWS - 1 + key_rows
            assert 0 <= first and first + key_rows <= rp.shape[1]
            row_slabs.append(rp[:, first:first + key_rows, :])
        mask_l.append(np.broadcast_to(row_ok[:, None, :, None] & col_ok[None, :, None, :], shape).reshape(nq, nk))
    slab = jnp.stack(row_slabs, axis=1).reshape(heads, len(reps), NA_Q_ROWS, key_rows, 2 * GRID_W - 1)
    toep = jnp.stack([slab[..., GRID_W - 1 - c:2 * GRID_W - 1 - c] for c in range(GRID_W)], axis=3)
    bias = toep.reshape(heads, len(reps), nq, nk)
    return jnp.where(jnp.asarray(np.stack(mask_l))[None], bias * LOG2E, NEG)


def _nat_attention(qkv, st, cache_k, cache_v, j, rpb, dh):
    heads = rpb.shape[0]
    n = st.seq
    p = cache_k.shape[3]
    rows = n // GRID_W
    assert rows % NA_Q_ROWS == 0 and rows >= NA_K_ROWS and dh % LANE == 0
    n_blocks = rows // NA_Q_ROWS
    key_rows = min(NA_K_ROWS, rows)
    nq, nk = NA_Q_ROWS * GRID_W, key_rows * GRID_W
    bias = _nat_bias(rpb, rows)
    btype = lambda i: jnp.where(i == 0, 0, jnp.where(i == n_blocks - 1, 2, 1))
    hb = _largest_divisor(heads, (4, 2, 1))
    hg = heads // hb
    kern = functools.partial(_nat_kernel, key_rows=key_rows, rows=rows, heads=hb, dh=dh)
    return pl.pallas_call(
        kern,
        grid=(st.nb, hg, n_blocks),
        in_specs=[pl.BlockSpec((nq, hb * dh), lambda b, h, i: (b * n_blocks + i, h)),
                  pl.BlockSpec((n, hb * dh), lambda b, h, i: (b, hg + h)),
                  pl.BlockSpec((n, hb * dh), lambda b, h, i: (b, 2 * hg + h)),
                  pl.BlockSpec((None, None, hb, p, dh), lambda b, h, i: (b, j, h, 0, 0)),
                  pl.BlockSpec((None, None, hb, p, dh), lambda b, h, i: (b, j, h, 0, 0)),
                  pl.BlockSpec((hb, None, nq, nk), lambda b, h, i: (h, btype(i), 0, 0))],
        out_specs=pl.BlockSpec((nq, hb * dh), lambda b, h, i: (b * n_blocks + i, h)),
        out_shape=jax.ShapeDtypeStruct((st.rows, heads * dh), BF16),
        compiler_params=_cparams(3, 56),
        name="nat_attention",
    )(qkv, qkv, qkv, cache_k, cache_v, bias)


def _joint_dense_kernel(q_ref, k_ref, v_ref, kc_ref, vc_ref, o_ref, *, chunk):
    q = q_ref[...]
    n = k_ref.shape[0]
    pieces = [(k_ref, v_ref, c0, min(chunk, n - c0)) for c0 in range(0, n, chunk)]
    pieces.append((kc_ref, vc_ref, 0, kc_ref.shape[0]))
    m = den = acc = None
    for kr, vr, c0, size in pieces:
        s = _qk(q, kr[c0:c0 + size, :])
        mc = jnp.max(s, axis=-1, keepdims=True)
        m_new = mc if m is None else jnp.maximum(m, mc)
        e = jnp.exp2(s - m_new)
        dc = jnp.sum(e, axis=-1, keepdims=True)
        pv = jnp.dot(e.astype(BF16), vr[c0:c0 + size, :], preferred_element_type=F32)
        if m is None:
            den, acc = dc, pv
        else:
            alpha = jnp.exp2(m - m_new)
            den, acc = alpha * den + dc, alpha * acc + pv
        m = m_new
    o_ref[...] = (acc * (1.0 / den)).astype(o_ref.dtype)


def _joint_dense_attention(qm, km, vm, kcm, vcm, st, p, *, n_heads, dq, dv):
    n = st.seq
    bq = _largest_divisor(n, (512, 256, 128, 64, 32, 16))
    nqb = n // bq
    return pl.pallas_call(
        functools.partial(_joint_dense_kernel, chunk=1024),
        grid=(st.nb, n_heads, nqb),
        in_specs=[pl.BlockSpec((bq, dq), lambda b, h, i: (b * nqb + i, h)),
                  pl.BlockSpec((n, dq), lambda b, h, i: (b, h)),
                  pl.BlockSpec((n, dv), lambda b, h, i: (b, h)),
                  pl.BlockSpec((p, dq), lambda b, h, i: (b, h)),
                  pl.BlockSpec((p, dv), lambda b, h, i: (b, h))],
        out_specs=pl.BlockSpec((bq, dv), lambda b, h, i: (b * nqb + i, h)),
        out_shape=jax.ShapeDtypeStruct((st.rows, n_heads * dv), BF16),
        compiler_params=_cparams(3, 48),
        name="mla_attention",
    )(qm, km, vm, kcm, vcm)


def _both_halves(x, s):
    x = x.astype(F32)
    low = lax.broadcasted_iota(jnp.int32, (1, LANE), 1) < LANE // 2
    keep = low if s == 0 else jnp.logical_not(low)
    return jnp.where(keep, x, pltpu.roll(x, LANE // 2, 1)).astype(BF16)


def _swa_step(sinks_ref, pair, q_ref, k, v, kc, vc, o_ref, *, dh, groups, local_mask):
    kv_per_step = LANE // dh
    assert kv_per_step == 2 and groups % 2 == 0
    rows = q_ref.shape[0]
    low = lax.broadcasted_iota(jnp.int32, (1, LANE), 1) < dh
    row_group = lax.broadcasted_iota(jnp.int32, (groups * rows, 1), 0) // rows
    for s in range(kv_per_step):
        kd, vd = _both_halves(k, s), _both_halves(v, s)
        q_parts = []
        for g in range(groups):
            c0 = ((s * groups + g) * dh // LANE) * LANE
            tile = q_ref[:, c0:c0 + LANE].astype(BF16)
            q_parts.append(jnp.where(low if g % 2 == 0 else jnp.logical_not(low), tile, jnp.zeros_like(tile)))
        q = jnp.concatenate(q_parts, axis=0)
        sink = jnp.zeros((groups * rows, 1), F32)
        for g in range(groups):
            sink = jnp.where(row_group == g, sinks_ref[(pair * kv_per_step + s) * groups + g], sink)
        s_loc = _qk(q, kd)
        if local_mask is not None:
            s_loc = jnp.where(local_mask, s_loc, NEG)
        if kc is not None:
            kcd = jnp.concatenate([kc[s], kc[s]], axis=-1).astype(BF16)
            vcd = jnp.concatenate([vc[s], vc[s]], axis=-1).astype(BF16)
            out = _attend([s_loc, _qk(q, kcd)], [vd, vcd], sink)
        else:
            out = _attend([s_loc], [vd], sink)
        for g in range(0, groups, 2):
            c0 = (s * groups + g) * dh
            o_ref[:, c0:c0 + LANE] = jnp.where(low, out[g * rows:(g + 1) * rows],
                                               out[(g + 1) * rows:(g + 2) * rows]).astype(o_ref.dtype)


def _swa_ctx_kernel(sinks_ref, q_ref, k_ref, v_ref, o_ref, ko_ref, vo_ref, *, dh, groups):
    pair = pl.program_id(1)
    k, v = k_ref[...], v_ref[...]
    for s in range(LANE // dh):
        ko_ref[s] = k[:, s * dh:(s + 1) * dh].astype(F32)
        vo_ref[s] = v[:, s * dh:(s + 1) * dh].astype(F32)
    _swa_step(sinks_ref, pair, q_ref, k, v, None, None, o_ref, dh=dh, groups=groups, local_mask=None)


def _swa_lat_kernel(sinks_ref, q_ref, k_ref, v_ref, kc_ref, vc_ref, o_ref, *, dh, groups, n):
    pair = pl.program_id(1)
    blk = pl.program_id(2)
    n_keys = min(3 * SWA_BLOCK, n)
    start = pl.multiple_of(jnp.clip((blk - 1) * SWA_BLOCK, 0, n - n_keys), SWA_BLOCK)
    k = k_ref[pl.ds(start, n_keys), :]
    v = v_ref[pl.ds(start, n_keys), :]
    qpos = blk * SWA_BLOCK + lax.broadcasted_iota(jnp.int32, (groups * SWA_BLOCK, 1), 0) % SWA_BLOCK
    kpos = start + lax.broadcasted_iota(jnp.int32, (1, n_keys), 1)
    mask = jnp.abs(qpos - kpos) <= SWA_WINDOW
    _swa_step(sinks_ref, pair, q_ref, k, v, kc_ref, vc_ref, o_ref, dh=dh, groups=groups, local_mask=mask)


def _swa_attention(qkv, st, sinks, *, heads, kvh, dh, cache=None):
    groups = heads // kvh
    kv_per_step = LANE // dh
    assert LANE % dh == 0 and kvh % kv_per_step == 0 and groups % kv_per_step == 0
    pairs = kvh // kv_per_step
    qw = kv_per_step * groups * dh
    k_blk = heads * dh // LANE
    v_blk = (heads + kvh) * dh // LANE
    n = st.seq
    common = dict(dh=dh, groups=groups)
    smem = pl.BlockSpec(memory_space=pltpu.SMEM)
    if cache is None:
        out = pl.pallas_call(
            functools.partial(_swa_ctx_kernel, **common),
            grid=(st.nb, pairs),
            in_specs=[smem,
                      pl.BlockSpec((n, qw), lambda b, c: (b, c)),
                      pl.BlockSpec((n, LANE), lambda b, c: (b, k_blk + c)),
                      pl.BlockSpec((n, LANE), lambda b, c: (b, v_blk + c))],
            out_specs=[pl.BlockSpec((n, qw), lambda b, c: (b, c)),
                       pl.BlockSpec((None, kv_per_step, n, dh), lambda b, c: (b, c, 0, 0)),
                       pl.BlockSpec((None, kv_per_step, n, dh), lambda b, c: (b, c, 0, 0))],
            out_shape=[jax.ShapeDtypeStruct((st.rows, heads * dh), BF16),
                       jax.ShapeDtypeStruct((st.nb, kvh, n, dh), F32),
                       jax.ShapeDtypeStruct((st.nb, kvh, n, dh), F32)],
            compiler_params=_cparams(2, 32),
            name="swa_ctx_attention",
        )(sinks, qkv, qkv, qkv)
        return out
    cache_k, cache_v, j = cache
    p = cache_k.shape[3]
    nblk = n // SWA_BLOCK
    assert n % SWA_BLOCK == 0
    return pl.pallas_call(
        functools.partial(_swa_lat_kernel, n=n, **common),
        grid=(st.nb, pairs, nblk),
        in_specs=[smem,
                  pl.BlockSpec((SWA_BLOCK, qw), lambda b, c, i: (b * nblk + i, c)),
                  pl.BlockSpec((n, LANE), lambda b, c, i: (b, k_blk + c)),
                  pl.BlockSpec((n, LANE), lambda b, c, i: (b, v_blk + c)),
                  pl.BlockSpec((None, None, kv_per_step, p, dh), lambda b, c, i: (b, j, c, 0, 0)),
                  pl.BlockSpec((None, None, kv_per_step, p, dh), lambda b, c, i: (b, j, c, 0, 0))],
        out_specs=pl.BlockSpec((SWA_BLOCK, qw), lambda b, c, i: (b * nblk + i, c)),
        out_shape=jax.ShapeDtypeStruct((st.rows, heads * dh), BF16),
        compiler_params=_cparams(3, 32),
        name="swa_attention",
    )(sinks, qkv, qkv, qkv, cache_k, cache_v)


def _mla_kv_kernel(*refs, norm, emit_xn, rope, row_chunk, norm_div):
    it = iter(refs)
    x_ref = next(it)
    g_ref = next(it) if norm else None
    w_ref, kr_ref, g1_ref, g2_ref = next(it), next(it), next(it), next(it)
    tabs = (next(it), next(it), next(it)) if rope else None
    k_ref, v_ref = next(it), next(it)
    xn_ref = next(it) if emit_xn else None
    xs_ref = next(it)

    @pl.when(pl.program_id(1) == 0)
    def _():
        _fill_lhs(x_ref, xs_ref, xn_ref, "norm" if norm else None, g_ref, None, None, row_chunk)

    acc = jnp.dot(xs_ref[...], w_ref[...], preferred_element_type=F32)
    kr = kr_ref[...]
    kr_ssq = jnp.sum(kr * kr, axis=-1, keepdims=True)
    for h in range(acc.shape[1] // (2 * LANE)):
        nope = acc[:, 2 * h * LANE:(2 * h + 1) * LANE]
        inv = lax.rsqrt((jnp.sum(nope * nope, axis=-1, keepdims=True) + kr_ssq) / norm_div + EPS)
        k_rot = (kr * inv) * g2_ref[...]
        if rope:
            k_rot = _rope_apply(k_rot, *(t[...] for t in tabs))
        k_ref[:, 2 * h * LANE:(2 * h + 1) * LANE] = ((nope * inv) * g1_ref[...]).astype(k_ref.dtype)
        k_ref[:, (2 * h + 1) * LANE:(2 * h + 2) * LANE] = k_rot.astype(k_ref.dtype)
        v_ref[:, h * LANE:(h + 1) * LANE] = acc[:, (2 * h + 1) * LANE:(2 * h + 2) * LANE].astype(v_ref.dtype)


def _mla_kv(x, x_block, w_ukv, kr, kr_block, g_kva, g1, g2, tabs, st, *, n_heads, norm_div, emit_xn, name):
    rows = x.shape[0]
    k, n = w_ukv.shape
    head_n = n // n_heads
    assert head_n == 2 * LANE, "nope and value widths must both be one lane tile"
    kx, kidx = x_block
    krw, kridx = kr_block
    assert kx == k and krw == LANE
    bm = st.bm
    norm = g_kva is not None
    rope = tabs is not None
    in_specs = [pl.BlockSpec((bm, k), lambda i, h: (i, kidx))]
    args = [x]
    if norm:
        in_specs.append(pl.BlockSpec((1, k), lambda i, h: (0, 0)))
        args.append(g_kva)
    hb = _largest_divisor(n_heads, (4, 2, 1))
    in_specs += [pl.BlockSpec((k, hb * head_n), lambda i, h: (0, h)),
                 pl.BlockSpec((bm, LANE), lambda i, h: (i, kridx)),
                 pl.BlockSpec((1, LANE), lambda i, h: (0, 0)),
                 pl.BlockSpec((1, LANE), lambda i, h: (0, 0))]
    args += [w_ukv, kr, g1, g2]
    if rope:
        tiles_per_seq = st.seq // bm
        in_specs += [pl.BlockSpec((bm, LANE), lambda i, h: (i % tiles_per_seq, 0))] * 3
        args += list(tabs)
    out_shape = [jax.ShapeDtypeStruct((rows, n_heads * 2 * LANE), BF16),
                 jax.ShapeDtypeStruct((rows, n_heads * LANE), BF16)]
    out_specs = [pl.BlockSpec((bm, hb * 2 * LANE), lambda i, h: (i, h)),
                 pl.BlockSpec((bm, hb * LANE), lambda i, h: (i, h))]
    if emit_xn:
        out_shape.append(jax.ShapeDtypeStruct((rows, k), F32))
        out_specs.append(pl.BlockSpec((bm, k), lambda i, h: (i, 0)))
    kern = functools.partial(_mla_kv_kernel, norm=norm, emit_xn=emit_xn, rope=rope, row_chunk=min(bm, 128),
                             norm_div=norm_div)
    return pl.pallas_call(
        kern,
        grid=(rows // bm, n_heads // hb),
        in_specs=in_specs,
        out_specs=out_specs,
        out_shape=out_shape,
        scratch_shapes=[pltpu.VMEM((bm, k), BF16)],
        compiler_params=_cparams(2, 40),
        name=name,
    )(*args)


def _band_plan(width, block):
    n_tiles = width // LANE
    lo = [((t * LANE) // block) * block for t in range(n_tiles)]
    hi = [(((t + 1) * LANE - 1) // block + 1) * block for t in range(n_tiles)]
    start = [(l // LANE) * LANE for l in lo]
    kb = max(-(-(h - s) // LANE) * LANE for h, s in zip(hi, start))
    kb = min(kb, width)
    start = [min(s, width - kb) for s in start]
    return start, kb


def _band_weights(w, width, block, start, kb):
    n_tiles = width // LANE
    wb = w.astype(BF16)
    tiles = []
    for t in range(n_tiles):
        pieces = []
        col = t * LANE
        while col < (t + 1) * LANE:
            blk = col // block
            col_end = min((blk + 1) * block, (t + 1) * LANE)
            sub = wb[blk, :, col - blk * block:col_end - blk * block]
            top = blk * block - start[t]
            pieces.append(jnp.pad(sub, ((top, kb - top - block), (0, 0))))
            col = col_end
        tiles.append(jnp.concatenate(pieces, axis=1))
    return jnp.stack(tiles)


def _gelu_tanh(x):
    cdf = 0.5 * (1.0 + jnp.tanh(np.float32(np.sqrt(2.0 / np.pi)) * (x + 0.044715 * (x * x * x))))
    return x * cdf


def _sigmoid(x):
    return 0.5 * (1.0 + jnp.tanh(0.5 * x))


def _lru_pass_kernel(*refs, reverse, starts, kb, bt, seq, nb, taps):
    left = taps // 2
    right = taps - 1 - left
    it = iter(refs)
    xp_ref, x_ref = next(it), next(it)
    xn_ref = next(it) if right > 0 else None
    cw_ref, cb_ref = next(it), next(it)
    wa_ref, wi_ref, ba_ref, bi_ref, lam_ref, h0_ref = (next(it) for _ in range(6))
    hsf_ref, gate_ref = (next(it), next(it)) if reverse else (None, None)
    out_ref, ht_ref = next(it), next(it)
    xc_s, a_s, bx_s, carry = next(it), next(it), next(it), next(it)

    step = pl.program_id(0)
    n_steps = pl.num_programs(0)
    tile = (n_steps - 1 - step) if reverse else step
    n_tiles = len(starts)
    rows = nb * bt

    @pl.when(step == 0)
    def _():
        carry[...] = h0_ref[...]

    pos = tile * bt + lax.broadcasted_iota(jnp.int32, (bt, 1, 1), 0)
    for t in range(n_tiles):
        lanes = slice(t * LANE, (t + 1) * LANE)
        parts = [xp_ref[:, :, lanes], x_ref[:, :, lanes]] + ([xn_ref[:, :, lanes]] if right > 0 else [])
        full = jnp.concatenate(parts, axis=0)
        acc = jnp.broadcast_to(cb_ref[:, lanes], (bt, nb, LANE))
        for k in range(taps):
            off = k - left
            shifted = full[k:k + bt]
            if off != 0:
                shifted = jnp.where((pos + off >= 0) & (pos + off < seq), shifted, 0.0)
            acc = acc + shifted * cw_ref[k:k + 1, lanes]
        xc_s[:, lanes] = acc.reshape(rows, LANE)

    neg_lam = -lam_ref[...]
    softplus = jnp.maximum(neg_lam, 0.0) + jnp.log1p(jnp.exp(-jnp.abs(neg_lam)))
    for t in range(n_tiles):
        lanes = slice(t * LANE, (t + 1) * LANE)
        xw = xc_s[:, starts[t]:starts[t] + kb].astype(BF16)
        r = _sigmoid(jnp.dot(xw, wa_ref[t], preferred_element_type=F32) + ba_ref[:, lanes])
        ig = _sigmoid(jnp.dot(xw, wi_ref[t], preferred_element_type=F32) + bi_ref[:, lanes])
        log_a = -LRU_C * r * softplus[:, lanes]
        a = jnp.exp(log_a)
        a_s[:, lanes] = a
        bx_s[:, lanes] = jnp.sqrt(-jnp.tanh(log_a) * (a * a + 1.0)) * (ig * xc_s[:, lanes])

    h = carry[...]
    for s in range(bt):
        ts = (bt - 1 - s) if reverse else s
        slab = slice(ts * nb, (ts + 1) * nb)
        h = a_s[slab, :] * h + bx_s[slab, :]
        a_s[slab, :] = h
    carry[...] = h
    ht_ref[...] = h
    hs = a_s[...].reshape(bt, nb, a_s.shape[1])
    if reverse:
        out_ref[...] = (_gelu_tanh(gate_ref[...]) * (hsf_ref[...] + hs)).astype(out_ref.dtype)
    else:
        out_ref[...] = hs


def _lru_pass(u, st, conv_w, conv_b, wa, wi, b_a, b_i, lam, h0, starts, kb, *, reverse, hs_fwd=None):
    c = conv_w.shape[1]
    taps = conv_w.shape[0]
    left, right = taps // 2, taps - 1 - taps // 2
    nb, seq = st.nb, st.seq
    bt = min(max(256 // nb, SUBLANE), seq)
    assert seq % bt == 0 and c % LANE == 0 and left > 0 and bt % left == 0 and (right == 0 or bt % right == 0)
    nt = seq // bt
    n_tiles = c // LANE
    u3 = u.reshape(seq, nb, 2 * c)
    tmap = (lambda s: nt - 1 - s) if reverse else (lambda s: s)
    full = lambda *shape: pl.BlockSpec(shape, lambda s: (0,) * len(shape))
    in_specs = [pl.BlockSpec((left, nb, c), lambda s: (jnp.maximum(tmap(s) * (bt // left) - 1, 0), 0, 0)),
                pl.BlockSpec((bt, nb, c), lambda s: (tmap(s), 0, 0))]
    args = [u3, u3]
    if right > 0:
        in_specs.append(pl.BlockSpec((right, nb, c),
                                     lambda s: (jnp.minimum((tmap(s) + 1) * (bt // right), seq // right - 1), 0, 0)))
        args.append(u3)
    in_specs += [full(taps, c), full(1, c), full(n_tiles, kb, LANE), full(n_tiles, kb, LANE), full(1, c), full(1, c),
                 full(1, c), full(nb, c)]
    args += [conv_w, conv_b.reshape(1, c), wa, wi, b_a.reshape(1, c), b_i.reshape(1, c), lam.reshape(1, c), h0]
    if reverse:
        in_specs += [pl.BlockSpec((bt, nb, c), lambda s: (tmap(s), 0, 0)),
                     pl.BlockSpec((bt, nb, c), lambda s: (tmap(s), 0, 1))]
        args += [hs_fwd, u3]
    kern = functools.partial(_lru_pass_kernel, reverse=reverse, starts=tuple(starts), kb=kb, bt=bt, seq=seq, nb=nb,
                             taps=taps)
    blk = nb * bt * c * 4 / MIB
    vmem = (2 + 2 + 3 + (4 if reverse else 0) + 4) * blk + 4 * n_tiles * kb * LANE * 2 / MIB + 8
    return pl.pallas_call(
        kern,
        grid=(nt,),
        in_specs=in_specs,
        out_specs=[pl.BlockSpec((bt, nb, c), lambda s: (tmap(s), 0, 0)),
                   pl.BlockSpec((nb, c), lambda s: (0, 0))],
        out_shape=[jax.ShapeDtypeStruct((seq, nb, c), F32),
                   jax.ShapeDtypeStruct((nb, c), F32)],
        scratch_shapes=[pltpu.VMEM((nb * bt, c), F32), pltpu.VMEM((nb * bt, c), F32),
                        pltpu.VMEM((nb * bt, c), F32), pltpu.VMEM((nb, c), F32)],
        compiler_params=_cparams(1, vmem),
        name="lru_bwd" if reverse else "lru_fwd",
    )(*args)


def _mixer_nat(xs, streams, mods, cache_k, cache_v, j, w_qkv, g_mix, g_q, g_k, rpb, w_o):
    heads, dh = rpb.shape[0], g_q.shape[0]
    w_qkv, w_o = w_qkv.astype(BF16), w_o.astype(BF16)
    gains = jnp.concatenate([jnp.tile(g_q * (dh ** -0.5 * LOG2E), heads), jnp.tile(g_k, heads),
                             jnp.ones((heads * dh,), F32)])[None]
    spec = dict(head_w=dh, norm_div=dh, norm_cols=2 * heads * dh, gains=gains)
    new_x, extra = [], None
    for x, st in zip(xs, streams):
        latent = not st.shared
        qkv = _proj(x, w_qkv, st, norm_g=g_mix, mod=(mods[0], mods[1]), heads=spec,
                    out_dtype=BF16 if latent else F32, name="nat_qkv")
        if latent:
            o = _nat_attention(qkv, st, cache_k, cache_v, j, rpb, dh)
        else:
            o, kc, vc = _ctx_attention(qkv, qkv, qkv, st, n_heads=heads, dq=dh, dv=dh, q_col=0, k_col=heads * dh,
                                       v_col=2 * heads * dh, emit_kv=True)
            extra = (kc, vc)
        new_x.append(_proj(o, w_o, st, res=x, gate=mods[2], name="nat_out"))
    return new_x, extra


def _mixer_lru(xs, streams, mods, state, w_in, g_mix, conv_w, conv_b, w_a, b_a, w_i, b_i, lam, w_out):
    c = conv_w.shape[1]
    block = w_a.shape[-1]
    w_in, w_out = w_in.astype(BF16), w_out.astype(BF16)
    starts, kb = _band_plan(c, block)
    wa = [_band_weights(w_a[d], c, block, starts, kb) for d in range(2)]
    wi = [_band_weights(w_i[d], c, block, starts, kb) for d in range(2)]
    new_x, st_out = [], None
    for x, st in zip(xs, streams):
        latent = not st.shared
        h0 = state.astype(F32) if latent else jnp.zeros((st.nb, 2, c), F32)
        u = _proj(x, w_in, st, norm_g=g_mix, mod=(mods[0], mods[1]), out_time_major=True, name="lru_in")
        hs_f, t_f = _lru_pass(u, st, conv_w, conv_b, wa[0], wi[0], b_a[0], b_i[0], lam[0], h0[:, 0], starts, kb,
                              reverse=False)
        y, t_b = _lru_pass(u, st, conv_w, conv_b, wa[1], wi[1], b_a[1], b_i[1], lam[1], h0[:, 1], starts, kb,
                           reverse=True, hs_fwd=hs_f)
        if not latent:
            st_out = jnp.stack([t_f, t_b], axis=1)
        new_x.append(_proj(y.reshape(st.seq, st.nb * c), w_out, st, res=x, gate=mods[2], x_time_major=True,
                           name="lru_out"))
    return new_x, st_out


def _mixer_mla(xs, streams, mods, cache_ckv, cache_kr, w_down, g_mix, g_qa, g_kva, w_uq, w_ukv, g_q, g_k, w_o):
    d_model = w_down.shape[0]
    q_rank, kv_rank = g_qa.shape[0], g_kva.shape[0]
    qk_dim = g_q.shape[0]
    heads = w_uq.shape[1] // qk_dim
    rope = w_down.shape[1] - q_rank - kv_rank
    nope = qk_dim - rope
    assert nope == LANE and rope <= LANE and kv_rank % LANE == 0 and q_rank % LANE == 0
    head_w = 2 * LANE
    q_pad = -q_rank % kv_rank
    kv_col = q_rank + q_pad
    tail_pad = -(kv_col + kv_rank + rope) % 512
    w_dn = jnp.concatenate([w_down[:, :q_rank], jnp.zeros((d_model, q_pad), F32),
                            w_down[:, q_rank:q_rank + kv_rank], w_down[:, q_rank + kv_rank:],
                            jnp.zeros((d_model, tail_pad), F32)], axis=1).astype(BF16)
    kr_blk = (kv_col + kv_rank) // LANE
    w_q = jnp.pad(w_uq.reshape(q_rank, heads, qk_dim), ((0, 0), (0, 0), (0, head_w - qk_dim)))
    w_q = w_q.reshape(q_rank, heads * head_w).astype(BF16)
    w_ukv, w_o = w_ukv.astype(BF16), w_o.astype(BF16)
    gq = jnp.tile(jnp.pad(g_q * (qk_dim ** -0.5 * LOG2E), (0, head_w - qk_dim)), heads)[None]
    g1, g2 = g_k[None, :nope], jnp.pad(g_k[nope:], (0, LANE - rope))[None]
    p = cache_ckv.shape[1]
    new_x, extra = [], None
    for x, st in zip(xs, streams):
        latent = not st.shared
        d = _proj(x, w_dn, st, norm_g=g_mix, mod=(mods[0], mods[1]), name="mla_down")
        q_tabs = _rope_tables(st.seq, rope, nope, head_w) if latent else None
        k_tabs = _rope_tables(st.seq, rope, 0, LANE) if latent else None
        q = _proj(d, w_q, st, x_block=(q_rank, 0), norm_g=g_qa[None],
                  heads=dict(head_w=head_w, norm_div=qk_dim, norm_cols=heads * head_w, gains=gq, tabs=q_tabs,
                             rope_tiles=(False, True)),
                  out_dtype=BF16, name="mla_uq")
        kv = _mla_kv(d, (kv_rank, kv_col // kv_rank), w_ukv, d, (LANE, kr_blk), g_kva[None], g1, g2, k_tabs, st,
                     n_heads=heads, norm_div=qk_dim, emit_xn=not latent, name="mla_ukv")
        if latent:
            k, v = kv
            cst = _Stream(st.nb, p, 0, True)
            krc = jnp.pad(cache_kr.reshape(st.nb * p, rope), ((0, 0), (0, LANE - rope)))
            kc, vc = _mla_kv(cache_ckv.reshape(st.nb * p, kv_rank), (kv_rank, 0), w_ukv, krc, (LANE, 0), None,
                             g1, g2, None, cst, n_heads=heads, norm_div=qk_dim, emit_xn=False,
                             name="mla_ukv_cache")
            o = _joint_dense_attention(q, k, v, kc, vc, st, p, n_heads=heads, dq=head_w, dv=LANE)
        else:
            k, v, ckv = kv
            o = _ctx_attention(q, k, v, st, n_heads=heads, dq=head_w, dv=LANE, q_col=0, k_col=0, v_col=0)
            kr_out = d[:, kv_col + kv_rank:kv_col + kv_rank + rope]
            extra = (ckv.reshape(st.nb, st.seq, kv_rank), kr_out.reshape(st.nb, st.seq, rope))
        new_x.append(_proj(o, w_o, st, res=x, gate=mods[2], name="mla_out"))
    return new_x, extra


def _mixer_swa(xs, streams, mods, cache_k, cache_v, j, w_qkv, g_mix, g_q, g_k, sinks, w_o):
    dh = g_q.shape[0]
    heads = sinks.shape[0]
    kvh = (w_qkv.shape[1] // dh - heads) // 2
    w_qkv, w_o = w_qkv.astype(BF16), w_o.astype(BF16)
    gains = jnp.concatenate([jnp.tile(g_q * (dh ** -0.5 * LOG2E), heads), jnp.tile(g_k, kvh),
                             jnp.ones((kvh * dh,), F32)])[None]
    sinks = sinks.astype(F32) * LOG2E
    new_x, extra = [], None
    for x, st in zip(xs, streams):
        latent = not st.shared
        tabs = _rope_tables(st.seq, dh, 0, dh) if latent else None
        if tabs is not None:
            tabs = tuple(jnp.tile(t, (1, LANE // dh)) for t in tabs)
        spec = dict(head_w=dh, norm_div=dh, norm_cols=(heads + kvh) * dh, gains=gains, tabs=tabs,
                    rope_tiles=(True,))
        qkv = _proj(x, w_qkv, st, norm_g=g_mix, mod=(mods[0], mods[1]), heads=spec,
                    out_dtype=BF16 if latent else F32, name="swa_qkv")
        if latent:
            o = _swa_attention(qkv, st, sinks, heads=heads, kvh=kvh, dh=dh, cache=(cache_k, cache_v, j))
        else:
            o, kc, vc = _swa_attention(qkv, st, sinks, heads=heads, kvh=kvh, dh=dh)
            extra = (kc, vc)
        new_x.append(_proj(o, w_o, st, res=x, gate=mods[2], name="swa_out"))
    return new_x, extra


def kernel(x_prompt, x_sample, cache_nat_k, cache_nat_v, state_lru, cache_mla_ckv, cache_mla_krope, cache_swa_k, cache_swa_v, c, c_ctx, norm_mix, norm_ffn, w_mod, b_mod, ffn_w_in, ffn_conv_w, ffn_conv_b, ffn_w_out, nat_w_qkv, nat_q_norm, nat_k_norm, nat_rpb, nat_w_o, lru_w_in, lru_conv_w, lru_conv_b, lru_w_a, lru_b_a, lru_w_i, lru_b_i, lru_lambda, lru_w_out, mla_w_down, mla_q_a_norm, mla_kv_a_norm, mla_w_uq, mla_w_ukv, mla_q_norm, mla_k_norm, mla_w_o, swa_w_qkv, swa_q_norm, swa_k_norm, swa_sinks, swa_w_o):
    bc, sc, d = x_prompt.shape
    bl, n, _ = x_sample.shape
    depth = w_mod.shape[0]
    streams = (_Stream(bc, sc, 0, True), _Stream(bl, n, 1, False))
    xs = [x_prompt.reshape(bc * sc, d), x_sample.reshape(bl * n, d)]

    n_cond = 1 + bl
    cond_rows = -(-n_cond // SUBLANE) * SUBLANE
    cond = jnp.zeros((cond_rows, d), F32).at[0].set(c_ctx).at[1:n_cond].set(c)
    mods = _modulation(cond, w_mod, b_mod)[:, :n_cond]

    nat_k_l, nat_v_l, lru_l, ckv_l, krope_l, swa_k_l, swa_v_l = [], [], [], [], [], [], []
    for l in range(depth):
        kind, j = l % 4, l // 4
        m6 = [mods[l, :, None, t * d:(t + 1) * d] for t in range(6)]
        g_mix = norm_mix[l].reshape(1, d)
        if kind == 0:
            xs, (kc, vc) = _mixer_nat(xs, streams, m6, cache_nat_k, cache_nat_v, j, nat_w_qkv[j], g_mix,
                                      nat_q_norm[j], nat_k_norm[j], nat_rpb[j], nat_w_o[j])
            nat_k_l.append(kc)
            nat_v_l.append(vc)
        elif kind == 1:
            xs, st = _mixer_lru(xs, streams, m6, state_lru[:, j], lru_w_in[j], g_mix, lru_conv_w[j], lru_conv_b[j],
                                lru_w_a[j], lru_b_a[j], lru_w_i[j], lru_b_i[j], lru_lambda[j], lru_w_out[j])
            lru_l.append(st)
        elif kind == 2:
            xs, (ckv, kr) = _mixer_mla(xs, streams, m6, cache_mla_ckv[:, j], cache_mla_krope[:, j], mla_w_down[j],
                                       g_mix, mla_q_a_norm[j], mla_kv_a_norm[j], mla_w_uq[j], mla_w_ukv[j],
                                       mla_q_norm[j], mla_k_norm[j], mla_w_o[j])
            ckv_l.append(ckv)
            krope_l.append(kr)
        else:
            xs, (kc, vc) = _mixer_swa(xs, streams, m6, cache_swa_k, cache_swa_v, j, swa_w_qkv[j], g_mix,
                                      swa_q_norm[j], swa_k_norm[j], swa_sinks[j], swa_w_o[j])
            swa_k_l.append(kc)
            swa_v_l.append(vc)
        w_in, w_out = ffn_w_in[l].astype(BF16), ffn_w_out[l].astype(BF16)
        xs = [_conv_ffn(x, st, norm_ffn[l].reshape(1, d), m6[3], m6[4], m6[5], w_in, ffn_conv_w[l],
                        ffn_conv_b[l], w_out) for x, st in zip(xs, streams)]

    return (xs[0].reshape(bc, sc, d), xs[1].reshape(bl, n, d), jnp.stack(nat_k_l, axis=1),
            jnp.stack(nat_v_l, axis=1), jnp.stack(lru_l, axis=1), jnp.stack(ckv_l, axis=1),
            jnp.stack(krope_l, axis=1), jnp.stack(swa_k_l, axis=1), jnp.stack(swa_v_l, axis=1))
```

```python
import functools

import numpy as np
import jax
import jax.numpy as jnp
from jax import lax
from jax.experimental import pallas as pl
from jax.experimental.pallas import tpu as pltpu

F32 = jnp.float32
BF16 = jnp.bfloat16

GRID_W = 64
NA_WIN_ROWS = 8
NA_WIN_COLS = 16
NA_Q_ROWS = 8
NA_K_ROWS = 16
LRU_C = 8.0
SWA_WINDOW = 128
SWA_BLOCK = 128
ROPE_BASE = 10000.0
ROPE_GROUP = 32
EPS = 1e-6
NEG = -1e30
LOG2E = float(np.log2(np.e))
LANE = 128
SUBLANE = 8
HALO = 16
MIB = 1024 * 1024
ROW_TILES = (1024, 512, 256, 128, 64, 32, 16)


def _cparams(n_axes, vmem_mib):
    return pltpu.CompilerParams(dimension_semantics=("arbitrary",) * n_axes,
                                vmem_limit_bytes=int(min(vmem_mib, 60) * MIB))


def _largest_divisor(n, candidates):
    for c in candidates:
        if n % c == 0:
            return c
    return n


class _Stream:
    def __init__(self, nb, seq, mod0, shared_mod):
        self.nb, self.seq, self.rows, self.mod0, self.shared = nb, seq, nb * seq, mod0, shared_mod
        self.bm = _largest_divisor(self.rows if shared_mod else seq, ROW_TILES)

    def mod_index(self, row0):
        return self.mod0 if self.shared else self.mod0 + row0 // self.seq


def _norm_mod(x, g, shift, scale):
    ms = jnp.mean(x * x, axis=-1, keepdims=True)
    return (x * lax.rsqrt(ms + EPS)) * (g * (1.0 + scale)) + shift


def _rms(x, g):
    return (x * lax.rsqrt(jnp.mean(x * x, axis=-1, keepdims=True) + EPS)) * g


def _modulation_kernel(c_ref, w_ref, b_ref, o_ref):
    c = c_ref[...]
    sc = (c * jax.nn.sigmoid(c)).astype(BF16)
    o_ref[...] = jnp.dot(sc, w_ref[...].astype(BF16), preferred_element_type=F32) + b_ref[...]


def _modulation(cond, w_mod, b_mod):
    depth, d, n = w_mod.shape
    rows = cond.shape[0]
    bn = _largest_divisor(n, (512, 256, 128))
    return pl.pallas_call(
        _modulation_kernel,
        grid=(depth, n // bn),
        in_specs=[pl.BlockSpec((rows, d), lambda l, j: (0, 0)),
                  pl.BlockSpec((None, d, bn), lambda l, j: (l, 0, j)),
                  pl.BlockSpec((None, 1, bn), lambda l, j: (l, 0, j))],
        out_specs=pl.BlockSpec((None, rows, bn), lambda l, j: (l, 0, j)),
        out_shape=jax.ShapeDtypeStruct((depth, rows, n), F32),
        compiler_params=_cparams(2, 32),
        name="modulation",
    )(cond, w_mod, b_mod.reshape(depth, 1, n))


def _rope_tables(n_tokens, rot_dim, lead, width):
    t = jnp.arange(n_tokens)
    row = (t // GRID_W).astype(F32)
    col = (t % GRID_W).astype(F32)
    half = rot_dim // 2
    inv = ROPE_BASE ** (-jnp.arange(0, half, 2, dtype=F32) / half)
    ar = row[:, None] * inv
    ac = col[:, None] * inv
    ang = jnp.concatenate([ar, ar, ac, ac], axis=-1)
    cos, sin = jnp.cos(ang), jnp.sin(ang)
    first = (np.arange(rot_dim) % ROPE_GROUP) < ROPE_GROUP // 2
    sin_a = jnp.where(first, -sin, 0.0)
    sin_b = jnp.where(first, 0.0, sin)
    pad = ((0, 0), (lead, width - lead - rot_dim))
    return (jnp.pad(cos, pad, constant_values=1.0), jnp.pad(sin_a, pad), jnp.pad(sin_b, pad))


def _rope_apply(y, cos, sin_a, sin_b):
    shift = ROPE_GROUP // 2
    return y * cos + pltpu.roll(y, LANE - shift, 1) * sin_a + pltpu.roll(y, shift, 1) * sin_b


def _fill_lhs(x_ref, xs_ref, xn_ref, prologue, g_ref, sh_ref, sc_ref, row_chunk):
    bm = x_ref.shape[0]

    def chunk(r, carry):
        rows = pl.ds(pl.multiple_of(r * row_chunk, row_chunk), row_chunk)
        x = x_ref[rows, :].astype(F32)
        if prologue == "norm_mod":
            x = _norm_mod(x, g_ref[...], sh_ref[...], sc_ref[...])
        elif prologue == "norm":
            x = _rms(x, g_ref[...])
        if xn_ref is not None:
            xn_ref[rows, :] = x
        xs_ref[rows, :] = x.astype(BF16)
        return carry
    n_chunks = bm // row_chunk
    lax.fori_loop(0, n_chunks, chunk, 0, unroll=2 if n_chunks % 2 == 0 else 1)


def _head_norm_store(acc, o_ref, hg_ref, tabs, head_w, norm_div, col0, norm_cols, rope_tiles):
    bn = acc.shape[1]
    period = tabs[0].shape[1] if tabs is not None else LANE
    lane = lax.broadcasted_iota(jnp.int32, (1, LANE), 1)
    for s0 in range(0, bn, max(head_w, LANE)):
        normed = None if norm_cols is None else (col0 + s0 < norm_cols)
        tiles = [acc[:, s0 + k * LANE:s0 + (k + 1) * LANE] for k in range(max(head_w, LANE) // LANE)]
        if head_w >= LANE:
            ssq = None
            for y in tiles:
                part = jnp.sum(y * y, axis=-1, keepdims=True)
                ssq = part if ssq is None else ssq + part
            inv = lax.rsqrt(ssq / norm_div + EPS)
        else:
            y2 = tiles[0] * tiles[0]
            low = lane < head_w
            s_lo = jnp.sum(jnp.where(low, y2, 0.0), axis=-1, keepdims=True)
            s_hi = jnp.sum(jnp.where(low, 0.0, y2), axis=-1, keepdims=True)
            inv = jnp.where(low, lax.rsqrt(s_lo / norm_div + EPS), lax.rsqrt(s_hi / norm_div + EPS))
        if normed is not None:
            inv = jnp.where(normed, inv, 1.0)
        for k, y in enumerate(tiles):
            c0 = s0 + k * LANE
            y = (y * inv) * hg_ref[:, c0:c0 + LANE]
            t0 = c0 % period
            if tabs is not None and rope_tiles[t0 // LANE]:
                rotated = _rope_apply(y, *(t[:, t0:t0 + LANE] for t in tabs))
                y = rotated if normed is None else jnp.where(normed, rotated, y)
            o_ref[:, c0:c0 + LANE] = y.astype(o_ref.dtype)


def _proj_kernel(*refs, prologue, emit_xn, epilogue, head_w, norm_div, norm_cols, rope, rope_tiles, row_chunk):
    it = iter(refs)
    x_ref = next(it)
    g_ref = next(it) if prologue is not None else None
    sh_ref, sc_ref = (next(it), next(it)) if prologue == "norm_mod" else (None, None)
    w_ref = next(it)
    if epilogue == "res":
        res_ref, gate_ref = next(it), next(it)
    if epilogue == "heads":
        hg_ref = next(it)
        tabs = (next(it), next(it), next(it)) if rope else None
    o_ref = next(it)
    xn_ref = next(it) if emit_xn else None
    xs_ref = next(it)
    j = pl.program_id(1)
    bn = o_ref.shape[1]

    @pl.when(j == 0)
    def _():
        _fill_lhs(x_ref, xs_ref, xn_ref, prologue, g_ref, sh_ref, sc_ref, row_chunk)

    acc = jnp.dot(xs_ref[...], w_ref[...], preferred_element_type=F32)
    if epilogue == "res":
        o_ref[...] = res_ref[...] + gate_ref[...] * acc
    elif epilogue == "heads":
        _head_norm_store(acc, o_ref, hg_ref, tabs, head_w, norm_div, j * bn, norm_cols, rope_tiles)
    else:
        o_ref[...] = acc.astype(o_ref.dtype)


def _proj(x, w, st, *, x_block=None, norm_g=None, mod=None, res=None, gate=None, heads=None, emit_xn=False,
          out_dtype=F32, bn=None, x_time_major=False, out_time_major=False, name="proj"):
    k, n = w.shape
    time_major = x_time_major or out_time_major
    bm = min(st.bm, st.seq) if time_major else st.bm
    tiles_per_seq = st.seq // bm if st.seq % bm == 0 else None
    if time_major:
        assert tiles_per_seq is not None and x_block is None
    if x_time_major:
        assert x.shape == (st.seq, st.nb * k)
        kidx = 0
    else:
        kx, kidx = x_block if x_block is not None else (x.shape[1], 0)
        assert kx == k and x.shape[0] == st.rows
    rows = st.rows
    prologue = None if norm_g is None else ("norm_mod" if mod is not None else "norm")
    epilogue = "res" if res is not None else ("heads" if heads is not None else None)
    rope = heads is not None and heads.get("tabs") is not None
    if bn is None:
        unit = LANE
        if epilogue == "heads":
            unit = max(heads["head_w"], LANE, heads["tabs"][0].shape[1] if rope else LANE)
        cap = 512 if epilogue == "res" else 1024
        bn = next((c for c in range(cap, unit - 1, -unit) if n % c == 0), n)
    mod_idx = lambda i: st.mod_index(i * bm)

    if x_time_major:
        in_specs = [pl.BlockSpec((bm, k), lambda i, j: (i % tiles_per_seq, i // tiles_per_seq))]
    else:
        in_specs = [pl.BlockSpec((bm, k), lambda i, j: (i, kidx))]
    args = [x]
    if prologue is not None:
        in_specs.append(pl.BlockSpec((1, k), lambda i, j: (0, 0)))
        args.append(norm_g)
    if prologue == "norm_mod":
        in_specs += [pl.BlockSpec((None, 1, k), lambda i, j: (mod_idx(i), 0, 0))] * 2
        args += list(mod)
    in_specs.append(pl.BlockSpec((k, bn), lambda i, j: (0, j)))
    args.append(w.astype(BF16))
    if epilogue == "res":
        in_specs += [pl.BlockSpec((bm, bn), lambda i, j: (i, j)),
                     pl.BlockSpec((None, 1, bn), lambda i, j: (mod_idx(i), 0, j))]
        args += [res, gate]
    head_w = norm_div = 0
    norm_cols = rope_tiles = None
    if epilogue == "heads":
        head_w, norm_div = heads["head_w"], heads["norm_div"]
        norm_cols = heads["norm_cols"] if heads["norm_cols"] < n else None
        assert bn % max(head_w, LANE) == 0 and heads["norm_cols"] % max(head_w, LANE) == 0
        in_specs.append(pl.BlockSpec((1, bn), lambda i, j: (0, j)))
        args.append(heads["gains"])
        if rope:
            period = heads["tabs"][0].shape[1]
            rope_tiles = heads["rope_tiles"]
            assert bn % period == 0 and tiles_per_seq is not None and len(rope_tiles) == period // LANE
            in_specs += [pl.BlockSpec((bm, period), lambda i, j: (i % tiles_per_seq, 0))] * 3
            args += list(heads["tabs"])
    if out_time_major:
        n_col_tiles = n // bn
        out_shape = [jax.ShapeDtypeStruct((st.seq, st.nb * n), out_dtype)]
        out_specs = [pl.BlockSpec((bm, bn), lambda i, j: (i % tiles_per_seq, (i // tiles_per_seq) * n_col_tiles + j))]
    else:
        out_shape = [jax.ShapeDtypeStruct((rows, n), out_dtype)]
        out_specs = [pl.BlockSpec((bm, bn), lambda i, j: (i, j))]
    if emit_xn:
        out_shape.append(jax.ShapeDtypeStruct((rows, k), F32))
        out_specs.append(pl.BlockSpec((bm, k), lambda i, j: (i, 0)))
    xbytes = x.dtype.itemsize
    vmem = (2 * bm * k * xbytes + bm * k * 2 + 2 * k * bn * 2 + (6 if epilogue == "res" else 4) * bm * bn * 4
            + (2 * bm * k * 4 if emit_xn else 0)) / MIB + 8
    kern = functools.partial(_proj_kernel, prologue=prologue, emit_xn=emit_xn, epilogue=epilogue, head_w=head_w,
                             norm_div=norm_div, norm_cols=norm_cols, rope=rope, rope_tiles=rope_tiles,
                             row_chunk=min(bm, 128))
    out = pl.pallas_call(
        kern,
        grid=(rows // bm, n // bn),
        in_specs=in_specs,
        out_specs=out_specs,
        out_shape=out_shape,
        scratch_shapes=[pltpu.VMEM((bm, k), BF16)],
        compiler_params=_cparams(2, vmem),
        name=name,
    )(*args)
    return out if emit_xn else out[0]


def _ffn_kernel(xp_ref, x_ref, xn_ref, g_ref, sh_ref, sc_ref, gate_ref, wa_ref, wb_ref, cw_ref, cb_ref,
                wo_ref, o_ref, h_ref, *, bm, seq, row_chunk):
    i = pl.program_id(0)
    c = pl.program_id(1)
    n_chunks = pl.num_programs(1)

    @pl.when(c == 0)
    def _():
        g, sh, sc = g_ref[...], sh_ref[...], sc_ref[...]
        h_ref[0:HALO, :] = _norm_mod(xp_ref[...], g, sh, sc).astype(BF16)
        h_ref[HALO + bm:, :] = _norm_mod(xn_ref[...], g, sh, sc).astype(BF16)

        def chunk(r, carry):
            src = pl.ds(pl.multiple_of(r * row_chunk, row_chunk), row_chunk)
            dst = pl.ds(pl.multiple_of(HALO + r * row_chunk, HALO), row_chunk)
            h_ref[dst, :] = _norm_mod(x_ref[src, :], g, sh, sc).astype(BF16)
            return carry
        n_row_chunks = bm // row_chunk
        lax.fori_loop(0, n_row_chunks, chunk, 0, unroll=2 if n_row_chunks % 2 == 0 else 1)
        o_ref[...] = jnp.zeros_like(o_ref)

    ua = jnp.dot(h_ref[...], wa_ref[...], preferred_element_type=F32)
    ub = jnp.dot(h_ref[HALO:HALO + bm, :], wb_ref[...], preferred_element_type=F32)
    n_all = bm + 2 * HALO
    u_prev = pltpu.roll(ua, 1, 0)[HALO:HALO + bm]
    u_next = pltpu.roll(ua, n_all - 1, 0)[HALO:HALO + bm]
    u_mid = ua[HALO:HALO + bm]
    pos = jnp.bitwise_and(i * bm + lax.broadcasted_iota(jnp.int32, (bm, 1), 0), seq - 1)
    u_prev = jnp.where(pos == 0, 0.0, u_prev)
    u_next = jnp.where(pos == seq - 1, 0.0, u_next)
    cw = cw_ref[...]
    a = cb_ref[...] + u_prev * cw[0:1] + u_mid * cw[1:2] + u_next * cw[2:3]
    gated = ((a * jax.nn.sigmoid(a)) * ub).astype(BF16)
    o_ref[...] += jnp.dot(gated, wo_ref[...], preferred_element_type=F32)

    @pl.when(c == n_chunks - 1)
    def _():
        o_ref[...] = x_ref[...] + gate_ref[...] * o_ref[...]


def _conv_ffn(x, st, g, shift, scale, gate, w_in, conv_w, conv_b, w_out, bm=1024, ck=512):
    m, d = x.shape
    d_ff = w_out.shape[0]
    bm = min(st.bm, bm)
    ck = _largest_divisor(d_ff, tuple(c for c in (512, 256, 128) if c <= ck))
    n_chunks = d_ff // ck
    n_halo_blocks = m // HALO
    assert st.seq & (st.seq - 1) == 0 and conv_w.shape[0] == 3
    mod_idx = lambda i: st.mod_index(i * bm)
    kern = functools.partial(_ffn_kernel, bm=bm, seq=st.seq, row_chunk=min(bm, 128))
    vmem = (4 * bm * d * 4 + (bm + 2 * HALO) * d * 2 + 6 * d * ck * 2 + 5 * (bm + 2 * HALO) * ck * 4) / MIB + 4
    return pl.pallas_call(
        kern,
        grid=(m // bm, n_chunks),
        in_specs=[
            pl.BlockSpec((HALO, d), lambda i, c: (jnp.maximum(i * (bm // HALO) - 1, 0), 0)),
            pl.BlockSpec((bm, d), lambda i, c: (i, 0)),
            pl.BlockSpec((HALO, d), lambda i, c: (jnp.minimum((i + 1) * (bm // HALO), n_halo_blocks - 1), 0)),
            pl.BlockSpec((1, d), lambda i, c: (0, 0)),
            pl.BlockSpec((None, 1, d), lambda i, c: (mod_idx(i), 0, 0)),
            pl.BlockSpec((None, 1, d), lambda i, c: (mod_idx(i), 0, 0)),
            pl.BlockSpec((None, 1, d), lambda i, c: (mod_idx(i), 0, 0)),
            pl.BlockSpec((d, ck), lambda i, c: (0, c)),
            pl.BlockSpec((d, ck), lambda i, c: (0, n_chunks + c)),
            pl.BlockSpec((conv_w.shape[0], ck), lambda i, c: (0, c)),
            pl.BlockSpec((1, ck), lambda i, c: (0, c)),
            pl.BlockSpec((ck, d), lambda i, c: (c, 0)),
        ],
        out_specs=pl.BlockSpec((bm, d), lambda i, c: (i, 0)),
        out_shape=jax.ShapeDtypeStruct((m, d), F32),
        scratch_shapes=[pltpu.VMEM((bm + 2 * HALO, d), BF16)],
        compiler_params=_cparams(2, vmem),
        name="conv_ffn",
    )(x, x, x, g, shift, scale, gate, w_in, w_in, conv_w, conv_b.reshape(1, d_ff), w_out)


def _qk(q, k):
    return lax.dot_general(q, k, (((1,), (1,)), ((), ())), preferred_element_type=F32)


def _attend(scores, values, sink=None):
    m = None
    for s in scores:
        mi = jnp.max(s, axis=-1, keepdims=True)
        m = mi if m is None else jnp.maximum(m, mi)
    if sink is not None:
        m = jnp.maximum(m, sink)
    es = [jnp.exp2(s - m) for s in scores]
    den = None
    for e in es:
        di = jnp.sum(e, axis=-1, keepdims=True)
        den = di if den is None else den + di
    if sink is not None:
        den = den + jnp.exp2(sink - m)
    out = None
    for e, v in zip(es, values):
        oi = jnp.dot(e.astype(BF16), v, preferred_element_type=F32)
        out = oi if out is None else out + oi
    return out * (1.0 / den)


def _ctx_attn_kernel(q_ref, k_ref, v_ref, *outs, heads, dq, dv, emit_kv):
    o_ref = outs[0]
    for h in range(heads):
        q = q_ref[:, h * dq:(h + 1) * dq].astype(BF16)
        k = k_ref[:, h * dq:(h + 1) * dq]
        v = v_ref[:, h * dv:(h + 1) * dv]
        if emit_kv:
            outs[1][h] = k.astype(F32)
            outs[2][h] = v.astype(F32)
        o = _attend([_qk(q, k.astype(BF16))], [v.astype(BF16)])
        o_ref[:, h * dv:(h + 1) * dv] = o.astype(o_ref.dtype)


def _ctx_attention(qm, km, vm, st, *, n_heads, dq, dv, q_col, k_col, v_col, emit_kv=False):
    hb = _largest_divisor(n_heads, (4, 2, 1))
    s = st.seq
    assert q_col % (hb * dq) == 0 and k_col % (hb * dq) == 0 and v_col % (hb * dv) == 0
    qo, ko, vo = q_col // (hb * dq), k_col // (hb * dq), v_col // (hb * dv)
    out_shape = [jax.ShapeDtypeStruct((st.rows, n_heads * dv), BF16)]
    out_specs = [pl.BlockSpec((s, hb * dv), lambda b, g: (b, g))]
    if emit_kv:
        out_shape += [jax.ShapeDtypeStruct((st.nb, n_heads, s, dq), F32),
                      jax.ShapeDtypeStruct((st.nb, n_heads, s, dv), F32)]
        out_specs += [pl.BlockSpec((None, hb, s, dq), lambda b, g: (b, g, 0, 0)),
                      pl.BlockSpec((None, hb, s, dv), lambda b, g: (b, g, 0, 0))]
    out = pl.pallas_call(
        functools.partial(_ctx_attn_kernel, heads=hb, dq=dq, dv=dv, emit_kv=emit_kv),
        grid=(st.nb, n_heads // hb),
        in_specs=[pl.BlockSpec((s, hb * dq), lambda b, g: (b, qo + g)),
                  pl.BlockSpec((s, hb * dq), lambda b, g: (b, ko + g)),
                  pl.BlockSpec((s, hb * dv), lambda b, g: (b, vo + g))],
        out_specs=out_specs,
        out_shape=out_shape,
        compiler_params=_cparams(2, 32),
        name="ctx_attention",
    )(qm, km, vm)
    return out if emit_kv else out[0]


def _nat_kernel(q_ref, k_ref, v_ref, kc_ref, vc_ref, bias_ref, o_ref, *, key_rows, rows, heads, dh):
    i = pl.program_id(2)
    n_keys = key_rows * GRID_W
    first_row = jnp.clip(i * NA_Q_ROWS - NA_WIN_ROWS // 2, 0, rows - key_rows)
    start = pl.multiple_of(first_row * GRID_W, GRID_W * 4)
    for h in range(heads):
        lanes = slice(h * dh, (h + 1) * dh)
        q = q_ref[:, lanes]
        k = k_ref[pl.ds(start, n_keys), lanes]
        v = v_ref[pl.ds(start, n_keys), lanes]
        s_loc = _qk(q, k) + bias_ref[h]
        s_ctx = _qk(q, kc_ref[h].astype(BF16))
        o_ref[:, lanes] = _attend([s_loc, s_ctx], [v, vc_ref[h].astype(BF16)]).astype(o_ref.dtype)


def _nat_bias(rpb, rows):
    n_blocks = rows // NA_Q_ROWS
    key_rows = min(NA_K_ROWS, rows)
    wr = min(NA_WIN_ROWS, rows)
    reps = [0, min(1, n_blocks - 1), n_blocks - 1]
    heads = rpb.shape[0]
    nq, nk = NA_Q_ROWS * GRID_W, key_rows * GRID_W
    shape = (NA_Q_ROWS, GRID_W, key_rows, GRID_W)
    qc = np.arange(GRID_W)
    cstart = np.clip(qc - NA_WIN_COLS // 2, 0, GRID_W - NA_WIN_COLS)
    col_ok = (qc[None, :] >= cstart[:, None]) & (qc[None, :] < cstart[:, None] + NA_WIN_COLS)
    rp = jnp.pad(rpb.astype(F32), ((0, 0), (key_rows, key_rows), (GRID_W - NA_WIN_COLS, GRID_W - NA_WIN_COLS)))
    row_slabs, mask_l = [], []
    for i in reps:
        ks = int(np.clip(i * NA_Q_ROWS - NA_WIN_ROWS // 2, 0, rows - key_rows))
        r = i * NA_Q_ROWS + np.arange(NA_Q_ROWS)
        rs = np.clip(r - wr // 2, 0, rows - wr)
        kr = ks + np.arange(key_rows)
        row_ok = (kr[None, :] >= rs[:, None]) & (kr[None, :] < rs[:, None] + wr)
        for rq in range(NA_Q_ROWS):
            first = ks - int(r[rq]) + NA_WIN_ROWS - 1 + key_rows
            assert 0 <= first and first + key_rows <= rp.shape[1]
            row_slabs.append(rp[:, first:first + key_rows, :])
        mask_l.append(np.broadcast_to(row_ok[:, None, :, None] & col_ok[None, :, None, :], shape).reshape(nq, nk))
    slab = jnp.stack(row_slabs, axis=1).reshape(heads, len(reps), NA_Q_ROWS, key_rows, 2 * GRID_W - 1)
    toep = jnp.stack([slab[..., GRID_W - 1 - c:2 * GRID_W - 1 - c] for c in range(GRID_W)], axis=3)
    bias = toep.reshape(heads, len(reps), nq, nk)
    return jnp.where(jnp.asarray(np.stack(mask_l))[None], bias * LOG2E, NEG)


def _nat_attention(qkv, st, cache_k, cache_v, j, rpb, dh):
    heads = rpb.shape[0]
    n = st.seq
    p = cache_k.shape[3]
    rows = n // GRID_W
    assert rows % NA_Q_ROWS == 0 and rows >= NA_K_ROWS and dh % LANE == 0
    n_blocks = rows // NA_Q_ROWS
    key_rows = min(NA_K_ROWS, rows)
    nq, nk = NA_Q_ROWS * GRID_W, key_rows * GRID_W
    bias = _nat_bias(rpb, rows)
    btype = lambda i: jnp.where(i == 0, 0, jnp.where(i == n_blocks - 1, 2, 1))
    hb = _largest_divisor(heads, (4, 2, 1))
    hg = heads // hb
    kern = functools.partial(_nat_kernel, key_rows=key_rows, rows=rows, heads=hb, dh=dh)
    return pl.pallas_call(
        kern,
        grid=(st.nb, hg, n_blocks),
        in_specs=[pl.BlockSpec((nq, hb * dh), lambda b, h, i: (b * n_blocks + i, h)),
                  pl.BlockSpec((n, hb * dh), lambda b, h, i: (b, hg + h)),
                  pl.BlockSpec((n, hb * dh), lambda b, h, i: (b, 2 * hg + h)),
                  pl.BlockSpec((None, None, hb, p, dh), lambda b, h, i: (b, j, h, 0, 0)),
                  pl.BlockSpec((None, None, hb, p, dh), lambda b, h, i: (b, j, h, 0, 0)),
                  pl.BlockSpec((hb, None, nq, nk), lambda b, h, i: (h, btype(i), 0, 0))],
        out_specs=pl.BlockSpec((nq, hb * dh), lambda b, h, i: (b * n_blocks + i, h)),
        out_shape=jax.ShapeDtypeStruct((st.rows, heads * dh), BF16),
        compiler_params=_cparams(3, 56),
        name="nat_attention",
    )(qkv, qkv, qkv, cache_k, cache_v, bias)


def _joint_dense_kernel(q_ref, k_ref, v_ref, kc_ref, vc_ref, o_ref, *, chunk):
    q = q_ref[...]
    n = k_ref.shape[0]
    pieces = [(k_ref, v_ref, c0, min(chunk, n - c0)) for c0 in range(0, n, chunk)]
    pieces.append((kc_ref, vc_ref, 0, kc_ref.shape[0]))
    m = den = acc = None
    for kr, vr, c0, size in pieces:
        s = _qk(q, kr[c0:c0 + size, :])
        mc = jnp.max(s, axis=-1, keepdims=True)
        m_new = mc if m is None else jnp.maximum(m, mc)
        e = jnp.exp2(s - m_new)
        dc = jnp.sum(e, axis=-1, keepdims=True)
        pv = jnp.dot(e.astype(BF16), vr[c0:c0 + size, :], preferred_element_type=F32)
        if m is None:
            den, acc = dc, pv
        else:
            alpha = jnp.exp2(m - m_new)
            den, acc = alpha * den + dc, alpha * acc + pv
        m = m_new
    o_ref[...] = (acc * (1.0 / den)).astype(o_ref.dtype)


def _joint_dense_attention(qm, km, vm, kcm, vcm, st, p, *, n_heads, dq, dv):
    n = st.seq
    bq = _largest_divisor(n, (1024, 512, 256, 128, 64, 32, 16))
    nqb = n // bq
    return pl.pallas_call(
        functools.partial(_joint_dense_kernel, chunk=1024),
        grid=(st.nb, n_heads, nqb),
        in_specs=[pl.BlockSpec((bq, dq), lambda b, h, i: (b * nqb + i, h)),
                  pl.BlockSpec((n, dq), lambda b, h, i: (b, h)),
                  pl.BlockSpec((n, dv), lambda b, h, i: (b, h)),
                  pl.BlockSpec((p, dq), lambda b, h, i: (b, h)),
                  pl.BlockSpec((p, dv), lambda b, h, i: (b, h))],
        out_specs=pl.BlockSpec((bq, dv), lambda b, h, i: (b * nqb + i, h)),
        out_shape=jax.ShapeDtypeStruct((st.rows, n_heads * dv), BF16),
        compiler_params=_cparams(3, 48),
        name="mla_attention",
    )(qm, km, vm, kcm, vcm)


def _both_halves(x, s):
    x = x.astype(F32)
    low = lax.broadcasted_iota(jnp.int32, (1, LANE), 1) < LANE // 2
    keep = low if s == 0 else jnp.logical_not(low)
    return jnp.where(keep, x, pltpu.roll(x, LANE // 2, 1)).astype(BF16)


def _swa_step(sinks_ref, pair, q_ref, k, v, kc, vc, o_ref, *, dh, groups, local_mask):
    kv_per_step = LANE // dh
    assert kv_per_step == 2 and groups % 2 == 0
    rows = q_ref.shape[0]
    low = lax.broadcasted_iota(jnp.int32, (1, LANE), 1) < dh
    row_group = lax.broadcasted_iota(jnp.int32, (groups * rows, 1), 0) // rows
    for s in range(kv_per_step):
        kd, vd = _both_halves(k, s), _both_halves(v, s)
        q_parts = []
        for g in range(groups):
            c0 = ((s * groups + g) * dh // LANE) * LANE
            tile = q_ref[:, c0:c0 + LANE].astype(BF16)
            q_parts.append(jnp.where(low if g % 2 == 0 else jnp.logical_not(low), tile, jnp.zeros_like(tile)))
        q = jnp.concatenate(q_parts, axis=0)
        sink = jnp.zeros((groups * rows, 1), F32)
        for g in range(groups):
            sink = jnp.where(row_group == g, sinks_ref[(pair * kv_per_step + s) * groups + g], sink)
        s_loc = _qk(q, kd)
        if local_mask is not None:
            s_loc = jnp.where(local_mask, s_loc, NEG)
        if kc is not None:
            kcd = jnp.concatenate([kc[s], kc[s]], axis=-1).astype(BF16)
            vcd = jnp.concatenate([vc[s], vc[s]], axis=-1).astype(BF16)
            out = _attend([s_loc, _qk(q, kcd)], [vd, vcd], sink)
        else:
            out = _attend([s_loc], [vd], sink)
        for g in range(0, groups, 2):
            c0 = (s * groups + g) * dh
            o_ref[:, c0:c0 + LANE] = jnp.where(low, out[g * rows:(g + 1) * rows],
                                               out[(g + 1) * rows:(g + 2) * rows]).astype(o_ref.dtype)


def _swa_ctx_kernel(sinks_ref, q_ref, k_ref, v_ref, o_ref, ko_ref, vo_ref, *, dh, groups):
    pair = pl.program_id(1)
    k, v = k_ref[...], v_ref[...]
    for s in range(LANE // dh):
        ko_ref[s] = k[:, s * dh:(s + 1) * dh].astype(F32)
        vo_ref[s] = v[:, s * dh:(s + 1) * dh].astype(F32)
    _swa_step(sinks_ref, pair, q_ref, k, v, None, None, o_ref, dh=dh, groups=groups, local_mask=None)


def _swa_lat_kernel(sinks_ref, q_ref, k_ref, v_ref, kc_ref, vc_ref, o_ref, *, dh, groups, n):
    pair = pl.program_id(1)
    blk = pl.program_id(2)
    n_keys = min(3 * SWA_BLOCK, n)
    start = pl.multiple_of(jnp.clip((blk - 1) * SWA_BLOCK, 0, n - n_keys), SWA_BLOCK)
    k = k_ref[pl.ds(start, n_keys), :]
    v = v_ref[pl.ds(start, n_keys), :]
    qpos = blk * SWA_BLOCK + lax.broadcasted_iota(jnp.int32, (groups * SWA_BLOCK, 1), 0) % SWA_BLOCK
    kpos = start + lax.broadcasted_iota(jnp.int32, (1, n_keys), 1)
    mask = jnp.abs(qpos - kpos) <= SWA_WINDOW
    _swa_step(sinks_ref, pair, q_ref, k, v, kc_ref, vc_ref, o_ref, dh=dh, groups=groups, local_mask=mask)


def _swa_attention(qkv, st, sinks, *, heads, kvh, dh, cache=None):
    groups = heads // kvh
    kv_per_step = LANE // dh
    assert LANE % dh == 0 and kvh % kv_per_step == 0 and groups % kv_per_step == 0
    pairs = kvh // kv_per_step
    qw = kv_per_step * groups * dh
    k_blk = heads * dh // LANE
    v_blk = (heads + kvh) * dh // LANE
    n = st.seq
    common = dict(dh=dh, groups=groups)
    smem = pl.BlockSpec(memory_space=pltpu.SMEM)
    if cache is None:
        out = pl.pallas_call(
            functools.partial(_swa_ctx_kernel, **common),
            grid=(st.nb, pairs),
            in_specs=[smem,
                      pl.BlockSpec((n, qw), lambda b, c: (b, c)),
                      pl.BlockSpec((n, LANE), lambda b, c: (b, k_blk + c)),
                      pl.BlockSpec((n, LANE), lambda b, c: (b, v_blk + c))],
            out_specs=[pl.BlockSpec((n, qw), lambda b, c: (b, c)),
                       pl.BlockSpec((None, kv_per_step, n, dh), lambda b, c: (b, c, 0, 0)),
                       pl.BlockSpec((None, kv_per_step, n, dh), lambda b, c: (b, c, 0, 0))],
            out_shape=[jax.ShapeDtypeStruct((st.rows, heads * dh), BF16),
                       jax.ShapeDtypeStruct((st.nb, kvh, n, dh), F32),
                       jax.ShapeDtypeStruct((st.nb, kvh, n, dh), F32)],
            compiler_params=_cparams(2, 32),
            name="swa_ctx_attention",
        )(sinks, qkv, qkv, qkv)
        return out
    cache_k, cache_v, j = cache
    p = cache_k.shape[3]
    nblk = n // SWA_BLOCK
    assert n % SWA_BLOCK == 0
    return pl.pallas_call(
        functools.partial(_swa_lat_kernel, n=n, **common),
        grid=(st.nb, pairs, nblk),
        in_specs=[smem,
                  pl.BlockSpec((SWA_BLOCK, qw), lambda b, c, i: (b * nblk + i, c)),
                  pl.BlockSpec((n, LANE), lambda b, c, i: (b, k_blk + c)),
                  pl.BlockSpec((n, LANE), lambda b, c, i: (b, v_blk + c)),
                  pl.BlockSpec((None, None, kv_per_step, p, dh), lambda b, c, i: (b, j, c, 0, 0)),
                  pl.BlockSpec((None, None, kv_per_step, p, dh), lambda b, c, i: (b, j, c, 0, 0))],
        out_specs=pl.BlockSpec((SWA_BLOCK, qw), lambda b, c, i: (b * nblk + i, c)),
        out_shape=jax.ShapeDtypeStruct((st.rows, heads * dh), BF16),
        compiler_params=_cparams(3, 32),
        name="swa_attention",
    )(sinks, qkv, qkv, qkv, cache_k, cache_v)


def _mla_kv_kernel(*refs, norm, emit_xn, rope, row_chunk, norm_div):
    it = iter(refs)
    x_ref = next(it)
    g_ref = next(it) if norm else None
    w_ref, kr_ref, g1_ref, g2_ref = next(it), next(it), next(it), next(it)
    tabs = (next(it), next(it), next(it)) if rope else None
    k_ref, v_ref = next(it), next(it)
    xn_ref = next(it) if emit_xn else None
    xs_ref = next(it)

    @pl.when(pl.program_id(1) == 0)
    def _():
        _fill_lhs(x_ref, xs_ref, xn_ref, "norm" if norm else None, g_ref, None, None, row_chunk)

    acc = jnp.dot(xs_ref[...], w_ref[...], preferred_element_type=F32)
    kr = kr_ref[...]
    kr_ssq = jnp.sum(kr * kr, axis=-1, keepdims=True)
    for h in range(acc.shape[1] // (2 * LANE)):
        nope = acc[:, 2 * h * LANE:(2 * h + 1) * LANE]
        inv = lax.rsqrt((jnp.sum(nope * nope, axis=-1, keepdims=True) + kr_ssq) / norm_div + EPS)
        k_rot = (kr * inv) * g2_ref[...]
        if rope:
            k_rot = _rope_apply(k_rot, *(t[...] for t in tabs))
        k_ref[:, 2 * h * LANE:(2 * h + 1) * LANE] = ((nope * inv) * g1_ref[...]).astype(k_ref.dtype)
        k_ref[:, (2 * h + 1) * LANE:(2 * h + 2) * LANE] = k_rot.astype(k_ref.dtype)
        v_ref[:, h * LANE:(h + 1) * LANE] = acc[:, (2 * h + 1) * LANE:(2 * h + 2) * LANE].astype(v_ref.dtype)


def _mla_kv(x, x_block, w_ukv, kr, kr_block, g_kva, g1, g2, tabs, st, *, n_heads, norm_div, emit_xn, name):
    rows = x.shape[0]
    k, n = w_ukv.shape
    head_n = n // n_heads
    assert head_n == 2 * LANE, "nope and value widths must both be one lane tile"
    kx, kidx = x_block
    krw, kridx = kr_block
    assert kx == k and krw == LANE
    bm = st.bm
    norm = g_kva is not None
    rope = tabs is not None
    in_specs = [pl.BlockSpec((bm, k), lambda i, h: (i, kidx))]
    args = [x]
    if norm:
        in_specs.append(pl.BlockSpec((1, k), lambda i, h: (0, 0)))
        args.append(g_kva)
    hb = _largest_divisor(n_heads, (4, 2, 1))
    in_specs += [pl.BlockSpec((k, hb * head_n), lambda i, h: (0, h)),
                 pl.BlockSpec((bm, LANE), lambda i, h: (i, kridx)),
                 pl.BlockSpec((1, LANE), lambda i, h: (0, 0)),
                 pl.BlockSpec((1, LANE), lambda i, h: (0, 0))]
    args += [w_ukv, kr, g1, g2]
    if rope:
        tiles_per_seq = st.seq // bm
        in_specs += [pl.BlockSpec((bm, LANE), lambda i, h: (i % tiles_per_seq, 0))] * 3
        args += list(tabs)
    out_shape = [jax.ShapeDtypeStruct((rows, n_heads * 2 * LANE), BF16),
                 jax.ShapeDtypeStruct((rows, n_heads * LANE), BF16)]
    out_specs = [pl.BlockSpec((bm, hb * 2 * LANE), lambda i, h: (i, h)),
                 pl.BlockSpec((bm, hb * LANE), lambda i, h: (i, h))]
    if emit_xn:
        out_shape.append(jax.ShapeDtypeStruct((rows, k), F32))
        out_specs.append(pl.BlockSpec((bm, k), lambda i, h: (i, 0)))
    kern = functools.partial(_mla_kv_kernel, norm=norm, emit_xn=emit_xn, rope=rope, row_chunk=min(bm, 128),
                             norm_div=norm_div)
    return pl.pallas_call(
        kern,
        grid=(rows // bm, n_heads // hb),
        in_specs=in_specs,
        out_specs=out_specs,
        out_shape=out_shape,
        scratch_shapes=[pltpu.VMEM((bm, k), BF16)],
        compiler_params=_cparams(2, 40),
        name=name,
    )(*args)


def _band_plan(width, block):
    n_tiles = width // LANE
    lo = [((t * LANE) // block) * block for t in range(n_tiles)]
    hi = [(((t + 1) * LANE - 1) // block + 1) * block for t in range(n_tiles)]
    start = [(l // LANE) * LANE for l in lo]
    kb = max(-(-(h - s) // LANE) * LANE for h, s in zip(hi, start))
    kb = min(kb, width)
    start = [min(s, width - kb) for s in start]
    return start, kb


def _band_weights(w, width, block, start, kb):
    n_tiles = width // LANE
    wb = w.astype(BF16)
    tiles = []
    for t in range(n_tiles):
        pieces = []
        col = t * LANE
        while col < (t + 1) * LANE:
            blk = col // block
            col_end = min((blk + 1) * block, (t + 1) * LANE)
            sub = wb[blk, :, col - blk * block:col_end - blk * block]
            top = blk * block - start[t]
            pieces.append(jnp.pad(sub, ((top, kb - top - block), (0, 0))))
            col = col_end
        tiles.append(jnp.concatenate(pieces, axis=1))
    return jnp.stack(tiles)


def _gelu_tanh(x):
    cdf = 0.5 * (1.0 + jnp.tanh(np.float32(np.sqrt(2.0 / np.pi)) * (x + 0.044715 * (x * x * x))))
    return x * cdf


def _sigmoid(x):
    return 0.5 * (1.0 + jnp.tanh(0.5 * x))


def _lru_pass_kernel(*refs, reverse, starts, kb, bt, seq, nb, taps):
    left = taps // 2
    right = taps - 1 - left
    it = iter(refs)
    xp_ref, x_ref = next(it), next(it)
    xn_ref = next(it) if right > 0 else None
    cw_ref, cb_ref = next(it), next(it)
    wa_ref, wi_ref, ba_ref, bi_ref, lam_ref, h0_ref = (next(it) for _ in range(6))
    hsf_ref, gate_ref = (next(it), next(it)) if reverse else (None, None)
    out_ref, ht_ref = next(it), next(it)
    xc_s, xb_s, a_s, bx_s, carry = next(it), next(it), next(it), next(it), next(it)

    step = pl.program_id(0)
    n_steps = pl.num_programs(0)
    tile = (n_steps - 1 - step) if reverse else step
    n_tiles = len(starts)
    rows = nb * bt

    @pl.when(step == 0)
    def _():
        carry[...] = h0_ref[...]

    for t in range(n_tiles):
        lanes = slice(t * LANE, (t + 1) * LANE)
        parts = [jnp.where(tile > 0, xp_ref[:, :, lanes], 0.0), x_ref[:, :, lanes]]
        if right > 0:
            parts.append(jnp.where(tile < n_steps - 1, xn_ref[:, :, lanes], 0.0))
        full = jnp.concatenate(parts, axis=0)
        acc = jnp.broadcast_to(cb_ref[:, lanes], (bt, nb, LANE))
        for k in range(taps):
            acc = acc + full[k:k + bt] * cw_ref[k:k + 1, lanes]
        acc = acc.reshape(rows, LANE)
        xc_s[:, lanes] = acc
        xb_s[:, lanes] = acc.astype(BF16)

    neg_lam = -lam_ref[...]
    softplus = jnp.maximum(neg_lam, 0.0) + jnp.log1p(jnp.exp(-jnp.abs(neg_lam)))
    for t in range(n_tiles):
        lanes = slice(t * LANE, (t + 1) * LANE)
        xw = xb_s[:, starts[t]:starts[t] + kb]
        r = _sigmoid(jnp.dot(xw, wa_ref[t], preferred_element_type=F32) + ba_ref[:, lanes])
        ig = _sigmoid(jnp.dot(xw, wi_ref[t], preferred_element_type=F32) + bi_ref[:, lanes])
        log_a = -LRU_C * r * softplus[:, lanes]
        a = jnp.exp(log_a)
        a_s[:, lanes] = a
        bx_s[:, lanes] = jnp.sqrt(-jnp.tanh(log_a) * (a * a + 1.0)) * (ig * xc_s[:, lanes])

    h = carry[...]
    for s in range(bt):
        ts = (bt - 1 - s) if reverse else s
        slab = slice(ts * nb, (ts + 1) * nb)
        h = a_s[slab, :] * h + bx_s[slab, :]
        a_s[slab, :] = h
    carry[...] = h
    ht_ref[...] = h
    hs = a_s[...].reshape(bt, nb, a_s.shape[1])
    if reverse:
        out_ref[...] = (_gelu_tanh(gate_ref[...]) * (hsf_ref[...] + hs)).astype(out_ref.dtype)
    else:
        out_ref[...] = hs


def _lru_pass(u, st, conv_w, conv_b, wa, wi, b_a, b_i, lam, h0, starts, kb, *, reverse, hs_fwd=None):
    c = conv_w.shape[1]
    taps = conv_w.shape[0]
    left, right = taps // 2, taps - 1 - taps // 2
    nb, seq = st.nb, st.seq
    bt = min(max(256 // nb, SUBLANE), seq)
    assert seq % bt == 0 and c % LANE == 0 and left > 0 and bt % left == 0 and (right == 0 or bt % right == 0)
    nt = seq // bt
    n_tiles = c // LANE
    u3 = u.reshape(seq, nb, 2 * c)
    tmap = (lambda s: nt - 1 - s) if reverse else (lambda s: s)
    full = lambda *shape: pl.BlockSpec(shape, lambda s: (0,) * len(shape))
    in_specs = [pl.BlockSpec((left, nb, c), lambda s: (jnp.maximum(tmap(s) * (bt // left) - 1, 0), 0, 0)),
                pl.BlockSpec((bt, nb, c), lambda s: (tmap(s), 0, 0))]
    args = [u3, u3]
    if right > 0:
        in_specs.append(pl.BlockSpec((right, nb, c),
                                     lambda s: (jnp.minimum((tmap(s) + 1) * (bt // right), seq // right - 1), 0, 0)))
        args.append(u3)
    in_specs += [full(taps, c), full(1, c), full(n_tiles, kb, LANE), full(n_tiles, kb, LANE), full(1, c), full(1, c),
                 full(1, c), full(nb, c)]
    args += [conv_w, conv_b.reshape(1, c), wa, wi, b_a.reshape(1, c), b_i.reshape(1, c), lam.reshape(1, c), h0]
    if reverse:
        in_specs += [pl.BlockSpec((bt, nb, c), lambda s: (tmap(s), 0, 0)),
                     pl.BlockSpec((bt, nb, c), lambda s: (tmap(s), 0, 1))]
        args += [hs_fwd, u3]
    kern = functools.partial(_lru_pass_kernel, reverse=reverse, starts=tuple(starts), kb=kb, bt=bt, seq=seq, nb=nb,
                             taps=taps)
    blk = nb * bt * c * 4 / MIB
    vmem = (2 + 2 + 3 + (4 if reverse else 0) + 4) * blk + 4 * n_tiles * kb * LANE * 2 / MIB + 8
    return pl.pallas_call(
        kern,
        grid=(nt,),
        in_specs=in_specs,
        out_specs=[pl.BlockSpec((bt, nb, c), lambda s: (tmap(s), 0, 0)),
                   pl.BlockSpec((nb, c), lambda s: (0, 0))],
        out_shape=[jax.ShapeDtypeStruct((seq, nb, c), F32),
                   jax.ShapeDtypeStruct((nb, c), F32)],
        scratch_shapes=[pltpu.VMEM((nb * bt, c), F32), pltpu.VMEM((nb * bt, c), BF16),
                        pltpu.VMEM((nb * bt, c), F32), pltpu.VMEM((nb * bt, c), F32), pltpu.VMEM((nb, c), F32)],
        compiler_params=_cparams(1, vmem),
        name="lru_bwd" if reverse else "lru_fwd",
    )(*args)


def _mixer_nat(xs, streams, mods, cache_k, cache_v, j, w_qkv, g_mix, g_q, g_k, rpb, w_o):
    heads, dh = rpb.shape[0], g_q.shape[0]
    w_qkv, w_o = w_qkv.astype(BF16), w_o.astype(BF16)
    gains = jnp.concatenate([jnp.tile(g_q * (dh ** -0.5 * LOG2E), heads), jnp.tile(g_k, heads),
                             jnp.ones((heads * dh,), F32)])[None]
    spec = dict(head_w=dh, norm_div=dh, norm_cols=2 * heads * dh, gains=gains)
    new_x, extra = [], None
    for x, st in zip(xs, streams):
        latent = not st.shared
        qkv = _proj(x, w_qkv, st, norm_g=g_mix, mod=(mods[0], mods[1]), heads=spec,
                    out_dtype=BF16 if latent else F32, name="nat_qkv")
        if latent:
            o = _nat_attention(qkv, st, cache_k, cache_v, j, rpb, dh)
        else:
            o, kc, vc = _ctx_attention(qkv, qkv, qkv, st, n_heads=heads, dq=dh, dv=dh, q_col=0, k_col=heads * dh,
                                       v_col=2 * heads * dh, emit_kv=True)
            extra = (kc, vc)
        new_x.append(_proj(o, w_o, st, res=x, gate=mods[2], name="nat_out"))
    return new_x, extra


def _mixer_lru(xs, streams, mods, state, w_in, g_mix, conv_w, conv_b, w_a, b_a, w_i, b_i, lam, w_out):
    c = conv_w.shape[1]
    block = w_a.shape[-1]
    w_in, w_out = w_in.astype(BF16), w_out.astype(BF16)
    starts, kb = _band_plan(c, block)
    wa = [_band_weights(w_a[d], c, block, starts, kb) for d in range(2)]
    wi = [_band_weights(w_i[d], c, block, starts, kb) for d in range(2)]
    new_x, st_out = [], None
    for x, st in zip(xs, streams):
        latent = not st.shared
        h0 = state.astype(F32) if latent else jnp.zeros((st.nb, 2, c), F32)
        u = _proj(x, w_in, st, norm_g=g_mix, mod=(mods[0], mods[1]), out_time_major=True, name="lru_in")
        hs_f, t_f = _lru_pass(u, st, conv_w, conv_b, wa[0], wi[0], b_a[0], b_i[0], lam[0], h0[:, 0], starts, kb,
                              reverse=False)
        y, t_b = _lru_pass(u, st, conv_w, conv_b, wa[1], wi[1], b_a[1], b_i[1], lam[1], h0[:, 1], starts, kb,
                           reverse=True, hs_fwd=hs_f)
        if not latent:
            st_out = jnp.stack([t_f, t_b], axis=1)
        new_x.append(_proj(y.reshape(st.seq, st.nb * c), w_out, st, res=x, gate=mods[2], x_time_major=True,
                           name="lru_out"))
    return new_x, st_out


def _mixer_mla(xs, streams, mods, cache_ckv, cache_kr, w_down, g_mix, g_qa, g_kva, w_uq, w_ukv, g_q, g_k, w_o):
    d_model = w_down.shape[0]
    q_rank, kv_rank = g_qa.shape[0], g_kva.shape[0]
    qk_dim = g_q.shape[0]
    heads = w_uq.shape[1] // qk_dim
    rope = w_down.shape[1] - q_rank - kv_rank
    nope = qk_dim - rope
    assert nope == LANE and rope <= LANE and kv_rank % LANE == 0 and q_rank % LANE == 0
    head_w = 2 * LANE
    q_pad = -q_rank % kv_rank
    kv_col = q_rank + q_pad
    tail_pad = -(kv_col + kv_rank + rope) % 512
    w_dn = jnp.concatenate([w_down[:, :q_rank], jnp.zeros((d_model, q_pad), F32),
                            w_down[:, q_rank:q_rank + kv_rank], w_down[:, q_rank + kv_rank:],
                            jnp.zeros((d_model, tail_pad), F32)], axis=1).astype(BF16)
    kr_blk = (kv_col + kv_rank) // LANE
    w_q = jnp.pad(w_uq.reshape(q_rank, heads, qk_dim), ((0, 0), (0, 0), (0, head_w - qk_dim)))
    w_q = w_q.reshape(q_rank, heads * head_w).astype(BF16)
    w_ukv, w_o = w_ukv.astype(BF16), w_o.astype(BF16)
    gq = jnp.tile(jnp.pad(g_q * (qk_dim ** -0.5 * LOG2E), (0, head_w - qk_dim)), heads)[None]
    g1, g2 = g_k[None, :nope], jnp.pad(g_k[nope:], (0, LANE - rope))[None]
    p = cache_ckv.shape[1]
    new_x, extra = [], None
    for x, st in zip(xs, streams):
        latent = not st.shared
        d = _proj(x, w_dn, st, norm_g=g_mix, mod=(mods[0], mods[1]), name="mla_down")
        q_tabs = _rope_tables(st.seq, rope, nope, head_w) if latent else None
        k_tabs = _rope_tables(st.seq, rope, 0, LANE) if latent else None
        q = _proj(d, w_q, st, x_block=(q_rank, 0), norm_g=g_qa[None],
                  heads=dict(head_w=head_w, norm_div=qk_dim, norm_cols=heads * head_w, gains=gq, tabs=q_tabs,
                             rope_tiles=(False, True)),
                  out_dtype=BF16, name="mla_uq")
        kv = _mla_kv(d, (kv_rank, kv_col // kv_rank), w_ukv, d, (LANE, kr_blk), g_kva[None], g1, g2, k_tabs, st,
                     n_heads=heads, norm_div=qk_dim, emit_xn=not latent, name="mla_ukv")
        if latent:
            k, v = kv
            cst = _Stream(st.nb, p, 0, True)
            krc = jnp.pad(cache_kr.reshape(st.nb * p, rope), ((0, 0), (0, LANE - rope)))
            kc, vc = _mla_kv(cache_ckv.reshape(st.nb * p, kv_rank), (kv_rank, 0), w_ukv, krc, (LANE, 0), None,
                             g1, g2, None, cst, n_heads=heads, norm_div=qk_dim, emit_xn=False,
                             name="mla_ukv_cache")
            o = _joint_dense_attention(q, k, v, kc, vc, st, p, n_heads=heads, dq=head_w, dv=LANE)
        else:
            k, v, ckv = kv
            o = _ctx_attention(q, k, v, st, n_heads=heads, dq=head_w, dv=LANE, q_col=0, k_col=0, v_col=0)
            kr_out = d[:, kv_col + kv_rank:kv_col + kv_rank + rope]
            extra = (ckv.reshape(st.nb, st.seq, kv_rank), kr_out.reshape(st.nb, st.seq, rope))
        new_x.append(_proj(o, w_o, st, res=x, gate=mods[2], name="mla_out"))
    return new_x, extra


def _mixer_swa(xs, streams, mods, cache_k, cache_v, j, w_qkv, g_mix, g_q, g_k, sinks, w_o):
    dh = g_q.shape[0]
    heads = sinks.shape[0]
    kvh = (w_qkv.shape[1] // dh - heads) // 2
    w_qkv, w_o = w_qkv.astype(BF16), w_o.astype(BF16)
    gains = jnp.concatenate([jnp.tile(g_q * (dh ** -0.5 * LOG2E), heads), jnp.tile(g_k, kvh),
                             jnp.ones((kvh * dh,), F32)])[None]
    sinks = sinks.astype(F32) * LOG2E
    new_x, extra = [], None
    for x, st in zip(xs, streams):
        latent = not st.shared
        tabs = _rope_tables(st.seq, dh, 0, dh) if latent else None
        if tabs is not None:
            tabs = tuple(jnp.tile(t, (1, LANE // dh)) for t in tabs)
        spec = dict(head_w=dh, norm_div=dh, norm_cols=(heads + kvh) * dh, gains=gains, tabs=tabs,
                    rope_tiles=(True,))
        qkv = _proj(x, w_qkv, st, norm_g=g_mix, mod=(mods[0], mods[1]), heads=spec,
                    out_dtype=BF16 if latent else F32, name="swa_qkv")
        if latent:
            o = _swa_attention(qkv, st, sinks, heads=heads, kvh=kvh, dh=dh, cache=(cache_k, cache_v, j))
        else:
            o, kc, vc = _swa_attention(qkv, st, sinks, heads=heads, kvh=kvh, dh=dh)
            extra = (kc, vc)
        new_x.append(_proj(o, w_o, st, res=x, gate=mods[2], name="swa_out"))
    return new_x, extra


def kernel(x_prompt, x_sample, cache_nat_k, cache_nat_v, state_lru, cache_mla_ckv, cache_mla_krope, cache_swa_k, cache_swa_v, c, c_ctx, norm_mix, norm_ffn, w_mod, b_mod, ffn_w_in, ffn_conv_w, ffn_conv_b, ffn_w_out, nat_w_qkv, nat_q_norm, nat_k_norm, nat_rpb, nat_w_o, lru_w_in, lru_conv_w, lru_conv_b, lru_w_a, lru_b_a, lru_w_i, lru_b_i, lru_lambda, lru_w_out, mla_w_down, mla_q_a_norm, mla_kv_a_norm, mla_w_uq, mla_w_ukv, mla_q_norm, mla_k_norm, mla_w_o, swa_w_qkv, swa_q_norm, swa_k_norm, swa_sinks, swa_w_o):
    bc, sc, d = x_prompt.shape
    bl, n, _ = x_sample.shape
    depth = w_mod.shape[0]
    streams = (_Stream(bc, sc, 0, True), _Stream(bl, n, 1, False))
    xs = [x_prompt.reshape(bc * sc, d), x_sample.reshape(bl * n, d)]

    n_cond = 1 + bl
    cond_rows = -(-n_cond // SUBLANE) * SUBLANE
    cond = jnp.zeros((cond_rows, d), F32).at[0].set(c_ctx).at[1:n_cond].set(c)
    mods = _modulation(cond, w_mod, b_mod)[:, :n_cond]

    nat_k_l, nat_v_l, lru_l, ckv_l, krope_l, swa_k_l, swa_v_l = [], [], [], [], [], [], []
    for l in range(depth):
        kind, j = l % 4, l // 4
        m6 = [mods[l, :, None, t * d:(t + 1) * d] for t in range(6)]
        g_mix = norm_mix[l].reshape(1, d)
        if kind == 0:
            xs, (kc, vc) = _mixer_nat(xs, streams, m6, cache_nat_k, cache_nat_v, j, nat_w_qkv[j], g_mix,
                                      nat_q_norm[j], nat_k_norm[j], nat_rpb[j], nat_w_o[j])
            nat_k_l.append(kc)
            nat_v_l.append(vc)
        elif kind == 1:
            xs, st = _mixer_lru(xs, streams, m6, state_lru[:, j], lru_w_in[j], g_mix, lru_conv_w[j], lru_conv_b[j],
                                lru_w_a[j], lru_b_a[j], lru_w_i[j], lru_b_i[j], lru_lambda[j], lru_w_out[j])
            lru_l.append(st)
        elif kind == 2:
            xs, (ckv, kr) = _mixer_mla(xs, streams, m6, cache_mla_ckv[:, j], cache_mla_krope[:, j], mla_w_down[j],
                                       g_mix, mla_q_a_norm[j], mla_kv_a_norm[j], mla_w_uq[j], mla_w_ukv[j],
                                       mla_q_norm[j], mla_k_norm[j], mla_w_o[j])
            ckv_l.append(ckv)
            krope_l.append(kr)
        else:
            xs, (kc, vc) = _mixer_swa(xs, streams, m6, cache_swa_k, cache_swa_v, j, swa_w_qkv[j], g_mix,
                                      swa_q_norm[j], swa_k_norm[j], swa_sinks[j], swa_w_o[j])
            swa_k_l.append(kc)
            swa_v_l.append(vc)
        w_in, w_out = ffn_w_in[l].astype(BF16), ffn_w_out[l].astype(BF16)
        xs = [_conv_ffn(x, st, norm_ffn[l].reshape(1, d), m6[3], m6[4], m6[5], w_in, ffn_conv_w[l],
                        ffn_conv_b[l], w_out) for x, st in zip(xs, streams)]

    return (xs[0].reshape(bc, sc, d), xs[1].reshape(bl, n, d), jnp.stack(nat_k_l, axis=1),
            jnp.stack(nat_v_l, axis=1), jnp.stack(lru_l, axis=1), jnp.stack(ckv_l, axis=1),
            jnp.stack(krope_l, axis=1), jnp.stack(swa_k_l, axis=1), jnp.stack(swa_v_l, axis=1))
```

```python
import functools

import numpy as np
import jax
import jax.numpy as jnp
from jax import lax
from jax.experimental import pallas as pl
from jax.experimental.pallas import tpu as pltpu

F32 = jnp.float32
BF16 = jnp.bfloat16

GRID_W = 64
NA_WIN_ROWS = 8
NA_WIN_COLS = 16
NA_Q_ROWS = 8
NA_K_ROWS = 16
LRU_C = 8.0
SWA_WINDOW = 128
SWA_BLOCK = 128
ROPE_BASE = 10000.0
ROPE_GROUP = 32
EPS = 1e-6
NEG = -1e30
LOG2E = float(np.log2(np.e))
LANE = 128
SUBLANE = 8
HALO = 16
MIB = 1024 * 1024
ROW_TILES = (1024, 512, 256, 128, 64, 32, 16)


def _cparams(n_axes, vmem_mib):
    return pltpu.CompilerParams(dimension_semantics=("arbitrary",) * n_axes,
                                vmem_limit_bytes=int(min(vmem_mib, 60) * MIB))


def _largest_divisor(n, candidates):
    for c in candidates:
        if n % c == 0:
            return c
    return n


class _Stream:
    def __init__(self, nb, seq, mod0, shared_mod):
        self.nb, self.seq, self.rows, self.mod0, self.shared = nb, seq, nb * seq, mod0, shared_mod
        self.bm = _largest_divisor(self.rows if shared_mod else seq, ROW_TILES)

    def mod_index(self, row0):
        return self.mod0 if self.shared else self.mod0 + row0 // self.seq


def _norm_mod(x, g, shift, scale):
    ms = jnp.mean(x * x, axis=-1, keepdims=True)
    return (x * lax.rsqrt(ms + EPS)) * (g * (1.0 + scale)) + shift


def _rms(x, g):
    return (x * lax.rsqrt(jnp.mean(x * x, axis=-1, keepdims=True) + EPS)) * g


def _modulation_kernel(c_ref, w_ref, b_ref, o_ref):
    c = c_ref[...]
    sc = (c * jax.nn.sigmoid(c)).astype(BF16)
    o_ref[...] = jnp.dot(sc, w_ref[...].astype(BF16), preferred_element_type=F32) + b_ref[...]


def _modulation(cond, w_mod, b_mod):
    depth, d, n = w_mod.shape
    rows = cond.shape[0]
    bn = _largest_divisor(n, (512, 256, 128))
    return pl.pallas_call(
        _modulation_kernel,
        grid=(depth, n // bn),
        in_specs=[pl.BlockSpec((rows, d), lambda l, j: (0, 0)),
                  pl.BlockSpec((None, d, bn), lambda l, j: (l, 0, j)),
                  pl.BlockSpec((None, 1, bn), lambda l, j: (l, 0, j))],
        out_specs=pl.BlockSpec((None, rows, bn), lambda l, j: (l, 0, j)),
        out_shape=jax.ShapeDtypeStruct((depth, rows, n), F32),
        compiler_params=_cparams(2, 32),
        name="modulation",
    )(cond, w_mod, b_mod.reshape(depth, 1, n))


def _rope_tables(n_tokens, rot_dim, lead, width):
    t = jnp.arange(n_tokens)
    row = (t // GRID_W).astype(F32)
    col = (t % GRID_W).astype(F32)
    half = rot_dim // 2
    inv = ROPE_BASE ** (-jnp.arange(0, half, 2, dtype=F32) / half)
    ar = row[:, None] * inv
    ac = col[:, None] * inv
    ang = jnp.concatenate([ar, ar, ac, ac], axis=-1)
    cos, sin = jnp.cos(ang), jnp.sin(ang)
    first = (np.arange(rot_dim) % ROPE_GROUP) < ROPE_GROUP // 2
    sin_a = jnp.where(first, -sin, 0.0)
    sin_b = jnp.where(first, 0.0, sin)
    pad = ((0, 0), (lead, width - lead - rot_dim))
    return (jnp.pad(cos, pad, constant_values=1.0), jnp.pad(sin_a, pad), jnp.pad(sin_b, pad))


def _rope_apply(y, cos, sin_a, sin_b):
    shift = ROPE_GROUP // 2
    return y * cos + pltpu.roll(y, LANE - shift, 1) * sin_a + pltpu.roll(y, shift, 1) * sin_b


def _fill_lhs(x_ref, xs_ref, xn_ref, prologue, g_ref, sh_ref, sc_ref, row_chunk):
    bm = x_ref.shape[0]

    def chunk(r, carry):
        rows = pl.ds(pl.multiple_of(r * row_chunk, row_chunk), row_chunk)
        x = x_ref[rows, :].astype(F32)
        if prologue == "norm_mod":
            x = _norm_mod(x, g_ref[...], sh_ref[...], sc_ref[...])
        elif prologue == "norm":
            x = _rms(x, g_ref[...])
        if xn_ref is not None:
            xn_ref[rows, :] = x
        xs_ref[rows, :] = x.astype(BF16)
        return carry
    n_chunks = bm // row_chunk
    lax.fori_loop(0, n_chunks, chunk, 0, unroll=2 if n_chunks % 2 == 0 else 1)


def _head_norm_store(acc, o_ref, hg_ref, tabs, head_w, norm_div, col0, norm_cols, rope_tiles):
    bn = acc.shape[1]
    period = tabs[0].shape[1] if tabs is not None else LANE
    lane = lax.broadcasted_iota(jnp.int32, (1, LANE), 1)
    for s0 in range(0, bn, max(head_w, LANE)):
        normed = None if norm_cols is None else (col0 + s0 < norm_cols)
        tiles = [acc[:, s0 + k * LANE:s0 + (k + 1) * LANE] for k in range(max(head_w, LANE) // LANE)]
        if head_w >= LANE:
            ssq = None
            for y in tiles:
                part = jnp.sum(y * y, axis=-1, keepdims=True)
                ssq = part if ssq is None else ssq + part
            inv = lax.rsqrt(ssq / norm_div + EPS)
        else:
            y2 = tiles[0] * tiles[0]
            low = lane < head_w
            s_lo = jnp.sum(jnp.where(low, y2, 0.0), axis=-1, keepdims=True)
            s_hi = jnp.sum(jnp.where(low, 0.0, y2), axis=-1, keepdims=True)
            inv = jnp.where(low, lax.rsqrt(s_lo / norm_div + EPS), lax.rsqrt(s_hi / norm_div + EPS))
        if normed is not None:
            inv = jnp.where(normed, inv, 1.0)
        for k, y in enumerate(tiles):
            c0 = s0 + k * LANE
            y = (y * inv) * hg_ref[:, c0:c0 + LANE]
            t0 = c0 % period
            if tabs is not None and rope_tiles[t0 // LANE]:
                rotated = _rope_apply(y, *(t[:, t0:t0 + LANE] for t in tabs))
                y = rotated if normed is None else jnp.where(normed, rotated, y)
            o_ref[:, c0:c0 + LANE] = y.astype(o_ref.dtype)


def _proj_kernel(*refs, prologue, emit_xn, epilogue, head_w, norm_div, norm_cols, rope, rope_tiles, row_chunk):
    it = iter(refs)
    x_ref = next(it)
    g_ref = next(it) if prologue is not None else None
    sh_ref, sc_ref = (next(it), next(it)) if prologue == "norm_mod" else (None, None)
    w_ref = next(it)
    if epilogue == "res":
        res_ref, gate_ref = next(it), next(it)
    if epilogue == "heads":
        hg_ref = next(it)
        tabs = (next(it), next(it), next(it)) if rope else None
    o_ref = next(it)
    xn_ref = next(it) if emit_xn else None
    xs_ref = next(it)
    j = pl.program_id(1)
    bn = o_ref.shape[1]

    @pl.when(j == 0)
    def _():
        _fill_lhs(x_ref, xs_ref, xn_ref, prologue, g_ref, sh_ref, sc_ref, row_chunk)

    acc = jnp.dot(xs_ref[...], w_ref[...], preferred_element_type=F32)
    if epilogue == "res":
        o_ref[...] = res_ref[...] + gate_ref[...] * acc
    elif epilogue == "heads":
        _head_norm_store(acc, o_ref, hg_ref, tabs, head_w, norm_div, j * bn, norm_cols, rope_tiles)
    else:
        o_ref[...] = acc.astype(o_ref.dtype)


def _proj(x, w, st, *, x_block=None, norm_g=None, mod=None, res=None, gate=None, heads=None, emit_xn=False,
          out_dtype=F32, bn=None, x_time_major=False, out_time_major=False, name="proj"):
    k, n = w.shape
    time_major = x_time_major or out_time_major
    bm = min(st.bm, st.seq) if time_major else st.bm
    tiles_per_seq = st.seq // bm if st.seq % bm == 0 else None
    if time_major:
        assert tiles_per_seq is not None and x_block is None
    if x_time_major:
        assert x.shape == (st.seq, st.nb * k)
        kidx = 0
    else:
        kx, kidx = x_block if x_block is not None else (x.shape[1], 0)
        assert kx == k and x.shape[0] == st.rows
    rows = st.rows
    prologue = None if norm_g is None else ("norm_mod" if mod is not None else "norm")
    epilogue = "res" if res is not None else ("heads" if heads is not None else None)
    rope = heads is not None and heads.get("tabs") is not None
    if bn is None:
        unit = LANE
        if epilogue == "heads":
            unit = max(heads["head_w"], LANE, heads["tabs"][0].shape[1] if rope else LANE)
        cap = 512 if epilogue == "res" else 1024
        bn = next((c for c in range(cap, unit - 1, -unit) if n % c == 0), n)
    mod_idx = lambda i: st.mod_index(i * bm)

    if x_time_major:
        in_specs = [pl.BlockSpec((bm, k), lambda i, j: (i % tiles_per_seq, i // tiles_per_seq))]
    else:
        in_specs = [pl.BlockSpec((bm, k), lambda i, j: (i, kidx))]
    args = [x]
    if prologue is not None:
        in_specs.append(pl.BlockSpec((1, k), lambda i, j: (0, 0)))
        args.append(norm_g)
    if prologue == "norm_mod":
        in_specs += [pl.BlockSpec((None, 1, k), lambda i, j: (mod_idx(i), 0, 0))] * 2
        args += list(mod)
    in_specs.append(pl.BlockSpec((k, bn), lambda i, j: (0, j)))
    args.append(w.astype(BF16))
    if epilogue == "res":
        in_specs += [pl.BlockSpec((bm, bn), lambda i, j: (i, j)),
                     pl.BlockSpec((None, 1, bn), lambda i, j: (mod_idx(i), 0, j))]
        args += [res, gate]
    head_w = norm_div = 0
    norm_cols = rope_tiles = None
    if epilogue == "heads":
        head_w, norm_div = heads["head_w"], heads["norm_div"]
        norm_cols = heads["norm_cols"] if heads["norm_cols"] < n else None
        assert bn % max(head_w, LANE) == 0 and heads["norm_cols"] % max(head_w, LANE) == 0
        in_specs.append(pl.BlockSpec((1, bn), lambda i, j: (0, j)))
        args.append(heads["gains"])
        if rope:
            period = heads["tabs"][0].shape[1]
            rope_tiles = heads["rope_tiles"]
            assert bn % period == 0 and tiles_per_seq is not None and len(rope_tiles) == period // LANE
            in_specs += [pl.BlockSpec((bm, period), lambda i, j: (i % tiles_per_seq, 0))] * 3
            args += list(heads["tabs"])
    if out_time_major:
        n_col_tiles = n // bn
        out_shape = [jax.ShapeDtypeStruct((st.seq, st.nb * n), out_dtype)]
        out_specs = [pl.BlockSpec((bm, bn), lambda i, j: (i % tiles_per_seq, (i // tiles_per_seq) * n_col_tiles + j))]
    else:
        out_shape = [jax.ShapeDtypeStruct((rows, n), out_dtype)]
        out_specs = [pl.BlockSpec((bm, bn), lambda i, j: (i, j))]
    if emit_xn:
        out_shape.append(jax.ShapeDtypeStruct((rows, k), F32))
        out_specs.append(pl.BlockSpec((bm, k), lambda i, j: (i, 0)))
    xbytes = x.dtype.itemsize
    vmem = (2 * bm * k * xbytes + bm * k * 2 + 2 * k * bn * 2 + (6 if epilogue == "res" else 4) * bm * bn * 4
            + (2 * bm * k * 4 if emit_xn else 0)) / MIB + 8
    kern = functools.partial(_proj_kernel, prologue=prologue, emit_xn=emit_xn, epilogue=epilogue, head_w=head_w,
                             norm_div=norm_div, norm_cols=norm_cols, rope=rope, rope_tiles=rope_tiles,
                             row_chunk=min(bm, 128))
    out = pl.pallas_call(
        kern,
        grid=(rows // bm, n // bn),
        in_specs=in_specs,
        out_specs=out_specs,
        out_shape=out_shape,
        scratch_shapes=[pltpu.VMEM((bm, k), BF16)],
        compiler_params=_cparams(2, vmem),
        name=name,
    )(*args)
    return out if emit_xn else out[0]


def _ffn_kernel(xp_ref, x_ref, xn_ref, g_ref, sh_ref, sc_ref, gate_ref, wa_ref, wb_ref, cw_ref, cb_ref,
                wo_ref, o_ref, h_ref, *, bm, seq, row_chunk):
    i = pl.program_id(0)
    c = pl.program_id(1)
    n_chunks = pl.num_programs(1)

    @pl.when(c == 0)
    def _():
        g, sh, sc = g_ref[...], sh_ref[...], sc_ref[...]
        h_ref[0:HALO, :] = _norm_mod(xp_ref[...], g, sh, sc).astype(BF16)
        h_ref[HALO + bm:, :] = _norm_mod(xn_ref[...], g, sh, sc).astype(BF16)

        def chunk(r, carry):
            src = pl.ds(pl.multiple_of(r * row_chunk, row_chunk), row_chunk)
            dst = pl.ds(pl.multiple_of(HALO + r * row_chunk, HALO), row_chunk)
            h_ref[dst, :] = _norm_mod(x_ref[src, :], g, sh, sc).astype(BF16)
            return carry
        n_row_chunks = bm // row_chunk
        lax.fori_loop(0, n_row_chunks, chunk, 0, unroll=2 if n_row_chunks % 2 == 0 else 1)
        o_ref[...] = jnp.zeros_like(o_ref)

    ua = jnp.dot(h_ref[...], wa_ref[...], preferred_element_type=F32)
    ub = jnp.dot(h_ref[HALO:HALO + bm, :], wb_ref[...], preferred_element_type=F32)
    n_all = bm + 2 * HALO
    u_prev = pltpu.roll(ua, 1, 0)[HALO:HALO + bm]
    u_next = pltpu.roll(ua, n_all - 1, 0)[HALO:HALO + bm]
    u_mid = ua[HALO:HALO + bm]
    pos = jnp.bitwise_and(i * bm + lax.broadcasted_iota(jnp.int32, (bm, 1), 0), seq - 1)
    u_prev = jnp.where(pos == 0, 0.0, u_prev)
    u_next = jnp.where(pos == seq - 1, 0.0, u_next)
    cw = cw_ref[...]
    a = cb_ref[...] + u_prev * cw[0:1] + u_mid * cw[1:2] + u_next * cw[2:3]
    gated = ((a * jax.nn.sigmoid(a)) * ub).astype(BF16)
    o_ref[...] += jnp.dot(gated, wo_ref[...], preferred_element_type=F32)

    @pl.when(c == n_chunks - 1)
    def _():
        o_ref[...] = x_ref[...] + gate_ref[...] * o_ref[...]


def _conv_ffn(x, st, g, shift, scale, gate, w_in, conv_w, conv_b, w_out, bm=1024, ck=512):
    m, d = x.shape
    d_ff = w_out.shape[0]
    bm = min(st.bm, bm)
    ck = _largest_divisor(d_ff, tuple(c for c in (512, 256, 128) if c <= ck))
    n_chunks = d_ff // ck
    n_halo_blocks = m // HALO
    assert st.seq & (st.seq - 1) == 0 and conv_w.shape[0] == 3
    mod_idx = lambda i: st.mod_index(i * bm)
    kern = functools.partial(_ffn_kernel, bm=bm, seq=st.seq, row_chunk=min(bm, 128))
    vmem = (4 * bm * d * 4 + (bm + 2 * HALO) * d * 2 + 6 * d * ck * 2 + 5 * (bm + 2 * HALO) * ck * 4) / MIB + 4
    return pl.pallas_call(
        kern,
        grid=(m // bm, n_chunks),
        in_specs=[
            pl.BlockSpec((HALO, d), lambda i, c: (jnp.maximum(i * (bm // HALO) - 1, 0), 0)),
            pl.BlockSpec((bm, d), lambda i, c: (i, 0)),
            pl.BlockSpec((HALO, d), lambda i, c: (jnp.minimum((i + 1) * (bm // HALO), n_halo_blocks - 1), 0)),
            pl.BlockSpec((1, d), lambda i, c: (0, 0)),
            pl.BlockSpec((None, 1, d), lambda i, c: (mod_idx(i), 0, 0)),
            pl.BlockSpec((None, 1, d), lambda i, c: (mod_idx(i), 0, 0)),
            pl.BlockSpec((None, 1, d), lambda i, c: (mod_idx(i), 0, 0)),
            pl.BlockSpec((d, ck), lambda i, c: (0, c)),
            pl.BlockSpec((d, ck), lambda i, c: (0, n_chunks + c)),
            pl.BlockSpec((conv_w.shape[0], ck), lambda i, c: (0, c)),
            pl.BlockSpec((1, ck), lambda i, c: (0, c)),
            pl.BlockSpec((ck, d), lambda i, c: (c, 0)),
        ],
        out_specs=pl.BlockSpec((bm, d), lambda i, c: (i, 0)),
        out_shape=jax.ShapeDtypeStruct((m, d), F32),
        scratch_shapes=[pltpu.VMEM((bm + 2 * HALO, d), BF16)],
        compiler_params=_cparams(2, vmem),
        name="conv_ffn",
    )(x, x, x, g, shift, scale, gate, w_in, w_in, conv_w, conv_b.reshape(1, d_ff), w_out)


def _qk(q, k):
    return lax.dot_general(q, k, (((1,), (1,)), ((), ())), preferred_element_type=F32)


def _attend(scores, values, sink=None):
    m = None
    for s in scores:
        mi = jnp.max(s, axis=-1, keepdims=True)
        m = mi if m is None else jnp.maximum(m, mi)
    if sink is not None:
        m = jnp.maximum(m, sink)
    es = [jnp.exp2(s - m) for s in scores]
    den = None
    for e in es:
        di = jnp.sum(e, axis=-1, keepdims=True)
        den = di if den is None else den + di
    if sink is not None:
        den = den + jnp.exp2(sink - m)
    out = None
    for e, v in zip(es, values):
        oi = jnp.dot(e.astype(BF16), v, preferred_element_type=F32)
        out = oi if out is None else out + oi
    return out * (1.0 / den)


def _ctx_attn_kernel(q_ref, k_ref, v_ref, *outs, heads, dq, dv, emit_kv):
    o_ref = outs[0]
    for h in range(heads):
        q = q_ref[:, h * dq:(h + 1) * dq].astype(BF16)
        k = k_ref[:, h * dq:(h + 1) * dq]
        v = v_ref[:, h * dv:(h + 1) * dv]
        if emit_kv:
            outs[1][h] = k.astype(F32)
            outs[2][h] = v.astype(F32)
        o = _attend([_qk(q, k.astype(BF16))], [v.astype(BF16)])
        o_ref[:, h * dv:(h + 1) * dv] = o.astype(o_ref.dtype)


def _ctx_attention(qm, km, vm, st, *, n_heads, dq, dv, q_col, k_col, v_col, emit_kv=False):
    hb = _largest_divisor(n_heads, (4, 2, 1))
    s = st.seq
    assert q_col % (hb * dq) == 0 and k_col % (hb * dq) == 0 and v_col % (hb * dv) == 0
    qo, ko, vo = q_col // (hb * dq), k_col // (hb * dq), v_col // (hb * dv)
    out_shape = [jax.ShapeDtypeStruct((st.rows, n_heads * dv), BF16)]
    out_specs = [pl.BlockSpec((s, hb * dv), lambda b, g: (b, g))]
    if emit_kv:
        out_shape += [jax.ShapeDtypeStruct((st.nb, n_heads, s, dq), F32),
                      jax.ShapeDtypeStruct((st.nb, n_heads, s, dv), F32)]
        out_specs += [pl.BlockSpec((None, hb, s, dq), lambda b, g: (b, g, 0, 0)),
                      pl.BlockSpec((None, hb, s, dv), lambda b, g: (b, g, 0, 0))]
    out = pl.pallas_call(
        functools.partial(_ctx_attn_kernel, heads=hb, dq=dq, dv=dv, emit_kv=emit_kv),
        grid=(st.nb, n_heads // hb),
        in_specs=[pl.BlockSpec((s, hb * dq), lambda b, g: (b, qo + g)),
                  pl.BlockSpec((s, hb * dq), lambda b, g: (b, ko + g)),
                  pl.BlockSpec((s, hb * dv), lambda b, g: (b, vo + g))],
        out_specs=out_specs,
        out_shape=out_shape,
        compiler_params=_cparams(2, 32),
        name="ctx_attention",
    )(qm, km, vm)
    return out if emit_kv else out[0]


def _nat_kernel(q_ref, k_ref, v_ref, kc_ref, vc_ref, bias_ref, o_ref, *, key_rows, rows, heads, dh):
    i = pl.program_id(2)
    n_keys = key_rows * GRID_W
    first_row = jnp.clip(i * NA_Q_ROWS - NA_WIN_ROWS // 2, 0, rows - key_rows)
    start = pl.multiple_of(first_row * GRID_W, GRID_W * 4)
    for h in range(heads):
        lanes = slice(h * dh, (h + 1) * dh)
        q = q_ref[:, lanes]
        k = k_ref[pl.ds(start, n_keys), lanes]
        v = v_ref[pl.ds(start, n_keys), lanes]
        s_loc = _qk(q, k) + bias_ref[h]
        s_ctx = _qk(q, kc_ref[h].astype(BF16))
        o_ref[:, lanes] = _attend([s_loc, s_ctx], [v, vc_ref[h].astype(BF16)]).astype(o_ref.dtype)


def _nat_bias(rpb, rows):
    n_blocks = rows // NA_Q_ROWS
    key_rows = min(NA_K_ROWS, rows)
    wr = min(NA_WIN_ROWS, rows)
    reps = [0, min(1, n_blocks - 1), n_blocks - 1]
    heads = rpb.shape[0]
    nq, nk = NA_Q_ROWS * GRID_W, key_rows * GRID_W
    shape = (NA_Q_ROWS, GRID_W, key_rows, GRID_W)
    qc = np.arange(GRID_W)
    cstart = np.clip(qc - NA_WIN_COLS // 2, 0, GRID_W - NA_WIN_COLS)
    col_ok = (qc[None, :] >= cstart[:, None]) & (qc[None, :] < cstart[:, None] + NA_WIN_COLS)
    rp = jnp.pad(rpb.astype(F32) * LOG2E,
                 ((0, 0), (key_rows, key_rows), (GRID_W - NA_WIN_COLS, GRID_W - NA_WIN_COLS)))
    row_slabs, mask_l = [], []
    for i in reps:
        ks = int(np.clip(i * NA_Q_ROWS - NA_WIN_ROWS // 2, 0, rows - key_rows))
        r = i * NA_Q_ROWS + np.arange(NA_Q_ROWS)
        rs = np.clip(r - wr // 2, 0, rows - wr)
        kr = ks + np.arange(key_rows)
        row_ok = (kr[None, :] >= rs[:, None]) & (kr[None, :] < rs[:, None] + wr)
        for rq in range(NA_Q_ROWS):
            first = ks - int(r[rq]) + NA_WIN_ROWS - 1 + key_rows
            assert 0 <= first and first + key_rows <= rp.shape[1]
            row_slabs.append(rp[:, first:first + key_rows, :])
        mask_l.append(np.broadcast_to(row_ok[:, None, :, None] & col_ok[None, :, None, :], shape).reshape(nq, nk))
    slab = jnp.stack(row_slabs, axis=1).reshape(heads, len(reps), NA_Q_ROWS, key_rows, 2 * GRID_W - 1)
    toep = jnp.stack([slab[..., GRID_W - 1 - c:2 * GRID_W - 1 - c] for c in range(GRID_W)], axis=3)
    bias = toep.reshape(heads, len(reps), nq, nk)
    return jnp.where(jnp.asarray(np.stack(mask_l))[None], bias, NEG)


def _nat_attention(qkv, st, cache_k, cache_v, j, rpb, dh):
    heads = rpb.shape[0]
    n = st.seq
    p = cache_k.shape[3]
    rows = n // GRID_W
    assert rows % NA_Q_ROWS == 0 and rows >= NA_K_ROWS and dh % LANE == 0
    n_blocks = rows // NA_Q_ROWS
    key_rows = min(NA_K_ROWS, rows)
    nq, nk = NA_Q_ROWS * GRID_W, key_rows * GRID_W
    bias = _nat_bias(rpb, rows)
    btype = lambda i: jnp.where(i == 0, 0, jnp.where(i == n_blocks - 1, 2, 1))
    hb = _largest_divisor(heads, (4, 2, 1))
    hg = heads // hb
    kern = functools.partial(_nat_kernel, key_rows=key_rows, rows=rows, heads=hb, dh=dh)
    return pl.pallas_call(
        kern,
        grid=(st.nb, hg, n_blocks),
        in_specs=[pl.BlockSpec((nq, hb * dh), lambda b, h, i: (b * n_blocks + i, h)),
                  pl.BlockSpec((n, hb * dh), lambda b, h, i: (b, hg + h)),
                  pl.BlockSpec((n, hb * dh), lambda b, h, i: (b, 2 * hg + h)),
                  pl.BlockSpec((None, None, hb, p, dh), lambda b, h, i: (b, j, h, 0, 0)),
                  pl.BlockSpec((None, None, hb, p, dh), lambda b, h, i: (b, j, h, 0, 0)),
                  pl.BlockSpec((hb, None, nq, nk), lambda b, h, i: (h, btype(i), 0, 0))],
        out_specs=pl.BlockSpec((nq, hb * dh), lambda b, h, i: (b * n_blocks + i, h)),
        out_shape=jax.ShapeDtypeStruct((st.rows, heads * dh), BF16),
        compiler_params=_cparams(3, 56),
        name="nat_attention",
    )(qkv, qkv, qkv, cache_k, cache_v, bias)


def _joint_dense_kernel(q_ref, k_ref, v_ref, kc_ref, vc_ref, o_ref, *, chunk):
    q = q_ref[...]
    n = k_ref.shape[0]
    pieces = [(k_ref, v_ref, c0, min(chunk, n - c0)) for c0 in range(0, n, chunk)]
    pieces.append((kc_ref, vc_ref, 0, kc_ref.shape[0]))
    m = den = acc = None
    for kr, vr, c0, size in pieces:
        s = _qk(q, kr[c0:c0 + size, :])
        mc = jnp.max(s, axis=-1, keepdims=True)
        m_new = mc if m is None else jnp.maximum(m, mc)
        e = jnp.exp2(s - m_new)
        dc = jnp.sum(e, axis=-1, keepdims=True)
        pv = jnp.dot(e.astype(BF16), vr[c0:c0 + size, :], preferred_element_type=F32)
        if m is None:
            den, acc = dc, pv
        else:
            alpha = jnp.exp2(m - m_new)
            den, acc = alpha * den + dc, alpha * acc + pv
        m = m_new
    o_ref[...] = (acc * (1.0 / den)).astype(o_ref.dtype)


def _joint_dense_attention(qm, km, vm, kcm, vcm, st, p, *, n_heads, dq, dv):
    n = st.seq
    bq = _largest_divisor(n, (1024, 512, 256, 128, 64, 32, 16))
    nqb = n // bq
    return pl.pallas_call(
        functools.partial(_joint_dense_kernel, chunk=1024),
        grid=(st.nb, n_heads, nqb),
        in_specs=[pl.BlockSpec((bq, dq), lambda b, h, i: (b * nqb + i, h)),
                  pl.BlockSpec((n, dq), lambda b, h, i: (b, h)),
                  pl.BlockSpec((n, dv), lambda b, h, i: (b, h)),
                  pl.BlockSpec((p, dq), lambda b, h, i: (b, h)),
                  pl.BlockSpec((p, dv), lambda b, h, i: (b, h))],
        out_specs=pl.BlockSpec((bq, dv), lambda b, h, i: (b * nqb + i, h)),
        out_shape=jax.ShapeDtypeStruct((st.rows, n_heads * dv), BF16),
        compiler_params=_cparams(3, 48),
        name="mla_attention",
    )(qm, km, vm, kcm, vcm)


def _both_halves(x, s):
    x = x.astype(F32)
    low = lax.broadcasted_iota(jnp.int32, (1, LANE), 1) < LANE // 2
    keep = low if s == 0 else jnp.logical_not(low)
    return jnp.where(keep, x, pltpu.roll(x, LANE // 2, 1)).astype(BF16)


def _swa_step(sinks_ref, pair, q_ref, k, v, kc, vc, o_ref, *, dh, groups, local_bias):
    kv_per_step = LANE // dh
    assert kv_per_step == 2 and groups % 2 == 0
    rows = q_ref.shape[0]
    low = lax.broadcasted_iota(jnp.int32, (1, LANE), 1) < dh
    row_group = lax.broadcasted_iota(jnp.int32, (groups * rows, 1), 0) // rows
    for s in range(kv_per_step):
        kd, vd = _both_halves(k, s), _both_halves(v, s)
        q_parts = []
        for g in range(groups):
            c0 = ((s * groups + g) * dh // LANE) * LANE
            tile = q_ref[:, c0:c0 + LANE].astype(BF16)
            q_parts.append(jnp.where(low if g % 2 == 0 else jnp.logical_not(low), tile, jnp.zeros_like(tile)))
        q = jnp.concatenate(q_parts, axis=0)
        sink = jnp.zeros((groups * rows, 1), F32)
        for g in range(groups):
            sink = jnp.where(row_group == g, sinks_ref[(pair * kv_per_step + s) * groups + g], sink)
        s_loc = _qk(q, kd)
        if local_bias is not None:
            s_loc = s_loc + jnp.concatenate([local_bias] * groups, axis=0)
        if kc is not None:
            kcd = jnp.concatenate([kc[s], kc[s]], axis=-1).astype(BF16)
            vcd = jnp.concatenate([vc[s], vc[s]], axis=-1).astype(BF16)
            out = _attend([s_loc, _qk(q, kcd)], [vd, vcd], sink)
        else:
            out = _attend([s_loc], [vd], sink)
        for g in range(0, groups, 2):
            c0 = (s * groups + g) * dh
            o_ref[:, c0:c0 + LANE] = jnp.where(low, out[g * rows:(g + 1) * rows],
                                               out[(g + 1) * rows:(g + 2) * rows]).astype(o_ref.dtype)


def _swa_ctx_kernel(sinks_ref, q_ref, k_ref, v_ref, o_ref, ko_ref, vo_ref, *, dh, groups):
    pair = pl.program_id(1)
    k, v = k_ref[...], v_ref[...]
    for s in range(LANE // dh):
        ko_ref[s] = k[:, s * dh:(s + 1) * dh].astype(F32)
        vo_ref[s] = v[:, s * dh:(s + 1) * dh].astype(F32)
    _swa_step(sinks_ref, pair, q_ref, k, v, None, None, o_ref, dh=dh, groups=groups, local_bias=None)


def _swa_lat_kernel(sinks_ref, q_ref, k_ref, v_ref, kc_ref, vc_ref, o_ref, *, dh, groups, n):
    pair = pl.program_id(1)
    blk = pl.program_id(2)
    n_keys = min(3 * SWA_BLOCK, n)
    start = pl.multiple_of(jnp.clip((blk - 1) * SWA_BLOCK, 0, n - n_keys), SWA_BLOCK)
    k = k_ref[pl.ds(start, n_keys), :]
    v = v_ref[pl.ds(start, n_keys), :]
    qpos = blk * SWA_BLOCK + lax.broadcasted_iota(jnp.int32, (SWA_BLOCK, 1), 0)
    kpos = start + lax.broadcasted_iota(jnp.int32, (1, n_keys), 1)
    bias = jnp.where(jnp.abs(qpos - kpos) <= SWA_WINDOW, 0.0, NEG)
    _swa_step(sinks_ref, pair, q_ref, k, v, kc_ref, vc_ref, o_ref, dh=dh, groups=groups, local_bias=bias)


def _swa_attention(qkv, st, sinks, *, heads, kvh, dh, cache=None):
    groups = heads // kvh
    kv_per_step = LANE // dh
    assert LANE % dh == 0 and kvh % kv_per_step == 0 and groups % kv_per_step == 0
    pairs = kvh // kv_per_step
    qw = kv_per_step * groups * dh
    k_blk = heads * dh // LANE
    v_blk = (heads + kvh) * dh // LANE
    n = st.seq
    common = dict(dh=dh, groups=groups)
    smem = pl.BlockSpec(memory_space=pltpu.SMEM)
    if cache is None:
        out = pl.pallas_call(
            functools.partial(_swa_ctx_kernel, **common),
            grid=(st.nb, pairs),
            in_specs=[smem,
                      pl.BlockSpec((n, qw), lambda b, c: (b, c)),
                      pl.BlockSpec((n, LANE), lambda b, c: (b, k_blk + c)),
                      pl.BlockSpec((n, LANE), lambda b, c: (b, v_blk + c))],
            out_specs=[pl.BlockSpec((n, qw), lambda b, c: (b, c)),
                       pl.BlockSpec((None, kv_per_step, n, dh), lambda b, c: (b, c, 0, 0)),
                       pl.BlockSpec((None, kv_per_step, n, dh), lambda b, c: (b, c, 0, 0))],
            out_shape=[jax.ShapeDtypeStruct((st.rows, heads * dh), BF16),
                       jax.ShapeDtypeStruct((st.nb, kvh, n, dh), F32),
                       jax.ShapeDtypeStruct((st.nb, kvh, n, dh), F32)],
            compiler_params=_cparams(2, 32),
            name="swa_ctx_attention",
        )(sinks, qkv, qkv, qkv)
        return out
    cache_k, cache_v, j = cache
    p = cache_k.shape[3]
    nblk = n // SWA_BLOCK
    assert n % SWA_BLOCK == 0
    return pl.pallas_call(
        functools.partial(_swa_lat_kernel, n=n, **common),
        grid=(st.nb, pairs, nblk),
        in_specs=[smem,
                  pl.BlockSpec((SWA_BLOCK, qw), lambda b, c, i: (b * nblk + i, c)),
                  pl.BlockSpec((n, LANE), lambda b, c, i: (b, k_blk + c)),
                  pl.BlockSpec((n, LANE), lambda b, c, i: (b, v_blk + c)),
                  pl.BlockSpec((None, None, kv_per_step, p, dh), lambda b, c, i: (b, j, c, 0, 0)),
                  pl.BlockSpec((None, None, kv_per_step, p, dh), lambda b, c, i: (b, j, c, 0, 0))],
        out_specs=pl.BlockSpec((SWA_BLOCK, qw), lambda b, c, i: (b * nblk + i, c)),
        out_shape=jax.ShapeDtypeStruct((st.rows, heads * dh), BF16),
        compiler_params=_cparams(3, 32),
        name="swa_attention",
    )(sinks, qkv, qkv, qkv, cache_k, cache_v)


def _mla_kv_kernel(*refs, norm, emit_xn, rope, row_chunk, norm_div):
    it = iter(refs)
    x_ref = next(it)
    g_ref = next(it) if norm else None
    w_ref, kr_ref, g1_ref, g2_ref = next(it), next(it), next(it), next(it)
    tabs = (next(it), next(it), next(it)) if rope else None
    k_ref, v_ref = next(it), next(it)
    xn_ref = next(it) if emit_xn else None
    xs_ref = next(it)

    @pl.when(pl.program_id(1) == 0)
    def _():
        _fill_lhs(x_ref, xs_ref, xn_ref, "norm" if norm else None, g_ref, None, None, row_chunk)

    acc = jnp.dot(xs_ref[...], w_ref[...], preferred_element_type=F32)
    kr = kr_ref[...]
    kr_ssq = jnp.sum(kr * kr, axis=-1, keepdims=True)
    shared = kr * g2_ref[...]
    if rope:
        shared = _rope_apply(shared, *(t[...] for t in tabs))
    for h in range(acc.shape[1] // (2 * LANE)):
        nope = acc[:, 2 * h * LANE:(2 * h + 1) * LANE]
        inv = lax.rsqrt((jnp.sum(nope * nope, axis=-1, keepdims=True) + kr_ssq) / norm_div + EPS)
        k_ref[:, 2 * h * LANE:(2 * h + 1) * LANE] = ((nope * inv) * g1_ref[...]).astype(k_ref.dtype)
        k_ref[:, (2 * h + 1) * LANE:(2 * h + 2) * LANE] = (shared * inv).astype(k_ref.dtype)
        v_ref[:, h * LANE:(h + 1) * LANE] = acc[:, (2 * h + 1) * LANE:(2 * h + 2) * LANE].astype(v_ref.dtype)


def _mla_kv(x, x_block, w_ukv, kr, kr_block, g_kva, g1, g2, tabs, st, *, n_heads, norm_div, emit_xn, name):
    rows = x.shape[0]
    k, n = w_ukv.shape
    head_n = n // n_heads
    assert head_n == 2 * LANE, "nope and value widths must both be one lane tile"
    kx, kidx = x_block
    krw, kridx = kr_block
    assert kx == k and krw == LANE
    bm = st.bm
    norm = g_kva is not None
    rope = tabs is not None
    in_specs = [pl.BlockSpec((bm, k), lambda i, h: (i, kidx))]
    args = [x]
    if norm:
        in_specs.append(pl.BlockSpec((1, k), lambda i, h: (0, 0)))
        args.append(g_kva)
    hb = _largest_divisor(n_heads, (4, 2, 1))
    in_specs += [pl.BlockSpec((k, hb * head_n), lambda i, h: (0, h)),
                 pl.BlockSpec((bm, LANE), lambda i, h: (i, kridx)),
                 pl.BlockSpec((1, LANE), lambda i, h: (0, 0)),
                 pl.BlockSpec((1, LANE), lambda i, h: (0, 0))]
    args += [w_ukv, kr, g1, g2]
    if rope:
        tiles_per_seq = st.seq // bm
        in_specs += [pl.BlockSpec((bm, LANE), lambda i, h: (i % tiles_per_seq, 0))] * 3
        args += list(tabs)
    out_shape = [jax.ShapeDtypeStruct((rows, n_heads * 2 * LANE), BF16),
                 jax.ShapeDtypeStruct((rows, n_heads * LANE), BF16)]
    out_specs = [pl.BlockSpec((bm, hb * 2 * LANE), lambda i, h: (i, h)),
                 pl.BlockSpec((bm, hb * LANE), lambda i, h: (i, h))]
    if emit_xn:
        out_shape.append(jax.ShapeDtypeStruct((rows, k), F32))
        out_specs.append(pl.BlockSpec((bm, k), lambda i, h: (i, 0)))
    kern = functools.partial(_mla_kv_kernel, norm=norm, emit_xn=emit_xn, rope=rope, row_chunk=min(bm, 128),
                             norm_div=norm_div)
    return pl.pallas_call(
        kern,
        grid=(rows // bm, n_heads // hb),
        in_specs=in_specs,
        out_specs=out_specs,
        out_shape=out_shape,
        scratch_shapes=[pltpu.VMEM((bm, k), BF16)],
        compiler_params=_cparams(2, 40),
        name=name,
    )(*args)


def _band_plan(width, block):
    n_tiles = width // LANE
    lo = [((t * LANE) // block) * block for t in range(n_tiles)]
    hi = [(((t + 1) * LANE - 1) // block + 1) * block for t in range(n_tiles)]
    start = [(l // LANE) * LANE for l in lo]
    kb = max(-(-(h - s) // LANE) * LANE for h, s in zip(hi, start))
    kb = min(kb, width)
    start = [min(s, width - kb) for s in start]
    return start, kb


def _band_weights(w, width, block, start, kb):
    n_tiles = width // LANE
    wb = w.astype(BF16)
    tiles = []
    for t in range(n_tiles):
        pieces = []
        col = t * LANE
        while col < (t + 1) * LANE:
            blk = col // block
            col_end = min((blk + 1) * block, (t + 1) * LANE)
            sub = wb[blk, :, col - blk * block:col_end - blk * block]
            top = blk * block - start[t]
            pieces.append(jnp.pad(sub, ((top, kb - top - block), (0, 0))))
            col = col_end
        tiles.append(jnp.concatenate(pieces, axis=1))
    return jnp.stack(tiles)


def _gelu_tanh(x):
    cdf = 0.5 * (1.0 + jnp.tanh(np.float32(np.sqrt(2.0 / np.pi)) * (x + 0.044715 * (x * x * x))))
    return x * cdf


def _sigmoid(x):
    return 0.5 * (1.0 + jnp.tanh(0.5 * x))


def _lru_pass_kernel(*refs, reverse, starts, kb, bt, seq, nb, taps):
    left = taps // 2
    right = taps - 1 - left
    it = iter(refs)
    xp_ref, x_ref = next(it), next(it)
    xn_ref = next(it) if right > 0 else None
    cw_ref, cb_ref = next(it), next(it)
    wa_ref, wi_ref, ba_ref, bi_ref, lam_ref, h0_ref = (next(it) for _ in range(6))
    hsf_ref, gate_ref = (next(it), next(it)) if reverse else (None, None)
    out_ref, ht_ref = next(it), next(it)
    xc_s, xb_s, a_s, bx_s, carry = next(it), next(it), next(it), next(it), next(it)

    step = pl.program_id(0)
    n_steps = pl.num_programs(0)
    tile = (n_steps - 1 - step) if reverse else step
    n_tiles = len(starts)
    rows = nb * bt

    @pl.when(step == 0)
    def _():
        carry[...] = h0_ref[...]

    for t in range(n_tiles):
        lanes = slice(t * LANE, (t + 1) * LANE)
        parts = [jnp.where(tile > 0, xp_ref[:, :, lanes], 0.0), x_ref[:, :, lanes]]
        if right > 0:
            parts.append(jnp.where(tile < n_steps - 1, xn_ref[:, :, lanes], 0.0))
        full = jnp.concatenate(parts, axis=0)
        acc = jnp.broadcast_to(cb_ref[:, lanes], (bt, nb, LANE))
        for k in range(taps):
            acc = acc + full[k:k + bt] * cw_ref[k:k + 1, lanes]
        acc = acc.reshape(rows, LANE)
        xc_s[:, lanes] = acc
        xb_s[:, lanes] = acc.astype(BF16)

    neg_lam = -lam_ref[...]
    softplus = jnp.maximum(neg_lam, 0.0) + jnp.log1p(jnp.exp(-jnp.abs(neg_lam)))
    for t in range(n_tiles):
        lanes = slice(t * LANE, (t + 1) * LANE)
        xw = xb_s[:, starts[t]:starts[t] + kb]
        r = _sigmoid(jnp.dot(xw, wa_ref[t], preferred_element_type=F32) + ba_ref[:, lanes])
        ig = _sigmoid(jnp.dot(xw, wi_ref[t], preferred_element_type=F32) + bi_ref[:, lanes])
        log_a = -LRU_C * r * softplus[:, lanes]
        a = jnp.exp(log_a)
        a_s[:, lanes] = a
        bx_s[:, lanes] = jnp.sqrt(-jnp.tanh(log_a) * (a * a + 1.0)) * (ig * xc_s[:, lanes])

    h = carry[...]
    for s in range(bt):
        ts = (bt - 1 - s) if reverse else s
        slab = slice(ts * nb, (ts + 1) * nb)
        h = a_s[slab, :] * h + bx_s[slab, :]
        a_s[slab, :] = h
    carry[...] = h
    ht_ref[...] = h
    hs = a_s[...].reshape(bt, nb, a_s.shape[1])
    if reverse:
        out_ref[...] = (_gelu_tanh(gate_ref[...]) * (hsf_ref[...] + hs)).astype(out_ref.dtype)
    else:
        out_ref[...] = hs


def _lru_pass(u, st, conv_w, conv_b, wa, wi, b_a, b_i, lam, h0, starts, kb, *, reverse, hs_fwd=None):
    c = conv_w.shape[1]
    taps = conv_w.shape[0]
    left, right = taps // 2, taps - 1 - taps // 2
    nb, seq = st.nb, st.seq
    bt = min(max(256 // nb, SUBLANE), seq)
    assert seq % bt == 0 and c % LANE == 0 and left > 0 and bt % left == 0 and (right == 0 or bt % right == 0)
    nt = seq // bt
    n_tiles = c // LANE
    u3 = u.reshape(seq, nb, 2 * c)
    tmap = (lambda s: nt - 1 - s) if reverse else (lambda s: s)
    full = lambda *shape: pl.BlockSpec(shape, lambda s: (0,) * len(shape))
    in_specs = [pl.BlockSpec((left, nb, c), lambda s: (jnp.maximum(tmap(s) * (bt // left) - 1, 0), 0, 0)),
                pl.BlockSpec((bt, nb, c), lambda s: (tmap(s), 0, 0))]
    args = [u3, u3]
    if right > 0:
        in_specs.append(pl.BlockSpec((right, nb, c),
                                     lambda s: (jnp.minimum((tmap(s) + 1) * (bt // right), seq // right - 1), 0, 0)))
        args.append(u3)
    in_specs += [full(taps, c), full(1, c), full(n_tiles, kb, LANE), full(n_tiles, kb, LANE), full(1, c), full(1, c),
                 full(1, c), full(nb, c)]
    args += [conv_w, conv_b.reshape(1, c), wa, wi, b_a.reshape(1, c), b_i.reshape(1, c), lam.reshape(1, c), h0]
    if reverse:
        in_specs += [pl.BlockSpec((bt, nb, c), lambda s: (tmap(s), 0, 0)),
                     pl.BlockSpec((bt, nb, c), lambda s: (tmap(s), 0, 1))]
        args += [hs_fwd, u3]
    kern = functools.partial(_lru_pass_kernel, reverse=reverse, starts=tuple(starts), kb=kb, bt=bt, seq=seq, nb=nb,
                             taps=taps)
    blk = nb * bt * c * 4 / MIB
    vmem = (2 + 2 + 3 + (4 if reverse else 0) + 4) * blk + 4 * n_tiles * kb * LANE * 2 / MIB + 8
    return pl.pallas_call(
        kern,
        grid=(nt,),
        in_specs=in_specs,
        out_specs=[pl.BlockSpec((bt, nb, c), lambda s: (tmap(s), 0, 0)),
                   pl.BlockSpec((nb, c), lambda s: (0, 0))],
        out_shape=[jax.ShapeDtypeStruct((seq, nb, c), F32),
                   jax.ShapeDtypeStruct((nb, c), F32)],
        scratch_shapes=[pltpu.VMEM((nb * bt, c), F32), pltpu.VMEM((nb * bt, c), BF16),
                        pltpu.VMEM((nb * bt, c), F32), pltpu.VMEM((nb * bt, c), F32), pltpu.VMEM((nb, c), F32)],
        compiler_params=_cparams(1, vmem),
        name="lru_bwd" if reverse else "lru_fwd",
    )(*args)


def _mixer_nat(xs, streams, mods, cache_k, cache_v, j, w_qkv, g_mix, g_q, g_k, rpb, w_o):
    heads, dh = rpb.shape[0], g_q.shape[0]
    w_qkv, w_o = w_qkv.astype(BF16), w_o.astype(BF16)
    gains = jnp.concatenate([jnp.tile(g_q * (dh ** -0.5 * LOG2E), heads), jnp.tile(g_k, heads),
                             jnp.ones((heads * dh,), F32)])[None]
    spec = dict(head_w=dh, norm_div=dh, norm_cols=2 * heads * dh, gains=gains)
    new_x, extra = [], None
    for x, st in zip(xs, streams):
        latent = not st.shared
        qkv = _proj(x, w_qkv, st, norm_g=g_mix, mod=(mods[0], mods[1]), heads=spec,
                    out_dtype=BF16 if latent else F32, name="nat_qkv")
        if latent:
            o = _nat_attention(qkv, st, cache_k, cache_v, j, rpb, dh)
        else:
            o, kc, vc = _ctx_attention(qkv, qkv, qkv, st, n_heads=heads, dq=dh, dv=dh, q_col=0, k_col=heads * dh,
                                       v_col=2 * heads * dh, emit_kv=True)
            extra = (kc, vc)
        new_x.append(_proj(o, w_o, st, res=x, gate=mods[2], name="nat_out"))
    return new_x, extra


def _mixer_lru(xs, streams, mods, state, w_in, g_mix, conv_w, conv_b, w_a, b_a, w_i, b_i, lam, w_out):
    c = conv_w.shape[1]
    block = w_a.shape[-1]
    w_in, w_out = w_in.astype(BF16), w_out.astype(BF16)
    starts, kb = _band_plan(c, block)
    wa = [_band_weights(w_a[d], c, block, starts, kb) for d in range(2)]
    wi = [_band_weights(w_i[d], c, block, starts, kb) for d in range(2)]
    new_x, st_out = [], None
    for x, st in zip(xs, streams):
        latent = not st.shared
        h0 = state.astype(F32) if latent else jnp.zeros((st.nb, 2, c), F32)
        u = _proj(x, w_in, st, norm_g=g_mix, mod=(mods[0], mods[1]), out_time_major=True, name="lru_in")
        hs_f, t_f = _lru_pass(u, st, conv_w, conv_b, wa[0], wi[0], b_a[0], b_i[0], lam[0], h0[:, 0], starts, kb,
                              reverse=False)
        y, t_b = _lru_pass(u, st, conv_w, conv_b, wa[1], wi[1], b_a[1], b_i[1], lam[1], h0[:, 1], starts, kb,
                           reverse=True, hs_fwd=hs_f)
        if not latent:
            st_out = jnp.stack([t_f, t_b], axis=1)
        new_x.append(_proj(y.reshape(st.seq, st.nb * c), w_out, st, res=x, gate=mods[2], x_time_major=True,
                           name="lru_out"))
    return new_x, st_out


def _mixer_mla(xs, streams, mods, cache_ckv, cache_kr, w_down, g_mix, g_qa, g_kva, w_uq, w_ukv, g_q, g_k, w_o):
    d_model = w_down.shape[0]
    q_rank, kv_rank = g_qa.shape[0], g_kva.shape[0]
    qk_dim = g_q.shape[0]
    heads = w_uq.shape[1] // qk_dim
    rope = w_down.shape[1] - q_rank - kv_rank
    nope = qk_dim - rope
    assert nope == LANE and rope <= LANE and kv_rank % LANE == 0 and q_rank % LANE == 0
    head_w = 2 * LANE
    q_pad = -q_rank % kv_rank
    kv_col = q_rank + q_pad
    tail_pad = -(kv_col + kv_rank + rope) % 512
    w_dn = jnp.concatenate([w_down[:, :q_rank], jnp.zeros((d_model, q_pad), F32),
                            w_down[:, q_rank:q_rank + kv_rank], w_down[:, q_rank + kv_rank:],
                            jnp.zeros((d_model, tail_pad), F32)], axis=1).astype(BF16)
    kr_blk = (kv_col + kv_rank) // LANE
    w_q = jnp.pad(w_uq.reshape(q_rank, heads, qk_dim), ((0, 0), (0, 0), (0, head_w - qk_dim)))
    w_q = w_q.reshape(q_rank, heads * head_w).astype(BF16)
    w_ukv, w_o = w_ukv.astype(BF16), w_o.astype(BF16)
    gq = jnp.tile(jnp.pad(g_q * (qk_dim ** -0.5 * LOG2E), (0, head_w - qk_dim)), heads)[None]
    g1, g2 = g_k[None, :nope], jnp.pad(g_k[nope:], (0, LANE - rope))[None]
    p = cache_ckv.shape[1]
    new_x, extra = [], None
    for x, st in zip(xs, streams):
        latent = not st.shared
        d = _proj(x, w_dn, st, norm_g=g_mix, mod=(mods[0], mods[1]), name="mla_down")
        q_tabs = _rope_tables(st.seq, rope, nope, head_w) if latent else None
        k_tabs = _rope_tables(st.seq, rope, 0, LANE) if latent else None
        q = _proj(d, w_q, st, x_block=(q_rank, 0), norm_g=g_qa[None],
                  heads=dict(head_w=head_w, norm_div=qk_dim, norm_cols=heads * head_w, gains=gq, tabs=q_tabs,
                             rope_tiles=(False, True)),
                  out_dtype=BF16, name="mla_uq")
        kv = _mla_kv(d, (kv_rank, kv_col // kv_rank), w_ukv, d, (LANE, kr_blk), g_kva[None], g1, g2, k_tabs, st,
                     n_heads=heads, norm_div=qk_dim, emit_xn=not latent, name="mla_ukv")
        if latent:
            k, v = kv
            cst = _Stream(st.nb, p, 0, True)
            krc = jnp.pad(cache_kr.reshape(st.nb * p, rope), ((0, 0), (0, LANE - rope)))
            kc, vc = _mla_kv(cache_ckv.reshape(st.nb * p, kv_rank), (kv_rank, 0), w_ukv, krc, (LANE, 0), None,
                             g1, g2, None, cst, n_heads=heads, norm_div=qk_dim, emit_xn=False,
                             name="mla_ukv_cache")
            o = _joint_dense_attention(q, k, v, kc, vc, st, p, n_heads=heads, dq=head_w, dv=LANE)
        else:
            k, v, ckv = kv
            o = _ctx_attention(q, k, v, st, n_heads=heads, dq=head_w, dv=LANE, q_col=0, k_col=0, v_col=0)
            kr_out = d[:, kv_col + kv_rank:kv_col + kv_rank + rope]
            extra = (ckv.reshape(st.nb, st.seq, kv_rank), kr_out.reshape(st.nb, st.seq, rope))
        new_x.append(_proj(o, w_o, st, res=x, gate=mods[2], name="mla_out"))
    return new_x, extra


def _mixer_swa(xs, streams, mods, cache_k, cache_v, j, w_qkv, g_mix, g_q, g_k, sinks, w_o):
    dh = g_q.shape[0]
    heads = sinks.shape[0]
    kvh = (w_qkv.shape[1] // dh - heads) // 2
    w_qkv, w_o = w_qkv.astype(BF16), w_o.astype(BF16)
    gains = jnp.concatenate([jnp.tile(g_q * (dh ** -0.5 * LOG2E), heads), jnp.tile(g_k, kvh),
                             jnp.ones((kvh * dh,), F32)])[None]
    sinks = sinks.astype(F32) * LOG2E
    new_x, extra = [], None
    for x, st in zip(xs, streams):
        latent = not st.shared
        tabs = _rope_tables(st.seq, dh, 0, dh) if latent else None
        if tabs is not None:
            tabs = tuple(jnp.tile(t, (1, LANE // dh)) for t in tabs)
        spec = dict(head_w=dh, norm_div=dh, norm_cols=(heads + kvh) * dh, gains=gains, tabs=tabs,
                    rope_tiles=(True,))
        qkv = _proj(x, w_qkv, st, norm_g=g_mix, mod=(mods[0], mods[1]), heads=spec,
                    out_dtype=BF16 if latent else F32, name="swa_qkv")
        if latent:
            o = _swa_attention(qkv, st, sinks, heads=heads, kvh=kvh, dh=dh, cache=(cache_k, cache_v, j))
        else:
            o, kc, vc = _swa_attention(qkv, st, sinks, heads=heads, kvh=kvh, dh=dh)
            extra = (kc, vc)
        new_x.append(_proj(o, w_o, st, res=x, gate=mods[2], name="swa_out"))
    return new_x, extra


def kernel(x_prompt, x_sample, cache_nat_k, cache_nat_v, state_lru, cache_mla_ckv, cache_mla_krope, cache_swa_k, cache_swa_v, c, c_ctx, norm_mix, norm_ffn, w_mod, b_mod, ffn_w_in, ffn_conv_w, ffn_conv_b, ffn_w_out, nat_w_qkv, nat_q_norm, nat_k_norm, nat_rpb, nat_w_o, lru_w_in, lru_conv_w, lru_conv_b, lru_w_a, lru_b_a, lru_w_i, lru_b_i, lru_lambda, lru_w_out, mla_w_down, mla_q_a_norm, mla_kv_a_norm, mla_w_uq, mla_w_ukv, mla_q_norm, mla_k_norm, mla_w_o, swa_w_qkv, swa_q_norm, swa_k_norm, swa_sinks, swa_w_o):
    bc, sc, d = x_prompt.shape
    bl, n, _ = x_sample.shape
    depth = w_mod.shape[0]
    streams = (_Stream(bc, sc, 0, True), _Stream(bl, n, 1, False))
    xs = [x_prompt.reshape(bc * sc, d), x_sample.reshape(bl * n, d)]

    n_cond = 1 + bl
    cond_rows = -(-n_cond // SUBLANE) * SUBLANE
    cond = jnp.zeros((cond_rows, d), F32).at[0].set(c_ctx).at[1:n_cond].set(c)
    mods = _modulation(cond, w_mod, b_mod)[:, :n_cond]

    nat_k_l, nat_v_l, lru_l, ckv_l, krope_l, swa_k_l, swa_v_l = [], [], [], [], [], [], []
    for l in range(depth):
        kind, j = l % 4, l // 4
        m6 = [mods[l, :, None, t * d:(t + 1) * d] for t in range(6)]
        g_mix = norm_mix[l].reshape(1, d)
        if kind == 0:
            xs, (kc, vc) = _mixer_nat(xs, streams, m6, cache_nat_k, cache_nat_v, j, nat_w_qkv[j], g_mix,
                                      nat_q_norm[j], nat_k_norm[j], nat_rpb[j], nat_w_o[j])
            nat_k_l.append(kc)
            nat_v_l.append(vc)
        elif kind == 1:
            xs, st = _mixer_lru(xs, streams, m6, state_lru[:, j], lru_w_in[j], g_mix, lru_conv_w[j], lru_conv_b[j],
                                lru_w_a[j], lru_b_a[j], lru_w_i[j], lru_b_i[j], lru_lambda[j], lru_w_out[j])
            lru_l.append(st)
        elif kind == 2:
            xs, (ckv, kr) = _mixer_mla(xs, streams, m6, cache_mla_ckv[:, j], cache_mla_krope[:, j], mla_w_down[j],
                                       g_mix, mla_q_a_norm[j], mla_kv_a_norm[j], mla_w_uq[j], mla_w_ukv[j],
                                       mla_q_norm[j], mla_k_norm[j], mla_w_o[j])
            ckv_l.append(ckv)
            krope_l.append(kr)
        else:
            xs, (kc, vc) = _mixer_swa(xs, streams, m6, cache_swa_k, cache_swa_v, j, swa_w_qkv[j], g_mix,
                                      swa_q_norm[j], swa_k_norm[j], swa_sinks[j], swa_w_o[j])
            swa_k_l.append(kc)
            swa_v_l.append(vc)
        w_in, w_out = ffn_w_in[l].astype(BF16), ffn_w_out[l].astype(BF16)
        xs = [_conv_ffn(x, st, norm_ffn[l].reshape(1, d), m6[3], m6[4], m6[5], w_in, ffn_conv_w[l],
                        ffn_conv_b[l], w_out) for x, st in zip(xs, streams)]

    return (xs[0].reshape(bc, sc, d), xs[1].reshape(bl, n, d), jnp.stack(nat_k_l, axis=1),
            jnp.stack(nat_v_l, axis=1), jnp.stack(lru_l, axis=1), jnp.stack(ckv_l, axis=1),
            jnp.stack(krope_l, axis=1), jnp.stack(swa_k_l, axis=1), jnp.stack(swa_v_l, axis=1))
```

```python
import functools

import numpy as np
import jax
import jax.numpy as jnp
from jax import lax
from jax.experimental import pallas as pl
from jax.experimental.pallas import tpu as pltpu

F32 = jnp.float32
BF16 = jnp.bfloat16

GRID_W = 64
NA_WIN_ROWS = 8
NA_WIN_COLS = 16
NA_Q_ROWS = 8
NA_K_ROWS = 16
LRU_C = 8.0
SWA_WINDOW = 128
SWA_BLOCK = 128
ROPE_BASE = 10000.0
ROPE_GROUP = 32
EPS = 1e-6
NEG = -1e30
LOG2E = float(np.log2(np.e))
LANE = 128
SUBLANE = 8
HALO = 16
MIB = 1024 * 1024
ROW_TILES = (1024, 512, 256, 128, 64, 32, 16)


def _cparams(n_axes, vmem_mib):
    return pltpu.CompilerParams(dimension_semantics=("arbitrary",) * n_axes,
                                vmem_limit_bytes=int(min(vmem_mib, 60) * MIB))


def _largest_divisor(n, candidates):
    for c in candidates:
        if n % c == 0:
            return c
    return n


class _Stream:
    def __init__(self, nb, seq, mod0, shared_mod):
        self.nb, self.seq, self.rows, self.mod0, self.shared = nb, seq, nb * seq, mod0, shared_mod
        self.bm = _largest_divisor(self.rows if shared_mod else seq, ROW_TILES)

    def mod_index(self, row0):
        return self.mod0 if self.shared else self.mod0 + row0 // self.seq


def _norm_mod(x, g, shift, scale):
    ms = jnp.mean(x * x, axis=-1, keepdims=True)
    return (x * lax.rsqrt(ms + EPS)) * (g * (1.0 + scale)) + shift


def _rms(x, g):
    return (x * lax.rsqrt(jnp.mean(x * x, axis=-1, keepdims=True) + EPS)) * g


def _modulation_kernel(c_ref, w_ref, b_ref, o_ref):
    c = c_ref[...]
    sc = (c * jax.nn.sigmoid(c)).astype(BF16)
    o_ref[...] = jnp.dot(sc, w_ref[...].astype(BF16), preferred_element_type=F32) + b_ref[...]


def _modulation(cond, w_mod, b_mod):
    depth, d, n = w_mod.shape
    rows = cond.shape[0]
    bn = _largest_divisor(n, (512, 256, 128))
    return pl.pallas_call(
        _modulation_kernel,
        grid=(depth, n // bn),
        in_specs=[pl.BlockSpec((rows, d), lambda l, j: (0, 0)),
                  pl.BlockSpec((None, d, bn), lambda l, j: (l, 0, j)),
                  pl.BlockSpec((None, 1, bn), lambda l, j: (l, 0, j))],
        out_specs=pl.BlockSpec((None, rows, bn), lambda l, j: (l, 0, j)),
        out_shape=jax.ShapeDtypeStruct((depth, rows, n), F32),
        compiler_params=_cparams(2, 32),
        name="modulation",
    )(cond, w_mod, b_mod.reshape(depth, 1, n))


def _rope_tables(n_tokens, rot_dim, lead, width):
    t = jnp.arange(n_tokens)
    row = (t // GRID_W).astype(F32)
    col = (t % GRID_W).astype(F32)
    half = rot_dim // 2
    inv = ROPE_BASE ** (-jnp.arange(0, half, 2, dtype=F32) / half)
    ar = row[:, None] * inv
    ac = col[:, None] * inv
    ang = jnp.concatenate([ar, ar, ac, ac], axis=-1)
    cos, sin = jnp.cos(ang), jnp.sin(ang)
    first = (np.arange(rot_dim) % ROPE_GROUP) < ROPE_GROUP // 2
    sin_a = jnp.where(first, -sin, 0.0)
    sin_b = jnp.where(first, 0.0, sin)
    pad = ((0, 0), (lead, width - lead - rot_dim))
    return (jnp.pad(cos, pad, constant_values=1.0), jnp.pad(sin_a, pad), jnp.pad(sin_b, pad))


def _rope_apply(y, cos, sin_a, sin_b):
    shift = ROPE_GROUP // 2
    return y * cos + pltpu.roll(y, LANE - shift, 1) * sin_a + pltpu.roll(y, shift, 1) * sin_b


def _fill_lhs(x_ref, xs_ref, xn_ref, prologue, g_ref, sh_ref, sc_ref, row_chunk):
    bm = x_ref.shape[0]

    def chunk(r, carry):
        rows = pl.ds(pl.multiple_of(r * row_chunk, row_chunk), row_chunk)
        x = x_ref[rows, :].astype(F32)
        if prologue == "norm_mod":
            x = _norm_mod(x, g_ref[...], sh_ref[...], sc_ref[...])
        elif prologue == "norm":
            x = _rms(x, g_ref[...])
        if xn_ref is not None:
            xn_ref[rows, :] = x
        xs_ref[rows, :] = x.astype(BF16)
        return carry
    n_chunks = bm // row_chunk
    lax.fori_loop(0, n_chunks, chunk, 0, unroll=2 if n_chunks % 2 == 0 else 1)


def _head_norm_store(acc, o_ref, hg_ref, tabs, head_w, norm_div, col0, norm_cols, rope_tiles):
    bn = acc.shape[1]
    period = tabs[0].shape[1] if tabs is not None else LANE
    if head_w < LANE:
        head_of_row = lax.broadcasted_iota(jnp.int32, (LANE, LANE), 0) // head_w
        head_of_col = lax.broadcasted_iota(jnp.int32, (LANE, LANE), 1) // head_w
        same_head = jnp.where(head_of_row == head_of_col, 1.0, 0.0).astype(BF16)
    for s0 in range(0, bn, max(head_w, LANE)):
        normed = None if norm_cols is None else (col0 + s0 < norm_cols)
        tiles = [acc[:, s0 + k * LANE:s0 + (k + 1) * LANE] for k in range(max(head_w, LANE) // LANE)]
        if head_w >= LANE:
            sq = None
            for y in tiles:
                sq = y * y if sq is None else sq + y * y
            inv = lax.rsqrt(jnp.sum(sq, axis=-1, keepdims=True) / norm_div + EPS)
        else:
            y2 = tiles[0] * tiles[0]
            hi = y2.astype(BF16)
            lo = (y2 - hi.astype(F32)).astype(BF16)
            ssq = (jnp.dot(hi, same_head, preferred_element_type=F32)
                   + jnp.dot(lo, same_head, preferred_element_type=F32))
            inv = lax.rsqrt(ssq / norm_div + EPS)
        if normed is not None:
            inv = jnp.where(normed, inv, 1.0)
        for k, y in enumerate(tiles):
            c0 = s0 + k * LANE
            y = (y * inv) * hg_ref[:, c0:c0 + LANE]
            t0 = c0 % period
            if tabs is not None and rope_tiles[t0 // LANE]:
                rotated = _rope_apply(y, *(t[:, t0:t0 + LANE] for t in tabs))
                y = rotated if normed is None else jnp.where(normed, rotated, y)
            o_ref[:, c0:c0 + LANE] = y.astype(o_ref.dtype)


def _proj_kernel(*refs, prologue, emit_xn, epilogue, head_w, norm_div, norm_cols, rope, rope_tiles, row_chunk):
    it = iter(refs)
    x_ref = next(it)
    g_ref = next(it) if prologue is not None else None
    sh_ref, sc_ref = (next(it), next(it)) if prologue == "norm_mod" else (None, None)
    w_ref = next(it)
    if epilogue == "res":
        res_ref, gate_ref = next(it), next(it)
    if epilogue == "heads":
        hg_ref = next(it)
        tabs = (next(it), next(it), next(it)) if rope else None
    o_ref = next(it)
    xn_ref = next(it) if emit_xn else None
    xs_ref = next(it)
    j = pl.program_id(1)
    bn = o_ref.shape[1]

    @pl.when(j == 0)
    def _():
        _fill_lhs(x_ref, xs_ref, xn_ref, prologue, g_ref, sh_ref, sc_ref, row_chunk)

    acc = jnp.dot(xs_ref[...], w_ref[...], preferred_element_type=F32)
    if epilogue == "res":
        o_ref[...] = res_ref[...] + gate_ref[...] * acc
    elif epilogue == "heads":
        _head_norm_store(acc, o_ref, hg_ref, tabs, head_w, norm_div, j * bn, norm_cols, rope_tiles)
    else:
        o_ref[...] = acc.astype(o_ref.dtype)


def _proj(x, w, st, *, x_block=None, norm_g=None, mod=None, res=None, gate=None, heads=None, emit_xn=False,
          out_dtype=F32, bn=None, x_time_major=False, out_time_major=False, name="proj"):
    k, n = w.shape
    time_major = x_time_major or out_time_major
    bm = min(st.bm, st.seq) if time_major else st.bm
    tiles_per_seq = st.seq // bm if st.seq % bm == 0 else None
    if time_major:
        assert tiles_per_seq is not None and x_block is None
    if x_time_major:
        assert x.shape == (st.seq, st.nb * k)
        kidx = 0
    else:
        kx, kidx = x_block if x_block is not None else (x.shape[1], 0)
        assert kx == k and x.shape[0] == st.rows
    rows = st.rows
    prologue = None if norm_g is None else ("norm_mod" if mod is not None else "norm")
    epilogue = "res" if res is not None else ("heads" if heads is not None else None)
    rope = heads is not None and heads.get("tabs") is not None
    if bn is None:
        unit = LANE
        if epilogue == "heads":
            unit = max(heads["head_w"], LANE, heads["tabs"][0].shape[1] if rope else LANE)
        cap = 512 if epilogue == "res" else 1024
        bn = next((c for c in range(cap, unit - 1, -unit) if n % c == 0), n)
    mod_idx = lambda i: st.mod_index(i * bm)

    if x_time_major:
        in_specs = [pl.BlockSpec((bm, k), lambda i, j: (i % tiles_per_seq, i // tiles_per_seq))]
    else:
        in_specs = [pl.BlockSpec((bm, k), lambda i, j: (i, kidx))]
    args = [x]
    if prologue is not None:
        in_specs.append(pl.BlockSpec((1, k), lambda i, j: (0, 0)))
        args.append(norm_g)
    if prologue == "norm_mod":
        in_specs += [pl.BlockSpec((None, 1, k), lambda i, j: (mod_idx(i), 0, 0))] * 2
        args += list(mod)
    in_specs.append(pl.BlockSpec((k, bn), lambda i, j: (0, j)))
    args.append(w.astype(BF16))
    if epilogue == "res":
        in_specs += [pl.BlockSpec((bm, bn), lambda i, j: (i, j)),
                     pl.BlockSpec((None, 1, bn), lambda i, j: (mod_idx(i), 0, j))]
        args += [res, gate]
    head_w = norm_div = 0
    norm_cols = rope_tiles = None
    if epilogue == "heads":
        head_w, norm_div = heads["head_w"], heads["norm_div"]
        norm_cols = heads["norm_cols"] if heads["norm_cols"] < n else None
        assert bn % max(head_w, LANE) == 0 and heads["norm_cols"] % max(head_w, LANE) == 0
        in_specs.append(pl.BlockSpec((1, bn), lambda i, j: (0, j)))
        args.append(heads["gains"])
        if rope:
            period = heads["tabs"][0].shape[1]
            rope_tiles = heads["rope_tiles"]
            assert bn % period == 0 and tiles_per_seq is not None and len(rope_tiles) == period // LANE
            in_specs += [pl.BlockSpec((bm, period), lambda i, j: (i % tiles_per_seq, 0))] * 3
            args += list(heads["tabs"])
    if out_time_major:
        n_col_tiles = n // bn
        out_shape = [jax.ShapeDtypeStruct((st.seq, st.nb * n), out_dtype)]
        out_specs = [pl.BlockSpec((bm, bn), lambda i, j: (i % tiles_per_seq, (i // tiles_per_seq) * n_col_tiles + j))]
    else:
        out_shape = [jax.ShapeDtypeStruct((rows, n), out_dtype)]
        out_specs = [pl.BlockSpec((bm, bn), lambda i, j: (i, j))]
    if emit_xn:
        out_shape.append(jax.ShapeDtypeStruct((rows, k), F32))
        out_specs.append(pl.BlockSpec((bm, k), lambda i, j: (i, 0)))
    xbytes = x.dtype.itemsize
    vmem = (2 * bm * k * xbytes + bm * k * 2 + 2 * k * bn * 2 + (6 if epilogue == "res" else 4) * bm * bn * 4
            + (2 * bm * k * 4 if emit_xn else 0)) / MIB + 8
    kern = functools.partial(_proj_kernel, prologue=prologue, emit_xn=emit_xn, epilogue=epilogue, head_w=head_w,
                             norm_div=norm_div, norm_cols=norm_cols, rope=rope, rope_tiles=rope_tiles,
                             row_chunk=min(bm, 128))
    out = pl.pallas_call(
        kern,
        grid=(rows // bm, n // bn),
        in_specs=in_specs,
        out_specs=out_specs,
        out_shape=out_shape,
        scratch_shapes=[pltpu.VMEM((bm, k), BF16)],
        compiler_params=_cparams(2, vmem),
        name=name,
    )(*args)
    return out if emit_xn else out[0]


def _ffn_kernel(xp_ref, x_ref, xn_ref, g_ref, sh_ref, sc_ref, gate_ref, wa_ref, wb_ref, cw_ref, cb_ref,
                wo_ref, o_ref, h_ref, *, bm, seq, row_chunk):
    i = pl.program_id(0)
    c = pl.program_id(1)
    n_chunks = pl.num_programs(1)

    @pl.when(c == 0)
    def _():
        g, sh, sc = g_ref[...], sh_ref[...], sc_ref[...]
        h_ref[0:HALO, :] = _norm_mod(xp_ref[...], g, sh, sc).astype(BF16)
        h_ref[HALO + bm:, :] = _norm_mod(xn_ref[...], g, sh, sc).astype(BF16)

        def chunk(r, carry):
            src = pl.ds(pl.multiple_of(r * row_chunk, row_chunk), row_chunk)
            dst = pl.ds(pl.multiple_of(HALO + r * row_chunk, HALO), row_chunk)
            h_ref[dst, :] = _norm_mod(x_ref[src, :], g, sh, sc).astype(BF16)
            return carry
        n_row_chunks = bm // row_chunk
        lax.fori_loop(0, n_row_chunks, chunk, 0, unroll=2 if n_row_chunks % 2 == 0 else 1)
        o_ref[...] = jnp.zeros_like(o_ref)

    ua = jnp.dot(h_ref[...], wa_ref[...], preferred_element_type=F32)
    ub = jnp.dot(h_ref[HALO:HALO + bm, :], wb_ref[...], preferred_element_type=F32)
    n_all = bm + 2 * HALO
    u_prev = pltpu.roll(ua, 1, 0)[HALO:HALO + bm]
    u_next = pltpu.roll(ua, n_all - 1, 0)[HALO:HALO + bm]
    u_mid = ua[HALO:HALO + bm]
    pos = jnp.bitwise_and(i * bm + lax.broadcasted_iota(jnp.int32, (bm, 1), 0), seq - 1)
    u_prev = jnp.where(pos == 0, 0.0, u_prev)
    u_next = jnp.where(pos == seq - 1, 0.0, u_next)
    cw = cw_ref[...]
    a = cb_ref[...] + u_prev * cw[0:1] + u_mid * cw[1:2] + u_next * cw[2:3]
    gated = ((a * jax.nn.sigmoid(a)) * ub).astype(BF16)
    o_ref[...] += jnp.dot(gated, wo_ref[...], preferred_element_type=F32)

    @pl.when(c == n_chunks - 1)
    def _():
        o_ref[...] = x_ref[...] + gate_ref[...] * o_ref[...]


def _conv_ffn(x, st, g, shift, scale, gate, w_in, conv_w, conv_b, w_out, bm=1024, ck=512):
    m, d = x.shape
    d_ff = w_out.shape[0]
    bm = min(st.bm, bm)
    ck = _largest_divisor(d_ff, tuple(c for c in (512, 256, 128) if c <= ck))
    n_chunks = d_ff // ck
    n_halo_blocks = m // HALO
    assert st.seq & (st.seq - 1) == 0 and conv_w.shape[0] == 3
    mod_idx = lambda i: st.mod_index(i * bm)
    kern = functools.partial(_ffn_kernel, bm=bm, seq=st.seq, row_chunk=min(bm, 128))
    vmem = (4 * bm * d * 4 + (bm + 2 * HALO) * d * 2 + 6 * d * ck * 2 + 5 * (bm + 2 * HALO) * ck * 4) / MIB + 4
    return pl.pallas_call(
        kern,
        grid=(m // bm, n_chunks),
        in_specs=[
            pl.BlockSpec((HALO, d), lambda i, c: (jnp.maximum(i * (bm // HALO) - 1, 0), 0)),
            pl.BlockSpec((bm, d), lambda i, c: (i, 0)),
            pl.BlockSpec((HALO, d), lambda i, c: (jnp.minimum((i + 1) * (bm // HALO), n_halo_blocks - 1), 0)),
            pl.BlockSpec((1, d), lambda i, c: (0, 0)),
            pl.BlockSpec((None, 1, d), lambda i, c: (mod_idx(i), 0, 0)),
            pl.BlockSpec((None, 1, d), lambda i, c: (mod_idx(i), 0, 0)),
            pl.BlockSpec((None, 1, d), lambda i, c: (mod_idx(i), 0, 0)),
            pl.BlockSpec((d, ck), lambda i, c: (0, c)),
            pl.BlockSpec((d, ck), lambda i, c: (0, n_chunks + c)),
            pl.BlockSpec((conv_w.shape[0], ck), lambda i, c: (0, c)),
            pl.BlockSpec((1, ck), lambda i, c: (0, c)),
            pl.BlockSpec((ck, d), lambda i, c: (c, 0)),
        ],
        out_specs=pl.BlockSpec((bm, d), lambda i, c: (i, 0)),
        out_shape=jax.ShapeDtypeStruct((m, d), F32),
        scratch_shapes=[pltpu.VMEM((bm + 2 * HALO, d), BF16)],
        compiler_params=_cparams(2, vmem),
        name="conv_ffn",
    )(x, x, x, g, shift, scale, gate, w_in, w_in, conv_w, conv_b.reshape(1, d_ff), w_out)


def _qk(q, k):
    return lax.dot_general(q, k, (((1,), (1,)), ((), ())), preferred_element_type=F32)


def _attend(scores, values, sink=None):
    m = None
    for s in scores:
        mi = jnp.max(s, axis=-1, keepdims=True)
        m = mi if m is None else jnp.maximum(m, mi)
    if sink is not None:
        m = jnp.maximum(m, sink)
    es = [jnp.exp2(s - m) for s in scores]
    den = None
    for e in es:
        di = jnp.sum(e, axis=-1, keepdims=True)
        den = di if den is None else den + di
    if sink is not None:
        den = den + jnp.exp2(sink - m)
    out = None
    for e, v in zip(es, values):
        oi = jnp.dot(e.astype(BF16), v, preferred_element_type=F32)
        out = oi if out is None else out + oi
    return out * (1.0 / den)


def _ctx_attn_kernel(q_ref, k_ref, v_ref, *outs, heads, dq, dv, emit_kv):
    o_ref = outs[0]
    for h in range(heads):
        q = q_ref[:, h * dq:(h + 1) * dq].astype(BF16)
        k = k_ref[:, h * dq:(h + 1) * dq]
        v = v_ref[:, h * dv:(h + 1) * dv]
        if emit_kv:
            outs[1][h] = k.astype(F32)
            outs[2][h] = v.astype(F32)
        o = _attend([_qk(q, k.astype(BF16))], [v.astype(BF16)])
        o_ref[:, h * dv:(h + 1) * dv] = o.astype(o_ref.dtype)


def _ctx_attention(qm, km, vm, st, *, n_heads, dq, dv, q_col, k_col, v_col, emit_kv=False):
    hb = _largest_divisor(n_heads, (4, 2, 1))
    s = st.seq
    assert q_col % (hb * dq) == 0 and k_col % (hb * dq) == 0 and v_col % (hb * dv) == 0
    qo, ko, vo = q_col // (hb * dq), k_col // (hb * dq), v_col // (hb * dv)
    out_shape = [jax.ShapeDtypeStruct((st.rows, n_heads * dv), BF16)]
    out_specs = [pl.BlockSpec((s, hb * dv), lambda b, g: (b, g))]
    if emit_kv:
        out_shape += [jax.ShapeDtypeStruct((st.nb, n_heads, s, dq), F32),
                      jax.ShapeDtypeStruct((st.nb, n_heads, s, dv), F32)]
        out_specs += [pl.BlockSpec((None, hb, s, dq), lambda b, g: (b, g, 0, 0)),
                      pl.BlockSpec((None, hb, s, dv), lambda b, g: (b, g, 0, 0))]
    out = pl.pallas_call(
        functools.partial(_ctx_attn_kernel, heads=hb, dq=dq, dv=dv, emit_kv=emit_kv),
        grid=(st.nb, n_heads // hb),
        in_specs=[pl.BlockSpec((s, hb * dq), lambda b, g: (b, qo + g)),
                  pl.BlockSpec((s, hb * dq), lambda b, g: (b, ko + g)),
                  pl.BlockSpec((s, hb * dv), lambda b, g: (b, vo + g))],
        out_specs=out_specs,
        out_shape=out_shape,
        compiler_params=_cparams(2, 32),
        name="ctx_attention",
    )(qm, km, vm)
    return out if emit_kv else out[0]


def _nat_kernel(q_ref, k_ref, v_ref, kc_ref, vc_ref, bias_ref, o_ref, *, key_rows, rows, heads, dh):
    i = pl.program_id(2)
    n_keys = key_rows * GRID_W
    first_row = jnp.clip(i * NA_Q_ROWS - NA_WIN_ROWS // 2, 0, rows - key_rows)
    start = pl.multiple_of(first_row * GRID_W, GRID_W * 4)
    for h in range(heads):
        lanes = slice(h * dh, (h + 1) * dh)
        q = q_ref[:, lanes]
        k = k_ref[pl.ds(start, n_keys), lanes]
        v = v_ref[pl.ds(start, n_keys), lanes]
        s_loc = _qk(q, k) + bias_ref[h]
        s_ctx = _qk(q, kc_ref[h].astype(BF16))
        o_ref[:, lanes] = _attend([s_loc, s_ctx], [v, vc_ref[h].astype(BF16)]).astype(o_ref.dtype)


def _nat_bias(rpb, rows):
    n_blocks = rows // NA_Q_ROWS
    key_rows = min(NA_K_ROWS, rows)
    wr = min(NA_WIN_ROWS, rows)
    reps = [0, min(1, n_blocks - 1), n_blocks - 1]
    heads = rpb.shape[0]
    nq, nk = NA_Q_ROWS * GRID_W, key_rows * GRID_W
    shape = (NA_Q_ROWS, GRID_W, key_rows, GRID_W)
    qc = np.arange(GRID_W)
    cstart = np.clip(qc - NA_WIN_COLS // 2, 0, GRID_W - NA_WIN_COLS)
    col_ok = (qc[None, :] >= cstart[:, None]) & (qc[None, :] < cstart[:, None] + NA_WIN_COLS)
    rp = jnp.pad(rpb.astype(F32) * LOG2E,
                 ((0, 0), (key_rows, key_rows), (GRID_W - NA_WIN_COLS, GRID_W - NA_WIN_COLS)))
    row_slabs, mask_l = [], []
    for i in reps:
        ks = int(np.clip(i * NA_Q_ROWS - NA_WIN_ROWS // 2, 0, rows - key_rows))
        r = i * NA_Q_ROWS + np.arange(NA_Q_ROWS)
        rs = np.clip(r - wr // 2, 0, rows - wr)
        kr = ks + np.arange(key_rows)
        row_ok = (kr[None, :] >= rs[:, None]) & (kr[None, :] < rs[:, None] + wr)
        for rq in range(NA_Q_ROWS):
            first = ks - int(r[rq]) + NA_WIN_ROWS - 1 + key_rows
            assert 0 <= first and first + key_rows <= rp.shape[1]
            row_slabs.append(rp[:, first:first + key_rows, :])
        mask_l.append(np.broadcast_to(row_ok[:, None, :, None] & col_ok[None, :, None, :], shape).reshape(nq, nk))
    slab = jnp.stack(row_slabs, axis=1).reshape(heads, len(reps), NA_Q_ROWS, key_rows, 2 * GRID_W - 1)
    toep = jnp.stack([slab[..., GRID_W - 1 - c:2 * GRID_W - 1 - c] for c in range(GRID_W)], axis=3)
    bias = toep.reshape(heads, len(reps), nq, nk)
    return jnp.where(jnp.asarray(np.stack(mask_l))[None], bias, NEG)


def _nat_attention(qkv, st, cache_k, cache_v, j, rpb, dh):
    heads = rpb.shape[0]
    n = st.seq
    p = cache_k.shape[3]
    rows = n // GRID_W
    assert rows % NA_Q_ROWS == 0 and rows >= NA_K_ROWS and dh % LANE == 0
    n_blocks = rows // NA_Q_ROWS
    key_rows = min(NA_K_ROWS, rows)
    nq, nk = NA_Q_ROWS * GRID_W, key_rows * GRID_W
    bias = _nat_bias(rpb, rows)
    btype = lambda i: jnp.where(i == 0, 0, jnp.where(i == n_blocks - 1, 2, 1))
    hb = _largest_divisor(heads, (4, 2, 1))
    hg = heads // hb
    kern = functools.partial(_nat_kernel, key_rows=key_rows, rows=rows, heads=hb, dh=dh)
    return pl.pallas_call(
        kern,
        grid=(st.nb, hg, n_blocks),
        in_specs=[pl.BlockSpec((nq, hb * dh), lambda b, h, i: (b * n_blocks + i, h)),
                  pl.BlockSpec((n, hb * dh), lambda b, h, i: (b, hg + h)),
                  pl.BlockSpec((n, hb * dh), lambda b, h, i: (b, 2 * hg + h)),
                  pl.BlockSpec((None, None, hb, p, dh), lambda b, h, i: (b, j, h, 0, 0)),
                  pl.BlockSpec((None, None, hb, p, dh), lambda b, h, i: (b, j, h, 0, 0)),
                  pl.BlockSpec((hb, None, nq, nk), lambda b, h, i: (h, btype(i), 0, 0))],
        out_specs=pl.BlockSpec((nq, hb * dh), lambda b, h, i: (b * n_blocks + i, h)),
        out_shape=jax.ShapeDtypeStruct((st.rows, heads * dh), BF16),
        compiler_params=_cparams(3, 56),
        name="nat_attention",
    )(qkv, qkv, qkv, cache_k, cache_v, bias)


def _joint_dense_kernel(q_ref, k_ref, v_ref, kc_ref, vc_ref, o_ref, *, chunk):
    q = q_ref[...]
    n = k_ref.shape[0]
    pieces = [(k_ref, v_ref, c0, min(chunk, n - c0)) for c0 in range(0, n, chunk)]
    pieces.append((kc_ref, vc_ref, 0, kc_ref.shape[0]))
    m = den = acc = None
    for kr, vr, c0, size in pieces:
        s = _qk(q, kr[c0:c0 + size, :])
        mc = jnp.max(s, axis=-1, keepdims=True)
        m_new = mc if m is None else jnp.maximum(m, mc)
        e = jnp.exp2(s - m_new)
        dc = jnp.sum(e, axis=-1, keepdims=True)
        pv = jnp.dot(e.astype(BF16), vr[c0:c0 + size, :], preferred_element_type=F32)
        if m is None:
            den, acc = dc, pv
        else:
            alpha = jnp.exp2(m - m_new)
            den, acc = alpha * den + dc, alpha * acc + pv
        m = m_new
    o_ref[...] = (acc * (1.0 / den)).astype(o_ref.dtype)


def _joint_dense_attention(qm, km, vm, kcm, vcm, st, p, *, n_heads, dq, dv):
    n = st.seq
    bq = _largest_divisor(n, (1024, 512, 256, 128, 64, 32, 16))
    nqb = n // bq
    return pl.pallas_call(
        functools.partial(_joint_dense_kernel, chunk=1024),
        grid=(st.nb, n_heads, nqb),
        in_specs=[pl.BlockSpec((bq, dq), lambda b, h, i: (b * nqb + i, h)),
                  pl.BlockSpec((n, dq), lambda b, h, i: (b, h)),
                  pl.BlockSpec((n, dv), lambda b, h, i: (b, h)),
                  pl.BlockSpec((p, dq), lambda b, h, i: (b, h)),
                  pl.BlockSpec((p, dv), lambda b, h, i: (b, h))],
        out_specs=pl.BlockSpec((bq, dv), lambda b, h, i: (b * nqb + i, h)),
        out_shape=jax.ShapeDtypeStruct((st.rows, n_heads * dv), BF16),
        compiler_params=_cparams(3, 48),
        name="mla_attention",
    )(qm, km, vm, kcm, vcm)


def _both_halves(x, s):
    x = x.astype(F32)
    low = lax.broadcasted_iota(jnp.int32, (1, LANE), 1) < LANE // 2
    keep = low if s == 0 else jnp.logical_not(low)
    return jnp.where(keep, x, pltpu.roll(x, LANE // 2, 1)).astype(BF16)


def _swa_step(sinks_ref, pair, q_ref, k, v, kc, vc, o_ref, *, dh, groups, local_bias):
    kv_per_step = LANE // dh
    assert kv_per_step == 2 and groups % 2 == 0
    rows = q_ref.shape[0]
    low = lax.broadcasted_iota(jnp.int32, (1, LANE), 1) < dh
    row_group = lax.broadcasted_iota(jnp.int32, (groups * rows, 1), 0) // rows
    for s in range(kv_per_step):
        kd, vd = _both_halves(k, s), _both_halves(v, s)
        q_parts = []
        for g in range(groups):
            c0 = ((s * groups + g) * dh // LANE) * LANE
            tile = q_ref[:, c0:c0 + LANE].astype(BF16)
            q_parts.append(jnp.where(low if g % 2 == 0 else jnp.logical_not(low), tile, jnp.zeros_like(tile)))
        q = jnp.concatenate(q_parts, axis=0)
        sink = jnp.zeros((groups * rows, 1), F32)
        for g in range(groups):
            sink = jnp.where(row_group == g, sinks_ref[(pair * kv_per_step + s) * groups + g], sink)
        s_loc = _qk(q, kd)
        if local_bias is not None:
            s_loc = s_loc + jnp.concatenate([local_bias] * groups, axis=0)
        if kc is not None:
            kcd = jnp.concatenate([kc[s], kc[s]], axis=-1).astype(BF16)
            vcd = jnp.concatenate([vc[s], vc[s]], axis=-1).astype(BF16)
            out = _attend([s_loc, _qk(q, kcd)], [vd, vcd], sink)
        else:
            out = _attend([s_loc], [vd], sink)
        for g in range(0, groups, 2):
            c0 = (s * groups + g) * dh
            o_ref[:, c0:c0 + LANE] = jnp.where(low, out[g * rows:(g + 1) * rows],
                                               out[(g + 1) * rows:(g + 2) * rows]).astype(o_ref.dtype)


def _swa_ctx_kernel(sinks_ref, q_ref, k_ref, v_ref, o_ref, ko_ref, vo_ref, *, dh, groups):
    pair = pl.program_id(1)
    k, v = k_ref[...], v_ref[...]
    for s in range(LANE // dh):
        ko_ref[s] = k[:, s * dh:(s + 1) * dh].astype(F32)
        vo_ref[s] = v[:, s * dh:(s + 1) * dh].astype(F32)
    _swa_step(sinks_ref, pair, q_ref, k, v, None, None, o_ref, dh=dh, groups=groups, local_bias=None)


def _swa_lat_kernel(sinks_ref, q_ref, k_ref, v_ref, kc_ref, vc_ref, o_ref, *, dh, groups, n):
    pair = pl.program_id(1)
    blk = pl.program_id(2)
    n_keys = min(3 * SWA_BLOCK, n)
    start = pl.multiple_of(jnp.clip((blk - 1) * SWA_BLOCK, 0, n - n_keys), SWA_BLOCK)
    k = k_ref[pl.ds(start, n_keys), :]
    v = v_ref[pl.ds(start, n_keys), :]
    qpos = blk * SWA_BLOCK + lax.broadcasted_iota(jnp.int32, (SWA_BLOCK, 1), 0)
    kpos = start + lax.broadcasted_iota(jnp.int32, (1, n_keys), 1)
    bias = jnp.where(jnp.abs(qpos - kpos) <= SWA_WINDOW, 0.0, NEG)
    _swa_step(sinks_ref, pair, q_ref, k, v, kc_ref, vc_ref, o_ref, dh=dh, groups=groups, local_bias=bias)


def _swa_attention(qkv, st, sinks, *, heads, kvh, dh, cache=None):
    groups = heads // kvh
    kv_per_step = LANE // dh
    assert LANE % dh == 0 and kvh % kv_per_step == 0 and groups % kv_per_step == 0
    pairs = kvh // kv_per_step
    qw = kv_per_step * groups * dh
    k_blk = heads * dh // LANE
    v_blk = (heads + kvh) * dh // LANE
    n = st.seq
    common = dict(dh=dh, groups=groups)
    smem = pl.BlockSpec(memory_space=pltpu.SMEM)
    if cache is None:
        out = pl.pallas_call(
            functools.partial(_swa_ctx_kernel, **common),
            grid=(st.nb, pairs),
            in_specs=[smem,
                      pl.BlockSpec((n, qw), lambda b, c: (b, c)),
                      pl.BlockSpec((n, LANE), lambda b, c: (b, k_blk + c)),
                      pl.BlockSpec((n, LANE), lambda b, c: (b, v_blk + c))],
            out_specs=[pl.BlockSpec((n, qw), lambda b, c: (b, c)),
                       pl.BlockSpec((None, kv_per_step, n, dh), lambda b, c: (b, c, 0, 0)),
                       pl.BlockSpec((None, kv_per_step, n, dh), lambda b, c: (b, c, 0, 0))],
            out_shape=[jax.ShapeDtypeStruct((st.rows, heads * dh), BF16),
                       jax.ShapeDtypeStruct((st.nb, kvh, n, dh), F32),
                       jax.ShapeDtypeStruct((st.nb, kvh, n, dh), F32)],
            compiler_params=_cparams(2, 32),
            name="swa_ctx_attention",
        )(sinks, qkv, qkv, qkv)
        return out
    cache_k, cache_v, j = cache
    p = cache_k.shape[3]
    nblk = n // SWA_BLOCK
    assert n % SWA_BLOCK == 0
    return pl.pallas_call(
        functools.partial(_swa_lat_kernel, n=n, **common),
        grid=(st.nb, pairs, nblk),
        in_specs=[smem,
                  pl.BlockSpec((SWA_BLOCK, qw), lambda b, c, i: (b * nblk + i, c)),
                  pl.BlockSpec((n, LANE), lambda b, c, i: (b, k_blk + c)),
                  pl.BlockSpec((n, LANE), lambda b, c, i: (b, v_blk + c)),
                  pl.BlockSpec((None, None, kv_per_step, p, dh), lambda b, c, i: (b, j, c, 0, 0)),
                  pl.BlockSpec((None, None, kv_per_step, p, dh), lambda b, c, i: (b, j, c, 0, 0))],
        out_specs=pl.BlockSpec((SWA_BLOCK, qw), lambda b, c, i: (b * nblk + i, c)),
        out_shape=jax.ShapeDtypeStruct((st.rows, heads * dh), BF16),
        compiler_params=_cparams(3, 32),
        name="swa_attention",
    )(sinks, qkv, qkv, qkv, cache_k, cache_v)


def _mla_kv_kernel(*refs, norm, emit_xn, rope, row_chunk, norm_div):
    it = iter(refs)
    x_ref = next(it)
    g_ref = next(it) if norm else None
    w_ref, kr_ref, g1_ref, g2_ref = next(it), next(it), next(it), next(it)
    tabs = (next(it), next(it), next(it)) if rope else None
    k_ref, v_ref = next(it), next(it)
    xn_ref = next(it) if emit_xn else None
    xs_ref = next(it)

    @pl.when(pl.program_id(1) == 0)
    def _():
        _fill_lhs(x_ref, xs_ref, xn_ref, "norm" if norm else None, g_ref, None, None, row_chunk)

    acc = jnp.dot(xs_ref[...], w_ref[...], preferred_element_type=F32)
    kr = kr_ref[...]
    kr_ssq = jnp.sum(kr * kr, axis=-1, keepdims=True)
    shared = kr * g2_ref[...]
    if rope:
        shared = _rope_apply(shared, *(t[...] for t in tabs))
    for h in range(acc.shape[1] // (2 * LANE)):
        nope = acc[:, 2 * h * LANE:(2 * h + 1) * LANE]
        inv = lax.rsqrt((jnp.sum(nope * nope, axis=-1, keepdims=True) + kr_ssq) / norm_div + EPS)
        k_ref[:, 2 * h * LANE:(2 * h + 1) * LANE] = ((nope * inv) * g1_ref[...]).astype(k_ref.dtype)
        k_ref[:, (2 * h + 1) * LANE:(2 * h + 2) * LANE] = (shared * inv).astype(k_ref.dtype)
        v_ref[:, h * LANE:(h + 1) * LANE] = acc[:, (2 * h + 1) * LANE:(2 * h + 2) * LANE].astype(v_ref.dtype)


def _mla_kv(x, x_block, w_ukv, kr, kr_block, g_kva, g1, g2, tabs, st, *, n_heads, norm_div, emit_xn, name):
    rows = x.shape[0]
    k, n = w_ukv.shape
    head_n = n // n_heads
    assert head_n == 2 * LANE, "nope and value widths must both be one lane tile"
    kx, kidx = x_block
    krw, kridx = kr_block
    assert kx == k and krw == LANE
    bm = st.bm
    norm = g_kva is not None
    rope = tabs is not None
    in_specs = [pl.BlockSpec((bm, k), lambda i, h: (i, kidx))]
    args = [x]
    if norm:
        in_specs.append(pl.BlockSpec((1, k), lambda i, h: (0, 0)))
        args.append(g_kva)
    hb = _largest_divisor(n_heads, (4, 2, 1))
    in_specs += [pl.BlockSpec((k, hb * head_n), lambda i, h: (0, h)),
                 pl.BlockSpec((bm, LANE), lambda i, h: (i, kridx)),
                 pl.BlockSpec((1, LANE), lambda i, h: (0, 0)),
                 pl.BlockSpec((1, LANE), lambda i, h: (0, 0))]
    args += [w_ukv, kr, g1, g2]
    if rope:
        tiles_per_seq = st.seq // bm
        in_specs += [pl.BlockSpec((bm, LANE), lambda i, h: (i % tiles_per_seq, 0))] * 3
        args += list(tabs)
    out_shape = [jax.ShapeDtypeStruct((rows, n_heads * 2 * LANE), BF16),
                 jax.ShapeDtypeStruct((rows, n_heads * LANE), BF16)]
    out_specs = [pl.BlockSpec((bm, hb * 2 * LANE), lambda i, h: (i, h)),
                 pl.BlockSpec((bm, hb * LANE), lambda i, h: (i, h))]
    if emit_xn:
        out_shape.append(jax.ShapeDtypeStruct((rows, k), F32))
        out_specs.append(pl.BlockSpec((bm, k), lambda i, h: (i, 0)))
    kern = functools.partial(_mla_kv_kernel, norm=norm, emit_xn=emit_xn, rope=rope, row_chunk=min(bm, 128),
                             norm_div=norm_div)
    return pl.pallas_call(
        kern,
        grid=(rows // bm, n_heads // hb),
        in_specs=in_specs,
        out_specs=out_specs,
        out_shape=out_shape,
        scratch_shapes=[pltpu.VMEM((bm, k), BF16)],
        compiler_params=_cparams(2, 40),
        name=name,
    )(*args)


def _band_plan(width, block):
    n_tiles = width // LANE
    lo = [((t * LANE) // block) * block for t in range(n_tiles)]
    hi = [(((t + 1) * LANE - 1) // block + 1) * block for t in range(n_tiles)]
    start = [(l // LANE) * LANE for l in lo]
    kb = max(-(-(h - s) // LANE) * LANE for h, s in zip(hi, start))
    kb = min(kb, width)
    start = [min(s, width - kb) for s in start]
    return start, kb


def _band_weights(w, width, block, start, kb):
    n_tiles = width // LANE
    wb = w.astype(BF16)
    tiles = []
    for t in range(n_tiles):
        pieces = []
        col = t * LANE
        while col < (t + 1) * LANE:
            blk = col // block
            col_end = min((blk + 1) * block, (t + 1) * LANE)
            sub = wb[blk, :, col - blk * block:col_end - blk * block]
            top = blk * block - start[t]
            pieces.append(jnp.pad(sub, ((top, kb - top - block), (0, 0))))
            col = col_end
        tiles.append(jnp.concatenate(pieces, axis=1))
    return jnp.stack(tiles)


def _gelu_tanh(x):
    cdf = 0.5 * (1.0 + jnp.tanh(np.float32(np.sqrt(2.0 / np.pi)) * (x + 0.044715 * (x * x * x))))
    return x * cdf


def _lru_pass_kernel(*refs, reverse, starts, kb, bt, seq, nb, taps):
    left = taps // 2
    right = taps - 1 - left
    it = iter(refs)
    xp_ref, x_ref = next(it), next(it)
    xn_ref = next(it) if right > 0 else None
    cw_ref, cb_ref = next(it), next(it)
    wa_ref, wi_ref, ba_ref, bi_ref, lam_ref, h0_ref = (next(it) for _ in range(6))
    hsf_ref, gate_ref = (next(it), next(it)) if reverse else (None, None)
    out_ref, ht_ref = next(it), next(it)
    xc_s, xb_s, a_s, bx_s, carry = next(it), next(it), next(it), next(it), next(it)

    step = pl.program_id(0)
    n_steps = pl.num_programs(0)
    tile = (n_steps - 1 - step) if reverse else step
    n_tiles = len(starts)
    rows = nb * bt

    @pl.when(step == 0)
    def _():
        carry[...] = h0_ref[...]

    for t in range(n_tiles):
        lanes = slice(t * LANE, (t + 1) * LANE)
        parts = [jnp.where(tile > 0, xp_ref[:, :, lanes], 0.0), x_ref[:, :, lanes]]
        if right > 0:
            parts.append(jnp.where(tile < n_steps - 1, xn_ref[:, :, lanes], 0.0))
        full = jnp.concatenate(parts, axis=0)
        acc = jnp.broadcast_to(cb_ref[:, lanes], (bt, nb, LANE))
        for k in range(taps):
            acc = acc + full[k:k + bt] * cw_ref[k:k + 1, lanes]
        acc = acc.reshape(rows, LANE)
        xc_s[:, lanes] = acc
        xb_s[:, lanes] = acc.astype(BF16)

    neg_lam = -lam_ref[...]
    softplus = jnp.maximum(neg_lam, 0.0) + jnp.log1p(jnp.exp(-jnp.abs(neg_lam)))
    half_rate = (-0.5 * LRU_C) * softplus
    half_ba, half_bi = 0.5 * ba_ref[...], 0.5 * bi_ref[...]
    for t in range(n_tiles):
        lanes = slice(t * LANE, (t + 1) * LANE)
        xw = xb_s[:, starts[t]:starts[t] + kb]
        tanh_a = jnp.tanh(jnp.dot(xw, wa_ref[t], preferred_element_type=F32) + half_ba[:, lanes])
        tanh_i = jnp.tanh(jnp.dot(xw, wi_ref[t], preferred_element_type=F32) + half_bi[:, lanes])
        log_a = half_rate[:, lanes] * tanh_a + half_rate[:, lanes]
        a = jnp.exp(log_a)
        half_x = 0.5 * xc_s[:, lanes]
        a_s[:, lanes] = a
        bx_s[:, lanes] = jnp.sqrt(-jnp.tanh(log_a) * (a * a + 1.0)) * (half_x * tanh_i + half_x)

    h = carry[...]
    for s in range(bt):
        ts = (bt - 1 - s) if reverse else s
        slab = slice(ts * nb, (ts + 1) * nb)
        h = a_s[slab, :] * h + bx_s[slab, :]
        a_s[slab, :] = h
    carry[...] = h
    ht_ref[...] = h
    hs = a_s[...].reshape(bt, nb, a_s.shape[1])
    if reverse:
        out_ref[...] = (_gelu_tanh(gate_ref[...]) * (hsf_ref[...] + hs)).astype(out_ref.dtype)
    else:
        out_ref[...] = hs


def _lru_pass(u, st, conv_w, conv_b, wa, wi, b_a, b_i, lam, h0, starts, kb, *, reverse, hs_fwd=None):
    c = conv_w.shape[1]
    taps = conv_w.shape[0]
    left, right = taps // 2, taps - 1 - taps // 2
    nb, seq = st.nb, st.seq
    bt = min(max(256 // nb, SUBLANE), seq)
    assert seq % bt == 0 and c % LANE == 0 and left > 0 and bt % left == 0 and (right == 0 or bt % right == 0)
    nt = seq // bt
    n_tiles = c // LANE
    u3 = u.reshape(seq, nb, 2 * c)
    tmap = (lambda s: nt - 1 - s) if reverse else (lambda s: s)
    full = lambda *shape: pl.BlockSpec(shape, lambda s: (0,) * len(shape))
    in_specs = [pl.BlockSpec((left, nb, c), lambda s: (jnp.maximum(tmap(s) * (bt // left) - 1, 0), 0, 0)),
                pl.BlockSpec((bt, nb, c), lambda s: (tmap(s), 0, 0))]
    args = [u3, u3]
    if right > 0:
        in_specs.append(pl.BlockSpec((right, nb, c),
                                     lambda s: (jnp.minimum((tmap(s) + 1) * (bt // right), seq // right - 1), 0, 0)))
        args.append(u3)
    in_specs += [full(taps, c), full(1, c), full(n_tiles, kb, LANE), full(n_tiles, kb, LANE), full(1, c), full(1, c),
                 full(1, c), full(nb, c)]
    args += [conv_w, conv_b.reshape(1, c), wa, wi, b_a.reshape(1, c), b_i.reshape(1, c), lam.reshape(1, c), h0]
    if reverse:
        in_specs += [pl.BlockSpec((bt, nb, c), lambda s: (tmap(s), 0, 0)),
                     pl.BlockSpec((bt, nb, c), lambda s: (tmap(s), 0, 1))]
        args += [hs_fwd, u3]
    kern = functools.partial(_lru_pass_kernel, reverse=reverse, starts=tuple(starts), kb=kb, bt=bt, seq=seq, nb=nb,
                             taps=taps)
    blk = nb * bt * c * 4 / MIB
    vmem = (2 + 2 + 3 + (4 if reverse else 0) + 4) * blk + 4 * n_tiles * kb * LANE * 2 / MIB + 8
    return pl.pallas_call(
        kern,
        grid=(nt,),
        in_specs=in_specs,
        out_specs=[pl.BlockSpec((bt, nb, c), lambda s: (tmap(s), 0, 0)),
                   pl.BlockSpec((nb, c), lambda s: (0, 0))],
        out_shape=[jax.ShapeDtypeStruct((seq, nb, c), F32),
                   jax.ShapeDtypeStruct((nb, c), F32)],
        scratch_shapes=[pltpu.VMEM((nb * bt, c), F32), pltpu.VMEM((nb * bt, c), BF16),
                        pltpu.VMEM((nb * bt, c), F32), pltpu.VMEM((nb * bt, c), F32), pltpu.VMEM((nb, c), F32)],
        compiler_params=_cparams(1, vmem),
        name="lru_bwd" if reverse else "lru_fwd",
    )(*args)


def _mixer_nat(xs, streams, mods, cache_k, cache_v, j, w_qkv, g_mix, g_q, g_k, rpb, w_o):
    heads, dh = rpb.shape[0], g_q.shape[0]
    w_qkv, w_o = w_qkv.astype(BF16), w_o.astype(BF16)
    gains = jnp.concatenate([jnp.tile(g_q * (dh ** -0.5 * LOG2E), heads), jnp.tile(g_k, heads),
                             jnp.ones((heads * dh,), F32)])[None]
    spec = dict(head_w=dh, norm_div=dh, norm_cols=2 * heads * dh, gains=gains)
    new_x, extra = [], None
    for x, st in zip(xs, streams):
        latent = not st.shared
        qkv = _proj(x, w_qkv, st, norm_g=g_mix, mod=(mods[0], mods[1]), heads=spec,
                    out_dtype=BF16 if latent else F32, name="nat_qkv")
        if latent:
            o = _nat_attention(qkv, st, cache_k, cache_v, j, rpb, dh)
        else:
            o, kc, vc = _ctx_attention(qkv, qkv, qkv, st, n_heads=heads, dq=dh, dv=dh, q_col=0, k_col=heads * dh,
                                       v_col=2 * heads * dh, emit_kv=True)
            extra = (kc, vc)
        new_x.append(_proj(o, w_o, st, res=x, gate=mods[2], name="nat_out"))
    return new_x, extra


def _mixer_lru(xs, streams, mods, state, w_in, g_mix, conv_w, conv_b, w_a, b_a, w_i, b_i, lam, w_out):
    c = conv_w.shape[1]
    block = w_a.shape[-1]
    w_in, w_out = w_in.astype(BF16), w_out.astype(BF16)
    starts, kb = _band_plan(c, block)
    wa = [_band_weights(0.5 * w_a[d], c, block, starts, kb) for d in range(2)]
    wi = [_band_weights(0.5 * w_i[d], c, block, starts, kb) for d in range(2)]
    new_x, st_out = [], None
    for x, st in zip(xs, streams):
        latent = not st.shared
        h0 = state.astype(F32) if latent else jnp.zeros((st.nb, 2, c), F32)
        u = _proj(x, w_in, st, norm_g=g_mix, mod=(mods[0], mods[1]), out_time_major=True, name="lru_in")
        hs_f, t_f = _lru_pass(u, st, conv_w, conv_b, wa[0], wi[0], b_a[0], b_i[0], lam[0], h0[:, 0], starts, kb,
                              reverse=False)
        y, t_b = _lru_pass(u, st, conv_w, conv_b, wa[1], wi[1], b_a[1], b_i[1], lam[1], h0[:, 1], starts, kb,
                           reverse=True, hs_fwd=hs_f)
        if not latent:
            st_out = jnp.stack([t_f, t_b], axis=1)
        new_x.append(_proj(y.reshape(st.seq, st.nb * c), w_out, st, res=x, gate=mods[2], x_time_major=True,
                           name="lru_out"))
    return new_x, st_out


def _mixer_mla(xs, streams, mods, cache_ckv, cache_kr, w_down, g_mix, g_qa, g_kva, w_uq, w_ukv, g_q, g_k, w_o):
    d_model = w_down.shape[0]
    q_rank, kv_rank = g_qa.shape[0], g_kva.shape[0]
    qk_dim = g_q.shape[0]
    heads = w_uq.shape[1] // qk_dim
    rope = w_down.shape[1] - q_rank - kv_rank
    nope = qk_dim - rope
    assert nope == LANE and rope <= LANE and kv_rank % LANE == 0 and q_rank % LANE == 0
    head_w = 2 * LANE
    q_pad = -q_rank % kv_rank
    kv_col = q_rank + q_pad
    tail_pad = -(kv_col + kv_rank + rope) % 512
    w_dn = jnp.concatenate([w_down[:, :q_rank], jnp.zeros((d_model, q_pad), F32),
                            w_down[:, q_rank:q_rank + kv_rank], w_down[:, q_rank + kv_rank:],
                            jnp.zeros((d_model, tail_pad), F32)], axis=1).astype(BF16)
    kr_blk = (kv_col + kv_rank) // LANE
    w_q = jnp.pad(w_uq.reshape(q_rank, heads, qk_dim), ((0, 0), (0, 0), (0, head_w - qk_dim)))
    w_q = w_q.reshape(q_rank, heads * head_w).astype(BF16)
    w_ukv, w_o = w_ukv.astype(BF16), w_o.astype(BF16)
    gq = jnp.tile(jnp.pad(g_q * (qk_dim ** -0.5 * LOG2E), (0, head_w - qk_dim)), heads)[None]
    g1, g2 = g_k[None, :nope], jnp.pad(g_k[nope:], (0, LANE - rope))[None]
    p = cache_ckv.shape[1]
    new_x, extra = [], None
    for x, st in zip(xs, streams):
        latent = not st.shared
        d = _proj(x, w_dn, st, norm_g=g_mix, mod=(mods[0], mods[1]), name="mla_down")
        q_tabs = _rope_tables(st.seq, rope, nope, head_w) if latent else None
        k_tabs = _rope_tables(st.seq, rope, 0, LANE) if latent else None
        q = _proj(d, w_q, st, x_block=(q_rank, 0), norm_g=g_qa[None],
                  heads=dict(head_w=head_w, norm_div=qk_dim, norm_cols=heads * head_w, gains=gq, tabs=q_tabs,
                             rope_tiles=(False, True)),
                  out_dtype=BF16, name="mla_uq")
        kv = _mla_kv(d, (kv_rank, kv_col // kv_rank), w_ukv, d, (LANE, kr_blk), g_kva[None], g1, g2, k_tabs, st,
                     n_heads=heads, norm_div=qk_dim, emit_xn=not latent, name="mla_ukv")
        if latent:
            k, v = kv
            cst = _Stream(st.nb, p, 0, True)
            krc = jnp.pad(cache_kr.reshape(st.nb * p, rope), ((0, 0), (0, LANE - rope)))
            kc, vc = _mla_kv(cache_ckv.reshape(st.nb * p, kv_rank), (kv_rank, 0), w_ukv, krc, (LANE, 0), None,
                             g1, g2, None, cst, n_heads=heads, norm_div=qk_dim, emit_xn=False,
                             name="mla_ukv_cache")
            o = _joint_dense_attention(q, k, v, kc, vc, st, p, n_heads=heads, dq=head_w, dv=LANE)
        else:
            k, v, ckv = kv
            o = _ctx_attention(q, k, v, st, n_heads=heads, dq=head_w, dv=LANE, q_col=0, k_col=0, v_col=0)
            kr_out = d[:, kv_col + kv_rank:kv_col + kv_rank + rope]
            extra = (ckv.reshape(st.nb, st.seq, kv_rank), kr_out.reshape(st.nb, st.seq, rope))
        new_x.append(_proj(o, w_o, st, res=x, gate=mods[2], name="mla_out"))
    return new_x, extra


def _mixer_swa(xs, streams, mods, cache_k, cache_v, j, w_qkv, g_mix, g_q, g_k, sinks, w_o):
    dh = g_q.shape[0]
    heads = sinks.shape[0]
    kvh = (w_qkv.shape[1] // dh - heads) // 2
    w_qkv, w_o = w_qkv.astype(BF16), w_o.astype(BF16)
    gains = jnp.concatenate([jnp.tile(g_q * (dh ** -0.5 * LOG2E), heads), jnp.tile(g_k, kvh),
                             jnp.ones((kvh * dh,), F32)])[None]
    sinks = sinks.astype(F32) * LOG2E
    new_x, extra = [], None
    for x, st in zip(xs, streams):
        latent = not st.shared
        tabs = _rope_tables(st.seq, dh, 0, dh) if latent else None
        if tabs is not None:
            tabs = tuple(jnp.tile(t, (1, LANE // dh)) for t in tabs)
        spec = dict(head_w=dh, norm_div=dh, norm_cols=(heads + kvh) * dh, gains=gains, tabs=tabs,
                    rope_tiles=(True,))
        qkv = _proj(x, w_qkv, st, norm_g=g_mix, mod=(mods[0], mods[1]), heads=spec,
                    out_dtype=BF16 if latent else F32, name="swa_qkv")
        if latent:
            o = _swa_attention(qkv, st, sinks, heads=heads, kvh=kvh, dh=dh, cache=(cache_k, cache_v, j))
        else:
            o, kc, vc = _swa_attention(qkv, st, sinks, heads=heads, kvh=kvh, dh=dh)
            extra = (kc, vc)
        new_x.append(_proj(o, w_o, st, res=x, gate=mods[2], name="swa_out"))
    return new_x, extra


def kernel(x_prompt, x_sample, cache_nat_k, cache_nat_v, state_lru, cache_mla_ckv, cache_mla_krope, cache_swa_k, cache_swa_v, c, c_ctx, norm_mix, norm_ffn, w_mod, b_mod, ffn_w_in, ffn_conv_w, ffn_conv_b, ffn_w_out, nat_w_qkv, nat_q_norm, nat_k_norm, nat_rpb, nat_w_o, lru_w_in, lru_conv_w, lru_conv_b, lru_w_a, lru_b_a, lru_w_i, lru_b_i, lru_lambda, lru_w_out, mla_w_down, mla_q_a_norm, mla_kv_a_norm, mla_w_uq, mla_w_ukv, mla_q_norm, mla_k_norm, mla_w_o, swa_w_qkv, swa_q_norm, swa_k_norm, swa_sinks, swa_w_o):
    bc, sc, d = x_prompt.shape
    bl, n, _ = x_sample.shape
    depth = w_mod.shape[0]
    streams = (_Stream(bc, sc, 0, True), _Stream(bl, n, 1, False))
    xs = [x_prompt.reshape(bc * sc, d), x_sample.reshape(bl * n, d)]

    n_cond = 1 + bl
    cond_rows = -(-n_cond // SUBLANE) * SUBLANE
    cond = jnp.zeros((cond_rows, d), F32).at[0].set(c_ctx).at[1:n_cond].set(c)
    mods = _modulation(cond, w_mod, b_mod)[:, :n_cond]

    nat_k_l, nat_v_l, lru_l, ckv_l, krope_l, swa_k_l, swa_v_l = [], [], [], [], [], [], []
    for l in range(depth):
        kind, j = l % 4, l // 4
        m6 = [mods[l, :, None, t * d:(t + 1) * d] for t in range(6)]
        g_mix = norm_mix[l].reshape(1, d)
        if kind == 0:
            xs, (kc, vc) = _mixer_nat(xs, streams, m6, cache_nat_k, cache_nat_v, j, nat_w_qkv[j], g_mix,
                                      nat_q_norm[j], nat_k_norm[j], nat_rpb[j], nat_w_o[j])
            nat_k_l.append(kc)
            nat_v_l.append(vc)
        elif kind == 1:
            xs, st = _mixer_lru(xs, streams, m6, state_lru[:, j], lru_w_in[j], g_mix, lru_conv_w[j], lru_conv_b[j],
                                lru_w_a[j], lru_b_a[j], lru_w_i[j], lru_b_i[j], lru_lambda[j], lru_w_out[j])
            lru_l.append(st)
        elif kind == 2:
            xs, (ckv, kr) = _mixer_mla(xs, streams, m6, cache_mla_ckv[:, j], cache_mla_krope[:, j], mla_w_down[j],
                                       g_mix, mla_q_a_norm[j], mla_kv_a_norm[j], mla_w_uq[j], mla_w_ukv[j],
                                       mla_q_norm[j], mla_k_norm[j], mla_w_o[j])
            ckv_l.append(ckv)
            krope_l.append(kr)
        else:
            xs, (kc, vc) = _mixer_swa(xs, streams, m6, cache_swa_k, cache_swa_v, j, swa_w_qkv[j], g_mix,
                                      swa_q_norm[j], swa_k_norm[j], swa_sinks[j], swa_w_o[j])
            swa_k_l.append(kc)
            swa_v_l.append(vc)
        w_in, w_out = ffn_w_in[l].astype(BF16), ffn_w_out[l].astype(BF16)
        xs = [_conv_ffn(x, st, norm_ffn[l].reshape(1, d), m6[3], m6[4], m6[5], w_in, ffn_conv_w[l],
                        ffn_conv_b[l], w_out) for x, st in zip(xs, streams)]

    return (xs[0].reshape(bc, sc, d), xs[1].reshape(bl, n, d), jnp.stack(nat_k_l, axis=1),
            jnp.stack(nat_v_l, axis=1), jnp.stack(lru_l, axis=1), jnp.stack(ckv_l, axis=1),
            jnp.stack(krope_l, axis=1), jnp.stack(swa_k_l, axis=1), jnp.stack(swa_v_l, axis=1))
```

```python
import functools

import numpy as np
import jax
import jax.numpy as jnp
from jax import lax
from jax.experimental import pallas as pl
from jax.experimental.pallas import tpu as pltpu

F32 = jnp.float32
BF16 = jnp.bfloat16

GRID_W = 64
NA_WIN_ROWS = 8
NA_WIN_COLS = 16
NA_Q_ROWS = 8
NA_K_ROWS = 16
LRU_C = 8.0
SWA_WINDOW = 128
SWA_BLOCK = 128
ROPE_BASE = 10000.0
ROPE_GROUP = 32
EPS = 1e-6
NEG = -1e30
LOG2E = float(np.log2(np.e))
LANE = 128
SUBLANE = 8
HALO = 16
MIB = 1024 * 1024
ROW_TILES = (1024, 512, 256, 128, 64, 32, 16)


def _cparams(n_axes, vmem_mib):
    return pltpu.CompilerParams(dimension_semantics=("arbitrary",) * n_axes,
                                vmem_limit_bytes=int(min(vmem_mib, 60) * MIB))


def _largest_divisor(n, candidates):
    for c in candidates:
        if n % c == 0:
            return c
    return n


class _Stream:
    def __init__(self, nb, seq, mod0, shared_mod):
        self.nb, self.seq, self.rows, self.mod0, self.shared = nb, seq, nb * seq, mod0, shared_mod
        self.bm = _largest_divisor(self.rows if shared_mod else seq, ROW_TILES)

    def mod_index(self, row0):
        return self.mod0 if self.shared else self.mod0 + row0 // self.seq


def _norm_mod(x, g, shift, scale):
    ms = jnp.mean(x * x, axis=-1, keepdims=True)
    return (x * lax.rsqrt(ms + EPS)) * (g * (1.0 + scale)) + shift


def _rms(x, g):
    return (x * lax.rsqrt(jnp.mean(x * x, axis=-1, keepdims=True) + EPS)) * g


def _modulation_kernel(c_ref, w_ref, b_ref, o_ref):
    c = c_ref[...]
    sc = (c * jax.nn.sigmoid(c)).astype(BF16)
    o_ref[...] = jnp.dot(sc, w_ref[...].astype(BF16), preferred_element_type=F32) + b_ref[...]


def _modulation(cond, w_mod, b_mod):
    depth, d, n = w_mod.shape
    rows = cond.shape[0]
    bn = _largest_divisor(n, (512, 256, 128))
    return pl.pallas_call(
        _modulation_kernel,
        grid=(depth, n // bn),
        in_specs=[pl.BlockSpec((rows, d), lambda l, j: (0, 0)),
                  pl.BlockSpec((None, d, bn), lambda l, j: (l, 0, j)),
                  pl.BlockSpec((None, 1, bn), lambda l, j: (l, 0, j))],
        out_specs=pl.BlockSpec((None, rows, bn), lambda l, j: (l, 0, j)),
        out_shape=jax.ShapeDtypeStruct((depth, rows, n), F32),
        compiler_params=_cparams(2, 32),
        name="modulation",
    )(cond, w_mod, b_mod.reshape(depth, 1, n))


def _rope_tables(n_tokens, rot_dim, lead, width):
    t = jnp.arange(n_tokens)
    row = (t // GRID_W).astype(F32)
    col = (t % GRID_W).astype(F32)
    half = rot_dim // 2
    inv = ROPE_BASE ** (-jnp.arange(0, half, 2, dtype=F32) / half)
    ar = row[:, None] * inv
    ac = col[:, None] * inv
    ang = jnp.concatenate([ar, ar, ac, ac], axis=-1)
    cos, sin = jnp.cos(ang), jnp.sin(ang)
    first = (np.arange(rot_dim) % ROPE_GROUP) < ROPE_GROUP // 2
    sin_a = jnp.where(first, -sin, 0.0)
    sin_b = jnp.where(first, 0.0, sin)
    pad = ((0, 0), (lead, width - lead - rot_dim))
    return (jnp.pad(cos, pad, constant_values=1.0), jnp.pad(sin_a, pad), jnp.pad(sin_b, pad))


def _rope_apply(y, cos, sin_a, sin_b):
    shift = ROPE_GROUP // 2
    return y * cos + pltpu.roll(y, LANE - shift, 1) * sin_a + pltpu.roll(y, shift, 1) * sin_b


def _rotate_half_matrix():
    shift = ROPE_GROUP // 2
    src = lax.broadcasted_iota(jnp.int32, (LANE, LANE), 0)
    dst = lax.broadcasted_iota(jnp.int32, (LANE, LANE), 1)
    first = (dst % ROPE_GROUP) < shift
    return jnp.where(first & (src == dst + shift), -1.0,
                     jnp.where(jnp.logical_not(first) & (src == dst - shift), 1.0, 0.0)).astype(BF16)


def _rope_apply_mxu(y, cos, sin_a, sin_b, perm):
    hi = y.astype(BF16)
    lo = (y - hi.astype(F32)).astype(BF16)
    rot = jnp.dot(hi, perm, preferred_element_type=F32) + jnp.dot(lo, perm, preferred_element_type=F32)
    return y * cos + rot * (sin_b - sin_a)


def _fill_lhs(x_ref, xs_ref, xn_ref, prologue, g_ref, sh_ref, sc_ref, row_chunk):
    bm = x_ref.shape[0]

    def chunk(r, carry):
        rows = pl.ds(pl.multiple_of(r * row_chunk, row_chunk), row_chunk)
        x = x_ref[rows, :].astype(F32)
        if prologue == "norm_mod":
            x = _norm_mod(x, g_ref[...], sh_ref[...], sc_ref[...])
        elif prologue == "norm":
            x = _rms(x, g_ref[...])
        if xn_ref is not None:
            xn_ref[rows, :] = x
        xs_ref[rows, :] = x.astype(BF16)
        return carry
    n_chunks = bm // row_chunk
    lax.fori_loop(0, n_chunks, chunk, 0, unroll=2 if n_chunks % 2 == 0 else 1)


def _head_norm_store(acc, o_ref, hg_ref, tabs, head_w, norm_div, col0, norm_cols, rope_tiles):
    bn = acc.shape[1]
    period = tabs[0].shape[1] if tabs is not None else LANE
    perm = _rotate_half_matrix() if tabs is not None and head_w >= LANE else None
    if head_w < LANE:
        head_of_row = lax.broadcasted_iota(jnp.int32, (LANE, LANE), 0) // head_w
        head_of_col = lax.broadcasted_iota(jnp.int32, (LANE, LANE), 1) // head_w
        same_head = jnp.where(head_of_row == head_of_col, 1.0, 0.0).astype(BF16)
    for s0 in range(0, bn, max(head_w, LANE)):
        normed = None if norm_cols is None else (col0 + s0 < norm_cols)
        tiles = [acc[:, s0 + k * LANE:s0 + (k + 1) * LANE] for k in range(max(head_w, LANE) // LANE)]
        if head_w >= LANE:
            sq = None
            for y in tiles:
                sq = y * y if sq is None else sq + y * y
            inv = lax.rsqrt(jnp.sum(sq, axis=-1, keepdims=True) / norm_div + EPS)
        else:
            y2 = tiles[0] * tiles[0]
            hi = y2.astype(BF16)
            lo = (y2 - hi.astype(F32)).astype(BF16)
            ssq = (jnp.dot(hi, same_head, preferred_element_type=F32)
                   + jnp.dot(lo, same_head, preferred_element_type=F32))
            inv = lax.rsqrt(ssq / norm_div + EPS)
        if normed is not None:
            inv = jnp.where(normed, inv, 1.0)
        for k, y in enumerate(tiles):
            c0 = s0 + k * LANE
            y = (y * inv) * hg_ref[:, c0:c0 + LANE]
            t0 = c0 % period
            if tabs is not None and rope_tiles[t0 // LANE]:
                tab = tuple(t[:, t0:t0 + LANE] for t in tabs)
                rotated = _rope_apply(y, *tab) if head_w < LANE else _rope_apply_mxu(y, *tab, perm)
                y = rotated if normed is None else jnp.where(normed, rotated, y)
            o_ref[:, c0:c0 + LANE] = y.astype(o_ref.dtype)


def _proj_kernel(*refs, prologue, emit_xn, epilogue, head_w, norm_div, norm_cols, rope, rope_tiles, row_chunk):
    it = iter(refs)
    x_ref = next(it)
    g_ref = next(it) if prologue is not None else None
    sh_ref, sc_ref = (next(it), next(it)) if prologue == "norm_mod" else (None, None)
    w_ref = next(it)
    if epilogue == "res":
        res_ref, gate_ref = next(it), next(it)
    if epilogue == "heads":
        hg_ref = next(it)
        tabs = (next(it), next(it), next(it)) if rope else None
    o_ref = next(it)
    xn_ref = next(it) if emit_xn else None
    xs_ref = next(it, None)
    j = pl.program_id(1)
    bn = o_ref.shape[1]

    if xs_ref is None:
        xs_ref = x_ref
    else:
        @pl.when(j == 0)
        def _():
            _fill_lhs(x_ref, xs_ref, xn_ref, prologue, g_ref, sh_ref, sc_ref, row_chunk)

    acc = jnp.dot(xs_ref[...], w_ref[...], preferred_element_type=F32)
    if epilogue == "res":
        o_ref[...] = res_ref[...] + gate_ref[...] * acc
    elif epilogue == "heads":
        _head_norm_store(acc, o_ref, hg_ref, tabs, head_w, norm_div, j * bn, norm_cols, rope_tiles)
    else:
        o_ref[...] = acc.astype(o_ref.dtype)


def _proj(x, w, st, *, x_block=None, norm_g=None, mod=None, res=None, gate=None, heads=None, emit_xn=False,
          out_dtype=F32, bn=None, x_time_major=False, out_time_major=False, name="proj"):
    k, n = w.shape
    time_major = x_time_major or out_time_major
    bm = min(st.bm, st.seq) if time_major else st.bm
    tiles_per_seq = st.seq // bm if st.seq % bm == 0 else None
    if time_major:
        assert tiles_per_seq is not None and x_block is None
    if x_time_major:
        assert x.shape == (st.seq, st.nb * k)
        kidx = 0
    else:
        kx, kidx = x_block if x_block is not None else (x.shape[1], 0)
        assert kx == k and x.shape[0] == st.rows
    rows = st.rows
    prologue = None if norm_g is None else ("norm_mod" if mod is not None else "norm")
    epilogue = "res" if res is not None else ("heads" if heads is not None else None)
    rope = heads is not None and heads.get("tabs") is not None
    if bn is None:
        unit = LANE
        if epilogue == "heads":
            unit = max(heads["head_w"], LANE, heads["tabs"][0].shape[1] if rope else LANE)
        cap = 512 if epilogue == "res" else 1024
        bn = next((c for c in range(cap, unit - 1, -unit) if n % c == 0), n)
    mod_idx = lambda i: st.mod_index(i * bm)

    if x_time_major:
        in_specs = [pl.BlockSpec((bm, k), lambda i, j: (i % tiles_per_seq, i // tiles_per_seq))]
    else:
        in_specs = [pl.BlockSpec((bm, k), lambda i, j: (i, kidx))]
    args = [x]
    if prologue is not None:
        in_specs.append(pl.BlockSpec((1, k), lambda i, j: (0, 0)))
        args.append(norm_g)
    if prologue == "norm_mod":
        in_specs += [pl.BlockSpec((None, 1, k), lambda i, j: (mod_idx(i), 0, 0))] * 2
        args += list(mod)
    in_specs.append(pl.BlockSpec((k, bn), lambda i, j: (0, j)))
    args.append(w.astype(BF16))
    if epilogue == "res":
        in_specs += [pl.BlockSpec((bm, bn), lambda i, j: (i, j)),
                     pl.BlockSpec((None, 1, bn), lambda i, j: (mod_idx(i), 0, j))]
        args += [res, gate]
    head_w = norm_div = 0
    norm_cols = rope_tiles = None
    if epilogue == "heads":
        head_w, norm_div = heads["head_w"], heads["norm_div"]
        norm_cols = heads["norm_cols"] if heads["norm_cols"] < n else None
        assert bn % max(head_w, LANE) == 0 and heads["norm_cols"] % max(head_w, LANE) == 0
        in_specs.append(pl.BlockSpec((1, bn), lambda i, j: (0, j)))
        args.append(heads["gains"])
        if rope:
            period = heads["tabs"][0].shape[1]
            rope_tiles = heads["rope_tiles"]
            assert bn % period == 0 and tiles_per_seq is not None and len(rope_tiles) == period // LANE
            in_specs += [pl.BlockSpec((bm, period), lambda i, j: (i % tiles_per_seq, 0))] * 3
            args += list(heads["tabs"])
    if out_time_major:
        n_col_tiles = n // bn
        out_shape = [jax.ShapeDtypeStruct((st.seq, st.nb * n), out_dtype)]
        out_specs = [pl.BlockSpec((bm, bn), lambda i, j: (i % tiles_per_seq, (i // tiles_per_seq) * n_col_tiles + j))]
    else:
        out_shape = [jax.ShapeDtypeStruct((rows, n), out_dtype)]
        out_specs = [pl.BlockSpec((bm, bn), lambda i, j: (i, j))]
    if emit_xn:
        out_shape.append(jax.ShapeDtypeStruct((rows, k), F32))
        out_specs.append(pl.BlockSpec((bm, k), lambda i, j: (i, 0)))
    xbytes = x.dtype.itemsize
    vmem = (2 * bm * k * xbytes + bm * k * 2 + 2 * k * bn * 2 + (6 if epilogue == "res" else 4) * bm * bn * 4
            + (2 * bm * k * 4 if emit_xn else 0)) / MIB + 8
    kern = functools.partial(_proj_kernel, prologue=prologue, emit_xn=emit_xn, epilogue=epilogue, head_w=head_w,
                             norm_div=norm_div, norm_cols=norm_cols, rope=rope, rope_tiles=rope_tiles,
                             row_chunk=min(bm, 128))
    direct_lhs = prologue is None and x.dtype == BF16 and not emit_xn
    out = pl.pallas_call(
        kern,
        grid=(rows // bm, n // bn),
        in_specs=in_specs,
        out_specs=out_specs,
        out_shape=out_shape,
        scratch_shapes=[] if direct_lhs else [pltpu.VMEM((bm, k), BF16)],
        compiler_params=_cparams(2, vmem),
        name=name,
    )(*args)
    return out if emit_xn else out[0]


def _ffn_kernel(xp_ref, x_ref, xn_ref, g_ref, sh_ref, sc_ref, gate_ref, wa_ref, wb_ref, cw_ref, cb_ref,
                wo_ref, o_ref, h_ref, *, bm, seq, row_chunk):
    i = pl.program_id(0)
    c = pl.program_id(1)
    n_chunks = pl.num_programs(1)

    @pl.when(c == 0)
    def _():
        g, sh, sc = g_ref[...], sh_ref[...], sc_ref[...]
        h_ref[0:HALO, :] = _norm_mod(xp_ref[...], g, sh, sc).astype(BF16)
        h_ref[HALO + bm:, :] = _norm_mod(xn_ref[...], g, sh, sc).astype(BF16)

        def chunk(r, carry):
            src = pl.ds(pl.multiple_of(r * row_chunk, row_chunk), row_chunk)
            dst = pl.ds(pl.multiple_of(HALO + r * row_chunk, HALO), row_chunk)
            h_ref[dst, :] = _norm_mod(x_ref[src, :], g, sh, sc).astype(BF16)
            return carry
        n_row_chunks = bm // row_chunk
        lax.fori_loop(0, n_row_chunks, chunk, 0, unroll=2 if n_row_chunks % 2 == 0 else 1)
        o_ref[...] = jnp.zeros_like(o_ref)

    ua = jnp.dot(h_ref[...], wa_ref[...], preferred_element_type=F32)
    ub = jnp.dot(h_ref[HALO:HALO + bm, :], wb_ref[...], preferred_element_type=F32)
    n_all = bm + 2 * HALO
    u_prev = pltpu.roll(ua, 1, 0)[HALO:HALO + bm]
    u_next = pltpu.roll(ua, n_all - 1, 0)[HALO:HALO + bm]
    u_mid = ua[HALO:HALO + bm]
    pos = jnp.bitwise_and(i * bm + lax.broadcasted_iota(jnp.int32, (bm, 1), 0), seq - 1)
    u_prev = jnp.where(pos == 0, 0.0, u_prev)
    u_next = jnp.where(pos == seq - 1, 0.0, u_next)
    cw = cw_ref[...]
    a = cb_ref[...] + u_prev * cw[0:1] + u_mid * cw[1:2] + u_next * cw[2:3]
    gated = ((a * jax.nn.sigmoid(a)) * ub).astype(BF16)
    o_ref[...] += jnp.dot(gated, wo_ref[...], preferred_element_type=F32)

    @pl.when(c == n_chunks - 1)
    def _():
        o_ref[...] = x_ref[...] + gate_ref[...] * o_ref[...]


def _conv_ffn(x, st, g, shift, scale, gate, w_in, conv_w, conv_b, w_out, bm=1024, ck=512):
    m, d = x.shape
    d_ff = w_out.shape[0]
    bm = min(st.bm, bm)
    ck = _largest_divisor(d_ff, tuple(c for c in (512, 256, 128) if c <= ck))
    n_chunks = d_ff // ck
    n_halo_blocks = m // HALO
    assert st.seq & (st.seq - 1) == 0 and conv_w.shape[0] == 3
    mod_idx = lambda i: st.mod_index(i * bm)
    kern = functools.partial(_ffn_kernel, bm=bm, seq=st.seq, row_chunk=min(bm, 128))
    vmem = (4 * bm * d * 4 + (bm + 2 * HALO) * d * 2 + 6 * d * ck * 2 + 5 * (bm + 2 * HALO) * ck * 4) / MIB + 4
    return pl.pallas_call(
        kern,
        grid=(m // bm, n_chunks),
        in_specs=[
            pl.BlockSpec((HALO, d), lambda i, c: (jnp.maximum(i * (bm // HALO) - 1, 0), 0)),
            pl.BlockSpec((bm, d), lambda i, c: (i, 0)),
            pl.BlockSpec((HALO, d), lambda i, c: (jnp.minimum((i + 1) * (bm // HALO), n_halo_blocks - 1), 0)),
            pl.BlockSpec((1, d), lambda i, c: (0, 0)),
            pl.BlockSpec((None, 1, d), lambda i, c: (mod_idx(i), 0, 0)),
            pl.BlockSpec((None, 1, d), lambda i, c: (mod_idx(i), 0, 0)),
            pl.BlockSpec((None, 1, d), lambda i, c: (mod_idx(i), 0, 0)),
            pl.BlockSpec((d, ck), lambda i, c: (0, c)),
            pl.BlockSpec((d, ck), lambda i, c: (0, n_chunks + c)),
            pl.BlockSpec((conv_w.shape[0], ck), lambda i, c: (0, c)),
            pl.BlockSpec((1, ck), lambda i, c: (0, c)),
            pl.BlockSpec((ck, d), lambda i, c: (c, 0)),
        ],
        out_specs=pl.BlockSpec((bm, d), lambda i, c: (i, 0)),
        out_shape=jax.ShapeDtypeStruct((m, d), F32),
        scratch_shapes=[pltpu.VMEM((bm + 2 * HALO, d), BF16)],
        compiler_params=_cparams(2, vmem),
        name="conv_ffn",
    )(x, x, x, g, shift, scale, gate, w_in, w_in, conv_w, conv_b.reshape(1, d_ff), w_out)


def _qk(q, k):
    return lax.dot_general(q, k, (((1,), (1,)), ((), ())), preferred_element_type=F32)


def _attend(scores, values, sink=None):
    m = None
    for s in scores:
        mi = jnp.max(s, axis=-1, keepdims=True)
        m = mi if m is None else jnp.maximum(m, mi)
    if sink is not None:
        m = jnp.maximum(m, sink)
    es = [jnp.exp2(s - m) for s in scores]
    den = None
    for e in es:
        di = jnp.sum(e, axis=-1, keepdims=True)
        den = di if den is None else den + di
    if sink is not None:
        den = den + jnp.exp2(sink - m)
    out = None
    for e, v in zip(es, values):
        oi = jnp.dot(e.astype(BF16), v, preferred_element_type=F32)
        out = oi if out is None else out + oi
    return out * (1.0 / den)


def _ctx_attn_kernel(q_ref, k_ref, v_ref, *outs, heads, dq, dv, emit_kv):
    o_ref = outs[0]
    for h in range(heads):
        q = q_ref[:, h * dq:(h + 1) * dq].astype(BF16)
        k = k_ref[:, h * dq:(h + 1) * dq]
        v = v_ref[:, h * dv:(h + 1) * dv]
        if emit_kv:
            outs[1][h] = k.astype(F32)
            outs[2][h] = v.astype(F32)
        o = _attend([_qk(q, k.astype(BF16))], [v.astype(BF16)])
        o_ref[:, h * dv:(h + 1) * dv] = o.astype(o_ref.dtype)


def _ctx_attention(qm, km, vm, st, *, n_heads, dq, dv, q_col, k_col, v_col, emit_kv=False):
    hb = _largest_divisor(n_heads, (4, 2, 1))
    s = st.seq
    assert q_col % (hb * dq) == 0 and k_col % (hb * dq) == 0 and v_col % (hb * dv) == 0
    qo, ko, vo = q_col // (hb * dq), k_col // (hb * dq), v_col // (hb * dv)
    out_shape = [jax.ShapeDtypeStruct((st.rows, n_heads * dv), BF16)]
    out_specs = [pl.BlockSpec((s, hb * dv), lambda b, g: (b, g))]
    if emit_kv:
        out_shape += [jax.ShapeDtypeStruct((st.nb, n_heads, s, dq), F32),
                      jax.ShapeDtypeStruct((st.nb, n_heads, s, dv), F32)]
        out_specs += [pl.BlockSpec((None, hb, s, dq), lambda b, g: (b, g, 0, 0)),
                      pl.BlockSpec((None, hb, s, dv), lambda b, g: (b, g, 0, 0))]
    out = pl.pallas_call(
        functools.partial(_ctx_attn_kernel, heads=hb, dq=dq, dv=dv, emit_kv=emit_kv),
        grid=(st.nb, n_heads // hb),
        in_specs=[pl.BlockSpec((s, hb * dq), lambda b, g: (b, qo + g)),
                  pl.BlockSpec((s, hb * dq), lambda b, g: (b, ko + g)),
                  pl.BlockSpec((s, hb * dv), lambda b, g: (b, vo + g))],
        out_specs=out_specs,
        out_shape=out_shape,
        compiler_params=_cparams(2, 32),
        name="ctx_attention",
    )(qm, km, vm)
    return out if emit_kv else out[0]


def _nat_kernel(q_ref, k_ref, v_ref, kc_ref, vc_ref, bias_ref, o_ref, *, key_rows, rows, heads, dh):
    i = pl.program_id(2)
    n_keys = key_rows * GRID_W
    first_row = jnp.clip(i * NA_Q_ROWS - NA_WIN_ROWS // 2, 0, rows - key_rows)
    start = pl.multiple_of(first_row * GRID_W, GRID_W * 4)
    for h in range(heads):
        lanes = slice(h * dh, (h + 1) * dh)
        q = q_ref[:, lanes]
        k = k_ref[pl.ds(start, n_keys), lanes]
        v = v_ref[pl.ds(start, n_keys), lanes]
        s_loc = _qk(q, k) + bias_ref[h]
        s_ctx = _qk(q, kc_ref[h].astype(BF16))
        o_ref[:, lanes] = _attend([s_loc, s_ctx], [v, vc_ref[h].astype(BF16)]).astype(o_ref.dtype)


def _nat_bias(rpb, rows):
    n_blocks = rows // NA_Q_ROWS
    key_rows = min(NA_K_ROWS, rows)
    wr = min(NA_WIN_ROWS, rows)
    reps = [0, min(1, n_blocks - 1), n_blocks - 1]
    heads = rpb.shape[0]
    nq, nk = NA_Q_ROWS * GRID_W, key_rows * GRID_W
    shape = (NA_Q_ROWS, GRID_W, key_rows, GRID_W)
    qc = np.arange(GRID_W)
    cstart = np.clip(qc - NA_WIN_COLS // 2, 0, GRID_W - NA_WIN_COLS)
    col_ok = (qc[None, :] >= cstart[:, None]) & (qc[None, :] < cstart[:, None] + NA_WIN_COLS)
    rp = jnp.pad(rpb.astype(F32) * LOG2E,
                 ((0, 0), (key_rows, key_rows), (GRID_W - NA_WIN_COLS, GRID_W - NA_WIN_COLS)))
    row_slabs, mask_l = [], []
    for i in reps:
        ks = int(np.clip(i * NA_Q_ROWS - NA_WIN_ROWS // 2, 0, rows - key_rows))
        r = i * NA_Q_ROWS + np.arange(NA_Q_ROWS)
        rs = np.clip(r - wr // 2, 0, rows - wr)
        kr = ks + np.arange(key_rows)
        row_ok = (kr[None, :] >= rs[:, None]) & (kr[None, :] < rs[:, None] + wr)
        for rq in range(NA_Q_ROWS):
            first = ks - int(r[rq]) + NA_WIN_ROWS - 1 + key_rows
            assert 0 <= first and first + key_rows <= rp.shape[1]
            row_slabs.append(rp[:, first:first + key_rows, :])
        mask_l.append(np.broadcast_to(row_ok[:, None, :, None] & col_ok[None, :, None, :], shape).reshape(nq, nk))
    slab = jnp.stack(row_slabs, axis=1).reshape(heads, len(reps), NA_Q_ROWS, key_rows, 2 * GRID_W - 1)
    toep = jnp.stack([slab[..., GRID_W - 1 - c:2 * GRID_W - 1 - c] for c in range(GRID_W)], axis=3)
    bias = toep.reshape(heads, len(reps), nq, nk)
    return jnp.where(jnp.asarray(np.stack(mask_l))[None], bias, NEG)


def _nat_attention(qkv, st, cache_k, cache_v, j, rpb, dh):
    heads = rpb.shape[0]
    n = st.seq
    p = cache_k.shape[3]
    rows = n // GRID_W
    assert rows % NA_Q_ROWS == 0 and rows >= NA_K_ROWS and dh % LANE == 0
    n_blocks = rows // NA_Q_ROWS
    key_rows = min(NA_K_ROWS, rows)
    nq, nk = NA_Q_ROWS * GRID_W, key_rows * GRID_W
    bias = _nat_bias(rpb, rows)
    btype = lambda i: jnp.where(i == 0, 0, jnp.where(i == n_blocks - 1, 2, 1))
    hb = _largest_divisor(heads, (4, 2, 1))
    hg = heads // hb
    kern = functools.partial(_nat_kernel, key_rows=key_rows, rows=rows, heads=hb, dh=dh)
    return pl.pallas_call(
        kern,
        grid=(st.nb, hg, n_blocks),
        in_specs=[pl.BlockSpec((nq, hb * dh), lambda b, h, i: (b * n_blocks + i, h)),
                  pl.BlockSpec((n, hb * dh), lambda b, h, i: (b, hg + h)),
                  pl.BlockSpec((n, hb * dh), lambda b, h, i: (b, 2 * hg + h)),
                  pl.BlockSpec((None, None, hb, p, dh), lambda b, h, i: (b, j, h, 0, 0)),
                  pl.BlockSpec((None, None, hb, p, dh), lambda b, h, i: (b, j, h, 0, 0)),
                  pl.BlockSpec((hb, None, nq, nk), lambda b, h, i: (h, btype(i), 0, 0))],
        out_specs=pl.BlockSpec((nq, hb * dh), lambda b, h, i: (b * n_blocks + i, h)),
        out_shape=jax.ShapeDtypeStruct((st.rows, heads * dh), BF16),
        compiler_params=_cparams(3, 56),
        name="nat_attention",
    )(qkv, qkv, qkv, cache_k, cache_v, bias)


def _joint_dense_kernel(q_ref, k_ref, v_ref, kc_ref, vc_ref, o_ref, *, chunk):
    q = q_ref[...]
    n = k_ref.shape[0]
    pieces = [(k_ref, v_ref, c0, min(chunk, n - c0)) for c0 in range(0, n, chunk)]
    pieces.append((kc_ref, vc_ref, 0, kc_ref.shape[0]))
    m = den = acc = None
    for kr, vr, c0, size in pieces:
        s = _qk(q, kr[c0:c0 + size, :])
        mc = jnp.max(s, axis=-1, keepdims=True)
        m_new = mc if m is None else jnp.maximum(m, mc)
        e = jnp.exp2(s - m_new)
        dc = jnp.sum(e, axis=-1, keepdims=True)
        pv = jnp.dot(e.astype(BF16), vr[c0:c0 + size, :], preferred_element_type=F32)
        if m is None:
            den, acc = dc, pv
        else:
            alpha = jnp.exp2(m - m_new)
            den, acc = alpha * den + dc, alpha * acc + pv
        m = m_new
    o_ref[...] = (acc * (1.0 / den)).astype(o_ref.dtype)


def _joint_dense_attention(qm, km, vm, kcm, vcm, st, p, *, n_heads, dq, dv):
    n = st.seq
    bq = _largest_divisor(n, (1024, 512, 256, 128, 64, 32, 16))
    nqb = n // bq
    return pl.pallas_call(
        functools.partial(_joint_dense_kernel, chunk=1024),
        grid=(st.nb, n_heads, nqb),
        in_specs=[pl.BlockSpec((bq, dq), lambda b, h, i: (b * nqb + i, h)),
                  pl.BlockSpec((n, dq), lambda b, h, i: (b, h)),
                  pl.BlockSpec((n, dv), lambda b, h, i: (b, h)),
                  pl.BlockSpec((p, dq), lambda b, h, i: (b, h)),
                  pl.BlockSpec((p, dv), lambda b, h, i: (b, h))],
        out_specs=pl.BlockSpec((bq, dv), lambda b, h, i: (b * nqb + i, h)),
        out_shape=jax.ShapeDtypeStruct((st.rows, n_heads * dv), BF16),
        compiler_params=_cparams(3, 48),
        name="mla_attention",
    )(qm, km, vm, kcm, vcm)


def _both_halves(x, s):
    x = x.astype(F32)
    low = lax.broadcasted_iota(jnp.int32, (1, LANE), 1) < LANE // 2
    keep = low if s == 0 else jnp.logical_not(low)
    return jnp.where(keep, x, pltpu.roll(x, LANE // 2, 1)).astype(BF16)


def _swa_step(sinks_ref, pair, q_ref, k, v, kc, vc, o_ref, *, dh, groups, local_bias):
    kv_per_step = LANE // dh
    assert kv_per_step == 2 and groups % 2 == 0
    rows = q_ref.shape[0]
    low = lax.broadcasted_iota(jnp.int32, (1, LANE), 1) < dh
    row_group = lax.broadcasted_iota(jnp.int32, (groups * rows, 1), 0) // rows
    for s in range(kv_per_step):
        kd, vd = _both_halves(k, s), _both_halves(v, s)
        q_parts = []
        for g in range(groups):
            c0 = ((s * groups + g) * dh // LANE) * LANE
            tile = q_ref[:, c0:c0 + LANE].astype(BF16)
            q_parts.append(jnp.where(low if g % 2 == 0 else jnp.logical_not(low), tile, jnp.zeros_like(tile)))
        q = jnp.concatenate(q_parts, axis=0)
        sink = jnp.zeros((groups * rows, 1), F32)
        for g in range(groups):
            sink = jnp.where(row_group == g, sinks_ref[(pair * kv_per_step + s) * groups + g], sink)
        s_loc = _qk(q, kd)
        if local_bias is not None:
            s_loc = s_loc + jnp.concatenate([local_bias] * groups, axis=0)
        if kc is not None:
            kcd = jnp.concatenate([kc[s], kc[s]], axis=-1).astype(BF16)
            vcd = jnp.concatenate([vc[s], vc[s]], axis=-1).astype(BF16)
            out = _attend([s_loc, _qk(q, kcd)], [vd, vcd], sink)
        else:
            out = _attend([s_loc], [vd], sink)
        for g in range(0, groups, 2):
            c0 = (s * groups + g) * dh
            o_ref[:, c0:c0 + LANE] = jnp.where(low, out[g * rows:(g + 1) * rows],
                                               out[(g + 1) * rows:(g + 2) * rows]).astype(o_ref.dtype)


def _swa_ctx_kernel(sinks_ref, q_ref, k_ref, v_ref, o_ref, ko_ref, vo_ref, *, dh, groups):
    pair = pl.program_id(1)
    k, v = k_ref[...], v_ref[...]
    for s in range(LANE // dh):
        ko_ref[s] = k[:, s * dh:(s + 1) * dh].astype(F32)
        vo_ref[s] = v[:, s * dh:(s + 1) * dh].astype(F32)
    _swa_step(sinks_ref, pair, q_ref, k, v, None, None, o_ref, dh=dh, groups=groups, local_bias=None)


def _swa_lat_kernel(sinks_ref, q_ref, k_ref, v_ref, kc_ref, vc_ref, o_ref, *, dh, groups, n):
    pair = pl.program_id(1)
    blk = pl.program_id(2)
    n_keys = min(3 * SWA_BLOCK, n)
    start = pl.multiple_of(jnp.clip((blk - 1) * SWA_BLOCK, 0, n - n_keys), SWA_BLOCK)
    k = k_ref[pl.ds(start, n_keys), :]
    v = v_ref[pl.ds(start, n_keys), :]
    qpos = blk * SWA_BLOCK + lax.broadcasted_iota(jnp.int32, (SWA_BLOCK, 1), 0)
    kpos = start + lax.broadcasted_iota(jnp.int32, (1, n_keys), 1)
    bias = jnp.where(jnp.abs(qpos - kpos) <= SWA_WINDOW, 0.0, NEG)
    _swa_step(sinks_ref, pair, q_ref, k, v, kc_ref, vc_ref, o_ref, dh=dh, groups=groups, local_bias=bias)


def _swa_attention(qkv, st, sinks, *, heads, kvh, dh, cache=None):
    groups = heads // kvh
    kv_per_step = LANE // dh
    assert LANE % dh == 0 and kvh % kv_per_step == 0 and groups % kv_per_step == 0
    pairs = kvh // kv_per_step
    qw = kv_per_step * groups * dh
    k_blk = heads * dh // LANE
    v_blk = (heads + kvh) * dh // LANE
    n = st.seq
    common = dict(dh=dh, groups=groups)
    smem = pl.BlockSpec(memory_space=pltpu.SMEM)
    if cache is None:
        out = pl.pallas_call(
            functools.partial(_swa_ctx_kernel, **common),
            grid=(st.nb, pairs),
            in_specs=[smem,
                      pl.BlockSpec((n, qw), lambda b, c: (b, c)),
                      pl.BlockSpec((n, LANE), lambda b, c: (b, k_blk + c)),
                      pl.BlockSpec((n, LANE), lambda b, c: (b, v_blk + c))],
            out_specs=[pl.BlockSpec((n, qw), lambda b, c: (b, c)),
                       pl.BlockSpec((None, kv_per_step, n, dh), lambda b, c: (b, c, 0, 0)),
                       pl.BlockSpec((None, kv_per_step, n, dh), lambda b, c: (b, c, 0, 0))],
            out_shape=[jax.ShapeDtypeStruct((st.rows, heads * dh), BF16),
                       jax.ShapeDtypeStruct((st.nb, kvh, n, dh), F32),
                       jax.ShapeDtypeStruct((st.nb, kvh, n, dh), F32)],
            compiler_params=_cparams(2, 32),
            name="swa_ctx_attention",
        )(sinks, qkv, qkv, qkv)
        return out
    cache_k, cache_v, j = cache
    p = cache_k.shape[3]
    nblk = n // SWA_BLOCK
    assert n % SWA_BLOCK == 0
    return pl.pallas_call(
        functools.partial(_swa_lat_kernel, n=n, **common),
        grid=(st.nb, pairs, nblk),
        in_specs=[smem,
                  pl.BlockSpec((SWA_BLOCK, qw), lambda b, c, i: (b * nblk + i, c)),
                  pl.BlockSpec((n, LANE), lambda b, c, i: (b, k_blk + c)),
                  pl.BlockSpec((n, LANE), lambda b, c, i: (b, v_blk + c)),
                  pl.BlockSpec((None, None, kv_per_step, p, dh), lambda b, c, i: (b, j, c, 0, 0)),
                  pl.BlockSpec((None, None, kv_per_step, p, dh), lambda b, c, i: (b, j, c, 0, 0))],
        out_specs=pl.BlockSpec((SWA_BLOCK, qw), lambda b, c, i: (b * nblk + i, c)),
        out_shape=jax.ShapeDtypeStruct((st.rows, heads * dh), BF16),
        compiler_params=_cparams(3, 32),
        name="swa_attention",
    )(sinks, qkv, qkv, qkv, cache_k, cache_v)


def _mla_kv_kernel(*refs, norm, emit_xn, rope, row_chunk, norm_div):
    it = iter(refs)
    x_ref = next(it)
    g_ref = next(it) if norm else None
    w_ref, kr_ref, g1_ref, g2_ref = next(it), next(it), next(it), next(it)
    tabs = (next(it), next(it), next(it)) if rope else None
    k_ref, v_ref = next(it), next(it)
    xn_ref = next(it) if emit_xn else None
    xs_ref = next(it)

    @pl.when(pl.program_id(1) == 0)
    def _():
        _fill_lhs(x_ref, xs_ref, xn_ref, "norm" if norm else None, g_ref, None, None, row_chunk)

    acc = jnp.dot(xs_ref[...], w_ref[...], preferred_element_type=F32)
    kr = kr_ref[...]
    kr_ssq = jnp.sum(kr * kr, axis=-1, keepdims=True)
    shared = kr * g2_ref[...]
    if rope:
        shared = _rope_apply(shared, *(t[...] for t in tabs))
    for h in range(acc.shape[1] // (2 * LANE)):
        nope = acc[:, 2 * h * LANE:(2 * h + 1) * LANE]
        inv = lax.rsqrt((jnp.sum(nope * nope, axis=-1, keepdims=True) + kr_ssq) / norm_div + EPS)
        k_ref[:, 2 * h * LANE:(2 * h + 1) * LANE] = ((nope * inv) * g1_ref[...]).astype(k_ref.dtype)
        k_ref[:, (2 * h + 1) * LANE:(2 * h + 2) * LANE] = (shared * inv).astype(k_ref.dtype)
        v_ref[:, h * LANE:(h + 1) * LANE] = acc[:, (2 * h + 1) * LANE:(2 * h + 2) * LANE].astype(v_ref.dtype)


def _mla_kv(x, x_block, w_ukv, kr, kr_block, g_kva, g1, g2, tabs, st, *, n_heads, norm_div, emit_xn, name):
    rows = x.shape[0]
    k, n = w_ukv.shape
    head_n = n // n_heads
    assert head_n == 2 * LANE, "nope and value widths must both be one lane tile"
    kx, kidx = x_block
    krw, kridx = kr_block
    assert kx == k and krw == LANE
    bm = st.bm
    norm = g_kva is not None
    rope = tabs is not None
    in_specs = [pl.BlockSpec((bm, k), lambda i, h: (i, kidx))]
    args = [x]
    if norm:
        in_specs.append(pl.BlockSpec((1, k), lambda i, h: (0, 0)))
        args.append(g_kva)
    hb = _largest_divisor(n_heads, (4, 2, 1))
    in_specs += [pl.BlockSpec((k, hb * head_n), lambda i, h: (0, h)),
                 pl.BlockSpec((bm, LANE), lambda i, h: (i, kridx)),
                 pl.BlockSpec((1, LANE), lambda i, h: (0, 0)),
                 pl.BlockSpec((1, LANE), lambda i, h: (0, 0))]
    args += [w_ukv, kr, g1, g2]
    if rope:
        tiles_per_seq = st.seq // bm
        in_specs += [pl.BlockSpec((bm, LANE), lambda i, h: (i % tiles_per_seq, 0))] * 3
        args += list(tabs)
    out_shape = [jax.ShapeDtypeStruct((rows, n_heads * 2 * LANE), BF16),
                 jax.ShapeDtypeStruct((rows, n_heads * LANE), BF16)]
    out_specs = [pl.BlockSpec((bm, hb * 2 * LANE), lambda i, h: (i, h)),
                 pl.BlockSpec((bm, hb * LANE), lambda i, h: (i, h))]
    if emit_xn:
        out_shape.append(jax.ShapeDtypeStruct((rows, k), F32))
        out_specs.append(pl.BlockSpec((bm, k), lambda i, h: (i, 0)))
    kern = functools.partial(_mla_kv_kernel, norm=norm, emit_xn=emit_xn, rope=rope, row_chunk=min(bm, 128),
                             norm_div=norm_div)
    return pl.pallas_call(
        kern,
        grid=(rows // bm, n_heads // hb),
        in_specs=in_specs,
        out_specs=out_specs,
        out_shape=out_shape,
        scratch_shapes=[pltpu.VMEM((bm, k), BF16)],
        compiler_params=_cparams(2, 40),
        name=name,
    )(*args)


def _band_plan(width, block):
    n_tiles = width // LANE
    lo = [((t * LANE) // block) * block for t in range(n_tiles)]
    hi = [(((t + 1) * LANE - 1) // block + 1) * block for t in range(n_tiles)]
    start = [(l // LANE) * LANE for l in lo]
    kb = max(-(-(h - s) // LANE) * LANE for h, s in zip(hi, start))
    kb = min(kb, width)
    start = [min(s, width - kb) for s in start]
    return start, kb


def _band_weights(w, width, block, start, kb):
    n_tiles = width // LANE
    wb = w.astype(BF16)
    tiles = []
    for t in range(n_tiles):
        pieces = []
        col = t * LANE
        while col < (t + 1) * LANE:
            blk = col // block
            col_end = min((blk + 1) * block, (t + 1) * LANE)
            sub = wb[blk, :, col - blk * block:col_end - blk * block]
            top = blk * block - start[t]
            pieces.append(jnp.pad(sub, ((top, kb - top - block), (0, 0))))
            col = col_end
        tiles.append(jnp.concatenate(pieces, axis=1))
    return jnp.stack(tiles)


def _gelu_tanh(x):
    cdf = 0.5 * (1.0 + jnp.tanh(np.float32(np.sqrt(2.0 / np.pi)) * (x + 0.044715 * (x * x * x))))
    return x * cdf


def _lru_pass_kernel(*refs, reverse, starts, kb, bt, seq, nb, taps):
    left = taps // 2
    right = taps - 1 - left
    it = iter(refs)
    xp_ref, x_ref = next(it), next(it)
    xn_ref = next(it) if right > 0 else None
    cw_ref, cb_ref = next(it), next(it)
    wa_ref, wi_ref, ba_ref, bi_ref, lam_ref, h0_ref = (next(it) for _ in range(6))
    hsf_ref, gate_ref = (next(it), next(it)) if reverse else (None, None)
    out_ref, ht_ref = next(it), next(it)
    xc_s, xb_s, a_s, bx_s, carry = next(it), next(it), next(it), next(it), next(it)

    step = pl.program_id(0)
    n_steps = pl.num_programs(0)
    tile = (n_steps - 1 - step) if reverse else step
    n_tiles = len(starts)
    rows = nb * bt

    @pl.when(step == 0)
    def _():
        carry[...] = h0_ref[...]

    for t in range(n_tiles):
        lanes = slice(t * LANE, (t + 1) * LANE)
        parts = [jnp.where(tile > 0, xp_ref[:, :, lanes], 0.0), x_ref[:, :, lanes]]
        if right > 0:
            parts.append(jnp.where(tile < n_steps - 1, xn_ref[:, :, lanes], 0.0))
        full = jnp.concatenate(parts, axis=0)
        acc = jnp.broadcast_to(cb_ref[:, lanes], (bt, nb, LANE))
        for k in range(taps):
            acc = acc + full[k:k + bt] * cw_ref[k:k + 1, lanes]
        acc = acc.reshape(rows, LANE)
        xc_s[:, lanes] = acc
        xb_s[:, lanes] = acc.astype(BF16)

    neg_lam = -lam_ref[...]
    softplus = jnp.maximum(neg_lam, 0.0) + jnp.log1p(jnp.exp(-jnp.abs(neg_lam)))
    half_rate = (-0.5 * LRU_C) * softplus
    half_ba, half_bi = 0.5 * ba_ref[...], 0.5 * bi_ref[...]
    for t in range(n_tiles):
        lanes = slice(t * LANE, (t + 1) * LANE)
        xw = xb_s[:, starts[t]:starts[t] + kb]
        tanh_a = jnp.tanh(jnp.dot(xw, wa_ref[t], preferred_element_type=F32) + half_ba[:, lanes])
        tanh_i = jnp.tanh(jnp.dot(xw, wi_ref[t], preferred_element_type=F32) + half_bi[:, lanes])
        log_a = half_rate[:, lanes] * tanh_a + half_rate[:, lanes]
        a = jnp.exp(log_a)
        half_x = 0.5 * xc_s[:, lanes]
        a_s[:, lanes] = a
        bx_s[:, lanes] = jnp.sqrt(-jnp.tanh(log_a) * (a * a + 1.0)) * (half_x * tanh_i + half_x)

    h = carry[...]
    for s in range(bt):
        ts = (bt - 1 - s) if reverse else s
        slab = slice(ts * nb, (ts + 1) * nb)
        h = a_s[slab, :] * h + bx_s[slab, :]
        a_s[slab, :] = h
    carry[...] = h
    ht_ref[...] = h
    hs = a_s[...].reshape(bt, nb, a_s.shape[1])
    if reverse:
        out_ref[...] = (_gelu_tanh(gate_ref[...]) * (hsf_ref[...] + hs)).astype(out_ref.dtype)
    else:
        out_ref[...] = hs


def _lru_pass(u, st, conv_w, conv_b, wa, wi, b_a, b_i, lam, h0, starts, kb, *, reverse, hs_fwd=None):
    c = conv_w.shape[1]
    taps = conv_w.shape[0]
    left, right = taps // 2, taps - 1 - taps // 2
    nb, seq = st.nb, st.seq
    bt = min(max(256 // nb, SUBLANE), seq)
    assert seq % bt == 0 and c % LANE == 0 and left > 0 and bt % left == 0 and (right == 0 or bt % right == 0)
    nt = seq // bt
    n_tiles = c // LANE
    u3 = u.reshape(seq, nb, 2 * c)
    tmap = (lambda s: nt - 1 - s) if reverse else (lambda s: s)
    full = lambda *shape: pl.BlockSpec(shape, lambda s: (0,) * len(shape))
    in_specs = [pl.BlockSpec((left, nb, c), lambda s: (jnp.maximum(tmap(s) * (bt // left) - 1, 0), 0, 0)),
                pl.BlockSpec((bt, nb, c), lambda s: (tmap(s), 0, 0))]
    args = [u3, u3]
    if right > 0:
        in_specs.append(pl.BlockSpec((right, nb, c),
                                     lambda s: (jnp.minimum((tmap(s) + 1) * (bt // right), seq // right - 1), 0, 0)))
        args.append(u3)
    in_specs += [full(taps, c), full(1, c), full(n_tiles, kb, LANE), full(n_tiles, kb, LANE), full(1, c), full(1, c),
                 full(1, c), full(nb, c)]
    args += [conv_w, conv_b.reshape(1, c), wa, wi, b_a.reshape(1, c), b_i.reshape(1, c), lam.reshape(1, c), h0]
    if reverse:
        in_specs += [pl.BlockSpec((bt, nb, c), lambda s: (tmap(s), 0, 0)),
                     pl.BlockSpec((bt, nb, c), lambda s: (tmap(s), 0, 1))]
        args += [hs_fwd, u3]
    kern = functools.partial(_lru_pass_kernel, reverse=reverse, starts=tuple(starts), kb=kb, bt=bt, seq=seq, nb=nb,
                             taps=taps)
    blk = nb * bt * c * 4 / MIB
    vmem = (2 + 2 + 3 + (4 if reverse else 0) + 4) * blk + 4 * n_tiles * kb * LANE * 2 / MIB + 8
    return pl.pallas_call(
        kern,
        grid=(nt,),
        in_specs=in_specs,
        out_specs=[pl.BlockSpec((bt, nb, c), lambda s: (tmap(s), 0, 0)),
                   pl.BlockSpec((nb, c), lambda s: (0, 0))],
        out_shape=[jax.ShapeDtypeStruct((seq, nb, c), F32),
                   jax.ShapeDtypeStruct((nb, c), F32)],
        scratch_shapes=[pltpu.VMEM((nb * bt, c), F32), pltpu.VMEM((nb * bt, c), BF16),
                        pltpu.VMEM((nb * bt, c), F32), pltpu.VMEM((nb * bt, c), F32), pltpu.VMEM((nb, c), F32)],
        compiler_params=_cparams(1, vmem),
        name="lru_bwd" if reverse else "lru_fwd",
    )(*args)


def _mixer_nat(xs, streams, mods, cache_k, cache_v, j, w_qkv, g_mix, g_q, g_k, rpb, w_o):
    heads, dh = rpb.shape[0], g_q.shape[0]
    w_qkv, w_o = w_qkv.astype(BF16), w_o.astype(BF16)
    gains = jnp.concatenate([jnp.tile(g_q * (dh ** -0.5 * LOG2E), heads), jnp.tile(g_k, heads),
                             jnp.ones((heads * dh,), F32)])[None]
    spec = dict(head_w=dh, norm_div=dh, norm_cols=2 * heads * dh, gains=gains)
    new_x, extra = [], None
    for x, st in zip(xs, streams):
        latent = not st.shared
        qkv = _proj(x, w_qkv, st, norm_g=g_mix, mod=(mods[0], mods[1]), heads=spec,
                    out_dtype=BF16 if latent else F32, name="nat_qkv")
        if latent:
            o = _nat_attention(qkv, st, cache_k, cache_v, j, rpb, dh)
        else:
            o, kc, vc = _ctx_attention(qkv, qkv, qkv, st, n_heads=heads, dq=dh, dv=dh, q_col=0, k_col=heads * dh,
                                       v_col=2 * heads * dh, emit_kv=True)
            extra = (kc, vc)
        new_x.append(_proj(o, w_o, st, res=x, gate=mods[2], name="nat_out"))
    return new_x, extra


def _mixer_lru(xs, streams, mods, state, w_in, g_mix, conv_w, conv_b, w_a, b_a, w_i, b_i, lam, w_out):
    c = conv_w.shape[1]
    block = w_a.shape[-1]
    w_in, w_out = w_in.astype(BF16), w_out.astype(BF16)
    starts, kb = _band_plan(c, block)
    wa = [_band_weights(0.5 * w_a[d], c, block, starts, kb) for d in range(2)]
    wi = [_band_weights(0.5 * w_i[d], c, block, starts, kb) for d in range(2)]
    new_x, st_out = [], None
    for x, st in zip(xs, streams):
        latent = not st.shared
        h0 = state.astype(F32) if latent else jnp.zeros((st.nb, 2, c), F32)
        u = _proj(x, w_in, st, norm_g=g_mix, mod=(mods[0], mods[1]), out_time_major=True, name="lru_in")
        hs_f, t_f = _lru_pass(u, st, conv_w, conv_b, wa[0], wi[0], b_a[0], b_i[0], lam[0], h0[:, 0], starts, kb,
                              reverse=False)
        y, t_b = _lru_pass(u, st, conv_w, conv_b, wa[1], wi[1], b_a[1], b_i[1], lam[1], h0[:, 1], starts, kb,
                           reverse=True, hs_fwd=hs_f)
        if not latent:
            st_out = jnp.stack([t_f, t_b], axis=1)
        new_x.append(_proj(y.reshape(st.seq, st.nb * c), w_out, st, res=x, gate=mods[2], x_time_major=True,
                           name="lru_out"))
    return new_x, st_out


def _mixer_mla(xs, streams, mods, cache_ckv, cache_kr, w_down, g_mix, g_qa, g_kva, w_uq, w_ukv, g_q, g_k, w_o):
    d_model = w_down.shape[0]
    q_rank, kv_rank = g_qa.shape[0], g_kva.shape[0]
    qk_dim = g_q.shape[0]
    heads = w_uq.shape[1] // qk_dim
    rope = w_down.shape[1] - q_rank - kv_rank
    nope = qk_dim - rope
    assert nope == LANE and rope <= LANE and kv_rank % LANE == 0 and q_rank % LANE == 0
    head_w = 2 * LANE
    q_pad = -q_rank % kv_rank
    kv_col = q_rank + q_pad
    tail_pad = -(kv_col + kv_rank + rope) % 512
    w_dn = jnp.concatenate([w_down[:, :q_rank], jnp.zeros((d_model, q_pad), F32),
                            w_down[:, q_rank:q_rank + kv_rank], w_down[:, q_rank + kv_rank:],
                            jnp.zeros((d_model, tail_pad), F32)], axis=1).astype(BF16)
    kr_blk = (kv_col + kv_rank) // LANE
    w_q = jnp.pad(w_uq.reshape(q_rank, heads, qk_dim), ((0, 0), (0, 0), (0, head_w - qk_dim)))
    w_q = w_q.reshape(q_rank, heads * head_w).astype(BF16)
    w_ukv, w_o = w_ukv.astype(BF16), w_o.astype(BF16)
    gq = jnp.tile(jnp.pad(g_q * (qk_dim ** -0.5 * LOG2E), (0, head_w - qk_dim)), heads)[None]
    g1, g2 = g_k[None, :nope], jnp.pad(g_k[nope:], (0, LANE - rope))[None]
    p = cache_ckv.shape[1]
    new_x, extra = [], None
    for x, st in zip(xs, streams):
        latent = not st.shared
        d = _proj(x, w_dn, st, norm_g=g_mix, mod=(mods[0], mods[1]), name="mla_down")
        q_tabs = _rope_tables(st.seq, rope, nope, head_w) if latent else None
        k_tabs = _rope_tables(st.seq, rope, 0, LANE) if latent else None
        q = _proj(d, w_q, st, x_block=(q_rank, 0), norm_g=g_qa[None],
                  heads=dict(head_w=head_w, norm_div=qk_dim, norm_cols=heads * head_w, gains=gq, tabs=q_tabs,
                             rope_tiles=(False, True)),
                  out_dtype=BF16, name="mla_uq")
        kv = _mla_kv(d, (kv_rank, kv_col // kv_rank), w_ukv, d, (LANE, kr_blk), g_kva[None], g1, g2, k_tabs, st,
                     n_heads=heads, norm_div=qk_dim, emit_xn=not latent, name="mla_ukv")
        if latent:
            k, v = kv
            cst = _Stream(st.nb, p, 0, True)
            krc = jnp.pad(cache_kr.reshape(st.nb * p, rope), ((0, 0), (0, LANE - rope)))
            kc, vc = _mla_kv(cache_ckv.reshape(st.nb * p, kv_rank), (kv_rank, 0), w_ukv, krc, (LANE, 0), None,
                             g1, g2, None, cst, n_heads=heads, norm_div=qk_dim, emit_xn=False,
                             name="mla_ukv_cache")
            o = _joint_dense_attention(q, k, v, kc, vc, st, p, n_heads=heads, dq=head_w, dv=LANE)
        else:
            k, v, ckv = kv
            o = _ctx_attention(q, k, v, st, n_heads=heads, dq=head_w, dv=LANE, q_col=0, k_col=0, v_col=0)
            kr_out = d[:, kv_col + kv_rank:kv_col + kv_rank + rope]
            extra = (ckv.reshape(st.nb, st.seq, kv_rank), kr_out.reshape(st.nb, st.seq, rope))
        new_x.append(_proj(o, w_o, st, res=x, gate=mods[2], name="mla_out"))
    return new_x, extra


def _mixer_swa(xs, streams, mods, cache_k, cache_v, j, w_qkv, g_mix, g_q, g_k, sinks, w_o):
    dh = g_q.shape[0]
    heads = sinks.shape[0]
    kvh = (w_qkv.shape[1] // dh - heads) // 2
    w_qkv, w_o = w_qkv.astype(BF16), w_o.astype(BF16)
    gains = jnp.concatenate([jnp.tile(g_q * (dh ** -0.5 * LOG2E), heads), jnp.tile(g_k, kvh),
                             jnp.ones((kvh * dh,), F32)])[None]
    sinks = sinks.astype(F32) * LOG2E
    new_x, extra = [], None
    for x, st in zip(xs, streams):
        latent = not st.shared
        tabs = _rope_tables(st.seq, dh, 0, dh) if latent else None
        if tabs is not None:
            tabs = tuple(jnp.tile(t, (1, LANE // dh)) for t in tabs)
        spec = dict(head_w=dh, norm_div=dh, norm_cols=(heads + kvh) * dh, gains=gains, tabs=tabs,
                    rope_tiles=(True,))
        qkv = _proj(x, w_qkv, st, norm_g=g_mix, mod=(mods[0], mods[1]), heads=spec,
                    out_dtype=BF16 if latent else F32, name="swa_qkv")
        if latent:
            o = _swa_attention(qkv, st, sinks, heads=heads, kvh=kvh, dh=dh, cache=(cache_k, cache_v, j))
        else:
            o, kc, vc = _swa_attention(qkv, st, sinks, heads=heads, kvh=kvh, dh=dh)
            extra = (kc, vc)
        new_x.append(_proj(o, w_o, st, res=x, gate=mods[2], name="swa_out"))
    return new_x, extra


def kernel(x_prompt, x_sample, cache_nat_k, cache_nat_v, state_lru, cache_mla_ckv, cache_mla_krope, cache_swa_k, cache_swa_v, c, c_ctx, norm_mix, norm_ffn, w_mod, b_mod, ffn_w_in, ffn_conv_w, ffn_conv_b, ffn_w_out, nat_w_qkv, nat_q_norm, nat_k_norm, nat_rpb, nat_w_o, lru_w_in, lru_conv_w, lru_conv_b, lru_w_a, lru_b_a, lru_w_i, lru_b_i, lru_lambda, lru_w_out, mla_w_down, mla_q_a_norm, mla_kv_a_norm, mla_w_uq, mla_w_ukv, mla_q_norm, mla_k_norm, mla_w_o, swa_w_qkv, swa_q_norm, swa_k_norm, swa_sinks, swa_w_o):
    bc, sc, d = x_prompt.shape
    bl, n, _ = x_sample.shape
    depth = w_mod.shape[0]
    streams = (_Stream(bc, sc, 0, True), _Stream(bl, n, 1, False))
    xs = [x_prompt.reshape(bc * sc, d), x_sample.reshape(bl * n, d)]

    n_cond = 1 + bl
    cond_rows = -(-n_cond // SUBLANE) * SUBLANE
    cond = jnp.zeros((cond_rows, d), F32).at[0].set(c_ctx).at[1:n_cond].set(c)
    mods = _modulation(cond, w_mod, b_mod)[:, :n_cond]

    nat_k_l, nat_v_l, lru_l, ckv_l, krope_l, swa_k_l, swa_v_l = [], [], [], [], [], [], []
    for l in range(depth):
        kind, j = l % 4, l // 4
        m6 = [mods[l, :, None, t * d:(t + 1) * d] for t in range(6)]
        g_mix = norm_mix[l].reshape(1, d)
        if kind == 0:
            xs, (kc, vc) = _mixer_nat(xs, streams, m6, cache_nat_k, cache_nat_v, j, nat_w_qkv[j], g_mix,
                                      nat_q_norm[j], nat_k_norm[j], nat_rpb[j], nat_w_o[j])
            nat_k_l.append(kc)
            nat_v_l.append(vc)
        elif kind == 1:
            xs, st = _mixer_lru(xs, streams, m6, state_lru[:, j], lru_w_in[j], g_mix, lru_conv_w[j], lru_conv_b[j],
                                lru_w_a[j], lru_b_a[j], lru_w_i[j], lru_b_i[j], lru_lambda[j], lru_w_out[j])
            lru_l.append(st)
        elif kind == 2:
            xs, (ckv, kr) = _mixer_mla(xs, streams, m6, cache_mla_ckv[:, j], cache_mla_krope[:, j], mla_w_down[j],
                                       g_mix, mla_q_a_norm[j], mla_kv_a_norm[j], mla_w_uq[j], mla_w_ukv[j],
                                       mla_q_norm[j], mla_k_norm[j], mla_w_o[j])
            ckv_l.append(ckv)
            krope_l.append(kr)
        else:
            xs, (kc, vc) = _mixer_swa(xs, streams, m6, cache_swa_k, cache_swa_v, j, swa_w_qkv[j], g_mix,
                                      swa_q_norm[j], swa_k_norm[j], swa_sinks[j], swa_w_o[j])
            swa_k_l.append(kc)
            swa_v_l.append(vc)
        w_in, w_out = ffn_w_in[l].astype(BF16), ffn_w_out[l].astype(BF16)
        xs = [_conv_ffn(x, st, norm_ffn[l].reshape(1, d), m6[3], m6[4], m6[5], w_in, ffn_conv_w[l],
                        ffn_conv_b[l], w_out) for x, st in zip(xs, streams)]

    return (xs[0].reshape(bc, sc, d), xs[1].reshape(bl, n, d), jnp.stack(nat_k_l, axis=1),
            jnp.stack(nat_v_l, axis=1), jnp.stack(lru_l, axis=1), jnp.stack(ckv_l, axis=1),
            jnp.stack(krope_l, axis=1), jnp.stack(swa_k_l, axis=1), jnp.stack(swa_v_l, axis=1))
```

```python
import functools

import numpy as np
import jax
import jax.numpy as jnp
from jax import lax
from jax.experimental import pallas as pl
from jax.experimental.pallas import tpu as pltpu

F32 = jnp.float32
BF16 = jnp.bfloat16

GRID_W = 64
NA_WIN_ROWS = 8
NA_WIN_COLS = 16
NA_Q_ROWS = 8
NA_K_ROWS = 16
LRU_C = 8.0
SWA_WINDOW = 128
SWA_BLOCK = 128
ROPE_BASE = 10000.0
ROPE_GROUP = 32
EPS = 1e-6
NEG = -1e30
LOG2E = float(np.log2(np.e))
LANE = 128
SUBLANE = 8
HALO = 16
MIB = 1024 * 1024
VMEM_LIMIT_CAP_MIB = 60
ROW_TILES = (1024, 512, 256, 128, 64, 32, 16)
MAX_COL_TILE = 1024
MAX_COL_TILE_RESIDUAL = 512
FFN_ROW_TILE, FFN_FF_CHUNK = 1024, 512
HEADS_PER_STEP = (4, 2, 1)
MLA_KEY_CHUNK = 1024
LRU_ROWS_PER_STEP = 256


def _cparams(n_axes, vmem_mib):
    return pltpu.CompilerParams(dimension_semantics=("arbitrary",) * n_axes,
                                vmem_limit_bytes=int(min(vmem_mib, VMEM_LIMIT_CAP_MIB) * MIB))


def _largest_divisor(n, candidates):
    for c in candidates:
        if n % c == 0:
            return c
    return n


class _Stream:
    def __init__(self, nb, seq, mod0, shared_mod):
        self.nb, self.seq, self.rows, self.mod0, self.shared = nb, seq, nb * seq, mod0, shared_mod
        self.bm = _largest_divisor(self.rows if shared_mod else seq, ROW_TILES)

    def mod_index(self, row0):
        return self.mod0 if self.shared else self.mod0 + row0 // self.seq


def _norm_mod(x, g, shift, scale):
    ms = jnp.mean(x * x, axis=-1, keepdims=True)
    return (x * lax.rsqrt(ms + EPS)) * (g * (1.0 + scale)) + shift


def _rms(x, g):
    return (x * lax.rsqrt(jnp.mean(x * x, axis=-1, keepdims=True) + EPS)) * g


def _modulation_kernel(c_ref, w_ref, b_ref, o_ref):
    c = c_ref[...]
    sc = (c * jax.nn.sigmoid(c)).astype(BF16)
    o_ref[...] = jnp.dot(sc, w_ref[...].astype(BF16), preferred_element_type=F32) + b_ref[...]


def _modulation(cond, w_mod, b_mod):
    depth, d, n = w_mod.shape
    rows = cond.shape[0]
    bn = _largest_divisor(n, (512, 256, 128))
    return pl.pallas_call(
        _modulation_kernel,
        grid=(depth, n // bn),
        in_specs=[pl.BlockSpec((rows, d), lambda l, j: (0, 0)),
                  pl.BlockSpec((None, d, bn), lambda l, j: (l, 0, j)),
                  pl.BlockSpec((None, 1, bn), lambda l, j: (l, 0, j))],
        out_specs=pl.BlockSpec((None, rows, bn), lambda l, j: (l, 0, j)),
        out_shape=jax.ShapeDtypeStruct((depth, rows, n), F32),
        compiler_params=_cparams(2, 32),
        name="modulation",
    )(cond, w_mod, b_mod.reshape(depth, 1, n))


def _rope_tables(n_tokens, rot_dim, lead, width):
    t = jnp.arange(n_tokens)
    row = (t // GRID_W).astype(F32)
    col = (t % GRID_W).astype(F32)
    half = rot_dim // 2
    inv = ROPE_BASE ** (-jnp.arange(0, half, 2, dtype=F32) / half)
    ar = row[:, None] * inv
    ac = col[:, None] * inv
    ang = jnp.concatenate([ar, ar, ac, ac], axis=-1)
    cos, sin = jnp.cos(ang), jnp.sin(ang)
    first = (np.arange(rot_dim) % ROPE_GROUP) < ROPE_GROUP // 2
    sin_a = jnp.where(first, -sin, 0.0)
    sin_b = jnp.where(first, 0.0, sin)
    pad = ((0, 0), (lead, width - lead - rot_dim))
    return (jnp.pad(cos, pad, constant_values=1.0), jnp.pad(sin_a, pad), jnp.pad(sin_b, pad))


def _rope_apply(y, cos, sin_a, sin_b):
    shift = ROPE_GROUP // 2
    return y * cos + pltpu.roll(y, LANE - shift, 1) * sin_a + pltpu.roll(y, shift, 1) * sin_b


def _rotate_half_matrix():
    shift = ROPE_GROUP // 2
    src = lax.broadcasted_iota(jnp.int32, (LANE, LANE), 0)
    dst = lax.broadcasted_iota(jnp.int32, (LANE, LANE), 1)
    first = (dst % ROPE_GROUP) < shift
    return jnp.where(first & (src == dst + shift), -1.0,
                     jnp.where(jnp.logical_not(first) & (src == dst - shift), 1.0, 0.0)).astype(BF16)


def _rope_apply_mxu(y, cos, sin_a, sin_b, perm):
    hi = y.astype(BF16)
    lo = (y - hi.astype(F32)).astype(BF16)
    rot = jnp.dot(hi, perm, preferred_element_type=F32) + jnp.dot(lo, perm, preferred_element_type=F32)
    return y * cos + rot * (sin_b - sin_a)


def _fill_lhs(x_ref, xs_ref, xn_ref, prologue, g_ref, sh_ref, sc_ref, row_chunk):
    bm = x_ref.shape[0]

    def chunk(r, carry):
        rows = pl.ds(pl.multiple_of(r * row_chunk, row_chunk), row_chunk)
        x = x_ref[rows, :].astype(F32)
        if prologue == "norm_mod":
            x = _norm_mod(x, g_ref[...], sh_ref[...], sc_ref[...])
        elif prologue == "norm":
            x = _rms(x, g_ref[...])
        if xn_ref is not None:
            xn_ref[rows, :] = x
        xs_ref[rows, :] = x.astype(BF16)
        return carry
    n_chunks = bm // row_chunk
    lax.fori_loop(0, n_chunks, chunk, 0, unroll=2 if n_chunks % 2 == 0 else 1)


def _head_norm_store(acc, o_ref, hg_ref, tabs, head_w, norm_div, col0, norm_cols, rope_tiles):
    bn = acc.shape[1]
    period = tabs[0].shape[1] if tabs is not None else LANE
    perm = _rotate_half_matrix() if tabs is not None and head_w >= LANE else None
    if head_w < LANE:
        head_of_row = lax.broadcasted_iota(jnp.int32, (LANE, LANE), 0) // head_w
        head_of_col = lax.broadcasted_iota(jnp.int32, (LANE, LANE), 1) // head_w
        same_head = jnp.where(head_of_row == head_of_col, 1.0, 0.0).astype(BF16)
    for s0 in range(0, bn, max(head_w, LANE)):
        normed = None if norm_cols is None else (col0 + s0 < norm_cols)
        tiles = [acc[:, s0 + k * LANE:s0 + (k + 1) * LANE] for k in range(max(head_w, LANE) // LANE)]
        if head_w >= LANE:
            sq = None
            for y in tiles:
                sq = y * y if sq is None else sq + y * y
            inv = lax.rsqrt(jnp.sum(sq, axis=-1, keepdims=True) / norm_div + EPS)
        else:
            y2 = tiles[0] * tiles[0]
            hi = y2.astype(BF16)
            lo = (y2 - hi.astype(F32)).astype(BF16)
            ssq = (jnp.dot(hi, same_head, preferred_element_type=F32)
                   + jnp.dot(lo, same_head, preferred_element_type=F32))
            inv = lax.rsqrt(ssq / norm_div + EPS)
        if normed is not None:
            inv = jnp.where(normed, inv, 1.0)
        for k, y in enumerate(tiles):
            c0 = s0 + k * LANE
            y = (y * inv) * hg_ref[:, c0:c0 + LANE]
            t0 = c0 % period
            if tabs is not None and rope_tiles[t0 // LANE]:
                tab = tuple(t[:, t0:t0 + LANE] for t in tabs)
                rotated = _rope_apply(y, *tab) if head_w < LANE else _rope_apply_mxu(y, *tab, perm)
                y = rotated if normed is None else jnp.where(normed, rotated, y)
            o_ref[:, c0:c0 + LANE] = y.astype(o_ref.dtype)


def _proj_kernel(*refs, prologue, emit_xn, epilogue, head_w, norm_div, norm_cols, rope, rope_tiles, row_chunk):
    it = iter(refs)
    x_ref = next(it)
    g_ref = next(it) if prologue is not None else None
    sh_ref, sc_ref = (next(it), next(it)) if prologue == "norm_mod" else (None, None)
    w_ref = next(it)
    if epilogue == "res":
        res_ref, gate_ref = next(it), next(it)
    if epilogue == "heads":
        hg_ref = next(it)
        tabs = (next(it), next(it), next(it)) if rope else None
    o_ref = next(it)
    xn_ref = next(it) if emit_xn else None
    xs_ref = next(it, None)
    j = pl.program_id(1)
    bn = o_ref.shape[1]

    if xs_ref is None:
        xs_ref = x_ref
    else:
        @pl.when(j == 0)
        def _():
            _fill_lhs(x_ref, xs_ref, xn_ref, prologue, g_ref, sh_ref, sc_ref, row_chunk)

    acc = jnp.dot(xs_ref[...], w_ref[...], preferred_element_type=F32)
    if epilogue == "res":
        o_ref[...] = res_ref[...] + gate_ref[...] * acc
    elif epilogue == "heads":
        _head_norm_store(acc, o_ref, hg_ref, tabs, head_w, norm_div, j * bn, norm_cols, rope_tiles)
    else:
        o_ref[...] = acc.astype(o_ref.dtype)


def _proj(x, w, st, *, x_block=None, norm_g=None, mod=None, res=None, gate=None, heads=None, emit_xn=False,
          out_dtype=F32, bn=None, x_time_major=False, out_time_major=False, name="proj"):
    k, n = w.shape
    time_major = x_time_major or out_time_major
    bm = min(st.bm, st.seq) if time_major else st.bm
    tiles_per_seq = st.seq // bm if st.seq % bm == 0 else None
    if time_major:
        assert tiles_per_seq is not None and x_block is None
    if x_time_major:
        assert x.shape == (st.seq, st.nb * k)
        kidx = 0
    else:
        kx, kidx = x_block if x_block is not None else (x.shape[1], 0)
        assert kx == k and x.shape[0] == st.rows
    rows = st.rows
    prologue = None if norm_g is None else ("norm_mod" if mod is not None else "norm")
    epilogue = "res" if res is not None else ("heads" if heads is not None else None)
    rope = heads is not None and heads.get("tabs") is not None
    if bn is None:
        unit = LANE
        if epilogue == "heads":
            unit = max(heads["head_w"], LANE, heads["tabs"][0].shape[1] if rope else LANE)
        cap = MAX_COL_TILE_RESIDUAL if epilogue == "res" and x.dtype != BF16 else MAX_COL_TILE
        bn = next((c for c in range(cap, unit - 1, -unit) if n % c == 0), n)
    mod_idx = lambda i: st.mod_index(i * bm)

    if x_time_major:
        in_specs = [pl.BlockSpec((bm, k), lambda i, j: (i % tiles_per_seq, i // tiles_per_seq))]
    else:
        in_specs = [pl.BlockSpec((bm, k), lambda i, j: (i, kidx))]
    args = [x]
    if prologue is not None:
        in_specs.append(pl.BlockSpec((1, k), lambda i, j: (0, 0)))
        args.append(norm_g)
    if prologue == "norm_mod":
        in_specs += [pl.BlockSpec((None, 1, k), lambda i, j: (mod_idx(i), 0, 0))] * 2
        args += list(mod)
    in_specs.append(pl.BlockSpec((k, bn), lambda i, j: (0, j)))
    args.append(w.astype(BF16))
    if epilogue == "res":
        in_specs += [pl.BlockSpec((bm, bn), lambda i, j: (i, j)),
                     pl.BlockSpec((None, 1, bn), lambda i, j: (mod_idx(i), 0, j))]
        args += [res, gate]
    head_w = norm_div = 0
    norm_cols = rope_tiles = None
    if epilogue == "heads":
        head_w, norm_div = heads["head_w"], heads["norm_div"]
        norm_cols = heads["norm_cols"] if heads["norm_cols"] < n else None
        assert bn % max(head_w, LANE) == 0 and heads["norm_cols"] % max(head_w, LANE) == 0
        in_specs.append(pl.BlockSpec((1, bn), lambda i, j: (0, j)))
        args.append(heads["gains"])
        if rope:
            period = heads["tabs"][0].shape[1]
            rope_tiles = heads["rope_tiles"]
            assert bn % period == 0 and tiles_per_seq is not None and len(rope_tiles) == period // LANE
            in_specs += [pl.BlockSpec((bm, period), lambda i, j: (i % tiles_per_seq, 0))] * 3
            args += list(heads["tabs"])
    if out_time_major:
        n_col_tiles = n // bn
        out_shape = [jax.ShapeDtypeStruct((st.seq, st.nb * n), out_dtype)]
        out_specs = [pl.BlockSpec((bm, bn), lambda i, j: (i % tiles_per_seq, (i // tiles_per_seq) * n_col_tiles + j))]
    else:
        out_shape = [jax.ShapeDtypeStruct((rows, n), out_dtype)]
        out_specs = [pl.BlockSpec((bm, bn), lambda i, j: (i, j))]
    if emit_xn:
        out_shape.append(jax.ShapeDtypeStruct((rows, k), F32))
        out_specs.append(pl.BlockSpec((bm, k), lambda i, j: (i, 0)))
    xbytes = x.dtype.itemsize
    vmem = (2 * bm * k * xbytes + bm * k * 2 + 2 * k * bn * 2 + (6 if epilogue == "res" else 4) * bm * bn * 4
            + (2 * bm * k * 4 if emit_xn else 0)) / MIB + 8
    kern = functools.partial(_proj_kernel, prologue=prologue, emit_xn=emit_xn, epilogue=epilogue, head_w=head_w,
                             norm_div=norm_div, norm_cols=norm_cols, rope=rope, rope_tiles=rope_tiles,
                             row_chunk=min(bm, 128))
    direct_lhs = prologue is None and x.dtype == BF16 and not emit_xn
    out = pl.pallas_call(
        kern,
        grid=(rows // bm, n // bn),
        in_specs=in_specs,
        out_specs=out_specs,
        out_shape=out_shape,
        scratch_shapes=[] if direct_lhs else [pltpu.VMEM((bm, k), BF16)],
        compiler_params=_cparams(2, vmem),
        name=name,
    )(*args)
    return out if emit_xn else out[0]


def _ffn_kernel(xp_ref, x_ref, xn_ref, g_ref, sh_ref, sc_ref, gate_ref, wa_ref, wb_ref, cw_ref, cb_ref,
                wo_ref, o_ref, h_ref, *, bm, seq, row_chunk):
    i = pl.program_id(0)
    c = pl.program_id(1)
    n_chunks = pl.num_programs(1)

    @pl.when(c == 0)
    def _():
        g, sh, sc = g_ref[...], sh_ref[...], sc_ref[...]
        h_ref[0:HALO, :] = _norm_mod(xp_ref[...], g, sh, sc).astype(BF16)
        h_ref[HALO + bm:, :] = _norm_mod(xn_ref[...], g, sh, sc).astype(BF16)

        def chunk(r, carry):
            src = pl.ds(pl.multiple_of(r * row_chunk, row_chunk), row_chunk)
            dst = pl.ds(pl.multiple_of(HALO + r * row_chunk, HALO), row_chunk)
            h_ref[dst, :] = _norm_mod(x_ref[src, :], g, sh, sc).astype(BF16)
            return carry
        n_row_chunks = bm // row_chunk
        lax.fori_loop(0, n_row_chunks, chunk, 0, unroll=2 if n_row_chunks % 2 == 0 else 1)
        o_ref[...] = jnp.zeros_like(o_ref)

    ua = jnp.dot(h_ref[...], wa_ref[...], preferred_element_type=F32)
    ub = jnp.dot(h_ref[HALO:HALO + bm, :], wb_ref[...], preferred_element_type=F32)
    n_all = bm + 2 * HALO
    u_prev = pltpu.roll(ua, 1, 0)[HALO:HALO + bm]
    u_next = pltpu.roll(ua, n_all - 1, 0)[HALO:HALO + bm]
    u_mid = ua[HALO:HALO + bm]
    pos = jnp.bitwise_and(i * bm + lax.broadcasted_iota(jnp.int32, (bm, 1), 0), seq - 1)
    u_prev = jnp.where(pos == 0, 0.0, u_prev)
    u_next = jnp.where(pos == seq - 1, 0.0, u_next)
    cw = cw_ref[...]
    a = cb_ref[...] + u_prev * cw[0:1] + u_mid * cw[1:2] + u_next * cw[2:3]
    gated = ((a * jax.nn.sigmoid(a)) * ub).astype(BF16)
    o_ref[...] += jnp.dot(gated, wo_ref[...], preferred_element_type=F32)

    @pl.when(c == n_chunks - 1)
    def _():
        o_ref[...] = x_ref[...] + gate_ref[...] * o_ref[...]


def _conv_ffn(x, st, g, shift, scale, gate, w_in, conv_w, conv_b, w_out, bm=FFN_ROW_TILE, ck=FFN_FF_CHUNK):
    m, d = x.shape
    d_ff = w_out.shape[0]
    bm = min(st.bm, bm)
    ck = _largest_divisor(d_ff, tuple(c for c in (512, 256, 128) if c <= ck))
    n_chunks = d_ff // ck
    n_halo_blocks = m // HALO
    assert st.seq & (st.seq - 1) == 0 and conv_w.shape[0] == 3
    mod_idx = lambda i: st.mod_index(i * bm)
    kern = functools.partial(_ffn_kernel, bm=bm, seq=st.seq, row_chunk=min(bm, 128))
    vmem = (4 * bm * d * 4 + (bm + 2 * HALO) * d * 2 + 6 * d * ck * 2 + 5 * (bm + 2 * HALO) * ck * 4) / MIB + 4
    return pl.pallas_call(
        kern,
        grid=(m // bm, n_chunks),
        in_specs=[
            pl.BlockSpec((HALO, d), lambda i, c: (jnp.maximum(i * (bm // HALO) - 1, 0), 0)),
            pl.BlockSpec((bm, d), lambda i, c: (i, 0)),
            pl.BlockSpec((HALO, d), lambda i, c: (jnp.minimum((i + 1) * (bm // HALO), n_halo_blocks - 1), 0)),
            pl.BlockSpec((1, d), lambda i, c: (0, 0)),
            pl.BlockSpec((None, 1, d), lambda i, c: (mod_idx(i), 0, 0)),
            pl.BlockSpec((None, 1, d), lambda i, c: (mod_idx(i), 0, 0)),
            pl.BlockSpec((None, 1, d), lambda i, c: (mod_idx(i), 0, 0)),
            pl.BlockSpec((d, ck), lambda i, c: (0, c)),
            pl.BlockSpec((d, ck), lambda i, c: (0, n_chunks + c)),
            pl.BlockSpec((conv_w.shape[0], ck), lambda i, c: (0, c)),
            pl.BlockSpec((1, ck), lambda i, c: (0, c)),
            pl.BlockSpec((ck, d), lambda i, c: (c, 0)),
        ],
        out_specs=pl.BlockSpec((bm, d), lambda i, c: (i, 0)),
        out_shape=jax.ShapeDtypeStruct((m, d), F32),
        scratch_shapes=[pltpu.VMEM((bm + 2 * HALO, d), BF16)],
        compiler_params=_cparams(2, vmem),
        name="conv_ffn",
    )(x, x, x, g, shift, scale, gate, w_in, w_in, conv_w, conv_b.reshape(1, d_ff), w_out)


def _qk(q, k):
    return lax.dot_general(q, k, (((1,), (1,)), ((), ())), preferred_element_type=F32)


def _attend(scores, values, sink=None):
    m = None
    for s in scores:
        mi = jnp.max(s, axis=-1, keepdims=True)
        m = mi if m is None else jnp.maximum(m, mi)
    if sink is not None:
        m = jnp.maximum(m, sink)
    es = [jnp.exp2(s - m) for s in scores]
    den = None
    for e in es:
        di = jnp.sum(e, axis=-1, keepdims=True)
        den = di if den is None else den + di
    if sink is not None:
        den = den + jnp.exp2(sink - m)
    out = None
    for e, v in zip(es, values):
        oi = jnp.dot(e.astype(BF16), v, preferred_element_type=F32)
        out = oi if out is None else out + oi
    return out * (1.0 / den)


def _ctx_attn_kernel(q_ref, k_ref, v_ref, *outs, heads, dq, dv, emit_kv):
    o_ref = outs[0]
    for h in range(heads):
        q = q_ref[:, h * dq:(h + 1) * dq].astype(BF16)
        k = k_ref[:, h * dq:(h + 1) * dq]
        v = v_ref[:, h * dv:(h + 1) * dv]
        if emit_kv:
            outs[1][h] = k.astype(F32)
            outs[2][h] = v.astype(F32)
        o = _attend([_qk(q, k.astype(BF16))], [v.astype(BF16)])
        o_ref[:, h * dv:(h + 1) * dv] = o.astype(o_ref.dtype)


def _ctx_attention(qm, km, vm, st, *, n_heads, dq, dv, q_col, k_col, v_col, emit_kv=False):
    hb = _largest_divisor(n_heads, HEADS_PER_STEP)
    s = st.seq
    assert q_col % (hb * dq) == 0 and k_col % (hb * dq) == 0 and v_col % (hb * dv) == 0
    qo, ko, vo = q_col // (hb * dq), k_col // (hb * dq), v_col // (hb * dv)
    out_shape = [jax.ShapeDtypeStruct((st.rows, n_heads * dv), BF16)]
    out_specs = [pl.BlockSpec((s, hb * dv), lambda b, g: (b, g))]
    if emit_kv:
        out_shape += [jax.ShapeDtypeStruct((st.nb, n_heads, s, dq), F32),
                      jax.ShapeDtypeStruct((st.nb, n_heads, s, dv), F32)]
        out_specs += [pl.BlockSpec((None, hb, s, dq), lambda b, g: (b, g, 0, 0)),
                      pl.BlockSpec((None, hb, s, dv), lambda b, g: (b, g, 0, 0))]
    out = pl.pallas_call(
        functools.partial(_ctx_attn_kernel, heads=hb, dq=dq, dv=dv, emit_kv=emit_kv),
        grid=(st.nb, n_heads // hb),
        in_specs=[pl.BlockSpec((s, hb * dq), lambda b, g: (b, qo + g)),
                  pl.BlockSpec((s, hb * dq), lambda b, g: (b, ko + g)),
                  pl.BlockSpec((s, hb * dv), lambda b, g: (b, vo + g))],
        out_specs=out_specs,
        out_shape=out_shape,
        compiler_params=_cparams(2, 32),
        name="ctx_attention",
    )(qm, km, vm)
    return out if emit_kv else out[0]


def _nat_kernel(q_ref, k_ref, v_ref, kc_ref, vc_ref, bias_ref, o_ref, *, key_rows, rows, heads, dh):
    i = pl.program_id(2)
    n_keys = key_rows * GRID_W
    first_row = jnp.clip(i * NA_Q_ROWS - NA_WIN_ROWS // 2, 0, rows - key_rows)
    start = pl.multiple_of(first_row * GRID_W, GRID_W * 4)
    for h in range(heads):
        lanes = slice(h * dh, (h + 1) * dh)
        q = q_ref[:, lanes]
        k = k_ref[pl.ds(start, n_keys), lanes]
        v = v_ref[pl.ds(start, n_keys), lanes]
        s_loc = _qk(q, k) + bias_ref[h]
        s_ctx = _qk(q, kc_ref[h].astype(BF16))
        o_ref[:, lanes] = _attend([s_loc, s_ctx], [v, vc_ref[h].astype(BF16)]).astype(o_ref.dtype)


def _nat_bias(rpb, rows):
    n_blocks = rows // NA_Q_ROWS
    key_rows = min(NA_K_ROWS, rows)
    wr = min(NA_WIN_ROWS, rows)
    reps = [0, min(1, n_blocks - 1), n_blocks - 1]
    heads = rpb.shape[0]
    nq, nk = NA_Q_ROWS * GRID_W, key_rows * GRID_W
    shape = (NA_Q_ROWS, GRID_W, key_rows, GRID_W)
    qc = np.arange(GRID_W)
    cstart = np.clip(qc - NA_WIN_COLS // 2, 0, GRID_W - NA_WIN_COLS)
    col_ok = (qc[None, :] >= cstart[:, None]) & (qc[None, :] < cstart[:, None] + NA_WIN_COLS)
    rp = jnp.pad(rpb.astype(F32) * LOG2E,
                 ((0, 0), (key_rows, key_rows), (GRID_W - NA_WIN_COLS, GRID_W - NA_WIN_COLS)))
    row_slabs, mask_l = [], []
    for i in reps:
        ks = int(np.clip(i * NA_Q_ROWS - NA_WIN_ROWS // 2, 0, rows - key_rows))
        r = i * NA_Q_ROWS + np.arange(NA_Q_ROWS)
        rs = np.clip(r - wr // 2, 0, rows - wr)
        kr = ks + np.arange(key_rows)
        row_ok = (kr[None, :] >= rs[:, None]) & (kr[None, :] < rs[:, None] + wr)
        for rq in range(NA_Q_ROWS):
            first = ks - int(r[rq]) + NA_WIN_ROWS - 1 + key_rows
            assert 0 <= first and first + key_rows <= rp.shape[1]
            row_slabs.append(rp[:, first:first + key_rows, :])
        mask_l.append(np.broadcast_to(row_ok[:, None, :, None] & col_ok[None, :, None, :], shape).reshape(nq, nk))
    slab = jnp.stack(row_slabs, axis=1).reshape(heads, len(reps), NA_Q_ROWS, key_rows, 2 * GRID_W - 1)
    toep = jnp.stack([slab[..., GRID_W - 1 - c:2 * GRID_W - 1 - c] for c in range(GRID_W)], axis=3)
    bias = toep.reshape(heads, len(reps), nq, nk)
    return jnp.where(jnp.asarray(np.stack(mask_l))[None], bias, NEG)


def _nat_attention(qkv, st, cache_k, cache_v, j, rpb, dh):
    heads = rpb.shape[0]
    n = st.seq
    p = cache_k.shape[3]
    rows = n // GRID_W
    assert rows % NA_Q_ROWS == 0 and rows >= NA_K_ROWS and dh % LANE == 0
    n_blocks = rows // NA_Q_ROWS
    key_rows = min(NA_K_ROWS, rows)
    nq, nk = NA_Q_ROWS * GRID_W, key_rows * GRID_W
    bias = _nat_bias(rpb, rows)
    btype = lambda i: jnp.where(i == 0, 0, jnp.where(i == n_blocks - 1, 2, 1))
    hb = _largest_divisor(heads, HEADS_PER_STEP)
    hg = heads // hb
    kern = functools.partial(_nat_kernel, key_rows=key_rows, rows=rows, heads=hb, dh=dh)
    return pl.pallas_call(
        kern,
        grid=(st.nb, hg, n_blocks),
        in_specs=[pl.BlockSpec((nq, hb * dh), lambda b, h, i: (b * n_blocks + i, h)),
                  pl.BlockSpec((n, hb * dh), lambda b, h, i: (b, hg + h)),
                  pl.BlockSpec((n, hb * dh), lambda b, h, i: (b, 2 * hg + h)),
                  pl.BlockSpec((None, None, hb, p, dh), lambda b, h, i: (b, j, h, 0, 0)),
                  pl.BlockSpec((None, None, hb, p, dh), lambda b, h, i: (b, j, h, 0, 0)),
                  pl.BlockSpec((hb, None, nq, nk), lambda b, h, i: (h, btype(i), 0, 0))],
        out_specs=pl.BlockSpec((nq, hb * dh), lambda b, h, i: (b * n_blocks + i, h)),
        out_shape=jax.ShapeDtypeStruct((st.rows, heads * dh), BF16),
        compiler_params=_cparams(3, 56),
        name="nat_attention",
    )(qkv, qkv, qkv, cache_k, cache_v, bias)


def _joint_dense_kernel(q_ref, k_ref, v_ref, kc_ref, vc_ref, o_ref, *, chunk):
    q = q_ref[...]
    n = k_ref.shape[0]
    pieces = [(k_ref, v_ref, c0, min(chunk, n - c0)) for c0 in range(0, n, chunk)]
    pieces.append((kc_ref, vc_ref, 0, kc_ref.shape[0]))
    m = den = acc = None
    for kr, vr, c0, size in pieces:
        s = _qk(q, kr[c0:c0 + size, :])
        mc = jnp.max(s, axis=-1, keepdims=True)
        m_new = mc if m is None else jnp.maximum(m, mc)
        e = jnp.exp2(s - m_new)
        dc = jnp.sum(e, axis=-1, keepdims=True)
        pv = jnp.dot(e.astype(BF16), vr[c0:c0 + size, :], preferred_element_type=F32)
        if m is None:
            den, acc = dc, pv
        else:
            alpha = jnp.exp2(m - m_new)
            den, acc = alpha * den + dc, alpha * acc + pv
        m = m_new
    o_ref[...] = (acc * (1.0 / den)).astype(o_ref.dtype)


def _joint_dense_attention(qm, km, vm, kcm, vcm, st, p, *, n_heads, dq, dv):
    n = st.seq
    bq = _largest_divisor(n, (1024, 512, 256, 128, 64, 32, 16))
    nqb = n // bq
    return pl.pallas_call(
        functools.partial(_joint_dense_kernel, chunk=MLA_KEY_CHUNK),
        grid=(st.nb, n_heads, nqb),
        in_specs=[pl.BlockSpec((bq, dq), lambda b, h, i: (b * nqb + i, h)),
                  pl.BlockSpec((n, dq), lambda b, h, i: (b, h)),
                  pl.BlockSpec((n, dv), lambda b, h, i: (b, h)),
                  pl.BlockSpec((p, dq), lambda b, h, i: (b, h)),
                  pl.BlockSpec((p, dv), lambda b, h, i: (b, h))],
        out_specs=pl.BlockSpec((bq, dv), lambda b, h, i: (b * nqb + i, h)),
        out_shape=jax.ShapeDtypeStruct((st.rows, n_heads * dv), BF16),
        compiler_params=_cparams(3, 48),
        name="mla_attention",
    )(qm, km, vm, kcm, vcm)


def _both_halves(x, s):
    x = x.astype(F32)
    low = lax.broadcasted_iota(jnp.int32, (1, LANE), 1) < LANE // 2
    keep = low if s == 0 else jnp.logical_not(low)
    return jnp.where(keep, x, pltpu.roll(x, LANE // 2, 1)).astype(BF16)


def _swa_step(sinks_ref, pair, q_ref, k, v, kc, vc, o_ref, *, dh, groups, local_bias):
    kv_per_step = LANE // dh
    assert kv_per_step == 2 and groups % 2 == 0
    rows = q_ref.shape[0]
    low = lax.broadcasted_iota(jnp.int32, (1, LANE), 1) < dh
    row_group = lax.broadcasted_iota(jnp.int32, (groups * rows, 1), 0) // rows
    for s in range(kv_per_step):
        kd, vd = _both_halves(k, s), _both_halves(v, s)
        q_parts = []
        for g in range(groups):
            c0 = ((s * groups + g) * dh // LANE) * LANE
            tile = q_ref[:, c0:c0 + LANE].astype(BF16)
            q_parts.append(jnp.where(low if g % 2 == 0 else jnp.logical_not(low), tile, jnp.zeros_like(tile)))
        q = jnp.concatenate(q_parts, axis=0)
        sink = jnp.zeros((groups * rows, 1), F32)
        for g in range(groups):
            sink = jnp.where(row_group == g, sinks_ref[(pair * kv_per_step + s) * groups + g], sink)
        s_loc = _qk(q, kd)
        if local_bias is not None:
            s_loc = s_loc + jnp.concatenate([local_bias] * groups, axis=0)
        if kc is not None:
            kcd = jnp.concatenate([kc[s], kc[s]], axis=-1).astype(BF16)
            vcd = jnp.concatenate([vc[s], vc[s]], axis=-1).astype(BF16)
            out = _attend([s_loc, _qk(q, kcd)], [vd, vcd], sink)
        else:
            out = _attend([s_loc], [vd], sink)
        for g in range(0, groups, 2):
            c0 = (s * groups + g) * dh
            o_ref[:, c0:c0 + LANE] = jnp.where(low, out[g * rows:(g + 1) * rows],
                                               out[(g + 1) * rows:(g + 2) * rows]).astype(o_ref.dtype)


def _swa_ctx_kernel(sinks_ref, q_ref, k_ref, v_ref, o_ref, ko_ref, vo_ref, *, dh, groups):
    pair = pl.program_id(1)
    k, v = k_ref[...], v_ref[...]
    for s in range(LANE // dh):
        ko_ref[s] = k[:, s * dh:(s + 1) * dh].astype(F32)
        vo_ref[s] = v[:, s * dh:(s + 1) * dh].astype(F32)
    _swa_step(sinks_ref, pair, q_ref, k, v, None, None, o_ref, dh=dh, groups=groups, local_bias=None)


def _swa_lat_kernel(sinks_ref, q_ref, k_ref, v_ref, kc_ref, vc_ref, o_ref, *, dh, groups, n):
    pair = pl.program_id(1)
    blk = pl.program_id(2)
    n_keys = min(3 * SWA_BLOCK, n)
    start = pl.multiple_of(jnp.clip((blk - 1) * SWA_BLOCK, 0, n - n_keys), SWA_BLOCK)
    k = k_ref[pl.ds(start, n_keys), :]
    v = v_ref[pl.ds(start, n_keys), :]
    qpos = blk * SWA_BLOCK + lax.broadcasted_iota(jnp.int32, (SWA_BLOCK, 1), 0)
    kpos = start + lax.broadcasted_iota(jnp.int32, (1, n_keys), 1)
    bias = jnp.where(jnp.abs(qpos - kpos) <= SWA_WINDOW, 0.0, NEG)
    _swa_step(sinks_ref, pair, q_ref, k, v, kc_ref, vc_ref, o_ref, dh=dh, groups=groups, local_bias=bias)


def _swa_attention(qkv, st, sinks, *, heads, kvh, dh, cache=None):
    groups = heads // kvh
    kv_per_step = LANE // dh
    assert LANE % dh == 0 and kvh % kv_per_step == 0 and groups % kv_per_step == 0
    pairs = kvh // kv_per_step
    qw = kv_per_step * groups * dh
    k_blk = heads * dh // LANE
    v_blk = (heads + kvh) * dh // LANE
    n = st.seq
    common = dict(dh=dh, groups=groups)
    smem = pl.BlockSpec(memory_space=pltpu.SMEM)
    if cache is None:
        out = pl.pallas_call(
            functools.partial(_swa_ctx_kernel, **common),
            grid=(st.nb, pairs),
            in_specs=[smem,
                      pl.BlockSpec((n, qw), lambda b, c: (b, c)),
                      pl.BlockSpec((n, LANE), lambda b, c: (b, k_blk + c)),
                      pl.BlockSpec((n, LANE), lambda b, c: (b, v_blk + c))],
            out_specs=[pl.BlockSpec((n, qw), lambda b, c: (b, c)),
                       pl.BlockSpec((None, kv_per_step, n, dh), lambda b, c: (b, c, 0, 0)),
                       pl.BlockSpec((None, kv_per_step, n, dh), lambda b, c: (b, c, 0, 0))],
            out_shape=[jax.ShapeDtypeStruct((st.rows, heads * dh), BF16),
                       jax.ShapeDtypeStruct((st.nb, kvh, n, dh), F32),
                       jax.ShapeDtypeStruct((st.nb, kvh, n, dh), F32)],
            compiler_params=_cparams(2, 32),
            name="swa_ctx_attention",
        )(sinks, qkv, qkv, qkv)
        return out
    cache_k, cache_v, j = cache
    p = cache_k.shape[3]
    nblk = n // SWA_BLOCK
    assert n % SWA_BLOCK == 0
    return pl.pallas_call(
        functools.partial(_swa_lat_kernel, n=n, **common),
        grid=(st.nb, pairs, nblk),
        in_specs=[smem,
                  pl.BlockSpec((SWA_BLOCK, qw), lambda b, c, i: (b * nblk + i, c)),
                  pl.BlockSpec((n, LANE), lambda b, c, i: (b, k_blk + c)),
                  pl.BlockSpec((n, LANE), lambda b, c, i: (b, v_blk + c)),
                  pl.BlockSpec((None, None, kv_per_step, p, dh), lambda b, c, i: (b, j, c, 0, 0)),
                  pl.BlockSpec((None, None, kv_per_step, p, dh), lambda b, c, i: (b, j, c, 0, 0))],
        out_specs=pl.BlockSpec((SWA_BLOCK, qw), lambda b, c, i: (b * nblk + i, c)),
        out_shape=jax.ShapeDtypeStruct((st.rows, heads * dh), BF16),
        compiler_params=_cparams(3, 32),
        name="swa_attention",
    )(sinks, qkv, qkv, qkv, cache_k, cache_v)


def _mla_kv_kernel(*refs, norm, emit_xn, rope, row_chunk, norm_div):
    it = iter(refs)
    x_ref = next(it)
    g_ref = next(it) if norm else None
    w_ref, kr_ref, g1_ref, g2_ref = next(it), next(it), next(it), next(it)
    tabs = (next(it), next(it), next(it)) if rope else None
    k_ref, v_ref = next(it), next(it)
    xn_ref = next(it) if emit_xn else None
    xs_ref = next(it)

    @pl.when(pl.program_id(1) == 0)
    def _():
        _fill_lhs(x_ref, xs_ref, xn_ref, "norm" if norm else None, g_ref, None, None, row_chunk)

    acc = jnp.dot(xs_ref[...], w_ref[...], preferred_element_type=F32)
    kr = kr_ref[...]
    kr_ssq = jnp.sum(kr * kr, axis=-1, keepdims=True)
    shared = kr * g2_ref[...]
    if rope:
        shared = _rope_apply(shared, *(t[...] for t in tabs))
    for h in range(acc.shape[1] // (2 * LANE)):
        nope = acc[:, 2 * h * LANE:(2 * h + 1) * LANE]
        inv = lax.rsqrt((jnp.sum(nope * nope, axis=-1, keepdims=True) + kr_ssq) / norm_div + EPS)
        k_ref[:, 2 * h * LANE:(2 * h + 1) * LANE] = ((nope * inv) * g1_ref[...]).astype(k_ref.dtype)
        k_ref[:, (2 * h + 1) * LANE:(2 * h + 2) * LANE] = (shared * inv).astype(k_ref.dtype)
        v_ref[:, h * LANE:(h + 1) * LANE] = acc[:, (2 * h + 1) * LANE:(2 * h + 2) * LANE].astype(v_ref.dtype)


def _mla_kv(x, x_block, w_ukv, kr, kr_block, g_kva, g1, g2, tabs, st, *, n_heads, norm_div, emit_xn, name):
    rows = x.shape[0]
    k, n = w_ukv.shape
    head_n = n // n_heads
    assert head_n == 2 * LANE, "nope and value widths must both be one lane tile"
    kx, kidx = x_block
    krw, kridx = kr_block
    assert kx == k and krw == LANE
    bm = st.bm
    norm = g_kva is not None
    rope = tabs is not None
    in_specs = [pl.BlockSpec((bm, k), lambda i, h: (i, kidx))]
    args = [x]
    if norm:
        in_specs.append(pl.BlockSpec((1, k), lambda i, h: (0, 0)))
        args.append(g_kva)
    hb = _largest_divisor(n_heads, HEADS_PER_STEP)
    in_specs += [pl.BlockSpec((k, hb * head_n), lambda i, h: (0, h)),
                 pl.BlockSpec((bm, LANE), lambda i, h: (i, kridx)),
                 pl.BlockSpec((1, LANE), lambda i, h: (0, 0)),
                 pl.BlockSpec((1, LANE), lambda i, h: (0, 0))]
    args += [w_ukv, kr, g1, g2]
    if rope:
        tiles_per_seq = st.seq // bm
        in_specs += [pl.BlockSpec((bm, LANE), lambda i, h: (i % tiles_per_seq, 0))] * 3
        args += list(tabs)
    out_shape = [jax.ShapeDtypeStruct((rows, n_heads * 2 * LANE), BF16),
                 jax.ShapeDtypeStruct((rows, n_heads * LANE), BF16)]
    out_specs = [pl.BlockSpec((bm, hb * 2 * LANE), lambda i, h: (i, h)),
                 pl.BlockSpec((bm, hb * LANE), lambda i, h: (i, h))]
    if emit_xn:
        out_shape.append(jax.ShapeDtypeStruct((rows, k), F32))
        out_specs.append(pl.BlockSpec((bm, k), lambda i, h: (i, 0)))
    kern = functools.partial(_mla_kv_kernel, norm=norm, emit_xn=emit_xn, rope=rope, row_chunk=min(bm, 128),
                             norm_div=norm_div)
    return pl.pallas_call(
        kern,
        grid=(rows // bm, n_heads // hb),
        in_specs=in_specs,
        out_specs=out_specs,
        out_shape=out_shape,
        scratch_shapes=[pltpu.VMEM((bm, k), BF16)],
        compiler_params=_cparams(2, 40),
        name=name,
    )(*args)


def _band_plan(width, block):
    n_tiles = width // LANE
    lo = [((t * LANE) // block) * block for t in range(n_tiles)]
    hi = [(((t + 1) * LANE - 1) // block + 1) * block for t in range(n_tiles)]
    start = [(l // LANE) * LANE for l in lo]
    kb = max(-(-(h - s) // LANE) * LANE for h, s in zip(hi, start))
    kb = min(kb, width)
    start = [min(s, width - kb) for s in start]
    return start, kb


def _band_weights(w, width, block, start, kb):
    n_tiles = width // LANE
    wb = w.astype(BF16)
    tiles = []
    for t in range(n_tiles):
        pieces = []
        col = t * LANE
        while col < (t + 1) * LANE:
            blk = col // block
            col_end = min((blk + 1) * block, (t + 1) * LANE)
            sub = wb[blk, :, col - blk * block:col_end - blk * block]
            top = blk * block - start[t]
            pieces.append(jnp.pad(sub, ((top, kb - top - block), (0, 0))))
            col = col_end
        tiles.append(jnp.concatenate(pieces, axis=1))
    return jnp.stack(tiles)


def _gelu_tanh(x):
    cdf = 0.5 * (1.0 + jnp.tanh(np.float32(np.sqrt(2.0 / np.pi)) * (x + 0.044715 * (x * x * x))))
    return x * cdf


def _lru_pass_kernel(*refs, reverse, starts, kb, bt, nb, taps):
    left = taps // 2
    right = taps - 1 - left
    it = iter(refs)
    xp_ref, x_ref = next(it), next(it)
    xn_ref = next(it) if right > 0 else None
    cw_ref, cb_ref = next(it), next(it)
    wa_ref, wi_ref, ba_ref, bi_ref, lam_ref, h0_ref = (next(it) for _ in range(6))
    hsf_ref, gate_ref = (next(it), next(it)) if reverse else (None, None)
    out_ref, ht_ref = next(it), next(it)
    xc_s, xb_s, a_s, bx_s, carry = next(it), next(it), next(it), next(it), next(it)

    step = pl.program_id(0)
    n_steps = pl.num_programs(0)
    tile = (n_steps - 1 - step) if reverse else step
    n_tiles = len(starts)
    rows = nb * bt

    @pl.when(step == 0)
    def _():
        carry[...] = h0_ref[...]

    for t in range(n_tiles):
        lanes = slice(t * LANE, (t + 1) * LANE)
        parts = [jnp.where(tile > 0, xp_ref[:, :, lanes], 0.0), x_ref[:, :, lanes]]
        if right > 0:
            parts.append(jnp.where(tile < n_steps - 1, xn_ref[:, :, lanes], 0.0))
        full = jnp.concatenate(parts, axis=0)
        acc = jnp.broadcast_to(cb_ref[:, lanes], (bt, nb, LANE))
        for k in range(taps):
            acc = acc + full[k:k + bt] * cw_ref[k:k + 1, lanes]
        acc = acc.reshape(rows, LANE)
        xc_s[:, lanes] = acc
        xb_s[:, lanes] = acc.astype(BF16)

    neg_lam = -lam_ref[...]
    softplus = jnp.maximum(neg_lam, 0.0) + jnp.log1p(jnp.exp(-jnp.abs(neg_lam)))
    half_rate = (-0.5 * LRU_C) * softplus
    half_ba, half_bi = 0.5 * ba_ref[...], 0.5 * bi_ref[...]
    for t in range(n_tiles):
        lanes = slice(t * LANE, (t + 1) * LANE)
        xw = xb_s[:, starts[t]:starts[t] + kb]
        tanh_a = jnp.tanh(jnp.dot(xw, wa_ref[t], preferred_element_type=F32) + half_ba[:, lanes])
        tanh_i = jnp.tanh(jnp.dot(xw, wi_ref[t], preferred_element_type=F32) + half_bi[:, lanes])
        log_a = half_rate[:, lanes] * tanh_a + half_rate[:, lanes]
        a = jnp.exp(log_a)
        half_x = 0.5 * xc_s[:, lanes]
        a_s[:, lanes] = a
        bx_s[:, lanes] = jnp.sqrt(-jnp.tanh(log_a) * (a * a + 1.0)) * (half_x * tanh_i + half_x)

    h = carry[...]
    for s in range(bt):
        ts = (bt - 1 - s) if reverse else s
        slab = slice(ts * nb, (ts + 1) * nb)
        h = a_s[slab, :] * h + bx_s[slab, :]
        a_s[slab, :] = h
    carry[...] = h
    ht_ref[...] = h
    hs = a_s[...].reshape(bt, nb, a_s.shape[1])
    if reverse:
        out_ref[...] = (_gelu_tanh(gate_ref[...]) * (hsf_ref[...] + hs)).astype(out_ref.dtype)
    else:
        out_ref[...] = hs


def _lru_pass(u, st, conv_w, conv_b, wa, wi, b_a, b_i, lam, h0, starts, kb, *, reverse, hs_fwd=None):
    c = conv_w.shape[1]
    taps = conv_w.shape[0]
    left, right = taps // 2, taps - 1 - taps // 2
    nb, seq = st.nb, st.seq
    bt = min(max(LRU_ROWS_PER_STEP // nb, SUBLANE), seq)
    assert seq % bt == 0 and c % LANE == 0 and left > 0 and bt % left == 0 and (right == 0 or bt % right == 0)
    nt = seq // bt
    n_tiles = c // LANE
    u3 = u.reshape(seq, nb, 2 * c)
    tmap = (lambda s: nt - 1 - s) if reverse else (lambda s: s)
    full = lambda *shape: pl.BlockSpec(shape, lambda s: (0,) * len(shape))
    in_specs = [pl.BlockSpec((left, nb, c), lambda s: (jnp.maximum(tmap(s) * (bt // left) - 1, 0), 0, 0)),
                pl.BlockSpec((bt, nb, c), lambda s: (tmap(s), 0, 0))]
    args = [u3, u3]
    if right > 0:
        in_specs.append(pl.BlockSpec((right, nb, c),
                                     lambda s: (jnp.minimum((tmap(s) + 1) * (bt // right), seq // right - 1), 0, 0)))
        args.append(u3)
    in_specs += [full(taps, c), full(1, c), full(n_tiles, kb, LANE), full(n_tiles, kb, LANE), full(1, c), full(1, c),
                 full(1, c), full(nb, c)]
    args += [conv_w, conv_b.reshape(1, c), wa, wi, b_a.reshape(1, c), b_i.reshape(1, c), lam.reshape(1, c), h0]
    if reverse:
        in_specs += [pl.BlockSpec((bt, nb, c), lambda s: (tmap(s), 0, 0)),
                     pl.BlockSpec((bt, nb, c), lambda s: (tmap(s), 0, 1))]
        args += [hs_fwd, u3]
    kern = functools.partial(_lru_pass_kernel, reverse=reverse, starts=tuple(starts), kb=kb, bt=bt, nb=nb, taps=taps)
    blk = nb * bt * c * 4 / MIB
    vmem = (2 + 2 + 3 + (4 if reverse else 0) + 4) * blk + 4 * n_tiles * kb * LANE * 2 / MIB + 8
    return pl.pallas_call(
        kern,
        grid=(nt,),
        in_specs=in_specs,
        out_specs=[pl.BlockSpec((bt, nb, c), lambda s: (tmap(s), 0, 0)),
                   pl.BlockSpec((nb, c), lambda s: (0, 0))],
        out_shape=[jax.ShapeDtypeStruct((seq, nb, c), F32),
                   jax.ShapeDtypeStruct((nb, c), F32)],
        scratch_shapes=[pltpu.VMEM((nb * bt, c), F32), pltpu.VMEM((nb * bt, c), BF16),
                        pltpu.VMEM((nb * bt, c), F32), pltpu.VMEM((nb * bt, c), F32), pltpu.VMEM((nb, c), F32)],
        compiler_params=_cparams(1, vmem),
        name="lru_bwd" if reverse else "lru_fwd",
    )(*args)


def _mixer_nat(xs, streams, mods, cache_k, cache_v, j, w_qkv, g_mix, g_q, g_k, rpb, w_o):
    heads, dh = rpb.shape[0], g_q.shape[0]
    w_qkv, w_o = w_qkv.astype(BF16), w_o.astype(BF16)
    gains = jnp.concatenate([jnp.tile(g_q * (dh ** -0.5 * LOG2E), heads), jnp.tile(g_k, heads),
                             jnp.ones((heads * dh,), F32)])[None]
    spec = dict(head_w=dh, norm_div=dh, norm_cols=2 * heads * dh, gains=gains)
    new_x, extra = [], None
    for x, st in zip(xs, streams):
        latent = not st.shared
        qkv = _proj(x, w_qkv, st, norm_g=g_mix, mod=(mods[0], mods[1]), heads=spec,
                    out_dtype=BF16 if latent else F32, name="nat_qkv")
        if latent:
            o = _nat_attention(qkv, st, cache_k, cache_v, j, rpb, dh)
        else:
            o, kc, vc = _ctx_attention(qkv, qkv, qkv, st, n_heads=heads, dq=dh, dv=dh, q_col=0, k_col=heads * dh,
                                       v_col=2 * heads * dh, emit_kv=True)
            extra = (kc, vc)
        new_x.append(_proj(o, w_o, st, res=x, gate=mods[2], name="nat_out"))
    return new_x, extra


def _mixer_lru(xs, streams, mods, state, w_in, g_mix, conv_w, conv_b, w_a, b_a, w_i, b_i, lam, w_out):
    c = conv_w.shape[1]
    block = w_a.shape[-1]
    w_in, w_out = w_in.astype(BF16), w_out.astype(BF16)
    starts, kb = _band_plan(c, block)
    wa = [_band_weights(0.5 * w_a[d], c, block, starts, kb) for d in range(2)]
    wi = [_band_weights(0.5 * w_i[d], c, block, starts, kb) for d in range(2)]
    new_x, st_out = [], None
    for x, st in zip(xs, streams):
        latent = not st.shared
        h0 = state.astype(F32) if latent else jnp.zeros((st.nb, 2, c), F32)
        u = _proj(x, w_in, st, norm_g=g_mix, mod=(mods[0], mods[1]), out_time_major=True, name="lru_in")
        hs_f, t_f = _lru_pass(u, st, conv_w, conv_b, wa[0], wi[0], b_a[0], b_i[0], lam[0], h0[:, 0], starts, kb,
                              reverse=False)
        y, t_b = _lru_pass(u, st, conv_w, conv_b, wa[1], wi[1], b_a[1], b_i[1], lam[1], h0[:, 1], starts, kb,
                           reverse=True, hs_fwd=hs_f)
        if not latent:
            st_out = jnp.stack([t_f, t_b], axis=1)
        new_x.append(_proj(y.reshape(st.seq, st.nb * c), w_out, st, res=x, gate=mods[2], x_time_major=True,
                           name="lru_out"))
    return new_x, st_out


def _mixer_mla(xs, streams, mods, cache_ckv, cache_kr, w_down, g_mix, g_qa, g_kva, w_uq, w_ukv, g_q, g_k, w_o):
    d_model = w_down.shape[0]
    q_rank, kv_rank = g_qa.shape[0], g_kva.shape[0]
    qk_dim = g_q.shape[0]
    heads = w_uq.shape[1] // qk_dim
    rope = w_down.shape[1] - q_rank - kv_rank
    nope = qk_dim - rope
    assert nope == LANE and rope <= LANE and kv_rank % LANE == 0 and q_rank % LANE == 0
    head_w = 2 * LANE
    q_pad = -q_rank % kv_rank
    kv_col = q_rank + q_pad
    tail_pad = -(kv_col + kv_rank + rope) % 512
    w_dn = jnp.concatenate([w_down[:, :q_rank], jnp.zeros((d_model, q_pad), F32),
                            w_down[:, q_rank:q_rank + kv_rank], w_down[:, q_rank + kv_rank:],
                            jnp.zeros((d_model, tail_pad), F32)], axis=1).astype(BF16)
    kr_blk = (kv_col + kv_rank) // LANE
    w_q = jnp.pad(w_uq.reshape(q_rank, heads, qk_dim), ((0, 0), (0, 0), (0, head_w - qk_dim)))
    w_q = w_q.reshape(q_rank, heads * head_w).astype(BF16)
    w_ukv, w_o = w_ukv.astype(BF16), w_o.astype(BF16)
    gq = jnp.tile(jnp.pad(g_q * (qk_dim ** -0.5 * LOG2E), (0, head_w - qk_dim)), heads)[None]
    g1, g2 = g_k[None, :nope], jnp.pad(g_k[nope:], (0, LANE - rope))[None]
    p = cache_ckv.shape[1]
    new_x, extra = [], None
    for x, st in zip(xs, streams):
        latent = not st.shared
        d = _proj(x, w_dn, st, norm_g=g_mix, mod=(mods[0], mods[1]), name="mla_down")
        q_tabs = _rope_tables(st.seq, rope, nope, head_w) if latent else None
        k_tabs = _rope_tables(st.seq, rope, 0, LANE) if latent else None
        q = _proj(d, w_q, st, x_block=(q_rank, 0), norm_g=g_qa[None],
                  heads=dict(head_w=head_w, norm_div=qk_dim, norm_cols=heads * head_w, gains=gq, tabs=q_tabs,
                             rope_tiles=(False, True)),
                  out_dtype=BF16, name="mla_uq")
        kv = _mla_kv(d, (kv_rank, kv_col // kv_rank), w_ukv, d, (LANE, kr_blk), g_kva[None], g1, g2, k_tabs, st,
                     n_heads=heads, norm_div=qk_dim, emit_xn=not latent, name="mla_ukv")
        if latent:
            k, v = kv
            cst = _Stream(st.nb, p, 0, True)
            krc = jnp.pad(cache_kr.reshape(st.nb * p, rope), ((0, 0), (0, LANE - rope)))
            kc, vc = _mla_kv(cache_ckv.reshape(st.nb * p, kv_rank), (kv_rank, 0), w_ukv, krc, (LANE, 0), None,
                             g1, g2, None, cst, n_heads=heads, norm_div=qk_dim, emit_xn=False,
                             name="mla_ukv_cache")
            o = _joint_dense_attention(q, k, v, kc, vc, st, p, n_heads=heads, dq=head_w, dv=LANE)
        else:
            k, v, ckv = kv
            o = _ctx_attention(q, k, v, st, n_heads=heads, dq=head_w, dv=LANE, q_col=0, k_col=0, v_col=0)
            kr_out = d[:, kv_col + kv_rank:kv_col + kv_rank + rope]
            extra = (ckv.reshape(st.nb, st.seq, kv_rank), kr_out.reshape(st.nb, st.seq, rope))
        new_x.append(_proj(o, w_o, st, res=x, gate=mods[2], name="mla_out"))
    return new_x, extra


def _mixer_swa(xs, streams, mods, cache_k, cache_v, j, w_qkv, g_mix, g_q, g_k, sinks, w_o):
    dh = g_q.shape[0]
    heads = sinks.shape[0]
    kvh = (w_qkv.shape[1] // dh - heads) // 2
    w_qkv, w_o = w_qkv.astype(BF16), w_o.astype(BF16)
    gains = jnp.concatenate([jnp.tile(g_q * (dh ** -0.5 * LOG2E), heads), jnp.tile(g_k, kvh),
                             jnp.ones((kvh * dh,), F32)])[None]
    sinks = sinks.astype(F32) * LOG2E
    new_x, extra = [], None
    for x, st in zip(xs, streams):
        latent = not st.shared
        tabs = _rope_tables(st.seq, dh, 0, dh) if latent else None
        if tabs is not None:
            tabs = tuple(jnp.tile(t, (1, LANE // dh)) for t in tabs)
        spec = dict(head_w=dh, norm_div=dh, norm_cols=(heads + kvh) * dh, gains=gains, tabs=tabs,
                    rope_tiles=(True,))
        qkv = _proj(x, w_qkv, st, norm_g=g_mix, mod=(mods[0], mods[1]), heads=spec,
                    out_dtype=BF16 if latent else F32, name="swa_qkv")
        if latent:
            o = _swa_attention(qkv, st, sinks, heads=heads, kvh=kvh, dh=dh, cache=(cache_k, cache_v, j))
        else:
            o, kc, vc = _swa_attention(qkv, st, sinks, heads=heads, kvh=kvh, dh=dh)
            extra = (kc, vc)
        new_x.append(_proj(o, w_o, st, res=x, gate=mods[2], name="swa_out"))
    return new_x, extra


def kernel(x_prompt, x_sample, cache_nat_k, cache_nat_v, state_lru, cache_mla_ckv, cache_mla_krope, cache_swa_k, cache_swa_v, c, c_ctx, norm_mix, norm_ffn, w_mod, b_mod, ffn_w_in, ffn_conv_w, ffn_conv_b, ffn_w_out, nat_w_qkv, nat_q_norm, nat_k_norm, nat_rpb, nat_w_o, lru_w_in, lru_conv_w, lru_conv_b, lru_w_a, lru_b_a, lru_w_i, lru_b_i, lru_lambda, lru_w_out, mla_w_down, mla_q_a_norm, mla_kv_a_norm, mla_w_uq, mla_w_ukv, mla_q_norm, mla_k_norm, mla_w_o, swa_w_qkv, swa_q_norm, swa_k_norm, swa_sinks, swa_w_o):
    bc, sc, d = x_prompt.shape
    bl, n, _ = x_sample.shape
    depth = w_mod.shape[0]
    streams = (_Stream(bc, sc, 0, True), _Stream(bl, n, 1, False))
    xs = [x_prompt.reshape(bc * sc, d), x_sample.reshape(bl * n, d)]

    n_cond = 1 + bl
    cond_rows = -(-n_cond // SUBLANE) * SUBLANE
    cond = jnp.zeros((cond_rows, d), F32).at[0].set(c_ctx).at[1:n_cond].set(c)
    mods = _modulation(cond, w_mod, b_mod)[:, :n_cond]

    nat_k_l, nat_v_l, lru_l, ckv_l, krope_l, swa_k_l, swa_v_l = [], [], [], [], [], [], []
    for l in range(depth):
        kind, j = l % 4, l // 4
        m6 = [mods[l, :, None, t * d:(t + 1) * d] for t in range(6)]
        g_mix = norm_mix[l].reshape(1, d)
        if kind == 0:
            xs, (kc, vc) = _mixer_nat(xs, streams, m6, cache_nat_k, cache_nat_v, j, nat_w_qkv[j], g_mix,
                                      nat_q_norm[j], nat_k_norm[j], nat_rpb[j], nat_w_o[j])
            nat_k_l.append(kc)
            nat_v_l.append(vc)
        elif kind == 1:
            xs, st = _mixer_lru(xs, streams, m6, state_lru[:, j], lru_w_in[j], g_mix, lru_conv_w[j], lru_conv_b[j],
                                lru_w_a[j], lru_b_a[j], lru_w_i[j], lru_b_i[j], lru_lambda[j], lru_w_out[j])
            lru_l.append(st)
        elif kind == 2:
            xs, (ckv, kr) = _mixer_mla(xs, streams, m6, cache_mla_ckv[:, j], cache_mla_krope[:, j], mla_w_down[j],
                                       g_mix, mla_q_a_norm[j], mla_kv_a_norm[j], mla_w_uq[j], mla_w_ukv[j],
                                       mla_q_norm[j], mla_k_norm[j], mla_w_o[j])
            ckv_l.append(ckv)
            krope_l.append(kr)
        else:
            xs, (kc, vc) = _mixer_swa(xs, streams, m6, cache_swa_k, cache_swa_v, j, swa_w_qkv[j], g_mix,
                                      swa_q_norm[j], swa_k_norm[j], swa_sinks[j], swa_w_o[j])
            swa_k_l.append(kc)
            swa_v_l.append(vc)
        w_in, w_out = ffn_w_in[l].astype(BF16), ffn_w_out[l].astype(BF16)
        xs = [_conv_ffn(x, st, norm_ffn[l].reshape(1, d), m6[3], m6[4], m6[5], w_in, ffn_conv_w[l],
                        ffn_conv_b[l], w_out) for x, st in zip(xs, streams)]

    return (xs[0].reshape(bc, sc, d), xs[1].reshape(bl, n, d), jnp.stack(nat_k_l, axis=1),
            jnp.stack(nat_v_l, axis=1), jnp.stack(lru_l, axis=1), jnp.stack(ckv_l, axis=1),
            jnp.stack(krope_l, axis=1), jnp.stack(swa_k_l, axis=1), jnp.stack(swa_v_l, axis=1))
```

```python
import functools

import numpy as np
import jax
import jax.numpy as jnp
from jax import lax
from jax.experimental import pallas as pl
from jax.experimental.pallas import tpu as pltpu

F32 = jnp.float32
BF16 = jnp.bfloat16

GRID_W = 64
NA_WIN_ROWS = 8
NA_WIN_COLS = 16
NA_Q_ROWS = 8
NA_K_ROWS = 16
LRU_C = 8.0
SWA_WINDOW = 128
SWA_BLOCK = 128
ROPE_BASE = 10000.0
ROPE_GROUP = 32
EPS = 1e-6
NEG = -1e30
LOG2E = float(np.log2(np.e))
LANE = 128
SUBLANE = 8
HALO = 16
MIB = 1024 * 1024
VMEM_LIMIT_CAP_MIB = 60
ROW_TILES = (1024, 512, 256, 128, 64, 32, 16)
MAX_COL_TILE = 1024
MAX_COL_TILE_RESIDUAL = 512
FFN_ROW_TILE, FFN_FF_CHUNK = 1024, 512
HEADS_PER_STEP = (4, 2, 1)
MLA_KEY_CHUNK = 1024
LRU_ROWS_PER_STEP = 256


def _cparams(n_axes, vmem_mib):
    return pltpu.CompilerParams(dimension_semantics=("arbitrary",) * n_axes,
                                vmem_limit_bytes=int(min(vmem_mib, VMEM_LIMIT_CAP_MIB) * MIB))


def _largest_divisor(n, candidates):
    for c in candidates:
        if n % c == 0:
            return c
    return n


class _Stream:
    def __init__(self, nb, seq, mod0, shared_mod):
        self.nb, self.seq, self.rows, self.mod0, self.shared = nb, seq, nb * seq, mod0, shared_mod
        self.bm = _largest_divisor(self.rows if shared_mod else seq, ROW_TILES)

    def mod_index(self, row0):
        return self.mod0 if self.shared else self.mod0 + row0 // self.seq


def _norm_mod(x, g, shift, scale):
    ms = jnp.mean(x * x, axis=-1, keepdims=True)
    return (x * lax.rsqrt(ms + EPS)) * (g * (1.0 + scale)) + shift


def _rms(x, g):
    return (x * lax.rsqrt(jnp.mean(x * x, axis=-1, keepdims=True) + EPS)) * g


def _modulation_kernel(c_ref, w_ref, b_ref, o_ref):
    c = c_ref[...]
    sc = (c * jax.nn.sigmoid(c)).astype(BF16)
    o_ref[...] = jnp.dot(sc, w_ref[...].astype(BF16), preferred_element_type=F32) + b_ref[...]


def _modulation(cond, w_mod, b_mod):
    depth, d, n = w_mod.shape
    rows = cond.shape[0]
    bn = _largest_divisor(n, (512, 256, 128))
    return pl.pallas_call(
        _modulation_kernel,
        grid=(depth, n // bn),
        in_specs=[pl.BlockSpec((rows, d), lambda l, j: (0, 0)),
                  pl.BlockSpec((None, d, bn), lambda l, j: (l, 0, j)),
                  pl.BlockSpec((None, 1, bn), lambda l, j: (l, 0, j))],
        out_specs=pl.BlockSpec((None, rows, bn), lambda l, j: (l, 0, j)),
        out_shape=jax.ShapeDtypeStruct((depth, rows, n), F32),
        compiler_params=_cparams(2, 32),
        name="modulation",
    )(cond, w_mod, b_mod.reshape(depth, 1, n))


def _rope_tables(n_tokens, rot_dim, lead, width):
    t = jnp.arange(n_tokens)
    row = (t // GRID_W).astype(F32)
    col = (t % GRID_W).astype(F32)
    half = rot_dim // 2
    inv = ROPE_BASE ** (-jnp.arange(0, half, 2, dtype=F32) / half)
    ar = row[:, None] * inv
    ac = col[:, None] * inv
    ang = jnp.concatenate([ar, ar, ac, ac], axis=-1)
    cos, sin = jnp.cos(ang), jnp.sin(ang)
    first = (np.arange(rot_dim) % ROPE_GROUP) < ROPE_GROUP // 2
    sin_a = jnp.where(first, -sin, 0.0)
    sin_b = jnp.where(first, 0.0, sin)
    pad = ((0, 0), (lead, width - lead - rot_dim))
    return (jnp.pad(cos, pad, constant_values=1.0), jnp.pad(sin_a, pad), jnp.pad(sin_b, pad))


def _rope_apply(y, cos, sin_a, sin_b):
    shift = ROPE_GROUP // 2
    return y * cos + pltpu.roll(y, LANE - shift, 1) * sin_a + pltpu.roll(y, shift, 1) * sin_b


def _rotate_half_matrix():
    shift = ROPE_GROUP // 2
    src = lax.broadcasted_iota(jnp.int32, (LANE, LANE), 0)
    dst = lax.broadcasted_iota(jnp.int32, (LANE, LANE), 1)
    first = (dst % ROPE_GROUP) < shift
    return jnp.where(first & (src == dst + shift), -1.0,
                     jnp.where(jnp.logical_not(first) & (src == dst - shift), 1.0, 0.0)).astype(BF16)


def _rope_apply_mxu(y, cos, sin_a, sin_b, perm):
    hi = y.astype(BF16)
    lo = (y - hi.astype(F32)).astype(BF16)
    rot = jnp.dot(hi, perm, preferred_element_type=F32) + jnp.dot(lo, perm, preferred_element_type=F32)
    return y * cos + rot * (sin_b - sin_a)


def _fill_lhs(x_ref, xs_ref, xn_ref, prologue, g_ref, sh_ref, sc_ref, row_chunk):
    bm = x_ref.shape[0]

    def chunk(r, carry):
        rows = pl.ds(pl.multiple_of(r * row_chunk, row_chunk), row_chunk)
        x = x_ref[rows, :].astype(F32)
        if prologue == "norm_mod":
            x = _norm_mod(x, g_ref[...], sh_ref[...], sc_ref[...])
        elif prologue == "norm":
            x = _rms(x, g_ref[...])
        if xn_ref is not None:
            xn_ref[rows, :] = x
        xs_ref[rows, :] = x.astype(BF16)
        return carry
    n_chunks = bm // row_chunk
    lax.fori_loop(0, n_chunks, chunk, 0, unroll=2 if n_chunks % 2 == 0 else 1)


def _head_norm_store(acc, o_ref, hg_ref, tabs, head_w, norm_div, col0, norm_cols, rope_tiles):
    bn = acc.shape[1]
    period = tabs[0].shape[1] if tabs is not None else LANE
    perm = _rotate_half_matrix() if tabs is not None and head_w >= LANE else None
    if head_w < LANE:
        head_of_row = lax.broadcasted_iota(jnp.int32, (LANE, LANE), 0) // head_w
        head_of_col = lax.broadcasted_iota(jnp.int32, (LANE, LANE), 1) // head_w
        same_head = jnp.where(head_of_row == head_of_col, 1.0, 0.0).astype(BF16)
    for s0 in range(0, bn, max(head_w, LANE)):
        normed = None if norm_cols is None else (col0 + s0 < norm_cols)
        tiles = [acc[:, s0 + k * LANE:s0 + (k + 1) * LANE] for k in range(max(head_w, LANE) // LANE)]
        if head_w >= LANE:
            sq = None
            for y in tiles:
                sq = y * y if sq is None else sq + y * y
            inv = lax.rsqrt(jnp.sum(sq, axis=-1, keepdims=True) / norm_div + EPS)
        else:
            y2 = tiles[0] * tiles[0]
            hi = y2.astype(BF16)
            lo = (y2 - hi.astype(F32)).astype(BF16)
            ssq = (jnp.dot(hi, same_head, preferred_element_type=F32)
                   + jnp.dot(lo, same_head, preferred_element_type=F32))
            inv = lax.rsqrt(ssq / norm_div + EPS)
        if normed is not None:
            inv = jnp.where(normed, inv, 1.0)
        for k, y in enumerate(tiles):
            c0 = s0 + k * LANE
            y = (y * inv) * hg_ref[:, c0:c0 + LANE]
            t0 = c0 % period
            if tabs is not None and rope_tiles[t0 // LANE]:
                tab = tuple(t[:, t0:t0 + LANE] for t in tabs)
                rotated = _rope_apply(y, *tab) if head_w < LANE else _rope_apply_mxu(y, *tab, perm)
                y = rotated if normed is None else jnp.where(normed, rotated, y)
            o_ref[:, c0:c0 + LANE] = y.astype(o_ref.dtype)


def _proj_kernel(*refs, prologue, emit_xn, epilogue, head_w, norm_div, norm_cols, rope, rope_tiles, row_chunk):
    it = iter(refs)
    x_ref = next(it)
    g_ref = next(it) if prologue is not None else None
    sh_ref, sc_ref = (next(it), next(it)) if prologue == "norm_mod" else (None, None)
    w_ref = next(it)
    if epilogue == "res":
        res_ref, gate_ref = next(it), next(it)
    if epilogue == "heads":
        hg_ref = next(it)
        tabs = (next(it), next(it), next(it)) if rope else None
    o_ref = next(it)
    xn_ref = next(it) if emit_xn else None
    xs_ref = next(it, None)
    j = pl.program_id(1)
    bn = o_ref.shape[1]

    if xs_ref is None:
        xs_ref = x_ref
    else:
        @pl.when(j == 0)
        def _():
            _fill_lhs(x_ref, xs_ref, xn_ref, prologue, g_ref, sh_ref, sc_ref, row_chunk)

    acc = jnp.dot(xs_ref[...], w_ref[...], preferred_element_type=F32)
    if epilogue == "res":
        o_ref[...] = res_ref[...] + gate_ref[...] * acc
    elif epilogue == "heads":
        _head_norm_store(acc, o_ref, hg_ref, tabs, head_w, norm_div, j * bn, norm_cols, rope_tiles)
    else:
        o_ref[...] = acc.astype(o_ref.dtype)


def _proj(x, w, st, *, x_block=None, norm_g=None, mod=None, res=None, gate=None, heads=None, emit_xn=False,
          out_dtype=F32, bn=None, x_time_major=False, out_time_major=False, name="proj"):
    k, n = w.shape
    time_major = x_time_major or out_time_major
    bm = min(st.bm, st.seq) if time_major else st.bm
    tiles_per_seq = st.seq // bm if st.seq % bm == 0 else None
    if time_major:
        assert tiles_per_seq is not None and x_block is None
    if x_time_major:
        assert x.shape == (st.seq, st.nb * k)
        kidx = 0
    else:
        kx, kidx = x_block if x_block is not None else (x.shape[1], 0)
        assert kx == k and x.shape[0] == st.rows
    rows = st.rows
    prologue = None if norm_g is None else ("norm_mod" if mod is not None else "norm")
    epilogue = "res" if res is not None else ("heads" if heads is not None else None)
    rope = heads is not None and heads.get("tabs") is not None
    if bn is None:
        unit = LANE
        if epilogue == "heads":
            unit = max(heads["head_w"], LANE, heads["tabs"][0].shape[1] if rope else LANE)
        cap = MAX_COL_TILE_RESIDUAL if epilogue == "res" and x.dtype != BF16 else MAX_COL_TILE
        bn = next((c for c in range(cap, unit - 1, -unit) if n % c == 0), n)
    mod_idx = lambda i: st.mod_index(i * bm)

    if x_time_major:
        in_specs = [pl.BlockSpec((bm, k), lambda i, j: (i % tiles_per_seq, i // tiles_per_seq))]
    else:
        in_specs = [pl.BlockSpec((bm, k), lambda i, j: (i, kidx))]
    args = [x]
    if prologue is not None:
        in_specs.append(pl.BlockSpec((1, k), lambda i, j: (0, 0)))
        args.append(norm_g)
    if prologue == "norm_mod":
        in_specs += [pl.BlockSpec((None, 1, k), lambda i, j: (mod_idx(i), 0, 0))] * 2
        args += list(mod)
    in_specs.append(pl.BlockSpec((k, bn), lambda i, j: (0, j)))
    args.append(w.astype(BF16))
    if epilogue == "res":
        in_specs += [pl.BlockSpec((bm, bn), lambda i, j: (i, j)),
                     pl.BlockSpec((None, 1, bn), lambda i, j: (mod_idx(i), 0, j))]
        args += [res, gate]
    head_w = norm_div = 0
    norm_cols = rope_tiles = None
    if epilogue == "heads":
        head_w, norm_div = heads["head_w"], heads["norm_div"]
        norm_cols = heads["norm_cols"] if heads["norm_cols"] < n else None
        assert bn % max(head_w, LANE) == 0 and heads["norm_cols"] % max(head_w, LANE) == 0
        in_specs.append(pl.BlockSpec((1, bn), lambda i, j: (0, j)))
        args.append(heads["gains"])
        if rope:
            period = heads["tabs"][0].shape[1]
            rope_tiles = heads["rope_tiles"]
            assert bn % period == 0 and tiles_per_seq is not None and len(rope_tiles) == period // LANE
            in_specs += [pl.BlockSpec((bm, period), lambda i, j: (i % tiles_per_seq, 0))] * 3
            args += list(heads["tabs"])
    if out_time_major:
        n_col_tiles = n // bn
        out_shape = [jax.ShapeDtypeStruct((st.seq, st.nb * n), out_dtype)]
        out_specs = [pl.BlockSpec((bm, bn), lambda i, j: (i % tiles_per_seq, (i // tiles_per_seq) * n_col_tiles + j))]
    else:
        out_shape = [jax.ShapeDtypeStruct((rows, n), out_dtype)]
        out_specs = [pl.BlockSpec((bm, bn), lambda i, j: (i, j))]
    if emit_xn:
        out_shape.append(jax.ShapeDtypeStruct((rows, k), F32))
        out_specs.append(pl.BlockSpec((bm, k), lambda i, j: (i, 0)))
    xbytes = x.dtype.itemsize
    vmem = (2 * bm * k * xbytes + bm * k * 2 + 2 * k * bn * 2 + (6 if epilogue == "res" else 4) * bm * bn * 4
            + (2 * bm * k * 4 if emit_xn else 0)) / MIB + 8
    kern = functools.partial(_proj_kernel, prologue=prologue, emit_xn=emit_xn, epilogue=epilogue, head_w=head_w,
                             norm_div=norm_div, norm_cols=norm_cols, rope=rope, rope_tiles=rope_tiles,
                             row_chunk=min(bm, 128))
    direct_lhs = prologue is None and x.dtype == BF16 and not emit_xn
    out = pl.pallas_call(
        kern,
        grid=(rows // bm, n // bn),
        in_specs=in_specs,
        out_specs=out_specs,
        out_shape=out_shape,
        scratch_shapes=[] if direct_lhs else [pltpu.VMEM((bm, k), BF16)],
        compiler_params=_cparams(2, vmem),
        name=name,
    )(*args)
    return out if emit_xn else out[0]


def _ffn_kernel(xp_ref, x_ref, xn_ref, g_ref, sh_ref, sc_ref, gate_ref, wa_ref, wb_ref, cw_ref, cb_ref,
                wo_ref, o_ref, h_ref, *, bm, seq, row_chunk):
    i = pl.program_id(0)
    c = pl.program_id(1)
    n_chunks = pl.num_programs(1)

    @pl.when(c == 0)
    def _():
        g, sh, sc = g_ref[...], sh_ref[...], sc_ref[...]
        h_ref[0:HALO, :] = _norm_mod(xp_ref[...], g, sh, sc).astype(BF16)
        h_ref[HALO + bm:, :] = _norm_mod(xn_ref[...], g, sh, sc).astype(BF16)

        def chunk(r, carry):
            src = pl.ds(pl.multiple_of(r * row_chunk, row_chunk), row_chunk)
            dst = pl.ds(pl.multiple_of(HALO + r * row_chunk, HALO), row_chunk)
            h_ref[dst, :] = _norm_mod(x_ref[src, :], g, sh, sc).astype(BF16)
            return carry
        n_row_chunks = bm // row_chunk
        lax.fori_loop(0, n_row_chunks, chunk, 0, unroll=2 if n_row_chunks % 2 == 0 else 1)
        o_ref[...] = jnp.zeros_like(o_ref)

    ua = jnp.dot(h_ref[...], wa_ref[...], preferred_element_type=F32)
    ub = jnp.dot(h_ref[HALO:HALO + bm, :], wb_ref[...], preferred_element_type=F32)
    n_all = bm + 2 * HALO
    u_prev = pltpu.roll(ua, 1, 0)[HALO:HALO + bm]
    u_next = pltpu.roll(ua, n_all - 1, 0)[HALO:HALO + bm]
    u_mid = ua[HALO:HALO + bm]
    pos = jnp.bitwise_and(i * bm + lax.broadcasted_iota(jnp.int32, (bm, 1), 0), seq - 1)
    u_prev = jnp.where(pos == 0, 0.0, u_prev)
    u_next = jnp.where(pos == seq - 1, 0.0, u_next)
    cw = cw_ref[...]
    a = cb_ref[...] + u_prev * cw[0:1] + u_mid * cw[1:2] + u_next * cw[2:3]
    gated = ((a * jax.nn.sigmoid(a)) * ub).astype(BF16)
    o_ref[...] += jnp.dot(gated, wo_ref[...], preferred_element_type=F32)

    @pl.when(c == n_chunks - 1)
    def _():
        o_ref[...] = x_ref[...] + gate_ref[...] * o_ref[...]


def _conv_ffn(x, st, g, shift, scale, gate, w_in, conv_w, conv_b, w_out, bm=FFN_ROW_TILE, ck=FFN_FF_CHUNK):
    m, d = x.shape
    d_ff = w_out.shape[0]
    bm = min(st.bm, bm)
    ck = _largest_divisor(d_ff, tuple(c for c in (512, 256, 128) if c <= ck))
    n_chunks = d_ff // ck
    n_halo_blocks = m // HALO
    assert st.seq & (st.seq - 1) == 0 and conv_w.shape[0] == 3
    mod_idx = lambda i: st.mod_index(i * bm)
    kern = functools.partial(_ffn_kernel, bm=bm, seq=st.seq, row_chunk=min(bm, 128))
    vmem = (4 * bm * d * 4 + (bm + 2 * HALO) * d * 2 + 6 * d * ck * 2 + 5 * (bm + 2 * HALO) * ck * 4) / MIB + 4
    return pl.pallas_call(
        kern,
        grid=(m // bm, n_chunks),
        in_specs=[
            pl.BlockSpec((HALO, d), lambda i, c: (jnp.maximum(i * (bm // HALO) - 1, 0), 0)),
            pl.BlockSpec((bm, d), lambda i, c: (i, 0)),
            pl.BlockSpec((HALO, d), lambda i, c: (jnp.minimum((i + 1) * (bm // HALO), n_halo_blocks - 1), 0)),
            pl.BlockSpec((1, d), lambda i, c: (0, 0)),
            pl.BlockSpec((None, 1, d), lambda i, c: (mod_idx(i), 0, 0)),
            pl.BlockSpec((None, 1, d), lambda i, c: (mod_idx(i), 0, 0)),
            pl.BlockSpec((None, 1, d), lambda i, c: (mod_idx(i), 0, 0)),
            pl.BlockSpec((d, ck), lambda i, c: (0, c)),
            pl.BlockSpec((d, ck), lambda i, c: (0, n_chunks + c)),
            pl.BlockSpec((conv_w.shape[0], ck), lambda i, c: (0, c)),
            pl.BlockSpec((1, ck), lambda i, c: (0, c)),
            pl.BlockSpec((ck, d), lambda i, c: (c, 0)),
        ],
        out_specs=pl.BlockSpec((bm, d), lambda i, c: (i, 0)),
        out_shape=jax.ShapeDtypeStruct((m, d), F32),
        scratch_shapes=[pltpu.VMEM((bm + 2 * HALO, d), BF16)],
        compiler_params=_cparams(2, vmem),
        name="conv_ffn",
    )(x, x, x, g, shift, scale, gate, w_in, w_in, conv_w, conv_b.reshape(1, d_ff), w_out)


def _qk(q, k):
    return lax.dot_general(q, k, (((1,), (1,)), ((), ())), preferred_element_type=F32)


def _attend(scores, values, sink=None):
    m = None
    for s in scores:
        mi = jnp.max(s, axis=-1, keepdims=True)
        m = mi if m is None else jnp.maximum(m, mi)
    if sink is not None:
        m = jnp.maximum(m, sink)
    es = [jnp.exp2(s - m) for s in scores]
    den = None
    for e in es:
        di = jnp.sum(e, axis=-1, keepdims=True)
        den = di if den is None else den + di
    if sink is not None:
        den = den + jnp.exp2(sink - m)
    out = None
    for e, v in zip(es, values):
        oi = jnp.dot(e.astype(BF16), v, preferred_element_type=F32)
        out = oi if out is None else out + oi
    return out * (1.0 / den)


def _ctx_attn_kernel(q_ref, k_ref, v_ref, *outs, heads, dq, dv, emit_kv):
    o_ref = outs[0]
    for h in range(heads):
        q = q_ref[:, h * dq:(h + 1) * dq].astype(BF16)
        k = k_ref[:, h * dq:(h + 1) * dq]
        v = v_ref[:, h * dv:(h + 1) * dv]
        if emit_kv:
            outs[1][h] = k.astype(F32)
            outs[2][h] = v.astype(F32)
        o = _attend([_qk(q, k.astype(BF16))], [v.astype(BF16)])
        o_ref[:, h * dv:(h + 1) * dv] = o.astype(o_ref.dtype)


def _ctx_attention(qm, km, vm, st, *, n_heads, dq, dv, q_col, k_col, v_col, emit_kv=False):
    hb = _largest_divisor(n_heads, HEADS_PER_STEP)
    s = st.seq
    assert q_col % (hb * dq) == 0 and k_col % (hb * dq) == 0 and v_col % (hb * dv) == 0
    qo, ko, vo = q_col // (hb * dq), k_col // (hb * dq), v_col // (hb * dv)
    out_shape = [jax.ShapeDtypeStruct((st.rows, n_heads * dv), BF16)]
    out_specs = [pl.BlockSpec((s, hb * dv), lambda b, g: (b, g))]
    if emit_kv:
        out_shape += [jax.ShapeDtypeStruct((st.nb, n_heads, s, dq), F32),
                      jax.ShapeDtypeStruct((st.nb, n_heads, s, dv), F32)]
        out_specs += [pl.BlockSpec((None, hb, s, dq), lambda b, g: (b, g, 0, 0)),
                      pl.BlockSpec((None, hb, s, dv), lambda b, g: (b, g, 0, 0))]
    out = pl.pallas_call(
        functools.partial(_ctx_attn_kernel, heads=hb, dq=dq, dv=dv, emit_kv=emit_kv),
        grid=(st.nb, n_heads // hb),
        in_specs=[pl.BlockSpec((s, hb * dq), lambda b, g: (b, qo + g)),
                  pl.BlockSpec((s, hb * dq), lambda b, g: (b, ko + g)),
                  pl.BlockSpec((s, hb * dv), lambda b, g: (b, vo + g))],
        out_specs=out_specs,
        out_shape=out_shape,
        compiler_params=_cparams(2, 32),
        name="ctx_attention",
    )(qm, km, vm)
    return out if emit_kv else out[0]


def _nat_kernel(q_ref, k_ref, v_ref, kc_ref, vc_ref, bias_ref, o_ref, *, key_rows, rows, heads, dh):
    i = pl.program_id(2)
    n_keys = key_rows * GRID_W
    first_row = jnp.clip(i * NA_Q_ROWS - NA_WIN_ROWS // 2, 0, rows - key_rows)
    start = pl.multiple_of(first_row * GRID_W, GRID_W * 4)
    for h in range(heads):
        lanes = slice(h * dh, (h + 1) * dh)
        q = q_ref[:, lanes]
        k = k_ref[pl.ds(start, n_keys), lanes]
        v = v_ref[pl.ds(start, n_keys), lanes]
        s_loc = _qk(q, k) + bias_ref[h]
        s_ctx = _qk(q, kc_ref[h].astype(BF16))
        o_ref[:, lanes] = _attend([s_loc, s_ctx], [v, vc_ref[h].astype(BF16)]).astype(o_ref.dtype)


def _nat_bias(rpb, rows):
    n_blocks = rows // NA_Q_ROWS
    key_rows = min(NA_K_ROWS, rows)
    wr = min(NA_WIN_ROWS, rows)
    reps = [0, min(1, n_blocks - 1), n_blocks - 1]
    heads = rpb.shape[0]
    nq, nk = NA_Q_ROWS * GRID_W, key_rows * GRID_W
    shape = (NA_Q_ROWS, GRID_W, key_rows, GRID_W)
    qc = np.arange(GRID_W)
    cstart = np.clip(qc - NA_WIN_COLS // 2, 0, GRID_W - NA_WIN_COLS)
    col_ok = (qc[None, :] >= cstart[:, None]) & (qc[None, :] < cstart[:, None] + NA_WIN_COLS)
    rp = jnp.pad(rpb.astype(F32) * LOG2E,
                 ((0, 0), (key_rows, key_rows), (GRID_W - NA_WIN_COLS, GRID_W - NA_WIN_COLS)))
    row_slabs, mask_l = [], []
    for i in reps:
        ks = int(np.clip(i * NA_Q_ROWS - NA_WIN_ROWS // 2, 0, rows - key_rows))
        r = i * NA_Q_ROWS + np.arange(NA_Q_ROWS)
        rs = np.clip(r - wr // 2, 0, rows - wr)
        kr = ks + np.arange(key_rows)
        row_ok = (kr[None, :] >= rs[:, None]) & (kr[None, :] < rs[:, None] + wr)
        for rq in range(NA_Q_ROWS):
            first = ks - int(r[rq]) + NA_WIN_ROWS - 1 + key_rows
            assert 0 <= first and first + key_rows <= rp.shape[1]
            row_slabs.append(rp[:, first:first + key_rows, :])
        mask_l.append(np.broadcast_to(row_ok[:, None, :, None] & col_ok[None, :, None, :], shape).reshape(nq, nk))
    slab = jnp.stack(row_slabs, axis=1).reshape(heads, len(reps), NA_Q_ROWS, key_rows, 2 * GRID_W - 1)
    toep = jnp.stack([slab[..., GRID_W - 1 - c:2 * GRID_W - 1 - c] for c in range(GRID_W)], axis=3)
    bias = toep.reshape(heads, len(reps), nq, nk)
    return jnp.where(jnp.asarray(np.stack(mask_l))[None], bias, NEG)


def _nat_attention(qkv, st, cache_k, cache_v, j, rpb, dh):
    heads = rpb.shape[0]
    n = st.seq
    p = cache_k.shape[3]
    rows = n // GRID_W
    assert rows % NA_Q_ROWS == 0 and rows >= NA_K_ROWS and dh % LANE == 0
    n_blocks = rows // NA_Q_ROWS
    key_rows = min(NA_K_ROWS, rows)
    nq, nk = NA_Q_ROWS * GRID_W, key_rows * GRID_W
    bias = _nat_bias(rpb, rows)
    btype = lambda i: jnp.where(i == 0, 0, jnp.where(i == n_blocks - 1, 2, 1))
    hb = _largest_divisor(heads, HEADS_PER_STEP)
    hg = heads // hb
    kern = functools.partial(_nat_kernel, key_rows=key_rows, rows=rows, heads=hb, dh=dh)
    return pl.pallas_call(
        kern,
        grid=(st.nb, hg, n_blocks),
        in_specs=[pl.BlockSpec((nq, hb * dh), lambda b, h, i: (b * n_blocks + i, h)),
                  pl.BlockSpec((n, hb * dh), lambda b, h, i: (b, hg + h)),
                  pl.BlockSpec((n, hb * dh), lambda b, h, i: (b, 2 * hg + h)),
                  pl.BlockSpec((None, None, hb, p, dh), lambda b, h, i: (b, j, h, 0, 0)),
                  pl.BlockSpec((None, None, hb, p, dh), lambda b, h, i: (b, j, h, 0, 0)),
                  pl.BlockSpec((hb, None, nq, nk), lambda b, h, i: (h, btype(i), 0, 0))],
        out_specs=pl.BlockSpec((nq, hb * dh), lambda b, h, i: (b * n_blocks + i, h)),
        out_shape=jax.ShapeDtypeStruct((st.rows, heads * dh), BF16),
        compiler_params=_cparams(3, 56),
        name="nat_attention",
    )(qkv, qkv, qkv, cache_k, cache_v, bias)


def _joint_dense_kernel(q_ref, k_ref, v_ref, kc_ref, vc_ref, o_ref, *, chunk):
    q = q_ref[...]
    n = k_ref.shape[0]
    pieces = [(k_ref, v_ref, c0, min(chunk, n - c0)) for c0 in range(0, n, chunk)]
    pieces.append((kc_ref, vc_ref, 0, kc_ref.shape[0]))
    m = den = acc = None
    for kr, vr, c0, size in pieces:
        s = _qk(q, kr[c0:c0 + size, :])
        mc = jnp.max(s, axis=-1, keepdims=True)
        m_new = mc if m is None else jnp.maximum(m, mc)
        e = jnp.exp2(s - m_new)
        dc = jnp.sum(e, axis=-1, keepdims=True)
        pv = jnp.dot(e.astype(BF16), vr[c0:c0 + size, :], preferred_element_type=F32)
        if m is None:
            den, acc = dc, pv
        else:
            alpha = jnp.exp2(m - m_new)
            den, acc = alpha * den + dc, alpha * acc + pv
        m = m_new
    o_ref[...] = (acc * (1.0 / den)).astype(o_ref.dtype)


def _joint_dense_attention(qm, km, vm, kcm, vcm, st, p, *, n_heads, dq, dv):
    n = st.seq
    bq = _largest_divisor(n, (1024, 512, 256, 128, 64, 32, 16))
    nqb = n // bq
    return pl.pallas_call(
        functools.partial(_joint_dense_kernel, chunk=MLA_KEY_CHUNK),
        grid=(st.nb, n_heads, nqb),
        in_specs=[pl.BlockSpec((bq, dq), lambda b, h, i: (b * nqb + i, h)),
                  pl.BlockSpec((n, dq), lambda b, h, i: (b, h)),
                  pl.BlockSpec((n, dv), lambda b, h, i: (b, h)),
                  pl.BlockSpec((p, dq), lambda b, h, i: (b, h)),
                  pl.BlockSpec((p, dv), lambda b, h, i: (b, h))],
        out_specs=pl.BlockSpec((bq, dv), lambda b, h, i: (b * nqb + i, h)),
        out_shape=jax.ShapeDtypeStruct((st.rows, n_heads * dv), BF16),
        compiler_params=_cparams(3, 48),
        name="mla_attention",
    )(qm, km, vm, kcm, vcm)


def _both_halves(x, s):
    x = x.astype(F32)
    low = lax.broadcasted_iota(jnp.int32, (1, LANE), 1) < LANE // 2
    keep = low if s == 0 else jnp.logical_not(low)
    return jnp.where(keep, x, pltpu.roll(x, LANE // 2, 1)).astype(BF16)


def _swa_step(sinks_ref, pair, q_ref, k, v, kc, vc, o_ref, *, dh, groups, local_bias):
    kv_per_step = LANE // dh
    assert kv_per_step == 2 and groups % 2 == 0
    rows = q_ref.shape[0]
    low = lax.broadcasted_iota(jnp.int32, (1, LANE), 1) < dh
    row_group = lax.broadcasted_iota(jnp.int32, (groups * rows, 1), 0) // rows
    for s in range(kv_per_step):
        kd, vd = _both_halves(k, s), _both_halves(v, s)
        q_parts = []
        for g in range(groups):
            c0 = ((s * groups + g) * dh // LANE) * LANE
            tile = q_ref[:, c0:c0 + LANE].astype(BF16)
            q_parts.append(jnp.where(low if g % 2 == 0 else jnp.logical_not(low), tile, jnp.zeros_like(tile)))
        q = jnp.concatenate(q_parts, axis=0)
        sink = jnp.zeros((groups * rows, 1), F32)
        for g in range(groups):
            sink = jnp.where(row_group == g, sinks_ref[(pair * kv_per_step + s) * groups + g], sink)
        s_loc = _qk(q, kd)
        if local_bias is not None:
            s_loc = s_loc + jnp.concatenate([local_bias] * groups, axis=0)
        if kc is not None:
            kcd = jnp.concatenate([kc[s], kc[s]], axis=-1).astype(BF16)
            vcd = jnp.concatenate([vc[s], vc[s]], axis=-1).astype(BF16)
            out = _attend([s_loc, _qk(q, kcd)], [vd, vcd], sink)
        else:
            out = _attend([s_loc], [vd], sink)
        for g in range(0, groups, 2):
            c0 = (s * groups + g) * dh
            o_ref[:, c0:c0 + LANE] = jnp.where(low, out[g * rows:(g + 1) * rows],
                                               out[(g + 1) * rows:(g + 2) * rows]).astype(o_ref.dtype)


def _swa_ctx_kernel(sinks_ref, q_ref, k_ref, v_ref, o_ref, ko_ref, vo_ref, *, dh, groups):
    pair = pl.program_id(1)
    k, v = k_ref[...], v_ref[...]
    for s in range(LANE // dh):
        ko_ref[s] = k[:, s * dh:(s + 1) * dh].astype(F32)
        vo_ref[s] = v[:, s * dh:(s + 1) * dh].astype(F32)
    _swa_step(sinks_ref, pair, q_ref, k, v, None, None, o_ref, dh=dh, groups=groups, local_bias=None)


def _swa_lat_kernel(sinks_ref, q_ref, k_ref, v_ref, kc_ref, vc_ref, o_ref, *, dh, groups, n):
    pair = pl.program_id(1)
    blk = pl.program_id(2)
    n_keys = min(3 * SWA_BLOCK, n)
    start = pl.multiple_of(jnp.clip((blk - 1) * SWA_BLOCK, 0, n - n_keys), SWA_BLOCK)
    k = k_ref[pl.ds(start, n_keys), :]
    v = v_ref[pl.ds(start, n_keys), :]
    qpos = blk * SWA_BLOCK + lax.broadcasted_iota(jnp.int32, (SWA_BLOCK, 1), 0)
    kpos = start + lax.broadcasted_iota(jnp.int32, (1, n_keys), 1)
    bias = jnp.where(jnp.abs(qpos - kpos) <= SWA_WINDOW, 0.0, NEG)
    _swa_step(sinks_ref, pair, q_ref, k, v, kc_ref, vc_ref, o_ref, dh=dh, groups=groups, local_bias=bias)


def _swa_attention(qkv, st, sinks, *, heads, kvh, dh, cache=None):
    groups = heads // kvh
    kv_per_step = LANE // dh
    assert LANE % dh == 0 and kvh % kv_per_step == 0 and groups % kv_per_step == 0
    pairs = kvh // kv_per_step
    qw = kv_per_step * groups * dh
    k_blk = heads * dh // LANE
    v_blk = (heads + kvh) * dh // LANE
    n = st.seq
    common = dict(dh=dh, groups=groups)
    smem = pl.BlockSpec(memory_space=pltpu.SMEM)
    if cache is None:
        out = pl.pallas_call(
            functools.partial(_swa_ctx_kernel, **common),
            grid=(st.nb, pairs),
            in_specs=[smem,
                      pl.BlockSpec((n, qw), lambda b, c: (b, c)),
                      pl.BlockSpec((n, LANE), lambda b, c: (b, k_blk + c)),
                      pl.BlockSpec((n, LANE), lambda b, c: (b, v_blk + c))],
            out_specs=[pl.BlockSpec((n, qw), lambda b, c: (b, c)),
                       pl.BlockSpec((None, kv_per_step, n, dh), lambda b, c: (b, c, 0, 0)),
                       pl.BlockSpec((None, kv_per_step, n, dh), lambda b, c: (b, c, 0, 0))],
            out_shape=[jax.ShapeDtypeStruct((st.rows, heads * dh), BF16),
                       jax.ShapeDtypeStruct((st.nb, kvh, n, dh), F32),
                       jax.ShapeDtypeStruct((st.nb, kvh, n, dh), F32)],
            compiler_params=_cparams(2, 32),
            name="swa_ctx_attention",
        )(sinks, qkv, qkv, qkv)
        return out
    cache_k, cache_v, j = cache
    p = cache_k.shape[3]
    nblk = n // SWA_BLOCK
    assert n % SWA_BLOCK == 0
    return pl.pallas_call(
        functools.partial(_swa_lat_kernel, n=n, **common),
        grid=(st.nb, pairs, nblk),
        in_specs=[smem,
                  pl.BlockSpec((SWA_BLOCK, qw), lambda b, c, i: (b * nblk + i, c)),
                  pl.BlockSpec((n, LANE), lambda b, c, i: (b, k_blk + c)),
                  pl.BlockSpec((n, LANE), lambda b, c, i: (b, v_blk + c)),
                  pl.BlockSpec((None, None, kv_per_step, p, dh), lambda b, c, i: (b, j, c, 0, 0)),
                  pl.BlockSpec((None, None, kv_per_step, p, dh), lambda b, c, i: (b, j, c, 0, 0))],
        out_specs=pl.BlockSpec((SWA_BLOCK, qw), lambda b, c, i: (b * nblk + i, c)),
        out_shape=jax.ShapeDtypeStruct((st.rows, heads * dh), BF16),
        compiler_params=_cparams(3, 32),
        name="swa_attention",
    )(sinks, qkv, qkv, qkv, cache_k, cache_v)


def _mla_kv_kernel(*refs, norm, emit_xn, rope, row_chunk, norm_div):
    it = iter(refs)
    x_ref = next(it)
    g_ref = next(it) if norm else None
    w_ref, kr_ref, g1_ref, g2_ref = next(it), next(it), next(it), next(it)
    tabs = (next(it), next(it), next(it)) if rope else None
    k_ref, v_ref = next(it), next(it)
    xn_ref = next(it) if emit_xn else None
    xs_ref = next(it)

    @pl.when(pl.program_id(1) == 0)
    def _():
        _fill_lhs(x_ref, xs_ref, xn_ref, "norm" if norm else None, g_ref, None, None, row_chunk)

    acc = jnp.dot(xs_ref[...], w_ref[...], preferred_element_type=F32)
    kr = kr_ref[...]
    kr_ssq = jnp.sum(kr * kr, axis=-1, keepdims=True)
    shared = kr * g2_ref[...]
    if rope:
        shared = _rope_apply(shared, *(t[...] for t in tabs))
    for h in range(acc.shape[1] // (2 * LANE)):
        nope = acc[:, 2 * h * LANE:(2 * h + 1) * LANE]
        inv = lax.rsqrt((jnp.sum(nope * nope, axis=-1, keepdims=True) + kr_ssq) / norm_div + EPS)
        k_ref[:, 2 * h * LANE:(2 * h + 1) * LANE] = ((nope * inv) * g1_ref[...]).astype(k_ref.dtype)
        k_ref[:, (2 * h + 1) * LANE:(2 * h + 2) * LANE] = (shared * inv).astype(k_ref.dtype)
        v_ref[:, h * LANE:(h + 1) * LANE] = acc[:, (2 * h + 1) * LANE:(2 * h + 2) * LANE].astype(v_ref.dtype)


def _mla_kv(x, x_block, w_ukv, kr, kr_block, g_kva, g1, g2, tabs, st, *, n_heads, norm_div, emit_xn, name):
    rows = x.shape[0]
    k, n = w_ukv.shape
    head_n = n // n_heads
    assert head_n == 2 * LANE, "nope and value widths must both be one lane tile"
    kx, kidx = x_block
    krw, kridx = kr_block
    assert kx == k and krw == LANE
    bm = st.bm
    norm = g_kva is not None
    rope = tabs is not None
    in_specs = [pl.BlockSpec((bm, k), lambda i, h: (i, kidx))]
    args = [x]
    if norm:
        in_specs.append(pl.BlockSpec((1, k), lambda i, h: (0, 0)))
        args.append(g_kva)
    hb = _largest_divisor(n_heads, HEADS_PER_STEP)
    in_specs += [pl.BlockSpec((k, hb * head_n), lambda i, h: (0, h)),
                 pl.BlockSpec((bm, LANE), lambda i, h: (i, kridx)),
                 pl.BlockSpec((1, LANE), lambda i, h: (0, 0)),
                 pl.BlockSpec((1, LANE), lambda i, h: (0, 0))]
    args += [w_ukv, kr, g1, g2]
    if rope:
        tiles_per_seq = st.seq // bm
        in_specs += [pl.BlockSpec((bm, LANE), lambda i, h: (i % tiles_per_seq, 0))] * 3
        args += list(tabs)
    out_shape = [jax.ShapeDtypeStruct((rows, n_heads * 2 * LANE), BF16),
                 jax.ShapeDtypeStruct((rows, n_heads * LANE), BF16)]
    out_specs = [pl.BlockSpec((bm, hb * 2 * LANE), lambda i, h: (i, h)),
                 pl.BlockSpec((bm, hb * LANE), lambda i, h: (i, h))]
    if emit_xn:
        out_shape.append(jax.ShapeDtypeStruct((rows, k), F32))
        out_specs.append(pl.BlockSpec((bm, k), lambda i, h: (i, 0)))
    kern = functools.partial(_mla_kv_kernel, norm=norm, emit_xn=emit_xn, rope=rope, row_chunk=min(bm, 128),
                             norm_div=norm_div)
    return pl.pallas_call(
        kern,
        grid=(rows // bm, n_heads // hb),
        in_specs=in_specs,
        out_specs=out_specs,
        out_shape=out_shape,
        scratch_shapes=[pltpu.VMEM((bm, k), BF16)],
        compiler_params=_cparams(2, 40),
        name=name,
    )(*args)


def _band_plan(width, block):
    n_tiles = width // LANE
    lo = [((t * LANE) // block) * block for t in range(n_tiles)]
    hi = [(((t + 1) * LANE - 1) // block + 1) * block for t in range(n_tiles)]
    start = [(l // LANE) * LANE for l in lo]
    kb = max(-(-(h - s) // LANE) * LANE for h, s in zip(hi, start))
    kb = min(kb, width)
    start = [min(s, width - kb) for s in start]
    return start, kb


def _band_weights(w, width, block, start, kb):
    n_tiles = width // LANE
    wb = w.astype(BF16)
    tiles = []
    for t in range(n_tiles):
        pieces = []
        col = t * LANE
        while col < (t + 1) * LANE:
            blk = col // block
            col_end = min((blk + 1) * block, (t + 1) * LANE)
            sub = wb[blk, :, col - blk * block:col_end - blk * block]
            top = blk * block - start[t]
            pieces.append(jnp.pad(sub, ((top, kb - top - block), (0, 0))))
            col = col_end
        tiles.append(jnp.concatenate(pieces, axis=1))
    return jnp.stack(tiles)


def _gelu_tanh(x):
    cdf = 0.5 * (1.0 + jnp.tanh(np.float32(np.sqrt(2.0 / np.pi)) * (x + 0.044715 * (x * x * x))))
    return x * cdf


def _lru_pass_kernel(*refs, reverse, starts, kb, bt, nb, taps):
    left = taps // 2
    right = taps - 1 - left
    it = iter(refs)
    if reverse:
        conv_ref = next(it)
    else:
        xp_ref, x_ref = next(it), next(it)
        xn_ref = next(it) if right > 0 else None
        cw_ref, cb_ref = next(it), next(it)
    wa_ref, wi_ref, ba_ref, bi_ref, lam_ref, h0_ref = (next(it) for _ in range(6))
    hsf_ref, gate_ref = (next(it), next(it)) if reverse else (None, None)
    out_ref, ht_ref = next(it), next(it)
    conv_out_ref = None if reverse else next(it)
    xc_s, xb_s, a_s, bx_s, carry = next(it), next(it), next(it), next(it), next(it)

    step = pl.program_id(0)
    n_steps = pl.num_programs(0)
    n_tiles = len(starts)
    rows = nb * bt

    @pl.when(step == 0)
    def _():
        carry[...] = h0_ref[...]

    for t in range(n_tiles):
        lanes = slice(t * LANE, (t + 1) * LANE)
        if reverse:
            acc = conv_ref[:, :, lanes].reshape(rows, LANE)
        else:
            parts = [jnp.where(step > 0, xp_ref[:, :, lanes], 0.0), x_ref[:, :, lanes]]
            if right > 0:
                parts.append(jnp.where(step < n_steps - 1, xn_ref[:, :, lanes], 0.0))
            full = jnp.concatenate(parts, axis=0)
            acc = jnp.broadcast_to(cb_ref[:, lanes], (bt, nb, LANE))
            for k in range(taps):
                acc = acc + full[k:k + bt] * cw_ref[k:k + 1, lanes]
            conv_out_ref[:, :, lanes] = acc
            acc = acc.reshape(rows, LANE)
        xc_s[:, lanes] = acc
        xb_s[:, lanes] = acc.astype(BF16)

    neg_lam = -lam_ref[...]
    softplus = jnp.maximum(neg_lam, 0.0) + jnp.log1p(jnp.exp(-jnp.abs(neg_lam)))
    half_rate = (-0.5 * LRU_C) * softplus
    half_ba, half_bi = 0.5 * ba_ref[...], 0.5 * bi_ref[...]
    for t in range(n_tiles):
        lanes = slice(t * LANE, (t + 1) * LANE)
        xw = xb_s[:, starts[t]:starts[t] + kb]
        tanh_a = jnp.tanh(jnp.dot(xw, wa_ref[t], preferred_element_type=F32) + half_ba[:, lanes])
        tanh_i = jnp.tanh(jnp.dot(xw, wi_ref[t], preferred_element_type=F32) + half_bi[:, lanes])
        log_a = half_rate[:, lanes] * tanh_a + half_rate[:, lanes]
        a = jnp.exp(log_a)
        half_x = 0.5 * xc_s[:, lanes]
        a_s[:, lanes] = a
        bx_s[:, lanes] = jnp.sqrt(-jnp.tanh(log_a) * (a * a + 1.0)) * (half_x * tanh_i + half_x)

    h = carry[...]
    for s in range(bt):
        ts = (bt - 1 - s) if reverse else s
        slab = slice(ts * nb, (ts + 1) * nb)
        h = a_s[slab, :] * h + bx_s[slab, :]
        a_s[slab, :] = h
    carry[...] = h
    ht_ref[...] = h
    hs = a_s[...].reshape(bt, nb, a_s.shape[1])
    if reverse:
        out_ref[...] = (_gelu_tanh(gate_ref[...]) * (hsf_ref[...] + hs)).astype(out_ref.dtype)
    else:
        out_ref[...] = hs


def _lru_pass(u, st, conv_w, conv_b, wa, wi, b_a, b_i, lam, h0, starts, kb, *, reverse, fwd=None):
    c = conv_w.shape[1]
    taps = conv_w.shape[0]
    left, right = taps // 2, taps - 1 - taps // 2
    nb, seq = st.nb, st.seq
    bt = min(max(LRU_ROWS_PER_STEP // nb, SUBLANE), seq)
    assert seq % bt == 0 and c % LANE == 0 and left > 0 and bt % left == 0 and (right == 0 or bt % right == 0)
    nt = seq // bt
    n_tiles = c // LANE
    u3 = u.reshape(seq, nb, 2 * c)
    tmap = (lambda s: nt - 1 - s) if reverse else (lambda s: s)
    full = lambda *shape: pl.BlockSpec(shape, lambda s: (0,) * len(shape))
    tile_spec = lambda col: pl.BlockSpec((bt, nb, c), lambda s: (tmap(s), 0, col))
    if reverse:
        in_specs, args = [tile_spec(0)], [fwd[1]]
    else:
        in_specs = [pl.BlockSpec((left, nb, c), lambda s: (jnp.maximum(s * (bt // left) - 1, 0), 0, 0)), tile_spec(0)]
        args = [u3, u3]
        if right > 0:
            in_specs.append(pl.BlockSpec((right, nb, c),
                                         lambda s: (jnp.minimum((s + 1) * (bt // right), seq // right - 1), 0, 0)))
            args.append(u3)
        in_specs += [full(taps, c), full(1, c)]
        args += [conv_w, conv_b.reshape(1, c)]
    in_specs += [full(n_tiles, kb, LANE), full(n_tiles, kb, LANE), full(1, c), full(1, c), full(1, c), full(nb, c)]
    args += [wa, wi, b_a.reshape(1, c), b_i.reshape(1, c), lam.reshape(1, c), h0]
    out_specs = [tile_spec(0), pl.BlockSpec((nb, c), lambda s: (0, 0))]
    out_shape = [jax.ShapeDtypeStruct((seq, nb, c), F32), jax.ShapeDtypeStruct((nb, c), F32)]
    if reverse:
        in_specs += [tile_spec(0), tile_spec(1)]
        args += [fwd[0], u3]
    else:
        out_specs.append(tile_spec(0))
        out_shape.append(jax.ShapeDtypeStruct((seq, nb, c), F32))
    kern = functools.partial(_lru_pass_kernel, reverse=reverse, starts=tuple(starts), kb=kb, bt=bt, nb=nb, taps=taps)
    blk = nb * bt * c * 4 / MIB
    vmem = (2 + 2 + 3 + (4 if reverse else 2) + 4) * blk + 4 * n_tiles * kb * LANE * 2 / MIB + 8
    return pl.pallas_call(
        kern,
        grid=(nt,),
        in_specs=in_specs,
        out_specs=out_specs,
        out_shape=out_shape,
        scratch_shapes=[pltpu.VMEM((nb * bt, c), F32), pltpu.VMEM((nb * bt, c), BF16),
                        pltpu.VMEM((nb * bt, c), F32), pltpu.VMEM((nb * bt, c), F32), pltpu.VMEM((nb, c), F32)],
        compiler_params=_cparams(1, vmem),
        name="lru_bwd" if reverse else "lru_fwd",
    )(*args)


def _mixer_nat(xs, streams, mods, cache_k, cache_v, j, w_qkv, g_mix, g_q, g_k, rpb, w_o):
    heads, dh = rpb.shape[0], g_q.shape[0]
    w_qkv, w_o = w_qkv.astype(BF16), w_o.astype(BF16)
    gains = jnp.concatenate([jnp.tile(g_q * (dh ** -0.5 * LOG2E), heads), jnp.tile(g_k, heads),
                             jnp.ones((heads * dh,), F32)])[None]
    spec = dict(head_w=dh, norm_div=dh, norm_cols=2 * heads * dh, gains=gains)
    new_x, extra = [], None
    for x, st in zip(xs, streams):
        latent = not st.shared
        qkv = _proj(x, w_qkv, st, norm_g=g_mix, mod=(mods[0], mods[1]), heads=spec,
                    out_dtype=BF16 if latent else F32, name="nat_qkv")
        if latent:
            o = _nat_attention(qkv, st, cache_k, cache_v, j, rpb, dh)
        else:
            o, kc, vc = _ctx_attention(qkv, qkv, qkv, st, n_heads=heads, dq=dh, dv=dh, q_col=0, k_col=heads * dh,
                                       v_col=2 * heads * dh, emit_kv=True)
            extra = (kc, vc)
        new_x.append(_proj(o, w_o, st, res=x, gate=mods[2], name="nat_out"))
    return new_x, extra


def _mixer_lru(xs, streams, mods, state, w_in, g_mix, conv_w, conv_b, w_a, b_a, w_i, b_i, lam, w_out):
    c = conv_w.shape[1]
    block = w_a.shape[-1]
    w_in, w_out = w_in.astype(BF16), w_out.astype(BF16)
    starts, kb = _band_plan(c, block)
    wa = [_band_weights(0.5 * w_a[d], c, block, starts, kb) for d in range(2)]
    wi = [_band_weights(0.5 * w_i[d], c, block, starts, kb) for d in range(2)]
    new_x, st_out = [], None
    for x, st in zip(xs, streams):
        latent = not st.shared
        h0 = state.astype(F32) if latent else jnp.zeros((st.nb, 2, c), F32)
        u = _proj(x, w_in, st, norm_g=g_mix, mod=(mods[0], mods[1]), out_time_major=True, name="lru_in")
        hs_f, t_f, conv = _lru_pass(u, st, conv_w, conv_b, wa[0], wi[0], b_a[0], b_i[0], lam[0], h0[:, 0], starts,
                                    kb, reverse=False)
        y, t_b = _lru_pass(u, st, conv_w, conv_b, wa[1], wi[1], b_a[1], b_i[1], lam[1], h0[:, 1], starts, kb,
                           reverse=True, fwd=(hs_f, conv))
        if not latent:
            st_out = jnp.stack([t_f, t_b], axis=1)
        new_x.append(_proj(y.reshape(st.seq, st.nb * c), w_out, st, res=x, gate=mods[2], x_time_major=True,
                           name="lru_out"))
    return new_x, st_out


def _mixer_mla(xs, streams, mods, cache_ckv, cache_kr, w_down, g_mix, g_qa, g_kva, w_uq, w_ukv, g_q, g_k, w_o):
    d_model = w_down.shape[0]
    q_rank, kv_rank = g_qa.shape[0], g_kva.shape[0]
    qk_dim = g_q.shape[0]
    heads = w_uq.shape[1] // qk_dim
    rope = w_down.shape[1] - q_rank - kv_rank
    nope = qk_dim - rope
    assert nope == LANE and rope <= LANE and kv_rank % LANE == 0 and q_rank % LANE == 0
    head_w = 2 * LANE
    q_pad = -q_rank % kv_rank
    kv_col = q_rank + q_pad
    tail_pad = -(kv_col + kv_rank + rope) % 512
    w_dn = jnp.concatenate([w_down[:, :q_rank], jnp.zeros((d_model, q_pad), F32),
                            w_down[:, q_rank:q_rank + kv_rank], w_down[:, q_rank + kv_rank:],
                            jnp.zeros((d_model, tail_pad), F32)], axis=1).astype(BF16)
    kr_blk = (kv_col + kv_rank) // LANE
    w_q = jnp.pad(w_uq.reshape(q_rank, heads, qk_dim), ((0, 0), (0, 0), (0, head_w - qk_dim)))
    w_q = w_q.reshape(q_rank, heads * head_w).astype(BF16)
    w_ukv, w_o = w_ukv.astype(BF16), w_o.astype(BF16)
    gq = jnp.tile(jnp.pad(g_q * (qk_dim ** -0.5 * LOG2E), (0, head_w - qk_dim)), heads)[None]
    g1, g2 = g_k[None, :nope], jnp.pad(g_k[nope:], (0, LANE - rope))[None]
    p = cache_ckv.shape[1]
    new_x, extra = [], None
    for x, st in zip(xs, streams):
        latent = not st.shared
        d = _proj(x, w_dn, st, norm_g=g_mix, mod=(mods[0], mods[1]), name="mla_down")
        q_tabs = _rope_tables(st.seq, rope, nope, head_w) if latent else None
        k_tabs = _rope_tables(st.seq, rope, 0, LANE) if latent else None
        q = _proj(d, w_q, st, x_block=(q_rank, 0), norm_g=g_qa[None],
                  heads=dict(head_w=head_w, norm_div=qk_dim, norm_cols=heads * head_w, gains=gq, tabs=q_tabs,
                             rope_tiles=(False, True)),
                  out_dtype=BF16, name="mla_uq")
        kv = _mla_kv(d, (kv_rank, kv_col // kv_rank), w_ukv, d, (LANE, kr_blk), g_kva[None], g1, g2, k_tabs, st,
                     n_heads=heads, norm_div=qk_dim, emit_xn=not latent, name="mla_ukv")
        if latent:
            k, v = kv
            cst = _Stream(st.nb, p, 0, True)
            krc = jnp.pad(cache_kr.reshape(st.nb * p, rope), ((0, 0), (0, LANE - rope)))
            kc, vc = _mla_kv(cache_ckv.reshape(st.nb * p, kv_rank), (kv_rank, 0), w_ukv, krc, (LANE, 0), None,
                             g1, g2, None, cst, n_heads=heads, norm_div=qk_dim, emit_xn=False,
                             name="mla_ukv_cache")
            o = _joint_dense_attention(q, k, v, kc, vc, st, p, n_heads=heads, dq=head_w, dv=LANE)
        else:
            k, v, ckv = kv
            o = _ctx_attention(q, k, v, st, n_heads=heads, dq=head_w, dv=LANE, q_col=0, k_col=0, v_col=0)
            kr_out = d[:, kv_col + kv_rank:kv_col + kv_rank + rope]
            extra = (ckv.reshape(st.nb, st.seq, kv_rank), kr_out.reshape(st.nb, st.seq, rope))
        new_x.append(_proj(o, w_o, st, res=x, gate=mods[2], name="mla_out"))
    return new_x, extra


def _mixer_swa(xs, streams, mods, cache_k, cache_v, j, w_qkv, g_mix, g_q, g_k, sinks, w_o):
    dh = g_q.shape[0]
    heads = sinks.shape[0]
    kvh = (w_qkv.shape[1] // dh - heads) // 2
    w_qkv, w_o = w_qkv.astype(BF16), w_o.astype(BF16)
    gains = jnp.concatenate([jnp.tile(g_q * (dh ** -0.5 * LOG2E), heads), jnp.tile(g_k, kvh),
                             jnp.ones((kvh * dh,), F32)])[None]
    sinks = sinks.astype(F32) * LOG2E
    new_x, extra = [], None
    for x, st in zip(xs, streams):
        latent = not st.shared
        tabs = _rope_tables(st.seq, dh, 0, dh) if latent else None
        if tabs is not None:
            tabs = tuple(jnp.tile(t, (1, LANE // dh)) for t in tabs)
        spec = dict(head_w=dh, norm_div=dh, norm_cols=(heads + kvh) * dh, gains=gains, tabs=tabs,
                    rope_tiles=(True,))
        qkv = _proj(x, w_qkv, st, norm_g=g_mix, mod=(mods[0], mods[1]), heads=spec,
                    out_dtype=BF16 if latent else F32, name="swa_qkv")
        if latent:
            o = _swa_attention(qkv, st, sinks, heads=heads, kvh=kvh, dh=dh, cache=(cache_k, cache_v, j))
        else:
            o, kc, vc = _swa_attention(qkv, st, sinks, heads=heads, kvh=kvh, dh=dh)
            extra = (kc, vc)
        new_x.append(_proj(o, w_o, st, res=x, gate=mods[2], name="swa_out"))
    return new_x, extra


def kernel(x_prompt, x_sample, cache_nat_k, cache_nat_v, state_lru, cache_mla_ckv, cache_mla_krope, cache_swa_k, cache_swa_v, c, c_ctx, norm_mix, norm_ffn, w_mod, b_mod, ffn_w_in, ffn_conv_w, ffn_conv_b, ffn_w_out, nat_w_qkv, nat_q_norm, nat_k_norm, nat_rpb, nat_w_o, lru_w_in, lru_conv_w, lru_conv_b, lru_w_a, lru_b_a, lru_w_i, lru_b_i, lru_lambda, lru_w_out, mla_w_down, mla_q_a_norm, mla_kv_a_norm, mla_w_uq, mla_w_ukv, mla_q_norm, mla_k_norm, mla_w_o, swa_w_qkv, swa_q_norm, swa_k_norm, swa_sinks, swa_w_o):
    bc, sc, d = x_prompt.shape
    bl, n, _ = x_sample.shape
    depth = w_mod.shape[0]
    streams = (_Stream(bc, sc, 0, True), _Stream(bl, n, 1, False))
    xs = [x_prompt.reshape(bc * sc, d), x_sample.reshape(bl * n, d)]

    n_cond = 1 + bl
    cond_rows = -(-n_cond // SUBLANE) * SUBLANE
    cond = jnp.zeros((cond_rows, d), F32).at[0].set(c_ctx).at[1:n_cond].set(c)
    mods = _modulation(cond, w_mod, b_mod)[:, :n_cond]

    nat_k_l, nat_v_l, lru_l, ckv_l, krope_l, swa_k_l, swa_v_l = [], [], [], [], [], [], []
    for l in range(depth):
        kind, j = l % 4, l // 4
        m6 = [mods[l, :, None, t * d:(t + 1) * d] for t in range(6)]
        g_mix = norm_mix[l].reshape(1, d)
        if kind == 0:
            xs, (kc, vc) = _mixer_nat(xs, streams, m6, cache_nat_k, cache_nat_v, j, nat_w_qkv[j], g_mix,
                                      nat_q_norm[j], nat_k_norm[j], nat_rpb[j], nat_w_o[j])
            nat_k_l.append(kc)
            nat_v_l.append(vc)
        elif kind == 1:
            xs, st = _mixer_lru(xs, streams, m6, state_lru[:, j], lru_w_in[j], g_mix, lru_conv_w[j], lru_conv_b[j],
                                lru_w_a[j], lru_b_a[j], lru_w_i[j], lru_b_i[j], lru_lambda[j], lru_w_out[j])
            lru_l.append(st)
        elif kind == 2:
            xs, (ckv, kr) = _mixer_mla(xs, streams, m6, cache_mla_ckv[:, j], cache_mla_krope[:, j], mla_w_down[j],
                                       g_mix, mla_q_a_norm[j], mla_kv_a_norm[j], mla_w_uq[j], mla_w_ukv[j],
                                       mla_q_norm[j], mla_k_norm[j], mla_w_o[j])
            ckv_l.append(ckv)
            krope_l.append(kr)
        else:
            xs, (kc, vc) = _mixer_swa(xs, streams, m6, cache_swa_k, cache_swa_v, j, swa_w_qkv[j], g_mix,
                                      swa_q_norm[j], swa_k_norm[j], swa_sinks[j], swa_w_o[j])
            swa_k_l.append(kc)
            swa_v_l.append(vc)
        w_in, w_out = ffn_w_in[l].astype(BF16), ffn_w_out[l].astype(BF16)
        xs = [_conv_ffn(x, st, norm_ffn[l].reshape(1, d), m6[3], m6[4], m6[5], w_in, ffn_conv_w[l],
                        ffn_conv_b[l], w_out) for x, st in zip(xs, streams)]

    return (xs[0].reshape(bc, sc, d), xs[1].reshape(bl, n, d), jnp.stack(nat_k_l, axis=1),
            jnp.stack(nat_v_l, axis=1), jnp.stack(lru_l, axis=1), jnp.stack(ckv_l, axis=1),
            jnp.stack(krope_l, axis=1), jnp.stack(swa_k_l, axis=1), jnp.stack(swa_v_l, axis=1))
```

```python
import functools

import numpy as np
import jax
import jax.numpy as jnp
from jax import lax
from jax.experimental import pallas as pl
from jax.experimental.pallas import tpu as pltpu

F32 = jnp.float32
BF16 = jnp.bfloat16

GRID_W = 64
NA_WIN_ROWS = 8
NA_WIN_COLS = 16
NA_Q_ROWS = 8
NA_K_ROWS = 16
LRU_C = 8.0
SWA_WINDOW = 128
SWA_BLOCK = 128
ROPE_BASE = 10000.0
ROPE_GROUP = 32
EPS = 1e-6
NEG = -1e30
LOG2E = float(np.log2(np.e))
LANE = 128
SUBLANE = 8
HALO = 16
MIB = 1024 * 1024
VMEM_LIMIT_CAP_MIB = 60
ROW_TILES = (1024, 512, 256, 128, 64, 32, 16)
MAX_COL_TILE = 1024
MAX_COL_TILE_RESIDUAL = 512
FFN_ROW_TILE, FFN_FF_CHUNK = 1024, 512
HEADS_PER_STEP = (4, 2, 1)
MLA_KEY_CHUNK = 1024
LRU_ROWS_PER_STEP = 256


def _cparams(n_axes, vmem_mib):
    return pltpu.CompilerParams(dimension_semantics=("arbitrary",) * n_axes,
                                vmem_limit_bytes=int(min(vmem_mib, VMEM_LIMIT_CAP_MIB) * MIB))


def _largest_divisor(n, candidates):
    for c in candidates:
        if n % c == 0:
            return c
    return n


class _Stream:
    def __init__(self, nb, seq, mod0, shared_mod):
        self.nb, self.seq, self.rows, self.mod0, self.shared = nb, seq, nb * seq, mod0, shared_mod
        self.bm = _largest_divisor(self.rows if shared_mod else seq, ROW_TILES)

    def mod_index(self, row0):
        return self.mod0 if self.shared else self.mod0 + row0 // self.seq


def _norm_mod(x, g, shift, scale):
    ms = jnp.mean(x * x, axis=-1, keepdims=True)
    return (x * lax.rsqrt(ms + EPS)) * (g * (1.0 + scale)) + shift


def _rms(x, g):
    return (x * lax.rsqrt(jnp.mean(x * x, axis=-1, keepdims=True) + EPS)) * g


def _modulation_kernel(c_ref, w_ref, b_ref, o_ref):
    c = c_ref[...]
    sc = (c * jax.nn.sigmoid(c)).astype(BF16)
    o_ref[...] = jnp.dot(sc, w_ref[...].astype(BF16), preferred_element_type=F32) + b_ref[...]


def _modulation(cond, w_mod, b_mod):
    depth, d, n = w_mod.shape
    rows = cond.shape[0]
    bn = _largest_divisor(n, (512, 256, 128))
    return pl.pallas_call(
        _modulation_kernel,
        grid=(depth, n // bn),
        in_specs=[pl.BlockSpec((rows, d), lambda l, j: (0, 0)),
                  pl.BlockSpec((None, d, bn), lambda l, j: (l, 0, j)),
                  pl.BlockSpec((None, 1, bn), lambda l, j: (l, 0, j))],
        out_specs=pl.BlockSpec((None, rows, bn), lambda l, j: (l, 0, j)),
        out_shape=jax.ShapeDtypeStruct((depth, rows, n), F32),
        compiler_params=_cparams(2, 32),
        name="modulation",
    )(cond, w_mod, b_mod.reshape(depth, 1, n))


def _rope_tables(n_tokens, rot_dim, lead, width):
    t = jnp.arange(n_tokens)
    row = (t // GRID_W).astype(F32)
    col = (t % GRID_W).astype(F32)
    half = rot_dim // 2
    inv = ROPE_BASE ** (-jnp.arange(0, half, 2, dtype=F32) / half)
    ar = row[:, None] * inv
    ac = col[:, None] * inv
    ang = jnp.concatenate([ar, ar, ac, ac], axis=-1)
    cos, sin = jnp.cos(ang), jnp.sin(ang)
    first = (np.arange(rot_dim) % ROPE_GROUP) < ROPE_GROUP // 2
    sin_a = jnp.where(first, -sin, 0.0)
    sin_b = jnp.where(first, 0.0, sin)
    pad = ((0, 0), (lead, width - lead - rot_dim))
    return (jnp.pad(cos, pad, constant_values=1.0), jnp.pad(sin_a, pad), jnp.pad(sin_b, pad))


def _rope_apply(y, cos, sin_a, sin_b):
    shift = ROPE_GROUP // 2
    return y * cos + pltpu.roll(y, LANE - shift, 1) * sin_a + pltpu.roll(y, shift, 1) * sin_b


def _rotate_half_matrix():
    shift = ROPE_GROUP // 2
    src = lax.broadcasted_iota(jnp.int32, (LANE, LANE), 0)
    dst = lax.broadcasted_iota(jnp.int32, (LANE, LANE), 1)
    first = (dst % ROPE_GROUP) < shift
    return jnp.where(first & (src == dst + shift), -1.0,
                     jnp.where(jnp.logical_not(first) & (src == dst - shift), 1.0, 0.0)).astype(BF16)


def _rope_apply_mxu(y, cos, sin_a, sin_b, perm):
    hi = y.astype(BF16)
    lo = (y - hi.astype(F32)).astype(BF16)
    rot = jnp.dot(hi, perm, preferred_element_type=F32) + jnp.dot(lo, perm, preferred_element_type=F32)
    return y * cos + rot * (sin_b - sin_a)


def _fill_lhs(x_ref, xs_ref, xn_ref, prologue, g_ref, sh_ref, sc_ref, row_chunk):
    bm = x_ref.shape[0]

    def chunk(r, carry):
        rows = pl.ds(pl.multiple_of(r * row_chunk, row_chunk), row_chunk)
        x = x_ref[rows, :].astype(F32)
        if prologue == "norm_mod":
            x = _norm_mod(x, g_ref[...], sh_ref[...], sc_ref[...])
        elif prologue == "norm":
            x = _rms(x, g_ref[...])
        if xn_ref is not None:
            xn_ref[rows, :] = x
        xs_ref[rows, :] = x.astype(BF16)
        return carry
    n_chunks = bm // row_chunk
    lax.fori_loop(0, n_chunks, chunk, 0, unroll=2 if n_chunks % 2 == 0 else 1)


def _head_norm_store(acc, o_ref, hg_ref, tabs, head_w, norm_div, col0, norm_cols, rope_tiles):
    bn = acc.shape[1]
    period = tabs[0].shape[1] if tabs is not None else LANE
    perm = _rotate_half_matrix() if tabs is not None and head_w >= LANE else None
    if head_w < LANE:
        head_of_row = lax.broadcasted_iota(jnp.int32, (LANE, LANE), 0) // head_w
        head_of_col = lax.broadcasted_iota(jnp.int32, (LANE, LANE), 1) // head_w
        same_head = jnp.where(head_of_row == head_of_col, 1.0, 0.0).astype(BF16)
    for s0 in range(0, bn, max(head_w, LANE)):
        normed = None if norm_cols is None else (col0 + s0 < norm_cols)
        tiles = [acc[:, s0 + k * LANE:s0 + (k + 1) * LANE] for k in range(max(head_w, LANE) // LANE)]
        if head_w >= LANE:
            sq = None
            for y in tiles:
                sq = y * y if sq is None else sq + y * y
            inv = lax.rsqrt(jnp.sum(sq, axis=-1, keepdims=True) / norm_div + EPS)
        else:
            y2 = tiles[0] * tiles[0]
            hi = y2.astype(BF16)
            lo = (y2 - hi.astype(F32)).astype(BF16)
            ssq = (jnp.dot(hi, same_head, preferred_element_type=F32)
                   + jnp.dot(lo, same_head, preferred_element_type=F32))
            inv = lax.rsqrt(ssq / norm_div + EPS)
        if normed is not None:
            inv = jnp.where(normed, inv, 1.0)
        for k, y in enumerate(tiles):
            c0 = s0 + k * LANE
            y = (y * inv) * hg_ref[:, c0:c0 + LANE]
            t0 = c0 % period
            if tabs is not None and rope_tiles[t0 // LANE]:
                tab = tuple(t[:, t0:t0 + LANE] for t in tabs)
                rotated = _rope_apply(y, *tab) if head_w < LANE else _rope_apply_mxu(y, *tab, perm)
                y = rotated if normed is None else jnp.where(normed, rotated, y)
            o_ref[:, c0:c0 + LANE] = y.astype(o_ref.dtype)


def _proj_kernel(*refs, prologue, emit_xn, epilogue, head_w, norm_div, norm_cols, rope, rope_tiles, row_chunk):
    it = iter(refs)
    x_ref = next(it)
    g_ref = next(it) if prologue is not None else None
    sh_ref, sc_ref = (next(it), next(it)) if prologue == "norm_mod" else (None, None)
    w_ref = next(it)
    if epilogue == "res":
        res_ref, gate_ref = next(it), next(it)
    if epilogue == "heads":
        hg_ref = next(it)
        tabs = (next(it), next(it), next(it)) if rope else None
    o_ref = next(it)
    xn_ref = next(it) if emit_xn else None
    xs_ref = next(it, None)
    j = pl.program_id(1)
    bn = o_ref.shape[1]

    if xs_ref is None:
        xs_ref = x_ref
    else:
        @pl.when(j == 0)
        def _():
            _fill_lhs(x_ref, xs_ref, xn_ref, prologue, g_ref, sh_ref, sc_ref, row_chunk)

    acc = jnp.dot(xs_ref[...], w_ref[...], preferred_element_type=F32)
    if epilogue == "res":
        o_ref[...] = res_ref[...] + gate_ref[...] * acc
    elif epilogue == "heads":
        _head_norm_store(acc, o_ref, hg_ref, tabs, head_w, norm_div, j * bn, norm_cols, rope_tiles)
    else:
        o_ref[...] = acc.astype(o_ref.dtype)


def _proj(x, w, st, *, x_block=None, norm_g=None, mod=None, res=None, gate=None, heads=None, emit_xn=False,
          out_dtype=F32, bn=None, x_time_major=False, out_time_major=False, name="proj"):
    k, n = w.shape
    time_major = x_time_major or out_time_major
    bm = min(st.bm, st.seq) if time_major else st.bm
    tiles_per_seq = st.seq // bm if st.seq % bm == 0 else None
    if time_major:
        assert tiles_per_seq is not None and x_block is None
    if x_time_major:
        assert x.shape == (st.seq, st.nb * k)
        kidx = 0
    else:
        kx, kidx = x_block if x_block is not None else (x.shape[1], 0)
        assert kx == k and x.shape[0] == st.rows
    rows = st.rows
    prologue = None if norm_g is None else ("norm_mod" if mod is not None else "norm")
    epilogue = "res" if res is not None else ("heads" if heads is not None else None)
    rope = heads is not None and heads.get("tabs") is not None
    if bn is None:
        unit = LANE
        if epilogue == "heads":
            unit = max(heads["head_w"], LANE, heads["tabs"][0].shape[1] if rope else LANE)
        cap = MAX_COL_TILE_RESIDUAL if epilogue == "res" and x.dtype != BF16 else MAX_COL_TILE
        bn = next((c for c in range(cap, unit - 1, -unit) if n % c == 0), n)
    mod_idx = lambda i: st.mod_index(i * bm)

    if x_time_major:
        in_specs = [pl.BlockSpec((bm, k), lambda i, j: (i % tiles_per_seq, i // tiles_per_seq))]
    else:
        in_specs = [pl.BlockSpec((bm, k), lambda i, j: (i, kidx))]
    args = [x]
    if prologue is not None:
        in_specs.append(pl.BlockSpec((1, k), lambda i, j: (0, 0)))
        args.append(norm_g)
    if prologue == "norm_mod":
        in_specs += [pl.BlockSpec((None, 1, k), lambda i, j: (mod_idx(i), 0, 0))] * 2
        args += list(mod)
    in_specs.append(pl.BlockSpec((k, bn), lambda i, j: (0, j)))
    args.append(w.astype(BF16))
    if epilogue == "res":
        in_specs += [pl.BlockSpec((bm, bn), lambda i, j: (i, j)),
                     pl.BlockSpec((None, 1, bn), lambda i, j: (mod_idx(i), 0, j))]
        args += [res, gate]
    head_w = norm_div = 0
    norm_cols = rope_tiles = None
    if epilogue == "heads":
        head_w, norm_div = heads["head_w"], heads["norm_div"]
        norm_cols = heads["norm_cols"] if heads["norm_cols"] < n else None
        assert bn % max(head_w, LANE) == 0 and heads["norm_cols"] % max(head_w, LANE) == 0
        in_specs.append(pl.BlockSpec((1, bn), lambda i, j: (0, j)))
        args.append(heads["gains"])
        if rope:
            period = heads["tabs"][0].shape[1]
            rope_tiles = heads["rope_tiles"]
            assert bn % period == 0 and tiles_per_seq is not None and len(rope_tiles) == period // LANE
            in_specs += [pl.BlockSpec((bm, period), lambda i, j: (i % tiles_per_seq, 0))] * 3
            args += list(heads["tabs"])
    if out_time_major:
        n_col_tiles = n // bn
        out_shape = [jax.ShapeDtypeStruct((st.seq, st.nb * n), out_dtype)]
        out_specs = [pl.BlockSpec((bm, bn), lambda i, j: (i % tiles_per_seq, (i // tiles_per_seq) * n_col_tiles + j))]
    else:
        out_shape = [jax.ShapeDtypeStruct((rows, n), out_dtype)]
        out_specs = [pl.BlockSpec((bm, bn), lambda i, j: (i, j))]
    if emit_xn:
        out_shape.append(jax.ShapeDtypeStruct((rows, k), F32))
        out_specs.append(pl.BlockSpec((bm, k), lambda i, j: (i, 0)))
    xbytes = x.dtype.itemsize
    vmem = (2 * bm * k * xbytes + bm * k * 2 + 2 * k * bn * 2 + (6 if epilogue == "res" else 4) * bm * bn * 4
            + (2 * bm * k * 4 if emit_xn else 0)) / MIB + 8
    kern = functools.partial(_proj_kernel, prologue=prologue, emit_xn=emit_xn, epilogue=epilogue, head_w=head_w,
                             norm_div=norm_div, norm_cols=norm_cols, rope=rope, rope_tiles=rope_tiles,
                             row_chunk=min(bm, 128))
    direct_lhs = prologue is None and x.dtype == BF16 and not emit_xn
    out = pl.pallas_call(
        kern,
        grid=(rows // bm, n // bn),
        in_specs=in_specs,
        out_specs=out_specs,
        out_shape=out_shape,
        scratch_shapes=[] if direct_lhs else [pltpu.VMEM((bm, k), BF16)],
        compiler_params=_cparams(2, vmem),
        name=name,
    )(*args)
    return out if emit_xn else out[0]


def _ffn_kernel(xp_ref, x_ref, xn_ref, g_ref, sh_ref, sc_ref, gate_ref, wa_ref, wb_ref, cw_ref, cb_ref,
                wo_ref, o_ref, h_ref, *, bm, seq, row_chunk):
    i = pl.program_id(0)
    c = pl.program_id(1)
    n_chunks = pl.num_programs(1)

    @pl.when(c == 0)
    def _():
        g, sh, sc = g_ref[...], sh_ref[...], sc_ref[...]
        h_ref[0:HALO, :] = _norm_mod(xp_ref[...], g, sh, sc).astype(BF16)
        h_ref[HALO + bm:, :] = _norm_mod(xn_ref[...], g, sh, sc).astype(BF16)

        def chunk(r, carry):
            src = pl.ds(pl.multiple_of(r * row_chunk, row_chunk), row_chunk)
            dst = pl.ds(pl.multiple_of(HALO + r * row_chunk, HALO), row_chunk)
            h_ref[dst, :] = _norm_mod(x_ref[src, :], g, sh, sc).astype(BF16)
            return carry
        n_row_chunks = bm // row_chunk
        lax.fori_loop(0, n_row_chunks, chunk, 0, unroll=2 if n_row_chunks % 2 == 0 else 1)
        o_ref[...] = jnp.zeros_like(o_ref)

    ua = jnp.dot(h_ref[...], wa_ref[...], preferred_element_type=F32)
    ub = jnp.dot(h_ref[HALO:HALO + bm, :], wb_ref[...], preferred_element_type=F32)
    n_all = bm + 2 * HALO
    u_prev = pltpu.roll(ua, 1, 0)[HALO:HALO + bm]
    u_next = pltpu.roll(ua, n_all - 1, 0)[HALO:HALO + bm]
    u_mid = ua[HALO:HALO + bm]
    pos = jnp.bitwise_and(i * bm + lax.broadcasted_iota(jnp.int32, (bm, 1), 0), seq - 1)
    u_prev = jnp.where(pos == 0, 0.0, u_prev)
    u_next = jnp.where(pos == seq - 1, 0.0, u_next)
    cw = cw_ref[...]
    a = cb_ref[...] + u_prev * cw[0:1] + u_mid * cw[1:2] + u_next * cw[2:3]
    gated = ((a * jax.nn.sigmoid(a)) * ub).astype(BF16)
    o_ref[...] += jnp.dot(gated, wo_ref[...], preferred_element_type=F32)

    @pl.when(c == n_chunks - 1)
    def _():
        o_ref[...] = x_ref[...] + gate_ref[...] * o_ref[...]


def _conv_ffn(x, st, g, shift, scale, gate, w_in, conv_w, conv_b, w_out, bm=FFN_ROW_TILE, ck=FFN_FF_CHUNK):
    m, d = x.shape
    d_ff = w_out.shape[0]
    bm = min(st.bm, bm)
    ck = _largest_divisor(d_ff, tuple(c for c in (512, 256, 128) if c <= ck))
    n_chunks = d_ff // ck
    n_halo_blocks = m // HALO
    assert st.seq & (st.seq - 1) == 0 and conv_w.shape[0] == 3
    mod_idx = lambda i: st.mod_index(i * bm)
    kern = functools.partial(_ffn_kernel, bm=bm, seq=st.seq, row_chunk=min(bm, 128))
    vmem = (4 * bm * d * 4 + (bm + 2 * HALO) * d * 2 + 6 * d * ck * 2 + 5 * (bm + 2 * HALO) * ck * 4) / MIB + 4
    return pl.pallas_call(
        kern,
        grid=(m // bm, n_chunks),
        in_specs=[
            pl.BlockSpec((HALO, d), lambda i, c: (jnp.maximum(i * (bm // HALO) - 1, 0), 0)),
            pl.BlockSpec((bm, d), lambda i, c: (i, 0)),
            pl.BlockSpec((HALO, d), lambda i, c: (jnp.minimum((i + 1) * (bm // HALO), n_halo_blocks - 1), 0)),
            pl.BlockSpec((1, d), lambda i, c: (0, 0)),
            pl.BlockSpec((None, 1, d), lambda i, c: (mod_idx(i), 0, 0)),
            pl.BlockSpec((None, 1, d), lambda i, c: (mod_idx(i), 0, 0)),
            pl.BlockSpec((None, 1, d), lambda i, c: (mod_idx(i), 0, 0)),
            pl.BlockSpec((d, ck), lambda i, c: (0, c)),
            pl.BlockSpec((d, ck), lambda i, c: (0, n_chunks + c)),
            pl.BlockSpec((conv_w.shape[0], ck), lambda i, c: (0, c)),
            pl.BlockSpec((1, ck), lambda i, c: (0, c)),
            pl.BlockSpec((ck, d), lambda i, c: (c, 0)),
        ],
        out_specs=pl.BlockSpec((bm, d), lambda i, c: (i, 0)),
        out_shape=jax.ShapeDtypeStruct((m, d), F32),
        scratch_shapes=[pltpu.VMEM((bm + 2 * HALO, d), BF16)],
        compiler_params=_cparams(2, vmem),
        name="conv_ffn",
    )(x, x, x, g, shift, scale, gate, w_in, w_in, conv_w, conv_b.reshape(1, d_ff), w_out)


def _qk(q, k):
    return lax.dot_general(q, k, (((1,), (1,)), ((), ())), preferred_element_type=F32)


def _attend(scores, values, sink=None):
    m = None
    for s in scores:
        mi = jnp.max(s, axis=-1, keepdims=True)
        m = mi if m is None else jnp.maximum(m, mi)
    if sink is not None:
        m = jnp.maximum(m, sink)
    es = [jnp.exp2(s - m) for s in scores]
    den = None
    for e in es:
        di = jnp.sum(e, axis=-1, keepdims=True)
        den = di if den is None else den + di
    if sink is not None:
        den = den + jnp.exp2(sink - m)
    out = None
    for e, v in zip(es, values):
        oi = jnp.dot(e.astype(BF16), v, preferred_element_type=F32)
        out = oi if out is None else out + oi
    return out * (1.0 / den)


def _ctx_attn_kernel(q_ref, k_ref, v_ref, *outs, heads, dq, dv, emit_kv):
    o_ref = outs[0]
    for h in range(heads):
        q = q_ref[:, h * dq:(h + 1) * dq].astype(BF16)
        k = k_ref[:, h * dq:(h + 1) * dq]
        v = v_ref[:, h * dv:(h + 1) * dv]
        if emit_kv:
            outs[1][h] = k.astype(F32)
            outs[2][h] = v.astype(F32)
        o = _attend([_qk(q, k.astype(BF16))], [v.astype(BF16)])
        o_ref[:, h * dv:(h + 1) * dv] = o.astype(o_ref.dtype)


def _ctx_attention(qm, km, vm, st, *, n_heads, dq, dv, q_col, k_col, v_col, emit_kv=False):
    hb = _largest_divisor(n_heads, HEADS_PER_STEP)
    s = st.seq
    assert q_col % (hb * dq) == 0 and k_col % (hb * dq) == 0 and v_col % (hb * dv) == 0
    qo, ko, vo = q_col // (hb * dq), k_col // (hb * dq), v_col // (hb * dv)
    out_shape = [jax.ShapeDtypeStruct((st.rows, n_heads * dv), BF16)]
    out_specs = [pl.BlockSpec((s, hb * dv), lambda b, g: (b, g))]
    if emit_kv:
        out_shape += [jax.ShapeDtypeStruct((st.nb, n_heads, s, dq), F32),
                      jax.ShapeDtypeStruct((st.nb, n_heads, s, dv), F32)]
        out_specs += [pl.BlockSpec((None, hb, s, dq), lambda b, g: (b, g, 0, 0)),
                      pl.BlockSpec((None, hb, s, dv), lambda b, g: (b, g, 0, 0))]
    out = pl.pallas_call(
        functools.partial(_ctx_attn_kernel, heads=hb, dq=dq, dv=dv, emit_kv=emit_kv),
        grid=(st.nb, n_heads // hb),
        in_specs=[pl.BlockSpec((s, hb * dq), lambda b, g: (b, qo + g)),
                  pl.BlockSpec((s, hb * dq), lambda b, g: (b, ko + g)),
                  pl.BlockSpec((s, hb * dv), lambda b, g: (b, vo + g))],
        out_specs=out_specs,
        out_shape=out_shape,
        compiler_params=_cparams(2, 32),
        name="ctx_attention",
    )(qm, km, vm)
    return out if emit_kv else out[0]


def _nat_kernel(q_ref, k_ref, v_ref, kc_ref, vc_ref, bias_ref, o_ref, *, key_rows, rows, heads, dh):
    i = pl.program_id(2)
    n_keys = key_rows * GRID_W
    first_row = jnp.clip(i * NA_Q_ROWS - NA_WIN_ROWS // 2, 0, rows - key_rows)
    start = pl.multiple_of(first_row * GRID_W, GRID_W * 4)
    for h in range(heads):
        lanes = slice(h * dh, (h + 1) * dh)
        q = q_ref[:, lanes]
        k = k_ref[pl.ds(start, n_keys), lanes]
        v = v_ref[pl.ds(start, n_keys), lanes]
        s_loc = _qk(q, k) + bias_ref[h]
        s_ctx = _qk(q, kc_ref[h].astype(BF16))
        o_ref[:, lanes] = _attend([s_loc, s_ctx], [v, vc_ref[h].astype(BF16)]).astype(o_ref.dtype)


def _nat_bias(rpb, rows):
    n_blocks = rows // NA_Q_ROWS
    key_rows = min(NA_K_ROWS, rows)
    wr = min(NA_WIN_ROWS, rows)
    reps = [0, min(1, n_blocks - 1), n_blocks - 1]
    heads = rpb.shape[0]
    nq, nk = NA_Q_ROWS * GRID_W, key_rows * GRID_W
    shape = (NA_Q_ROWS, GRID_W, key_rows, GRID_W)
    qc = np.arange(GRID_W)
    cstart = np.clip(qc - NA_WIN_COLS // 2, 0, GRID_W - NA_WIN_COLS)
    col_ok = (qc[None, :] >= cstart[:, None]) & (qc[None, :] < cstart[:, None] + NA_WIN_COLS)
    rp = jnp.pad(rpb.astype(F32) * LOG2E,
                 ((0, 0), (key_rows, key_rows), (GRID_W - NA_WIN_COLS, GRID_W - NA_WIN_COLS)))
    row_slabs, mask_l = [], []
    for i in reps:
        ks = int(np.clip(i * NA_Q_ROWS - NA_WIN_ROWS // 2, 0, rows - key_rows))
        r = i * NA_Q_ROWS + np.arange(NA_Q_ROWS)
        rs = np.clip(r - wr // 2, 0, rows - wr)
        kr = ks + np.arange(key_rows)
        row_ok = (kr[None, :] >= rs[:, None]) & (kr[None, :] < rs[:, None] + wr)
        for rq in range(NA_Q_ROWS):
            first = ks - int(r[rq]) + NA_WIN_ROWS - 1 + key_rows
            assert 0 <= first and first + key_rows <= rp.shape[1]
            row_slabs.append(rp[:, first:first + key_rows, :])
        mask_l.append(np.broadcast_to(row_ok[:, None, :, None] & col_ok[None, :, None, :], shape).reshape(nq, nk))
    slab = jnp.stack(row_slabs, axis=1).reshape(heads, len(reps), NA_Q_ROWS, key_rows, 2 * GRID_W - 1)
    toep = jnp.stack([slab[..., GRID_W - 1 - c:2 * GRID_W - 1 - c] for c in range(GRID_W)], axis=3)
    bias = toep.reshape(heads, len(reps), nq, nk)
    return jnp.where(jnp.asarray(np.stack(mask_l))[None], bias, NEG)


def _nat_attention(qkv, st, cache_k, cache_v, j, rpb, dh):
    heads = rpb.shape[0]
    n = st.seq
    p = cache_k.shape[3]
    rows = n // GRID_W
    assert rows % NA_Q_ROWS == 0 and rows >= NA_K_ROWS and dh % LANE == 0
    n_blocks = rows // NA_Q_ROWS
    key_rows = min(NA_K_ROWS, rows)
    nq, nk = NA_Q_ROWS * GRID_W, key_rows * GRID_W
    bias = _nat_bias(rpb, rows)
    btype = lambda i: jnp.where(i == 0, 0, jnp.where(i == n_blocks - 1, 2, 1))
    hb = _largest_divisor(heads, HEADS_PER_STEP)
    hg = heads // hb
    kern = functools.partial(_nat_kernel, key_rows=key_rows, rows=rows, heads=hb, dh=dh)
    return pl.pallas_call(
        kern,
        grid=(st.nb, hg, n_blocks),
        in_specs=[pl.BlockSpec((nq, hb * dh), lambda b, h, i: (b * n_blocks + i, h)),
                  pl.BlockSpec((n, hb * dh), lambda b, h, i: (b, hg + h)),
                  pl.BlockSpec((n, hb * dh), lambda b, h, i: (b, 2 * hg + h)),
                  pl.BlockSpec((None, None, hb, p, dh), lambda b, h, i: (b, j, h, 0, 0)),
                  pl.BlockSpec((None, None, hb, p, dh), lambda b, h, i: (b, j, h, 0, 0)),
                  pl.BlockSpec((hb, None, nq, nk), lambda b, h, i: (h, btype(i), 0, 0))],
        out_specs=pl.BlockSpec((nq, hb * dh), lambda b, h, i: (b * n_blocks + i, h)),
        out_shape=jax.ShapeDtypeStruct((st.rows, heads * dh), BF16),
        compiler_params=_cparams(3, 56),
        name="nat_attention",
    )(qkv, qkv, qkv, cache_k, cache_v, bias)


def _joint_dense_kernel(q_ref, k_ref, v_ref, kc_ref, vc_ref, o_ref, *, chunk):
    q = q_ref[...]
    n = k_ref.shape[0]
    pieces = [(k_ref, v_ref, c0, min(chunk, n - c0)) for c0 in range(0, n, chunk)]
    pieces.append((kc_ref, vc_ref, 0, kc_ref.shape[0]))
    m = den = acc = None
    for kr, vr, c0, size in pieces:
        s = _qk(q, kr[c0:c0 + size, :])
        mc = jnp.max(s, axis=-1, keepdims=True)
        m_new = mc if m is None else jnp.maximum(m, mc)
        e = jnp.exp2(s - m_new)
        dc = jnp.sum(e, axis=-1, keepdims=True)
        pv = jnp.dot(e.astype(BF16), vr[c0:c0 + size, :], preferred_element_type=F32)
        if m is None:
            den, acc = dc, pv
        else:
            alpha = jnp.exp2(m - m_new)
            den, acc = alpha * den + dc, alpha * acc + pv
        m = m_new
    o_ref[...] = (acc * (1.0 / den)).astype(o_ref.dtype)


def _joint_dense_attention(qm, km, vm, kcm, vcm, st, p, *, n_heads, dq, dv):
    n = st.seq
    bq = _largest_divisor(n, (1024, 512, 256, 128, 64, 32, 16))
    nqb = n // bq
    return pl.pallas_call(
        functools.partial(_joint_dense_kernel, chunk=MLA_KEY_CHUNK),
        grid=(st.nb, n_heads, nqb),
        in_specs=[pl.BlockSpec((bq, dq), lambda b, h, i: (b * nqb + i, h)),
                  pl.BlockSpec((n, dq), lambda b, h, i: (b, h)),
                  pl.BlockSpec((n, dv), lambda b, h, i: (b, h)),
                  pl.BlockSpec((p, dq), lambda b, h, i: (b, h)),
                  pl.BlockSpec((p, dv), lambda b, h, i: (b, h))],
        out_specs=pl.BlockSpec((bq, dv), lambda b, h, i: (b * nqb + i, h)),
        out_shape=jax.ShapeDtypeStruct((st.rows, n_heads * dv), BF16),
        compiler_params=_cparams(3, 48),
        name="mla_attention",
    )(qm, km, vm, kcm, vcm)


def _both_halves(x, s):
    x = x.astype(F32)
    low = lax.broadcasted_iota(jnp.int32, (1, LANE), 1) < LANE // 2
    keep = low if s == 0 else jnp.logical_not(low)
    return jnp.where(keep, x, pltpu.roll(x, LANE // 2, 1)).astype(BF16)


def _swa_step(sinks_ref, pair, q_ref, k, v, kc, vc, o_ref, *, dh, groups, local_bias):
    kv_per_step = LANE // dh
    assert kv_per_step == 2 and groups % 2 == 0
    rows = q_ref.shape[0]
    low = lax.broadcasted_iota(jnp.int32, (1, LANE), 1) < dh
    row_group = lax.broadcasted_iota(jnp.int32, (groups * rows, 1), 0) // rows
    for s in range(kv_per_step):
        kd, vd = _both_halves(k, s), _both_halves(v, s)
        q_parts = []
        for g in range(groups):
            c0 = ((s * groups + g) * dh // LANE) * LANE
            tile = q_ref[:, c0:c0 + LANE].astype(BF16)
            q_parts.append(jnp.where(low if g % 2 == 0 else jnp.logical_not(low), tile, jnp.zeros_like(tile)))
        q = jnp.concatenate(q_parts, axis=0)
        sink = jnp.zeros((groups * rows, 1), F32)
        for g in range(groups):
            sink = jnp.where(row_group == g, sinks_ref[(pair * kv_per_step + s) * groups + g], sink)
        s_loc = _qk(q, kd)
        if local_bias is not None:
            s_loc = s_loc + jnp.concatenate([local_bias] * groups, axis=0)
        if kc is not None:
            kcd = jnp.concatenate([kc[s], kc[s]], axis=-1).astype(BF16)
            vcd = jnp.concatenate([vc[s], vc[s]], axis=-1).astype(BF16)
            out = _attend([s_loc, _qk(q, kcd)], [vd, vcd], sink)
        else:
            out = _attend([s_loc], [vd], sink)
        for g in range(0, groups, 2):
            c0 = (s * groups + g) * dh
            o_ref[:, c0:c0 + LANE] = jnp.where(low, out[g * rows:(g + 1) * rows],
                                               out[(g + 1) * rows:(g + 2) * rows]).astype(o_ref.dtype)


def _swa_ctx_kernel(sinks_ref, q_ref, k_ref, v_ref, o_ref, ko_ref, vo_ref, *, dh, groups):
    pair = pl.program_id(1)
    k, v = k_ref[...], v_ref[...]
    for s in range(LANE // dh):
        ko_ref[s] = k[:, s * dh:(s + 1) * dh].astype(F32)
        vo_ref[s] = v[:, s * dh:(s + 1) * dh].astype(F32)
    _swa_step(sinks_ref, pair, q_ref, k, v, None, None, o_ref, dh=dh, groups=groups, local_bias=None)


def _swa_lat_kernel(sinks_ref, q_ref, k_ref, v_ref, kc_ref, vc_ref, o_ref, *, dh, groups, n):
    pair = pl.program_id(1)
    blk = pl.program_id(2)
    n_keys = min(3 * SWA_BLOCK, n)
    start = pl.multiple_of(jnp.clip((blk - 1) * SWA_BLOCK, 0, n - n_keys), SWA_BLOCK)
    k = k_ref[pl.ds(start, n_keys), :]
    v = v_ref[pl.ds(start, n_keys), :]
    qpos = blk * SWA_BLOCK + lax.broadcasted_iota(jnp.int32, (SWA_BLOCK, 1), 0)
    kpos = start + lax.broadcasted_iota(jnp.int32, (1, n_keys), 1)
    bias = jnp.where(jnp.abs(qpos - kpos) <= SWA_WINDOW, 0.0, NEG)
    _swa_step(sinks_ref, pair, q_ref, k, v, kc_ref, vc_ref, o_ref, dh=dh, groups=groups, local_bias=bias)


def _swa_attention(qkv, st, sinks, *, heads, kvh, dh, cache=None):
    groups = heads // kvh
    kv_per_step = LANE // dh
    assert LANE % dh == 0 and kvh % kv_per_step == 0 and groups % kv_per_step == 0
    pairs = kvh // kv_per_step
    qw = kv_per_step * groups * dh
    k_blk = heads * dh // LANE
    v_blk = (heads + kvh) * dh // LANE
    n = st.seq
    common = dict(dh=dh, groups=groups)
    smem = pl.BlockSpec(memory_space=pltpu.SMEM)
    if cache is None:
        out = pl.pallas_call(
            functools.partial(_swa_ctx_kernel, **common),
            grid=(st.nb, pairs),
            in_specs=[smem,
                      pl.BlockSpec((n, qw), lambda b, c: (b, c)),
                      pl.BlockSpec((n, LANE), lambda b, c: (b, k_blk + c)),
                      pl.BlockSpec((n, LANE), lambda b, c: (b, v_blk + c))],
            out_specs=[pl.BlockSpec((n, qw), lambda b, c: (b, c)),
                       pl.BlockSpec((None, kv_per_step, n, dh), lambda b, c: (b, c, 0, 0)),
                       pl.BlockSpec((None, kv_per_step, n, dh), lambda b, c: (b, c, 0, 0))],
            out_shape=[jax.ShapeDtypeStruct((st.rows, heads * dh), BF16),
                       jax.ShapeDtypeStruct((st.nb, kvh, n, dh), F32),
                       jax.ShapeDtypeStruct((st.nb, kvh, n, dh), F32)],
            compiler_params=_cparams(2, 32),
            name="swa_ctx_attention",
        )(sinks, qkv, qkv, qkv)
        return out
    cache_k, cache_v, j = cache
    p = cache_k.shape[3]
    nblk = n // SWA_BLOCK
    assert n % SWA_BLOCK == 0
    return pl.pallas_call(
        functools.partial(_swa_lat_kernel, n=n, **common),
        grid=(st.nb, pairs, nblk),
        in_specs=[smem,
                  pl.BlockSpec((SWA_BLOCK, qw), lambda b, c, i: (b * nblk + i, c)),
                  pl.BlockSpec((n, LANE), lambda b, c, i: (b, k_blk + c)),
                  pl.BlockSpec((n, LANE), lambda b, c, i: (b, v_blk + c)),
                  pl.BlockSpec((None, None, kv_per_step, p, dh), lambda b, c, i: (b, j, c, 0, 0)),
                  pl.BlockSpec((None, None, kv_per_step, p, dh), lambda b, c, i: (b, j, c, 0, 0))],
        out_specs=pl.BlockSpec((SWA_BLOCK, qw), lambda b, c, i: (b * nblk + i, c)),
        out_shape=jax.ShapeDtypeStruct((st.rows, heads * dh), BF16),
        compiler_params=_cparams(3, 32),
        name="swa_attention",
    )(sinks, qkv, qkv, qkv, cache_k, cache_v)


def _mla_kv_kernel(*refs, norm, emit_xn, rope, row_chunk, norm_div):
    it = iter(refs)
    x_ref = next(it)
    g_ref = next(it) if norm else None
    w_ref, kr_ref, g1_ref, g2_ref = next(it), next(it), next(it), next(it)
    tabs = (next(it), next(it), next(it)) if rope else None
    k_ref, v_ref = next(it), next(it)
    xn_ref = next(it) if emit_xn else None
    xs_ref = next(it)

    @pl.when(pl.program_id(1) == 0)
    def _():
        _fill_lhs(x_ref, xs_ref, xn_ref, "norm" if norm else None, g_ref, None, None, row_chunk)

    acc = jnp.dot(xs_ref[...], w_ref[...], preferred_element_type=F32)
    kr = kr_ref[...]
    kr_ssq = jnp.sum(kr * kr, axis=-1, keepdims=True)
    shared = kr * g2_ref[...]
    if rope:
        shared = _rope_apply(shared, *(t[...] for t in tabs))
    for h in range(acc.shape[1] // (2 * LANE)):
        nope = acc[:, 2 * h * LANE:(2 * h + 1) * LANE]
        inv = lax.rsqrt((jnp.sum(nope * nope, axis=-1, keepdims=True) + kr_ssq) / norm_div + EPS)
        k_ref[:, 2 * h * LANE:(2 * h + 1) * LANE] = ((nope * inv) * g1_ref[...]).astype(k_ref.dtype)
        k_ref[:, (2 * h + 1) * LANE:(2 * h + 2) * LANE] = (shared * inv).astype(k_ref.dtype)
        v_ref[:, h * LANE:(h + 1) * LANE] = acc[:, (2 * h + 1) * LANE:(2 * h + 2) * LANE].astype(v_ref.dtype)


def _mla_kv(x, x_block, w_ukv, kr, kr_block, g_kva, g1, g2, tabs, st, *, n_heads, norm_div, emit_xn, name):
    rows = x.shape[0]
    k, n = w_ukv.shape
    head_n = n // n_heads
    assert head_n == 2 * LANE, "nope and value widths must both be one lane tile"
    kx, kidx = x_block
    krw, kridx = kr_block
    assert kx == k and krw == LANE
    bm = st.bm
    norm = g_kva is not None
    rope = tabs is not None
    in_specs = [pl.BlockSpec((bm, k), lambda i, h: (i, kidx))]
    args = [x]
    if norm:
        in_specs.append(pl.BlockSpec((1, k), lambda i, h: (0, 0)))
        args.append(g_kva)
    hb = _largest_divisor(n_heads, HEADS_PER_STEP)
    in_specs += [pl.BlockSpec((k, hb * head_n), lambda i, h: (0, h)),
                 pl.BlockSpec((bm, LANE), lambda i, h: (i, kridx)),
                 pl.BlockSpec((1, LANE), lambda i, h: (0, 0)),
                 pl.BlockSpec((1, LANE), lambda i, h: (0, 0))]
    args += [w_ukv, kr, g1, g2]
    if rope:
        tiles_per_seq = st.seq // bm
        in_specs += [pl.BlockSpec((bm, LANE), lambda i, h: (i % tiles_per_seq, 0))] * 3
        args += list(tabs)
    out_shape = [jax.ShapeDtypeStruct((rows, n_heads * 2 * LANE), BF16),
                 jax.ShapeDtypeStruct((rows, n_heads * LANE), BF16)]
    out_specs = [pl.BlockSpec((bm, hb * 2 * LANE), lambda i, h: (i, h)),
                 pl.BlockSpec((bm, hb * LANE), lambda i, h: (i, h))]
    if emit_xn:
        out_shape.append(jax.ShapeDtypeStruct((rows, k), F32))
        out_specs.append(pl.BlockSpec((bm, k), lambda i, h: (i, 0)))
    kern = functools.partial(_mla_kv_kernel, norm=norm, emit_xn=emit_xn, rope=rope, row_chunk=min(bm, 128),
                             norm_div=norm_div)
    return pl.pallas_call(
        kern,
        grid=(rows // bm, n_heads // hb),
        in_specs=in_specs,
        out_specs=out_specs,
        out_shape=out_shape,
        scratch_shapes=[pltpu.VMEM((bm, k), BF16)],
        compiler_params=_cparams(2, 40),
        name=name,
    )(*args)


def _band_plan(width, block):
    n_tiles = width // LANE
    lo = [((t * LANE) // block) * block for t in range(n_tiles)]
    hi = [(((t + 1) * LANE - 1) // block + 1) * block for t in range(n_tiles)]
    start = [(l // LANE) * LANE for l in lo]
    kb = max(-(-(h - s) // LANE) * LANE for h, s in zip(hi, start))
    kb = min(kb, width)
    start = [min(s, width - kb) for s in start]
    return start, kb


def _band_weights(w, width, block, start, kb):
    n_tiles = width // LANE
    wb = w.astype(BF16)
    tiles = []
    for t in range(n_tiles):
        pieces = []
        col = t * LANE
        while col < (t + 1) * LANE:
            blk = col // block
            col_end = min((blk + 1) * block, (t + 1) * LANE)
            sub = wb[blk, :, col - blk * block:col_end - blk * block]
            top = blk * block - start[t]
            pieces.append(jnp.pad(sub, ((top, kb - top - block), (0, 0))))
            col = col_end
        tiles.append(jnp.concatenate(pieces, axis=1))
    return jnp.stack(tiles)


def _gelu_tanh(x):
    cdf = 0.5 * (1.0 + jnp.tanh(np.float32(np.sqrt(2.0 / np.pi)) * (x + 0.044715 * (x * x * x))))
    return x * cdf


def _lru_pass_kernel(*refs, reverse, starts, kb, bt, nb, taps):
    left = taps // 2
    right = taps - 1 - left
    it = iter(refs)
    if reverse:
        conv_ref = next(it)
    else:
        xp_ref, x_ref = next(it), next(it)
        xn_ref = next(it) if right > 0 else None
        cw_ref, cb_ref = next(it), next(it)
    wa_ref, wi_ref, ba_ref, bi_ref, lam_ref, h0_ref = (next(it) for _ in range(6))
    hsf_ref, gate_ref = (next(it), next(it)) if reverse else (None, None)
    out_ref, ht_ref = next(it), next(it)
    conv_out_ref = None if reverse else next(it)
    xc_s, xb_s, a_s, bx_s, carry = next(it), next(it), next(it), next(it), next(it)

    step = pl.program_id(0)
    n_steps = pl.num_programs(0)
    n_tiles = len(starts)
    rows = nb * bt

    @pl.when(step == 0)
    def _():
        carry[...] = h0_ref[...]

    for t in range(n_tiles):
        lanes = slice(t * LANE, (t + 1) * LANE)
        if reverse:
            acc = conv_ref[:, :, lanes].reshape(rows, LANE)
        else:
            parts = [jnp.where(step > 0, xp_ref[:, :, lanes], 0.0), x_ref[:, :, lanes]]
            if right > 0:
                parts.append(jnp.where(step < n_steps - 1, xn_ref[:, :, lanes], 0.0))
            full = jnp.concatenate(parts, axis=0)
            acc = jnp.broadcast_to(cb_ref[:, lanes], (bt, nb, LANE))
            for k in range(taps):
                acc = acc + full[k:k + bt] * cw_ref[k:k + 1, lanes]
            conv_out_ref[:, :, lanes] = acc
            acc = acc.reshape(rows, LANE)
        xc_s[:, lanes] = acc
        xb_s[:, lanes] = acc.astype(BF16)

    neg_lam = -lam_ref[...]
    softplus = jnp.maximum(neg_lam, 0.0) + jnp.log1p(jnp.exp(-jnp.abs(neg_lam)))
    half_rate = (-0.5 * LRU_C) * softplus
    half_ba, half_bi = 0.5 * ba_ref[...], 0.5 * bi_ref[...]
    for t in range(n_tiles):
        lanes = slice(t * LANE, (t + 1) * LANE)
        xw = xb_s[:, starts[t]:starts[t] + kb]
        tanh_a = jnp.tanh(jnp.dot(xw, wa_ref[t], preferred_element_type=F32) + half_ba[:, lanes])
        tanh_i = jnp.tanh(jnp.dot(xw, wi_ref[t], preferred_element_type=F32) + half_bi[:, lanes])
        log_a = half_rate[:, lanes] * tanh_a + half_rate[:, lanes]
        a = jnp.exp(log_a)
        half_x = 0.5 * xc_s[:, lanes]
        a_s[:, lanes] = a
        bx_s[:, lanes] = jnp.sqrt(-jnp.tanh(log_a) * (a * a + 1.0)) * (half_x * tanh_i + half_x)

    h = carry[...]
    for s in range(bt):
        ts = (bt - 1 - s) if reverse else s
        slab = slice(ts * nb, (ts + 1) * nb)
        h = a_s[slab, :] * h + bx_s[slab, :]
        a_s[slab, :] = h
    carry[...] = h
    ht_ref[...] = h
    hs = a_s[...].reshape(bt, nb, a_s.shape[1])
    if reverse:
        out_ref[...] = (_gelu_tanh(gate_ref[...]) * (hsf_ref[...] + hs)).astype(out_ref.dtype)
    else:
        out_ref[...] = hs


def _lru_pass(u, st, conv_w, conv_b, wa, wi, b_a, b_i, lam, h0, starts, kb, *, reverse, fwd=None):
    c = conv_w.shape[1]
    taps = conv_w.shape[0]
    left, right = taps // 2, taps - 1 - taps // 2
    nb, seq = st.nb, st.seq
    bt = min(max(LRU_ROWS_PER_STEP // nb, SUBLANE), seq)
    assert seq % bt == 0 and c % LANE == 0 and left > 0 and bt % left == 0 and (right == 0 or bt % right == 0)
    nt = seq // bt
    n_tiles = c // LANE
    u3 = u.reshape(seq, nb, 2 * c)
    tmap = (lambda s: nt - 1 - s) if reverse else (lambda s: s)
    full = lambda *shape: pl.BlockSpec(shape, lambda s: (0,) * len(shape))
    tile_spec = lambda col: pl.BlockSpec((bt, nb, c), lambda s: (tmap(s), 0, col))
    if reverse:
        in_specs, args = [tile_spec(0)], [fwd[1]]
    else:
        in_specs = [pl.BlockSpec((left, nb, c), lambda s: (jnp.maximum(s * (bt // left) - 1, 0), 0, 0)), tile_spec(0)]
        args = [u3, u3]
        if right > 0:
            in_specs.append(pl.BlockSpec((right, nb, c),
                                         lambda s: (jnp.minimum((s + 1) * (bt // right), seq // right - 1), 0, 0)))
            args.append(u3)
        in_specs += [full(taps, c), full(1, c)]
        args += [conv_w, conv_b.reshape(1, c)]
    in_specs += [full(n_tiles, kb, LANE), full(n_tiles, kb, LANE), full(1, c), full(1, c), full(1, c), full(nb, c)]
    args += [wa, wi, b_a.reshape(1, c), b_i.reshape(1, c), lam.reshape(1, c), h0]
    out_specs = [tile_spec(0), pl.BlockSpec((nb, c), lambda s: (0, 0))]
    out_shape = [jax.ShapeDtypeStruct((seq, nb, c), F32), jax.ShapeDtypeStruct((nb, c), F32)]
    if reverse:
        in_specs += [tile_spec(0), tile_spec(1)]
        args += [fwd[0], u3]
    else:
        out_specs.append(tile_spec(0))
        out_shape.append(jax.ShapeDtypeStruct((seq, nb, c), F32))
    kern = functools.partial(_lru_pass_kernel, reverse=reverse, starts=tuple(starts), kb=kb, bt=bt, nb=nb, taps=taps)
    blk = nb * bt * c * 4 / MIB
    vmem = (2 + 2 + 3 + (4 if reverse else 2) + 4) * blk + 4 * n_tiles * kb * LANE * 2 / MIB + 8
    return pl.pallas_call(
        kern,
        grid=(nt,),
        in_specs=in_specs,
        out_specs=out_specs,
        out_shape=out_shape,
        scratch_shapes=[pltpu.VMEM((nb * bt, c), F32), pltpu.VMEM((nb * bt, c), BF16),
                        pltpu.VMEM((nb * bt, c), F32), pltpu.VMEM((nb * bt, c), F32), pltpu.VMEM((nb, c), F32)],
        compiler_params=_cparams(1, vmem),
        name="lru_bwd" if reverse else "lru_fwd",
    )(*args)


def _mixer_nat(xs, streams, mods, cache_k, cache_v, j, w_qkv, g_mix, g_q, g_k, rpb, w_o):
    heads, dh = rpb.shape[0], g_q.shape[0]
    w_qkv, w_o = w_qkv.astype(BF16), w_o.astype(BF16)
    gains = jnp.concatenate([jnp.tile(g_q * (dh ** -0.5 * LOG2E), heads), jnp.tile(g_k, heads),
                             jnp.ones((heads * dh,), F32)])[None]
    spec = dict(head_w=dh, norm_div=dh, norm_cols=2 * heads * dh, gains=gains)
    new_x, extra = [], None
    for x, st in zip(xs, streams):
        latent = not st.shared
        qkv = _proj(x, w_qkv, st, norm_g=g_mix, mod=(mods[0], mods[1]), heads=spec,
                    out_dtype=BF16 if latent else F32, name="nat_qkv")
        if latent:
            o = _nat_attention(qkv, st, cache_k, cache_v, j, rpb, dh)
        else:
            o, kc, vc = _ctx_attention(qkv, qkv, qkv, st, n_heads=heads, dq=dh, dv=dh, q_col=0, k_col=heads * dh,
                                       v_col=2 * heads * dh, emit_kv=True)
            extra = (kc, vc)
        new_x.append(_proj(o, w_o, st, res=x, gate=mods[2], name="nat_out"))
    return new_x, extra


def _mixer_lru(xs, streams, mods, state, w_in, g_mix, conv_w, conv_b, w_a, b_a, w_i, b_i, lam, w_out):
    c = conv_w.shape[1]
    block = w_a.shape[-1]
    w_in, w_out = w_in.astype(BF16), w_out.astype(BF16)
    starts, kb = _band_plan(c, block)
    wa = [_band_weights(0.5 * w_a[d], c, block, starts, kb) for d in range(2)]
    wi = [_band_weights(0.5 * w_i[d], c, block, starts, kb) for d in range(2)]
    new_x, st_out = [], None
    for x, st in zip(xs, streams):
        latent = not st.shared
        h0 = state.astype(F32) if latent else jnp.zeros((st.nb, 2, c), F32)
        u = _proj(x, w_in, st, norm_g=g_mix, mod=(mods[0], mods[1]), out_time_major=True, name="lru_in")
        hs_f, t_f, conv = _lru_pass(u, st, conv_w, conv_b, wa[0], wi[0], b_a[0], b_i[0], lam[0], h0[:, 0], starts,
                                    kb, reverse=False)
        y, t_b = _lru_pass(u, st, conv_w, conv_b, wa[1], wi[1], b_a[1], b_i[1], lam[1], h0[:, 1], starts, kb,
                           reverse=True, fwd=(hs_f, conv))
        if not latent:
            st_out = jnp.stack([t_f, t_b], axis=1)
        new_x.append(_proj(y.reshape(st.seq, st.nb * c), w_out, st, res=x, gate=mods[2], x_time_major=True,
                           name="lru_out"))
    return new_x, st_out


def _mixer_mla(xs, streams, mods, cache_ckv, cache_kr, w_down, g_mix, g_qa, g_kva, w_uq, w_ukv, g_q, g_k, w_o):
    d_model = w_down.shape[0]
    q_rank, kv_rank = g_qa.shape[0], g_kva.shape[0]
    qk_dim = g_q.shape[0]
    heads = w_uq.shape[1] // qk_dim
    rope = w_down.shape[1] - q_rank - kv_rank
    nope = qk_dim - rope
    assert nope == LANE and rope <= LANE and kv_rank % LANE == 0 and q_rank % LANE == 0
    head_w = 2 * LANE
    q_pad = -q_rank % kv_rank
    kv_col = q_rank + q_pad
    zeros = lambda width: jnp.zeros((d_model, width), F32)
    w_qd, w_kvd, w_rd = w_down[:, :q_rank], w_down[:, q_rank:q_rank + kv_rank], w_down[:, q_rank + kv_rank:]
    if q_pad >= LANE:
        kr_col = q_rank
        w_dn = jnp.concatenate([w_qd, w_rd, zeros(q_pad - rope), w_kvd], axis=1).astype(BF16)
    else:
        kr_col = kv_col + kv_rank
        w_dn = jnp.concatenate([w_qd, zeros(q_pad), w_kvd, w_rd, zeros(LANE - rope)], axis=1).astype(BF16)
    kr_blk = kr_col // LANE
    w_q = jnp.pad(w_uq.reshape(q_rank, heads, qk_dim), ((0, 0), (0, 0), (0, head_w - qk_dim)))
    w_q = w_q.reshape(q_rank, heads * head_w).astype(BF16)
    w_ukv, w_o = w_ukv.astype(BF16), w_o.astype(BF16)
    gq = jnp.tile(jnp.pad(g_q * (qk_dim ** -0.5 * LOG2E), (0, head_w - qk_dim)), heads)[None]
    g1, g2 = g_k[None, :nope], jnp.pad(g_k[nope:], (0, LANE - rope))[None]
    p = cache_ckv.shape[1]
    new_x, extra = [], None
    for x, st in zip(xs, streams):
        latent = not st.shared
        d = _proj(x, w_dn, st, norm_g=g_mix, mod=(mods[0], mods[1]), name="mla_down")
        q_tabs = _rope_tables(st.seq, rope, nope, head_w) if latent else None
        k_tabs = _rope_tables(st.seq, rope, 0, LANE) if latent else None
        q = _proj(d, w_q, st, x_block=(q_rank, 0), norm_g=g_qa[None],
                  heads=dict(head_w=head_w, norm_div=qk_dim, norm_cols=heads * head_w, gains=gq, tabs=q_tabs,
                             rope_tiles=(False, True)),
                  out_dtype=BF16, name="mla_uq")
        kv = _mla_kv(d, (kv_rank, kv_col // kv_rank), w_ukv, d, (LANE, kr_blk), g_kva[None], g1, g2, k_tabs, st,
                     n_heads=heads, norm_div=qk_dim, emit_xn=not latent, name="mla_ukv")
        if latent:
            k, v = kv
            cst = _Stream(st.nb, p, 0, True)
            krc = jnp.pad(cache_kr.reshape(st.nb * p, rope), ((0, 0), (0, LANE - rope)))
            kc, vc = _mla_kv(cache_ckv.reshape(st.nb * p, kv_rank), (kv_rank, 0), w_ukv, krc, (LANE, 0), None,
                             g1, g2, None, cst, n_heads=heads, norm_div=qk_dim, emit_xn=False,
                             name="mla_ukv_cache")
            o = _joint_dense_attention(q, k, v, kc, vc, st, p, n_heads=heads, dq=head_w, dv=LANE)
        else:
            k, v, ckv = kv
            o = _ctx_attention(q, k, v, st, n_heads=heads, dq=head_w, dv=LANE, q_col=0, k_col=0, v_col=0)
            kr_out = d[:, kr_col:kr_col + rope]
            extra = (ckv.reshape(st.nb, st.seq, kv_rank), kr_out.reshape(st.nb, st.seq, rope))
        new_x.append(_proj(o, w_o, st, res=x, gate=mods[2], name="mla_out"))
    return new_x, extra


def _mixer_swa(xs, streams, mods, cache_k, cache_v, j, w_qkv, g_mix, g_q, g_k, sinks, w_o):
    dh = g_q.shape[0]
    heads = sinks.shape[0]
    kvh = (w_qkv.shape[1] // dh - heads) // 2
    w_qkv, w_o = w_qkv.astype(BF16), w_o.astype(BF16)
    gains = jnp.concatenate([jnp.tile(g_q * (dh ** -0.5 * LOG2E), heads), jnp.tile(g_k, kvh),
                             jnp.ones((kvh * dh,), F32)])[None]
    sinks = sinks.astype(F32) * LOG2E
    new_x, extra = [], None
    for x, st in zip(xs, streams):
        latent = not st.shared
        tabs = _rope_tables(st.seq, dh, 0, dh) if latent else None
        if tabs is not None:
            tabs = tuple(jnp.tile(t, (1, LANE // dh)) for t in tabs)
        spec = dict(head_w=dh, norm_div=dh, norm_cols=(heads + kvh) * dh, gains=gains, tabs=tabs,
                    rope_tiles=(True,))
        qkv = _proj(x, w_qkv, st, norm_g=g_mix, mod=(mods[0], mods[1]), heads=spec,
                    out_dtype=BF16 if latent else F32, name="swa_qkv")
        if latent:
            o = _swa_attention(qkv, st, sinks, heads=heads, kvh=kvh, dh=dh, cache=(cache_k, cache_v, j))
        else:
            o, kc, vc = _swa_attention(qkv, st, sinks, heads=heads, kvh=kvh, dh=dh)
            extra = (kc, vc)
        new_x.append(_proj(o, w_o, st, res=x, gate=mods[2], name="swa_out"))
    return new_x, extra


def kernel(x_prompt, x_sample, cache_nat_k, cache_nat_v, state_lru, cache_mla_ckv, cache_mla_krope, cache_swa_k, cache_swa_v, c, c_ctx, norm_mix, norm_ffn, w_mod, b_mod, ffn_w_in, ffn_conv_w, ffn_conv_b, ffn_w_out, nat_w_qkv, nat_q_norm, nat_k_norm, nat_rpb, nat_w_o, lru_w_in, lru_conv_w, lru_conv_b, lru_w_a, lru_b_a, lru_w_i, lru_b_i, lru_lambda, lru_w_out, mla_w_down, mla_q_a_norm, mla_kv_a_norm, mla_w_uq, mla_w_ukv, mla_q_norm, mla_k_norm, mla_w_o, swa_w_qkv, swa_q_norm, swa_k_norm, swa_sinks, swa_w_o):
    bc, sc, d = x_prompt.shape
    bl, n, _ = x_sample.shape
    depth = w_mod.shape[0]
    streams = (_Stream(bc, sc, 0, True), _Stream(bl, n, 1, False))
    xs = [x_prompt.reshape(bc * sc, d), x_sample.reshape(bl * n, d)]

    n_cond = 1 + bl
    cond_rows = -(-n_cond // SUBLANE) * SUBLANE
    cond = jnp.zeros((cond_rows, d), F32).at[0].set(c_ctx).at[1:n_cond].set(c)
    mods = _modulation(cond, w_mod, b_mod)[:, :n_cond]

    nat_k_l, nat_v_l, lru_l, ckv_l, krope_l, swa_k_l, swa_v_l = [], [], [], [], [], [], []
    for l in range(depth):
        kind, j = l % 4, l // 4
        m6 = [mods[l, :, None, t * d:(t + 1) * d] for t in range(6)]
        g_mix = norm_mix[l].reshape(1, d)
        if kind == 0:
            xs, (kc, vc) = _mixer_nat(xs, streams, m6, cache_nat_k, cache_nat_v, j, nat_w_qkv[j], g_mix,
                                      nat_q_norm[j], nat_k_norm[j], nat_rpb[j], nat_w_o[j])
            nat_k_l.append(kc)
            nat_v_l.append(vc)
        elif kind == 1:
            xs, st = _mixer_lru(xs, streams, m6, state_lru[:, j], lru_w_in[j], g_mix, lru_conv_w[j], lru_conv_b[j],
                                lru_w_a[j], lru_b_a[j], lru_w_i[j], lru_b_i[j], lru_lambda[j], lru_w_out[j])
            lru_l.append(st)
        elif kind == 2:
            xs, (ckv, kr) = _mixer_mla(xs, streams, m6, cache_mla_ckv[:, j], cache_mla_krope[:, j], mla_w_down[j],
                                       g_mix, mla_q_a_norm[j], mla_kv_a_norm[j], mla_w_uq[j], mla_w_ukv[j],
                                       mla_q_norm[j], mla_k_norm[j], mla_w_o[j])
            ckv_l.append(ckv)
            krope_l.append(kr)
        else:
            xs, (kc, vc) = _mixer_swa(xs, streams, m6, cache_swa_k, cache_swa_v, j, swa_w_qkv[j], g_mix,
                                      swa_q_norm[j], swa_k_norm[j], swa_sinks[j], swa_w_o[j])
            swa_k_l.append(kc)
            swa_v_l.append(vc)
        w_in, w_out = ffn_w_in[l].astype(BF16), ffn_w_out[l].astype(BF16)
        xs = [_conv_ffn(x, st, norm_ffn[l].reshape(1, d), m6[3], m6[4], m6[5], w_in, ffn_conv_w[l],
                        ffn_conv_b[l], w_out) for x, st in zip(xs, streams)]

    return (xs[0].reshape(bc, sc, d), xs[1].reshape(bl, n, d), jnp.stack(nat_k_l, axis=1),
            jnp.stack(nat_v_l, axis=1), jnp.stack(lru_l, axis=1), jnp.stack(ckv_l, axis=1),
            jnp.stack(krope_l, axis=1), jnp.stack(swa_k_l, axis=1), jnp.stack(swa_v_l, axis=1))
```

```python
import functools

import numpy as np
import jax
import jax.numpy as jnp
from jax import lax
from jax.experimental import pallas as pl
from jax.experimental.pallas import tpu as pltpu

F32 = jnp.float32
BF16 = jnp.bfloat16

GRID_W = 64
NA_WIN_ROWS = 8
NA_WIN_COLS = 16
NA_Q_ROWS = 8
NA_K_ROWS = 16
LRU_C = 8.0
SWA_WINDOW = 128
SWA_BLOCK = 128
ROPE_BASE = 10000.0
ROPE_GROUP = 32
EPS = 1e-6
NEG = -1e30
LOG2E = float(np.log2(np.e))
LANE = 128
SUBLANE = 8
HALO = 16
MIB = 1024 * 1024
VMEM_LIMIT_CAP_MIB = 60
ROW_TILES = (1024, 512, 256, 128, 64, 32, 16)
MAX_COL_TILE = 1024
MAX_COL_TILE_RESIDUAL = 512
FFN_ROW_TILE, FFN_FF_CHUNK = 1024, 512
HEADS_PER_STEP = (4, 2, 1)
MLA_KEY_CHUNK = 1024
LRU_ROWS_PER_STEP = 256


def _cparams(n_axes, vmem_mib):
    return pltpu.CompilerParams(dimension_semantics=("arbitrary",) * n_axes,
                                vmem_limit_bytes=int(min(vmem_mib, VMEM_LIMIT_CAP_MIB) * MIB))


def _largest_divisor(n, candidates):
    for c in candidates:
        if n % c == 0:
            return c
    return n


class _Stream:
    def __init__(self, nb, seq, mod0, shared_mod):
        self.nb, self.seq, self.rows, self.mod0, self.shared = nb, seq, nb * seq, mod0, shared_mod
        self.bm = _largest_divisor(self.rows if shared_mod else seq, ROW_TILES)

    def mod_index(self, row0):
        return self.mod0 if self.shared else self.mod0 + row0 // self.seq


def _norm_mod(x, g, shift, scale):
    ms = jnp.mean(x * x, axis=-1, keepdims=True)
    return (x * lax.rsqrt(ms + EPS)) * (g * (1.0 + scale)) + shift


def _rms(x, g):
    return (x * lax.rsqrt(jnp.mean(x * x, axis=-1, keepdims=True) + EPS)) * g


def _modulation_kernel(c_ref, w_ref, b_ref, o_ref):
    c = c_ref[...]
    sc = (c * jax.nn.sigmoid(c)).astype(BF16)
    o_ref[...] = jnp.dot(sc, w_ref[...].astype(BF16), preferred_element_type=F32) + b_ref[...]


def _modulation(cond, w_mod, b_mod):
    depth, d, n = w_mod.shape
    rows = cond.shape[0]
    bn = _largest_divisor(n, (512, 256, 128))
    return pl.pallas_call(
        _modulation_kernel,
        grid=(depth, n // bn),
        in_specs=[pl.BlockSpec((rows, d), lambda l, j: (0, 0)),
                  pl.BlockSpec((None, d, bn), lambda l, j: (l, 0, j)),
                  pl.BlockSpec((None, 1, bn), lambda l, j: (l, 0, j))],
        out_specs=pl.BlockSpec((None, rows, bn), lambda l, j: (l, 0, j)),
        out_shape=jax.ShapeDtypeStruct((depth, rows, n), F32),
        compiler_params=_cparams(2, 32),
        name="modulation",
    )(cond, w_mod, b_mod.reshape(depth, 1, n))


def _rope_tables(n_tokens, rot_dim, lead, width):
    t = jnp.arange(n_tokens)
    row = (t // GRID_W).astype(F32)
    col = (t % GRID_W).astype(F32)
    half = rot_dim // 2
    inv = ROPE_BASE ** (-jnp.arange(0, half, 2, dtype=F32) / half)
    ar = row[:, None] * inv
    ac = col[:, None] * inv
    ang = jnp.concatenate([ar, ar, ac, ac], axis=-1)
    cos, sin = jnp.cos(ang), jnp.sin(ang)
    first = (np.arange(rot_dim) % ROPE_GROUP) < ROPE_GROUP // 2
    sin_a = jnp.where(first, -sin, 0.0)
    sin_b = jnp.where(first, 0.0, sin)
    pad = ((0, 0), (lead, width - lead - rot_dim))
    return (jnp.pad(cos, pad, constant_values=1.0), jnp.pad(sin_a, pad), jnp.pad(sin_b, pad))


def _rope_apply(y, cos, sin_a, sin_b):
    shift = ROPE_GROUP // 2
    return y * cos + pltpu.roll(y, LANE - shift, 1) * sin_a + pltpu.roll(y, shift, 1) * sin_b


def _rotate_half_matrix():
    shift = ROPE_GROUP // 2
    src = lax.broadcasted_iota(jnp.int32, (LANE, LANE), 0)
    dst = lax.broadcasted_iota(jnp.int32, (LANE, LANE), 1)
    first = (dst % ROPE_GROUP) < shift
    return jnp.where(first & (src == dst + shift), -1.0,
                     jnp.where(jnp.logical_not(first) & (src == dst - shift), 1.0, 0.0)).astype(BF16)


def _rope_apply_mxu(y, cos, sin_a, sin_b, perm):
    rot = _dot_hi_lo(y, perm)
    return y * cos + rot * (sin_b - sin_a)


def _dot_hi_lo(y, m):
    hi = y.astype(BF16)
    lo = (y - hi.astype(F32)).astype(BF16)
    return jnp.dot(jnp.concatenate([hi, lo], axis=1), jnp.concatenate([m, m], axis=0), preferred_element_type=F32)


def _fill_lhs(x_ref, xs_ref, xn_ref, prologue, g_ref, sh_ref, sc_ref, row_chunk):
    bm = x_ref.shape[0]

    def chunk(r, carry):
        rows = pl.ds(pl.multiple_of(r * row_chunk, row_chunk), row_chunk)
        x = x_ref[rows, :].astype(F32)
        if prologue == "norm_mod":
            x = _norm_mod(x, g_ref[...], sh_ref[...], sc_ref[...])
        elif prologue == "norm":
            x = _rms(x, g_ref[...])
        if xn_ref is not None:
            xn_ref[rows, :] = x
        xs_ref[rows, :] = x.astype(BF16)
        return carry
    n_chunks = bm // row_chunk
    lax.fori_loop(0, n_chunks, chunk, 0, unroll=2 if n_chunks % 2 == 0 else 1)


def _head_norm_store(acc, o_ref, hg_ref, tabs, head_w, norm_div, col0, norm_cols, rope_tiles):
    bn = acc.shape[1]
    period = tabs[0].shape[1] if tabs is not None else LANE
    perm = _rotate_half_matrix() if tabs is not None and head_w >= LANE else None
    if head_w < LANE:
        head_of_row = lax.broadcasted_iota(jnp.int32, (LANE, LANE), 0) // head_w
        head_of_col = lax.broadcasted_iota(jnp.int32, (LANE, LANE), 1) // head_w
        same_head = jnp.where(head_of_row == head_of_col, 1.0, 0.0).astype(BF16)
    for s0 in range(0, bn, max(head_w, LANE)):
        normed = None if norm_cols is None else (col0 + s0 < norm_cols)
        tiles = [acc[:, s0 + k * LANE:s0 + (k + 1) * LANE] for k in range(max(head_w, LANE) // LANE)]
        if head_w >= LANE:
            sq = None
            for y in tiles:
                sq = y * y if sq is None else sq + y * y
            inv = lax.rsqrt(jnp.sum(sq, axis=-1, keepdims=True) / norm_div + EPS)
        else:
            inv = lax.rsqrt(_dot_hi_lo(tiles[0] * tiles[0], same_head) / norm_div + EPS)
        if normed is not None:
            inv = jnp.where(normed, inv, 1.0)
        for k, y in enumerate(tiles):
            c0 = s0 + k * LANE
            y = (y * inv) * hg_ref[:, c0:c0 + LANE]
            t0 = c0 % period
            if tabs is not None and rope_tiles[t0 // LANE]:
                tab = tuple(t[:, t0:t0 + LANE] for t in tabs)
                rotated = _rope_apply(y, *tab) if head_w < LANE else _rope_apply_mxu(y, *tab, perm)
                y = rotated if normed is None else jnp.where(normed, rotated, y)
            o_ref[:, c0:c0 + LANE] = y.astype(o_ref.dtype)


def _proj_kernel(*refs, prologue, emit_xn, epilogue, head_w, norm_div, norm_cols, rope, rope_tiles, row_chunk):
    it = iter(refs)
    x_ref = next(it)
    g_ref = next(it) if prologue is not None else None
    sh_ref, sc_ref = (next(it), next(it)) if prologue == "norm_mod" else (None, None)
    w_ref = next(it)
    if epilogue == "res":
        res_ref, gate_ref = next(it), next(it)
    if epilogue == "heads":
        hg_ref = next(it)
        tabs = (next(it), next(it), next(it)) if rope else None
    o_ref = next(it)
    xn_ref = next(it) if emit_xn else None
    xs_ref = next(it, None)
    j = pl.program_id(1)
    bn = o_ref.shape[1]

    if xs_ref is None:
        xs_ref = x_ref
    else:
        @pl.when(j == 0)
        def _():
            _fill_lhs(x_ref, xs_ref, xn_ref, prologue, g_ref, sh_ref, sc_ref, row_chunk)

    acc = jnp.dot(xs_ref[...], w_ref[...], preferred_element_type=F32)
    if epilogue == "res":
        o_ref[...] = res_ref[...] + gate_ref[...] * acc
    elif epilogue == "heads":
        _head_norm_store(acc, o_ref, hg_ref, tabs, head_w, norm_div, j * bn, norm_cols, rope_tiles)
    else:
        o_ref[...] = acc.astype(o_ref.dtype)


def _proj(x, w, st, *, x_block=None, norm_g=None, mod=None, res=None, gate=None, heads=None, emit_xn=False,
          out_dtype=F32, bn=None, x_time_major=False, out_time_major=False, name="proj"):
    k, n = w.shape
    time_major = x_time_major or out_time_major
    bm = min(st.bm, st.seq) if time_major else st.bm
    tiles_per_seq = st.seq // bm if st.seq % bm == 0 else None
    if time_major:
        assert tiles_per_seq is not None and x_block is None
    if x_time_major:
        assert x.shape == (st.seq, st.nb * k)
        kidx = 0
    else:
        kx, kidx = x_block if x_block is not None else (x.shape[1], 0)
        assert kx == k and x.shape[0] == st.rows
    rows = st.rows
    prologue = None if norm_g is None else ("norm_mod" if mod is not None else "norm")
    epilogue = "res" if res is not None else ("heads" if heads is not None else None)
    rope = heads is not None and heads.get("tabs") is not None
    if bn is None:
        unit = LANE
        if epilogue == "heads":
            unit = max(heads["head_w"], LANE, heads["tabs"][0].shape[1] if rope else LANE)
        cap = MAX_COL_TILE_RESIDUAL if epilogue == "res" and x.dtype != BF16 else MAX_COL_TILE
        bn = next((c for c in range(cap, unit - 1, -unit) if n % c == 0), n)
    mod_idx = lambda i: st.mod_index(i * bm)

    if x_time_major:
        in_specs = [pl.BlockSpec((bm, k), lambda i, j: (i % tiles_per_seq, i // tiles_per_seq))]
    else:
        in_specs = [pl.BlockSpec((bm, k), lambda i, j: (i, kidx))]
    args = [x]
    if prologue is not None:
        in_specs.append(pl.BlockSpec((1, k), lambda i, j: (0, 0)))
        args.append(norm_g)
    if prologue == "norm_mod":
        in_specs += [pl.BlockSpec((None, 1, k), lambda i, j: (mod_idx(i), 0, 0))] * 2
        args += list(mod)
    in_specs.append(pl.BlockSpec((k, bn), lambda i, j: (0, j)))
    args.append(w.astype(BF16))
    if epilogue == "res":
        in_specs += [pl.BlockSpec((bm, bn), lambda i, j: (i, j)),
                     pl.BlockSpec((None, 1, bn), lambda i, j: (mod_idx(i), 0, j))]
        args += [res, gate]
    head_w = norm_div = 0
    norm_cols = rope_tiles = None
    if epilogue == "heads":
        head_w, norm_div = heads["head_w"], heads["norm_div"]
        norm_cols = heads["norm_cols"] if heads["norm_cols"] < n else None
        assert bn % max(head_w, LANE) == 0 and heads["norm_cols"] % max(head_w, LANE) == 0
        in_specs.append(pl.BlockSpec((1, bn), lambda i, j: (0, j)))
        args.append(heads["gains"])
        if rope:
            period = heads["tabs"][0].shape[1]
            rope_tiles = heads["rope_tiles"]
            assert bn % period == 0 and tiles_per_seq is not None and len(rope_tiles) == period // LANE
            in_specs += [pl.BlockSpec((bm, period), lambda i, j: (i % tiles_per_seq, 0))] * 3
            args += list(heads["tabs"])
    if out_time_major:
        n_col_tiles = n // bn
        out_shape = [jax.ShapeDtypeStruct((st.seq, st.nb * n), out_dtype)]
        out_specs = [pl.BlockSpec((bm, bn), lambda i, j: (i % tiles_per_seq, (i // tiles_per_seq) * n_col_tiles + j))]
    else:
        out_shape = [jax.ShapeDtypeStruct((rows, n), out_dtype)]
        out_specs = [pl.BlockSpec((bm, bn), lambda i, j: (i, j))]
    if emit_xn:
        out_shape.append(jax.ShapeDtypeStruct((rows, k), F32))
        out_specs.append(pl.BlockSpec((bm, k), lambda i, j: (i, 0)))
    xbytes = x.dtype.itemsize
    vmem = (2 * bm * k * xbytes + bm * k * 2 + 2 * k * bn * 2 + (6 if epilogue == "res" else 4) * bm * bn * 4
            + (2 * bm * k * 4 if emit_xn else 0)) / MIB + 8
    kern = functools.partial(_proj_kernel, prologue=prologue, emit_xn=emit_xn, epilogue=epilogue, head_w=head_w,
                             norm_div=norm_div, norm_cols=norm_cols, rope=rope, rope_tiles=rope_tiles,
                             row_chunk=min(bm, 128))
    direct_lhs = prologue is None and x.dtype == BF16 and not emit_xn
    out = pl.pallas_call(
        kern,
        grid=(rows // bm, n // bn),
        in_specs=in_specs,
        out_specs=out_specs,
        out_shape=out_shape,
        scratch_shapes=[] if direct_lhs else [pltpu.VMEM((bm, k), BF16)],
        compiler_params=_cparams(2, vmem),
        name=name,
    )(*args)
    return out if emit_xn else out[0]


def _ffn_kernel(xp_ref, x_ref, xn_ref, g_ref, sh_ref, sc_ref, gate_ref, wa_ref, wb_ref, cw_ref, cb_ref,
                wo_ref, o_ref, h_ref, *, bm, seq, row_chunk):
    i = pl.program_id(0)
    c = pl.program_id(1)
    n_chunks = pl.num_programs(1)

    @pl.when(c == 0)
    def _():
        g, sh, sc = g_ref[...], sh_ref[...], sc_ref[...]
        h_ref[0:HALO, :] = _norm_mod(xp_ref[...], g, sh, sc).astype(BF16)
        h_ref[HALO + bm:, :] = _norm_mod(xn_ref[...], g, sh, sc).astype(BF16)

        def chunk(r, carry):
            src = pl.ds(pl.multiple_of(r * row_chunk, row_chunk), row_chunk)
            dst = pl.ds(pl.multiple_of(HALO + r * row_chunk, HALO), row_chunk)
            h_ref[dst, :] = _norm_mod(x_ref[src, :], g, sh, sc).astype(BF16)
            return carry
        n_row_chunks = bm // row_chunk
        lax.fori_loop(0, n_row_chunks, chunk, 0, unroll=2 if n_row_chunks % 2 == 0 else 1)
        o_ref[...] = jnp.zeros_like(o_ref)

    ua = jnp.dot(h_ref[...], wa_ref[...], preferred_element_type=F32)
    ub = jnp.dot(h_ref[HALO:HALO + bm, :], wb_ref[...], preferred_element_type=F32)
    n_all = bm + 2 * HALO
    u_prev = pltpu.roll(ua, 1, 0)[HALO:HALO + bm]
    u_next = pltpu.roll(ua, n_all - 1, 0)[HALO:HALO + bm]
    u_mid = ua[HALO:HALO + bm]
    pos = jnp.bitwise_and(i * bm + lax.broadcasted_iota(jnp.int32, (bm, 1), 0), seq - 1)
    u_prev = jnp.where(pos == 0, 0.0, u_prev)
    u_next = jnp.where(pos == seq - 1, 0.0, u_next)
    cw = cw_ref[...]
    a = cb_ref[...] + u_prev * cw[0:1] + u_mid * cw[1:2] + u_next * cw[2:3]
    gated = ((a * jax.nn.sigmoid(a)) * ub).astype(BF16)
    o_ref[...] += jnp.dot(gated, wo_ref[...], preferred_element_type=F32)

    @pl.when(c == n_chunks - 1)
    def _():
        o_ref[...] = x_ref[...] + gate_ref[...] * o_ref[...]


def _conv_ffn(x, st, g, shift, scale, gate, w_in, conv_w, conv_b, w_out, bm=FFN_ROW_TILE, ck=FFN_FF_CHUNK):
    m, d = x.shape
    d_ff = w_out.shape[0]
    bm = min(st.bm, bm)
    ck = _largest_divisor(d_ff, tuple(c for c in (512, 256, 128) if c <= ck))
    n_chunks = d_ff // ck
    n_halo_blocks = m // HALO
    assert st.seq & (st.seq - 1) == 0 and conv_w.shape[0] == 3
    mod_idx = lambda i: st.mod_index(i * bm)
    kern = functools.partial(_ffn_kernel, bm=bm, seq=st.seq, row_chunk=min(bm, 128))
    vmem = (4 * bm * d * 4 + (bm + 2 * HALO) * d * 2 + 6 * d * ck * 2 + 5 * (bm + 2 * HALO) * ck * 4) / MIB + 4
    return pl.pallas_call(
        kern,
        grid=(m // bm, n_chunks),
        in_specs=[
            pl.BlockSpec((HALO, d), lambda i, c: (jnp.maximum(i * (bm // HALO) - 1, 0), 0)),
            pl.BlockSpec((bm, d), lambda i, c: (i, 0)),
            pl.BlockSpec((HALO, d), lambda i, c: (jnp.minimum((i + 1) * (bm // HALO), n_halo_blocks - 1), 0)),
            pl.BlockSpec((1, d), lambda i, c: (0, 0)),
            pl.BlockSpec((None, 1, d), lambda i, c: (mod_idx(i), 0, 0)),
            pl.BlockSpec((None, 1, d), lambda i, c: (mod_idx(i), 0, 0)),
            pl.BlockSpec((None, 1, d), lambda i, c: (mod_idx(i), 0, 0)),
            pl.BlockSpec((d, ck), lambda i, c: (0, c)),
            pl.BlockSpec((d, ck), lambda i, c: (0, n_chunks + c)),
            pl.BlockSpec((conv_w.shape[0], ck), lambda i, c: (0, c)),
            pl.BlockSpec((1, ck), lambda i, c: (0, c)),
            pl.BlockSpec((ck, d), lambda i, c: (c, 0)),
        ],
        out_specs=pl.BlockSpec((bm, d), lambda i, c: (i, 0)),
        out_shape=jax.ShapeDtypeStruct((m, d), F32),
        scratch_shapes=[pltpu.VMEM((bm + 2 * HALO, d), BF16)],
        compiler_params=_cparams(2, vmem),
        name="conv_ffn",
    )(x, x, x, g, shift, scale, gate, w_in, w_in, conv_w, conv_b.reshape(1, d_ff), w_out)


def _qk(q, k):
    return lax.dot_general(q, k, (((1,), (1,)), ((), ())), preferred_element_type=F32)


def _attend(scores, values, sink=None):
    m = None
    for s in scores:
        mi = jnp.max(s, axis=-1, keepdims=True)
        m = mi if m is None else jnp.maximum(m, mi)
    if sink is not None:
        m = jnp.maximum(m, sink)
    es = [jnp.exp2(s - m) for s in scores]
    den = None
    for e in es:
        di = jnp.sum(e, axis=-1, keepdims=True)
        den = di if den is None else den + di
    if sink is not None:
        den = den + jnp.exp2(sink - m)
    out = None
    for e, v in zip(es, values):
        oi = jnp.dot(e.astype(BF16), v, preferred_element_type=F32)
        out = oi if out is None else out + oi
    return out * (1.0 / den)


def _ctx_attn_kernel(q_ref, k_ref, v_ref, *outs, heads, dq, dv, emit_kv):
    o_ref = outs[0]
    for h in range(heads):
        q = q_ref[:, h * dq:(h + 1) * dq].astype(BF16)
        k = k_ref[:, h * dq:(h + 1) * dq]
        v = v_ref[:, h * dv:(h + 1) * dv]
        if emit_kv:
            outs[1][h] = k.astype(F32)
            outs[2][h] = v.astype(F32)
        o = _attend([_qk(q, k.astype(BF16))], [v.astype(BF16)])
        o_ref[:, h * dv:(h + 1) * dv] = o.astype(o_ref.dtype)


def _ctx_attention(qm, km, vm, st, *, n_heads, dq, dv, q_col, k_col, v_col, emit_kv=False):
    hb = _largest_divisor(n_heads, HEADS_PER_STEP)
    s = st.seq
    assert q_col % (hb * dq) == 0 and k_col % (hb * dq) == 0 and v_col % (hb * dv) == 0
    qo, ko, vo = q_col // (hb * dq), k_col // (hb * dq), v_col // (hb * dv)
    out_shape = [jax.ShapeDtypeStruct((st.rows, n_heads * dv), BF16)]
    out_specs = [pl.BlockSpec((s, hb * dv), lambda b, g: (b, g))]
    if emit_kv:
        out_shape += [jax.ShapeDtypeStruct((st.nb, n_heads, s, dq), F32),
                      jax.ShapeDtypeStruct((st.nb, n_heads, s, dv), F32)]
        out_specs += [pl.BlockSpec((None, hb, s, dq), lambda b, g: (b, g, 0, 0)),
                      pl.BlockSpec((None, hb, s, dv), lambda b, g: (b, g, 0, 0))]
    out = pl.pallas_call(
        functools.partial(_ctx_attn_kernel, heads=hb, dq=dq, dv=dv, emit_kv=emit_kv),
        grid=(st.nb, n_heads // hb),
        in_specs=[pl.BlockSpec((s, hb * dq), lambda b, g: (b, qo + g)),
                  pl.BlockSpec((s, hb * dq), lambda b, g: (b, ko + g)),
                  pl.BlockSpec((s, hb * dv), lambda b, g: (b, vo + g))],
        out_specs=out_specs,
        out_shape=out_shape,
        compiler_params=_cparams(2, 32),
        name="ctx_attention",
    )(qm, km, vm)
    return out if emit_kv else out[0]


def _nat_kernel(q_ref, k_ref, v_ref, kc_ref, vc_ref, bias_ref, o_ref, *, key_rows, rows, heads, dh):
    i = pl.program_id(2)
    n_keys = key_rows * GRID_W
    first_row = jnp.clip(i * NA_Q_ROWS - NA_WIN_ROWS // 2, 0, rows - key_rows)
    start = pl.multiple_of(first_row * GRID_W, GRID_W * 4)
    for h in range(heads):
        lanes = slice(h * dh, (h + 1) * dh)
        q = q_ref[:, lanes]
        k = k_ref[pl.ds(start, n_keys), lanes]
        v = v_ref[pl.ds(start, n_keys), lanes]
        s_loc = _qk(q, k) + bias_ref[h]
        s_ctx = _qk(q, kc_ref[h].astype(BF16))
        o_ref[:, lanes] = _attend([s_loc, s_ctx], [v, vc_ref[h].astype(BF16)]).astype(o_ref.dtype)


def _nat_bias(rpb, rows):
    n_blocks = rows // NA_Q_ROWS
    key_rows = min(NA_K_ROWS, rows)
    wr = min(NA_WIN_ROWS, rows)
    reps = [0, min(1, n_blocks - 1), n_blocks - 1]
    heads = rpb.shape[0]
    nq, nk = NA_Q_ROWS * GRID_W, key_rows * GRID_W
    shape = (NA_Q_ROWS, GRID_W, key_rows, GRID_W)
    qc = np.arange(GRID_W)
    cstart = np.clip(qc - NA_WIN_COLS // 2, 0, GRID_W - NA_WIN_COLS)
    col_ok = (qc[None, :] >= cstart[:, None]) & (qc[None, :] < cstart[:, None] + NA_WIN_COLS)
    rp = jnp.pad(rpb.astype(F32) * LOG2E,
                 ((0, 0), (key_rows, key_rows), (GRID_W - NA_WIN_COLS, GRID_W - NA_WIN_COLS)))
    row_slabs, mask_l = [], []
    for i in reps:
        ks = int(np.clip(i * NA_Q_ROWS - NA_WIN_ROWS // 2, 0, rows - key_rows))
        r = i * NA_Q_ROWS + np.arange(NA_Q_ROWS)
        rs = np.clip(r - wr // 2, 0, rows - wr)
        kr = ks + np.arange(key_rows)
        row_ok = (kr[None, :] >= rs[:, None]) & (kr[None, :] < rs[:, None] + wr)
        for rq in range(NA_Q_ROWS):
            first = ks - int(r[rq]) + NA_WIN_ROWS - 1 + key_rows
            assert 0 <= first and first + key_rows <= rp.shape[1]
            row_slabs.append(rp[:, first:first + key_rows, :])
        mask_l.append(np.broadcast_to(row_ok[:, None, :, None] & col_ok[None, :, None, :], shape).reshape(nq, nk))
    slab = jnp.stack(row_slabs, axis=1).reshape(heads, len(reps), NA_Q_ROWS, key_rows, 2 * GRID_W - 1)
    toep = jnp.stack([slab[..., GRID_W - 1 - c:2 * GRID_W - 1 - c] for c in range(GRID_W)], axis=3)
    bias = toep.reshape(heads, len(reps), nq, nk)
    return jnp.where(jnp.asarray(np.stack(mask_l))[None], bias, NEG)


def _nat_attention(qkv, st, cache_k, cache_v, j, rpb, dh):
    heads = rpb.shape[0]
    n = st.seq
    p = cache_k.shape[3]
    rows = n // GRID_W
    assert rows % NA_Q_ROWS == 0 and rows >= NA_K_ROWS and dh % LANE == 0
    n_blocks = rows // NA_Q_ROWS
    key_rows = min(NA_K_ROWS, rows)
    nq, nk = NA_Q_ROWS * GRID_W, key_rows * GRID_W
    bias = _nat_bias(rpb, rows)
    btype = lambda i: jnp.where(i == 0, 0, jnp.where(i == n_blocks - 1, 2, 1))
    hb = _largest_divisor(heads, HEADS_PER_STEP)
    hg = heads // hb
    kern = functools.partial(_nat_kernel, key_rows=key_rows, rows=rows, heads=hb, dh=dh)
    return pl.pallas_call(
        kern,
        grid=(st.nb, hg, n_blocks),
        in_specs=[pl.BlockSpec((nq, hb * dh), lambda b, h, i: (b * n_blocks + i, h)),
                  pl.BlockSpec((n, hb * dh), lambda b, h, i: (b, hg + h)),
                  pl.BlockSpec((n, hb * dh), lambda b, h, i: (b, 2 * hg + h)),
                  pl.BlockSpec((None, None, hb, p, dh), lambda b, h, i: (b, j, h, 0, 0)),
                  pl.BlockSpec((None, None, hb, p, dh), lambda b, h, i: (b, j, h, 0, 0)),
                  pl.BlockSpec((hb, None, nq, nk), lambda b, h, i: (h, btype(i), 0, 0))],
        out_specs=pl.BlockSpec((nq, hb * dh), lambda b, h, i: (b * n_blocks + i, h)),
        out_shape=jax.ShapeDtypeStruct((st.rows, heads * dh), BF16),
        compiler_params=_cparams(3, 56),
        name="nat_attention",
    )(qkv, qkv, qkv, cache_k, cache_v, bias)


def _joint_dense_kernel(q_ref, k_ref, v_ref, kc_ref, vc_ref, o_ref, *, chunk):
    q = q_ref[...]
    n = k_ref.shape[0]
    pieces = [(k_ref, v_ref, c0, min(chunk, n - c0)) for c0 in range(0, n, chunk)]
    pieces.append((kc_ref, vc_ref, 0, kc_ref.shape[0]))
    m = den = acc = None
    for kr, vr, c0, size in pieces:
        s = _qk(q, kr[c0:c0 + size, :])
        mc = jnp.max(s, axis=-1, keepdims=True)
        m_new = mc if m is None else jnp.maximum(m, mc)
        e = jnp.exp2(s - m_new)
        dc = jnp.sum(e, axis=-1, keepdims=True)
        pv = jnp.dot(e.astype(BF16), vr[c0:c0 + size, :], preferred_element_type=F32)
        if m is None:
            den, acc = dc, pv
        else:
            alpha = jnp.exp2(m - m_new)
            den, acc = alpha * den + dc, alpha * acc + pv
        m = m_new
    o_ref[...] = (acc * (1.0 / den)).astype(o_ref.dtype)


def _joint_dense_attention(qm, km, vm, kcm, vcm, st, p, *, n_heads, dq, dv):
    n = st.seq
    bq = _largest_divisor(n, (1024, 512, 256, 128, 64, 32, 16))
    nqb = n // bq
    return pl.pallas_call(
        functools.partial(_joint_dense_kernel, chunk=MLA_KEY_CHUNK),
        grid=(st.nb, n_heads, nqb),
        in_specs=[pl.BlockSpec((bq, dq), lambda b, h, i: (b * nqb + i, h)),
                  pl.BlockSpec((n, dq), lambda b, h, i: (b, h)),
                  pl.BlockSpec((n, dv), lambda b, h, i: (b, h)),
                  pl.BlockSpec((p, dq), lambda b, h, i: (b, h)),
                  pl.BlockSpec((p, dv), lambda b, h, i: (b, h))],
        out_specs=pl.BlockSpec((bq, dv), lambda b, h, i: (b * nqb + i, h)),
        out_shape=jax.ShapeDtypeStruct((st.rows, n_heads * dv), BF16),
        compiler_params=_cparams(3, 48),
        name="mla_attention",
    )(qm, km, vm, kcm, vcm)


def _both_halves(x, s):
    x = x.astype(F32)
    low = lax.broadcasted_iota(jnp.int32, (1, LANE), 1) < LANE // 2
    keep = low if s == 0 else jnp.logical_not(low)
    return jnp.where(keep, x, pltpu.roll(x, LANE // 2, 1)).astype(BF16)


def _swa_step(sinks_ref, pair, q_ref, k, v, kc, vc, o_ref, *, dh, groups, local_bias):
    kv_per_step = LANE // dh
    assert kv_per_step == 2 and groups % 2 == 0
    rows = q_ref.shape[0]
    low = lax.broadcasted_iota(jnp.int32, (1, LANE), 1) < dh
    row_group = lax.broadcasted_iota(jnp.int32, (groups * rows, 1), 0) // rows
    for s in range(kv_per_step):
        kd, vd = _both_halves(k, s), _both_halves(v, s)
        q_parts = []
        for g in range(groups):
            c0 = ((s * groups + g) * dh // LANE) * LANE
            tile = q_ref[:, c0:c0 + LANE].astype(BF16)
            q_parts.append(jnp.where(low if g % 2 == 0 else jnp.logical_not(low), tile, jnp.zeros_like(tile)))
        q = jnp.concatenate(q_parts, axis=0)
        sink = jnp.zeros((groups * rows, 1), F32)
        for g in range(groups):
            sink = jnp.where(row_group == g, sinks_ref[(pair * kv_per_step + s) * groups + g], sink)
        s_loc = _qk(q, kd)
        if local_bias is not None:
            s_loc = s_loc + jnp.concatenate([local_bias] * groups, axis=0)
        if kc is not None:
            kcd = jnp.concatenate([kc[s], kc[s]], axis=-1).astype(BF16)
            vcd = jnp.concatenate([vc[s], vc[s]], axis=-1).astype(BF16)
            out = _attend([s_loc, _qk(q, kcd)], [vd, vcd], sink)
        else:
            out = _attend([s_loc], [vd], sink)
        for g in range(0, groups, 2):
            c0 = (s * groups + g) * dh
            o_ref[:, c0:c0 + LANE] = jnp.where(low, out[g * rows:(g + 1) * rows],
                                               out[(g + 1) * rows:(g + 2) * rows]).astype(o_ref.dtype)


def _swa_ctx_kernel(sinks_ref, q_ref, k_ref, v_ref, o_ref, ko_ref, vo_ref, *, dh, groups):
    pair = pl.program_id(1)
    k, v = k_ref[...], v_ref[...]
    for s in range(LANE // dh):
        ko_ref[s] = k[:, s * dh:(s + 1) * dh].astype(F32)
        vo_ref[s] = v[:, s * dh:(s + 1) * dh].astype(F32)
    _swa_step(sinks_ref, pair, q_ref, k, v, None, None, o_ref, dh=dh, groups=groups, local_bias=None)


def _swa_lat_kernel(sinks_ref, q_ref, k_ref, v_ref, kc_ref, vc_ref, o_ref, *, dh, groups, n):
    pair = pl.program_id(1)
    blk = pl.program_id(2)
    n_keys = min(3 * SWA_BLOCK, n)
    start = pl.multiple_of(jnp.clip((blk - 1) * SWA_BLOCK, 0, n - n_keys), SWA_BLOCK)
    k = k_ref[pl.ds(start, n_keys), :]
    v = v_ref[pl.ds(start, n_keys), :]
    qpos = blk * SWA_BLOCK + lax.broadcasted_iota(jnp.int32, (SWA_BLOCK, 1), 0)
    kpos = start + lax.broadcasted_iota(jnp.int32, (1, n_keys), 1)
    bias = jnp.where(jnp.abs(qpos - kpos) <= SWA_WINDOW, 0.0, NEG)
    _swa_step(sinks_ref, pair, q_ref, k, v, kc_ref, vc_ref, o_ref, dh=dh, groups=groups, local_bias=bias)


def _swa_attention(qkv, st, sinks, *, heads, kvh, dh, cache=None):
    groups = heads // kvh
    kv_per_step = LANE // dh
    assert LANE % dh == 0 and kvh % kv_per_step == 0 and groups % kv_per_step == 0
    pairs = kvh // kv_per_step
    qw = kv_per_step * groups * dh
    k_blk = heads * dh // LANE
    v_blk = (heads + kvh) * dh // LANE
    n = st.seq
    common = dict(dh=dh, groups=groups)
    smem = pl.BlockSpec(memory_space=pltpu.SMEM)
    if cache is None:
        out = pl.pallas_call(
            functools.partial(_swa_ctx_kernel, **common),
            grid=(st.nb, pairs),
            in_specs=[smem,
                      pl.BlockSpec((n, qw), lambda b, c: (b, c)),
                      pl.BlockSpec((n, LANE), lambda b, c: (b, k_blk + c)),
                      pl.BlockSpec((n, LANE), lambda b, c: (b, v_blk + c))],
            out_specs=[pl.BlockSpec((n, qw), lambda b, c: (b, c)),
                       pl.BlockSpec((None, kv_per_step, n, dh), lambda b, c: (b, c, 0, 0)),
                       pl.BlockSpec((None, kv_per_step, n, dh), lambda b, c: (b, c, 0, 0))],
            out_shape=[jax.ShapeDtypeStruct((st.rows, heads * dh), BF16),
                       jax.ShapeDtypeStruct((st.nb, kvh, n, dh), F32),
                       jax.ShapeDtypeStruct((st.nb, kvh, n, dh), F32)],
            compiler_params=_cparams(2, 32),
            name="swa_ctx_attention",
        )(sinks, qkv, qkv, qkv)
        return out
    cache_k, cache_v, j = cache
    p = cache_k.shape[3]
    nblk = n // SWA_BLOCK
    assert n % SWA_BLOCK == 0
    return pl.pallas_call(
        functools.partial(_swa_lat_kernel, n=n, **common),
        grid=(st.nb, pairs, nblk),
        in_specs=[smem,
                  pl.BlockSpec((SWA_BLOCK, qw), lambda b, c, i: (b * nblk + i, c)),
                  pl.BlockSpec((n, LANE), lambda b, c, i: (b, k_blk + c)),
                  pl.BlockSpec((n, LANE), lambda b, c, i: (b, v_blk + c)),
                  pl.BlockSpec((None, None, kv_per_step, p, dh), lambda b, c, i: (b, j, c, 0, 0)),
                  pl.BlockSpec((None, None, kv_per_step, p, dh), lambda b, c, i: (b, j, c, 0, 0))],
        out_specs=pl.BlockSpec((SWA_BLOCK, qw), lambda b, c, i: (b * nblk + i, c)),
        out_shape=jax.ShapeDtypeStruct((st.rows, heads * dh), BF16),
        compiler_params=_cparams(3, 32),
        name="swa_attention",
    )(sinks, qkv, qkv, qkv, cache_k, cache_v)


def _mla_kv_kernel(*refs, norm, emit_xn, rope, row_chunk, norm_div):
    it = iter(refs)
    x_ref = next(it)
    g_ref = next(it) if norm else None
    w_ref, kr_ref, g1_ref, g2_ref = next(it), next(it), next(it), next(it)
    tabs = (next(it), next(it), next(it)) if rope else None
    k_ref, v_ref = next(it), next(it)
    xn_ref = next(it) if emit_xn else None
    xs_ref = next(it)

    @pl.when(pl.program_id(1) == 0)
    def _():
        _fill_lhs(x_ref, xs_ref, xn_ref, "norm" if norm else None, g_ref, None, None, row_chunk)

    acc = jnp.dot(xs_ref[...], w_ref[...], preferred_element_type=F32)
    kr = kr_ref[...]
    kr_ssq = jnp.sum(kr * kr, axis=-1, keepdims=True)
    shared = kr * g2_ref[...]
    if rope:
        shared = _rope_apply(shared, *(t[...] for t in tabs))
    for h in range(acc.shape[1] // (2 * LANE)):
        nope = acc[:, 2 * h * LANE:(2 * h + 1) * LANE]
        inv = lax.rsqrt((jnp.sum(nope * nope, axis=-1, keepdims=True) + kr_ssq) / norm_div + EPS)
        k_ref[:, 2 * h * LANE:(2 * h + 1) * LANE] = ((nope * inv) * g1_ref[...]).astype(k_ref.dtype)
        k_ref[:, (2 * h + 1) * LANE:(2 * h + 2) * LANE] = (shared * inv).astype(k_ref.dtype)
        v_ref[:, h * LANE:(h + 1) * LANE] = acc[:, (2 * h + 1) * LANE:(2 * h + 2) * LANE].astype(v_ref.dtype)


def _mla_kv(x, x_block, w_ukv, kr, kr_block, g_kva, g1, g2, tabs, st, *, n_heads, norm_div, emit_xn, name):
    rows = x.shape[0]
    k, n = w_ukv.shape
    head_n = n // n_heads
    assert head_n == 2 * LANE, "nope and value widths must both be one lane tile"
    kx, kidx = x_block
    krw, kridx = kr_block
    assert kx == k and krw == LANE
    bm = st.bm
    norm = g_kva is not None
    rope = tabs is not None
    in_specs = [pl.BlockSpec((bm, k), lambda i, h: (i, kidx))]
    args = [x]
    if norm:
        in_specs.append(pl.BlockSpec((1, k), lambda i, h: (0, 0)))
        args.append(g_kva)
    hb = _largest_divisor(n_heads, HEADS_PER_STEP)
    in_specs += [pl.BlockSpec((k, hb * head_n), lambda i, h: (0, h)),
                 pl.BlockSpec((bm, LANE), lambda i, h: (i, kridx)),
                 pl.BlockSpec((1, LANE), lambda i, h: (0, 0)),
                 pl.BlockSpec((1, LANE), lambda i, h: (0, 0))]
    args += [w_ukv, kr, g1, g2]
    if rope:
        tiles_per_seq = st.seq // bm
        in_specs += [pl.BlockSpec((bm, LANE), lambda i, h: (i % tiles_per_seq, 0))] * 3
        args += list(tabs)
    out_shape = [jax.ShapeDtypeStruct((rows, n_heads * 2 * LANE), BF16),
                 jax.ShapeDtypeStruct((rows, n_heads * LANE), BF16)]
    out_specs = [pl.BlockSpec((bm, hb * 2 * LANE), lambda i, h: (i, h)),
                 pl.BlockSpec((bm, hb * LANE), lambda i, h: (i, h))]
    if emit_xn:
        out_shape.append(jax.ShapeDtypeStruct((rows, k), F32))
        out_specs.append(pl.BlockSpec((bm, k), lambda i, h: (i, 0)))
    kern = functools.partial(_mla_kv_kernel, norm=norm, emit_xn=emit_xn, rope=rope, row_chunk=min(bm, 128),
                             norm_div=norm_div)
    return pl.pallas_call(
        kern,
        grid=(rows // bm, n_heads // hb),
        in_specs=in_specs,
        out_specs=out_specs,
        out_shape=out_shape,
        scratch_shapes=[pltpu.VMEM((bm, k), BF16)],
        compiler_params=_cparams(2, 40),
        name=name,
    )(*args)


def _band_plan(width, block):
    n_tiles = width // LANE
    lo = [((t * LANE) // block) * block for t in range(n_tiles)]
    hi = [(((t + 1) * LANE - 1) // block + 1) * block for t in range(n_tiles)]
    start = [(l // LANE) * LANE for l in lo]
    kb = max(-(-(h - s) // LANE) * LANE for h, s in zip(hi, start))
    kb = min(kb, width)
    start = [min(s, width - kb) for s in start]
    return start, kb


def _band_weights(w, width, block, start, kb):
    n_tiles = width // LANE
    wb = w.astype(BF16)
    tiles = []
    for t in range(n_tiles):
        pieces = []
        col = t * LANE
        while col < (t + 1) * LANE:
            blk = col // block
            col_end = min((blk + 1) * block, (t + 1) * LANE)
            sub = wb[blk, :, col - blk * block:col_end - blk * block]
            top = blk * block - start[t]
            pieces.append(jnp.pad(sub, ((top, kb - top - block), (0, 0))))
            col = col_end
        tiles.append(jnp.concatenate(pieces, axis=1))
    return jnp.stack(tiles)


def _gelu_tanh(x):
    cdf = 0.5 * (1.0 + jnp.tanh(np.float32(np.sqrt(2.0 / np.pi)) * (x + 0.044715 * (x * x * x))))
    return x * cdf


def _lru_pass_kernel(*refs, reverse, starts, kb, bt, nb, taps):
    left = taps // 2
    right = taps - 1 - left
    it = iter(refs)
    if reverse:
        conv_ref = next(it)
    else:
        xp_ref, x_ref = next(it), next(it)
        xn_ref = next(it) if right > 0 else None
        cw_ref, cb_ref = next(it), next(it)
    wa_ref, wi_ref, ba_ref, bi_ref, lam_ref, h0_ref = (next(it) for _ in range(6))
    hsf_ref, gate_ref = (next(it), next(it)) if reverse else (None, None)
    out_ref, ht_ref = next(it), next(it)
    conv_out_ref = None if reverse else next(it)
    xc_s, xb_s, a_s, bx_s, carry = next(it), next(it), next(it), next(it), next(it)

    step = pl.program_id(0)
    n_steps = pl.num_programs(0)
    n_tiles = len(starts)
    rows = nb * bt

    @pl.when(step == 0)
    def _():
        carry[...] = h0_ref[...]

    for t in range(n_tiles):
        lanes = slice(t * LANE, (t + 1) * LANE)
        if reverse:
            acc = conv_ref[:, :, lanes].reshape(rows, LANE)
        else:
            parts = [jnp.where(step > 0, xp_ref[:, :, lanes], 0.0), x_ref[:, :, lanes]]
            if right > 0:
                parts.append(jnp.where(step < n_steps - 1, xn_ref[:, :, lanes], 0.0))
            full = jnp.concatenate(parts, axis=0)
            acc = jnp.broadcast_to(cb_ref[:, lanes], (bt, nb, LANE))
            for k in range(taps):
                acc = acc + full[k:k + bt] * cw_ref[k:k + 1, lanes]
            conv_out_ref[:, :, lanes] = acc
            acc = acc.reshape(rows, LANE)
        xc_s[:, lanes] = acc
        xb_s[:, lanes] = acc.astype(BF16)

    neg_lam = -lam_ref[...]
    softplus = jnp.maximum(neg_lam, 0.0) + jnp.log1p(jnp.exp(-jnp.abs(neg_lam)))
    half_rate = (-0.5 * LRU_C) * softplus
    half_ba, half_bi = 0.5 * ba_ref[...], 0.5 * bi_ref[...]
    for t in range(n_tiles):
        lanes = slice(t * LANE, (t + 1) * LANE)
        xw = xb_s[:, starts[t]:starts[t] + kb]
        tanh_a = jnp.tanh(jnp.dot(xw, wa_ref[t], preferred_element_type=F32) + half_ba[:, lanes])
        tanh_i = jnp.tanh(jnp.dot(xw, wi_ref[t], preferred_element_type=F32) + half_bi[:, lanes])
        log_a = half_rate[:, lanes] * tanh_a + half_rate[:, lanes]
        a = jnp.exp(log_a)
        half_x = 0.5 * xc_s[:, lanes]
        a_s[:, lanes] = a
        bx_s[:, lanes] = jnp.sqrt(-jnp.tanh(log_a) * (a * a + 1.0)) * (half_x * tanh_i + half_x)

    h = carry[...]
    for s in range(bt):
        ts = (bt - 1 - s) if reverse else s
        slab = slice(ts * nb, (ts + 1) * nb)
        h = a_s[slab, :] * h + bx_s[slab, :]
        a_s[slab, :] = h
    carry[...] = h
    ht_ref[...] = h
    hs = a_s[...].reshape(bt, nb, a_s.shape[1])
    if reverse:
        out_ref[...] = (_gelu_tanh(gate_ref[...]) * (hsf_ref[...] + hs)).astype(out_ref.dtype)
    else:
        out_ref[...] = hs


def _lru_pass(u, st, conv_w, conv_b, wa, wi, b_a, b_i, lam, h0, starts, kb, *, reverse, fwd=None):
    c = conv_w.shape[1]
    taps = conv_w.shape[0]
    left, right = taps // 2, taps - 1 - taps // 2
    nb, seq = st.nb, st.seq
    bt = min(max(LRU_ROWS_PER_STEP // nb, SUBLANE), seq)
    assert seq % bt == 0 and c % LANE == 0 and left > 0 and bt % left == 0 and (right == 0 or bt % right == 0)
    nt = seq // bt
    n_tiles = c // LANE
    u3 = u.reshape(seq, nb, 2 * c)
    tmap = (lambda s: nt - 1 - s) if reverse else (lambda s: s)
    full = lambda *shape: pl.BlockSpec(shape, lambda s: (0,) * len(shape))
    tile_spec = lambda col: pl.BlockSpec((bt, nb, c), lambda s: (tmap(s), 0, col))
    if reverse:
        in_specs, args = [tile_spec(0)], [fwd[1]]
    else:
        in_specs = [pl.BlockSpec((left, nb, c), lambda s: (jnp.maximum(s * (bt // left) - 1, 0), 0, 0)), tile_spec(0)]
        args = [u3, u3]
        if right > 0:
            in_specs.append(pl.BlockSpec((right, nb, c),
                                         lambda s: (jnp.minimum((s + 1) * (bt // right), seq // right - 1), 0, 0)))
            args.append(u3)
        in_specs += [full(taps, c), full(1, c)]
        args += [conv_w, conv_b.reshape(1, c)]
    in_specs += [full(n_tiles, kb, LANE), full(n_tiles, kb, LANE), full(1, c), full(1, c), full(1, c), full(nb, c)]
    args += [wa, wi, b_a.reshape(1, c), b_i.reshape(1, c), lam.reshape(1, c), h0]
    out_specs = [tile_spec(0), pl.BlockSpec((nb, c), lambda s: (0, 0))]
    out_shape = [jax.ShapeDtypeStruct((seq, nb, c), F32), jax.ShapeDtypeStruct((nb, c), F32)]
    if reverse:
        in_specs += [tile_spec(0), tile_spec(1)]
        args += [fwd[0], u3]
    else:
        out_specs.append(tile_spec(0))
        out_shape.append(jax.ShapeDtypeStruct((seq, nb, c), F32))
    kern = functools.partial(_lru_pass_kernel, reverse=reverse, starts=tuple(starts), kb=kb, bt=bt, nb=nb, taps=taps)
    blk = nb * bt * c * 4 / MIB
    vmem = (2 + 2 + 3 + (4 if reverse else 2) + 4) * blk + 4 * n_tiles * kb * LANE * 2 / MIB + 8
    return pl.pallas_call(
        kern,
        grid=(nt,),
        in_specs=in_specs,
        out_specs=out_specs,
        out_shape=out_shape,
        scratch_shapes=[pltpu.VMEM((nb * bt, c), F32), pltpu.VMEM((nb * bt, c), BF16),
                        pltpu.VMEM((nb * bt, c), F32), pltpu.VMEM((nb * bt, c), F32), pltpu.VMEM((nb, c), F32)],
        compiler_params=_cparams(1, vmem),
        name="lru_bwd" if reverse else "lru_fwd",
    )(*args)


def _mixer_nat(xs, streams, mods, cache_k, cache_v, j, w_qkv, g_mix, g_q, g_k, rpb, w_o):
    heads, dh = rpb.shape[0], g_q.shape[0]
    w_qkv, w_o = w_qkv.astype(BF16), w_o.astype(BF16)
    gains = jnp.concatenate([jnp.tile(g_q * (dh ** -0.5 * LOG2E), heads), jnp.tile(g_k, heads),
                             jnp.ones((heads * dh,), F32)])[None]
    spec = dict(head_w=dh, norm_div=dh, norm_cols=2 * heads * dh, gains=gains)
    new_x, extra = [], None
    for x, st in zip(xs, streams):
        latent = not st.shared
        qkv = _proj(x, w_qkv, st, norm_g=g_mix, mod=(mods[0], mods[1]), heads=spec,
                    out_dtype=BF16 if latent else F32, name="nat_qkv")
        if latent:
            o = _nat_attention(qkv, st, cache_k, cache_v, j, rpb, dh)
        else:
            o, kc, vc = _ctx_attention(qkv, qkv, qkv, st, n_heads=heads, dq=dh, dv=dh, q_col=0, k_col=heads * dh,
                                       v_col=2 * heads * dh, emit_kv=True)
            extra = (kc, vc)
        new_x.append(_proj(o, w_o, st, res=x, gate=mods[2], name="nat_out"))
    return new_x, extra


def _mixer_lru(xs, streams, mods, state, w_in, g_mix, conv_w, conv_b, w_a, b_a, w_i, b_i, lam, w_out):
    c = conv_w.shape[1]
    block = w_a.shape[-1]
    w_in, w_out = w_in.astype(BF16), w_out.astype(BF16)
    starts, kb = _band_plan(c, block)
    wa = [_band_weights(0.5 * w_a[d], c, block, starts, kb) for d in range(2)]
    wi = [_band_weights(0.5 * w_i[d], c, block, starts, kb) for d in range(2)]
    new_x, st_out = [], None
    for x, st in zip(xs, streams):
        latent = not st.shared
        h0 = state.astype(F32) if latent else jnp.zeros((st.nb, 2, c), F32)
        u = _proj(x, w_in, st, norm_g=g_mix, mod=(mods[0], mods[1]), out_time_major=True, name="lru_in")
        hs_f, t_f, conv = _lru_pass(u, st, conv_w, conv_b, wa[0], wi[0], b_a[0], b_i[0], lam[0], h0[:, 0], starts,
                                    kb, reverse=False)
        y, t_b = _lru_pass(u, st, conv_w, conv_b, wa[1], wi[1], b_a[1], b_i[1], lam[1], h0[:, 1], starts, kb,
                           reverse=True, fwd=(hs_f, conv))
        if not latent:
            st_out = jnp.stack([t_f, t_b], axis=1)
        new_x.append(_proj(y.reshape(st.seq, st.nb * c), w_out, st, res=x, gate=mods[2], x_time_major=True,
                           name="lru_out"))
    return new_x, st_out


def _mixer_mla(xs, streams, mods, cache_ckv, cache_kr, w_down, g_mix, g_qa, g_kva, w_uq, w_ukv, g_q, g_k, w_o):
    d_model = w_down.shape[0]
    q_rank, kv_rank = g_qa.shape[0], g_kva.shape[0]
    qk_dim = g_q.shape[0]
    heads = w_uq.shape[1] // qk_dim
    rope = w_down.shape[1] - q_rank - kv_rank
    nope = qk_dim - rope
    assert nope == LANE and rope <= LANE and kv_rank % LANE == 0 and q_rank % LANE == 0
    head_w = 2 * LANE
    q_pad = -q_rank % kv_rank
    kv_col = q_rank + q_pad
    zeros = lambda width: jnp.zeros((d_model, width), F32)
    w_qd, w_kvd, w_rd = w_down[:, :q_rank], w_down[:, q_rank:q_rank + kv_rank], w_down[:, q_rank + kv_rank:]
    if q_pad >= LANE:
        kr_col = q_rank
        w_dn = jnp.concatenate([w_qd, w_rd, zeros(q_pad - rope), w_kvd], axis=1).astype(BF16)
    else:
        kr_col = kv_col + kv_rank
        w_dn = jnp.concatenate([w_qd, zeros(q_pad), w_kvd, w_rd, zeros(LANE - rope)], axis=1).astype(BF16)
    kr_blk = kr_col // LANE
    w_q = jnp.pad(w_uq.reshape(q_rank, heads, qk_dim), ((0, 0), (0, 0), (0, head_w - qk_dim)))
    w_q = w_q.reshape(q_rank, heads * head_w).astype(BF16)
    w_ukv, w_o = w_ukv.astype(BF16), w_o.astype(BF16)
    gq = jnp.tile(jnp.pad(g_q * (qk_dim ** -0.5 * LOG2E), (0, head_w - qk_dim)), heads)[None]
    g1, g2 = g_k[None, :nope], jnp.pad(g_k[nope:], (0, LANE - rope))[None]
    p = cache_ckv.shape[1]
    new_x, extra = [], None
    for x, st in zip(xs, streams):
        latent = not st.shared
        d = _proj(x, w_dn, st, norm_g=g_mix, mod=(mods[0], mods[1]), name="mla_down")
        q_tabs = _rope_tables(st.seq, rope, nope, head_w) if latent else None
        k_tabs = _rope_tables(st.seq, rope, 0, LANE) if latent else None
        q = _proj(d, w_q, st, x_block=(q_rank, 0), norm_g=g_qa[None],
                  heads=dict(head_w=head_w, norm_div=qk_dim, norm_cols=heads * head_w, gains=gq, tabs=q_tabs,
                             rope_tiles=(False, True)),
                  out_dtype=BF16, name="mla_uq")
        kv = _mla_kv(d, (kv_rank, kv_col // kv_rank), w_ukv, d, (LANE, kr_blk), g_kva[None], g1, g2, k_tabs, st,
                     n_heads=heads, norm_div=qk_dim, emit_xn=not latent, name="mla_ukv")
        if latent:
            k, v = kv
            cst = _Stream(st.nb, p, 0, True)
            krc = jnp.pad(cache_kr.reshape(st.nb * p, rope), ((0, 0), (0, LANE - rope)))
            kc, vc = _mla_kv(cache_ckv.reshape(st.nb * p, kv_rank), (kv_rank, 0), w_ukv, krc, (LANE, 0), None,
                             g1, g2, None, cst, n_heads=heads, norm_div=qk_dim, emit_xn=False,
                             name="mla_ukv_cache")
            o = _joint_dense_attention(q, k, v, kc, vc, st, p, n_heads=heads, dq=head_w, dv=LANE)
        else:
            k, v, ckv = kv
            o = _ctx_attention(q, k, v, st, n_heads=heads, dq=head_w, dv=LANE, q_col=0, k_col=0, v_col=0)
            kr_out = d[:, kr_col:kr_col + rope]
            extra = (ckv.reshape(st.nb, st.seq, kv_rank), kr_out.reshape(st.nb, st.seq, rope))
        new_x.append(_proj(o, w_o, st, res=x, gate=mods[2], name="mla_out"))
    return new_x, extra


def _mixer_swa(xs, streams, mods, cache_k, cache_v, j, w_qkv, g_mix, g_q, g_k, sinks, w_o):
    dh = g_q.shape[0]
    heads = sinks.shape[0]
    kvh = (w_qkv.shape[1] // dh - heads) // 2
    w_qkv, w_o = w_qkv.astype(BF16), w_o.astype(BF16)
    gains = jnp.concatenate([jnp.tile(g_q * (dh ** -0.5 * LOG2E), heads), jnp.tile(g_k, kvh),
                             jnp.ones((kvh * dh,), F32)])[None]
    sinks = sinks.astype(F32) * LOG2E
    new_x, extra = [], None
    for x, st in zip(xs, streams):
        latent = not st.shared
        tabs = _rope_tables(st.seq, dh, 0, dh) if latent else None
        if tabs is not None:
            tabs = tuple(jnp.tile(t, (1, LANE // dh)) for t in tabs)
        spec = dict(head_w=dh, norm_div=dh, norm_cols=(heads + kvh) * dh, gains=gains, tabs=tabs,
                    rope_tiles=(True,))
        qkv = _proj(x, w_qkv, st, norm_g=g_mix, mod=(mods[0], mods[1]), heads=spec,
                    out_dtype=BF16 if latent else F32, name="swa_qkv")
        if latent:
            o = _swa_attention(qkv, st, sinks, heads=heads, kvh=kvh, dh=dh, cache=(cache_k, cache_v, j))
        else:
            o, kc, vc = _swa_attention(qkv, st, sinks, heads=heads, kvh=kvh, dh=dh)
            extra = (kc, vc)
        new_x.append(_proj(o, w_o, st, res=x, gate=mods[2], name="swa_out"))
    return new_x, extra


def kernel(x_prompt, x_sample, cache_nat_k, cache_nat_v, state_lru, cache_mla_ckv, cache_mla_krope, cache_swa_k, cache_swa_v, c, c_ctx, norm_mix, norm_ffn, w_mod, b_mod, ffn_w_in, ffn_conv_w, ffn_conv_b, ffn_w_out, nat_w_qkv, nat_q_norm, nat_k_norm, nat_rpb, nat_w_o, lru_w_in, lru_conv_w, lru_conv_b, lru_w_a, lru_b_a, lru_w_i, lru_b_i, lru_lambda, lru_w_out, mla_w_down, mla_q_a_norm, mla_kv_a_norm, mla_w_uq, mla_w_ukv, mla_q_norm, mla_k_norm, mla_w_o, swa_w_qkv, swa_q_norm, swa_k_norm, swa_sinks, swa_w_o):
    bc, sc, d = x_prompt.shape
    bl, n, _ = x_sample.shape
    depth = w_mod.shape[0]
    streams = (_Stream(bc, sc, 0, True), _Stream(bl, n, 1, False))
    xs = [x_prompt.reshape(bc * sc, d), x_sample.reshape(bl * n, d)]

    n_cond = 1 + bl
    cond_rows = -(-n_cond // SUBLANE) * SUBLANE
    cond = jnp.zeros((cond_rows, d), F32).at[0].set(c_ctx).at[1:n_cond].set(c)
    mods = _modulation(cond, w_mod, b_mod)[:, :n_cond]

    nat_k_l, nat_v_l, lru_l, ckv_l, krope_l, swa_k_l, swa_v_l = [], [], [], [], [], [], []
    for l in range(depth):
        kind, j = l % 4, l // 4
        m6 = [mods[l, :, None, t * d:(t + 1) * d] for t in range(6)]
        g_mix = norm_mix[l].reshape(1, d)
        if kind == 0:
            xs, (kc, vc) = _mixer_nat(xs, streams, m6, cache_nat_k, cache_nat_v, j, nat_w_qkv[j], g_mix,
                                      nat_q_norm[j], nat_k_norm[j], nat_rpb[j], nat_w_o[j])
            nat_k_l.append(kc)
            nat_v_l.append(vc)
        elif kind == 1:
            xs, st = _mixer_lru(xs, streams, m6, state_lru[:, j], lru_w_in[j], g_mix, lru_conv_w[j], lru_conv_b[j],
                                lru_w_a[j], lru_b_a[j], lru_w_i[j], lru_b_i[j], lru_lambda[j], lru_w_out[j])
            lru_l.append(st)
        elif kind == 2:
            xs, (ckv, kr) = _mixer_mla(xs, streams, m6, cache_mla_ckv[:, j], cache_mla_krope[:, j], mla_w_down[j],
                                       g_mix, mla_q_a_norm[j], mla_kv_a_norm[j], mla_w_uq[j], mla_w_ukv[j],
                                       mla_q_norm[j], mla_k_norm[j], mla_w_o[j])
            ckv_l.append(ckv)
            krope_l.append(kr)
        else:
            xs, (kc, vc) = _mixer_swa(xs, streams, m6, cache_swa_k, cache_swa_v, j, swa_w_qkv[j], g_mix,
                                      swa_q_norm[j], swa_k_norm[j], swa_sinks[j], swa_w_o[j])
            swa_k_l.append(kc)
            swa_v_l.append(vc)
        w_in, w_out = ffn_w_in[l].astype(BF16), ffn_w_out[l].astype(BF16)
        xs = [_conv_ffn(x, st, norm_ffn[l].reshape(1, d), m6[3], m6[4], m6[5], w_in, ffn_conv_w[l],
                        ffn_conv_b[l], w_out) for x, st in zip(xs, streams)]

    return (xs[0].reshape(bc, sc, d), xs[1].reshape(bl, n, d), jnp.stack(nat_k_l, axis=1),
            jnp.stack(nat_v_l, axis=1), jnp.stack(lru_l, axis=1), jnp.stack(ckv_l, axis=1),
            jnp.stack(krope_l, axis=1), jnp.stack(swa_k_l, axis=1), jnp.stack(swa_v_l, axis=1))
```

```python
import functools

import numpy as np
import jax
import jax.numpy as jnp
from jax import lax
from jax.experimental import pallas as pl
from jax.experimental.pallas import tpu as pltpu

F32 = jnp.float32
BF16 = jnp.bfloat16

GRID_W = 64
NA_WIN_ROWS = 8
NA_WIN_COLS = 16
NA_Q_ROWS = 8
NA_K_ROWS = 16
LRU_C = 8.0
SWA_WINDOW = 128
SWA_BLOCK = 128
ROPE_BASE = 10000.0
ROPE_GROUP = 32
EPS = 1e-6
NEG = -1e30
LOG2E = float(np.log2(np.e))
LANE = 128
SUBLANE = 8
HALO = 16
MIB = 1024 * 1024
VMEM_LIMIT_CAP_MIB = 60
ROW_TILES = (1024, 512, 256, 128, 64, 32, 16)
MAX_COL_TILE = 1024
MAX_COL_TILE_RESIDUAL = 512
FFN_ROW_TILE, FFN_FF_CHUNK = 1024, 512
HEADS_PER_STEP = (4, 2, 1)
MLA_KEY_CHUNK = 2048
LRU_ROWS_PER_STEP = 256


def _cparams(n_axes, vmem_mib):
    return pltpu.CompilerParams(dimension_semantics=("arbitrary",) * n_axes,
                                vmem_limit_bytes=int(min(vmem_mib, VMEM_LIMIT_CAP_MIB) * MIB))


def _largest_divisor(n, candidates):
    for c in candidates:
        if n % c == 0:
            return c
    return n


class _Stream:
    def __init__(self, nb, seq, mod0, shared_mod):
        self.nb, self.seq, self.rows, self.mod0, self.shared = nb, seq, nb * seq, mod0, shared_mod
        self.bm = _largest_divisor(self.rows if shared_mod else seq, ROW_TILES)

    def mod_index(self, row0):
        return self.mod0 if self.shared else self.mod0 + row0 // self.seq


def _norm_mod(x, g, shift, scale):
    ms = jnp.mean(x * x, axis=-1, keepdims=True)
    return (x * lax.rsqrt(ms + EPS)) * (g * (1.0 + scale)) + shift


def _rms(x, g):
    return (x * lax.rsqrt(jnp.mean(x * x, axis=-1, keepdims=True) + EPS)) * g


def _modulation_kernel(c_ref, w_ref, b_ref, o_ref):
    c = c_ref[...]
    sc = (c * jax.nn.sigmoid(c)).astype(BF16)
    o_ref[...] = jnp.dot(sc, w_ref[...].astype(BF16), preferred_element_type=F32) + b_ref[...]


def _modulation(cond, w_mod, b_mod):
    depth, d, n = w_mod.shape
    rows = cond.shape[0]
    bn = _largest_divisor(n, (512, 256, 128))
    return pl.pallas_call(
        _modulation_kernel,
        grid=(depth, n // bn),
        in_specs=[pl.BlockSpec((rows, d), lambda l, j: (0, 0)),
                  pl.BlockSpec((None, d, bn), lambda l, j: (l, 0, j)),
                  pl.BlockSpec((None, 1, bn), lambda l, j: (l, 0, j))],
        out_specs=pl.BlockSpec((None, rows, bn), lambda l, j: (l, 0, j)),
        out_shape=jax.ShapeDtypeStruct((depth, rows, n), F32),
        compiler_params=_cparams(2, 32),
        name="modulation",
    )(cond, w_mod, b_mod.reshape(depth, 1, n))


def _rope_tables(n_tokens, rot_dim, lead, width):
    t = jnp.arange(n_tokens)
    row = (t // GRID_W).astype(F32)
    col = (t % GRID_W).astype(F32)
    half = rot_dim // 2
    inv = ROPE_BASE ** (-jnp.arange(0, half, 2, dtype=F32) / half)
    ar = row[:, None] * inv
    ac = col[:, None] * inv
    ang = jnp.concatenate([ar, ar, ac, ac], axis=-1)
    cos, sin = jnp.cos(ang), jnp.sin(ang)
    first = (np.arange(rot_dim) % ROPE_GROUP) < ROPE_GROUP // 2
    sin_a = jnp.where(first, -sin, 0.0)
    sin_b = jnp.where(first, 0.0, sin)
    pad = ((0, 0), (lead, width - lead - rot_dim))
    return (jnp.pad(cos, pad, constant_values=1.0), jnp.pad(sin_a, pad), jnp.pad(sin_b, pad))


def _rope_apply(y, cos, sin_a, sin_b):
    shift = ROPE_GROUP // 2
    return y * cos + pltpu.roll(y, LANE - shift, 1) * sin_a + pltpu.roll(y, shift, 1) * sin_b


def _rotate_half_matrix():
    shift = ROPE_GROUP // 2
    src = lax.broadcasted_iota(jnp.int32, (LANE, LANE), 0)
    dst = lax.broadcasted_iota(jnp.int32, (LANE, LANE), 1)
    first = (dst % ROPE_GROUP) < shift
    return jnp.where(first & (src == dst + shift), -1.0,
                     jnp.where(jnp.logical_not(first) & (src == dst - shift), 1.0, 0.0)).astype(BF16)


def _rope_apply_mxu(y, cos, sin_a, sin_b, perm):
    rot = _dot_hi_lo(y, perm)
    return y * cos + rot * (sin_b - sin_a)


def _dot_hi_lo(y, m):
    hi = y.astype(BF16)
    lo = (y - hi.astype(F32)).astype(BF16)
    return jnp.dot(jnp.concatenate([hi, lo], axis=1), jnp.concatenate([m, m], axis=0), preferred_element_type=F32)


def _fill_lhs(x_ref, xs_ref, xn_ref, prologue, g_ref, sh_ref, sc_ref, row_chunk):
    bm = x_ref.shape[0]

    def chunk(r, carry):
        rows = pl.ds(pl.multiple_of(r * row_chunk, row_chunk), row_chunk)
        x = x_ref[rows, :].astype(F32)
        if prologue == "norm_mod":
            x = _norm_mod(x, g_ref[...], sh_ref[...], sc_ref[...])
        elif prologue == "norm":
            x = _rms(x, g_ref[...])
        if xn_ref is not None:
            xn_ref[rows, :] = x
        xs_ref[rows, :] = x.astype(BF16)
        return carry
    n_chunks = bm // row_chunk
    lax.fori_loop(0, n_chunks, chunk, 0, unroll=2 if n_chunks % 2 == 0 else 1)


def _head_norm_store(acc, o_ref, hg_ref, tabs, head_w, norm_div, col0, norm_cols, rope_tiles):
    bn = acc.shape[1]
    period = tabs[0].shape[1] if tabs is not None else LANE
    perm = _rotate_half_matrix() if tabs is not None and head_w >= LANE else None
    if head_w < LANE:
        head_of_row = lax.broadcasted_iota(jnp.int32, (LANE, LANE), 0) // head_w
        head_of_col = lax.broadcasted_iota(jnp.int32, (LANE, LANE), 1) // head_w
        same_head = jnp.where(head_of_row == head_of_col, 1.0, 0.0).astype(BF16)
    for s0 in range(0, bn, max(head_w, LANE)):
        normed = None if norm_cols is None else (col0 + s0 < norm_cols)
        tiles = [acc[:, s0 + k * LANE:s0 + (k + 1) * LANE] for k in range(max(head_w, LANE) // LANE)]
        if head_w >= LANE:
            sq = None
            for y in tiles:
                sq = y * y if sq is None else sq + y * y
            inv = lax.rsqrt(jnp.sum(sq, axis=-1, keepdims=True) / norm_div + EPS)
        else:
            inv = lax.rsqrt(_dot_hi_lo(tiles[0] * tiles[0], same_head) / norm_div + EPS)
        if normed is not None:
            inv = jnp.where(normed, inv, 1.0)
        for k, y in enumerate(tiles):
            c0 = s0 + k * LANE
            y = (y * inv) * hg_ref[:, c0:c0 + LANE]
            t0 = c0 % period
            if tabs is not None and rope_tiles[t0 // LANE]:
                tab = tuple(t[:, t0:t0 + LANE] for t in tabs)
                rotated = _rope_apply(y, *tab) if head_w < LANE else _rope_apply_mxu(y, *tab, perm)
                y = rotated if normed is None else jnp.where(normed, rotated, y)
            o_ref[:, c0:c0 + LANE] = y.astype(o_ref.dtype)


def _proj_kernel(*refs, prologue, emit_xn, epilogue, head_w, norm_div, norm_cols, rope, rope_tiles, row_chunk):
    it = iter(refs)
    x_ref = next(it)
    g_ref = next(it) if prologue is not None else None
    sh_ref, sc_ref = (next(it), next(it)) if prologue == "norm_mod" else (None, None)
    w_ref = next(it)
    if epilogue == "res":
        res_ref, gate_ref = next(it), next(it)
    if epilogue == "heads":
        hg_ref = next(it)
        tabs = (next(it), next(it), next(it)) if rope else None
    o_ref = next(it)
    xn_ref = next(it) if emit_xn else None
    xs_ref = next(it, None)
    j = pl.program_id(1)
    bn = o_ref.shape[1]

    if xs_ref is None:
        xs_ref = x_ref
    else:
        @pl.when(j == 0)
        def _():
            _fill_lhs(x_ref, xs_ref, xn_ref, prologue, g_ref, sh_ref, sc_ref, row_chunk)

    acc = jnp.dot(xs_ref[...], w_ref[...], preferred_element_type=F32)
    if epilogue == "res":
        o_ref[...] = res_ref[...] + gate_ref[...] * acc
    elif epilogue == "heads":
        _head_norm_store(acc, o_ref, hg_ref, tabs, head_w, norm_div, j * bn, norm_cols, rope_tiles)
    else:
        o_ref[...] = acc.astype(o_ref.dtype)


def _proj(x, w, st, *, x_block=None, norm_g=None, mod=None, res=None, gate=None, heads=None, emit_xn=False,
          out_dtype=F32, bn=None, x_time_major=False, out_time_major=False, name="proj"):
    k, n = w.shape
    time_major = x_time_major or out_time_major
    bm = min(st.bm, st.seq) if time_major else st.bm
    tiles_per_seq = st.seq // bm if st.seq % bm == 0 else None
    if time_major:
        assert tiles_per_seq is not None and x_block is None
    if x_time_major:
        assert x.shape == (st.seq, st.nb * k)
        kidx = 0
    else:
        kx, kidx = x_block if x_block is not None else (x.shape[1], 0)
        assert kx == k and x.shape[0] == st.rows
    rows = st.rows
    prologue = None if norm_g is None else ("norm_mod" if mod is not None else "norm")
    epilogue = "res" if res is not None else ("heads" if heads is not None else None)
    rope = heads is not None and heads.get("tabs") is not None
    if bn is None:
        unit = LANE
        if epilogue == "heads":
            unit = max(heads["head_w"], LANE, heads["tabs"][0].shape[1] if rope else LANE)
        cap = MAX_COL_TILE_RESIDUAL if epilogue == "res" and x.dtype != BF16 else MAX_COL_TILE
        bn = next((c for c in range(cap, unit - 1, -unit) if n % c == 0), n)
    mod_idx = lambda i: st.mod_index(i * bm)

    if x_time_major:
        in_specs = [pl.BlockSpec((bm, k), lambda i, j: (i % tiles_per_seq, i // tiles_per_seq))]
    else:
        in_specs = [pl.BlockSpec((bm, k), lambda i, j: (i, kidx))]
    args = [x]
    if prologue is not None:
        in_specs.append(pl.BlockSpec((1, k), lambda i, j: (0, 0)))
        args.append(norm_g)
    if prologue == "norm_mod":
        in_specs += [pl.BlockSpec((None, 1, k), lambda i, j: (mod_idx(i), 0, 0))] * 2
        args += list(mod)
    in_specs.append(pl.BlockSpec((k, bn), lambda i, j: (0, j)))
    args.append(w.astype(BF16))
    if epilogue == "res":
        in_specs += [pl.BlockSpec((bm, bn), lambda i, j: (i, j)),
                     pl.BlockSpec((None, 1, bn), lambda i, j: (mod_idx(i), 0, j))]
        args += [res, gate]
    head_w = norm_div = 0
    norm_cols = rope_tiles = None
    if epilogue == "heads":
        head_w, norm_div = heads["head_w"], heads["norm_div"]
        norm_cols = heads["norm_cols"] if heads["norm_cols"] < n else None
        assert bn % max(head_w, LANE) == 0 and heads["norm_cols"] % max(head_w, LANE) == 0
        in_specs.append(pl.BlockSpec((1, bn), lambda i, j: (0, j)))
        args.append(heads["gains"])
        if rope:
            period = heads["tabs"][0].shape[1]
            rope_tiles = heads["rope_tiles"]
            assert bn % period == 0 and tiles_per_seq is not None and len(rope_tiles) == period // LANE
            in_specs += [pl.BlockSpec((bm, period), lambda i, j: (i % tiles_per_seq, 0))] * 3
            args += list(heads["tabs"])
    if out_time_major:
        n_col_tiles = n // bn
        out_shape = [jax.ShapeDtypeStruct((st.seq, st.nb * n), out_dtype)]
        out_specs = [pl.BlockSpec((bm, bn), lambda i, j: (i % tiles_per_seq, (i // tiles_per_seq) * n_col_tiles + j))]
    else:
        out_shape = [jax.ShapeDtypeStruct((rows, n), out_dtype)]
        out_specs = [pl.BlockSpec((bm, bn), lambda i, j: (i, j))]
    if emit_xn:
        out_shape.append(jax.ShapeDtypeStruct((rows, k), F32))
        out_specs.append(pl.BlockSpec((bm, k), lambda i, j: (i, 0)))
    xbytes = x.dtype.itemsize
    vmem = (2 * bm * k * xbytes + bm * k * 2 + 2 * k * bn * 2 + (6 if epilogue == "res" else 4) * bm * bn * 4
            + (2 * bm * k * 4 if emit_xn else 0)) / MIB + 8
    kern = functools.partial(_proj_kernel, prologue=prologue, emit_xn=emit_xn, epilogue=epilogue, head_w=head_w,
                             norm_div=norm_div, norm_cols=norm_cols, rope=rope, rope_tiles=rope_tiles,
                             row_chunk=min(bm, 128))
    direct_lhs = prologue is None and x.dtype == BF16 and not emit_xn
    out = pl.pallas_call(
        kern,
        grid=(rows // bm, n // bn),
        in_specs=in_specs,
        out_specs=out_specs,
        out_shape=out_shape,
        scratch_shapes=[] if direct_lhs else [pltpu.VMEM((bm, k), BF16)],
        compiler_params=_cparams(2, vmem),
        name=name,
    )(*args)
    return out if emit_xn else out[0]


def _ffn_kernel(xp_ref, x_ref, xn_ref, g_ref, sh_ref, sc_ref, gate_ref, wa_ref, wb_ref, cw_ref, cb_ref,
                wo_ref, o_ref, h_ref, *, bm, seq, row_chunk):
    i = pl.program_id(0)
    c = pl.program_id(1)
    n_chunks = pl.num_programs(1)

    @pl.when(c == 0)
    def _():
        g, sh, sc = g_ref[...], sh_ref[...], sc_ref[...]
        h_ref[0:HALO, :] = _norm_mod(xp_ref[...], g, sh, sc).astype(BF16)
        h_ref[HALO + bm:, :] = _norm_mod(xn_ref[...], g, sh, sc).astype(BF16)

        def chunk(r, carry):
            src = pl.ds(pl.multiple_of(r * row_chunk, row_chunk), row_chunk)
            dst = pl.ds(pl.multiple_of(HALO + r * row_chunk, HALO), row_chunk)
            h_ref[dst, :] = _norm_mod(x_ref[src, :], g, sh, sc).astype(BF16)
            return carry
        n_row_chunks = bm // row_chunk
        lax.fori_loop(0, n_row_chunks, chunk, 0, unroll=2 if n_row_chunks % 2 == 0 else 1)
        o_ref[...] = jnp.zeros_like(o_ref)

    ua = jnp.dot(h_ref[...], wa_ref[...], preferred_element_type=F32)
    ub = jnp.dot(h_ref[HALO:HALO + bm, :], wb_ref[...], preferred_element_type=F32)
    n_all = bm + 2 * HALO
    u_prev = pltpu.roll(ua, 1, 0)[HALO:HALO + bm]
    u_next = pltpu.roll(ua, n_all - 1, 0)[HALO:HALO + bm]
    u_mid = ua[HALO:HALO + bm]
    pos = jnp.bitwise_and(i * bm + lax.broadcasted_iota(jnp.int32, (bm, 1), 0), seq - 1)
    u_prev = jnp.where(pos == 0, 0.0, u_prev)
    u_next = jnp.where(pos == seq - 1, 0.0, u_next)
    cw = cw_ref[...]
    a = cb_ref[...] + u_prev * cw[0:1] + u_mid * cw[1:2] + u_next * cw[2:3]
    gated = ((a * jax.nn.sigmoid(a)) * ub).astype(BF16)
    o_ref[...] += jnp.dot(gated, wo_ref[...], preferred_element_type=F32)

    @pl.when(c == n_chunks - 1)
    def _():
        o_ref[...] = x_ref[...] + gate_ref[...] * o_ref[...]


def _conv_ffn(x, st, g, shift, scale, gate, w_in, conv_w, conv_b, w_out, bm=FFN_ROW_TILE, ck=FFN_FF_CHUNK):
    m, d = x.shape
    d_ff = w_out.shape[0]
    bm = min(st.bm, bm)
    ck = _largest_divisor(d_ff, tuple(c for c in (512, 256, 128) if c <= ck))
    n_chunks = d_ff // ck
    n_halo_blocks = m // HALO
    assert st.seq & (st.seq - 1) == 0 and conv_w.shape[0] == 3
    mod_idx = lambda i: st.mod_index(i * bm)
    kern = functools.partial(_ffn_kernel, bm=bm, seq=st.seq, row_chunk=min(bm, 128))
    vmem = (4 * bm * d * 4 + (bm + 2 * HALO) * d * 2 + 6 * d * ck * 2 + 5 * (bm + 2 * HALO) * ck * 4) / MIB + 4
    return pl.pallas_call(
        kern,
        grid=(m // bm, n_chunks),
        in_specs=[
            pl.BlockSpec((HALO, d), lambda i, c: (jnp.maximum(i * (bm // HALO) - 1, 0), 0)),
            pl.BlockSpec((bm, d), lambda i, c: (i, 0)),
            pl.BlockSpec((HALO, d), lambda i, c: (jnp.minimum((i + 1) * (bm // HALO), n_halo_blocks - 1), 0)),
            pl.BlockSpec((1, d), lambda i, c: (0, 0)),
            pl.BlockSpec((None, 1, d), lambda i, c: (mod_idx(i), 0, 0)),
            pl.BlockSpec((None, 1, d), lambda i, c: (mod_idx(i), 0, 0)),
            pl.BlockSpec((None, 1, d), lambda i, c: (mod_idx(i), 0, 0)),
            pl.BlockSpec((d, ck), lambda i, c: (0, c)),
            pl.BlockSpec((d, ck), lambda i, c: (0, n_chunks + c)),
            pl.BlockSpec((conv_w.shape[0], ck), lambda i, c: (0, c)),
            pl.BlockSpec((1, ck), lambda i, c: (0, c)),
            pl.BlockSpec((ck, d), lambda i, c: (c, 0)),
        ],
        out_specs=pl.BlockSpec((bm, d), lambda i, c: (i, 0)),
        out_shape=jax.ShapeDtypeStruct((m, d), F32),
        scratch_shapes=[pltpu.VMEM((bm + 2 * HALO, d), BF16)],
        compiler_params=_cparams(2, vmem),
        name="conv_ffn",
    )(x, x, x, g, shift, scale, gate, w_in, w_in, conv_w, conv_b.reshape(1, d_ff), w_out)


def _qk(q, k):
    return lax.dot_general(q, k, (((1,), (1,)), ((), ())), preferred_element_type=F32)


def _attend(scores, values, sink=None):
    m = None
    for s in scores:
        mi = jnp.max(s, axis=-1, keepdims=True)
        m = mi if m is None else jnp.maximum(m, mi)
    if sink is not None:
        m = jnp.maximum(m, sink)
    es = [jnp.exp2(s - m) for s in scores]
    den = None
    for e in es:
        di = jnp.sum(e, axis=-1, keepdims=True)
        den = di if den is None else den + di
    if sink is not None:
        den = den + jnp.exp2(sink - m)
    out = None
    for e, v in zip(es, values):
        oi = jnp.dot(e.astype(BF16), v, preferred_element_type=F32)
        out = oi if out is None else out + oi
    return out * (1.0 / den)


def _ctx_attn_kernel(q_ref, k_ref, v_ref, *outs, heads, dq, dv, emit_kv):
    o_ref = outs[0]
    for h in range(heads):
        q = q_ref[:, h * dq:(h + 1) * dq].astype(BF16)
        k = k_ref[:, h * dq:(h + 1) * dq]
        v = v_ref[:, h * dv:(h + 1) * dv]
        if emit_kv:
            outs[1][h] = k.astype(F32)
            outs[2][h] = v.astype(F32)
        o = _attend([_qk(q, k.astype(BF16))], [v.astype(BF16)])
        o_ref[:, h * dv:(h + 1) * dv] = o.astype(o_ref.dtype)


def _ctx_attention(qm, km, vm, st, *, n_heads, dq, dv, q_col, k_col, v_col, emit_kv=False):
    hb = _largest_divisor(n_heads, HEADS_PER_STEP)
    s = st.seq
    assert q_col % (hb * dq) == 0 and k_col % (hb * dq) == 0 and v_col % (hb * dv) == 0
    qo, ko, vo = q_col // (hb * dq), k_col // (hb * dq), v_col // (hb * dv)
    out_shape = [jax.ShapeDtypeStruct((st.rows, n_heads * dv), BF16)]
    out_specs = [pl.BlockSpec((s, hb * dv), lambda b, g: (b, g))]
    if emit_kv:
        out_shape += [jax.ShapeDtypeStruct((st.nb, n_heads, s, dq), F32),
                      jax.ShapeDtypeStruct((st.nb, n_heads, s, dv), F32)]
        out_specs += [pl.BlockSpec((None, hb, s, dq), lambda b, g: (b, g, 0, 0)),
                      pl.BlockSpec((None, hb, s, dv), lambda b, g: (b, g, 0, 0))]
    out = pl.pallas_call(
        functools.partial(_ctx_attn_kernel, heads=hb, dq=dq, dv=dv, emit_kv=emit_kv),
        grid=(st.nb, n_heads // hb),
        in_specs=[pl.BlockSpec((s, hb * dq), lambda b, g: (b, qo + g)),
                  pl.BlockSpec((s, hb * dq), lambda b, g: (b, ko + g)),
                  pl.BlockSpec((s, hb * dv), lambda b, g: (b, vo + g))],
        out_specs=out_specs,
        out_shape=out_shape,
        compiler_params=_cparams(2, 32),
        name="ctx_attention",
    )(qm, km, vm)
    return out if emit_kv else out[0]


def _nat_kernel(q_ref, k_ref, v_ref, kc_ref, vc_ref, bias_ref, o_ref, *, key_rows, rows, heads, dh):
    i = pl.program_id(2)
    n_keys = key_rows * GRID_W
    first_row = jnp.clip(i * NA_Q_ROWS - NA_WIN_ROWS // 2, 0, rows - key_rows)
    start = pl.multiple_of(first_row * GRID_W, GRID_W * 4)
    for h in range(heads):
        lanes = slice(h * dh, (h + 1) * dh)
        q = q_ref[:, lanes]
        k = k_ref[pl.ds(start, n_keys), lanes]
        v = v_ref[pl.ds(start, n_keys), lanes]
        s_loc = _qk(q, k) + bias_ref[h]
        s_ctx = _qk(q, kc_ref[h].astype(BF16))
        o_ref[:, lanes] = _attend([s_loc, s_ctx], [v, vc_ref[h].astype(BF16)]).astype(o_ref.dtype)


def _nat_bias(rpb, rows):
    n_blocks = rows // NA_Q_ROWS
    key_rows = min(NA_K_ROWS, rows)
    wr = min(NA_WIN_ROWS, rows)
    reps = [0, min(1, n_blocks - 1), n_blocks - 1]
    heads = rpb.shape[0]
    nq, nk = NA_Q_ROWS * GRID_W, key_rows * GRID_W
    shape = (NA_Q_ROWS, GRID_W, key_rows, GRID_W)
    qc = np.arange(GRID_W)
    cstart = np.clip(qc - NA_WIN_COLS // 2, 0, GRID_W - NA_WIN_COLS)
    col_ok = (qc[None, :] >= cstart[:, None]) & (qc[None, :] < cstart[:, None] + NA_WIN_COLS)
    rp = jnp.pad(rpb.astype(F32) * LOG2E,
                 ((0, 0), (key_rows, key_rows), (GRID_W - NA_WIN_COLS, GRID_W - NA_WIN_COLS)))
    row_slabs, mask_l = [], []
    for i in reps:
        ks = int(np.clip(i * NA_Q_ROWS - NA_WIN_ROWS // 2, 0, rows - key_rows))
        r = i * NA_Q_ROWS + np.arange(NA_Q_ROWS)
        rs = np.clip(r - wr // 2, 0, rows - wr)
        kr = ks + np.arange(key_rows)
        row_ok = (kr[None, :] >= rs[:, None]) & (kr[None, :] < rs[:, None] + wr)
        for rq in range(NA_Q_ROWS):
            first = ks - int(r[rq]) + NA_WIN_ROWS - 1 + key_rows
            assert 0 <= first and first + key_rows <= rp.shape[1]
            row_slabs.append(rp[:, first:first + key_rows, :])
        mask_l.append(np.broadcast_to(row_ok[:, None, :, None] & col_ok[None, :, None, :], shape).reshape(nq, nk))
    slab = jnp.stack(row_slabs, axis=1).reshape(heads, len(reps), NA_Q_ROWS, key_rows, 2 * GRID_W - 1)
    toep = jnp.stack([slab[..., GRID_W - 1 - c:2 * GRID_W - 1 - c] for c in range(GRID_W)], axis=3)
    bias = toep.reshape(heads, len(reps), nq, nk)
    return jnp.where(jnp.asarray(np.stack(mask_l))[None], bias, NEG)


def _nat_attention(qkv, st, cache_k, cache_v, j, rpb, dh):
    heads = rpb.shape[0]
    n = st.seq
    p = cache_k.shape[3]
    rows = n // GRID_W
    assert rows % NA_Q_ROWS == 0 and rows >= NA_K_ROWS and dh % LANE == 0
    n_blocks = rows // NA_Q_ROWS
    key_rows = min(NA_K_ROWS, rows)
    nq, nk = NA_Q_ROWS * GRID_W, key_rows * GRID_W
    bias = _nat_bias(rpb, rows)
    btype = lambda i: jnp.where(i == 0, 0, jnp.where(i == n_blocks - 1, 2, 1))
    hb = _largest_divisor(heads, HEADS_PER_STEP)
    hg = heads // hb
    kern = functools.partial(_nat_kernel, key_rows=key_rows, rows=rows, heads=hb, dh=dh)
    return pl.pallas_call(
        kern,
        grid=(st.nb, hg, n_blocks),
        in_specs=[pl.BlockSpec((nq, hb * dh), lambda b, h, i: (b * n_blocks + i, h)),
                  pl.BlockSpec((n, hb * dh), lambda b, h, i: (b, hg + h)),
                  pl.BlockSpec((n, hb * dh), lambda b, h, i: (b, 2 * hg + h)),
                  pl.BlockSpec((None, None, hb, p, dh), lambda b, h, i: (b, j, h, 0, 0)),
                  pl.BlockSpec((None, None, hb, p, dh), lambda b, h, i: (b, j, h, 0, 0)),
                  pl.BlockSpec((hb, None, nq, nk), lambda b, h, i: (h, btype(i), 0, 0))],
        out_specs=pl.BlockSpec((nq, hb * dh), lambda b, h, i: (b * n_blocks + i, h)),
        out_shape=jax.ShapeDtypeStruct((st.rows, heads * dh), BF16),
        compiler_params=_cparams(3, 56),
        name="nat_attention",
    )(qkv, qkv, qkv, cache_k, cache_v, bias)


def _joint_dense_kernel(q_ref, k_ref, v_ref, kc_ref, vc_ref, o_ref, *, chunk):
    q = q_ref[...]
    n = k_ref.shape[0]
    pieces = [(k_ref, v_ref, c0, min(chunk, n - c0)) for c0 in range(0, n, chunk)]
    pieces.append((kc_ref, vc_ref, 0, kc_ref.shape[0]))
    m = den = acc = None
    for kr, vr, c0, size in pieces:
        s = _qk(q, kr[c0:c0 + size, :])
        mc = jnp.max(s, axis=-1, keepdims=True)
        m_new = mc if m is None else jnp.maximum(m, mc)
        e = jnp.exp2(s - m_new)
        dc = jnp.sum(e, axis=-1, keepdims=True)
        pv = jnp.dot(e.astype(BF16), vr[c0:c0 + size, :], preferred_element_type=F32)
        if m is None:
            den, acc = dc, pv
        else:
            alpha = jnp.exp2(m - m_new)
            den, acc = alpha * den + dc, alpha * acc + pv
        m = m_new
    o_ref[...] = (acc * (1.0 / den)).astype(o_ref.dtype)


def _joint_dense_attention(qm, km, vm, kcm, vcm, st, p, *, n_heads, dq, dv):
    n = st.seq
    bq = _largest_divisor(n, (1024, 512, 256, 128, 64, 32, 16))
    nqb = n // bq
    return pl.pallas_call(
        functools.partial(_joint_dense_kernel, chunk=MLA_KEY_CHUNK),
        grid=(st.nb, n_heads, nqb),
        in_specs=[pl.BlockSpec((bq, dq), lambda b, h, i: (b * nqb + i, h)),
                  pl.BlockSpec((n, dq), lambda b, h, i: (b, h)),
                  pl.BlockSpec((n, dv), lambda b, h, i: (b, h)),
                  pl.BlockSpec((p, dq), lambda b, h, i: (b, h)),
                  pl.BlockSpec((p, dv), lambda b, h, i: (b, h))],
        out_specs=pl.BlockSpec((bq, dv), lambda b, h, i: (b * nqb + i, h)),
        out_shape=jax.ShapeDtypeStruct((st.rows, n_heads * dv), BF16),
        compiler_params=_cparams(3, 48),
        name="mla_attention",
    )(qm, km, vm, kcm, vcm)


def _both_halves(x, s):
    x = x.astype(F32)
    low = lax.broadcasted_iota(jnp.int32, (1, LANE), 1) < LANE // 2
    keep = low if s == 0 else jnp.logical_not(low)
    return jnp.where(keep, x, pltpu.roll(x, LANE // 2, 1)).astype(BF16)


def _swa_step(sinks_ref, pair, q_ref, k, v, kc, vc, o_ref, *, dh, groups, local_bias):
    kv_per_step = LANE // dh
    assert kv_per_step == 2 and groups % 2 == 0
    rows = q_ref.shape[0]
    low = lax.broadcasted_iota(jnp.int32, (1, LANE), 1) < dh
    row_group = lax.broadcasted_iota(jnp.int32, (groups * rows, 1), 0) // rows
    for s in range(kv_per_step):
        kd, vd = _both_halves(k, s), _both_halves(v, s)
        q_parts = []
        for g in range(groups):
            c0 = ((s * groups + g) * dh // LANE) * LANE
            tile = q_ref[:, c0:c0 + LANE].astype(BF16)
            q_parts.append(jnp.where(low if g % 2 == 0 else jnp.logical_not(low), tile, jnp.zeros_like(tile)))
        q = jnp.concatenate(q_parts, axis=0)
        sink = jnp.zeros((groups * rows, 1), F32)
        for g in range(groups):
            sink = jnp.where(row_group == g, sinks_ref[(pair * kv_per_step + s) * groups + g], sink)
        s_loc = _qk(q, kd)
        if local_bias is not None:
            s_loc = s_loc + jnp.concatenate([local_bias] * groups, axis=0)
        if kc is not None:
            kcd = jnp.concatenate([kc[s], kc[s]], axis=-1).astype(BF16)
            vcd = jnp.concatenate([vc[s], vc[s]], axis=-1).astype(BF16)
            out = _attend([s_loc, _qk(q, kcd)], [vd, vcd], sink)
        else:
            out = _attend([s_loc], [vd], sink)
        for g in range(0, groups, 2):
            c0 = (s * groups + g) * dh
            o_ref[:, c0:c0 + LANE] = jnp.where(low, out[g * rows:(g + 1) * rows],
                                               out[(g + 1) * rows:(g + 2) * rows]).astype(o_ref.dtype)


def _swa_ctx_kernel(sinks_ref, q_ref, k_ref, v_ref, o_ref, ko_ref, vo_ref, *, dh, groups):
    pair = pl.program_id(1)
    k, v = k_ref[...], v_ref[...]
    for s in range(LANE // dh):
        ko_ref[s] = k[:, s * dh:(s + 1) * dh].astype(F32)
        vo_ref[s] = v[:, s * dh:(s + 1) * dh].astype(F32)
    _swa_step(sinks_ref, pair, q_ref, k, v, None, None, o_ref, dh=dh, groups=groups, local_bias=None)


def _swa_lat_kernel(sinks_ref, q_ref, k_ref, v_ref, kc_ref, vc_ref, o_ref, *, dh, groups, n):
    pair = pl.program_id(1)
    blk = pl.program_id(2)
    n_keys = min(3 * SWA_BLOCK, n)
    start = pl.multiple_of(jnp.clip((blk - 1) * SWA_BLOCK, 0, n - n_keys), SWA_BLOCK)
    k = k_ref[pl.ds(start, n_keys), :]
    v = v_ref[pl.ds(start, n_keys), :]
    qpos = blk * SWA_BLOCK + lax.broadcasted_iota(jnp.int32, (SWA_BLOCK, 1), 0)
    kpos = start + lax.broadcasted_iota(jnp.int32, (1, n_keys), 1)
    bias = jnp.where(jnp.abs(qpos - kpos) <= SWA_WINDOW, 0.0, NEG)
    _swa_step(sinks_ref, pair, q_ref, k, v, kc_ref, vc_ref, o_ref, dh=dh, groups=groups, local_bias=bias)


def _swa_attention(qkv, st, sinks, *, heads, kvh, dh, cache=None):
    groups = heads // kvh
    kv_per_step = LANE // dh
    assert LANE % dh == 0 and kvh % kv_per_step == 0 and groups % kv_per_step == 0
    pairs = kvh // kv_per_step
    qw = kv_per_step * groups * dh
    k_blk = heads * dh // LANE
    v_blk = (heads + kvh) * dh // LANE
    n = st.seq
    common = dict(dh=dh, groups=groups)
    smem = pl.BlockSpec(memory_space=pltpu.SMEM)
    if cache is None:
        out = pl.pallas_call(
            functools.partial(_swa_ctx_kernel, **common),
            grid=(st.nb, pairs),
            in_specs=[smem,
                      pl.BlockSpec((n, qw), lambda b, c: (b, c)),
                      pl.BlockSpec((n, LANE), lambda b, c: (b, k_blk + c)),
                      pl.BlockSpec((n, LANE), lambda b, c: (b, v_blk + c))],
            out_specs=[pl.BlockSpec((n, qw), lambda b, c: (b, c)),
                       pl.BlockSpec((None, kv_per_step, n, dh), lambda b, c: (b, c, 0, 0)),
                       pl.BlockSpec((None, kv_per_step, n, dh), lambda b, c: (b, c, 0, 0))],
            out_shape=[jax.ShapeDtypeStruct((st.rows, heads * dh), BF16),
                       jax.ShapeDtypeStruct((st.nb, kvh, n, dh), F32),
                       jax.ShapeDtypeStruct((st.nb, kvh, n, dh), F32)],
            compiler_params=_cparams(2, 32),
            name="swa_ctx_attention",
        )(sinks, qkv, qkv, qkv)
        return out
    cache_k, cache_v, j = cache
    p = cache_k.shape[3]
    nblk = n // SWA_BLOCK
    assert n % SWA_BLOCK == 0
    return pl.pallas_call(
        functools.partial(_swa_lat_kernel, n=n, **common),
        grid=(st.nb, pairs, nblk),
        in_specs=[smem,
                  pl.BlockSpec((SWA_BLOCK, qw), lambda b, c, i: (b * nblk + i, c)),
                  pl.BlockSpec((n, LANE), lambda b, c, i: (b, k_blk + c)),
                  pl.BlockSpec((n, LANE), lambda b, c, i: (b, v_blk + c)),
                  pl.BlockSpec((None, None, kv_per_step, p, dh), lambda b, c, i: (b, j, c, 0, 0)),
                  pl.BlockSpec((None, None, kv_per_step, p, dh), lambda b, c, i: (b, j, c, 0, 0))],
        out_specs=pl.BlockSpec((SWA_BLOCK, qw), lambda b, c, i: (b * nblk + i, c)),
        out_shape=jax.ShapeDtypeStruct((st.rows, heads * dh), BF16),
        compiler_params=_cparams(3, 32),
        name="swa_attention",
    )(sinks, qkv, qkv, qkv, cache_k, cache_v)


def _mla_kv_kernel(*refs, norm, emit_xn, rope, row_chunk, norm_div):
    it = iter(refs)
    x_ref = next(it)
    g_ref = next(it) if norm else None
    w_ref, kr_ref, g1_ref, g2_ref = next(it), next(it), next(it), next(it)
    tabs = (next(it), next(it), next(it)) if rope else None
    k_ref, v_ref = next(it), next(it)
    xn_ref = next(it) if emit_xn else None
    xs_ref = next(it)

    @pl.when(pl.program_id(1) == 0)
    def _():
        _fill_lhs(x_ref, xs_ref, xn_ref, "norm" if norm else None, g_ref, None, None, row_chunk)

    acc = jnp.dot(xs_ref[...], w_ref[...], preferred_element_type=F32)
    kr = kr_ref[...]
    kr_ssq = jnp.sum(kr * kr, axis=-1, keepdims=True)
    shared = kr * g2_ref[...]
    if rope:
        shared = _rope_apply(shared, *(t[...] for t in tabs))
    for h in range(acc.shape[1] // (2 * LANE)):
        nope = acc[:, 2 * h * LANE:(2 * h + 1) * LANE]
        inv = lax.rsqrt((jnp.sum(nope * nope, axis=-1, keepdims=True) + kr_ssq) / norm_div + EPS)
        k_ref[:, 2 * h * LANE:(2 * h + 1) * LANE] = ((nope * inv) * g1_ref[...]).astype(k_ref.dtype)
        k_ref[:, (2 * h + 1) * LANE:(2 * h + 2) * LANE] = (shared * inv).astype(k_ref.dtype)
        v_ref[:, h * LANE:(h + 1) * LANE] = acc[:, (2 * h + 1) * LANE:(2 * h + 2) * LANE].astype(v_ref.dtype)


def _mla_kv(x, x_block, w_ukv, kr, kr_block, g_kva, g1, g2, tabs, st, *, n_heads, norm_div, emit_xn, name):
    rows = x.shape[0]
    k, n = w_ukv.shape
    head_n = n // n_heads
    assert head_n == 2 * LANE, "nope and value widths must both be one lane tile"
    kx, kidx = x_block
    krw, kridx = kr_block
    assert kx == k and krw == LANE
    bm = st.bm
    norm = g_kva is not None
    rope = tabs is not None
    in_specs = [pl.BlockSpec((bm, k), lambda i, h: (i, kidx))]
    args = [x]
    if norm:
        in_specs.append(pl.BlockSpec((1, k), lambda i, h: (0, 0)))
        args.append(g_kva)
    hb = _largest_divisor(n_heads, HEADS_PER_STEP)
    in_specs += [pl.BlockSpec((k, hb * head_n), lambda i, h: (0, h)),
                 pl.BlockSpec((bm, LANE), lambda i, h: (i, kridx)),
                 pl.BlockSpec((1, LANE), lambda i, h: (0, 0)),
                 pl.BlockSpec((1, LANE), lambda i, h: (0, 0))]
    args += [w_ukv, kr, g1, g2]
    if rope:
        tiles_per_seq = st.seq // bm
        in_specs += [pl.BlockSpec((bm, LANE), lambda i, h: (i % tiles_per_seq, 0))] * 3
        args += list(tabs)
    out_shape = [jax.ShapeDtypeStruct((rows, n_heads * 2 * LANE), BF16),
                 jax.ShapeDtypeStruct((rows, n_heads * LANE), BF16)]
    out_specs = [pl.BlockSpec((bm, hb * 2 * LANE), lambda i, h: (i, h)),
                 pl.BlockSpec((bm, hb * LANE), lambda i, h: (i, h))]
    if emit_xn:
        out_shape.append(jax.ShapeDtypeStruct((rows, k), F32))
        out_specs.append(pl.BlockSpec((bm, k), lambda i, h: (i, 0)))
    kern = functools.partial(_mla_kv_kernel, norm=norm, emit_xn=emit_xn, rope=rope, row_chunk=min(bm, 128),
                             norm_div=norm_div)
    return pl.pallas_call(
        kern,
        grid=(rows // bm, n_heads // hb),
        in_specs=in_specs,
        out_specs=out_specs,
        out_shape=out_shape,
        scratch_shapes=[pltpu.VMEM((bm, k), BF16)],
        compiler_params=_cparams(2, 40),
        name=name,
    )(*args)


def _band_plan(width, block):
    n_tiles = width // LANE
    lo = [((t * LANE) // block) * block for t in range(n_tiles)]
    hi = [(((t + 1) * LANE - 1) // block + 1) * block for t in range(n_tiles)]
    start = [(l // LANE) * LANE for l in lo]
    kb = max(-(-(h - s) // LANE) * LANE for h, s in zip(hi, start))
    kb = min(kb, width)
    start = [min(s, width - kb) for s in start]
    return start, kb


def _band_weights(w, width, block, start, kb):
    n_tiles = width // LANE
    wb = w.astype(BF16)
    tiles = []
    for t in range(n_tiles):
        pieces = []
        col = t * LANE
        while col < (t + 1) * LANE:
            blk = col // block
            col_end = min((blk + 1) * block, (t + 1) * LANE)
            sub = wb[blk, :, col - blk * block:col_end - blk * block]
            top = blk * block - start[t]
            pieces.append(jnp.pad(sub, ((top, kb - top - block), (0, 0))))
            col = col_end
        tiles.append(jnp.concatenate(pieces, axis=1))
    return jnp.stack(tiles)


def _gelu_tanh(x):
    cdf = 0.5 * (1.0 + jnp.tanh(np.float32(np.sqrt(2.0 / np.pi)) * (x + 0.044715 * (x * x * x))))
    return x * cdf


def _lru_pass_kernel(*refs, reverse, starts, kb, bt, nb, taps):
    left = taps // 2
    right = taps - 1 - left
    it = iter(refs)
    if reverse:
        conv_ref = next(it)
    else:
        xp_ref, x_ref = next(it), next(it)
        xn_ref = next(it) if right > 0 else None
        cw_ref, cb_ref = next(it), next(it)
    wa_ref, wi_ref, ba_ref, bi_ref, lam_ref, h0_ref = (next(it) for _ in range(6))
    hsf_ref, gate_ref = (next(it), next(it)) if reverse else (None, None)
    out_ref, ht_ref = next(it), next(it)
    conv_out_ref = None if reverse else next(it)
    xc_s, xb_s, a_s, bx_s, carry = next(it), next(it), next(it), next(it), next(it)

    step = pl.program_id(0)
    n_steps = pl.num_programs(0)
    n_tiles = len(starts)
    rows = nb * bt

    @pl.when(step == 0)
    def _():
        carry[...] = h0_ref[...]

    for t in range(n_tiles):
        lanes = slice(t * LANE, (t + 1) * LANE)
        if reverse:
            acc = conv_ref[:, :, lanes].reshape(rows, LANE)
        else:
            parts = [jnp.where(step > 0, xp_ref[:, :, lanes], 0.0), x_ref[:, :, lanes]]
            if right > 0:
                parts.append(jnp.where(step < n_steps - 1, xn_ref[:, :, lanes], 0.0))
            full = jnp.concatenate(parts, axis=0)
            acc = jnp.broadcast_to(cb_ref[:, lanes], (bt, nb, LANE))
            for k in range(taps):
                acc = acc + full[k:k + bt] * cw_ref[k:k + 1, lanes]
            conv_out_ref[:, :, lanes] = acc
            acc = acc.reshape(rows, LANE)
        xc_s[:, lanes] = acc
        xb_s[:, lanes] = acc.astype(BF16)

    neg_lam = -lam_ref[...]
    softplus = jnp.maximum(neg_lam, 0.0) + jnp.log1p(jnp.exp(-jnp.abs(neg_lam)))
    half_rate = (-0.5 * LRU_C) * softplus
    half_ba, half_bi = 0.5 * ba_ref[...], 0.5 * bi_ref[...]
    for t in range(n_tiles):
        lanes = slice(t * LANE, (t + 1) * LANE)
        xw = xb_s[:, starts[t]:starts[t] + kb]
        tanh_a = jnp.tanh(jnp.dot(xw, wa_ref[t], preferred_element_type=F32) + half_ba[:, lanes])
        tanh_i = jnp.tanh(jnp.dot(xw, wi_ref[t], preferred_element_type=F32) + half_bi[:, lanes])
        log_a = half_rate[:, lanes] * tanh_a + half_rate[:, lanes]
        a = jnp.exp(log_a)
        half_x = 0.5 * xc_s[:, lanes]
        a_s[:, lanes] = a
        bx_s[:, lanes] = jnp.sqrt(-jnp.tanh(log_a) * (a * a + 1.0)) * (half_x * tanh_i + half_x)

    h = carry[...]
    for s in range(bt):
        ts = (bt - 1 - s) if reverse else s
        slab = slice(ts * nb, (ts + 1) * nb)
        h = a_s[slab, :] * h + bx_s[slab, :]
        a_s[slab, :] = h
    carry[...] = h
    ht_ref[...] = h
    hs = a_s[...].reshape(bt, nb, a_s.shape[1])
    if reverse:
        out_ref[...] = (_gelu_tanh(gate_ref[...]) * (hsf_ref[...] + hs)).astype(out_ref.dtype)
    else:
        out_ref[...] = hs


def _lru_pass(u, st, conv_w, conv_b, wa, wi, b_a, b_i, lam, h0, starts, kb, *, reverse, fwd=None):
    c = conv_w.shape[1]
    taps = conv_w.shape[0]
    left, right = taps // 2, taps - 1 - taps // 2
    nb, seq = st.nb, st.seq
    bt = min(max(LRU_ROWS_PER_STEP // nb, SUBLANE), seq)
    assert seq % bt == 0 and c % LANE == 0 and left > 0 and bt % left == 0 and (right == 0 or bt % right == 0)
    nt = seq // bt
    n_tiles = c // LANE
    u3 = u.reshape(seq, nb, 2 * c)
    tmap = (lambda s: nt - 1 - s) if reverse else (lambda s: s)
    full = lambda *shape: pl.BlockSpec(shape, lambda s: (0,) * len(shape))
    tile_spec = lambda col: pl.BlockSpec((bt, nb, c), lambda s: (tmap(s), 0, col))
    if reverse:
        in_specs, args = [tile_spec(0)], [fwd[1]]
    else:
        in_specs = [pl.BlockSpec((left, nb, c), lambda s: (jnp.maximum(s * (bt // left) - 1, 0), 0, 0)), tile_spec(0)]
        args = [u3, u3]
        if right > 0:
            in_specs.append(pl.BlockSpec((right, nb, c),
                                         lambda s: (jnp.minimum((s + 1) * (bt // right), seq // right - 1), 0, 0)))
            args.append(u3)
        in_specs += [full(taps, c), full(1, c)]
        args += [conv_w, conv_b.reshape(1, c)]
    in_specs += [full(n_tiles, kb, LANE), full(n_tiles, kb, LANE), full(1, c), full(1, c), full(1, c), full(nb, c)]
    args += [wa, wi, b_a.reshape(1, c), b_i.reshape(1, c), lam.reshape(1, c), h0]
    out_specs = [tile_spec(0), pl.BlockSpec((nb, c), lambda s: (0, 0))]
    out_shape = [jax.ShapeDtypeStruct((seq, nb, c), F32), jax.ShapeDtypeStruct((nb, c), F32)]
    if reverse:
        in_specs += [tile_spec(0), tile_spec(1)]
        args += [fwd[0], u3]
    else:
        out_specs.append(tile_spec(0))
        out_shape.append(jax.ShapeDtypeStruct((seq, nb, c), F32))
    kern = functools.partial(_lru_pass_kernel, reverse=reverse, starts=tuple(starts), kb=kb, bt=bt, nb=nb, taps=taps)
    blk = nb * bt * c * 4 / MIB
    vmem = (2 + 2 + 3 + (4 if reverse else 2) + 4) * blk + 4 * n_tiles * kb * LANE * 2 / MIB + 8
    return pl.pallas_call(
        kern,
        grid=(nt,),
        in_specs=in_specs,
        out_specs=out_specs,
        out_shape=out_shape,
        scratch_shapes=[pltpu.VMEM((nb * bt, c), F32), pltpu.VMEM((nb * bt, c), BF16),
                        pltpu.VMEM((nb * bt, c), F32), pltpu.VMEM((nb * bt, c), F32), pltpu.VMEM((nb, c), F32)],
        compiler_params=_cparams(1, vmem),
        name="lru_bwd" if reverse else "lru_fwd",
    )(*args)


def _mixer_nat(xs, streams, mods, cache_k, cache_v, j, w_qkv, g_mix, g_q, g_k, rpb, w_o):
    heads, dh = rpb.shape[0], g_q.shape[0]
    w_qkv, w_o = w_qkv.astype(BF16), w_o.astype(BF16)
    gains = jnp.concatenate([jnp.tile(g_q * (dh ** -0.5 * LOG2E), heads), jnp.tile(g_k, heads),
                             jnp.ones((heads * dh,), F32)])[None]
    spec = dict(head_w=dh, norm_div=dh, norm_cols=2 * heads * dh, gains=gains)
    new_x, extra = [], None
    for x, st in zip(xs, streams):
        latent = not st.shared
        qkv = _proj(x, w_qkv, st, norm_g=g_mix, mod=(mods[0], mods[1]), heads=spec,
                    out_dtype=BF16 if latent else F32, name="nat_qkv")
        if latent:
            o = _nat_attention(qkv, st, cache_k, cache_v, j, rpb, dh)
        else:
            o, kc, vc = _ctx_attention(qkv, qkv, qkv, st, n_heads=heads, dq=dh, dv=dh, q_col=0, k_col=heads * dh,
                                       v_col=2 * heads * dh, emit_kv=True)
            extra = (kc, vc)
        new_x.append(_proj(o, w_o, st, res=x, gate=mods[2], name="nat_out"))
    return new_x, extra


def _mixer_lru(xs, streams, mods, state, w_in, g_mix, conv_w, conv_b, w_a, b_a, w_i, b_i, lam, w_out):
    c = conv_w.shape[1]
    block = w_a.shape[-1]
    w_in, w_out = w_in.astype(BF16), w_out.astype(BF16)
    starts, kb = _band_plan(c, block)
    wa = [_band_weights(0.5 * w_a[d], c, block, starts, kb) for d in range(2)]
    wi = [_band_weights(0.5 * w_i[d], c, block, starts, kb) for d in range(2)]
    new_x, st_out = [], None
    for x, st in zip(xs, streams):
        latent = not st.shared
        h0 = state.astype(F32) if latent else jnp.zeros((st.nb, 2, c), F32)
        u = _proj(x, w_in, st, norm_g=g_mix, mod=(mods[0], mods[1]), out_time_major=True, name="lru_in")
        hs_f, t_f, conv = _lru_pass(u, st, conv_w, conv_b, wa[0], wi[0], b_a[0], b_i[0], lam[0], h0[:, 0], starts,
                                    kb, reverse=False)
        y, t_b = _lru_pass(u, st, conv_w, conv_b, wa[1], wi[1], b_a[1], b_i[1], lam[1], h0[:, 1], starts, kb,
                           reverse=True, fwd=(hs_f, conv))
        if not latent:
            st_out = jnp.stack([t_f, t_b], axis=1)
        new_x.append(_proj(y.reshape(st.seq, st.nb * c), w_out, st, res=x, gate=mods[2], x_time_major=True,
                           name="lru_out"))
    return new_x, st_out


def _mixer_mla(xs, streams, mods, cache_ckv, cache_kr, w_down, g_mix, g_qa, g_kva, w_uq, w_ukv, g_q, g_k, w_o):
    d_model = w_down.shape[0]
    q_rank, kv_rank = g_qa.shape[0], g_kva.shape[0]
    qk_dim = g_q.shape[0]
    heads = w_uq.shape[1] // qk_dim
    rope = w_down.shape[1] - q_rank - kv_rank
    nope = qk_dim - rope
    assert nope == LANE and rope <= LANE and kv_rank % LANE == 0 and q_rank % LANE == 0
    head_w = 2 * LANE
    q_pad = -q_rank % kv_rank
    kv_col = q_rank + q_pad
    zeros = lambda width: jnp.zeros((d_model, width), F32)
    w_qd, w_kvd, w_rd = w_down[:, :q_rank], w_down[:, q_rank:q_rank + kv_rank], w_down[:, q_rank + kv_rank:]
    if q_pad >= LANE:
        kr_col = q_rank
        w_dn = jnp.concatenate([w_qd, w_rd, zeros(q_pad - rope), w_kvd], axis=1).astype(BF16)
    else:
        kr_col = kv_col + kv_rank
        w_dn = jnp.concatenate([w_qd, zeros(q_pad), w_kvd, w_rd, zeros(LANE - rope)], axis=1).astype(BF16)
    kr_blk = kr_col // LANE
    w_q = jnp.pad(w_uq.reshape(q_rank, heads, qk_dim), ((0, 0), (0, 0), (0, head_w - qk_dim)))
    w_q = w_q.reshape(q_rank, heads * head_w).astype(BF16)
    w_ukv, w_o = w_ukv.astype(BF16), w_o.astype(BF16)
    gq = jnp.tile(jnp.pad(g_q * (qk_dim ** -0.5 * LOG2E), (0, head_w - qk_dim)), heads)[None]
    g1, g2 = g_k[None, :nope], jnp.pad(g_k[nope:], (0, LANE - rope))[None]
    p = cache_ckv.shape[1]
    new_x, extra = [], None
    for x, st in zip(xs, streams):
        latent = not st.shared
        d = _proj(x, w_dn, st, norm_g=g_mix, mod=(mods[0], mods[1]), name="mla_down")
        q_tabs = _rope_tables(st.seq, rope, nope, head_w) if latent else None
        k_tabs = _rope_tables(st.seq, rope, 0, LANE) if latent else None
        q = _proj(d, w_q, st, x_block=(q_rank, 0), norm_g=g_qa[None],
                  heads=dict(head_w=head_w, norm_div=qk_dim, norm_cols=heads * head_w, gains=gq, tabs=q_tabs,
                             rope_tiles=(False, True)),
                  out_dtype=BF16, name="mla_uq")
        kv = _mla_kv(d, (kv_rank, kv_col // kv_rank), w_ukv, d, (LANE, kr_blk), g_kva[None], g1, g2, k_tabs, st,
                     n_heads=heads, norm_div=qk_dim, emit_xn=not latent, name="mla_ukv")
        if latent:
            k, v = kv
            cst = _Stream(st.nb, p, 0, True)
            krc = jnp.pad(cache_kr.reshape(st.nb * p, rope), ((0, 0), (0, LANE - rope)))
            kc, vc = _mla_kv(cache_ckv.reshape(st.nb * p, kv_rank), (kv_rank, 0), w_ukv, krc, (LANE, 0), None,
                             g1, g2, None, cst, n_heads=heads, norm_div=qk_dim, emit_xn=False,
                             name="mla_ukv_cache")
            o = _joint_dense_attention(q, k, v, kc, vc, st, p, n_heads=heads, dq=head_w, dv=LANE)
        else:
            k, v, ckv = kv
            o = _ctx_attention(q, k, v, st, n_heads=heads, dq=head_w, dv=LANE, q_col=0, k_col=0, v_col=0)
            kr_out = d[:, kr_col:kr_col + rope]
            extra = (ckv.reshape(st.nb, st.seq, kv_rank), kr_out.reshape(st.nb, st.seq, rope))
        new_x.append(_proj(o, w_o, st, res=x, gate=mods[2], name="mla_out"))
    return new_x, extra


def _mixer_swa(xs, streams, mods, cache_k, cache_v, j, w_qkv, g_mix, g_q, g_k, sinks, w_o):
    dh = g_q.shape[0]
    heads = sinks.shape[0]
    kvh = (w_qkv.shape[1] // dh - heads) // 2
    w_qkv, w_o = w_qkv.astype(BF16), w_o.astype(BF16)
    gains = jnp.concatenate([jnp.tile(g_q * (dh ** -0.5 * LOG2E), heads), jnp.tile(g_k, kvh),
                             jnp.ones((kvh * dh,), F32)])[None]
    sinks = sinks.astype(F32) * LOG2E
    new_x, extra = [], None
    for x, st in zip(xs, streams):
        latent = not st.shared
        tabs = _rope_tables(st.seq, dh, 0, dh) if latent else None
        if tabs is not None:
            tabs = tuple(jnp.tile(t, (1, LANE // dh)) for t in tabs)
        spec = dict(head_w=dh, norm_div=dh, norm_cols=(heads + kvh) * dh, gains=gains, tabs=tabs,
                    rope_tiles=(True,))
        qkv = _proj(x, w_qkv, st, norm_g=g_mix, mod=(mods[0], mods[1]), heads=spec,
                    out_dtype=BF16 if latent else F32, name="swa_qkv")
        if latent:
            o = _swa_attention(qkv, st, sinks, heads=heads, kvh=kvh, dh=dh, cache=(cache_k, cache_v, j))
        else:
            o, kc, vc = _swa_attention(qkv, st, sinks, heads=heads, kvh=kvh, dh=dh)
            extra = (kc, vc)
        new_x.append(_proj(o, w_o, st, res=x, gate=mods[2], name="swa_out"))
    return new_x, extra


def kernel(x_prompt, x_sample, cache_nat_k, cache_nat_v, state_lru, cache_mla_ckv, cache_mla_krope, cache_swa_k, cache_swa_v, c, c_ctx, norm_mix, norm_ffn, w_mod, b_mod, ffn_w_in, ffn_conv_w, ffn_conv_b, ffn_w_out, nat_w_qkv, nat_q_norm, nat_k_norm, nat_rpb, nat_w_o, lru_w_in, lru_conv_w, lru_conv_b, lru_w_a, lru_b_a, lru_w_i, lru_b_i, lru_lambda, lru_w_out, mla_w_down, mla_q_a_norm, mla_kv_a_norm, mla_w_uq, mla_w_ukv, mla_q_norm, mla_k_norm, mla_w_o, swa_w_qkv, swa_q_norm, swa_k_norm, swa_sinks, swa_w_o):
    bc, sc, d = x_prompt.shape
    bl, n, _ = x_sample.shape
    depth = w_mod.shape[0]
    streams = (_Stream(bc, sc, 0, True), _Stream(bl, n, 1, False))
    xs = [x_prompt.reshape(bc * sc, d), x_sample.reshape(bl * n, d)]

    n_cond = 1 + bl
    cond_rows = -(-n_cond // SUBLANE) * SUBLANE
    cond = jnp.zeros((cond_rows, d), F32).at[0].set(c_ctx).at[1:n_cond].set(c)
    mods = _modulation(cond, w_mod, b_mod)[:, :n_cond]

    nat_k_l, nat_v_l, lru_l, ckv_l, krope_l, swa_k_l, swa_v_l = [], [], [], [], [], [], []
    for l in range(depth):
        kind, j = l % 4, l // 4
        m6 = [mods[l, :, None, t * d:(t + 1) * d] for t in range(6)]
        g_mix = norm_mix[l].reshape(1, d)
        if kind == 0:
            xs, (kc, vc) = _mixer_nat(xs, streams, m6, cache_nat_k, cache_nat_v, j, nat_w_qkv[j], g_mix,
                                      nat_q_norm[j], nat_k_norm[j], nat_rpb[j], nat_w_o[j])
            nat_k_l.append(kc)
            nat_v_l.append(vc)
        elif kind == 1:
            xs, st = _mixer_lru(xs, streams, m6, state_lru[:, j], lru_w_in[j], g_mix, lru_conv_w[j], lru_conv_b[j],
                                lru_w_a[j], lru_b_a[j], lru_w_i[j], lru_b_i[j], lru_lambda[j], lru_w_out[j])
            lru_l.append(st)
        elif kind == 2:
            xs, (ckv, kr) = _mixer_mla(xs, streams, m6, cache_mla_ckv[:, j], cache_mla_krope[:, j], mla_w_down[j],
                                       g_mix, mla_q_a_norm[j], mla_kv_a_norm[j], mla_w_uq[j], mla_w_ukv[j],
                                       mla_q_norm[j], mla_k_norm[j], mla_w_o[j])
            ckv_l.append(ckv)
            krope_l.append(kr)
        else:
            xs, (kc, vc) = _mixer_swa(xs, streams, m6, cache_swa_k, cache_swa_v, j, swa_w_qkv[j], g_mix,
                                      swa_q_norm[j], swa_k_norm[j], swa_sinks[j], swa_w_o[j])
            swa_k_l.append(kc)
            swa_v_l.append(vc)
        w_in, w_out = ffn_w_in[l].astype(BF16), ffn_w_out[l].astype(BF16)
        xs = [_conv_ffn(x, st, norm_ffn[l].reshape(1, d), m6[3], m6[4], m6[5], w_in, ffn_conv_w[l],
                        ffn_conv_b[l], w_out) for x, st in zip(xs, streams)]

    return (xs[0].reshape(bc, sc, d), xs[1].reshape(bl, n, d), jnp.stack(nat_k_l, axis=1),
            jnp.stack(nat_v_l, axis=1), jnp.stack(lru_l, axis=1), jnp.stack(ckv_l, axis=1),
            jnp.stack(krope_l, axis=1), jnp.stack(swa_k_l, axis=1), jnp.stack(swa_v_l, axis=1))
```

```python
import functools

import numpy as np
import jax
import jax.numpy as jnp
from jax import lax
from jax.experimental import pallas as pl
from jax.experimental.pallas import tpu as pltpu

F32 = jnp.float32
BF16 = jnp.bfloat16

GRID_W = 64
NA_WIN_ROWS = 8
NA_WIN_COLS = 16
NA_Q_ROWS = 4
NA_K_ROWS = 12
LRU_C = 8.0
SWA_WINDOW = 128
SWA_BLOCK = 128
ROPE_BASE = 10000.0
ROPE_GROUP = 32
EPS = 1e-6
NEG = -1e30
LOG2E = float(np.log2(np.e))
LANE = 128
SUBLANE = 8
HALO = 16
MIB = 1024 * 1024
VMEM_LIMIT_CAP_MIB = 60
ROW_TILES = (1024, 512, 256, 128, 64, 32, 16)
MAX_COL_TILE = 1024
MAX_COL_TILE_RESIDUAL = 512
FFN_ROW_TILE, FFN_FF_CHUNK = 1024, 512
HEADS_PER_STEP = (4, 2, 1)
MLA_KEY_CHUNK = 1024
LRU_ROWS_PER_STEP = 256


def _cparams(n_axes, vmem_mib):
    return pltpu.CompilerParams(dimension_semantics=("arbitrary",) * n_axes,
                                vmem_limit_bytes=int(min(vmem_mib, VMEM_LIMIT_CAP_MIB) * MIB))


def _largest_divisor(n, candidates):
    for c in candidates:
        if n % c == 0:
            return c
    return n


class _Stream:
    def __init__(self, nb, seq, mod0, shared_mod):
        self.nb, self.seq, self.rows, self.mod0, self.shared = nb, seq, nb * seq, mod0, shared_mod
        self.bm = _largest_divisor(self.rows if shared_mod else seq, ROW_TILES)

    def mod_index(self, row0):
        return self.mod0 if self.shared else self.mod0 + row0 // self.seq


def _norm_mod(x, g, shift, scale):
    ms = jnp.mean(x * x, axis=-1, keepdims=True)
    return (x * lax.rsqrt(ms + EPS)) * (g * (1.0 + scale)) + shift


def _rms(x, g):
    return (x * lax.rsqrt(jnp.mean(x * x, axis=-1, keepdims=True) + EPS)) * g


def _modulation_kernel(c_ref, w_ref, b_ref, o_ref):
    c = c_ref[...]
    sc = (c * jax.nn.sigmoid(c)).astype(BF16)
    o_ref[...] = jnp.dot(sc, w_ref[...].astype(BF16), preferred_element_type=F32) + b_ref[...]


def _modulation(cond, w_mod, b_mod):
    depth, d, n = w_mod.shape
    rows = cond.shape[0]
    bn = _largest_divisor(n, (512, 256, 128))
    return pl.pallas_call(
        _modulation_kernel,
        grid=(depth, n // bn),
        in_specs=[pl.BlockSpec((rows, d), lambda l, j: (0, 0)),
                  pl.BlockSpec((None, d, bn), lambda l, j: (l, 0, j)),
                  pl.BlockSpec((None, 1, bn), lambda l, j: (l, 0, j))],
        out_specs=pl.BlockSpec((None, rows, bn), lambda l, j: (l, 0, j)),
        out_shape=jax.ShapeDtypeStruct((depth, rows, n), F32),
        compiler_params=_cparams(2, 32),
        name="modulation",
    )(cond, w_mod, b_mod.reshape(depth, 1, n))


def _rope_tables(n_tokens, rot_dim, lead, width):
    t = jnp.arange(n_tokens)
    row = (t // GRID_W).astype(F32)
    col = (t % GRID_W).astype(F32)
    half = rot_dim // 2
    inv = ROPE_BASE ** (-jnp.arange(0, half, 2, dtype=F32) / half)
    ar = row[:, None] * inv
    ac = col[:, None] * inv
    ang = jnp.concatenate([ar, ar, ac, ac], axis=-1)
    cos, sin = jnp.cos(ang), jnp.sin(ang)
    first = (np.arange(rot_dim) % ROPE_GROUP) < ROPE_GROUP // 2
    sin_a = jnp.where(first, -sin, 0.0)
    sin_b = jnp.where(first, 0.0, sin)
    pad = ((0, 0), (lead, width - lead - rot_dim))
    return (jnp.pad(cos, pad, constant_values=1.0), jnp.pad(sin_a, pad), jnp.pad(sin_b, pad))


def _rope_apply(y, cos, sin_a, sin_b):
    shift = ROPE_GROUP // 2
    return y * cos + pltpu.roll(y, LANE - shift, 1) * sin_a + pltpu.roll(y, shift, 1) * sin_b


def _rotate_half_matrix():
    shift = ROPE_GROUP // 2
    src = lax.broadcasted_iota(jnp.int32, (LANE, LANE), 0)
    dst = lax.broadcasted_iota(jnp.int32, (LANE, LANE), 1)
    first = (dst % ROPE_GROUP) < shift
    return jnp.where(first & (src == dst + shift), -1.0,
                     jnp.where(jnp.logical_not(first) & (src == dst - shift), 1.0, 0.0)).astype(BF16)


def _rope_apply_mxu(y, cos, sin_a, sin_b, perm):
    rot = _dot_hi_lo(y, perm)
    return y * cos + rot * (sin_b - sin_a)


def _dot_hi_lo(y, m):
    hi = y.astype(BF16)
    lo = (y - hi.astype(F32)).astype(BF16)
    return jnp.dot(jnp.concatenate([hi, lo], axis=1), jnp.concatenate([m, m], axis=0), preferred_element_type=F32)


def _fill_lhs(x_ref, xs_ref, xn_ref, prologue, g_ref, sh_ref, sc_ref, row_chunk):
    bm = x_ref.shape[0]

    def chunk(r, carry):
        rows = pl.ds(pl.multiple_of(r * row_chunk, row_chunk), row_chunk)
        x = x_ref[rows, :].astype(F32)
        if prologue == "norm_mod":
            x = _norm_mod(x, g_ref[...], sh_ref[...], sc_ref[...])
        elif prologue == "norm":
            x = _rms(x, g_ref[...])
        if xn_ref is not None:
            xn_ref[rows, :] = x
        xs_ref[rows, :] = x.astype(BF16)
        return carry
    n_chunks = bm // row_chunk
    lax.fori_loop(0, n_chunks, chunk, 0, unroll=2 if n_chunks % 2 == 0 else 1)


def _head_norm_store(acc, o_ref, hg_ref, tabs, head_w, norm_div, col0, norm_cols, rope_tiles):
    bn = acc.shape[1]
    period = tabs[0].shape[1] if tabs is not None else LANE
    perm = _rotate_half_matrix() if tabs is not None and head_w >= LANE else None
    if head_w < LANE:
        head_of_row = lax.broadcasted_iota(jnp.int32, (LANE, LANE), 0) // head_w
        head_of_col = lax.broadcasted_iota(jnp.int32, (LANE, LANE), 1) // head_w
        same_head = jnp.where(head_of_row == head_of_col, 1.0, 0.0).astype(BF16)
    for s0 in range(0, bn, max(head_w, LANE)):
        normed = None if norm_cols is None else (col0 + s0 < norm_cols)
        tiles = [acc[:, s0 + k * LANE:s0 + (k + 1) * LANE] for k in range(max(head_w, LANE) // LANE)]
        if head_w >= LANE:
            sq = None
            for y in tiles:
                sq = y * y if sq is None else sq + y * y
            inv = lax.rsqrt(jnp.sum(sq, axis=-1, keepdims=True) / norm_div + EPS)
        else:
            inv = lax.rsqrt(_dot_hi_lo(tiles[0] * tiles[0], same_head) / norm_div + EPS)
        if normed is not None:
            inv = jnp.where(normed, inv, 1.0)
        for k, y in enumerate(tiles):
            c0 = s0 + k * LANE
            y = (y * inv) * hg_ref[:, c0:c0 + LANE]
            t0 = c0 % period
            if tabs is not None and rope_tiles[t0 // LANE]:
                tab = tuple(t[:, t0:t0 + LANE] for t in tabs)
                rotated = _rope_apply(y, *tab) if head_w < LANE else _rope_apply_mxu(y, *tab, perm)
                y = rotated if normed is None else jnp.where(normed, rotated, y)
            o_ref[:, c0:c0 + LANE] = y.astype(o_ref.dtype)


def _proj_kernel(*refs, prologue, emit_xn, epilogue, head_w, norm_div, norm_cols, rope, rope_tiles, row_chunk):
    it = iter(refs)
    x_ref = next(it)
    g_ref = next(it) if prologue is not None else None
    sh_ref, sc_ref = (next(it), next(it)) if prologue == "norm_mod" else (None, None)
    w_ref = next(it)
    if epilogue == "res":
        res_ref, gate_ref = next(it), next(it)
    if epilogue == "heads":
        hg_ref = next(it)
        tabs = (next(it), next(it), next(it)) if rope else None
    o_ref = next(it)
    xn_ref = next(it) if emit_xn else None
    xs_ref = next(it, None)
    j = pl.program_id(1)
    bn = o_ref.shape[1]

    if xs_ref is None:
        xs_ref = x_ref
    else:
        @pl.when(j == 0)
        def _():
            _fill_lhs(x_ref, xs_ref, xn_ref, prologue, g_ref, sh_ref, sc_ref, row_chunk)

    acc = jnp.dot(xs_ref[...], w_ref[...], preferred_element_type=F32)
    if epilogue == "res":
        o_ref[...] = res_ref[...] + gate_ref[...] * acc
    elif epilogue == "heads":
        _head_norm_store(acc, o_ref, hg_ref, tabs, head_w, norm_div, j * bn, norm_cols, rope_tiles)
    else:
        o_ref[...] = acc.astype(o_ref.dtype)


def _proj(x, w, st, *, x_block=None, norm_g=None, mod=None, res=None, gate=None, heads=None, emit_xn=False,
          out_dtype=F32, bn=None, x_time_major=False, out_time_major=False, name="proj"):
    k, n = w.shape
    time_major = x_time_major or out_time_major
    bm = min(st.bm, st.seq) if time_major else st.bm
    tiles_per_seq = st.seq // bm if st.seq % bm == 0 else None
    if time_major:
        assert tiles_per_seq is not None and x_block is None
    if x_time_major:
        assert x.shape == (st.seq, st.nb * k)
        kidx = 0
    else:
        kx, kidx = x_block if x_block is not None else (x.shape[1], 0)
        assert kx == k and x.shape[0] == st.rows
    rows = st.rows
    prologue = None if norm_g is None else ("norm_mod" if mod is not None else "norm")
    epilogue = "res" if res is not None else ("heads" if heads is not None else None)
    rope = heads is not None and heads.get("tabs") is not None
    if bn is None:
        unit = LANE
        if epilogue == "heads":
            unit = max(heads["head_w"], LANE, heads["tabs"][0].shape[1] if rope else LANE)
        cap = MAX_COL_TILE_RESIDUAL if epilogue == "res" and x.dtype != BF16 else MAX_COL_TILE
        bn = next((c for c in range(cap, unit - 1, -unit) if n % c == 0), n)
    mod_idx = lambda i: st.mod_index(i * bm)

    if x_time_major:
        in_specs = [pl.BlockSpec((bm, k), lambda i, j: (i % tiles_per_seq, i // tiles_per_seq))]
    else:
        in_specs = [pl.BlockSpec((bm, k), lambda i, j: (i, kidx))]
    args = [x]
    if prologue is not None:
        in_specs.append(pl.BlockSpec((1, k), lambda i, j: (0, 0)))
        args.append(norm_g)
    if prologue == "norm_mod":
        in_specs += [pl.BlockSpec((None, 1, k), lambda i, j: (mod_idx(i), 0, 0))] * 2
        args += list(mod)
    in_specs.append(pl.BlockSpec((k, bn), lambda i, j: (0, j)))
    args.append(w.astype(BF16))
    if epilogue == "res":
        in_specs += [pl.BlockSpec((bm, bn), lambda i, j: (i, j)),
                     pl.BlockSpec((None, 1, bn), lambda i, j: (mod_idx(i), 0, j))]
        args += [res, gate]
    head_w = norm_div = 0
    norm_cols = rope_tiles = None
    if epilogue == "heads":
        head_w, norm_div = heads["head_w"], heads["norm_div"]
        norm_cols = heads["norm_cols"] if heads["norm_cols"] < n else None
        assert bn % max(head_w, LANE) == 0 and heads["norm_cols"] % max(head_w, LANE) == 0
        in_specs.append(pl.BlockSpec((1, bn), lambda i, j: (0, j)))
        args.append(heads["gains"])
        if rope:
            period = heads["tabs"][0].shape[1]
            rope_tiles = heads["rope_tiles"]
            assert bn % period == 0 and tiles_per_seq is not None and len(rope_tiles) == period // LANE
            in_specs += [pl.BlockSpec((bm, period), lambda i, j: (i % tiles_per_seq, 0))] * 3
            args += list(heads["tabs"])
    if out_time_major:
        n_col_tiles = n // bn
        out_shape = [jax.ShapeDtypeStruct((st.seq, st.nb * n), out_dtype)]
        out_specs = [pl.BlockSpec((bm, bn), lambda i, j: (i % tiles_per_seq, (i // tiles_per_seq) * n_col_tiles + j))]
    else:
        out_shape = [jax.ShapeDtypeStruct((rows, n), out_dtype)]
        out_specs = [pl.BlockSpec((bm, bn), lambda i, j: (i, j))]
    if emit_xn:
        out_shape.append(jax.ShapeDtypeStruct((rows, k), F32))
        out_specs.append(pl.BlockSpec((bm, k), lambda i, j: (i, 0)))
    xbytes = x.dtype.itemsize
    vmem = (2 * bm * k * xbytes + bm * k * 2 + 2 * k * bn * 2 + (6 if epilogue == "res" else 4) * bm * bn * 4
            + (2 * bm * k * 4 if emit_xn else 0)) / MIB + 8
    kern = functools.partial(_proj_kernel, prologue=prologue, emit_xn=emit_xn, epilogue=epilogue, head_w=head_w,
                             norm_div=norm_div, norm_cols=norm_cols, rope=rope, rope_tiles=rope_tiles,
                             row_chunk=min(bm, 128))
    direct_lhs = prologue is None and x.dtype == BF16 and not emit_xn
    out = pl.pallas_call(
        kern,
        grid=(rows // bm, n // bn),
        in_specs=in_specs,
        out_specs=out_specs,
        out_shape=out_shape,
        scratch_shapes=[] if direct_lhs else [pltpu.VMEM((bm, k), BF16)],
        compiler_params=_cparams(2, vmem),
        name=name,
    )(*args)
    return out if emit_xn else out[0]


def _ffn_kernel(xp_ref, x_ref, xn_ref, g_ref, sh_ref, sc_ref, gate_ref, wa_ref, wb_ref, cw_ref, cb_ref,
                wo_ref, o_ref, h_ref, *, bm, seq, row_chunk):
    i = pl.program_id(0)
    c = pl.program_id(1)
    n_chunks = pl.num_programs(1)

    @pl.when(c == 0)
    def _():
        g, sh, sc = g_ref[...], sh_ref[...], sc_ref[...]
        h_ref[0:HALO, :] = _norm_mod(xp_ref[...], g, sh, sc).astype(BF16)
        h_ref[HALO + bm:, :] = _norm_mod(xn_ref[...], g, sh, sc).astype(BF16)

        def chunk(r, carry):
            src = pl.ds(pl.multiple_of(r * row_chunk, row_chunk), row_chunk)
            dst = pl.ds(pl.multiple_of(HALO + r * row_chunk, HALO), row_chunk)
            h_ref[dst, :] = _norm_mod(x_ref[src, :], g, sh, sc).astype(BF16)
            return carry
        n_row_chunks = bm // row_chunk
        lax.fori_loop(0, n_row_chunks, chunk, 0, unroll=2 if n_row_chunks % 2 == 0 else 1)
        o_ref[...] = jnp.zeros_like(o_ref)

    ua = jnp.dot(h_ref[...], wa_ref[...], preferred_element_type=F32)
    ub = jnp.dot(h_ref[HALO:HALO + bm, :], wb_ref[...], preferred_element_type=F32)
    n_all = bm + 2 * HALO
    u_prev = pltpu.roll(ua, 1, 0)[HALO:HALO + bm]
    u_next = pltpu.roll(ua, n_all - 1, 0)[HALO:HALO + bm]
    u_mid = ua[HALO:HALO + bm]
    pos = jnp.bitwise_and(i * bm + lax.broadcasted_iota(jnp.int32, (bm, 1), 0), seq - 1)
    u_prev = jnp.where(pos == 0, 0.0, u_prev)
    u_next = jnp.where(pos == seq - 1, 0.0, u_next)
    cw = cw_ref[...]
    a = cb_ref[...] + u_prev * cw[0:1] + u_mid * cw[1:2] + u_next * cw[2:3]
    gated = ((a * jax.nn.sigmoid(a)) * ub).astype(BF16)
    o_ref[...] += jnp.dot(gated, wo_ref[...], preferred_element_type=F32)

    @pl.when(c == n_chunks - 1)
    def _():
        o_ref[...] = x_ref[...] + gate_ref[...] * o_ref[...]


def _conv_ffn(x, st, g, shift, scale, gate, w_in, conv_w, conv_b, w_out, bm=FFN_ROW_TILE, ck=FFN_FF_CHUNK):
    m, d = x.shape
    d_ff = w_out.shape[0]
    bm = min(st.bm, bm)
    ck = _largest_divisor(d_ff, tuple(c for c in (512, 256, 128) if c <= ck))
    n_chunks = d_ff // ck
    n_halo_blocks = m // HALO
    assert st.seq & (st.seq - 1) == 0 and conv_w.shape[0] == 3
    mod_idx = lambda i: st.mod_index(i * bm)
    kern = functools.partial(_ffn_kernel, bm=bm, seq=st.seq, row_chunk=min(bm, 128))
    vmem = (4 * bm * d * 4 + (bm + 2 * HALO) * d * 2 + 6 * d * ck * 2 + 5 * (bm + 2 * HALO) * ck * 4) / MIB + 4
    return pl.pallas_call(
        kern,
        grid=(m // bm, n_chunks),
        in_specs=[
            pl.BlockSpec((HALO, d), lambda i, c: (jnp.maximum(i * (bm // HALO) - 1, 0), 0)),
            pl.BlockSpec((bm, d), lambda i, c: (i, 0)),
            pl.BlockSpec((HALO, d), lambda i, c: (jnp.minimum((i + 1) * (bm // HALO), n_halo_blocks - 1), 0)),
            pl.BlockSpec((1, d), lambda i, c: (0, 0)),
            pl.BlockSpec((None, 1, d), lambda i, c: (mod_idx(i), 0, 0)),
            pl.BlockSpec((None, 1, d), lambda i, c: (mod_idx(i), 0, 0)),
            pl.BlockSpec((None, 1, d), lambda i, c: (mod_idx(i), 0, 0)),
            pl.BlockSpec((d, ck), lambda i, c: (0, c)),
            pl.BlockSpec((d, ck), lambda i, c: (0, n_chunks + c)),
            pl.BlockSpec((conv_w.shape[0], ck), lambda i, c: (0, c)),
            pl.BlockSpec((1, ck), lambda i, c: (0, c)),
            pl.BlockSpec((ck, d), lambda i, c: (c, 0)),
        ],
        out_specs=pl.BlockSpec((bm, d), lambda i, c: (i, 0)),
        out_shape=jax.ShapeDtypeStruct((m, d), F32),
        scratch_shapes=[pltpu.VMEM((bm + 2 * HALO, d), BF16)],
        compiler_params=_cparams(2, vmem),
        name="conv_ffn",
    )(x, x, x, g, shift, scale, gate, w_in, w_in, conv_w, conv_b.reshape(1, d_ff), w_out)


def _qk(q, k):
    return lax.dot_general(q, k, (((1,), (1,)), ((), ())), preferred_element_type=F32)


def _attend(scores, values, sink=None):
    m = None
    for s in scores:
        mi = jnp.max(s, axis=-1, keepdims=True)
        m = mi if m is None else jnp.maximum(m, mi)
    if sink is not None:
        m = jnp.maximum(m, sink)
    es = [jnp.exp2(s - m) for s in scores]
    den = None
    for e in es:
        di = jnp.sum(e, axis=-1, keepdims=True)
        den = di if den is None else den + di
    if sink is not None:
        den = den + jnp.exp2(sink - m)
    out = None
    for e, v in zip(es, values):
        oi = jnp.dot(e.astype(BF16), v, preferred_element_type=F32)
        out = oi if out is None else out + oi
    return out * (1.0 / den)


def _ctx_attn_kernel(q_ref, k_ref, v_ref, *outs, heads, dq, dv, emit_kv):
    o_ref = outs[0]
    for h in range(heads):
        q = q_ref[:, h * dq:(h + 1) * dq].astype(BF16)
        k = k_ref[:, h * dq:(h + 1) * dq]
        v = v_ref[:, h * dv:(h + 1) * dv]
        if emit_kv:
            outs[1][h] = k.astype(F32)
            outs[2][h] = v.astype(F32)
        o = _attend([_qk(q, k.astype(BF16))], [v.astype(BF16)])
        o_ref[:, h * dv:(h + 1) * dv] = o.astype(o_ref.dtype)


def _ctx_attention(qm, km, vm, st, *, n_heads, dq, dv, q_col, k_col, v_col, emit_kv=False):
    hb = _largest_divisor(n_heads, HEADS_PER_STEP)
    s = st.seq
    assert q_col % (hb * dq) == 0 and k_col % (hb * dq) == 0 and v_col % (hb * dv) == 0
    qo, ko, vo = q_col // (hb * dq), k_col // (hb * dq), v_col // (hb * dv)
    out_shape = [jax.ShapeDtypeStruct((st.rows, n_heads * dv), BF16)]
    out_specs = [pl.BlockSpec((s, hb * dv), lambda b, g: (b, g))]
    if emit_kv:
        out_shape += [jax.ShapeDtypeStruct((st.nb, n_heads, s, dq), F32),
                      jax.ShapeDtypeStruct((st.nb, n_heads, s, dv), F32)]
        out_specs += [pl.BlockSpec((None, hb, s, dq), lambda b, g: (b, g, 0, 0)),
                      pl.BlockSpec((None, hb, s, dv), lambda b, g: (b, g, 0, 0))]
    out = pl.pallas_call(
        functools.partial(_ctx_attn_kernel, heads=hb, dq=dq, dv=dv, emit_kv=emit_kv),
        grid=(st.nb, n_heads // hb),
        in_specs=[pl.BlockSpec((s, hb * dq), lambda b, g: (b, qo + g)),
                  pl.BlockSpec((s, hb * dq), lambda b, g: (b, ko + g)),
                  pl.BlockSpec((s, hb * dv), lambda b, g: (b, vo + g))],
        out_specs=out_specs,
        out_shape=out_shape,
        compiler_params=_cparams(2, 32),
        name="ctx_attention",
    )(qm, km, vm)
    return out if emit_kv else out[0]


def _nat_kernel(q_ref, k_ref, v_ref, kc_ref, vc_ref, bias_ref, o_ref, *, key_rows, rows, heads, dh):
    i = pl.program_id(2)
    n_keys = key_rows * GRID_W
    first_row = jnp.clip(i * NA_Q_ROWS - NA_WIN_ROWS // 2, 0, rows - key_rows)
    start = pl.multiple_of(first_row * GRID_W, GRID_W * 4)
    for h in range(heads):
        lanes = slice(h * dh, (h + 1) * dh)
        q = q_ref[:, lanes]
        k = k_ref[pl.ds(start, n_keys), lanes]
        v = v_ref[pl.ds(start, n_keys), lanes]
        s_loc = _qk(q, k) + bias_ref[h]
        s_ctx = _qk(q, kc_ref[h].astype(BF16))
        o_ref[:, lanes] = _attend([s_loc, s_ctx], [v, vc_ref[h].astype(BF16)]).astype(o_ref.dtype)


def _nat_bias(rpb, rows):
    n_blocks = rows // NA_Q_ROWS
    key_rows = min(NA_K_ROWS, rows)
    wr = min(NA_WIN_ROWS, rows)
    reps = [0, min(1, n_blocks - 1), n_blocks - 1]
    heads = rpb.shape[0]
    nq, nk = NA_Q_ROWS * GRID_W, key_rows * GRID_W
    shape = (NA_Q_ROWS, GRID_W, key_rows, GRID_W)
    qc = np.arange(GRID_W)
    cstart = np.clip(qc - NA_WIN_COLS // 2, 0, GRID_W - NA_WIN_COLS)
    col_ok = (qc[None, :] >= cstart[:, None]) & (qc[None, :] < cstart[:, None] + NA_WIN_COLS)
    rp = jnp.pad(rpb.astype(F32) * LOG2E,
                 ((0, 0), (key_rows, key_rows), (GRID_W - NA_WIN_COLS, GRID_W - NA_WIN_COLS)))
    row_slabs, mask_l = [], []
    for i in reps:
        ks = int(np.clip(i * NA_Q_ROWS - NA_WIN_ROWS // 2, 0, rows - key_rows))
        r = i * NA_Q_ROWS + np.arange(NA_Q_ROWS)
        rs = np.clip(r - wr // 2, 0, rows - wr)
        kr = ks + np.arange(key_rows)
        row_ok = (kr[None, :] >= rs[:, None]) & (kr[None, :] < rs[:, None] + wr)
        for rq in range(NA_Q_ROWS):
            first = ks - int(r[rq]) + NA_WIN_ROWS - 1 + key_rows
            assert 0 <= first and first + key_rows <= rp.shape[1]
            row_slabs.append(rp[:, first:first + key_rows, :])
        mask_l.append(np.broadcast_to(row_ok[:, None, :, None] & col_ok[None, :, None, :], shape).reshape(nq, nk))
    slab = jnp.stack(row_slabs, axis=1).reshape(heads, len(reps), NA_Q_ROWS, key_rows, 2 * GRID_W - 1)
    toep = jnp.stack([slab[..., GRID_W - 1 - c:2 * GRID_W - 1 - c] for c in range(GRID_W)], axis=3)
    bias = toep.reshape(heads, len(reps), nq, nk)
    return jnp.where(jnp.asarray(np.stack(mask_l))[None], bias, NEG)


def _nat_attention(qkv, st, cache_k, cache_v, j, rpb, dh):
    heads = rpb.shape[0]
    n = st.seq
    p = cache_k.shape[3]
    rows = n // GRID_W
    assert rows % NA_Q_ROWS == 0 and rows >= NA_K_ROWS and dh % LANE == 0
    n_blocks = rows // NA_Q_ROWS
    key_rows = min(NA_K_ROWS, rows)
    nq, nk = NA_Q_ROWS * GRID_W, key_rows * GRID_W
    bias = _nat_bias(rpb, rows)
    btype = lambda i: jnp.where(i == 0, 0, jnp.where(i == n_blocks - 1, 2, 1))
    hb = _largest_divisor(heads, HEADS_PER_STEP)
    hg = heads // hb
    kern = functools.partial(_nat_kernel, key_rows=key_rows, rows=rows, heads=hb, dh=dh)
    return pl.pallas_call(
        kern,
        grid=(st.nb, hg, n_blocks),
        in_specs=[pl.BlockSpec((nq, hb * dh), lambda b, h, i: (b * n_blocks + i, h)),
                  pl.BlockSpec((n, hb * dh), lambda b, h, i: (b, hg + h)),
                  pl.BlockSpec((n, hb * dh), lambda b, h, i: (b, 2 * hg + h)),
                  pl.BlockSpec((None, None, hb, p, dh), lambda b, h, i: (b, j, h, 0, 0)),
                  pl.BlockSpec((None, None, hb, p, dh), lambda b, h, i: (b, j, h, 0, 0)),
                  pl.BlockSpec((hb, None, nq, nk), lambda b, h, i: (h, btype(i), 0, 0))],
        out_specs=pl.BlockSpec((nq, hb * dh), lambda b, h, i: (b * n_blocks + i, h)),
        out_shape=jax.ShapeDtypeStruct((st.rows, heads * dh), BF16),
        compiler_params=_cparams(3, 56),
        name="nat_attention",
    )(qkv, qkv, qkv, cache_k, cache_v, bias)


def _joint_dense_kernel(q_ref, k_ref, v_ref, kc_ref, vc_ref, o_ref, *, chunk):
    q = q_ref[...]
    n = k_ref.shape[0]
    pieces = [(k_ref, v_ref, c0, min(chunk, n - c0)) for c0 in range(0, n, chunk)]
    pieces.append((kc_ref, vc_ref, 0, kc_ref.shape[0]))
    m = den = acc = None
    for kr, vr, c0, size in pieces:
        s = _qk(q, kr[c0:c0 + size, :])
        mc = jnp.max(s, axis=-1, keepdims=True)
        m_new = mc if m is None else jnp.maximum(m, mc)
        e = jnp.exp2(s - m_new)
        dc = jnp.sum(e, axis=-1, keepdims=True)
        pv = jnp.dot(e.astype(BF16), vr[c0:c0 + size, :], preferred_element_type=F32)
        if m is None:
            den, acc = dc, pv
        else:
            alpha = jnp.exp2(m - m_new)
            den, acc = alpha * den + dc, alpha * acc + pv
        m = m_new
    o_ref[...] = (acc * (1.0 / den)).astype(o_ref.dtype)


def _joint_dense_attention(qm, km, vm, kcm, vcm, st, p, *, n_heads, dq, dv):
    n = st.seq
    bq = _largest_divisor(n, (1024, 512, 256, 128, 64, 32, 16))
    nqb = n // bq
    return pl.pallas_call(
        functools.partial(_joint_dense_kernel, chunk=MLA_KEY_CHUNK),
        grid=(st.nb, n_heads, nqb),
        in_specs=[pl.BlockSpec((bq, dq), lambda b, h, i: (b * nqb + i, h)),
                  pl.BlockSpec((n, dq), lambda b, h, i: (b, h)),
                  pl.BlockSpec((n, dv), lambda b, h, i: (b, h)),
                  pl.BlockSpec((p, dq), lambda b, h, i: (b, h)),
                  pl.BlockSpec((p, dv), lambda b, h, i: (b, h))],
        out_specs=pl.BlockSpec((bq, dv), lambda b, h, i: (b * nqb + i, h)),
        out_shape=jax.ShapeDtypeStruct((st.rows, n_heads * dv), BF16),
        compiler_params=_cparams(3, 48),
        name="mla_attention",
    )(qm, km, vm, kcm, vcm)


def _both_halves(x, s):
    x = x.astype(F32)
    low = lax.broadcasted_iota(jnp.int32, (1, LANE), 1) < LANE // 2
    keep = low if s == 0 else jnp.logical_not(low)
    return jnp.where(keep, x, pltpu.roll(x, LANE // 2, 1)).astype(BF16)


def _swa_step(sinks_ref, pair, q_ref, k, v, kc, vc, o_ref, *, dh, groups, local_bias):
    kv_per_step = LANE // dh
    assert kv_per_step == 2 and groups % 2 == 0
    rows = q_ref.shape[0]
    low = lax.broadcasted_iota(jnp.int32, (1, LANE), 1) < dh
    row_group = lax.broadcasted_iota(jnp.int32, (groups * rows, 1), 0) // rows
    for s in range(kv_per_step):
        kd, vd = _both_halves(k, s), _both_halves(v, s)
        q_parts = []
        for g in range(groups):
            c0 = ((s * groups + g) * dh // LANE) * LANE
            tile = q_ref[:, c0:c0 + LANE].astype(BF16)
            q_parts.append(jnp.where(low if g % 2 == 0 else jnp.logical_not(low), tile, jnp.zeros_like(tile)))
        q = jnp.concatenate(q_parts, axis=0)
        sink = jnp.zeros((groups * rows, 1), F32)
        for g in range(groups):
            sink = jnp.where(row_group == g, sinks_ref[(pair * kv_per_step + s) * groups + g], sink)
        s_loc = _qk(q, kd)
        if local_bias is not None:
            s_loc = s_loc + jnp.concatenate([local_bias] * groups, axis=0)
        if kc is not None:
            kcd = jnp.concatenate([kc[s], kc[s]], axis=-1).astype(BF16)
            vcd = jnp.concatenate([vc[s], vc[s]], axis=-1).astype(BF16)
            out = _attend([s_loc, _qk(q, kcd)], [vd, vcd], sink)
        else:
            out = _attend([s_loc], [vd], sink)
        for g in range(0, groups, 2):
            c0 = (s * groups + g) * dh
            o_ref[:, c0:c0 + LANE] = jnp.where(low, out[g * rows:(g + 1) * rows],
                                               out[(g + 1) * rows:(g + 2) * rows]).astype(o_ref.dtype)


def _swa_ctx_kernel(sinks_ref, q_ref, k_ref, v_ref, o_ref, ko_ref, vo_ref, *, dh, groups):
    pair = pl.program_id(1)
    k, v = k_ref[...], v_ref[...]
    for s in range(LANE // dh):
        ko_ref[s] = k[:, s * dh:(s + 1) * dh].astype(F32)
        vo_ref[s] = v[:, s * dh:(s + 1) * dh].astype(F32)
    _swa_step(sinks_ref, pair, q_ref, k, v, None, None, o_ref, dh=dh, groups=groups, local_bias=None)


def _swa_lat_kernel(sinks_ref, q_ref, k_ref, v_ref, kc_ref, vc_ref, o_ref, *, dh, groups, n):
    pair = pl.program_id(1)
    blk = pl.program_id(2)
    n_keys = min(3 * SWA_BLOCK, n)
    start = pl.multiple_of(jnp.clip((blk - 1) * SWA_BLOCK, 0, n - n_keys), SWA_BLOCK)
    k = k_ref[pl.ds(start, n_keys), :]
    v = v_ref[pl.ds(start, n_keys), :]
    qpos = blk * SWA_BLOCK + lax.broadcasted_iota(jnp.int32, (SWA_BLOCK, 1), 0)
    kpos = start + lax.broadcasted_iota(jnp.int32, (1, n_keys), 1)
    bias = jnp.where(jnp.abs(qpos - kpos) <= SWA_WINDOW, 0.0, NEG)
    _swa_step(sinks_ref, pair, q_ref, k, v, kc_ref, vc_ref, o_ref, dh=dh, groups=groups, local_bias=bias)


def _swa_attention(qkv, st, sinks, *, heads, kvh, dh, cache=None):
    groups = heads // kvh
    kv_per_step = LANE // dh
    assert LANE % dh == 0 and kvh % kv_per_step == 0 and groups % kv_per_step == 0
    pairs = kvh // kv_per_step
    qw = kv_per_step * groups * dh
    k_blk = heads * dh // LANE
    v_blk = (heads + kvh) * dh // LANE
    n = st.seq
    common = dict(dh=dh, groups=groups)
    smem = pl.BlockSpec(memory_space=pltpu.SMEM)
    if cache is None:
        out = pl.pallas_call(
            functools.partial(_swa_ctx_kernel, **common),
            grid=(st.nb, pairs),
            in_specs=[smem,
                      pl.BlockSpec((n, qw), lambda b, c: (b, c)),
                      pl.BlockSpec((n, LANE), lambda b, c: (b, k_blk + c)),
                      pl.BlockSpec((n, LANE), lambda b, c: (b, v_blk + c))],
            out_specs=[pl.BlockSpec((n, qw), lambda b, c: (b, c)),
                       pl.BlockSpec((None, kv_per_step, n, dh), lambda b, c: (b, c, 0, 0)),
                       pl.BlockSpec((None, kv_per_step, n, dh), lambda b, c: (b, c, 0, 0))],
            out_shape=[jax.ShapeDtypeStruct((st.rows, heads * dh), BF16),
                       jax.ShapeDtypeStruct((st.nb, kvh, n, dh), F32),
                       jax.ShapeDtypeStruct((st.nb, kvh, n, dh), F32)],
            compiler_params=_cparams(2, 32),
            name="swa_ctx_attention",
        )(sinks, qkv, qkv, qkv)
        return out
    cache_k, cache_v, j = cache
    p = cache_k.shape[3]
    nblk = n // SWA_BLOCK
    assert n % SWA_BLOCK == 0
    return pl.pallas_call(
        functools.partial(_swa_lat_kernel, n=n, **common),
        grid=(st.nb, pairs, nblk),
        in_specs=[smem,
                  pl.BlockSpec((SWA_BLOCK, qw), lambda b, c, i: (b * nblk + i, c)),
                  pl.BlockSpec((n, LANE), lambda b, c, i: (b, k_blk + c)),
                  pl.BlockSpec((n, LANE), lambda b, c, i: (b, v_blk + c)),
                  pl.BlockSpec((None, None, kv_per_step, p, dh), lambda b, c, i: (b, j, c, 0, 0)),
                  pl.BlockSpec((None, None, kv_per_step, p, dh), lambda b, c, i: (b, j, c, 0, 0))],
        out_specs=pl.BlockSpec((SWA_BLOCK, qw), lambda b, c, i: (b * nblk + i, c)),
        out_shape=jax.ShapeDtypeStruct((st.rows, heads * dh), BF16),
        compiler_params=_cparams(3, 32),
        name="swa_attention",
    )(sinks, qkv, qkv, qkv, cache_k, cache_v)


def _mla_kv_kernel(*refs, norm, emit_xn, rope, row_chunk, norm_div):
    it = iter(refs)
    x_ref = next(it)
    g_ref = next(it) if norm else None
    w_ref, kr_ref, g1_ref, g2_ref = next(it), next(it), next(it), next(it)
    tabs = (next(it), next(it), next(it)) if rope else None
    k_ref, v_ref = next(it), next(it)
    xn_ref = next(it) if emit_xn else None
    xs_ref = next(it)

    @pl.when(pl.program_id(1) == 0)
    def _():
        _fill_lhs(x_ref, xs_ref, xn_ref, "norm" if norm else None, g_ref, None, None, row_chunk)

    acc = jnp.dot(xs_ref[...], w_ref[...], preferred_element_type=F32)
    kr = kr_ref[...]
    kr_ssq = jnp.sum(kr * kr, axis=-1, keepdims=True)
    shared = kr * g2_ref[...]
    if rope:
        shared = _rope_apply(shared, *(t[...] for t in tabs))
    for h in range(acc.shape[1] // (2 * LANE)):
        nope = acc[:, 2 * h * LANE:(2 * h + 1) * LANE]
        inv = lax.rsqrt((jnp.sum(nope * nope, axis=-1, keepdims=True) + kr_ssq) / norm_div + EPS)
        k_ref[:, 2 * h * LANE:(2 * h + 1) * LANE] = ((nope * inv) * g1_ref[...]).astype(k_ref.dtype)
        k_ref[:, (2 * h + 1) * LANE:(2 * h + 2) * LANE] = (shared * inv).astype(k_ref.dtype)
        v_ref[:, h * LANE:(h + 1) * LANE] = acc[:, (2 * h + 1) * LANE:(2 * h + 2) * LANE].astype(v_ref.dtype)


def _mla_kv(x, x_block, w_ukv, kr, kr_block, g_kva, g1, g2, tabs, st, *, n_heads, norm_div, emit_xn, name):
    rows = x.shape[0]
    k, n = w_ukv.shape
    head_n = n // n_heads
    assert head_n == 2 * LANE, "nope and value widths must both be one lane tile"
    kx, kidx = x_block
    krw, kridx = kr_block
    assert kx == k and krw == LANE
    bm = st.bm
    norm = g_kva is not None
    rope = tabs is not None
    in_specs = [pl.BlockSpec((bm, k), lambda i, h: (i, kidx))]
    args = [x]
    if norm:
        in_specs.append(pl.BlockSpec((1, k), lambda i, h: (0, 0)))
        args.append(g_kva)
    hb = _largest_divisor(n_heads, HEADS_PER_STEP)
    in_specs += [pl.BlockSpec((k, hb * head_n), lambda i, h: (0, h)),
                 pl.BlockSpec((bm, LANE), lambda i, h: (i, kridx)),
                 pl.BlockSpec((1, LANE), lambda i, h: (0, 0)),
                 pl.BlockSpec((1, LANE), lambda i, h: (0, 0))]
    args += [w_ukv, kr, g1, g2]
    if rope:
        tiles_per_seq = st.seq // bm
        in_specs += [pl.BlockSpec((bm, LANE), lambda i, h: (i % tiles_per_seq, 0))] * 3
        args += list(tabs)
    out_shape = [jax.ShapeDtypeStruct((rows, n_heads * 2 * LANE), BF16),
                 jax.ShapeDtypeStruct((rows, n_heads * LANE), BF16)]
    out_specs = [pl.BlockSpec((bm, hb * 2 * LANE), lambda i, h: (i, h)),
                 pl.BlockSpec((bm, hb * LANE), lambda i, h: (i, h))]
    if emit_xn:
        out_shape.append(jax.ShapeDtypeStruct((rows, k), F32))
        out_specs.append(pl.BlockSpec((bm, k), lambda i, h: (i, 0)))
    kern = functools.partial(_mla_kv_kernel, norm=norm, emit_xn=emit_xn, rope=rope, row_chunk=min(bm, 128),
                             norm_div=norm_div)
    return pl.pallas_call(
        kern,
        grid=(rows // bm, n_heads // hb),
        in_specs=in_specs,
        out_specs=out_specs,
        out_shape=out_shape,
        scratch_shapes=[pltpu.VMEM((bm, k), BF16)],
        compiler_params=_cparams(2, 40),
        name=name,
    )(*args)


def _band_plan(width, block):
    n_tiles = width // LANE
    lo = [((t * LANE) // block) * block for t in range(n_tiles)]
    hi = [(((t + 1) * LANE - 1) // block + 1) * block for t in range(n_tiles)]
    start = [(l // LANE) * LANE for l in lo]
    kb = max(-(-(h - s) // LANE) * LANE for h, s in zip(hi, start))
    kb = min(kb, width)
    start = [min(s, width - kb) for s in start]
    return start, kb


def _band_weights(w, width, block, start, kb):
    n_tiles = width // LANE
    wb = w.astype(BF16)
    tiles = []
    for t in range(n_tiles):
        pieces = []
        col = t * LANE
        while col < (t + 1) * LANE:
            blk = col // block
            col_end = min((blk + 1) * block, (t + 1) * LANE)
            sub = wb[blk, :, col - blk * block:col_end - blk * block]
            top = blk * block - start[t]
            pieces.append(jnp.pad(sub, ((top, kb - top - block), (0, 0))))
            col = col_end
        tiles.append(jnp.concatenate(pieces, axis=1))
    return jnp.stack(tiles)


def _gelu_tanh(x):
    cdf = 0.5 * (1.0 + jnp.tanh(np.float32(np.sqrt(2.0 / np.pi)) * (x + 0.044715 * (x * x * x))))
    return x * cdf


def _lru_pass_kernel(*refs, reverse, starts, kb, bt, nb, taps):
    left = taps // 2
    right = taps - 1 - left
    it = iter(refs)
    if reverse:
        conv_ref = next(it)
    else:
        xp_ref, x_ref = next(it), next(it)
        xn_ref = next(it) if right > 0 else None
        cw_ref, cb_ref = next(it), next(it)
    wa_ref, wi_ref, ba_ref, bi_ref, lam_ref, h0_ref = (next(it) for _ in range(6))
    hsf_ref, gate_ref = (next(it), next(it)) if reverse else (None, None)
    out_ref, ht_ref = next(it), next(it)
    conv_out_ref = None if reverse else next(it)
    xc_s, xb_s, a_s, bx_s, carry = next(it), next(it), next(it), next(it), next(it)

    step = pl.program_id(0)
    n_steps = pl.num_programs(0)
    n_tiles = len(starts)
    rows = nb * bt

    @pl.when(step == 0)
    def _():
        carry[...] = h0_ref[...]

    for t in range(n_tiles):
        lanes = slice(t * LANE, (t + 1) * LANE)
        if reverse:
            acc = conv_ref[:, :, lanes].reshape(rows, LANE)
        else:
            parts = [jnp.where(step > 0, xp_ref[:, :, lanes], 0.0), x_ref[:, :, lanes]]
            if right > 0:
                parts.append(jnp.where(step < n_steps - 1, xn_ref[:, :, lanes], 0.0))
            full = jnp.concatenate(parts, axis=0)
            acc = jnp.broadcast_to(cb_ref[:, lanes], (bt, nb, LANE))
            for k in range(taps):
                acc = acc + full[k:k + bt] * cw_ref[k:k + 1, lanes]
            conv_out_ref[:, :, lanes] = acc
            acc = acc.reshape(rows, LANE)
        xc_s[:, lanes] = acc
        xb_s[:, lanes] = acc.astype(BF16)

    neg_lam = -lam_ref[...]
    softplus = jnp.maximum(neg_lam, 0.0) + jnp.log1p(jnp.exp(-jnp.abs(neg_lam)))
    half_rate = (-0.5 * LRU_C) * softplus
    half_ba, half_bi = 0.5 * ba_ref[...], 0.5 * bi_ref[...]
    for t in range(n_tiles):
        lanes = slice(t * LANE, (t + 1) * LANE)
        xw = xb_s[:, starts[t]:starts[t] + kb]
        tanh_a = jnp.tanh(jnp.dot(xw, wa_ref[t], preferred_element_type=F32) + half_ba[:, lanes])
        tanh_i = jnp.tanh(jnp.dot(xw, wi_ref[t], preferred_element_type=F32) + half_bi[:, lanes])
        log_a = half_rate[:, lanes] * tanh_a + half_rate[:, lanes]
        a = jnp.exp(log_a)
        half_x = 0.5 * xc_s[:, lanes]
        a_s[:, lanes] = a
        bx_s[:, lanes] = jnp.sqrt(-jnp.tanh(log_a) * (a * a + 1.0)) * (half_x * tanh_i + half_x)

    h = carry[...]
    for s in range(bt):
        ts = (bt - 1 - s) if reverse else s
        slab = slice(ts * nb, (ts + 1) * nb)
        h = a_s[slab, :] * h + bx_s[slab, :]
        a_s[slab, :] = h
    carry[...] = h
    ht_ref[...] = h
    hs = a_s[...].reshape(bt, nb, a_s.shape[1])
    if reverse:
        out_ref[...] = (_gelu_tanh(gate_ref[...]) * (hsf_ref[...] + hs)).astype(out_ref.dtype)
    else:
        out_ref[...] = hs


def _lru_pass(u, st, conv_w, conv_b, wa, wi, b_a, b_i, lam, h0, starts, kb, *, reverse, fwd=None):
    c = conv_w.shape[1]
    taps = conv_w.shape[0]
    left, right = taps // 2, taps - 1 - taps // 2
    nb, seq = st.nb, st.seq
    bt = min(max(LRU_ROWS_PER_STEP // nb, SUBLANE), seq)
    assert seq % bt == 0 and c % LANE == 0 and left > 0 and bt % left == 0 and (right == 0 or bt % right == 0)
    nt = seq // bt
    n_tiles = c // LANE
    u3 = u.reshape(seq, nb, 2 * c)
    tmap = (lambda s: nt - 1 - s) if reverse else (lambda s: s)
    full = lambda *shape: pl.BlockSpec(shape, lambda s: (0,) * len(shape))
    tile_spec = lambda col: pl.BlockSpec((bt, nb, c), lambda s: (tmap(s), 0, col))
    if reverse:
        in_specs, args = [tile_spec(0)], [fwd[1]]
    else:
        in_specs = [pl.BlockSpec((left, nb, c), lambda s: (jnp.maximum(s * (bt // left) - 1, 0), 0, 0)), tile_spec(0)]
        args = [u3, u3]
        if right > 0:
            in_specs.append(pl.BlockSpec((right, nb, c),
                                         lambda s: (jnp.minimum((s + 1) * (bt // right), seq // right - 1), 0, 0)))
            args.append(u3)
        in_specs += [full(taps, c), full(1, c)]
        args += [conv_w, conv_b.reshape(1, c)]
    in_specs += [full(n_tiles, kb, LANE), full(n_tiles, kb, LANE), full(1, c), full(1, c), full(1, c), full(nb, c)]
    args += [wa, wi, b_a.reshape(1, c), b_i.reshape(1, c), lam.reshape(1, c), h0]
    out_specs = [tile_spec(0), pl.BlockSpec((nb, c), lambda s: (0, 0))]
    out_shape = [jax.ShapeDtypeStruct((seq, nb, c), F32), jax.ShapeDtypeStruct((nb, c), F32)]
    if reverse:
        in_specs += [tile_spec(0), tile_spec(1)]
        args += [fwd[0], u3]
    else:
        out_specs.append(tile_spec(0))
        out_shape.append(jax.ShapeDtypeStruct((seq, nb, c), F32))
    kern = functools.partial(_lru_pass_kernel, reverse=reverse, starts=tuple(starts), kb=kb, bt=bt, nb=nb, taps=taps)
    blk = nb * bt * c * 4 / MIB
    vmem = (2 + 2 + 3 + (4 if reverse else 2) + 4) * blk + 4 * n_tiles * kb * LANE * 2 / MIB + 8
    return pl.pallas_call(
        kern,
        grid=(nt,),
        in_specs=in_specs,
        out_specs=out_specs,
        out_shape=out_shape,
        scratch_shapes=[pltpu.VMEM((nb * bt, c), F32), pltpu.VMEM((nb * bt, c), BF16),
                        pltpu.VMEM((nb * bt, c), F32), pltpu.VMEM((nb * bt, c), F32), pltpu.VMEM((nb, c), F32)],
        compiler_params=_cparams(1, vmem),
        name="lru_bwd" if reverse else "lru_fwd",
    )(*args)


def _mixer_nat(xs, streams, mods, cache_k, cache_v, j, w_qkv, g_mix, g_q, g_k, rpb, w_o):
    heads, dh = rpb.shape[0], g_q.shape[0]
    w_qkv, w_o = w_qkv.astype(BF16), w_o.astype(BF16)
    gains = jnp.concatenate([jnp.tile(g_q * (dh ** -0.5 * LOG2E), heads), jnp.tile(g_k, heads),
                             jnp.ones((heads * dh,), F32)])[None]
    spec = dict(head_w=dh, norm_div=dh, norm_cols=2 * heads * dh, gains=gains)
    new_x, extra = [], None
    for x, st in zip(xs, streams):
        latent = not st.shared
        qkv = _proj(x, w_qkv, st, norm_g=g_mix, mod=(mods[0], mods[1]), heads=spec,
                    out_dtype=BF16 if latent else F32, name="nat_qkv")
        if latent:
            o = _nat_attention(qkv, st, cache_k, cache_v, j, rpb, dh)
        else:
            o, kc, vc = _ctx_attention(qkv, qkv, qkv, st, n_heads=heads, dq=dh, dv=dh, q_col=0, k_col=heads * dh,
                                       v_col=2 * heads * dh, emit_kv=True)
            extra = (kc, vc)
        new_x.append(_proj(o, w_o, st, res=x, gate=mods[2], name="nat_out"))
    return new_x, extra


def _mixer_lru(xs, streams, mods, state, w_in, g_mix, conv_w, conv_b, w_a, b_a, w_i, b_i, lam, w_out):
    c = conv_w.shape[1]
    block = w_a.shape[-1]
    w_in, w_out = w_in.astype(BF16), w_out.astype(BF16)
    starts, kb = _band_plan(c, block)
    wa = [_band_weights(0.5 * w_a[d], c, block, starts, kb) for d in range(2)]
    wi = [_band_weights(0.5 * w_i[d], c, block, starts, kb) for d in range(2)]
    new_x, st_out = [], None
    for x, st in zip(xs, streams):
        latent = not st.shared
        h0 = state.astype(F32) if latent else jnp.zeros((st.nb, 2, c), F32)
        u = _proj(x, w_in, st, norm_g=g_mix, mod=(mods[0], mods[1]), out_time_major=True, name="lru_in")
        hs_f, t_f, conv = _lru_pass(u, st, conv_w, conv_b, wa[0], wi[0], b_a[0], b_i[0], lam[0], h0[:, 0], starts,
                                    kb, reverse=False)
        y, t_b = _lru_pass(u, st, conv_w, conv_b, wa[1], wi[1], b_a[1], b_i[1], lam[1], h0[:, 1], starts, kb,
                           reverse=True, fwd=(hs_f, conv))
        if not latent:
            st_out = jnp.stack([t_f, t_b], axis=1)
        new_x.append(_proj(y.reshape(st.seq, st.nb * c), w_out, st, res=x, gate=mods[2], x_time_major=True,
                           name="lru_out"))
    return new_x, st_out


def _mixer_mla(xs, streams, mods, cache_ckv, cache_kr, w_down, g_mix, g_qa, g_kva, w_uq, w_ukv, g_q, g_k, w_o):
    d_model = w_down.shape[0]
    q_rank, kv_rank = g_qa.shape[0], g_kva.shape[0]
    qk_dim = g_q.shape[0]
    heads = w_uq.shape[1] // qk_dim
    rope = w_down.shape[1] - q_rank - kv_rank
    nope = qk_dim - rope
    assert nope == LANE and rope <= LANE and kv_rank % LANE == 0 and q_rank % LANE == 0
    head_w = 2 * LANE
    q_pad = -q_rank % kv_rank
    kv_col = q_rank + q_pad
    zeros = lambda width: jnp.zeros((d_model, width), F32)
    w_qd, w_kvd, w_rd = w_down[:, :q_rank], w_down[:, q_rank:q_rank + kv_rank], w_down[:, q_rank + kv_rank:]
    if q_pad >= LANE:
        kr_col = q_rank
        w_dn = jnp.concatenate([w_qd, w_rd, zeros(q_pad - rope), w_kvd], axis=1).astype(BF16)
    else:
        kr_col = kv_col + kv_rank
        w_dn = jnp.concatenate([w_qd, zeros(q_pad), w_kvd, w_rd, zeros(LANE - rope)], axis=1).astype(BF16)
    kr_blk = kr_col // LANE
    w_q = jnp.pad(w_uq.reshape(q_rank, heads, qk_dim), ((0, 0), (0, 0), (0, head_w - qk_dim)))
    w_q = w_q.reshape(q_rank, heads * head_w).astype(BF16)
    w_ukv, w_o = w_ukv.astype(BF16), w_o.astype(BF16)
    gq = jnp.tile(jnp.pad(g_q * (qk_dim ** -0.5 * LOG2E), (0, head_w - qk_dim)), heads)[None]
    g1, g2 = g_k[None, :nope], jnp.pad(g_k[nope:], (0, LANE - rope))[None]
    p = cache_ckv.shape[1]
    new_x, extra = [], None
    for x, st in zip(xs, streams):
        latent = not st.shared
        d = _proj(x, w_dn, st, norm_g=g_mix, mod=(mods[0], mods[1]), name="mla_down")
        q_tabs = _rope_tables(st.seq, rope, nope, head_w) if latent else None
        k_tabs = _rope_tables(st.seq, rope, 0, LANE) if latent else None
        q = _proj(d, w_q, st, x_block=(q_rank, 0), norm_g=g_qa[None],
                  heads=dict(head_w=head_w, norm_div=qk_dim, norm_cols=heads * head_w, gains=gq, tabs=q_tabs,
                             rope_tiles=(False, True)),
                  out_dtype=BF16, name="mla_uq")
        kv = _mla_kv(d, (kv_rank, kv_col // kv_rank), w_ukv, d, (LANE, kr_blk), g_kva[None], g1, g2, k_tabs, st,
                     n_heads=heads, norm_div=qk_dim, emit_xn=not latent, name="mla_ukv")
        if latent:
            k, v = kv
            cst = _Stream(st.nb, p, 0, True)
            krc = jnp.pad(cache_kr.reshape(st.nb * p, rope), ((0, 0), (0, LANE - rope)))
            kc, vc = _mla_kv(cache_ckv.reshape(st.nb * p, kv_rank), (kv_rank, 0), w_ukv, krc, (LANE, 0), None,
                             g1, g2, None, cst, n_heads=heads, norm_div=qk_dim, emit_xn=False,
                             name="mla_ukv_cache")
            o = _joint_dense_attention(q, k, v, kc, vc, st, p, n_heads=heads, dq=head_w, dv=LANE)
        else:
            k, v, ckv = kv
            o = _ctx_attention(q, k, v, st, n_heads=heads, dq=head_w, dv=LANE, q_col=0, k_col=0, v_col=0)
            kr_out = d[:, kr_col:kr_col + rope]
            extra = (ckv.reshape(st.nb, st.seq, kv_rank), kr_out.reshape(st.nb, st.seq, rope))
        new_x.append(_proj(o, w_o, st, res=x, gate=mods[2], name="mla_out"))
    return new_x, extra


def _mixer_swa(xs, streams, mods, cache_k, cache_v, j, w_qkv, g_mix, g_q, g_k, sinks, w_o):
    dh = g_q.shape[0]
    heads = sinks.shape[0]
    kvh = (w_qkv.shape[1] // dh - heads) // 2
    w_qkv, w_o = w_qkv.astype(BF16), w_o.astype(BF16)
    gains = jnp.concatenate([jnp.tile(g_q * (dh ** -0.5 * LOG2E), heads), jnp.tile(g_k, kvh),
                             jnp.ones((kvh * dh,), F32)])[None]
    sinks = sinks.astype(F32) * LOG2E
    new_x, extra = [], None
    for x, st in zip(xs, streams):
        latent = not st.shared
        tabs = _rope_tables(st.seq, dh, 0, dh) if latent else None
        if tabs is not None:
            tabs = tuple(jnp.tile(t, (1, LANE // dh)) for t in tabs)
        spec = dict(head_w=dh, norm_div=dh, norm_cols=(heads + kvh) * dh, gains=gains, tabs=tabs,
                    rope_tiles=(True,))
        qkv = _proj(x, w_qkv, st, norm_g=g_mix, mod=(mods[0], mods[1]), heads=spec,
                    out_dtype=BF16 if latent else F32, name="swa_qkv")
        if latent:
            o = _swa_attention(qkv, st, sinks, heads=heads, kvh=kvh, dh=dh, cache=(cache_k, cache_v, j))
        else:
            o, kc, vc = _swa_attention(qkv, st, sinks, heads=heads, kvh=kvh, dh=dh)
            extra = (kc, vc)
        new_x.append(_proj(o, w_o, st, res=x, gate=mods[2], name="swa_out"))
    return new_x, extra


def kernel(x_prompt, x_sample, cache_nat_k, cache_nat_v, state_lru, cache_mla_ckv, cache_mla_krope, cache_swa_k, cache_swa_v, c, c_ctx, norm_mix, norm_ffn, w_mod, b_mod, ffn_w_in, ffn_conv_w, ffn_conv_b, ffn_w_out, nat_w_qkv, nat_q_norm, nat_k_norm, nat_rpb, nat_w_o, lru_w_in, lru_conv_w, lru_conv_b, lru_w_a, lru_b_a, lru_w_i, lru_b_i, lru_lambda, lru_w_out, mla_w_down, mla_q_a_norm, mla_kv_a_norm, mla_w_uq, mla_w_ukv, mla_q_norm, mla_k_norm, mla_w_o, swa_w_qkv, swa_q_norm, swa_k_norm, swa_sinks, swa_w_o):
    bc, sc, d = x_prompt.shape
    bl, n, _ = x_sample.shape
    depth = w_mod.shape[0]
    streams = (_Stream(bc, sc, 0, True), _Stream(bl, n, 1, False))
    xs = [x_prompt.reshape(bc * sc, d), x_sample.reshape(bl * n, d)]

    n_cond = 1 + bl
    cond_rows = -(-n_cond // SUBLANE) * SUBLANE
    cond = jnp.zeros((cond_rows, d), F32).at[0].set(c_ctx).at[1:n_cond].set(c)
    mods = _modulation(cond, w_mod, b_mod)[:, :n_cond]

    nat_k_l, nat_v_l, lru_l, ckv_l, krope_l, swa_k_l, swa_v_l = [], [], [], [], [], [], []
    for l in range(depth):
        kind, j = l % 4, l // 4
        m6 = [mods[l, :, None, t * d:(t + 1) * d] for t in range(6)]
        g_mix = norm_mix[l].reshape(1, d)
        if kind == 0:
            xs, (kc, vc) = _mixer_nat(xs, streams, m6, cache_nat_k, cache_nat_v, j, nat_w_qkv[j], g_mix,
                                      nat_q_norm[j], nat_k_norm[j], nat_rpb[j], nat_w_o[j])
            nat_k_l.append(kc)
            nat_v_l.append(vc)
        elif kind == 1:
            xs, st = _mixer_lru(xs, streams, m6, state_lru[:, j], lru_w_in[j], g_mix, lru_conv_w[j], lru_conv_b[j],
                                lru_w_a[j], lru_b_a[j], lru_w_i[j], lru_b_i[j], lru_lambda[j], lru_w_out[j])
            lru_l.append(st)
        elif kind == 2:
            xs, (ckv, kr) = _mixer_mla(xs, streams, m6, cache_mla_ckv[:, j], cache_mla_krope[:, j], mla_w_down[j],
                                       g_mix, mla_q_a_norm[j], mla_kv_a_norm[j], mla_w_uq[j], mla_w_ukv[j],
                                       mla_q_norm[j], mla_k_norm[j], mla_w_o[j])
            ckv_l.append(ckv)
            krope_l.append(kr)
        else:
            xs, (kc, vc) = _mixer_swa(xs, streams, m6, cache_swa_k, cache_swa_v, j, swa_w_qkv[j], g_mix,
                                      swa_q_norm[j], swa_k_norm[j], swa_sinks[j], swa_w_o[j])
            swa_k_l.append(kc)
            swa_v_l.append(vc)
        w_in, w_out = ffn_w_in[l].astype(BF16), ffn_w_out[l].astype(BF16)
        xs = [_conv_ffn(x, st, norm_ffn[l].reshape(1, d), m6[3], m6[4], m6[5], w_in, ffn_conv_w[l],
                        ffn_conv_b[l], w_out) for x, st in zip(xs, streams)]

    return (xs[0].reshape(bc, sc, d), xs[1].reshape(bl, n, d), jnp.stack(nat_k_l, axis=1),
            jnp.stack(nat_v_l, axis=1), jnp.stack(lru_l, axis=1), jnp.stack(ckv_l, axis=1),
            jnp.stack(krope_l, axis=1), jnp.stack(swa_k_l, axis=1), jnp.stack(swa_v_l, axis=1))
```
